```python
import math
import jax, jax.numpy as jnp
from jax import lax
import numpy as np


D_MODEL = 1024
BATCH = 16
SEQ = 2048
DEPTH = 2

EPS = 1e-6
RET_HEADS = 4
RET_DK = 128
RET_DV = 128
RET_CHUNK = 128
ROPE_BASE = 10000.0
DA_PATTERNS = ((128, 1), (512, 4), (2048, 16))
DA_HEADS_PER_GROUP = 4
DA_HEADS = DA_HEADS_PER_GROUP * len(DA_PATTERNS)
DA_DH = 64
ML_HEADS = 4
ML_DK = 64
ML_DV = 128
ML_CHUNK = 128
ML_CONV = 4
D_FF = 2816
N_EXPERTS = 8
TOP_K = 2
D_FF_EXPERT = 3584

SPLIT_SIZES = (RET_HEADS * RET_DK, RET_HEADS * RET_DK, RET_HEADS * RET_DV, RET_HEADS * RET_DV,
               DA_HEADS * DA_DH, DA_HEADS * DA_DH, DA_HEADS * DA_DH,
               ML_HEADS * ML_DK, ML_HEADS * ML_DK, ML_HEADS * ML_DV, ML_HEADS * ML_DV, ML_HEADS, ML_HEADS,
               D_MODEL, D_MODEL, D_MODEL)
P_IN = sum(SPLIT_SIZES)

kernel_name = 'hybrid_gated_retention_dilattn_mlstm_moe'


def rms_norm(x, w):
    xf = x.astype(jnp.float32)
    y = xf * lax.rsqrt(jnp.mean(xf * xf, axis=-1, keepdims=True) + EPS)
    return (y * w.astype(jnp.float32)).astype(x.dtype)


def head_rms(x, w):
    xf = x.astype(jnp.float32)
    return xf * lax.rsqrt(jnp.mean(xf * xf, axis=-1, keepdims=True) + EPS) * w.astype(jnp.float32)


def rotary(x, pos):
    half = x.shape[-1] // 2
    inv = jnp.power(ROPE_BASE, -jnp.arange(half, dtype=jnp.float32) / half)
    ang = pos.astype(jnp.float32)[:, None] * inv[None, :]
    cos = jnp.cos(ang)[None, :, None, :]
    sin = jnp.sin(ang)[None, :, None, :]
    xf = x.astype(jnp.float32)
    x1, x2 = xf[..., :half], xf[..., half:]
    return jnp.concatenate([x1 * cos - x2 * sin, x1 * sin + x2 * cos], axis=-1)


def to_chunks(t, c):
    b, s = t.shape[:2]
    t = t.reshape((b, s // c, c) + t.shape[2:])
    return jnp.swapaxes(jnp.swapaxes(t, 0, 1), 2, 3)


def from_chunks(t):
    t = jnp.swapaxes(jnp.swapaxes(t, 2, 3), 0, 1)
    b, n, c = t.shape[:3]
    return t.reshape((b, n * c) + t.shape[3:])


def retention(q, k, v):
    B, S, H, DK = q.shape
    DV = v.shape[-1]
    C = RET_CHUNK
    q = q.astype(jnp.float32)
    k = k.astype(jnp.float32) * (DK ** -0.5)
    v = v.astype(jnp.float32)
    log_gamma = jnp.log1p(-jnp.exp2(-5.0 - jnp.arange(H, dtype=jnp.float32)))
    idx = jnp.arange(C, dtype=jnp.float32)
    rel = idx[:, None] - idx[None, :]
    intra = jnp.where(rel >= 0, jnp.exp(log_gamma[:, None, None] * jnp.maximum(rel, 0.0)), 0.0)
    q_dec = jnp.exp(log_gamma[:, None] * (idx + 1.0))[None, :, :, None]
    k_dec = jnp.exp(log_gamma[:, None] * (C - 1.0 - idx))[None, :, :, None]
    c_dec = jnp.exp(log_gamma * C)[None, :, None, None]

    def step(state, blk):
        qb, kb, vb = blk
        scores = jnp.einsum('bhik,bhjk->bhij', qb, kb) * intra
        out = (jnp.einsum('bhij,bhjv->bhiv', scores, vb)
               + jnp.einsum('bhik,bhkv->bhiv', qb, state) * q_dec)
        state = state * c_dec + jnp.einsum('bhjk,bhjv->bhkv', kb * k_dec, vb)
        return state, out

    state0 = jnp.zeros((B, H, DK, DV), jnp.float32)
    _, out = lax.scan(step, state0, (to_chunks(q, C), to_chunks(k, C), to_chunks(v, C)))
    return from_chunks(out)


def dilated_group(q, k, v, window, dilation):
    B, S, Hg, Dh = q.shape
    M = window // dilation
    L = -(-S // dilation)
    L = -(-L // M) * M
    nb = L // M
    S_pad = L * dilation

    def to_sub(t):
        t = jnp.pad(t, ((0, 0), (0, S_pad - S), (0, 0), (0, 0)))
        t = t.reshape(B, L, dilation, Hg, Dh).transpose(0, 2, 3, 1, 4)
        return t.reshape(B, dilation, Hg, nb, M, Dh)

    def with_prev(t):
        prev = jnp.pad(t[:, :, :, :-1], ((0, 0), (0, 0), (0, 0), (1, 0), (0, 0), (0, 0)))
        return jnp.concatenate([prev, t], axis=4)

    qs = to_sub(q)
    kk = with_prev(to_sub(k))
    vv = with_prev(to_sub(v))
    s = jnp.einsum('brhnqd,brhnkd->brhnqk', qs, kk) * (Dh ** -0.5)
    qi = jnp.arange(M)[:, None]
    kj = jnp.arange(2 * M)[None, :]
    dist = qi + M - kj
    band = (dist >= 0) & (dist <= M)
    first = (jnp.arange(nb)[:, None, None] == 0) & (kj[None] < M)
    valid = band[None] & jnp.logical_not(first)
    s = jnp.where(valid, s, -jnp.inf)
    m = jnp.max(s, axis=-1, keepdims=True)
    p = jnp.exp(s - m)
    l = jnp.sum(p, axis=-1, keepdims=True)
    o = jnp.einsum('brhnqk,brhnkd->brhnqd', p, vv) / l
    lse = (m + jnp.log(l))[..., 0]
    o = o.reshape(B, dilation, Hg, L, Dh).transpose(0, 3, 1, 2, 4).reshape(B, S_pad, Hg, Dh)[:, :S]
    lse = lse.reshape(B, dilation, Hg, L).transpose(0, 3, 1, 2).reshape(B, S_pad, Hg)[:, :S]
    return o, lse


def dilated_attention(q, k, v):
    outs, lses = [], []
    for g, (window, dilation) in enumerate(DA_PATTERNS):
        sl = slice(g * DA_HEADS_PER_GROUP, (g + 1) * DA_HEADS_PER_GROUP)
        o, lse = dilated_group(q[:, :, sl], k[:, :, sl], v[:, :, sl], window, dilation)
        outs.append(o)
        lses.append(lse)
    outs = jnp.stack(outs, axis=0)
    w = jax.nn.softmax(jnp.stack(lses, axis=0), axis=0)
    return jnp.sum(w[..., None] * outs, axis=0)


def causal_dwconv(x, w):
    K, ch = w.shape
    return lax.conv_general_dilated(x, w[:, None, :].astype(x.dtype), window_strides=(1,),
                                    padding=((K - 1, 0),), dimension_numbers=('NWC', 'WIO', 'NWC'),
                                    feature_group_count=ch)


def mlstm(q, k, v, i_pre, f_pre):
    B, S, H, DK = q.shape
    DV = v.shape[-1]
    C = ML_CHUNK
    q = q.astype(jnp.float32)
    k = k.astype(jnp.float32) * (DK ** -0.5)
    v = v.astype(jnp.float32)
    ig = i_pre.astype(jnp.float32)
    logf = jax.nn.log_sigmoid(f_pre.astype(jnp.float32))
    causal = jnp.tril(jnp.ones((C, C), dtype=bool))

    def step(carry, blk):
        Cm, nm, mm = carry
        qb, kb, vb, ib, fb = blk
        b = jnp.cumsum(fb, axis=-1)
        D = jnp.where(causal, b[..., :, None] - b[..., None, :] + ib[..., None, :], -jnp.inf)
        inter = b + mm[..., None]
        m = jnp.maximum(inter, jnp.max(D, axis=-1))
        Dw = jnp.exp(D - m[..., None])
        wi = jnp.exp(inter - m)
        s = jnp.einsum('bhid,bhjd->bhij', qb, kb) * Dw
        num = jnp.einsum('bhij,bhjv->bhiv', s, vb) + wi[..., None] * jnp.einsum('bhid,bhdv->bhiv', qb, Cm)
        den = jnp.sum(s, axis=-1) + wi * jnp.einsum('bhid,bhd->bhi', qb, nm)
        h = num / jnp.maximum(jnp.abs(den), jnp.exp(-m))[..., None]
        m_new = m[..., -1]
        wk = jnp.exp(b[..., -1:] - b + ib - m_new[..., None])
        wc = jnp.exp(b[..., -1] + mm - m_new)
        Cm = wc[..., None, None] * Cm + jnp.einsum('bhj,bhjd,bhjv->bhdv', wk, kb, vb)
        nm = wc[..., None] * nm + jnp.einsum('bhj,bhjd->bhd', wk, kb)
        return (Cm, nm, m_new), h

    carry0 = (jnp.zeros((B, H, DK, DV), jnp.float32), jnp.zeros((B, H, DK), jnp.float32),
              jnp.zeros((B, H), jnp.float32))
    _, out = lax.scan(step, carry0, (to_chunks(q, C), to_chunks(k, C), to_chunks(v, C),
                                     to_chunks(ig, C), to_chunks(logf, C)))
    return from_chunks(out)


def hybrid_mixer(h, w_in, ret_norm_w, da_q_norm_w, da_k_norm_w, ml_conv_w, ml_i_bias, ml_f_bias,
                 ml_norm_w, w_br_ret, w_br_da, w_br_ml, w_out, pos):
    B, S, _ = h.shape
    dt = h.dtype
    z = h @ w_in
    offsets = np.cumsum(np.array(SPLIT_SIZES))[:-1].tolist()
    (rq, rk, rv, rg, dq, dk, dv, mq, mk, mv, mo, mi, mf, g_ret, g_da, g_ml) = jnp.split(z, offsets, axis=-1)

    r_out = retention(rotary(rq.reshape(B, S, RET_HEADS, RET_DK), pos),
                      rotary(rk.reshape(B, S, RET_HEADS, RET_DK), pos),
                      rv.reshape(B, S, RET_HEADS, RET_DV))
    r_out = head_rms(r_out, ret_norm_w.reshape(RET_HEADS, RET_DV)).reshape(B, S, RET_HEADS * RET_DV)
    o_ret = (r_out * jax.nn.silu(rg.astype(jnp.float32))).astype(dt)

    qa = head_rms(dq.reshape(B, S, DA_HEADS, DA_DH), da_q_norm_w)
    ka = head_rms(dk.reshape(B, S, DA_HEADS, DA_DH), da_k_norm_w)
    va = dv.reshape(B, S, DA_HEADS, DA_DH).astype(jnp.float32)
    o_da = dilated_attention(qa, ka, va).reshape(B, S, DA_HEADS_PER_GROUP * DA_DH).astype(dt)

    qk = jax.nn.silu(causal_dwconv(jnp.concatenate([mq, mk], axis=-1), ml_conv_w))
    mq_c, mk_c = jnp.split(qk, 2, axis=-1)
    m_out = mlstm(mq_c.reshape(B, S, ML_HEADS, ML_DK), mk_c.reshape(B, S, ML_HEADS, ML_DK),
                  mv.reshape(B, S, ML_HEADS, ML_DV), mi + ml_i_bias, mf + ml_f_bias)
    m_out = head_rms(m_out, ml_norm_w.reshape(ML_HEADS, ML_DV)).reshape(B, S, ML_HEADS * ML_DV)
    o_ml = (m_out * jax.nn.sigmoid(mo.astype(jnp.float32))).astype(dt)

    y = (jax.nn.sigmoid(g_ret) * (o_ret @ w_br_ret)
         + jax.nn.sigmoid(g_da) * (o_da @ w_br_da)
         + jax.nn.sigmoid(g_ml) * (o_ml @ w_br_ml))
    return y @ w_out


def swiglu(h, w_gate, w_up, w_down):
    return (jax.nn.silu(h @ w_gate) * (h @ w_up)) @ w_down


def moe_swiglu(h, w_router, b_router, w_gate, w_up, w_down):
    B, S, D = h.shape
    ht = h.reshape(B * S, D)
    logits = (ht @ w_router).astype(jnp.float32) + b_router.astype(jnp.float32)
    top_v, top_i = lax.top_k(logits, TOP_K)
    probs = jax.nn.softmax(top_v, axis=-1)
    gates = jnp.sum(jax.nn.one_hot(top_i, N_EXPERTS, dtype=jnp.float32) * probs[..., None], axis=1)
    out = jnp.zeros((B * S, D), jnp.float32)
    for e in range(N_EXPERTS):
        a = jax.nn.silu(ht @ w_gate[e]) * (ht @ w_up[e])
        out = out + gates[:, e:e + 1] * (a @ w_down[e]).astype(jnp.float32)
    return out.astype(h.dtype).reshape(B, S, D)


def setup_inputs(seed: int = 0) -> dict:
    key = jax.random.key(seed)
    ks = jax.random.split(key, 24)
    f32 = jnp.float32

    def nrm(k, shape, scale):
        return jax.random.normal(k, shape, f32) * scale

    n_dense = (DEPTH + 1) // 2
    n_moe = DEPTH // 2
    ret_w = RET_HEADS * RET_DV
    da_out = DA_HEADS_PER_GROUP * DA_DH
    ml_w = ML_HEADS * ML_DV
    return {
        'x': nrm(ks[0], (BATCH, SEQ, D_MODEL), 1.0),
        'norm1_w': 1.0 + nrm(ks[1], (DEPTH, D_MODEL), 0.05),
        'w_in': nrm(ks[2], (DEPTH, D_MODEL, P_IN), D_MODEL ** -0.5),
        'ret_norm_w': 1.0 + nrm(ks[3], (DEPTH, ret_w), 0.05),
        'da_q_norm_w': 1.0 + nrm(ks[4], (DEPTH, DA_DH), 0.05),
        'da_k_norm_w': 1.0 + nrm(ks[5], (DEPTH, DA_DH), 0.05),
        'ml_conv_w': nrm(ks[6], (DEPTH, ML_CONV, 2 * ML_HEADS * ML_DK), ML_CONV ** -0.5),
        'ml_i_bias': nrm(ks[7], (DEPTH, ML_HEADS), 0.1),
        'ml_f_bias': jnp.linspace(3.0, 6.0, ML_HEADS, dtype=f32)[None, :] + nrm(ks[8], (DEPTH, ML_HEADS), 0.1),
        'ml_norm_w': 1.0 + nrm(ks[9], (DEPTH, ml_w), 0.05),
        'w_br_ret': nrm(ks[10], (DEPTH, ret_w, D_MODEL), ret_w ** -0.5),
        'w_br_da': nrm(ks[11], (DEPTH, da_out, D_MODEL), da_out ** -0.5),
        'w_br_ml': nrm(ks[12], (DEPTH, ml_w, D_MODEL), ml_w ** -0.5),
        'w_out': nrm(ks[13], (DEPTH, D_MODEL, D_MODEL), D_MODEL ** -0.5),
        'norm2_w': 1.0 + nrm(ks[14], (DEPTH, D_MODEL), 0.05),
        'ffn_w_gate': nrm(ks[15], (n_dense, D_MODEL, D_FF), D_MODEL ** -0.5),
        'ffn_w_up': nrm(ks[16], (n_dense, D_MODEL, D_FF), D_MODEL ** -0.5),
        'ffn_w_down': nrm(ks[17], (n_dense, D_FF, D_MODEL), D_FF ** -0.5),
        'moe_w_router': nrm(ks[18], (n_moe, D_MODEL, N_EXPERTS), D_MODEL ** -0.5),
        'moe_b_router': nrm(ks[19], (n_moe, N_EXPERTS), 0.01),
        'moe_w_gate': nrm(ks[20], (n_moe, N_EXPERTS, D_MODEL, D_FF_EXPERT), D_MODEL ** -0.5),
        'moe_w_up': nrm(ks[21], (n_moe, N_EXPERTS, D_MODEL, D_FF_EXPERT), D_MODEL ** -0.5),
        'moe_w_down': nrm(ks[22], (n_moe, N_EXPERTS, D_FF_EXPERT, D_MODEL), D_FF_EXPERT ** -0.5),
    }


def reference(x, norm1_w, w_in, ret_norm_w, da_q_norm_w, da_k_norm_w, ml_conv_w, ml_i_bias, ml_f_bias,
              ml_norm_w, w_br_ret, w_br_da, w_br_ml, w_out, norm2_w, ffn_w_gate, ffn_w_up, ffn_w_down,
              moe_w_router, moe_b_router, moe_w_gate, moe_w_up, moe_w_down):
    pos = jnp.arange(x.shape[1], dtype=jnp.int32)
    for layer in range(DEPTH):
        h = rms_norm(x, norm1_w[layer])
        x = x + hybrid_mixer(h, w_in[layer], ret_norm_w[layer], da_q_norm_w[layer], da_k_norm_w[layer],
                             ml_conv_w[layer], ml_i_bias[layer], ml_f_bias[layer], ml_norm_w[layer],
                             w_br_ret[layer], w_br_da[layer], w_br_ml[layer], w_out[layer], pos)
        h = rms_norm(x, norm2_w[layer])
        j = layer // 2
        if layer % 2 == 0:
            x = x + swiglu(h, ffn_w_gate[j], ffn_w_up[j], ffn_w_down[j])
        else:
            x = x + moe_swiglu(h, moe_w_router[j], moe_b_router[j], moe_w_gate[j], moe_w_up[j], moe_w_down[j])
    return x
```

```python
import functools
import math

import jax
import jax.numpy as jnp
from jax import lax
from jax.experimental import pallas as pl
from jax.experimental.pallas import tpu as pltpu

F32 = jnp.float32
BF16 = jnp.bfloat16

EPS = 1e-6
D_MODEL = 1024
CHUNK = 128
LANES = 128
ROPE_BASE = 10000.0
RET_HEADS = 4
DA_GROUPS = ((1, 16), (4, 4), (16, 1))
DA_DH = 64
ML_HEADS = 4
ML_DK = 64
N_EXPERTS = 8
VMEM_LIMIT = 56 * 1024 * 1024

NEG = -1e30


def _cparams(*sem):
    return pltpu.CompilerParams(dimension_semantics=sem, vmem_limit_bytes=VMEM_LIMIT)


def _dot(a, b):
    return jnp.dot(a, b, preferred_element_type=F32)


def _dot_nt(a, b):
    return lax.dot_general(a, b, (((1,), (1,)), ((), ())), preferred_element_type=F32)


def _dot_tn(a, b):
    return lax.dot_general(a, b, (((0,), (0,)), ((), ())), preferred_element_type=F32)


def _sigmoid(x):
    return 1.0 / (1.0 + jnp.exp(-x))


def _silu(x):
    return x * _sigmoid(x)


def _norm_matmul_kernel(x_ref, nw_ref, w_ref, o_ref, h_ref):
    @pl.when(pl.program_id(1) == 0)
    def _():
        x = x_ref[...]
        ms = jnp.mean(x * x, axis=-1, keepdims=True)
        h_ref[...] = (x * lax.rsqrt(ms + EPS) * nw_ref[...]).astype(BF16)

    o_ref[...] = _dot(h_ref[...], w_ref[...]).astype(o_ref.dtype)


def _norm_matmul(x, nw, w, out_dtype, tm, tn, name):
    T, D = x.shape
    N = w.shape[1]
    return pl.pallas_call(
        _norm_matmul_kernel,
        grid=(T // tm, N // tn),
        in_specs=[pl.BlockSpec((tm, D), lambda i, j: (i, 0)),
                  pl.BlockSpec((1, D), lambda i, j: (0, 0)),
                  pl.BlockSpec((D, tn), lambda i, j: (0, j))],
        out_specs=pl.BlockSpec((tm, tn), lambda i, j: (i, j)),
        out_shape=jax.ShapeDtypeStruct((T, N), out_dtype),
        scratch_shapes=[pltpu.VMEM((tm, D), BF16)],
        compiler_params=_cparams("parallel", "arbitrary"),
        name=name,
    )(x, nw, w)


def _retention_kernel(z_ref, cos_ref, sin_ref, nw_ref, o_ref):
    S = z_ref.shape[0]
    n_chunks = S // CHUNK
    HW = RET_HEADS * LANES
    row = lax.broadcasted_iota(jnp.int32, (CHUNK, CHUNK), 0).astype(F32)
    col = lax.broadcasted_iota(jnp.int32, (CHUNK, CHUNK), 1).astype(F32)
    rcol = lax.broadcasted_iota(jnp.int32, (CHUNK, 1), 0).astype(F32)
    for h in range(RET_HEADS):
        lg = math.log1p(-(2.0 ** (-5.0 - h)))
        rel = row - col
        intra = jnp.where(rel >= 0, jnp.exp(lg * jnp.maximum(rel, 0.0)), 0.0)
        q_dec = jnp.exp(lg * (rcol + 1.0))
        k_dec = jnp.exp(lg * (CHUNK - 1.0 - rcol)) * (LANES ** -0.5)
        c_dec = math.exp(lg * CHUNK)
        nw = nw_ref[:, h * LANES:(h + 1) * LANES]

        def body(n, state, h=h, intra=intra, q_dec=q_dec, k_dec=k_dec, c_dec=c_dec, nw=nw):
            r0 = pl.multiple_of(n * CHUNK, CHUNK)
            rows = pl.ds(r0, CHUNK)
            cos = cos_ref[rows, :]
            sin = sin_ref[rows, :]
            q = z_ref[rows, h * LANES:(h + 1) * LANES].astype(F32)
            k = z_ref[rows, HW + h * LANES:HW + (h + 1) * LANES].astype(F32)
            v = z_ref[rows, 2 * HW + h * LANES:2 * HW + (h + 1) * LANES]
            g = z_ref[rows, 3 * HW + h * LANES:3 * HW + (h + 1) * LANES].astype(F32)
            q = q * cos + pltpu.roll(q, LANES // 2, 1) * sin
            k = k * cos + pltpu.roll(k, LANES // 2, 1) * sin
            qb = q.astype(BF16)
            kb = (k * (LANES ** -0.5)).astype(BF16)
            s = _dot_nt(qb, kb) * intra
            o = _dot(s.astype(BF16), v) + _dot(qb, state.astype(BF16)) * q_dec
            kd = (k * k_dec).astype(BF16)
            state = state * c_dec + _dot_tn(kd, v)
            ms = jnp.mean(o * o, axis=-1, keepdims=True)
            o = o * lax.rsqrt(ms + EPS) * nw
            o_ref[rows, h * LANES:(h + 1) * LANES] = (o * _silu(g)).astype(o_ref.dtype)
            return state

        lax.fori_loop(0, n_chunks, body, jnp.zeros((LANES, LANES), F32))


def _retention(zb, cos, sin, nw, B, S):
    T = B * S
    W = RET_HEADS * LANES
    return pl.pallas_call(
        _retention_kernel,
        grid=(B,),
        in_specs=[pl.BlockSpec((S, 4 * W), lambda b: (b, 0)),
                  pl.BlockSpec((S, LANES), lambda b: (0, 0)),
                  pl.BlockSpec((S, LANES), lambda b: (0, 0)),
                  pl.BlockSpec((1, W), lambda b: (0, 0))],
        out_specs=pl.BlockSpec((S, W), lambda b: (b, 0)),
        out_shape=jax.ShapeDtypeStruct((T, W), BF16),
        compiler_params=_cparams("parallel"),
        name="retention",
    )(zb, cos, sin, nw)


def _da_kernel(*refs):
    z_refs = refs[:9]
    wq_ref, wk_ref, o_ref, qn_ref, kn_ref, v_ref, np_ref, lp_ref, mp_ref = refs[9:18]
    nn_refs, ln_refs, mn_refs = refs[18:21], refs[21:24], refs[24:27]
    S = o_ref.shape[0]
    lane = lax.broadcasted_iota(jnp.int32, (1, LANES), 1)
    lo = lane < DA_DH
    row = lax.broadcasted_iota(jnp.int32, (CHUNK, CHUNK), 0)
    col = lax.broadcasted_iota(jnp.int32, (CHUNK, CHUNK), 1)
    mask_cur = col <= row
    mask_prev = col >= row

    def head_norm(x, w):
        x2 = x * x
        ms_lo = jnp.sum(jnp.where(lo, x2, 0.0), axis=-1, keepdims=True) * (1.0 / DA_DH)
        ms_hi = jnp.sum(jnp.where(lo, 0.0, x2), axis=-1, keepdims=True) * (1.0 / DA_DH)
        scale = jnp.where(lo, lax.rsqrt(ms_lo + EPS), lax.rsqrt(ms_hi + EPS))
        return x * scale * w

    RB = 256
    for g, (dil, nb) in enumerate(DA_GROUPS):
        L = S // dil
        zq_ref, zk_ref, zv_ref = z_refs[3 * g:3 * g + 3]
        for rho in range(dil):
            for piece in range(max(L // RB, 1)):
                n_rows = min(RB, L)
                src = pl.ds(rho + dil * piece * n_rows, n_rows, stride=dil) if dil > 1 else pl.ds(
                    piece * n_rows, n_rows)
                dst = pl.ds(rho * L + piece * n_rows, n_rows)
                q = zq_ref[src, :]
                k = zk_ref[src, :]
                v = zv_ref[src, :]
                qn_ref[dst, :] = (head_norm(q, wq_ref[...]) * (DA_DH ** -0.5)).astype(BF16)
                kn_ref[dst, :] = head_norm(k, wk_ref[...]).astype(BF16)
                v_ref[dst, :] = v.astype(BF16)

        def block(r0, with_prev):
            rows = pl.ds(r0, CHUNK)
            q = qn_ref[rows, :]
            zero = jnp.zeros_like(q)
            qs = (jnp.where(lo, q, zero), jnp.where(lo, zero, q))
            kc = kn_ref[rows, :]
            vc = v_ref[rows, :]
            one = jnp.ones_like(vc)
            if with_prev:
                prow = pl.ds(r0 - CHUNK, CHUNK)
                kp = kn_ref[prow, :]
                vp = v_ref[prow, :]
            accs, ms = [], []
            for hs in range(2):
                sc = jnp.where(mask_cur, _dot_nt(qs[hs], kc), NEG)
                m = jnp.max(sc, axis=-1, keepdims=True)
                if with_prev:
                    sp = jnp.where(mask_prev, _dot_nt(qs[hs], kp), NEG)
                    m = jnp.maximum(m, jnp.max(sp, axis=-1, keepdims=True))
                vaug_c = jnp.where(lo, vc, one) if hs == 0 else jnp.where(lo, one, vc)
                acc = _dot(jnp.exp(sc - m).astype(BF16), vaug_c)
                if with_prev:
                    vaug_p = jnp.where(lo, vp, one) if hs == 0 else jnp.where(lo, one, vp)
                    acc = acc + _dot(jnp.exp(sp - m).astype(BF16), vaug_p)
                accs.append(acc)
                ms.append(m)
            sw0 = pltpu.roll(accs[0], DA_DH, 1)
            sw1 = pltpu.roll(accs[1], DA_DH, 1)
            np_ref[rows, :] = jnp.where(lo, accs[0], accs[1])
            lp_ref[rows, :] = jnp.where(lo, sw0, sw1)
            mp_ref[rows, :] = jnp.where(lo, ms[0], ms[1])

        def segment(sgi, carry, nb=nb):
            base = pl.multiple_of(sgi * (nb * CHUNK), CHUNK)
            block(base, False)
            if nb > 1:
                def inner(n, c):
                    block(pl.multiple_of(base + n * CHUNK, CHUNK), True)
                    return c
                lax.fori_loop(1, nb, inner, 0)
            return carry

        lax.fori_loop(0, dil, segment, 0)

        for rho in range(dil):
            for piece in range(max(L // RB, 1)):
                n_rows = min(RB, L)
                dst = pl.ds(rho + dil * piece * n_rows, n_rows, stride=dil) if dil > 1 else pl.ds(
                    piece * n_rows, n_rows)
                src = pl.ds(rho * L + piece * n_rows, n_rows)
                nn_refs[g][dst, :] = np_ref[src, :]
                ln_refs[g][dst, :] = lp_ref[src, :]
                mn_refs[g][dst, :] = mp_ref[src, :]

    for piece in range(S // RB):
        rows = pl.ds(piece * RB, RB)
        m = jnp.maximum(jnp.maximum(mn_refs[0][rows, :], mn_refs[1][rows, :]), mn_refs[2][rows, :])
        num = jnp.zeros((RB, LANES), F32)
        den = jnp.zeros((RB, LANES), F32)
        for g in range(3):
            e = jnp.exp(mn_refs[g][rows, :] - m)
            num = num + e * nn_refs[g][rows, :]
            den = den + e * ln_refs[g][rows, :]
        o_ref[rows, :] = (num / den).astype(o_ref.dtype)


def _dilated_attention(zf, wq, wk, B, S):
    T = B * S
    slab = lambda k: pl.BlockSpec((S, LANES), lambda b, p: (b, 9 * p + k))
    return pl.pallas_call(
        _da_kernel,
        grid=(B, 2),
        in_specs=[slab(k) for k in range(9)] + [
                  pl.BlockSpec((1, LANES), lambda b, p: (0, 0)),
                  pl.BlockSpec((1, LANES), lambda b, p: (0, 0))],
        out_specs=pl.BlockSpec((S, LANES), lambda b, p: (b, p)),
        out_shape=jax.ShapeDtypeStruct((T, 2 * LANES), BF16),
        scratch_shapes=[pltpu.VMEM((S, LANES), BF16), pltpu.VMEM((S, LANES), BF16),
                        pltpu.VMEM((S, LANES), BF16),
                        pltpu.VMEM((S, LANES), F32), pltpu.VMEM((S, LANES), F32),
                        pltpu.VMEM((S, LANES), F32),
                        pltpu.VMEM((S, LANES), F32)] + [pltpu.VMEM((S, LANES), F32)] * 9,
        compiler_params=_cparams("parallel", "arbitrary"),
        name="dilated_attention",
    )(*([zf] * 9), wq, wk)


def _mlstm_kernel(qk_ref, v_ref, og_ref, gate_ref, cw_ref, gb_ref, nw_ref, o_ref,
                  xp_ref, qc_ref, st_ref):
    S = qk_ref.shape[0]
    n_chunks = S // CHUNK
    W = qk_ref.shape[1]
    PAD = 8
    xp_ref[0:PAD, :] = jnp.zeros((PAD, W), F32)
    RB = 256
    for piece in range(S // RB):
        xp_ref[PAD + piece * RB:PAD + (piece + 1) * RB, :] = qk_ref[piece * RB:(piece + 1) * RB, :].astype(F32)
    lane_w = lax.broadcasted_iota(jnp.int32, (1, W), 1)
    kscale = jnp.where(lane_w >= W // 2, ML_DK ** -0.5, 1.0).astype(F32)
    for piece in range(S // RB):
        acc = jnp.zeros((RB, W), F32)
        for i in range(4):
            acc = acc + cw_ref[i:i + 1, :] * xp_ref[pl.ds(PAD + piece * RB - 3 + i, RB), :]
        qc_ref[piece * RB:(piece + 1) * RB, :] = (_silu(acc) * kscale).astype(BF16)

    lane = lax.broadcasted_iota(jnp.int32, (1, LANES), 1)
    lo = lane < ML_DK
    row = lax.broadcasted_iota(jnp.int32, (CHUNK, CHUNK), 0)
    col = lax.broadcasted_iota(jnp.int32, (CHUNK, CHUNK), 1)
    causal = col <= row
    tril = causal.astype(F32)
    srow = lax.broadcasted_iota(jnp.int32, (LANES, 1), 0) < ML_DK
    st_ref[...] = jnp.zeros(st_ref.shape, F32)

    def chunk(n, mms):
        r0 = pl.multiple_of(n * CHUNK, CHUNK)
        rows = pl.ds(r0, CHUNK)
        gp = gate_ref[rows, :] + gb_ref[...]
        logf = jnp.minimum(gp, 0.0) - jnp.log(1.0 + jnp.exp(-jnp.abs(gp)))
        is_f = (lane >= ML_HEADS) & (lane < 2 * ML_HEADS)
        cum = jnp.dot(tril, jnp.where(is_f, logf, 0.0), preferred_element_type=F32,
                      precision=lax.Precision.HIGHEST)
        e = jnp.where(lane < ML_HEADS, gp, cum)
        et = e.T
        new_mms = []
        for pair in range(2):
            q2 = qc_ref[rows, pair * LANES:(pair + 1) * LANES]
            k2 = qc_ref[rows, W // 2 + pair * LANES:W // 2 + (pair + 1) * LANES]
            zero = jnp.zeros_like(q2)
            st = st_ref[pair]
            stb = st.astype(BF16)
            upd = jnp.zeros((LANES, 2 * LANES), F32)
            wcs = []
            for hs in range(2):
                h = 2 * pair + hs
                mm = mms[h]
                i_col = e[:, h:h + 1]
                a_col = e[:, ML_HEADS + h:ML_HEADS + h + 1]
                c_row = et[h:h + 1, :] - et[ML_HEADS + h:ML_HEADS + h + 1, :]
                dmat = jnp.where(causal, a_col + c_row, NEG)
                inter = a_col + mm
                m = jnp.maximum(inter, jnp.max(dmat, axis=-1, keepdims=True))
                dw = jnp.exp(dmat - m)
                wi = jnp.exp(inter - m)
                qh = jnp.where(lo, q2, zero) if hs == 0 else jnp.where(lo, zero, q2)
                kh = jnp.where(lo, k2, zero) if hs == 0 else jnp.where(lo, zero, k2)
                vh = v_ref[rows, h * LANES:(h + 1) * LANES]
                vaug = jnp.concatenate([vh, jnp.ones_like(vh)], axis=-1)
                s = _dot_nt(qh, k2) * dw
                acc = _dot(s.astype(BF16), vaug) + wi * _dot(qh, stb)
                num = acc[:, :LANES]
                den = acc[:, LANES:]
                hv = num / jnp.maximum(jnp.abs(den), jnp.exp(-m))
                ms = jnp.mean(hv * hv, axis=-1, keepdims=True)
                hv = hv * lax.rsqrt(ms + EPS) * nw_ref[:, h * LANES:(h + 1) * LANES]
                og = og_ref[rows, h * LANES:(h + 1) * LANES].astype(F32)
                o_ref[rows, h * LANES:(h + 1) * LANES] = (hv * _sigmoid(og)).astype(o_ref.dtype)
                m_new = m[CHUNK - 1:CHUNK, :]
                a_last = a_col[CHUNK - 1:CHUNK, :]
                wk = jnp.exp(a_last - a_col + i_col - m_new)
                wcs.append(jnp.exp(a_last + mm - m_new))
                upd = upd + _dot_tn((kh.astype(F32) * wk).astype(BF16), vaug)
                new_mms.append(m_new)
            st_ref[pair] = jnp.where(srow, wcs[0], wcs[1]) * st + upd
        return tuple(new_mms)

    lax.fori_loop(0, n_chunks, chunk, tuple(jnp.zeros((1, 1), F32) for _ in range(ML_HEADS)))


def _mlstm(zb, zf, conv_w, gate_bias, nw, B, S, qk_blk, gate_blk):
    T = B * S
    W = ML_HEADS * LANES
    return pl.pallas_call(
        _mlstm_kernel,
        grid=(B,),
        in_specs=[pl.BlockSpec((S, W), lambda b: (b, qk_blk)),
                  pl.BlockSpec((S, W), lambda b: (b, qk_blk + 1)),
                  pl.BlockSpec((S, W), lambda b: (b, qk_blk + 2)),
                  pl.BlockSpec((S, LANES), lambda b: (b, gate_blk)),
                  pl.BlockSpec((4, W), lambda b: (0, 0)),
                  pl.BlockSpec((1, LANES), lambda b: (0, 0)),
                  pl.BlockSpec((1, W), lambda b: (0, 0))],
        out_specs=pl.BlockSpec((S, W), lambda b: (b, 0)),
        out_shape=jax.ShapeDtypeStruct((T, W), BF16),
        scratch_shapes=[pltpu.VMEM((S + 8, W), F32), pltpu.VMEM((S, W), BF16),
                        pltpu.VMEM((2, LANES, 2 * LANES), F32)],
        compiler_params=_cparams("parallel"),
        name="mlstm",
    )(zb, zb, zb, zf, conv_w, gate_bias, nw)


def _merge_kernel(x_ref, oret_ref, oda_ref, oml_ref, gr_ref, gd_ref, gm_ref,
                  wr_ref, wd_ref, wm_ref, wo_ref, o_ref):
    y = _sigmoid(gr_ref[...].astype(F32)) * _dot(oret_ref[...], wr_ref[...])
    y = y + _sigmoid(gd_ref[...].astype(F32)) * _dot(oda_ref[...], wd_ref[...])
    y = y + _sigmoid(gm_ref[...].astype(F32)) * _dot(oml_ref[...], wm_ref[...])
    o_ref[...] = x_ref[...] + _dot(y.astype(BF16), wo_ref[...])


def _merge(x, o_ret, o_da, o_ml, zb, g_blk, wr, wd, wm, wo, tm):
    T, D = x.shape
    full = lambda a: pl.BlockSpec(a.shape, lambda i: (0, 0))
    return pl.pallas_call(
        _merge_kernel,
        grid=(T // tm,),
        in_specs=[pl.BlockSpec((tm, D), lambda i: (i, 0)),
                  pl.BlockSpec((tm, o_ret.shape[1]), lambda i: (i, 0)),
                  pl.BlockSpec((tm, o_da.shape[1]), lambda i: (i, 0)),
                  pl.BlockSpec((tm, o_ml.shape[1]), lambda i: (i, 0)),
                  pl.BlockSpec((tm, D), lambda i: (i, g_blk)),
                  pl.BlockSpec((tm, D), lambda i: (i, g_blk + 1)),
                  pl.BlockSpec((tm, D), lambda i: (i, g_blk + 2)),
                  full(wr), full(wd), full(wm), full(wo)],
        out_specs=pl.BlockSpec((tm, D), lambda i: (i, 0)),
        out_shape=jax.ShapeDtypeStruct((T, D), F32),
        compiler_params=_cparams("parallel"),
        name="merge_outproj",
    )(x, o_ret, o_da, o_ml, zb, zb, zb, wr, wd, wm, wo)


def _ffn_kernel(x_ref, nw_ref, wg_ref, wu_ref, wd_ref, o_ref, h_ref, acc_ref):
    j = pl.program_id(1)

    @pl.when(j == 0)
    def _():
        x = x_ref[...]
        ms = jnp.mean(x * x, axis=-1, keepdims=True)
        h_ref[...] = (x * lax.rsqrt(ms + EPS) * nw_ref[...]).astype(BF16)
        acc_ref[...] = x

    h = h_ref[...]
    a = _silu(_dot(h, wg_ref[...])) * _dot(h, wu_ref[...])
    acc_ref[...] += _dot(a.astype(BF16), wd_ref[...])

    @pl.when(j == pl.num_programs(1) - 1)
    def _():
        o_ref[...] = acc_ref[...]


def _ffn(x, nw, wg, wu, wd, tm, tf):
    T, D = x.shape
    F = wg.shape[1]
    return pl.pallas_call(
        _ffn_kernel,
        grid=(T // tm, F // tf),
        in_specs=[pl.BlockSpec((tm, D), lambda i, j: (i, 0)),
                  pl.BlockSpec((1, D), lambda i, j: (0, 0)),
                  pl.BlockSpec((D, tf), lambda i, j: (0, j)),
                  pl.BlockSpec((D, tf), lambda i, j: (0, j)),
                  pl.BlockSpec((tf, D), lambda i, j: (j, 0))],
        out_specs=pl.BlockSpec((tm, D), lambda i, j: (i, 0)),
        out_shape=jax.ShapeDtypeStruct((T, D), F32),
        scratch_shapes=[pltpu.VMEM((tm, D), BF16), pltpu.VMEM((tm, D), F32)],
        compiler_params=_cparams("parallel", "arbitrary"),
        name="swiglu_ffn",
    )(x, nw, wg, wu, wd)


def _router_kernel(x_ref, nw_ref, wr_ref, br_ref, h_ref, g_ref):
    x = x_ref[...]
    ms = jnp.mean(x * x, axis=-1, keepdims=True)
    h = x * lax.rsqrt(ms + EPS) * nw_ref[...]
    h_ref[...] = h.astype(BF16)
    logits = jnp.dot(h, wr_ref[...], preferred_element_type=F32,
                     precision=lax.Precision.HIGHEST) + br_ref[...]
    lane = lax.broadcasted_iota(jnp.int32, logits.shape, 1).astype(F32)
    logits = jnp.where(lane < N_EXPERTS, logits, NEG)
    m1 = jnp.max(logits, axis=-1, keepdims=True)
    i1 = jnp.min(jnp.where(logits == m1, lane, float(LANES)), axis=-1, keepdims=True)
    rest = jnp.where(lane == i1, NEG, logits)
    m2 = jnp.max(rest, axis=-1, keepdims=True)
    i2 = jnp.min(jnp.where(rest == m2, lane, float(LANES)), axis=-1, keepdims=True)
    e2 = jnp.exp(m2 - m1)
    p1 = 1.0 / (1.0 + e2)
    p2 = e2 / (1.0 + e2)
    g_ref[...] = jnp.where(lane == i1, p1, 0.0) + jnp.where(lane == i2, p2, 0.0)


def _router(x, nw, wr, br, tm):
    T, D = x.shape
    return pl.pallas_call(
        _router_kernel,
        grid=(T // tm,),
        in_specs=[pl.BlockSpec((tm, D), lambda i: (i, 0)),
                  pl.BlockSpec((1, D), lambda i: (0, 0)),
                  pl.BlockSpec((D, LANES), lambda i: (0, 0)),
                  pl.BlockSpec((1, LANES), lambda i: (0, 0))],
        out_specs=[pl.BlockSpec((tm, D), lambda i: (i, 0)),
                   pl.BlockSpec((tm, LANES), lambda i: (i, 0))],
        out_shape=[jax.ShapeDtypeStruct((T, D), BF16), jax.ShapeDtypeStruct((T, LANES), F32)],
        compiler_params=_cparams("parallel"),
        name="moe_router",
    )(x, nw, wr, br)


def _moe_dense_kernel(x_ref, h_ref, g_ref, wg_ref, wu_ref, wd_ref, o_ref, acc_ref):
    e = pl.program_id(1)
    j = pl.program_id(2)

    @pl.when((e == 0) & (j == 0))
    def _():
        acc_ref[...] = x_ref[...]

    h = h_ref[...]
    lane = lax.broadcasted_iota(jnp.int32, g_ref.shape, 1)
    ge = jnp.sum(jnp.where(lane == e, g_ref[...], 0.0), axis=-1, keepdims=True)
    a = _silu(_dot(h, wg_ref[...])) * _dot(h, wu_ref[...])
    acc_ref[...] += _dot((a * ge).astype(BF16), wd_ref[...])

    @pl.when((e == pl.num_programs(1) - 1) & (j == pl.num_programs(2) - 1))
    def _():
        o_ref[...] = acc_ref[...]


def _moe_dense(x, h, gates, wg, wu, wd, tm, tf):
    T, D = x.shape
    E, _, F = wg.shape
    return pl.pallas_call(
        _moe_dense_kernel,
        grid=(T // tm, E, F // tf),
        in_specs=[pl.BlockSpec((tm, D), lambda i, e, j: (i, 0)),
                  pl.BlockSpec((tm, D), lambda i, e, j: (i, 0)),
                  pl.BlockSpec((tm, LANES), lambda i, e, j: (i, 0)),
                  pl.BlockSpec((None, D, tf), lambda i, e, j: (e, 0, j)),
                  pl.BlockSpec((None, D, tf), lambda i, e, j: (e, 0, j)),
                  pl.BlockSpec((None, tf, D), lambda i, e, j: (e, j, 0))],
        out_specs=pl.BlockSpec((tm, D), lambda i, e, j: (i, 0)),
        out_shape=jax.ShapeDtypeStruct((T, D), F32),
        scratch_shapes=[pltpu.VMEM((tm, D), F32)],
        compiler_params=_cparams("parallel", "arbitrary", "arbitrary"),
        name="moe_experts",
    )(x, h, gates, wg, wu, wd)


def _split_w_in(w_in):
    sizes = (512, 512, 512, 512, 768, 768, 768, 256, 256, 512, 512, 4, 4, 1024, 1024, 1024)
    offs = [0]
    for s in sizes:
        offs.append(offs[-1] + s)
    part = lambda i: w_in[:, offs[i]:offs[i + 1]]
    rq, rk, rv, rg, dq, dk, dv, mq, mk, mv, mo, mi, mf, g_ret, g_da, g_ml = (part(i) for i in range(16))
    wb = jnp.concatenate([rq, rk, rv, rg, g_ret, g_da, g_ml, mq, mk, mv, mo], axis=1).astype(BF16)
    cols = []
    for p in range(2):
        for g in range(3):
            for t in (dq, dk, dv):
                cols.append(t[:, g * 256 + p * 128:g * 256 + (p + 1) * 128])
    pad = jnp.zeros((w_in.shape[0], 2 * LANES - 8), w_in.dtype)
    wf = jnp.concatenate(cols + [mi, mf, pad], axis=1).astype(BF16)
    return wb, wf


def _rope_tables(S):
    half = LANES // 2
    inv = jnp.power(ROPE_BASE, -jnp.arange(half, dtype=F32) / half)
    ang = jnp.arange(S, dtype=F32)[:, None] * inv[None, :]
    cos = jnp.cos(ang)
    sin = jnp.sin(ang)
    return jnp.concatenate([cos, cos], axis=1), jnp.concatenate([-sin, sin], axis=1)


def kernel(x, norm1_w, w_in, ret_norm_w, da_q_norm_w, da_k_norm_w, ml_conv_w, ml_i_bias, ml_f_bias,
           ml_norm_w, w_br_ret, w_br_da, w_br_ml, w_out, norm2_w, ffn_w_gate, ffn_w_up, ffn_w_down,
           moe_w_router, moe_b_router, moe_w_gate, moe_w_up, moe_w_down):
    B, S, D = x.shape
    T = B * S
    depth = w_in.shape[0]
    cos, sin = _rope_tables(S)
    xt = x.reshape(T, D)
    for layer in range(depth):
        wb, wf = _split_w_in(w_in[layer])
        nw1 = norm1_w[layer].reshape(1, D)
        zb = _norm_matmul(xt, nw1, wb, BF16, 1024, 512, "inproj_bf16")
        zf = _norm_matmul(xt, nw1, wf, F32, 1024, 512, "inproj_f32")
        o_ret = _retention(zb, cos, sin, ret_norm_w[layer].reshape(1, -1), B, S)
        wq = jnp.tile(da_q_norm_w[layer], 2).reshape(1, LANES)
        wk = jnp.tile(da_k_norm_w[layer], 2).reshape(1, LANES)
        o_da = _dilated_attention(zf, wq, wk, B, S)
        gate_bias = jnp.concatenate([ml_i_bias[layer], ml_f_bias[layer],
                                     jnp.zeros((LANES - 2 * ML_HEADS,), F32)]).reshape(1, LANES)
        o_ml = _mlstm(zb, zf, ml_conv_w[layer], gate_bias, ml_norm_w[layer].reshape(1, -1), B, S,
                      qk_blk=10, gate_blk=18)
        xt = _merge(xt, o_ret, o_da, o_ml, zb, 2,
                    w_br_ret[layer].astype(BF16), w_br_da[layer].astype(BF16),
                    w_br_ml[layer].astype(BF16), w_out[layer].astype(BF16), 512)
        nw2 = norm2_w[layer].reshape(1, D)
        j = layer // 2
        if layer % 2 == 0:
            xt = _ffn(xt, nw2, ffn_w_gate[j].astype(BF16), ffn_w_up[j].astype(BF16),
                      ffn_w_down[j].astype(BF16), 1024, 256)
        else:
            wr = jnp.pad(moe_w_router[j], ((0, 0), (0, LANES - N_EXPERTS)))
            br = jnp.pad(moe_b_router[j], (0, LANES - N_EXPERTS)).reshape(1, LANES)
            h2, gates = _router(xt, nw2, wr, br, 1024)
            xt = _moe_dense(xt, h2, gates, moe_w_gate[j].astype(BF16), moe_w_up[j].astype(BF16),
                            moe_w_down[j].astype(BF16), 1024, 512)
    return xt.reshape(B, S, D)
```

```python
import functools
import math

import jax
import jax.numpy as jnp
from jax import lax
from jax.experimental import pallas as pl
from jax.experimental.pallas import tpu as pltpu
from jax.experimental.pallas import tpu_sc as plsc

F32 = jnp.float32
BF16 = jnp.bfloat16

EPS = 1e-6
D_MODEL = 1024
CHUNK = 128
LANES = 128
ROPE_BASE = 10000.0
RET_HEADS = 4
DA_GROUPS = ((1, 16), (4, 4), (16, 1))
DA_DH = 64
ML_HEADS = 4
ML_DK = 64
N_EXPERTS = 8
TOP_K = 2
MOE_TILE = 512
SC_WINDOW = 128
VMEM_LIMIT = 56 * 1024 * 1024

NEG = -1e30


def _cparams(*sem):
    return pltpu.CompilerParams(dimension_semantics=sem, vmem_limit_bytes=VMEM_LIMIT)


def _dot(a, b):
    return jnp.dot(a, b, preferred_element_type=F32)


def _dot_nt(a, b):
    return lax.dot_general(a, b, (((1,), (1,)), ((), ())), preferred_element_type=F32)


def _dot_tn(a, b):
    return lax.dot_general(a, b, (((0,), (0,)), ((), ())), preferred_element_type=F32)


def _sigmoid(x):
    return 1.0 / (1.0 + jnp.exp(-x))


def _silu(x):
    return x * _sigmoid(x)


def _norm_matmul_kernel(x_ref, nw_ref, w_ref, o_ref, h_ref):
    @pl.when(pl.program_id(1) == 0)
    def _():
        x = x_ref[...]
        ms = jnp.mean(x * x, axis=-1, keepdims=True)
        h_ref[...] = (x * lax.rsqrt(ms + EPS) * nw_ref[...]).astype(BF16)

    o_ref[...] = _dot(h_ref[...], w_ref[...]).astype(o_ref.dtype)


def _norm_matmul(x, nw, w, out_dtype, tm, tn, name):
    T, D = x.shape
    N = w.shape[1]
    return pl.pallas_call(
        _norm_matmul_kernel,
        grid=(T // tm, N // tn),
        in_specs=[pl.BlockSpec((tm, D), lambda i, j: (i, 0)),
                  pl.BlockSpec((1, D), lambda i, j: (0, 0)),
                  pl.BlockSpec((D, tn), lambda i, j: (0, j))],
        out_specs=pl.BlockSpec((tm, tn), lambda i, j: (i, j)),
        out_shape=jax.ShapeDtypeStruct((T, N), out_dtype),
        scratch_shapes=[pltpu.VMEM((tm, D), BF16)],
        compiler_params=_cparams("parallel", "arbitrary"),
        name=name,
    )(x, nw, w)


def _retention_kernel(z_ref, cos_ref, sin_ref, nw_ref, o_ref):
    S = z_ref.shape[0]
    n_chunks = S // CHUNK
    HW = RET_HEADS * LANES
    row = lax.broadcasted_iota(jnp.int32, (CHUNK, CHUNK), 0).astype(F32)
    col = lax.broadcasted_iota(jnp.int32, (CHUNK, CHUNK), 1).astype(F32)
    rcol = lax.broadcasted_iota(jnp.int32, (CHUNK, 1), 0).astype(F32)
    for h in range(RET_HEADS):
        lg = math.log1p(-(2.0 ** (-5.0 - h)))
        rel = row - col
        intra = jnp.where(rel >= 0, jnp.exp(lg * jnp.maximum(rel, 0.0)), 0.0)
        q_dec = jnp.exp(lg * (rcol + 1.0))
        k_dec = jnp.exp(lg * (CHUNK - 1.0 - rcol)) * (LANES ** -0.5)
        c_dec = math.exp(lg * CHUNK)
        nw = nw_ref[:, h * LANES:(h + 1) * LANES]

        def body(n, state, h=h, intra=intra, q_dec=q_dec, k_dec=k_dec, c_dec=c_dec, nw=nw):
            r0 = pl.multiple_of(n * CHUNK, CHUNK)
            rows = pl.ds(r0, CHUNK)
            cos = cos_ref[rows, :]
            sin = sin_ref[rows, :]
            q = z_ref[rows, h * LANES:(h + 1) * LANES].astype(F32)
            k = z_ref[rows, HW + h * LANES:HW + (h + 1) * LANES].astype(F32)
            v = z_ref[rows, 2 * HW + h * LANES:2 * HW + (h + 1) * LANES]
            g = z_ref[rows, 3 * HW + h * LANES:3 * HW + (h + 1) * LANES].astype(F32)
            q = q * cos + pltpu.roll(q, LANES // 2, 1) * sin
            k = k * cos + pltpu.roll(k, LANES // 2, 1) * sin
            qb = q.astype(BF16)
            kb = (k * (LANES ** -0.5)).astype(BF16)
            s = _dot_nt(qb, kb) * intra
            o = _dot(s.astype(BF16), v) + _dot(qb, state.astype(BF16)) * q_dec
            kd = (k * k_dec).astype(BF16)
            state = state * c_dec + _dot_tn(kd, v)
            ms = jnp.mean(o * o, axis=-1, keepdims=True)
            o = o * lax.rsqrt(ms + EPS) * nw
            o_ref[rows, h * LANES:(h + 1) * LANES] = (o * _silu(g)).astype(o_ref.dtype)
            return state

        lax.fori_loop(0, n_chunks, body, jnp.zeros((LANES, LANES), F32))


def _retention(zb, cos, sin, nw, B, S):
    T = B * S
    W = RET_HEADS * LANES
    return pl.pallas_call(
        _retention_kernel,
        grid=(B,),
        in_specs=[pl.BlockSpec((S, 4 * W), lambda b: (b, 0)),
                  pl.BlockSpec((S, LANES), lambda b: (0, 0)),
                  pl.BlockSpec((S, LANES), lambda b: (0, 0)),
                  pl.BlockSpec((1, W), lambda b: (0, 0))],
        out_specs=pl.BlockSpec((S, W), lambda b: (b, 0)),
        out_shape=jax.ShapeDtypeStruct((T, W), BF16),
        compiler_params=_cparams("parallel"),
        name="retention",
    )(zb, cos, sin, nw)


def _da_kernel(*refs):
    z_refs = refs[:9]
    wq_ref, wk_ref, o_ref, qn_ref, kn_ref, v_ref, np_ref, lp_ref, mp_ref = refs[9:18]
    nn_refs, ln_refs, mn_refs = refs[18:21], refs[21:24], refs[24:27]
    S = o_ref.shape[0]
    lane = lax.broadcasted_iota(jnp.int32, (1, LANES), 1)
    lo = lane < DA_DH
    row = lax.broadcasted_iota(jnp.int32, (CHUNK, CHUNK), 0)
    col = lax.broadcasted_iota(jnp.int32, (CHUNK, CHUNK), 1)
    mask_cur = col <= row
    mask_prev = col >= row

    def head_norm(x, w):
        x2 = x * x
        ms_lo = jnp.sum(jnp.where(lo, x2, 0.0), axis=-1, keepdims=True) * (1.0 / DA_DH)
        ms_hi = jnp.sum(jnp.where(lo, 0.0, x2), axis=-1, keepdims=True) * (1.0 / DA_DH)
        scale = jnp.where(lo, lax.rsqrt(ms_lo + EPS), lax.rsqrt(ms_hi + EPS))
        return x * scale * w

    RB = 256
    for g, (dil, nb) in enumerate(DA_GROUPS):
        L = S // dil
        zq_ref, zk_ref, zv_ref = z_refs[3 * g:3 * g + 3]
        for rho in range(dil):
            for piece in range(max(L // RB, 1)):
                n_rows = min(RB, L)
                src = pl.ds(rho + dil * piece * n_rows, n_rows, stride=dil) if dil > 1 else pl.ds(
                    piece * n_rows, n_rows)
                dst = pl.ds(rho * L + piece * n_rows, n_rows)
                q = zq_ref[src, :]
                k = zk_ref[src, :]
                v = zv_ref[src, :]
                qn_ref[dst, :] = (head_norm(q, wq_ref[...]) * (DA_DH ** -0.5)).astype(BF16)
                kn_ref[dst, :] = head_norm(k, wk_ref[...]).astype(BF16)
                v_ref[dst, :] = v.astype(BF16)

        def block(r0, with_prev):
            rows = pl.ds(r0, CHUNK)
            q = qn_ref[rows, :]
            zero = jnp.zeros_like(q)
            qs = (jnp.where(lo, q, zero), jnp.where(lo, zero, q))
            kc = kn_ref[rows, :]
            vc = v_ref[rows, :]
            one = jnp.ones_like(vc)
            if with_prev:
                prow = pl.ds(r0 - CHUNK, CHUNK)
                kp = kn_ref[prow, :]
                vp = v_ref[prow, :]
            accs, ms = [], []
            for hs in range(2):
                sc = jnp.where(mask_cur, _dot_nt(qs[hs], kc), NEG)
                m = jnp.max(sc, axis=-1, keepdims=True)
                if with_prev:
                    sp = jnp.where(mask_prev, _dot_nt(qs[hs], kp), NEG)
                    m = jnp.maximum(m, jnp.max(sp, axis=-1, keepdims=True))
                vaug_c = jnp.where(lo, vc, one) if hs == 0 else jnp.where(lo, one, vc)
                acc = _dot(jnp.exp(sc - m).astype(BF16), vaug_c)
                if with_prev:
                    vaug_p = jnp.where(lo, vp, one) if hs == 0 else jnp.where(lo, one, vp)
                    acc = acc + _dot(jnp.exp(sp - m).astype(BF16), vaug_p)
                accs.append(acc)
                ms.append(m)
            sw0 = pltpu.roll(accs[0], DA_DH, 1)
            sw1 = pltpu.roll(accs[1], DA_DH, 1)
            np_ref[rows, :] = jnp.where(lo, accs[0], accs[1])
            lp_ref[rows, :] = jnp.where(lo, sw0, sw1)
            mp_ref[rows, :] = jnp.where(lo, ms[0], ms[1])

        def segment(sgi, carry, nb=nb):
            base = pl.multiple_of(sgi * (nb * CHUNK), CHUNK)
            block(base, False)
            if nb > 1:
                def inner(n, c):
                    block(pl.multiple_of(base + n * CHUNK, CHUNK), True)
                    return c
                lax.fori_loop(1, nb, inner, 0)
            return carry

        lax.fori_loop(0, dil, segment, 0)

        for rho in range(dil):
            for piece in range(max(L // RB, 1)):
                n_rows = min(RB, L)
                dst = pl.ds(rho + dil * piece * n_rows, n_rows, stride=dil) if dil > 1 else pl.ds(
                    piece * n_rows, n_rows)
                src = pl.ds(rho * L + piece * n_rows, n_rows)
                nn_refs[g][dst, :] = np_ref[src, :]
                ln_refs[g][dst, :] = lp_ref[src, :]
                mn_refs[g][dst, :] = mp_ref[src, :]

    for piece in range(S // RB):
        rows = pl.ds(piece * RB, RB)
        m = jnp.maximum(jnp.maximum(mn_refs[0][rows, :], mn_refs[1][rows, :]), mn_refs[2][rows, :])
        num = jnp.zeros((RB, LANES), F32)
        den = jnp.zeros((RB, LANES), F32)
        for g in range(3):
            e = jnp.exp(mn_refs[g][rows, :] - m)
            num = num + e * nn_refs[g][rows, :]
            den = den + e * ln_refs[g][rows, :]
        o_ref[rows, :] = (num / den).astype(o_ref.dtype)


def _dilated_attention(zf, wq, wk, B, S):
    T = B * S
    slab = lambda k: pl.BlockSpec((S, LANES), lambda b, p: (b, 9 * p + k))
    return pl.pallas_call(
        _da_kernel,
        grid=(B, 2),
        in_specs=[slab(k) for k in range(9)] + [
                  pl.BlockSpec((1, LANES), lambda b, p: (0, 0)),
                  pl.BlockSpec((1, LANES), lambda b, p: (0, 0))],
        out_specs=pl.BlockSpec((S, LANES), lambda b, p: (b, p)),
        out_shape=jax.ShapeDtypeStruct((T, 2 * LANES), BF16),
        scratch_shapes=[pltpu.VMEM((S, LANES), BF16), pltpu.VMEM((S, LANES), BF16),
                        pltpu.VMEM((S, LANES), BF16),
                        pltpu.VMEM((S, LANES), F32), pltpu.VMEM((S, LANES), F32),
                        pltpu.VMEM((S, LANES), F32),
                        pltpu.VMEM((S, LANES), F32)] + [pltpu.VMEM((S, LANES), F32)] * 9,
        compiler_params=_cparams("parallel", "arbitrary"),
        name="dilated_attention",
    )(*([zf] * 9), wq, wk)


def _mlstm_kernel(qk_ref, v_ref, og_ref, gate_ref, cw_ref, gb_ref, nw_ref, o_ref,
                  xp_ref, qc_ref, st_ref):
    S = qk_ref.shape[0]
    n_chunks = S // CHUNK
    W = qk_ref.shape[1]
    PAD = 8
    xp_ref[0:PAD, :] = jnp.zeros((PAD, W), F32)
    RB = 256
    for piece in range(S // RB):
        xp_ref[PAD + piece * RB:PAD + (piece + 1) * RB, :] = qk_ref[piece * RB:(piece + 1) * RB, :].astype(F32)
    lane_w = lax.broadcasted_iota(jnp.int32, (1, W), 1)
    kscale = jnp.where(lane_w >= W // 2, ML_DK ** -0.5, 1.0).astype(F32)
    for piece in range(S // RB):
        acc = jnp.zeros((RB, W), F32)
        for i in range(4):
            acc = acc + cw_ref[i:i + 1, :] * xp_ref[pl.ds(PAD + piece * RB - 3 + i, RB), :]
        qc_ref[piece * RB:(piece + 1) * RB, :] = (_silu(acc) * kscale).astype(BF16)

    lane = lax.broadcasted_iota(jnp.int32, (1, LANES), 1)
    lo = lane < ML_DK
    row = lax.broadcasted_iota(jnp.int32, (CHUNK, CHUNK), 0)
    col = lax.broadcasted_iota(jnp.int32, (CHUNK, CHUNK), 1)
    causal = col <= row
    tril = causal.astype(F32)
    srow = lax.broadcasted_iota(jnp.int32, (LANES, 1), 0) < ML_DK
    st_ref[...] = jnp.zeros(st_ref.shape, F32)

    def chunk(n, mms):
        r0 = pl.multiple_of(n * CHUNK, CHUNK)
        rows = pl.ds(r0, CHUNK)
        gp = gate_ref[rows, :] + gb_ref[...]
        logf = jnp.minimum(gp, 0.0) - jnp.log(1.0 + jnp.exp(-jnp.abs(gp)))
        is_f = (lane >= ML_HEADS) & (lane < 2 * ML_HEADS)
        cum = jnp.dot(tril, jnp.where(is_f, logf, 0.0), preferred_element_type=F32,
                      precision=lax.Precision.HIGHEST)
        e = jnp.where(lane < ML_HEADS, gp, cum)
        et = e.T
        new_mms = []
        for pair in range(2):
            q2 = qc_ref[rows, pair * LANES:(pair + 1) * LANES]
            k2 = qc_ref[rows, W // 2 + pair * LANES:W // 2 + (pair + 1) * LANES]
            zero = jnp.zeros_like(q2)
            st = st_ref[pair]
            stb = st.astype(BF16)
            upd = jnp.zeros((LANES, 2 * LANES), F32)
            wcs = []
            for hs in range(2):
                h = 2 * pair + hs
                mm = mms[h]
                i_col = e[:, h:h + 1]
                a_col = e[:, ML_HEADS + h:ML_HEADS + h + 1]
                c_row = et[h:h + 1, :] - et[ML_HEADS + h:ML_HEADS + h + 1, :]
                dmat = jnp.where(causal, a_col + c_row, NEG)
                inter = a_col + mm
                m = jnp.maximum(inter, jnp.max(dmat, axis=-1, keepdims=True))
                dw = jnp.exp(dmat - m)
                wi = jnp.exp(inter - m)
                qh = jnp.where(lo, q2, zero) if hs == 0 else jnp.where(lo, zero, q2)
                kh = jnp.where(lo, k2, zero) if hs == 0 else jnp.where(lo, zero, k2)
                vh = v_ref[rows, h * LANES:(h + 1) * LANES]
                vaug = jnp.concatenate([vh, jnp.ones_like(vh)], axis=-1)
                s = _dot_nt(qh, k2) * dw
                acc = _dot(s.astype(BF16), vaug) + wi * _dot(qh, stb)
                num = acc[:, :LANES]
                den = acc[:, LANES:]
                hv = num / jnp.maximum(jnp.abs(den), jnp.exp(-m))
                ms = jnp.mean(hv * hv, axis=-1, keepdims=True)
                hv = hv * lax.rsqrt(ms + EPS) * nw_ref[:, h * LANES:(h + 1) * LANES]
                og = og_ref[rows, h * LANES:(h + 1) * LANES].astype(F32)
                o_ref[rows, h * LANES:(h + 1) * LANES] = (hv * _sigmoid(og)).astype(o_ref.dtype)
                m_new = m[CHUNK - 1:CHUNK, :]
                a_last = a_col[CHUNK - 1:CHUNK, :]
                wk = jnp.exp(a_last - a_col + i_col - m_new)
                wcs.append(jnp.exp(a_last + mm - m_new))
                upd = upd + _dot_tn((kh.astype(F32) * wk).astype(BF16), vaug)
                new_mms.append(m_new)
            st_ref[pair] = jnp.where(srow, wcs[0], wcs[1]) * st + upd
        return tuple(new_mms)

    lax.fori_loop(0, n_chunks, chunk, tuple(jnp.zeros((1, 1), F32) for _ in range(ML_HEADS)))


def _mlstm(zb, zf, conv_w, gate_bias, nw, B, S, qk_blk, gate_blk):
    T = B * S
    W = ML_HEADS * LANES
    return pl.pallas_call(
        _mlstm_kernel,
        grid=(B,),
        in_specs=[pl.BlockSpec((S, W), lambda b: (b, qk_blk)),
                  pl.BlockSpec((S, W), lambda b: (b, qk_blk + 1)),
                  pl.BlockSpec((S, W), lambda b: (b, qk_blk + 2)),
                  pl.BlockSpec((S, LANES), lambda b: (b, gate_blk)),
                  pl.BlockSpec((4, W), lambda b: (0, 0)),
                  pl.BlockSpec((1, LANES), lambda b: (0, 0)),
                  pl.BlockSpec((1, W), lambda b: (0, 0))],
        out_specs=pl.BlockSpec((S, W), lambda b: (b, 0)),
        out_shape=jax.ShapeDtypeStruct((T, W), BF16),
        scratch_shapes=[pltpu.VMEM((S + 8, W), F32), pltpu.VMEM((S, W), BF16),
                        pltpu.VMEM((2, LANES, 2 * LANES), F32)],
        compiler_params=_cparams("parallel"),
        name="mlstm",
    )(zb, zb, zb, zf, conv_w, gate_bias, nw)


def _merge_kernel(x_ref, oret_ref, oda_ref, oml_ref, gr_ref, gd_ref, gm_ref,
                  wr_ref, wd_ref, wm_ref, wo_ref, o_ref):
    y = _sigmoid(gr_ref[...].astype(F32)) * _dot(oret_ref[...], wr_ref[...])
    y = y + _sigmoid(gd_ref[...].astype(F32)) * _dot(oda_ref[...], wd_ref[...])
    y = y + _sigmoid(gm_ref[...].astype(F32)) * _dot(oml_ref[...], wm_ref[...])
    o_ref[...] = x_ref[...] + _dot(y.astype(BF16), wo_ref[...])


def _merge(x, o_ret, o_da, o_ml, zb, g_blk, wr, wd, wm, wo, tm):
    T, D = x.shape
    full = lambda a: pl.BlockSpec(a.shape, lambda i: (0, 0))
    return pl.pallas_call(
        _merge_kernel,
        grid=(T // tm,),
        in_specs=[pl.BlockSpec((tm, D), lambda i: (i, 0)),
                  pl.BlockSpec((tm, o_ret.shape[1]), lambda i: (i, 0)),
                  pl.BlockSpec((tm, o_da.shape[1]), lambda i: (i, 0)),
                  pl.BlockSpec((tm, o_ml.shape[1]), lambda i: (i, 0)),
                  pl.BlockSpec((tm, D), lambda i: (i, g_blk)),
                  pl.BlockSpec((tm, D), lambda i: (i, g_blk + 1)),
                  pl.BlockSpec((tm, D), lambda i: (i, g_blk + 2)),
                  full(wr), full(wd), full(wm), full(wo)],
        out_specs=pl.BlockSpec((tm, D), lambda i: (i, 0)),
        out_shape=jax.ShapeDtypeStruct((T, D), F32),
        compiler_params=_cparams("parallel"),
        name="merge_outproj",
    )(x, o_ret, o_da, o_ml, zb, zb, zb, wr, wd, wm, wo)


def _ffn_kernel(x_ref, nw_ref, wg_ref, wu_ref, wd_ref, o_ref, h_ref, acc_ref):
    j = pl.program_id(1)

    @pl.when(j == 0)
    def _():
        x = x_ref[...]
        ms = jnp.mean(x * x, axis=-1, keepdims=True)
        h_ref[...] = (x * lax.rsqrt(ms + EPS) * nw_ref[...]).astype(BF16)
        acc_ref[...] = x

    h = h_ref[...]
    a = _silu(_dot(h, wg_ref[...])) * _dot(h, wu_ref[...])
    acc_ref[...] += _dot(a.astype(BF16), wd_ref[...])

    @pl.when(j == pl.num_programs(1) - 1)
    def _():
        o_ref[...] = acc_ref[...]


def _ffn(x, nw, wg, wu, wd, tm, tf):
    T, D = x.shape
    F = wg.shape[1]
    return pl.pallas_call(
        _ffn_kernel,
        grid=(T // tm, F // tf),
        in_specs=[pl.BlockSpec((tm, D), lambda i, j: (i, 0)),
                  pl.BlockSpec((1, D), lambda i, j: (0, 0)),
                  pl.BlockSpec((D, tf), lambda i, j: (0, j)),
                  pl.BlockSpec((D, tf), lambda i, j: (0, j)),
                  pl.BlockSpec((tf, D), lambda i, j: (j, 0))],
        out_specs=pl.BlockSpec((tm, D), lambda i, j: (i, 0)),
        out_shape=jax.ShapeDtypeStruct((T, D), F32),
        scratch_shapes=[pltpu.VMEM((tm, D), BF16), pltpu.VMEM((tm, D), F32)],
        compiler_params=_cparams("parallel", "arbitrary"),
        name="swiglu_ffn",
    )(x, nw, wg, wu, wd)


HI16 = 0xFFFF0000


def _pack_bf16_pairs(a, b):
    ab = pltpu.bitcast(a.astype(BF16).astype(F32), jnp.uint32)
    bb = pltpu.bitcast(b.astype(BF16).astype(F32), jnp.uint32)
    return (ab >> 16) | (bb & jnp.uint32(HI16))


def _unpack_bf16_pairs(p):
    lo = pltpu.bitcast(p << 16, F32)
    hi = pltpu.bitcast(p & jnp.uint32(HI16), F32)
    return lo, hi


def _router_kernel(x_ref, nw_ref, wr_ref, br_ref, ha_ref, hb_ref, sel_ref):
    x = x_ref[...]
    ms = jnp.mean(x * x, axis=-1, keepdims=True)
    h = x * lax.rsqrt(ms + EPS) * nw_ref[...]
    Q = h.shape[1] // 4
    ha_ref[...] = _pack_bf16_pairs(h[:, 0:Q], h[:, Q:2 * Q])
    hb_ref[...] = _pack_bf16_pairs(h[:, 2 * Q:3 * Q], h[:, 3 * Q:4 * Q])
    logits = jnp.dot(h, wr_ref[...], preferred_element_type=F32,
                     precision=lax.Precision.HIGHEST) + br_ref[...]
    lane = lax.broadcasted_iota(jnp.int32, logits.shape, 1).astype(F32)
    logits = jnp.where(lane < N_EXPERTS, logits, NEG)
    m1 = jnp.max(logits, axis=-1, keepdims=True)
    i1 = jnp.min(jnp.where(logits == m1, lane, float(LANES)), axis=-1, keepdims=True)
    rest = jnp.where(lane == i1, NEG, logits)
    m2 = jnp.max(rest, axis=-1, keepdims=True)
    i2 = jnp.min(jnp.where(rest == m2, lane, float(LANES)), axis=-1, keepdims=True)
    e2 = jnp.exp(m2 - m1)
    p1 = 1.0 / (1.0 + e2)
    p2 = e2 / (1.0 + e2)
    sel_ref[...] = jnp.where(lane == 0.0, i1, jnp.where(lane == 1.0, i2,
                             jnp.where(lane == 2.0, p1, jnp.where(lane == 3.0, p2, 0.0))))


def _router(x, nw, wr, br, tm):
    T, D = x.shape
    Q = D // 4
    return pl.pallas_call(
        _router_kernel,
        grid=(T // tm,),
        in_specs=[pl.BlockSpec((tm, D), lambda i: (i, 0)),
                  pl.BlockSpec((1, D), lambda i: (0, 0)),
                  pl.BlockSpec((D, LANES), lambda i: (0, 0)),
                  pl.BlockSpec((1, LANES), lambda i: (0, 0))],
        out_specs=[pl.BlockSpec((tm, Q), lambda i: (i, 0)),
                   pl.BlockSpec((tm, Q), lambda i: (i, 0)),
                   pl.BlockSpec((tm, LANES), lambda i: (i, 0))],
        out_shape=[jax.ShapeDtypeStruct((T, Q), jnp.uint32), jax.ShapeDtypeStruct((T, Q), jnp.uint32),
                   jax.ShapeDtypeStruct((T, LANES), F32)],
        compiler_params=_cparams("parallel"),
        name="moe_router",
    )(x, nw, wr, br)


def _moe_rank_kernel(sel_ref, pos_ref, meta_ref, cnt_ref, offs_ref, carry_ref):
    ph = pl.program_id(0)
    i = pl.program_id(1)
    tm = sel_ref.shape[0]
    lane = lax.broadcasted_iota(jnp.int32, (tm, LANES), 1).astype(F32)
    lane1 = lax.broadcasted_iota(jnp.int32, (1, LANES), 1).astype(F32)
    sel = sel_ref[...]
    i1 = sel[:, 0:1]
    i2 = sel[:, 1:2]
    onehot = jnp.where((lane == i1) | (lane == i2), 1.0, 0.0)
    colsum = jnp.sum(onehot, axis=0, keepdims=True)

    @pl.when((ph == 0) & (i == 0))
    def _():
        cnt_ref[...] = jnp.zeros(cnt_ref.shape, F32)

    @pl.when(ph == 0)
    def _():
        cnt_ref[...] += colsum

    def padded_counts():
        return jnp.floor((cnt_ref[...] + (MOE_TILE - 1.0)) * (1.0 / MOE_TILE)) * MOE_TILE

    @pl.when((ph == 1) & (i == 0))
    def _():
        k = lax.broadcasted_iota(jnp.int32, (LANES, LANES), 0)
        e = lax.broadcasted_iota(jnp.int32, (LANES, LANES), 1)
        upper = (k < e).astype(F32)
        offs_ref[...] = jnp.dot(padded_counts(), upper, preferred_element_type=F32,
                                precision=lax.Precision.HIGHEST)
        carry_ref[...] = jnp.zeros(carry_ref.shape, F32)

    @pl.when(ph == 1)
    def _():
        r = lax.broadcasted_iota(jnp.int32, (tm, tm), 0)
        c = lax.broadcasted_iota(jnp.int32, (tm, tm), 1)
        before = (c < r).astype(BF16)
        rank = _dot(before, onehot.astype(BF16)) + carry_ref[0:1, :]
        row = rank + offs_ref[0:1, :]
        pos1 = jnp.sum(jnp.where(lane == i1, row, 0.0), axis=-1, keepdims=True)
        pos2 = jnp.sum(jnp.where(lane == i2, row, 0.0), axis=-1, keepdims=True)
        both = jnp.where(lane == 0.0, pos1, jnp.where(lane == 1.0, pos2, 0.0))
        pos_ref[...] = both.T[0:8, :].astype(jnp.int32)
        carry_ref[...] += colsum

    @pl.when((ph == 1) & (i == pl.num_programs(1) - 1))
    def _():
        nrow = meta_ref.shape[0]
        padded = padded_counts()[0:1, :]
        offs = offs_ref[0:1, :]
        ends = offs + padded
        start = lax.broadcasted_iota(jnp.int32, (nrow, 1), 0).astype(F32) * MOE_TILE
        is_e = lane1 < N_EXPERTS
        te = jnp.sum(jnp.where(is_e & (ends <= start), 1.0, 0.0), axis=-1, keepdims=True)
        te = jnp.minimum(te, N_EXPERTS - 1.0)
        valid_end = jnp.sum(jnp.where(lane1 == te, offs + cnt_ref[0:1, :], 0.0), axis=-1, keepdims=True)
        nv = jnp.clip(valid_end - start, 0.0, MOE_TILE)
        total = jnp.sum(jnp.where(is_e, padded, 0.0), axis=-1, keepdims=True)
        active = jnp.where(start < total, 1.0, 0.0)
        meta_ref[...] = jnp.where(lane1 == 0.0, te, jnp.where(lane1 == 1.0, nv,
                                  jnp.where(lane1 == 2.0, active, 0.0)))


def _moe_rank(sel, tm, meta_rows):
    T = sel.shape[0]
    nt = T // tm
    return pl.pallas_call(
        _moe_rank_kernel,
        grid=(2, nt),
        in_specs=[pl.BlockSpec((tm, LANES), lambda ph, i: (i, 0))],
        out_specs=[pl.BlockSpec((None, 8, tm), lambda ph, i: (i * ph, 0, 0)),
                   pl.BlockSpec((meta_rows, LANES), lambda ph, i: (0, 0))],
        out_shape=[jax.ShapeDtypeStruct((nt, 8, tm), jnp.int32),
                   jax.ShapeDtypeStruct((meta_rows, LANES), F32)],
        scratch_shapes=[pltpu.VMEM((8, LANES), F32), pltpu.VMEM((8, LANES), F32),
                        pltpu.VMEM((8, LANES), F32)],
        compiler_params=_cparams("arbitrary", "arbitrary"),
        name="moe_rank",
    )(sel)


def _sc_mesh():
    return plsc.VectorSubcoreMesh(core_axis_name="core", subcore_axis_name="subcore")


def _sc_scatter_rows(x, idx, n_rows):
    n_idx = idx.shape[1]
    nt = x.shape[0] // SC_WINDOW
    width = x.shape[1]

    @functools.partial(pl.kernel, out_type=jax.ShapeDtypeStruct((n_rows, width), x.dtype),
                       mesh=_sc_mesh())
    def scatter_kernel(x_hbm, i_hbm, o_hbm):
        def body(x_vmem, i_vmem):
            pltpu.sync_copy(x_vmem, o_hbm.at[i_vmem.at[0]])

        pltpu.emit_pipeline(
            body, grid=(n_idx // SC_WINDOW,),
            in_specs=[pl.BlockSpec((SC_WINDOW, width), lambda i: (i % nt, 0)),
                      pl.BlockSpec((1, SC_WINDOW), lambda i: (0, i))],
            out_specs=[],
            core_axis_name=("core", "subcore"), dimension_semantics=(pltpu.PARALLEL,),
        )(x_hbm, i_hbm)

    return scatter_kernel(x, idx)


def _sc_gather_rows(x, idx):
    n_idx = idx.shape[1]
    width = x.shape[1]

    @functools.partial(pl.kernel, out_type=jax.ShapeDtypeStruct((n_idx, width), x.dtype),
                       mesh=_sc_mesh())
    def gather_kernel(x_hbm, i_hbm, o_hbm):
        def body(i_vmem, o_vmem):
            pltpu.sync_copy(x_hbm.at[i_vmem.at[0]], o_vmem)

        pltpu.emit_pipeline(
            body, grid=(n_idx // SC_WINDOW,),
            in_specs=[pl.BlockSpec((1, SC_WINDOW), lambda i: (0, i))],
            out_specs=[pl.BlockSpec((SC_WINDOW, width), lambda i: (i, 0))],
            core_axis_name=("core", "subcore"), dimension_semantics=(pltpu.PARALLEL,),
        )(i_hbm, o_hbm)

    return gather_kernel(x, idx)


def _moe_group_kernel(te_ref, nv_ref, na_ref, xa_ref, xb_ref, wg_ref, wu_ref, wd_ref,
                      ya_ref, yb_ref, h_ref, acc_ref):
    j = pl.program_id(0)
    f = pl.program_id(1)

    @pl.when(j < na_ref[0])
    def _():
        @pl.when(f == 0)
        def _():
            tm = h_ref.shape[0]
            Q = xa_ref.shape[1]
            valid = lax.broadcasted_iota(jnp.int32, (tm, 1), 0) < nv_ref[j]
            for src, c0 in ((xa_ref, 0), (xb_ref, 2 * Q)):
                lo, hi = _unpack_bf16_pairs(src[...])
                h_ref[:, c0:c0 + Q] = jnp.where(valid, lo, 0.0).astype(BF16)
                h_ref[:, c0 + Q:c0 + 2 * Q] = jnp.where(valid, hi, 0.0).astype(BF16)
            acc_ref[...] = jnp.zeros(acc_ref.shape, F32)

        h = h_ref[...]
        a = _silu(_dot(h, wg_ref[...])) * _dot(h, wu_ref[...])
        acc_ref[...] += _dot(a.astype(BF16), wd_ref[...])

        @pl.when(f == pl.num_programs(1) - 1)
        def _():
            Q = ya_ref.shape[1]
            ya_ref[...] = _pack_bf16_pairs(acc_ref[:, 0:Q], acc_ref[:, Q:2 * Q])
            yb_ref[...] = _pack_bf16_pairs(acc_ref[:, 2 * Q:3 * Q], acc_ref[:, 3 * Q:4 * Q])


def _moe_group(te, nv, na, xa, xb, wg, wu, wd, tf):
    R, Q = xa.shape
    E, D, F = wg.shape
    nf = F // tf
    tile = lambda j, f, te, nv, na: (jnp.minimum(j, na[0] - 1), 0)
    ff = lambda j, f, na: jnp.where(j < na[0], f, nf - 1)
    grid_spec = pltpu.PrefetchScalarGridSpec(
        num_scalar_prefetch=3,
        grid=(R // MOE_TILE, nf),
        in_specs=[pl.BlockSpec((MOE_TILE, Q), tile),
                  pl.BlockSpec((MOE_TILE, Q), tile),
                  pl.BlockSpec((None, D, tf), lambda j, f, te, nv, na: (te[j], 0, ff(j, f, na))),
                  pl.BlockSpec((None, D, tf), lambda j, f, te, nv, na: (te[j], 0, ff(j, f, na))),
                  pl.BlockSpec((None, tf, D), lambda j, f, te, nv, na: (te[j], ff(j, f, na), 0))],
        out_specs=[pl.BlockSpec((MOE_TILE, Q), tile), pl.BlockSpec((MOE_TILE, Q), tile)],
        scratch_shapes=[pltpu.VMEM((MOE_TILE, D), BF16), pltpu.VMEM((MOE_TILE, D), F32)],
    )
    return pl.pallas_call(
        _moe_group_kernel,
        grid_spec=grid_spec,
        out_shape=[jax.ShapeDtypeStruct((R, Q), jnp.uint32), jax.ShapeDtypeStruct((R, Q), jnp.uint32)],
        compiler_params=_cparams("arbitrary", "arbitrary"),
        name="moe_experts",
    )(te, nv, na, xa, xb, wg, wu, wd)


def _moe_combine_kernel(x_ref, sel_ref, a1_ref, a2_ref, b1_ref, b2_ref, o_ref):
    sel = sel_ref[...]
    p1 = sel[:, 2:3]
    p2 = sel[:, 3:4]
    Q = a1_ref.shape[1]
    for first, second, c0 in ((a1_ref, a2_ref, 0), (b1_ref, b2_ref, 2 * Q)):
        lo1, hi1 = _unpack_bf16_pairs(first[...])
        lo2, hi2 = _unpack_bf16_pairs(second[...])
        o_ref[:, c0:c0 + Q] = x_ref[:, c0:c0 + Q] + (p1 * lo1 + p2 * lo2)
        o_ref[:, c0 + Q:c0 + 2 * Q] = x_ref[:, c0 + Q:c0 + 2 * Q] + (p1 * hi1 + p2 * hi2)


def _moe_combine(x, sel, ga, gb, tm):
    T, D = x.shape
    Q = ga.shape[1]
    nt = T // tm
    return pl.pallas_call(
        _moe_combine_kernel,
        grid=(nt,),
        in_specs=[pl.BlockSpec((tm, D), lambda i: (i, 0)),
                  pl.BlockSpec((tm, LANES), lambda i: (i, 0)),
                  pl.BlockSpec((tm, Q), lambda i: (i, 0)),
                  pl.BlockSpec((tm, Q), lambda i: (i + nt, 0)),
                  pl.BlockSpec((tm, Q), lambda i: (i, 0)),
                  pl.BlockSpec((tm, Q), lambda i: (i + nt, 0))],
        out_specs=pl.BlockSpec((tm, D), lambda i: (i, 0)),
        out_shape=jax.ShapeDtypeStruct((T, D), F32),
        compiler_params=_cparams("parallel"),
        name="moe_combine",
    )(x, sel, ga, ga, gb, gb)


def _moe(x, nw, w_router, b_router, wg, wu, wd):
    T, D = x.shape
    wr = jnp.pad(w_router, ((0, 0), (0, LANES - N_EXPERTS)))
    br = jnp.pad(b_router, (0, LANES - N_EXPERTS)).reshape(1, LANES)
    ha, hb, sel = _router(x, nw, wr, br, 1024)
    n_rows = TOP_K * T + N_EXPERTS * MOE_TILE
    n_tiles = n_rows // MOE_TILE
    pos, meta = _moe_rank(sel, 512, 256)
    idx = jnp.concatenate([pos[:, 0, :].reshape(1, T), pos[:, 1, :].reshape(1, T)], axis=1)
    te = meta[:n_tiles, 0].astype(jnp.int32)
    nv = meta[:n_tiles, 1].astype(jnp.int32)
    na = jnp.sum(meta[:n_tiles, 2]).astype(jnp.int32).reshape(1)
    te = jnp.where(jnp.arange(n_tiles) < na[0], te, te[na[0] - 1])
    xa = _sc_scatter_rows(ha, idx, n_rows)
    xb = _sc_scatter_rows(hb, idx, n_rows)
    ya, yb = _moe_group(te, nv, na, xa, xb, wg, wu, wd, 512)
    ga = _sc_gather_rows(ya, idx)
    gb = _sc_gather_rows(yb, idx)
    return _moe_combine(x, sel, ga, gb, 1024)


def _split_w_in(w_in):
    sizes = (512, 512, 512, 512, 768, 768, 768, 256, 256, 512, 512, 4, 4, 1024, 1024, 1024)
    offs = [0]
    for s in sizes:
        offs.append(offs[-1] + s)
    part = lambda i: w_in[:, offs[i]:offs[i + 1]]
    rq, rk, rv, rg, dq, dk, dv, mq, mk, mv, mo, mi, mf, g_ret, g_da, g_ml = (part(i) for i in range(16))
    wb = jnp.concatenate([rq, rk, rv, rg, g_ret, g_da, g_ml, mq, mk, mv, mo], axis=1).astype(BF16)
    cols = []
    for p in range(2):
        for g in range(3):
            for t in (dq, dk, dv):
                cols.append(t[:, g * 256 + p * 128:g * 256 + (p + 1) * 128])
    pad = jnp.zeros((w_in.shape[0], 2 * LANES - 8), w_in.dtype)
    wf = jnp.concatenate(cols + [mi, mf, pad], axis=1).astype(BF16)
    return wb, wf


def _rope_tables(S):
    half = LANES // 2
    inv = jnp.power(ROPE_BASE, -jnp.arange(half, dtype=F32) / half)
    ang = jnp.arange(S, dtype=F32)[:, None] * inv[None, :]
    cos = jnp.cos(ang)
    sin = jnp.sin(ang)
    return jnp.concatenate([cos, cos], axis=1), jnp.concatenate([-sin, sin], axis=1)


def kernel(x, norm1_w, w_in, ret_norm_w, da_q_norm_w, da_k_norm_w, ml_conv_w, ml_i_bias, ml_f_bias,
           ml_norm_w, w_br_ret, w_br_da, w_br_ml, w_out, norm2_w, ffn_w_gate, ffn_w_up, ffn_w_down,
           moe_w_router, moe_b_router, moe_w_gate, moe_w_up, moe_w_down):
    B, S, D = x.shape
    T = B * S
    depth = w_in.shape[0]
    cos, sin = _rope_tables(S)
    xt = x.reshape(T, D)
    for layer in range(depth):
        wb, wf = _split_w_in(w_in[layer])
        nw1 = norm1_w[layer].reshape(1, D)
        zb = _norm_matmul(xt, nw1, wb, BF16, 1024, 512, "inproj_bf16")
        zf = _norm_matmul(xt, nw1, wf, F32, 1024, 512, "inproj_f32")
        o_ret = _retention(zb, cos, sin, ret_norm_w[layer].reshape(1, -1), B, S)
        wq = jnp.tile(da_q_norm_w[layer], 2).reshape(1, LANES)
        wk = jnp.tile(da_k_norm_w[layer], 2).reshape(1, LANES)
        o_da = _dilated_attention(zf, wq, wk, B, S)
        gate_bias = jnp.concatenate([ml_i_bias[layer], ml_f_bias[layer],
                                     jnp.zeros((LANES - 2 * ML_HEADS,), F32)]).reshape(1, LANES)
        o_ml = _mlstm(zb, zf, ml_conv_w[layer], gate_bias, ml_norm_w[layer].reshape(1, -1), B, S,
                      qk_blk=10, gate_blk=18)
        xt = _merge(xt, o_ret, o_da, o_ml, zb, 2,
                    w_br_ret[layer].astype(BF16), w_br_da[layer].astype(BF16),
                    w_br_ml[layer].astype(BF16), w_out[layer].astype(BF16), 512)
        nw2 = norm2_w[layer].reshape(1, D)
        j = layer // 2
        if layer % 2 == 0:
            xt = _ffn(xt, nw2, ffn_w_gate[j].astype(BF16), ffn_w_up[j].astype(BF16),
                      ffn_w_down[j].astype(BF16), 1024, 256)
        else:
            xt = _moe(xt, nw2, moe_w_router[j], moe_b_router[j], moe_w_gate[j].astype(BF16),
                      moe_w_up[j].astype(BF16), moe_w_down[j].astype(BF16))
    return xt.reshape(B, S, D)
```

```python
import functools
import math

import jax
import jax.numpy as jnp
from jax import lax
from jax.experimental import pallas as pl
from jax.experimental.pallas import tpu as pltpu
from jax.experimental.pallas import tpu_sc as plsc

F32 = jnp.float32
BF16 = jnp.bfloat16

EPS = 1e-6
D_MODEL = 1024
CHUNK = 128
LANES = 128
ROPE_BASE = 10000.0
RET_HEADS = 4
DA_GROUPS = ((1, 16), (4, 4), (16, 1))
DA_DH = 64
DA_UNROLL = 4
ML_HEADS = 4
ML_DK = 64
N_EXPERTS = 8
TOP_K = 2
MOE_TILE = 512
SC_WINDOW = 128
VMEM_LIMIT = 56 * 1024 * 1024

NEG = -1e30


def _cparams(*sem):
    return pltpu.CompilerParams(dimension_semantics=sem, vmem_limit_bytes=VMEM_LIMIT)


def _dot(a, b):
    return jnp.dot(a, b, preferred_element_type=F32)


def _dot_nt(a, b):
    return lax.dot_general(a, b, (((1,), (1,)), ((), ())), preferred_element_type=F32)


def _dot_tn(a, b):
    return lax.dot_general(a, b, (((0,), (0,)), ((), ())), preferred_element_type=F32)


def _sigmoid(x):
    return 1.0 / (1.0 + jnp.exp(-x))


def _silu(x):
    return x * _sigmoid(x)


def _norm_matmul_kernel(x_ref, nw_ref, w_ref, o_ref, h_ref):
    @pl.when(pl.program_id(1) == 0)
    def _():
        x = x_ref[...]
        ms = jnp.mean(x * x, axis=-1, keepdims=True)
        h_ref[...] = (x * lax.rsqrt(ms + EPS) * nw_ref[...]).astype(BF16)

    o_ref[...] = _dot(h_ref[...], w_ref[...]).astype(o_ref.dtype)


def _norm_matmul(x, nw, w, out_dtype, tm, tn, name):
    T, D = x.shape
    N = w.shape[1]
    return pl.pallas_call(
        _norm_matmul_kernel,
        grid=(T // tm, N // tn),
        in_specs=[pl.BlockSpec((tm, D), lambda i, j: (i, 0)),
                  pl.BlockSpec((1, D), lambda i, j: (0, 0)),
                  pl.BlockSpec((D, tn), lambda i, j: (0, j))],
        out_specs=pl.BlockSpec((tm, tn), lambda i, j: (i, j)),
        out_shape=jax.ShapeDtypeStruct((T, N), out_dtype),
        scratch_shapes=[pltpu.VMEM((tm, D), BF16)],
        compiler_params=_cparams("parallel", "arbitrary"),
        name=name,
    )(x, nw, w)


def _retention_kernel(z_ref, cos_ref, sin_ref, nw_ref, o_ref, dec_ref, st_ref):
    S = z_ref.shape[0]
    n_chunks = S // CHUNK
    H = RET_HEADS
    HW = H * LANES
    row = lax.broadcasted_iota(jnp.int32, (CHUNK, CHUNK), 0).astype(F32)
    col = lax.broadcasted_iota(jnp.int32, (CHUNK, CHUNK), 1).astype(F32)
    lgs = [math.log1p(-(2.0 ** (-5.0 - h))) for h in range(H)]
    for h, lg in enumerate(lgs):
        rel = row - col
        dec_ref[h] = jnp.where(rel >= 0, jnp.exp(lg * jnp.maximum(rel, 0.0)), 0.0)
        dec_ref[H + h] = jnp.exp(lg * (row + 1.0))
        dec_ref[2 * H + h] = jnp.exp(lg * (CHUNK - 1.0 - row)) * (LANES ** -0.5)
    st_ref[...] = jnp.zeros(st_ref.shape, F32)

    def body(n, carry):
        r0 = pl.multiple_of(n * CHUNK, CHUNK)
        rows = pl.ds(r0, CHUNK)
        cos = cos_ref[rows, :]
        sin = sin_ref[rows, :]
        qbs, vs, ss, iqs, kvs = [], [], [], [], []
        for h in range(H):
            q = z_ref[rows, h * LANES:(h + 1) * LANES].astype(F32)
            k = z_ref[rows, HW + h * LANES:HW + (h + 1) * LANES].astype(F32)
            v = z_ref[rows, 2 * HW + h * LANES:2 * HW + (h + 1) * LANES]
            q = q * cos + pltpu.roll(q, LANES // 2, 1) * sin
            k = k * cos + pltpu.roll(k, LANES // 2, 1) * sin
            qb = q.astype(BF16)
            kb = (k * (LANES ** -0.5)).astype(BF16)
            kd = (k * dec_ref[2 * H + h]).astype(BF16)
            ss.append(_dot_nt(qb, kb))
            iqs.append(_dot(qb, st_ref[h].astype(BF16)))
            kvs.append(_dot_tn(kd, v))
            vs.append(v)
        ps = [(ss[h] * dec_ref[h]).astype(BF16) for h in range(H)]
        os_ = [_dot(ps[h], vs[h]) + iqs[h] * dec_ref[H + h] for h in range(H)]
        for h in range(H):
            st_ref[h] = st_ref[h] * math.exp(lgs[h] * CHUNK) + kvs[h]
            o = os_[h]
            ms = jnp.mean(o * o, axis=-1, keepdims=True)
            o = o * lax.rsqrt(ms + EPS) * nw_ref[:, h * LANES:(h + 1) * LANES]
            g = z_ref[rows, 3 * HW + h * LANES:3 * HW + (h + 1) * LANES].astype(F32)
            o_ref[rows, h * LANES:(h + 1) * LANES] = (o * _silu(g)).astype(o_ref.dtype)
        return carry

    lax.fori_loop(0, n_chunks, body, 0)


def _retention(zb, cos, sin, nw, B, S):
    T = B * S
    W = RET_HEADS * LANES
    return pl.pallas_call(
        _retention_kernel,
        grid=(B,),
        in_specs=[pl.BlockSpec((S, 4 * W), lambda b: (b, 0)),
                  pl.BlockSpec((S, LANES), lambda b: (0, 0)),
                  pl.BlockSpec((S, LANES), lambda b: (0, 0)),
                  pl.BlockSpec((1, W), lambda b: (0, 0))],
        out_specs=pl.BlockSpec((S, W), lambda b: (b, 0)),
        out_shape=jax.ShapeDtypeStruct((T, W), BF16),
        scratch_shapes=[pltpu.VMEM((3 * RET_HEADS, CHUNK, LANES), F32),
                        pltpu.VMEM((RET_HEADS, LANES, LANES), F32)],
        compiler_params=_cparams("parallel"),
        name="retention",
    )(zb, cos, sin, nw)


def _da_kernel(*refs):
    z_refs = refs[:9]
    wq_ref, wk_ref, o_ref, qn_ref, kn_ref, v_ref, np_ref, lp_ref, mp_ref = refs[9:18]
    nn_refs, ln_refs, mn_refs = refs[18:21], refs[21:24], refs[24:27]
    S = o_ref.shape[0]
    lane = lax.broadcasted_iota(jnp.int32, (1, LANES), 1)
    lo = lane < DA_DH
    row = lax.broadcasted_iota(jnp.int32, (CHUNK, CHUNK), 0)
    col = lax.broadcasted_iota(jnp.int32, (CHUNK, CHUNK), 1)
    mask_cur = col <= row
    mask_prev = col >= row

    seg = (lax.broadcasted_iota(jnp.int32, (LANES, LANES), 0) // DA_DH
           == lax.broadcasted_iota(jnp.int32, (LANES, LANES), 1) // DA_DH).astype(BF16) * (1.0 / DA_DH)

    def head_norm(x, w):
        x2 = x * x
        hi = x2.astype(BF16)
        lo_part = (x2 - hi.astype(F32)).astype(BF16)
        ms = _dot(hi, seg) + _dot(lo_part, seg)
        return x * lax.rsqrt(ms + EPS) * w

    RB = 256
    for g, (dil, nb) in enumerate(DA_GROUPS):
        L = S // dil
        zq_ref, zk_ref, zv_ref = z_refs[3 * g:3 * g + 3]
        for rho in range(dil):
            for piece in range(max(L // RB, 1)):
                n_rows = min(RB, L)
                src = pl.ds(rho + dil * piece * n_rows, n_rows, stride=dil) if dil > 1 else pl.ds(
                    piece * n_rows, n_rows)
                dst = pl.ds(rho * L + piece * n_rows, n_rows)
                q = zq_ref[src, :]
                k = zk_ref[src, :]
                v = zv_ref[src, :]
                qn_ref[dst, :] = (head_norm(q, wq_ref[...]) * (DA_DH ** -0.5)).astype(BF16)
                kn_ref[dst, :] = head_norm(k, wk_ref[...]).astype(BF16)
                v_ref[dst, :] = v.astype(BF16)

        def batch(r0, chained, first_has_prev):
            def kv(rows):
                k = kn_ref[rows, :]
                v = v_ref[rows, :]
                one = jnp.ones_like(v)
                return k, (jnp.where(lo, v, one), jnp.where(lo, one, v))

            prev = kv(pl.ds(r0 - CHUNK, CHUNK)) if first_has_prev else None
            chains = []
            for u in range(DA_UNROLL):
                rows = pl.ds(r0 + u * CHUNK, CHUNK)
                cur = kv(rows)
                p = prev if (chained and (u > 0 or first_has_prev)) else None
                q = qn_ref[rows, :]
                zero = jnp.zeros_like(q)
                for hs, qh in enumerate((jnp.where(lo, q, zero), jnp.where(lo, zero, q))):
                    sc = _dot_nt(qh, cur[0])
                    sp = None if p is None else _dot_nt(qh, p[0])
                    chains.append((sc, sp, cur[1][hs], None if p is None else p[1][hs]))
                prev = cur
            probs = []
            for sc, sp, _, _ in chains:
                sc = jnp.where(mask_cur, sc, NEG)
                m = jnp.max(sc, axis=-1, keepdims=True)
                if sp is not None:
                    sp = jnp.where(mask_prev, sp, NEG)
                    m = jnp.maximum(m, jnp.max(sp, axis=-1, keepdims=True))
                    sp = jnp.exp(sp - m).astype(BF16)
                probs.append((jnp.exp(sc - m).astype(BF16), sp, m))
            accs = []
            for (pc, pp, _), (_, _, vc, vp) in zip(probs, chains):
                acc = _dot(pc, vc)
                if pp is not None:
                    acc = acc + _dot(pp, vp)
                accs.append(acc)
            nums, dens, maxs = [], [], []
            for u in range(DA_UNROLL):
                a0, a1 = accs[2 * u], accs[2 * u + 1]
                nums.append(jnp.where(lo, a0, a1))
                dens.append(jnp.where(lo, pltpu.roll(a0, DA_DH, 1), pltpu.roll(a1, DA_DH, 1)))
                maxs.append(jnp.where(lo, probs[2 * u][2], probs[2 * u + 1][2]))
            rows = pl.ds(r0, DA_UNROLL * CHUNK)
            np_ref[rows, :] = jnp.concatenate(nums, axis=0)
            lp_ref[rows, :] = jnp.concatenate(dens, axis=0)
            mp_ref[rows, :] = jnp.concatenate(maxs, axis=0)

        span = DA_UNROLL * CHUNK
        if nb == 1:
            def singles(i, c):
                batch(pl.multiple_of(i * span, span), False, False)
                return c
            lax.fori_loop(0, S // span, singles, 0)
        else:
            def segment(sgi, c, nb=nb):
                base = pl.multiple_of(sgi * (nb * CHUNK), span)
                batch(base, True, False)
                if nb > DA_UNROLL:
                    def inner(n, c2):
                        batch(pl.multiple_of(base + n * span, span), True, True)
                        return c2
                    lax.fori_loop(1, nb // DA_UNROLL, inner, 0)
                return c
            lax.fori_loop(0, dil, segment, 0)

        for rho in range(dil):
            for piece in range(max(L // RB, 1)):
                n_rows = min(RB, L)
                dst = pl.ds(rho + dil * piece * n_rows, n_rows, stride=dil) if dil > 1 else pl.ds(
                    piece * n_rows, n_rows)
                src = pl.ds(rho * L + piece * n_rows, n_rows)
                nn_refs[g][dst, :] = np_ref[src, :]
                ln_refs[g][dst, :] = lp_ref[src, :]
                mn_refs[g][dst, :] = mp_ref[src, :]

    for piece in range(S // RB):
        rows = pl.ds(piece * RB, RB)
        m = jnp.maximum(jnp.maximum(mn_refs[0][rows, :], mn_refs[1][rows, :]), mn_refs[2][rows, :])
        num = jnp.zeros((RB, LANES), F32)
        den = jnp.zeros((RB, LANES), F32)
        for g in range(3):
            e = jnp.exp(mn_refs[g][rows, :] - m)
            num = num + e * nn_refs[g][rows, :]
            den = den + e * ln_refs[g][rows, :]
        o_ref[rows, :] = (num / den).astype(o_ref.dtype)


def _dilated_attention(zf, wq, wk, B, S):
    T = B * S
    slab = lambda k: pl.BlockSpec((S, LANES), lambda b, p: (b, 9 * p + k))
    return pl.pallas_call(
        _da_kernel,
        grid=(B, 2),
        in_specs=[slab(k) for k in range(9)] + [
                  pl.BlockSpec((1, LANES), lambda b, p: (0, 0)),
                  pl.BlockSpec((1, LANES), lambda b, p: (0, 0))],
        out_specs=pl.BlockSpec((S, LANES), lambda b, p: (b, p)),
        out_shape=jax.ShapeDtypeStruct((T, 2 * LANES), BF16),
        scratch_shapes=[pltpu.VMEM((S, LANES), BF16), pltpu.VMEM((S, LANES), BF16),
                        pltpu.VMEM((S, LANES), BF16),
                        pltpu.VMEM((S, LANES), F32), pltpu.VMEM((S, LANES), F32),
                        pltpu.VMEM((S, LANES), F32)] + [pltpu.VMEM((S, LANES), F32)] * 9,
        compiler_params=_cparams("parallel", "arbitrary"),
        name="dilated_attention",
    )(*([zf] * 9), wq, wk)


def _mlstm_kernel(qk_ref, v_ref, og_ref, gate_ref, cw_ref, gb_ref, nw_ref, o_ref,
                  xp_ref, qc_ref, st_ref, e_ref, et_ref):
    S = qk_ref.shape[0]
    n_chunks = S // CHUNK
    W = qk_ref.shape[1]
    PAD = 8
    xp_ref[0:PAD, :] = jnp.zeros((PAD, W), F32)
    RB = 256
    for piece in range(S // RB):
        xp_ref[PAD + piece * RB:PAD + (piece + 1) * RB, :] = qk_ref[piece * RB:(piece + 1) * RB, :].astype(F32)
    for piece in range(S // RB):
        for cs in range(W // LANES):
            cols = slice(cs * LANES, (cs + 1) * LANES)
            acc = cw_ref[0:1, cols] * xp_ref[pl.ds(PAD + piece * RB - 3, RB), cols]
            for i in range(1, 4):
                acc = acc + cw_ref[i:i + 1, cols] * xp_ref[pl.ds(PAD + piece * RB - 3 + i, RB), cols]
            scale = ML_DK ** -0.5 if cs >= W // LANES // 2 else 1.0
            qc_ref[piece * RB:(piece + 1) * RB, cols] = (_silu(acc) * scale).astype(BF16)

    lane = lax.broadcasted_iota(jnp.int32, (1, LANES), 1)
    lo = lane < ML_DK
    row = lax.broadcasted_iota(jnp.int32, (CHUNK, CHUNK), 0)
    col = lax.broadcasted_iota(jnp.int32, (CHUNK, CHUNK), 1)
    causal = col <= row
    tril = causal.astype(F32)
    srow = lax.broadcasted_iota(jnp.int32, (LANES, 1), 0) < ML_DK
    st_ref[...] = jnp.zeros(st_ref.shape, F32)

    is_f = (lane >= ML_HEADS) & (lane < 2 * ML_HEADS)
    for n in range(n_chunks):
        rows = pl.ds(n * CHUNK, CHUNK)
        gp = gate_ref[rows, :] + gb_ref[...]
        logf = jnp.minimum(gp, 0.0) - jnp.log(1.0 + jnp.exp(-jnp.abs(gp)))
        cum = jnp.dot(tril, jnp.where(is_f, logf, 0.0), preferred_element_type=F32,
                      precision=lax.Precision.HIGHEST)
        e = jnp.where(lane < ML_HEADS, gp, cum)
        e_ref[rows, :] = e
        et_ref[n] = e.T[0:2 * ML_HEADS, :]

    def chunk(n, mms):
        r0 = pl.multiple_of(n * CHUNK, CHUNK)
        rows = pl.ds(r0, CHUNK)
        e = e_ref[rows, :]
        et = et_ref[n]
        qhs, khs, vaugs, s_raws, iqs = [], [], [], [], []
        for pair in range(2):
            q2 = qc_ref[rows, pair * LANES:(pair + 1) * LANES]
            k2 = qc_ref[rows, W // 2 + pair * LANES:W // 2 + (pair + 1) * LANES]
            zero = jnp.zeros_like(q2)
            stb = st_ref[pair].astype(BF16)
            for hs in range(2):
                h = 2 * pair + hs
                qh = jnp.where(lo, q2, zero) if hs == 0 else jnp.where(lo, zero, q2)
                kh = jnp.where(lo, k2, zero) if hs == 0 else jnp.where(lo, zero, k2)
                vh = v_ref[rows, h * LANES:(h + 1) * LANES]
                s_raws.append(_dot_nt(qh, k2))
                iqs.append(_dot(qh, stb))
                khs.append(kh)
                vaugs.append(jnp.concatenate([vh, jnp.ones_like(vh)], axis=-1))
        ps, wis, ms_, kws, wcs, new_mms = [], [], [], [], [], []
        for h in range(ML_HEADS):
            mm = mms[h]
            i_b = jnp.broadcast_to(e[:, h:h + 1], (CHUNK, LANES))
            a_b = jnp.broadcast_to(e[:, ML_HEADS + h:ML_HEADS + h + 1], (CHUNK, LANES))
            c_row = et[h:h + 1, :] - et[ML_HEADS + h:ML_HEADS + h + 1, :]
            dmat = jnp.where(causal, a_b + c_row, NEG)
            inter = a_b + mm
            m = jnp.maximum(inter, jnp.max(dmat, axis=-1, keepdims=True))
            ps.append((s_raws[h] * jnp.exp(dmat - m)).astype(BF16))
            wis.append(jnp.exp(inter - m))
            ms_.append(m)
            m_new = m[CHUNK - 1:CHUNK, :]
            a_last = a_b[CHUNK - 1:CHUNK, :]
            wk = jnp.exp(a_last - a_b + i_b - m_new)
            kws.append((khs[h].astype(F32) * wk).astype(BF16))
            wc = jnp.exp(a_last + mm - m_new)
            wcs.append(jnp.concatenate([wc, wc], axis=1))
            new_mms.append(m_new)
        accs = [_dot(ps[h], vaugs[h]) + jnp.concatenate([wis[h], wis[h]], axis=1) * iqs[h]
                for h in range(ML_HEADS)]
        upds = [_dot_tn(kws[h], vaugs[h]) for h in range(ML_HEADS)]
        for pair in range(2):
            h0, h1 = 2 * pair, 2 * pair + 1
            st_ref[pair] = jnp.where(srow, wcs[h0], wcs[h1]) * st_ref[pair] + (upds[h0] + upds[h1])
        for h in range(ML_HEADS):
            num = accs[h][:, :LANES]
            den = accs[h][:, LANES:]
            hv = num / jnp.maximum(jnp.abs(den), jnp.exp(-ms_[h]))
            var = jnp.mean(hv * hv, axis=-1, keepdims=True)
            hv = hv * lax.rsqrt(var + EPS) * nw_ref[:, h * LANES:(h + 1) * LANES]
            og = og_ref[rows, h * LANES:(h + 1) * LANES].astype(F32)
            o_ref[rows, h * LANES:(h + 1) * LANES] = (hv * _sigmoid(og)).astype(o_ref.dtype)
        return tuple(new_mms)

    lax.fori_loop(0, n_chunks, chunk, tuple(jnp.zeros((1, LANES), F32) for _ in range(ML_HEADS)))


def _mlstm(zb, zf, conv_w, gate_bias, nw, B, S, qk_blk, gate_blk):
    T = B * S
    W = ML_HEADS * LANES
    return pl.pallas_call(
        _mlstm_kernel,
        grid=(B,),
        in_specs=[pl.BlockSpec((S, W), lambda b: (b, qk_blk)),
                  pl.BlockSpec((S, W), lambda b: (b, qk_blk + 1)),
                  pl.BlockSpec((S, W), lambda b: (b, qk_blk + 2)),
                  pl.BlockSpec((S, LANES), lambda b: (b, gate_blk)),
                  pl.BlockSpec((4, W), lambda b: (0, 0)),
                  pl.BlockSpec((1, LANES), lambda b: (0, 0)),
                  pl.BlockSpec((1, W), lambda b: (0, 0))],
        out_specs=pl.BlockSpec((S, W), lambda b: (b, 0)),
        out_shape=jax.ShapeDtypeStruct((T, W), BF16),
        scratch_shapes=[pltpu.VMEM((S + 8, W), F32), pltpu.VMEM((S, W), BF16),
                        pltpu.VMEM((2, LANES, 2 * LANES), F32),
                        pltpu.VMEM((S, LANES), F32),
                        pltpu.VMEM((S // CHUNK, 2 * ML_HEADS, LANES), F32)],
        compiler_params=_cparams("parallel"),
        name="mlstm",
    )(zb, zb, zb, zf, conv_w, gate_bias, nw)


def _merge_kernel(x_ref, oret_ref, oda_ref, oml_ref, gr_ref, gd_ref, gm_ref,
                  wr_ref, wd_ref, wm_ref, wo_ref, o_ref):
    y = _sigmoid(gr_ref[...].astype(F32)) * _dot(oret_ref[...], wr_ref[...])
    y = y + _sigmoid(gd_ref[...].astype(F32)) * _dot(oda_ref[...], wd_ref[...])
    y = y + _sigmoid(gm_ref[...].astype(F32)) * _dot(oml_ref[...], wm_ref[...])
    o_ref[...] = x_ref[...] + _dot(y.astype(BF16), wo_ref[...])


def _merge(x, o_ret, o_da, o_ml, zb, g_blk, wr, wd, wm, wo, tm):
    T, D = x.shape
    full = lambda a: pl.BlockSpec(a.shape, lambda i: (0, 0))
    return pl.pallas_call(
        _merge_kernel,
        grid=(T // tm,),
        in_specs=[pl.BlockSpec((tm, D), lambda i: (i, 0)),
                  pl.BlockSpec((tm, o_ret.shape[1]), lambda i: (i, 0)),
                  pl.BlockSpec((tm, o_da.shape[1]), lambda i: (i, 0)),
                  pl.BlockSpec((tm, o_ml.shape[1]), lambda i: (i, 0)),
                  pl.BlockSpec((tm, D), lambda i: (i, g_blk)),
                  pl.BlockSpec((tm, D), lambda i: (i, g_blk + 1)),
                  pl.BlockSpec((tm, D), lambda i: (i, g_blk + 2)),
                  full(wr), full(wd), full(wm), full(wo)],
        out_specs=pl.BlockSpec((tm, D), lambda i: (i, 0)),
        out_shape=jax.ShapeDtypeStruct((T, D), F32),
        compiler_params=_cparams("parallel"),
        name="merge_outproj",
    )(x, o_ret, o_da, o_ml, zb, zb, zb, wr, wd, wm, wo)


def _ffn_kernel(x_ref, nw_ref, wg_ref, wu_ref, wd_ref, o_ref, h_ref, acc_ref):
    j = pl.program_id(1)

    @pl.when(j == 0)
    def _():
        x = x_ref[...]
        ms = jnp.mean(x * x, axis=-1, keepdims=True)
        h_ref[...] = (x * lax.rsqrt(ms + EPS) * nw_ref[...]).astype(BF16)
        acc_ref[...] = x

    h = h_ref[...]
    a = _silu(_dot(h, wg_ref[...])) * _dot(h, wu_ref[...])
    acc_ref[...] += _dot(a.astype(BF16), wd_ref[...])

    @pl.when(j == pl.num_programs(1) - 1)
    def _():
        o_ref[...] = acc_ref[...]


def _ffn(x, nw, wg, wu, wd, tm, tf):
    T, D = x.shape
    F = wg.shape[1]
    return pl.pallas_call(
        _ffn_kernel,
        grid=(T // tm, F // tf),
        in_specs=[pl.BlockSpec((tm, D), lambda i, j: (i, 0)),
                  pl.BlockSpec((1, D), lambda i, j: (0, 0)),
                  pl.BlockSpec((D, tf), lambda i, j: (0, j)),
                  pl.BlockSpec((D, tf), lambda i, j: (0, j)),
                  pl.BlockSpec((tf, D), lambda i, j: (j, 0))],
        out_specs=pl.BlockSpec((tm, D), lambda i, j: (i, 0)),
        out_shape=jax.ShapeDtypeStruct((T, D), F32),
        scratch_shapes=[pltpu.VMEM((tm, D), BF16), pltpu.VMEM((tm, D), F32)],
        compiler_params=_cparams("parallel", "arbitrary"),
        name="swiglu_ffn",
    )(x, nw, wg, wu, wd)


HI16 = 0xFFFF0000


def _pack_bf16_pairs(a, b):
    ab = pltpu.bitcast(a.astype(BF16).astype(F32), jnp.uint32)
    bb = pltpu.bitcast(b.astype(BF16).astype(F32), jnp.uint32)
    return (ab >> 16) | (bb & jnp.uint32(HI16))


def _unpack_bf16_pairs(p):
    lo = pltpu.bitcast(p << 16, F32)
    hi = pltpu.bitcast(p & jnp.uint32(HI16), F32)
    return lo, hi


def _router_kernel(x_ref, nw_ref, wr_ref, br_ref, ha_ref, hb_ref, sel_ref):
    x = x_ref[...]
    ms = jnp.mean(x * x, axis=-1, keepdims=True)
    h = x * lax.rsqrt(ms + EPS) * nw_ref[...]
    Q = h.shape[1] // 4
    ha_ref[...] = _pack_bf16_pairs(h[:, 0:Q], h[:, Q:2 * Q])
    hb_ref[...] = _pack_bf16_pairs(h[:, 2 * Q:3 * Q], h[:, 3 * Q:4 * Q])
    logits = jnp.dot(h, wr_ref[...], preferred_element_type=F32,
                     precision=lax.Precision.HIGHEST) + br_ref[...]
    lane = lax.broadcasted_iota(jnp.int32, logits.shape, 1).astype(F32)
    logits = jnp.where(lane < N_EXPERTS, logits, NEG)
    m1 = jnp.max(logits, axis=-1, keepdims=True)
    i1 = jnp.min(jnp.where(logits == m1, lane, float(LANES)), axis=-1, keepdims=True)
    rest = jnp.where(lane == i1, NEG, logits)
    m2 = jnp.max(rest, axis=-1, keepdims=True)
    i2 = jnp.min(jnp.where(rest == m2, lane, float(LANES)), axis=-1, keepdims=True)
    e2 = jnp.exp(m2 - m1)
    p1 = 1.0 / (1.0 + e2)
    p2 = e2 / (1.0 + e2)
    sel_ref[...] = jnp.where(lane == 0.0, i1, jnp.where(lane == 1.0, i2,
                             jnp.where(lane == 2.0, p1, jnp.where(lane == 3.0, p2, 0.0))))


def _router(x, nw, wr, br, tm):
    T, D = x.shape
    Q = D // 4
    return pl.pallas_call(
        _router_kernel,
        grid=(T // tm,),
        in_specs=[pl.BlockSpec((tm, D), lambda i: (i, 0)),
                  pl.BlockSpec((1, D), lambda i: (0, 0)),
                  pl.BlockSpec((D, LANES), lambda i: (0, 0)),
                  pl.BlockSpec((1, LANES), lambda i: (0, 0))],
        out_specs=[pl.BlockSpec((tm, Q), lambda i: (i, 0)),
                   pl.BlockSpec((tm, Q), lambda i: (i, 0)),
                   pl.BlockSpec((tm, LANES), lambda i: (i, 0))],
        out_shape=[jax.ShapeDtypeStruct((T, Q), jnp.uint32), jax.ShapeDtypeStruct((T, Q), jnp.uint32),
                   jax.ShapeDtypeStruct((T, LANES), F32)],
        compiler_params=_cparams("parallel"),
        name="moe_router",
    )(x, nw, wr, br)


def _moe_rank_kernel(sel_ref, pos_ref, meta_ref, cnt_ref, offs_ref, carry_ref):
    ph = pl.program_id(0)
    i = pl.program_id(1)
    tm = sel_ref.shape[0]
    lane = lax.broadcasted_iota(jnp.int32, (tm, LANES), 1).astype(F32)
    lane1 = lax.broadcasted_iota(jnp.int32, (1, LANES), 1).astype(F32)
    sel = sel_ref[...]
    i1 = sel[:, 0:1]
    i2 = sel[:, 1:2]
    onehot = jnp.where((lane == i1) | (lane == i2), 1.0, 0.0)
    colsum = jnp.sum(onehot, axis=0, keepdims=True)

    @pl.when((ph == 0) & (i == 0))
    def _():
        cnt_ref[...] = jnp.zeros(cnt_ref.shape, F32)

    @pl.when(ph == 0)
    def _():
        cnt_ref[...] += colsum

    def padded_counts():
        return jnp.floor((cnt_ref[...] + (MOE_TILE - 1.0)) * (1.0 / MOE_TILE)) * MOE_TILE

    @pl.when((ph == 1) & (i == 0))
    def _():
        k = lax.broadcasted_iota(jnp.int32, (LANES, LANES), 0)
        e = lax.broadcasted_iota(jnp.int32, (LANES, LANES), 1)
        upper = (k < e).astype(F32)
        offs_ref[...] = jnp.dot(padded_counts(), upper, preferred_element_type=F32,
                                precision=lax.Precision.HIGHEST)
        carry_ref[...] = jnp.zeros(carry_ref.shape, F32)

    @pl.when(ph == 1)
    def _():
        r = lax.broadcasted_iota(jnp.int32, (tm, tm), 0)
        c = lax.broadcasted_iota(jnp.int32, (tm, tm), 1)
        before = (c < r).astype(BF16)
        rank = _dot(before, onehot.astype(BF16)) + carry_ref[0:1, :]
        row = rank + offs_ref[0:1, :]
        pos1 = jnp.sum(jnp.where(lane == i1, row, 0.0), axis=-1, keepdims=True)
        pos2 = jnp.sum(jnp.where(lane == i2, row, 0.0), axis=-1, keepdims=True)
        both = jnp.where(lane == 0.0, pos1, jnp.where(lane == 1.0, pos2, 0.0))
        pos_ref[...] = both.T[0:8, :].astype(jnp.int32)
        carry_ref[...] += colsum

    @pl.when((ph == 1) & (i == pl.num_programs(1) - 1))
    def _():
        nrow = meta_ref.shape[0]
        padded = padded_counts()[0:1, :]
        offs = offs_ref[0:1, :]
        ends = offs + padded
        start = lax.broadcasted_iota(jnp.int32, (nrow, 1), 0).astype(F32) * MOE_TILE
        is_e = lane1 < N_EXPERTS
        te = jnp.sum(jnp.where(is_e & (ends <= start), 1.0, 0.0), axis=-1, keepdims=True)
        te = jnp.minimum(te, N_EXPERTS - 1.0)
        valid_end = jnp.sum(jnp.where(lane1 == te, offs + cnt_ref[0:1, :], 0.0), axis=-1, keepdims=True)
        nv = jnp.clip(valid_end - start, 0.0, MOE_TILE)
        total = jnp.sum(jnp.where(is_e, padded, 0.0), axis=-1, keepdims=True)
        active = jnp.where(start < total, 1.0, 0.0)
        meta_ref[...] = jnp.where(lane1 == 0.0, te, jnp.where(lane1 == 1.0, nv,
                                  jnp.where(lane1 == 2.0, active, 0.0)))


def _moe_rank(sel, tm, meta_rows):
    T = sel.shape[0]
    nt = T // tm
    return pl.pallas_call(
        _moe_rank_kernel,
        grid=(2, nt),
        in_specs=[pl.BlockSpec((tm, LANES), lambda ph, i: (i, 0))],
        out_specs=[pl.BlockSpec((None, 8, tm), lambda ph, i: (i * ph, 0, 0)),
                   pl.BlockSpec((meta_rows, LANES), lambda ph, i: (0, 0))],
        out_shape=[jax.ShapeDtypeStruct((nt, 8, tm), jnp.int32),
                   jax.ShapeDtypeStruct((meta_rows, LANES), F32)],
        scratch_shapes=[pltpu.VMEM((8, LANES), F32), pltpu.VMEM((8, LANES), F32),
                        pltpu.VMEM((8, LANES), F32)],
        compiler_params=_cparams("arbitrary", "arbitrary"),
        name="moe_rank",
    )(sel)


def _sc_mesh():
    return plsc.VectorSubcoreMesh(core_axis_name="core", subcore_axis_name="subcore")


def _sc_scatter_rows(x, idx, n_rows):
    n_idx = idx.shape[1]
    nt = x.shape[0] // SC_WINDOW
    width = x.shape[1]

    @functools.partial(pl.kernel, out_type=jax.ShapeDtypeStruct((n_rows, width), x.dtype),
                       mesh=_sc_mesh())
    def scatter_kernel(x_hbm, i_hbm, o_hbm):
        def body(x_vmem, i_vmem):
            pltpu.sync_copy(x_vmem, o_hbm.at[i_vmem.at[0]])

        pltpu.emit_pipeline(
            body, grid=(n_idx // SC_WINDOW,),
            in_specs=[pl.BlockSpec((SC_WINDOW, width), lambda i: (i % nt, 0)),
                      pl.BlockSpec((1, SC_WINDOW), lambda i: (0, i))],
            out_specs=[],
            core_axis_name=("core", "subcore"), dimension_semantics=(pltpu.PARALLEL,),
        )(x_hbm, i_hbm)

    return scatter_kernel(x, idx)


def _sc_gather_rows(x, idx):
    n_idx = idx.shape[1]
    width = x.shape[1]

    @functools.partial(pl.kernel, out_type=jax.ShapeDtypeStruct((n_idx, width), x.dtype),
                       mesh=_sc_mesh())
    def gather_kernel(x_hbm, i_hbm, o_hbm):
        def body(i_vmem, o_vmem):
            pltpu.sync_copy(x_hbm.at[i_vmem.at[0]], o_vmem)

        pltpu.emit_pipeline(
            body, grid=(n_idx // SC_WINDOW,),
            in_specs=[pl.BlockSpec((1, SC_WINDOW), lambda i: (0, i))],
            out_specs=[pl.BlockSpec((SC_WINDOW, width), lambda i: (i, 0))],
            core_axis_name=("core", "subcore"), dimension_semantics=(pltpu.PARALLEL,),
        )(i_hbm, o_hbm)

    return gather_kernel(x, idx)


def _moe_group_kernel(te_ref, nv_ref, na_ref, xa_ref, xb_ref, wg_ref, wu_ref, wd_ref,
                      ya_ref, yb_ref, h_ref, acc_ref):
    j = pl.program_id(0)
    f = pl.program_id(1)

    @pl.when(j < na_ref[0])
    def _():
        @pl.when(f == 0)
        def _():
            tm = h_ref.shape[0]
            Q = xa_ref.shape[1]
            valid = lax.broadcasted_iota(jnp.int32, (tm, 1), 0) < nv_ref[j]
            for src, c0 in ((xa_ref, 0), (xb_ref, 2 * Q)):
                lo, hi = _unpack_bf16_pairs(src[...])
                h_ref[:, c0:c0 + Q] = jnp.where(valid, lo, 0.0).astype(BF16)
                h_ref[:, c0 + Q:c0 + 2 * Q] = jnp.where(valid, hi, 0.0).astype(BF16)
            acc_ref[...] = jnp.zeros(acc_ref.shape, F32)

        h = h_ref[...]
        a = _silu(_dot(h, wg_ref[...])) * _dot(h, wu_ref[...])
        acc_ref[...] += _dot(a.astype(BF16), wd_ref[...])

        @pl.when(f == pl.num_programs(1) - 1)
        def _():
            Q = ya_ref.shape[1]
            ya_ref[...] = _pack_bf16_pairs(acc_ref[:, 0:Q], acc_ref[:, Q:2 * Q])
            yb_ref[...] = _pack_bf16_pairs(acc_ref[:, 2 * Q:3 * Q], acc_ref[:, 3 * Q:4 * Q])


def _moe_group(te, nv, na, xa, xb, wg, wu, wd, tf):
    R, Q = xa.shape
    E, D, F = wg.shape
    nf = F // tf
    tile = lambda j, f, te, nv, na: (jnp.minimum(j, na[0] - 1), 0)
    ff = lambda j, f, na: jnp.where(j < na[0], f, nf - 1)
    grid_spec = pltpu.PrefetchScalarGridSpec(
        num_scalar_prefetch=3,
        grid=(R // MOE_TILE, nf),
        in_specs=[pl.BlockSpec((MOE_TILE, Q), tile),
                  pl.BlockSpec((MOE_TILE, Q), tile),
                  pl.BlockSpec((None, D, tf), lambda j, f, te, nv, na: (te[j], 0, ff(j, f, na))),
                  pl.BlockSpec((None, D, tf), lambda j, f, te, nv, na: (te[j], 0, ff(j, f, na))),
                  pl.BlockSpec((None, tf, D), lambda j, f, te, nv, na: (te[j], ff(j, f, na), 0))],
        out_specs=[pl.BlockSpec((MOE_TILE, Q), tile), pl.BlockSpec((MOE_TILE, Q), tile)],
        scratch_shapes=[pltpu.VMEM((MOE_TILE, D), BF16), pltpu.VMEM((MOE_TILE, D), F32)],
    )
    return pl.pallas_call(
        _moe_group_kernel,
        grid_spec=grid_spec,
        out_shape=[jax.ShapeDtypeStruct((R, Q), jnp.uint32), jax.ShapeDtypeStruct((R, Q), jnp.uint32)],
        compiler_params=_cparams("arbitrary", "arbitrary"),
        name="moe_experts",
    )(te, nv, na, xa, xb, wg, wu, wd)


def _moe_combine_kernel(x_ref, sel_ref, a1_ref, a2_ref, b1_ref, b2_ref, o_ref):
    sel = sel_ref[...]
    p1 = sel[:, 2:3]
    p2 = sel[:, 3:4]
    Q = a1_ref.shape[1]
    for first, second, c0 in ((a1_ref, a2_ref, 0), (b1_ref, b2_ref, 2 * Q)):
        lo1, hi1 = _unpack_bf16_pairs(first[...])
        lo2, hi2 = _unpack_bf16_pairs(second[...])
        o_ref[:, c0:c0 + Q] = x_ref[:, c0:c0 + Q] + (p1 * lo1 + p2 * lo2)
        o_ref[:, c0 + Q:c0 + 2 * Q] = x_ref[:, c0 + Q:c0 + 2 * Q] + (p1 * hi1 + p2 * hi2)


def _moe_combine(x, sel, ga, gb, tm):
    T, D = x.shape
    Q = ga.shape[1]
    nt = T // tm
    return pl.pallas_call(
        _moe_combine_kernel,
        grid=(nt,),
        in_specs=[pl.BlockSpec((tm, D), lambda i: (i, 0)),
                  pl.BlockSpec((tm, LANES), lambda i: (i, 0)),
                  pl.BlockSpec((tm, Q), lambda i: (i, 0)),
                  pl.BlockSpec((tm, Q), lambda i: (i + nt, 0)),
                  pl.BlockSpec((tm, Q), lambda i: (i, 0)),
                  pl.BlockSpec((tm, Q), lambda i: (i + nt, 0))],
        out_specs=pl.BlockSpec((tm, D), lambda i: (i, 0)),
        out_shape=jax.ShapeDtypeStruct((T, D), F32),
        compiler_params=_cparams("parallel"),
        name="moe_combine",
    )(x, sel, ga, ga, gb, gb)


def _moe(x, nw, w_router, b_router, wg, wu, wd):
    T, D = x.shape
    wr = jnp.pad(w_router, ((0, 0), (0, LANES - N_EXPERTS)))
    br = jnp.pad(b_router, (0, LANES - N_EXPERTS)).reshape(1, LANES)
    ha, hb, sel = _router(x, nw, wr, br, 1024)
    n_rows = TOP_K * T + N_EXPERTS * MOE_TILE
    n_tiles = n_rows // MOE_TILE
    pos, meta = _moe_rank(sel, 512, 256)
    idx = jnp.concatenate([pos[:, 0, :].reshape(1, T), pos[:, 1, :].reshape(1, T)], axis=1)
    te = meta[:n_tiles, 0].astype(jnp.int32)
    nv = meta[:n_tiles, 1].astype(jnp.int32)
    na = jnp.sum(meta[:n_tiles, 2]).astype(jnp.int32).reshape(1)
    te = jnp.where(jnp.arange(n_tiles) < na[0], te, te[na[0] - 1])
    xa = _sc_scatter_rows(ha, idx, n_rows)
    xb = _sc_scatter_rows(hb, idx, n_rows)
    ya, yb = _moe_group(te, nv, na, xa, xb, wg, wu, wd, 1792)
    ga = _sc_gather_rows(ya, idx)
    gb = _sc_gather_rows(yb, idx)
    return _moe_combine(x, sel, ga, gb, 1024)


def _split_w_in(w_in):
    sizes = (512, 512, 512, 512, 768, 768, 768, 256, 256, 512, 512, 4, 4, 1024, 1024, 1024)
    offs = [0]
    for s in sizes:
        offs.append(offs[-1] + s)
    part = lambda i: w_in[:, offs[i]:offs[i + 1]]
    rq, rk, rv, rg, dq, dk, dv, mq, mk, mv, mo, mi, mf, g_ret, g_da, g_ml = (part(i) for i in range(16))
    wb = jnp.concatenate([rq, rk, rv, rg, g_ret, g_da, g_ml, mq, mk, mv, mo], axis=1).astype(BF16)
    cols = []
    for p in range(2):
        for g in range(3):
            for t in (dq, dk, dv):
                cols.append(t[:, g * 256 + p * 128:g * 256 + (p + 1) * 128])
    pad = jnp.zeros((w_in.shape[0], 2 * LANES - 8), w_in.dtype)
    wf = jnp.concatenate(cols + [mi, mf, pad], axis=1).astype(BF16)
    return wb, wf


def _rope_tables(S):
    half = LANES // 2
    inv = jnp.power(ROPE_BASE, -jnp.arange(half, dtype=F32) / half)
    ang = jnp.arange(S, dtype=F32)[:, None] * inv[None, :]
    cos = jnp.cos(ang)
    sin = jnp.sin(ang)
    return jnp.concatenate([cos, cos], axis=1), jnp.concatenate([-sin, sin], axis=1)


def kernel(x, norm1_w, w_in, ret_norm_w, da_q_norm_w, da_k_norm_w, ml_conv_w, ml_i_bias, ml_f_bias,
           ml_norm_w, w_br_ret, w_br_da, w_br_ml, w_out, norm2_w, ffn_w_gate, ffn_w_up, ffn_w_down,
           moe_w_router, moe_b_router, moe_w_gate, moe_w_up, moe_w_down):
    B, S, D = x.shape
    T = B * S
    depth = w_in.shape[0]
    cos, sin = _rope_tables(S)
    xt = x.reshape(T, D)
    for layer in range(depth):
        wb, wf = _split_w_in(w_in[layer])
        nw1 = norm1_w[layer].reshape(1, D)
        zb = _norm_matmul(xt, nw1, wb, BF16, 1024, 1664, "inproj_bf16")
        zf = _norm_matmul(xt, nw1, wf, F32, 1024, 1280, "inproj_f32")
        o_ret = _retention(zb, cos, sin, ret_norm_w[layer].reshape(1, -1), B, S)
        wq = jnp.tile(da_q_norm_w[layer], 2).reshape(1, LANES)
        wk = jnp.tile(da_k_norm_w[layer], 2).reshape(1, LANES)
        o_da = _dilated_attention(zf, wq, wk, B, S)
        gate_bias = jnp.concatenate([ml_i_bias[layer], ml_f_bias[layer],
                                     jnp.zeros((LANES - 2 * ML_HEADS,), F32)]).reshape(1, LANES)
        o_ml = _mlstm(zb, zf, ml_conv_w[layer], gate_bias, ml_norm_w[layer].reshape(1, -1), B, S,
                      qk_blk=10, gate_blk=18)
        xt = _merge(xt, o_ret, o_da, o_ml, zb, 2,
                    w_br_ret[layer].astype(BF16), w_br_da[layer].astype(BF16),
                    w_br_ml[layer].astype(BF16), w_out[layer].astype(BF16), 512)
        nw2 = norm2_w[layer].reshape(1, D)
        j = layer // 2
        if layer % 2 == 0:
            xt = _ffn(xt, nw2, ffn_w_gate[j].astype(BF16), ffn_w_up[j].astype(BF16),
                      ffn_w_down[j].astype(BF16), 512, 1408)
        else:
            xt = _moe(xt, nw2, moe_w_router[j], moe_b_router[j], moe_w_gate[j].astype(BF16),
                      moe_w_up[j].astype(BF16), moe_w_down[j].astype(BF16))
    return xt.reshape(B, S, D)
```

```python
import functools
import math

import jax
import jax.numpy as jnp
from jax import lax
from jax.experimental import pallas as pl
from jax.experimental.pallas import tpu as pltpu
from jax.experimental.pallas import tpu_sc as plsc

F32 = jnp.float32
BF16 = jnp.bfloat16

EPS = 1e-6
D_MODEL = 1024
CHUNK = 128
LANES = 128
MXU_N = 256
ROPE_BASE = 10000.0
RET_HEADS = 4
DA_GROUPS = ((1, 16), (4, 4), (16, 1))
DA_DH = 64
DA_UNROLL = 4
ML_HEADS = 4
ML_DK = 64
N_EXPERTS = 8
TOP_K = 2
MOE_TILE = 1024
SC_WINDOW = 128
VMEM_LIMIT = 56 * 1024 * 1024

NEG = -1e30


def _cparams(*sem):
    return pltpu.CompilerParams(dimension_semantics=sem, vmem_limit_bytes=VMEM_LIMIT)


def _dot(a, b):
    return jnp.dot(a, b, preferred_element_type=F32)


def _dot_nt(a, b):
    return lax.dot_general(a, b, (((1,), (1,)), ((), ())), preferred_element_type=F32)


def _dot_tn(a, b):
    return lax.dot_general(a, b, (((0,), (0,)), ((), ())), preferred_element_type=F32)


def _sigmoid(x):
    return 1.0 / (1.0 + jnp.exp(-x))


def _silu(x):
    return x * _sigmoid(x)


def _inproj_kernel(nb, x_ref, nw_ref, wb_ref, wf_ref, zb_ref, zf_ref, h_ref):
    j = pl.program_id(1)

    @pl.when(j == 0)
    def _():
        rc = 256
        for r in range(x_ref.shape[0] // rc):
            rows = pl.ds(r * rc, rc)
            x = x_ref[rows, :]
            ms = jnp.mean(x * x, axis=-1, keepdims=True)
            h = (x * lax.rsqrt(ms + EPS) * nw_ref[...]).astype(BF16)
            h_ref[rows, :] = h
            zb_ref[rows, :] = _dot(h, wb_ref[...]).astype(zb_ref.dtype)

    @pl.when((j > 0) & (j < nb))
    def _():
        zb_ref[...] = _dot(h_ref[...], wb_ref[...]).astype(zb_ref.dtype)

    @pl.when(j >= nb)
    def _():
        zf_ref[...] = _dot(h_ref[...], wf_ref[...])


def _inproj(x, nw, wb, wf, tm, tnb, tnf):
    T, D = x.shape
    nb = wb.shape[1] // tnb
    nf = wf.shape[1] // tnf
    bcol = lambda j: jnp.minimum(j, nb - 1)
    fcol = lambda j: jnp.maximum(j - nb, 0)
    return pl.pallas_call(
        functools.partial(_inproj_kernel, nb),
        grid=(T // tm, nb + nf),
        in_specs=[pl.BlockSpec((tm, D), lambda i, j: (i, 0)),
                  pl.BlockSpec((1, D), lambda i, j: (0, 0)),
                  pl.BlockSpec((D, tnb), lambda i, j: (0, bcol(j))),
                  pl.BlockSpec((D, tnf), lambda i, j: (0, fcol(j)))],
        out_specs=[pl.BlockSpec((tm, tnb), lambda i, j: (i, bcol(j))),
                   pl.BlockSpec((tm, tnf), lambda i, j: (i, fcol(j)))],
        out_shape=[jax.ShapeDtypeStruct((T, wb.shape[1]), BF16),
                   jax.ShapeDtypeStruct((T, wf.shape[1]), F32)],
        scratch_shapes=[pltpu.VMEM((tm, D), BF16)],
        compiler_params=_cparams("parallel", "arbitrary"),
        name="inproj",
    )(x, nw, wb, wf)


def _retention_kernel(z_ref, cos_ref, sin_ref, nw_ref, o_ref, dec_ref, st_ref):
    S = z_ref.shape[0]
    n_chunks = S // CHUNK
    H = RET_HEADS
    HW = H * LANES
    row = lax.broadcasted_iota(jnp.int32, (CHUNK, CHUNK), 0).astype(F32)
    col = lax.broadcasted_iota(jnp.int32, (CHUNK, CHUNK), 1).astype(F32)
    lgs = [math.log1p(-(2.0 ** (-5.0 - h))) for h in range(H)]
    for h, lg in enumerate(lgs):
        rel = row - col
        dec_ref[h] = jnp.where(rel >= 0, jnp.exp(lg * jnp.maximum(rel, 0.0)), 0.0)
        dec_ref[H + h] = jnp.exp(lg * (row + 1.0))
        dec_ref[2 * H + h] = jnp.exp(lg * (CHUNK - 1.0 - row)) * (LANES ** -0.5)
    st_ref[...] = jnp.zeros(st_ref.shape, F32)

    def body(n, carry):
        r0 = pl.multiple_of(n * CHUNK, CHUNK)
        rows = pl.ds(r0, CHUNK)
        cos = cos_ref[rows, :]
        sin = sin_ref[rows, :]
        qbs, vs, ss, iqs, kvs = [], [], [], [], []
        for h in range(H):
            q = z_ref[rows, h * LANES:(h + 1) * LANES].astype(F32)
            k = z_ref[rows, HW + h * LANES:HW + (h + 1) * LANES].astype(F32)
            v = z_ref[rows, 2 * HW + h * LANES:2 * HW + (h + 1) * LANES]
            q = q * cos + pltpu.roll(q, LANES // 2, 1) * sin
            k = k * cos + pltpu.roll(k, LANES // 2, 1) * sin
            qb = q.astype(BF16)
            kb = (k * (LANES ** -0.5)).astype(BF16)
            kd = (k * dec_ref[2 * H + h]).astype(BF16)
            ss.append(_dot_nt(qb, kb))
            iqs.append(_dot(qb, st_ref[h].astype(BF16)))
            kvs.append(_dot_tn(kd, v))
            vs.append(v)
        ps = [(ss[h] * dec_ref[h]).astype(BF16) for h in range(H)]
        os_ = [_dot(ps[h], vs[h]) + iqs[h] * dec_ref[H + h] for h in range(H)]
        for h in range(H):
            st_ref[h] = st_ref[h] * math.exp(lgs[h] * CHUNK) + kvs[h]
            o = os_[h]
            ms = jnp.mean(o * o, axis=-1, keepdims=True)
            o = o * lax.rsqrt(ms + EPS) * nw_ref[:, h * LANES:(h + 1) * LANES]
            g = z_ref[rows, 3 * HW + h * LANES:3 * HW + (h + 1) * LANES].astype(F32)
            o_ref[rows, h * LANES:(h + 1) * LANES] = (o * _silu(g)).astype(o_ref.dtype)
        return carry

    lax.fori_loop(0, n_chunks, body, 0)


def _retention(zb, cos, sin, nw, B, S):
    T = B * S
    W = RET_HEADS * LANES
    return pl.pallas_call(
        _retention_kernel,
        grid=(B,),
        in_specs=[pl.BlockSpec((S, 4 * W), lambda b: (b, 0)),
                  pl.BlockSpec((S, LANES), lambda b: (0, 0)),
                  pl.BlockSpec((S, LANES), lambda b: (0, 0)),
                  pl.BlockSpec((1, W), lambda b: (0, 0))],
        out_specs=pl.BlockSpec((S, W), lambda b: (b, 0)),
        out_shape=jax.ShapeDtypeStruct((T, W), BF16),
        scratch_shapes=[pltpu.VMEM((3 * RET_HEADS, CHUNK, LANES), F32),
                        pltpu.VMEM((RET_HEADS, LANES, LANES), F32)],
        compiler_params=_cparams("parallel"),
        name="retention",
    )(zb, cos, sin, nw)


def _da_kernel(*refs):
    z_refs = refs[:9]
    wq_ref, wk_ref, o_ref, qn_ref, kn_ref, v_ref, np_ref, lp_ref, mp_ref = refs[9:18]
    nn_refs, ln_refs, mn_refs = refs[18:21], refs[21:24], refs[24:27]
    S = o_ref.shape[0]
    lane = lax.broadcasted_iota(jnp.int32, (1, LANES), 1)
    lo = lane < DA_DH
    row = lax.broadcasted_iota(jnp.int32, (CHUNK, CHUNK), 0)
    col = lax.broadcasted_iota(jnp.int32, (CHUNK, CHUNK), 1)
    mask_cur = col <= row
    mask_prev = col >= row

    seg = (lax.broadcasted_iota(jnp.int32, (LANES, LANES), 0) // DA_DH
           == lax.broadcasted_iota(jnp.int32, (LANES, LANES), 1) // DA_DH).astype(BF16) * (1.0 / DA_DH)

    def head_norm(x, w):
        x2 = x * x
        hi = x2.astype(BF16)
        lo_part = (x2 - hi.astype(F32)).astype(BF16)
        ms = _dot(hi, seg) + _dot(lo_part, seg)
        return x * lax.rsqrt(ms + EPS) * w

    RB = 256
    for g, (dil, nb) in enumerate(DA_GROUPS):
        L = S // dil
        zq_ref, zk_ref, zv_ref = z_refs[3 * g:3 * g + 3]
        for rho in range(dil):
            for piece in range(max(L // RB, 1)):
                n_rows = min(RB, L)
                src = pl.ds(rho + dil * piece * n_rows, n_rows, stride=dil) if dil > 1 else pl.ds(
                    piece * n_rows, n_rows)
                dst = pl.ds(rho * L + piece * n_rows, n_rows)
                q = zq_ref[src, :]
                k = zk_ref[src, :]
                v = zv_ref[src, :]
                qn_ref[dst, :] = (head_norm(q, wq_ref[...]) * (DA_DH ** -0.5)).astype(BF16)
                kn_ref[dst, :] = head_norm(k, wk_ref[...]).astype(BF16)
                v_ref[dst, :] = v.astype(BF16)

        def batch(r0, chained, first_has_prev):
            def kv(rows):
                k = kn_ref[rows, :]
                v = v_ref[rows, :]
                one = jnp.ones_like(v)
                return k, (jnp.where(lo, v, one), jnp.where(lo, one, v))

            prev = kv(pl.ds(r0 - CHUNK, CHUNK)) if first_has_prev else None
            chains = []
            for u in range(DA_UNROLL):
                rows = pl.ds(r0 + u * CHUNK, CHUNK)
                cur = kv(rows)
                p = prev if (chained and (u > 0 or first_has_prev)) else None
                q = qn_ref[rows, :]
                zero = jnp.zeros_like(q)
                for hs, qh in enumerate((jnp.where(lo, q, zero), jnp.where(lo, zero, q))):
                    sc = _dot_nt(qh, cur[0])
                    sp = None if p is None else _dot_nt(qh, p[0])
                    chains.append((sc, sp, cur[1][hs], None if p is None else p[1][hs]))
                prev = cur
            probs = []
            for sc, sp, _, _ in chains:
                sc = jnp.where(mask_cur, sc, NEG)
                m = jnp.max(sc, axis=-1, keepdims=True)
                if sp is not None:
                    sp = jnp.where(mask_prev, sp, NEG)
                    m = jnp.maximum(m, jnp.max(sp, axis=-1, keepdims=True))
                    sp = jnp.exp(sp - m).astype(BF16)
                probs.append((jnp.exp(sc - m).astype(BF16), sp, m))
            accs = []
            for (pc, pp, _), (_, _, vc, vp) in zip(probs, chains):
                acc = _dot(pc, vc)
                if pp is not None:
                    acc = acc + _dot(pp, vp)
                accs.append(acc)
            nums, dens, maxs = [], [], []
            for u in range(DA_UNROLL):
                a0, a1 = accs[2 * u], accs[2 * u + 1]
                nums.append(jnp.where(lo, a0, a1))
                dens.append(jnp.where(lo, pltpu.roll(a0, DA_DH, 1), pltpu.roll(a1, DA_DH, 1)))
                maxs.append(jnp.where(lo, probs[2 * u][2], probs[2 * u + 1][2]))
            rows = pl.ds(r0, DA_UNROLL * CHUNK)
            np_ref[rows, :] = jnp.concatenate(nums, axis=0)
            lp_ref[rows, :] = jnp.concatenate(dens, axis=0)
            mp_ref[rows, :] = jnp.concatenate(maxs, axis=0)

        span = DA_UNROLL * CHUNK
        if nb == 1:
            def singles(i, c):
                batch(pl.multiple_of(i * span, span), False, False)
                return c
            lax.fori_loop(0, S // span, singles, 0)
        else:
            def segment(sgi, c, nb=nb):
                base = pl.multiple_of(sgi * (nb * CHUNK), span)
                batch(base, True, False)
                if nb > DA_UNROLL:
                    def inner(n, c2):
                        batch(pl.multiple_of(base + n * span, span), True, True)
                        return c2
                    lax.fori_loop(1, nb // DA_UNROLL, inner, 0)
                return c
            lax.fori_loop(0, dil, segment, 0)

        for rho in range(dil):
            for piece in range(max(L // RB, 1)):
                n_rows = min(RB, L)
                dst = pl.ds(rho + dil * piece * n_rows, n_rows, stride=dil) if dil > 1 else pl.ds(
                    piece * n_rows, n_rows)
                src = pl.ds(rho * L + piece * n_rows, n_rows)
                nn_refs[g][dst, :] = np_ref[src, :]
                ln_refs[g][dst, :] = lp_ref[src, :]
                mn_refs[g][dst, :] = mp_ref[src, :]

    for piece in range(S // RB):
        rows = pl.ds(piece * RB, RB)
        m = jnp.maximum(jnp.maximum(mn_refs[0][rows, :], mn_refs[1][rows, :]), mn_refs[2][rows, :])
        num = jnp.zeros((RB, LANES), F32)
        den = jnp.zeros((RB, LANES), F32)
        for g in range(3):
            e = jnp.exp(mn_refs[g][rows, :] - m)
            num = num + e * nn_refs[g][rows, :]
            den = den + e * ln_refs[g][rows, :]
        o_ref[rows, :] = (num / den).astype(o_ref.dtype)


def _dilated_attention(zf, wq, wk, B, S):
    T = B * S
    slab = lambda k: pl.BlockSpec((S, LANES), lambda b, p: (b, 9 * p + k))
    return pl.pallas_call(
        _da_kernel,
        grid=(B, 2),
        in_specs=[slab(k) for k in range(9)] + [
                  pl.BlockSpec((1, LANES), lambda b, p: (0, 0)),
                  pl.BlockSpec((1, LANES), lambda b, p: (0, 0))],
        out_specs=pl.BlockSpec((S, LANES), lambda b, p: (b, p)),
        out_shape=jax.ShapeDtypeStruct((T, 2 * LANES), BF16),
        scratch_shapes=[pltpu.VMEM((S, LANES), BF16), pltpu.VMEM((S, LANES), BF16),
                        pltpu.VMEM((S, LANES), BF16),
                        pltpu.VMEM((S, LANES), F32), pltpu.VMEM((S, LANES), F32),
                        pltpu.VMEM((S, LANES), F32)] + [pltpu.VMEM((S, LANES), F32)] * 9,
        compiler_params=_cparams("parallel", "arbitrary"),
        name="dilated_attention",
    )(*([zf] * 9), wq, wk)


def _mlstm_kernel(qk_ref, v_ref, og_ref, gate_ref, cw_ref, gb_ref, nw_ref, o_ref,
                  xp_ref, qc_ref, st_ref, e_ref, et_ref):
    S = qk_ref.shape[0]
    n_chunks = S // CHUNK
    W = qk_ref.shape[1]
    PAD = 8
    xp_ref[0:PAD, :] = jnp.zeros((PAD, W), F32)
    RB = 256
    for piece in range(S // RB):
        xp_ref[PAD + piece * RB:PAD + (piece + 1) * RB, :] = qk_ref[piece * RB:(piece + 1) * RB, :].astype(F32)
    for piece in range(S // RB):
        for cs in range(W // LANES):
            cols = slice(cs * LANES, (cs + 1) * LANES)
            acc = cw_ref[0:1, cols] * xp_ref[pl.ds(PAD + piece * RB - 3, RB), cols]
            for i in range(1, 4):
                acc = acc + cw_ref[i:i + 1, cols] * xp_ref[pl.ds(PAD + piece * RB - 3 + i, RB), cols]
            scale = ML_DK ** -0.5 if cs >= W // LANES // 2 else 1.0
            qc_ref[piece * RB:(piece + 1) * RB, cols] = (_silu(acc) * scale).astype(BF16)

    lane = lax.broadcasted_iota(jnp.int32, (1, LANES), 1)
    lo = lane < ML_DK
    row = lax.broadcasted_iota(jnp.int32, (CHUNK, CHUNK), 0)
    col = lax.broadcasted_iota(jnp.int32, (CHUNK, CHUNK), 1)
    causal = col <= row
    tril = causal.astype(F32)
    srow = lax.broadcasted_iota(jnp.int32, (LANES, 1), 0) < ML_DK
    st_ref[...] = jnp.zeros(st_ref.shape, F32)

    is_f = (lane >= ML_HEADS) & (lane < 2 * ML_HEADS)
    for n in range(n_chunks):
        rows = pl.ds(n * CHUNK, CHUNK)
        gp = gate_ref[rows, :] + gb_ref[...]
        logf = jnp.minimum(gp, 0.0) - jnp.log(1.0 + jnp.exp(-jnp.abs(gp)))
        cum = jnp.dot(tril, jnp.where(is_f, logf, 0.0), preferred_element_type=F32,
                      precision=lax.Precision.HIGHEST)
        e = jnp.where(lane < ML_HEADS, gp, cum)
        e_ref[rows, :] = e
        et_ref[n] = e.T[0:2 * ML_HEADS, :]

    def chunk(n, mms):
        r0 = pl.multiple_of(n * CHUNK, CHUNK)
        rows = pl.ds(r0, CHUNK)
        e = e_ref[rows, :]
        et = et_ref[n]
        qhs, khs, vaugs, s_raws, iqs = [], [], [], [], []
        for pair in range(2):
            q2 = qc_ref[rows, pair * LANES:(pair + 1) * LANES]
            k2 = qc_ref[rows, W // 2 + pair * LANES:W // 2 + (pair + 1) * LANES]
            zero = jnp.zeros_like(q2)
            stb = st_ref[pair].astype(BF16)
            for hs in range(2):
                h = 2 * pair + hs
                qh = jnp.where(lo, q2, zero) if hs == 0 else jnp.where(lo, zero, q2)
                kh = jnp.where(lo, k2, zero) if hs == 0 else jnp.where(lo, zero, k2)
                vh = v_ref[rows, h * LANES:(h + 1) * LANES]
                s_raws.append(_dot_nt(qh, k2))
                iqs.append(_dot(qh, stb))
                khs.append(kh)
                vaugs.append(jnp.concatenate([vh, jnp.ones_like(vh)], axis=-1))
        ps, wis, ms_, kws, wcs, new_mms = [], [], [], [], [], []
        for h in range(ML_HEADS):
            mm = mms[h]
            i_b = jnp.broadcast_to(e[:, h:h + 1], (CHUNK, LANES))
            a_b = jnp.broadcast_to(e[:, ML_HEADS + h:ML_HEADS + h + 1], (CHUNK, LANES))
            c_row = et[h:h + 1, :] - et[ML_HEADS + h:ML_HEADS + h + 1, :]
            dmat = jnp.where(causal, a_b + c_row, NEG)
            inter = a_b + mm
            m = jnp.maximum(inter, jnp.max(dmat, axis=-1, keepdims=True))
            ps.append((s_raws[h] * jnp.exp(dmat - m)).astype(BF16))
            wis.append(jnp.exp(inter - m))
            ms_.append(m)
            m_new = m[CHUNK - 1:CHUNK, :]
            a_last = a_b[CHUNK - 1:CHUNK, :]
            wk = jnp.exp(a_last - a_b + i_b - m_new)
            kws.append((khs[h].astype(F32) * wk).astype(BF16))
            wc = jnp.exp(a_last + mm - m_new)
            wcs.append(jnp.concatenate([wc, wc], axis=1))
            new_mms.append(m_new)
        accs = [_dot(ps[h], vaugs[h]) + jnp.concatenate([wis[h], wis[h]], axis=1) * iqs[h]
                for h in range(ML_HEADS)]
        upds = [_dot_tn(kws[h], vaugs[h]) for h in range(ML_HEADS)]
        for pair in range(2):
            h0, h1 = 2 * pair, 2 * pair + 1
            st_ref[pair] = jnp.where(srow, wcs[h0], wcs[h1]) * st_ref[pair] + (upds[h0] + upds[h1])
        for h in range(ML_HEADS):
            num = accs[h][:, :LANES]
            den = accs[h][:, LANES:]
            hv = num / jnp.maximum(jnp.abs(den), jnp.exp(-ms_[h]))
            var = jnp.mean(hv * hv, axis=-1, keepdims=True)
            hv = hv * lax.rsqrt(var + EPS) * nw_ref[:, h * LANES:(h + 1) * LANES]
            og = og_ref[rows, h * LANES:(h + 1) * LANES].astype(F32)
            o_ref[rows, h * LANES:(h + 1) * LANES] = (hv * _sigmoid(og)).astype(o_ref.dtype)
        return tuple(new_mms)

    lax.fori_loop(0, n_chunks, chunk, tuple(jnp.zeros((1, LANES), F32) for _ in range(ML_HEADS)))


def _mlstm(zb, zf, conv_w, gate_bias, nw, B, S, qk_blk, gate_blk):
    T = B * S
    W = ML_HEADS * LANES
    return pl.pallas_call(
        _mlstm_kernel,
        grid=(B,),
        in_specs=[pl.BlockSpec((S, W), lambda b: (b, qk_blk)),
                  pl.BlockSpec((S, W), lambda b: (b, qk_blk + 1)),
                  pl.BlockSpec((S, W), lambda b: (b, qk_blk + 2)),
                  pl.BlockSpec((S, LANES), lambda b: (b, gate_blk)),
                  pl.BlockSpec((4, W), lambda b: (0, 0)),
                  pl.BlockSpec((1, LANES), lambda b: (0, 0)),
                  pl.BlockSpec((1, W), lambda b: (0, 0))],
        out_specs=pl.BlockSpec((S, W), lambda b: (b, 0)),
        out_shape=jax.ShapeDtypeStruct((T, W), BF16),
        scratch_shapes=[pltpu.VMEM((S + 8, W), F32), pltpu.VMEM((S, W), BF16),
                        pltpu.VMEM((2, LANES, 2 * LANES), F32),
                        pltpu.VMEM((S, LANES), F32),
                        pltpu.VMEM((S // CHUNK, 2 * ML_HEADS, LANES), F32)],
        compiler_params=_cparams("parallel"),
        name="mlstm",
    )(zb, zb, zb, zf, conv_w, gate_bias, nw)


def _merge_kernel(x_ref, oret_ref, oda_ref, oml_ref, gr_ref, gd_ref, gm_ref,
                  wr_ref, wd_ref, wm_ref, wo_ref, o_ref):
    y = _sigmoid(gr_ref[...].astype(F32)) * _dot(oret_ref[...], wr_ref[...])
    y = y + _sigmoid(gd_ref[...].astype(F32)) * _dot(oda_ref[...], wd_ref[...])
    y = y + _sigmoid(gm_ref[...].astype(F32)) * _dot(oml_ref[...], wm_ref[...])
    o_ref[...] = x_ref[...] + _dot(y.astype(BF16), wo_ref[...])


def _merge(x, o_ret, o_da, o_ml, zb, g_blk, wr, wd, wm, wo, tm):
    T, D = x.shape
    full = lambda a: pl.BlockSpec(a.shape, lambda i: (0, 0))
    return pl.pallas_call(
        _merge_kernel,
        grid=(T // tm,),
        in_specs=[pl.BlockSpec((tm, D), lambda i: (i, 0)),
                  pl.BlockSpec((tm, o_ret.shape[1]), lambda i: (i, 0)),
                  pl.BlockSpec((tm, o_da.shape[1]), lambda i: (i, 0)),
                  pl.BlockSpec((tm, o_ml.shape[1]), lambda i: (i, 0)),
                  pl.BlockSpec((tm, D), lambda i: (i, g_blk)),
                  pl.BlockSpec((tm, D), lambda i: (i, g_blk + 1)),
                  pl.BlockSpec((tm, D), lambda i: (i, g_blk + 2)),
                  full(wr), full(wd), full(wm), full(wo)],
        out_specs=pl.BlockSpec((tm, D), lambda i: (i, 0)),
        out_shape=jax.ShapeDtypeStruct((T, D), F32),
        compiler_params=_cparams("parallel"),
        name="merge_outproj",
    )(x, o_ret, o_da, o_ml, zb, zb, zb, wr, wd, wm, wo)


def _swiglu_block(h, wg_ref, wu_ref, wd_ref, a_ref):
    tf = wg_ref.shape[1]
    c0 = 0
    while c0 < tf:
        w = min(MXU_N, tf - c0)
        g = _dot(h, wg_ref[:, c0:c0 + w].astype(BF16))
        u = _dot(h, wu_ref[:, c0:c0 + w].astype(BF16))
        a_ref[:, c0:c0 + w] = (_silu(g) * u).astype(BF16)
        c0 += w
    return _dot(a_ref[...], wd_ref[...].astype(BF16))


def _ffn_kernel(x_ref, nw_ref, wg_ref, wu_ref, wd_ref, o_ref, h_ref, acc_ref, a_ref):
    j = pl.program_id(1)

    @pl.when(j == 0)
    def _():
        x = x_ref[...]
        ms = jnp.mean(x * x, axis=-1, keepdims=True)
        h_ref[...] = (x * lax.rsqrt(ms + EPS) * nw_ref[...]).astype(BF16)
        acc_ref[...] = x

    acc_ref[...] += _swiglu_block(h_ref[...], wg_ref, wu_ref, wd_ref, a_ref)

    @pl.when(j == pl.num_programs(1) - 1)
    def _():
        o_ref[...] = acc_ref[...]


def _ffn(x, nw, wg, wu, wd, tm, tf):
    T, D = x.shape
    F = wg.shape[1]
    return pl.pallas_call(
        _ffn_kernel,
        grid=(T // tm, F // tf),
        in_specs=[pl.BlockSpec((tm, D), lambda i, j: (i, 0)),
                  pl.BlockSpec((1, D), lambda i, j: (0, 0)),
                  pl.BlockSpec((D, tf), lambda i, j: (0, j)),
                  pl.BlockSpec((D, tf), lambda i, j: (0, j)),
                  pl.BlockSpec((tf, D), lambda i, j: (j, 0))],
        out_specs=pl.BlockSpec((tm, D), lambda i, j: (i, 0)),
        out_shape=jax.ShapeDtypeStruct((T, D), F32),
        scratch_shapes=[pltpu.VMEM((tm, D), BF16), pltpu.VMEM((tm, D), F32),
                        pltpu.VMEM((tm, tf), BF16)],
        compiler_params=_cparams("parallel", "arbitrary"),
        name="swiglu_ffn",
    )(x, nw, wg, wu, wd)


HI16 = 0xFFFF0000


def _pack_bf16_pairs(a, b):
    ab = pltpu.bitcast(a.astype(BF16).astype(F32), jnp.uint32)
    bb = pltpu.bitcast(b.astype(BF16).astype(F32), jnp.uint32)
    return (ab >> 16) | (bb & jnp.uint32(HI16))


def _unpack_bf16_pairs(p):
    lo = pltpu.bitcast(p << 16, F32)
    hi = pltpu.bitcast(p & jnp.uint32(HI16), F32)
    return lo, hi


def _router_kernel(x_ref, nw_ref, wr_ref, br_ref, ha_ref, hb_ref, sel_ref):
    x = x_ref[...]
    ms = jnp.mean(x * x, axis=-1, keepdims=True)
    h = x * lax.rsqrt(ms + EPS) * nw_ref[...]
    Q = h.shape[1] // 4
    ha_ref[...] = _pack_bf16_pairs(h[:, 0:Q], h[:, Q:2 * Q])
    hb_ref[...] = _pack_bf16_pairs(h[:, 2 * Q:3 * Q], h[:, 3 * Q:4 * Q])
    logits = jnp.dot(h, wr_ref[...], preferred_element_type=F32,
                     precision=lax.Precision.HIGHEST) + br_ref[...]
    lane = lax.broadcasted_iota(jnp.int32, logits.shape, 1).astype(F32)
    logits = jnp.where(lane < N_EXPERTS, logits, NEG)
    m1 = jnp.max(logits, axis=-1, keepdims=True)
    i1 = jnp.min(jnp.where(logits == m1, lane, float(LANES)), axis=-1, keepdims=True)
    rest = jnp.where(lane == i1, NEG, logits)
    m2 = jnp.max(rest, axis=-1, keepdims=True)
    i2 = jnp.min(jnp.where(rest == m2, lane, float(LANES)), axis=-1, keepdims=True)
    e2 = jnp.exp(m2 - m1)
    p1 = 1.0 / (1.0 + e2)
    p2 = e2 / (1.0 + e2)
    sel_ref[...] = jnp.where(lane == 0.0, i1, jnp.where(lane == 1.0, i2,
                             jnp.where(lane == 2.0, p1, jnp.where(lane == 3.0, p2, 0.0))))


def _router(x, nw, wr, br, tm):
    T, D = x.shape
    Q = D // 4
    return pl.pallas_call(
        _router_kernel,
        grid=(T // tm,),
        in_specs=[pl.BlockSpec((tm, D), lambda i: (i, 0)),
                  pl.BlockSpec((1, D), lambda i: (0, 0)),
                  pl.BlockSpec((D, LANES), lambda i: (0, 0)),
                  pl.BlockSpec((1, LANES), lambda i: (0, 0))],
        out_specs=[pl.BlockSpec((tm, Q), lambda i: (i, 0)),
                   pl.BlockSpec((tm, Q), lambda i: (i, 0)),
                   pl.BlockSpec((tm, LANES), lambda i: (i, 0))],
        out_shape=[jax.ShapeDtypeStruct((T, Q), jnp.uint32), jax.ShapeDtypeStruct((T, Q), jnp.uint32),
                   jax.ShapeDtypeStruct((T, LANES), F32)],
        compiler_params=_cparams("parallel"),
        name="moe_router",
    )(x, nw, wr, br)


def _moe_rank_kernel(sel_ref, pos_ref, meta_ref, cnt_ref, offs_ref, carry_ref):
    ph = pl.program_id(0)
    i = pl.program_id(1)
    tm = sel_ref.shape[0]
    lane = lax.broadcasted_iota(jnp.int32, (tm, LANES), 1).astype(F32)
    lane1 = lax.broadcasted_iota(jnp.int32, (1, LANES), 1).astype(F32)
    sel = sel_ref[...]
    i1 = sel[:, 0:1]
    i2 = sel[:, 1:2]
    onehot = jnp.where((lane == i1) | (lane == i2), 1.0, 0.0)
    colsum = jnp.sum(onehot, axis=0, keepdims=True)

    @pl.when((ph == 0) & (i == 0))
    def _():
        cnt_ref[...] = jnp.zeros(cnt_ref.shape, F32)

    @pl.when(ph == 0)
    def _():
        cnt_ref[...] += colsum

    def padded_counts():
        return jnp.floor((cnt_ref[...] + (MOE_TILE - 1.0)) * (1.0 / MOE_TILE)) * MOE_TILE

    @pl.when((ph == 1) & (i == 0))
    def _():
        k = lax.broadcasted_iota(jnp.int32, (LANES, LANES), 0)
        e = lax.broadcasted_iota(jnp.int32, (LANES, LANES), 1)
        upper = (k < e).astype(F32)
        offs_ref[...] = jnp.dot(padded_counts(), upper, preferred_element_type=F32,
                                precision=lax.Precision.HIGHEST)
        carry_ref[...] = jnp.zeros(carry_ref.shape, F32)

    @pl.when(ph == 1)
    def _():
        r = lax.broadcasted_iota(jnp.int32, (tm, tm), 0)
        c = lax.broadcasted_iota(jnp.int32, (tm, tm), 1)
        before = (c < r).astype(BF16)
        rank = _dot(before, onehot.astype(BF16)) + carry_ref[0:1, :]
        row = rank + offs_ref[0:1, :]
        pos1 = jnp.sum(jnp.where(lane == i1, row, 0.0), axis=-1, keepdims=True)
        pos2 = jnp.sum(jnp.where(lane == i2, row, 0.0), axis=-1, keepdims=True)
        both = jnp.where(lane == 0.0, pos1, jnp.where(lane == 1.0, pos2, 0.0))
        pos_ref[...] = both.T[0:8, :].astype(jnp.int32)
        carry_ref[...] += colsum

    @pl.when((ph == 1) & (i == pl.num_programs(1) - 1))
    def _():
        nrow = meta_ref.shape[0]
        padded = padded_counts()[0:1, :]
        offs = offs_ref[0:1, :]
        ends = offs + padded
        start = lax.broadcasted_iota(jnp.int32, (nrow, 1), 0).astype(F32) * MOE_TILE
        is_e = lane1 < N_EXPERTS
        te = jnp.sum(jnp.where(is_e & (ends <= start), 1.0, 0.0), axis=-1, keepdims=True)
        te = jnp.minimum(te, N_EXPERTS - 1.0)
        valid_end = jnp.sum(jnp.where(lane1 == te, offs + cnt_ref[0:1, :], 0.0), axis=-1, keepdims=True)
        nv = jnp.clip(valid_end - start, 0.0, MOE_TILE)
        total = jnp.sum(jnp.where(is_e, padded, 0.0), axis=-1, keepdims=True)
        active = jnp.where(start < total, 1.0, 0.0)
        meta_ref[...] = jnp.where(lane1 == 0.0, te, jnp.where(lane1 == 1.0, nv,
                                  jnp.where(lane1 == 2.0, active, 0.0)))


def _moe_rank(sel, tm, meta_rows):
    T = sel.shape[0]
    nt = T // tm
    return pl.pallas_call(
        _moe_rank_kernel,
        grid=(2, nt),
        in_specs=[pl.BlockSpec((tm, LANES), lambda ph, i: (i, 0))],
        out_specs=[pl.BlockSpec((None, 8, tm), lambda ph, i: (i * ph, 0, 0)),
                   pl.BlockSpec((meta_rows, LANES), lambda ph, i: (0, 0))],
        out_shape=[jax.ShapeDtypeStruct((nt, 8, tm), jnp.int32),
                   jax.ShapeDtypeStruct((meta_rows, LANES), F32)],
        scratch_shapes=[pltpu.VMEM((8, LANES), F32), pltpu.VMEM((8, LANES), F32),
                        pltpu.VMEM((8, LANES), F32)],
        compiler_params=_cparams("arbitrary", "arbitrary"),
        name="moe_rank",
    )(sel)


def _sc_mesh():
    return plsc.VectorSubcoreMesh(core_axis_name="core", subcore_axis_name="subcore")


def _sc_scatter_rows(x, idx, n_rows):
    n_idx = idx.shape[1]
    nt = x.shape[0] // SC_WINDOW
    width = x.shape[1]

    @functools.partial(pl.kernel, out_type=jax.ShapeDtypeStruct((n_rows, width), x.dtype),
                       mesh=_sc_mesh())
    def scatter_kernel(x_hbm, i_hbm, o_hbm):
        def body(x_vmem, i_vmem):
            pltpu.sync_copy(x_vmem, o_hbm.at[i_vmem.at[0]])

        pltpu.emit_pipeline(
            body, grid=(n_idx // SC_WINDOW,),
            in_specs=[pl.BlockSpec((SC_WINDOW, width), lambda i: (i % nt, 0)),
                      pl.BlockSpec((1, SC_WINDOW), lambda i: (0, i))],
            out_specs=[],
            core_axis_name=("core", "subcore"), dimension_semantics=(pltpu.PARALLEL,),
        )(x_hbm, i_hbm)

    return scatter_kernel(x, idx)


def _sc_gather_rows(x, idx):
    n_idx = idx.shape[1]
    width = x.shape[1]

    @functools.partial(pl.kernel, out_type=jax.ShapeDtypeStruct((n_idx, width), x.dtype),
                       mesh=_sc_mesh())
    def gather_kernel(x_hbm, i_hbm, o_hbm):
        def body(i_vmem, o_vmem):
            pltpu.sync_copy(x_hbm.at[i_vmem.at[0]], o_vmem)

        pltpu.emit_pipeline(
            body, grid=(n_idx // SC_WINDOW,),
            in_specs=[pl.BlockSpec((1, SC_WINDOW), lambda i: (0, i))],
            out_specs=[pl.BlockSpec((SC_WINDOW, width), lambda i: (i, 0))],
            core_axis_name=("core", "subcore"), dimension_semantics=(pltpu.PARALLEL,),
        )(i_hbm, o_hbm)

    return gather_kernel(x, idx)


def _moe_group_kernel(te_ref, nv_ref, na_ref, xa_ref, xb_ref, wg_ref, wu_ref, wd_ref,
                      ya_ref, yb_ref, h_ref, acc_ref, a_ref):
    j = pl.program_id(0)
    f = pl.program_id(1)

    @pl.when(j < na_ref[0])
    def _():
        @pl.when(f == 0)
        def _():
            tm = h_ref.shape[0]
            Q = xa_ref.shape[1]
            valid = lax.broadcasted_iota(jnp.int32, (tm, 1), 0) < nv_ref[j]
            for src, c0 in ((xa_ref, 0), (xb_ref, 2 * Q)):
                lo, hi = _unpack_bf16_pairs(src[...])
                h_ref[:, c0:c0 + Q] = jnp.where(valid, lo, 0.0).astype(BF16)
                h_ref[:, c0 + Q:c0 + 2 * Q] = jnp.where(valid, hi, 0.0).astype(BF16)
            acc_ref[...] = jnp.zeros(acc_ref.shape, F32)

        acc_ref[...] += _swiglu_block(h_ref[...], wg_ref, wu_ref, wd_ref, a_ref)

        @pl.when(f == pl.num_programs(1) - 1)
        def _():
            Q = ya_ref.shape[1]
            ya_ref[...] = _pack_bf16_pairs(acc_ref[:, 0:Q], acc_ref[:, Q:2 * Q])
            yb_ref[...] = _pack_bf16_pairs(acc_ref[:, 2 * Q:3 * Q], acc_ref[:, 3 * Q:4 * Q])


def _moe_group(te, nv, na, xa, xb, wg, wu, wd, tf):
    R, Q = xa.shape
    E, D, F = wg.shape
    nf = F // tf
    tile = lambda j, f, te, nv, na: (jnp.minimum(j, na[0] - 1), 0)
    ff = lambda j, f, na: jnp.where(j < na[0], f, nf - 1)
    grid_spec = pltpu.PrefetchScalarGridSpec(
        num_scalar_prefetch=3,
        grid=(R // MOE_TILE, nf),
        in_specs=[pl.BlockSpec((MOE_TILE, Q), tile),
                  pl.BlockSpec((MOE_TILE, Q), tile),
                  pl.BlockSpec((None, D, tf), lambda j, f, te, nv, na: (te[j], 0, ff(j, f, na))),
                  pl.BlockSpec((None, D, tf), lambda j, f, te, nv, na: (te[j], 0, ff(j, f, na))),
                  pl.BlockSpec((None, tf, D), lambda j, f, te, nv, na: (te[j], ff(j, f, na), 0))],
        out_specs=[pl.BlockSpec((MOE_TILE, Q), tile), pl.BlockSpec((MOE_TILE, Q), tile)],
        scratch_shapes=[pltpu.VMEM((MOE_TILE, D), BF16), pltpu.VMEM((MOE_TILE, D), F32),
                        pltpu.VMEM((MOE_TILE, tf), BF16)],
    )
    return pl.pallas_call(
        _moe_group_kernel,
        grid_spec=grid_spec,
        out_shape=[jax.ShapeDtypeStruct((R, Q), jnp.uint32), jax.ShapeDtypeStruct((R, Q), jnp.uint32)],
        compiler_params=_cparams("arbitrary", "arbitrary"),
        name="moe_experts",
    )(te, nv, na, xa, xb, wg, wu, wd)


def _moe_combine_kernel(x_ref, sel_ref, a1_ref, a2_ref, b1_ref, b2_ref, o_ref):
    sel = sel_ref[...]
    p1 = sel[:, 2:3]
    p2 = sel[:, 3:4]
    Q = a1_ref.shape[1]
    for first, second, c0 in ((a1_ref, a2_ref, 0), (b1_ref, b2_ref, 2 * Q)):
        lo1, hi1 = _unpack_bf16_pairs(first[...])
        lo2, hi2 = _unpack_bf16_pairs(second[...])
        o_ref[:, c0:c0 + Q] = x_ref[:, c0:c0 + Q] + (p1 * lo1 + p2 * lo2)
        o_ref[:, c0 + Q:c0 + 2 * Q] = x_ref[:, c0 + Q:c0 + 2 * Q] + (p1 * hi1 + p2 * hi2)


def _moe_combine(x, sel, ga, gb, tm):
    T, D = x.shape
    Q = ga.shape[1]
    nt = T // tm
    return pl.pallas_call(
        _moe_combine_kernel,
        grid=(nt,),
        in_specs=[pl.BlockSpec((tm, D), lambda i: (i, 0)),
                  pl.BlockSpec((tm, LANES), lambda i: (i, 0)),
                  pl.BlockSpec((tm, Q), lambda i: (i, 0)),
                  pl.BlockSpec((tm, Q), lambda i: (i + nt, 0)),
                  pl.BlockSpec((tm, Q), lambda i: (i, 0)),
                  pl.BlockSpec((tm, Q), lambda i: (i + nt, 0))],
        out_specs=pl.BlockSpec((tm, D), lambda i: (i, 0)),
        out_shape=jax.ShapeDtypeStruct((T, D), F32),
        compiler_params=_cparams("parallel"),
        name="moe_combine",
    )(x, sel, ga, ga, gb, gb)


def _moe(x, nw, w_router, b_router, wg, wu, wd):
    T, D = x.shape
    wr = jnp.pad(w_router, ((0, 0), (0, LANES - N_EXPERTS)))
    br = jnp.pad(b_router, (0, LANES - N_EXPERTS)).reshape(1, LANES)
    ha, hb, sel = _router(x, nw, wr, br, 1024)
    n_rows = TOP_K * T + N_EXPERTS * MOE_TILE
    n_tiles = n_rows // MOE_TILE
    pos, meta = _moe_rank(sel, 512, 256)
    idx = jnp.concatenate([pos[:, 0, :].reshape(1, T), pos[:, 1, :].reshape(1, T)], axis=1)
    te = meta[:n_tiles, 0].astype(jnp.int32)
    nv = meta[:n_tiles, 1].astype(jnp.int32)
    na = jnp.sum(meta[:n_tiles, 2]).astype(jnp.int32).reshape(1)
    te = jnp.where(jnp.arange(n_tiles) < na[0], te, te[na[0] - 1])
    xa = _sc_scatter_rows(ha, idx, n_rows)
    xb = _sc_scatter_rows(hb, idx, n_rows)
    ya, yb = _moe_group(te, nv, na, xa, xb, wg, wu, wd, wg.shape[2] // 4)
    ga = _sc_gather_rows(ya, idx)
    gb = _sc_gather_rows(yb, idx)
    return _moe_combine(x, sel, ga, gb, 1024)


def _split_w_in(w_in):
    sizes = (512, 512, 512, 512, 768, 768, 768, 256, 256, 512, 512, 4, 4, 1024, 1024, 1024)
    offs = [0]
    for s in sizes:
        offs.append(offs[-1] + s)
    part = lambda i: w_in[:, offs[i]:offs[i + 1]]
    rq, rk, rv, rg, dq, dk, dv, mq, mk, mv, mo, mi, mf, g_ret, g_da, g_ml = (part(i) for i in range(16))
    wb = jnp.concatenate([rq, rk, rv, rg, g_ret, g_da, g_ml, mq, mk, mv, mo], axis=1).astype(BF16)
    cols = []
    for p in range(2):
        for g in range(3):
            for t in (dq, dk, dv):
                cols.append(t[:, g * 256 + p * 128:g * 256 + (p + 1) * 128])
    pad = jnp.zeros((w_in.shape[0], 2 * LANES - 8), w_in.dtype)
    wf = jnp.concatenate(cols + [mi, mf, pad], axis=1).astype(BF16)
    return wb, wf


def _rope_tables(S):
    half = LANES // 2
    inv = jnp.power(ROPE_BASE, -jnp.arange(half, dtype=F32) / half)
    ang = jnp.arange(S, dtype=F32)[:, None] * inv[None, :]
    cos = jnp.cos(ang)
    sin = jnp.sin(ang)
    return jnp.concatenate([cos, cos], axis=1), jnp.concatenate([-sin, sin], axis=1)


def kernel(x, norm1_w, w_in, ret_norm_w, da_q_norm_w, da_k_norm_w, ml_conv_w, ml_i_bias, ml_f_bias,
           ml_norm_w, w_br_ret, w_br_da, w_br_ml, w_out, norm2_w, ffn_w_gate, ffn_w_up, ffn_w_down,
           moe_w_router, moe_b_router, moe_w_gate, moe_w_up, moe_w_down):
    B, S, D = x.shape
    T = B * S
    depth = w_in.shape[0]
    cos, sin = _rope_tables(S)
    xt = x.reshape(T, D)
    for layer in range(depth):
        wb, wf = _split_w_in(w_in[layer])
        nw1 = norm1_w[layer].reshape(1, D)
        zb, zf = _inproj(xt, nw1, wb, wf, 1024, 1664, 1280)
        o_ret = _retention(zb, cos, sin, ret_norm_w[layer].reshape(1, -1), B, S)
        wq = jnp.tile(da_q_norm_w[layer], 2).reshape(1, LANES)
        wk = jnp.tile(da_k_norm_w[layer], 2).reshape(1, LANES)
        o_da = _dilated_attention(zf, wq, wk, B, S)
        gate_bias = jnp.concatenate([ml_i_bias[layer], ml_f_bias[layer],
                                     jnp.zeros((LANES - 2 * ML_HEADS,), F32)]).reshape(1, LANES)
        o_ml = _mlstm(zb, zf, ml_conv_w[layer], gate_bias, ml_norm_w[layer].reshape(1, -1), B, S,
                      qk_blk=10, gate_blk=18)
        xt = _merge(xt, o_ret, o_da, o_ml, zb, 2,
                    w_br_ret[layer].astype(BF16), w_br_da[layer].astype(BF16),
                    w_br_ml[layer].astype(BF16), w_out[layer].astype(BF16), 512)
        nw2 = norm2_w[layer].reshape(1, D)
        j = layer // 2
        if layer % 2 == 0:
            xt = _ffn(xt, nw2, ffn_w_gate[j].astype(BF16), ffn_w_up[j].astype(BF16),
                      ffn_w_down[j].astype(BF16), 1024, 1408)
        else:
            xt = _moe(xt, nw2, moe_w_router[j], moe_b_router[j], moe_w_gate[j], moe_w_up[j],
                      moe_w_down[j])
    return xt.reshape(B, S, D)
```

```python
import functools
import math

import jax
import jax.numpy as jnp
from jax import lax
from jax.experimental import pallas as pl
from jax.experimental.pallas import tpu as pltpu
from jax.experimental.pallas import tpu_sc as plsc

F32 = jnp.float32
BF16 = jnp.bfloat16

EPS = 1e-6
D_MODEL = 1024
CHUNK = 128
LANES = 128
MXU_N = 256
ROPE_BASE = 10000.0
RET_HEADS = 4
DA_GROUPS = ((1, 16), (4, 4), (16, 1))
DA_DH = 64
DA_UNROLL = 4
ML_HEADS = 4
ML_DK = 64
N_EXPERTS = 8
TOP_K = 2
MOE_TILE = 512
SC_WINDOW = 128
VMEM_LIMIT = 56 * 1024 * 1024

NEG = -1e30


def _cparams(*sem):
    return pltpu.CompilerParams(dimension_semantics=sem, vmem_limit_bytes=VMEM_LIMIT)


def _dot(a, b):
    return jnp.dot(a, b, preferred_element_type=F32)


def _dot_nt(a, b):
    return lax.dot_general(a, b, (((1,), (1,)), ((), ())), preferred_element_type=F32)


def _dot_tn(a, b):
    return lax.dot_general(a, b, (((0,), (0,)), ((), ())), preferred_element_type=F32)


def _sigmoid(x):
    return 1.0 / (1.0 + jnp.exp(-x))


def _silu(x):
    return x * _sigmoid(x)


def _inproj_kernel(nb, x_ref, nw_ref, wb_ref, wf_ref, zb_ref, zf_ref, h_ref):
    j = pl.program_id(1)

    @pl.when(j == 0)
    def _():
        rc = 512
        for r in range(x_ref.shape[0] // rc):
            rows = pl.ds(r * rc, rc)
            x = x_ref[rows, :]
            ms = jnp.mean(x * x, axis=-1, keepdims=True)
            h = (x * lax.rsqrt(ms + EPS) * nw_ref[...]).astype(BF16)
            h_ref[rows, :] = h
            zb_ref[rows, :] = _dot(h, wb_ref[...]).astype(zb_ref.dtype)

    @pl.when((j > 0) & (j < nb))
    def _():
        zb_ref[...] = _dot(h_ref[...], wb_ref[...]).astype(zb_ref.dtype)

    @pl.when(j >= nb)
    def _():
        zf_ref[...] = _dot(h_ref[...], wf_ref[...])


def _inproj(x, nw, wb, wf, tm, tnb, tnf):
    T, D = x.shape
    nb = wb.shape[1] // tnb
    nf = wf.shape[1] // tnf
    bcol = lambda j: jnp.minimum(j, nb - 1)
    fcol = lambda j: jnp.maximum(j - nb, 0)
    return pl.pallas_call(
        functools.partial(_inproj_kernel, nb),
        grid=(T // tm, nb + nf),
        in_specs=[pl.BlockSpec((tm, D), lambda i, j: (i, 0)),
                  pl.BlockSpec((1, D), lambda i, j: (0, 0)),
                  pl.BlockSpec((D, tnb), lambda i, j: (0, bcol(j))),
                  pl.BlockSpec((D, tnf), lambda i, j: (0, fcol(j)))],
        out_specs=[pl.BlockSpec((tm, tnb), lambda i, j: (i, bcol(j))),
                   pl.BlockSpec((tm, tnf), lambda i, j: (i, fcol(j)))],
        out_shape=[jax.ShapeDtypeStruct((T, wb.shape[1]), BF16),
                   jax.ShapeDtypeStruct((T, wf.shape[1]), F32)],
        scratch_shapes=[pltpu.VMEM((tm, D), BF16)],
        compiler_params=_cparams("parallel", "arbitrary"),
        name="inproj",
    )(x, nw, wb, wf)


def _retention_kernel(z_ref, cos_ref, sin_ref, nw_ref, o_ref, dec_ref, st_ref):
    S = z_ref.shape[0]
    n_chunks = S // CHUNK
    H = RET_HEADS
    HW = H * LANES
    row = lax.broadcasted_iota(jnp.int32, (CHUNK, CHUNK), 0).astype(F32)
    col = lax.broadcasted_iota(jnp.int32, (CHUNK, CHUNK), 1).astype(F32)
    lgs = [math.log1p(-(2.0 ** (-5.0 - h))) for h in range(H)]
    for h, lg in enumerate(lgs):
        rel = row - col
        dec_ref[h] = jnp.where(rel >= 0, jnp.exp(lg * jnp.maximum(rel, 0.0)), 0.0)
        dec_ref[H + h] = jnp.exp(lg * (row + 1.0))
        dec_ref[2 * H + h] = jnp.exp(lg * (CHUNK - 1.0 - row)) * (LANES ** -0.5)
    st_ref[...] = jnp.zeros(st_ref.shape, F32)

    def body(n, carry):
        r0 = pl.multiple_of(n * CHUNK, CHUNK)
        rows = pl.ds(r0, CHUNK)
        cos = cos_ref[rows, :]
        sin = sin_ref[rows, :]
        qbs, vs, ss, iqs, kvs = [], [], [], [], []
        for h in range(H):
            q = z_ref[rows, h * LANES:(h + 1) * LANES].astype(F32)
            k = z_ref[rows, HW + h * LANES:HW + (h + 1) * LANES].astype(F32)
            v = z_ref[rows, 2 * HW + h * LANES:2 * HW + (h + 1) * LANES]
            q = q * cos + pltpu.roll(q, LANES // 2, 1) * sin
            k = k * cos + pltpu.roll(k, LANES // 2, 1) * sin
            qb = q.astype(BF16)
            kb = (k * (LANES ** -0.5)).astype(BF16)
            kd = (k * dec_ref[2 * H + h]).astype(BF16)
            ss.append(_dot_nt(qb, kb))
            iqs.append(_dot(qb, st_ref[h].astype(BF16)))
            kvs.append(_dot_tn(kd, v))
            vs.append(v)
        ps = [(ss[h] * dec_ref[h]).astype(BF16) for h in range(H)]
        os_ = [_dot(ps[h], vs[h]) + iqs[h] * dec_ref[H + h] for h in range(H)]
        for h in range(H):
            st_ref[h] = st_ref[h] * math.exp(lgs[h] * CHUNK) + kvs[h]
            o = os_[h]
            ms = jnp.mean(o * o, axis=-1, keepdims=True)
            o = o * lax.rsqrt(ms + EPS) * nw_ref[:, h * LANES:(h + 1) * LANES]
            g = z_ref[rows, 3 * HW + h * LANES:3 * HW + (h + 1) * LANES].astype(F32)
            o_ref[rows, h * LANES:(h + 1) * LANES] = (o * _silu(g)).astype(o_ref.dtype)
        return carry

    lax.fori_loop(0, n_chunks, body, 0)


def _retention(zb, cos, sin, nw, B, S):
    T = B * S
    W = RET_HEADS * LANES
    return pl.pallas_call(
        _retention_kernel,
        grid=(B,),
        in_specs=[pl.BlockSpec((S, 4 * W), lambda b: (b, 0)),
                  pl.BlockSpec((S, LANES), lambda b: (0, 0)),
                  pl.BlockSpec((S, LANES), lambda b: (0, 0)),
                  pl.BlockSpec((1, W), lambda b: (0, 0))],
        out_specs=pl.BlockSpec((S, W), lambda b: (b, 0)),
        out_shape=jax.ShapeDtypeStruct((T, W), BF16),
        scratch_shapes=[pltpu.VMEM((3 * RET_HEADS, CHUNK, LANES), F32),
                        pltpu.VMEM((RET_HEADS, LANES, LANES), F32)],
        compiler_params=_cparams("parallel"),
        name="retention",
    )(zb, cos, sin, nw)


def _da_kernel(*refs):
    z_refs = refs[:9]
    wq_ref, wk_ref, o_ref, qn_ref, kn_ref, v_ref, np_ref, lp_ref, mp_ref = refs[9:18]
    nn_refs, ln_refs, mn_refs = refs[18:21], refs[21:24], refs[24:27]
    stage_refs = refs[27:30]
    S = o_ref.shape[0]
    lane = lax.broadcasted_iota(jnp.int32, (1, LANES), 1)
    lo = lane < DA_DH
    row = lax.broadcasted_iota(jnp.int32, (CHUNK, CHUNK), 0)
    col = lax.broadcasted_iota(jnp.int32, (CHUNK, CHUNK), 1)
    mask_cur = col <= row
    mask_prev = col >= row

    seg = (lax.broadcasted_iota(jnp.int32, (LANES, LANES), 0) // DA_DH
           == lax.broadcasted_iota(jnp.int32, (LANES, LANES), 1) // DA_DH).astype(BF16) * (1.0 / DA_DH)

    def head_norm(x, w):
        x2 = x * x
        hi = x2.astype(BF16)
        lo_part = (x2 - hi.astype(F32)).astype(BF16)
        ms = _dot(hi, seg) + _dot(lo_part, seg)
        return x * lax.rsqrt(ms + EPS) * w

    RB = 256
    for g, (dil, nb) in enumerate(DA_GROUPS):
        L = S // dil
        zq_ref, zk_ref, zv_ref = z_refs[3 * g:3 * g + 3]
        if dil > 1:
            for piece in range(S // RB):
                rows = pl.ds(piece * RB, RB)
                for src_ref, dst_ref in zip((zq_ref, zk_ref, zv_ref), stage_refs):
                    dst_ref[rows, :] = src_ref[rows, :].astype(F32)
            zq_ref, zk_ref, zv_ref = stage_refs
        for rho in range(dil):
            for piece in range(max(L // RB, 1)):
                n_rows = min(RB, L)
                src = pl.ds(rho + dil * piece * n_rows, n_rows, stride=dil) if dil > 1 else pl.ds(
                    piece * n_rows, n_rows)
                dst = pl.ds(rho * L + piece * n_rows, n_rows)
                q = zq_ref[src, :].astype(F32)
                k = zk_ref[src, :].astype(F32)
                v = zv_ref[src, :]
                qn_ref[dst, :] = (head_norm(q, wq_ref[...]) * (DA_DH ** -0.5)).astype(BF16)
                kn_ref[dst, :] = head_norm(k, wk_ref[...]).astype(BF16)
                v_ref[dst, :] = v.astype(BF16)

        def batch(r0, chained, first_has_prev):
            def kv(rows):
                k = kn_ref[rows, :]
                v = v_ref[rows, :]
                one = jnp.ones_like(v)
                return k, (jnp.where(lo, v, one), jnp.where(lo, one, v))

            prev = kv(pl.ds(r0 - CHUNK, CHUNK)) if first_has_prev else None
            chains = []
            for u in range(DA_UNROLL):
                rows = pl.ds(r0 + u * CHUNK, CHUNK)
                cur = kv(rows)
                p = prev if (chained and (u > 0 or first_has_prev)) else None
                q = qn_ref[rows, :]
                zero = jnp.zeros_like(q)
                for hs, qh in enumerate((jnp.where(lo, q, zero), jnp.where(lo, zero, q))):
                    sc = _dot_nt(qh, cur[0])
                    sp = None if p is None else _dot_nt(qh, p[0])
                    chains.append((sc, sp, cur[1][hs], None if p is None else p[1][hs]))
                prev = cur
            probs = []
            for sc, sp, _, _ in chains:
                sc = jnp.where(mask_cur, sc, NEG)
                m = jnp.max(sc, axis=-1, keepdims=True)
                if sp is not None:
                    sp = jnp.where(mask_prev, sp, NEG)
                    m = jnp.maximum(m, jnp.max(sp, axis=-1, keepdims=True))
                    sp = jnp.exp(sp - m).astype(BF16)
                probs.append((jnp.exp(sc - m).astype(BF16), sp, m))
            accs = []
            for (pc, pp, _), (_, _, vc, vp) in zip(probs, chains):
                acc = _dot(pc, vc)
                if pp is not None:
                    acc = acc + _dot(pp, vp)
                accs.append(acc)
            nums, dens, maxs = [], [], []
            for u in range(DA_UNROLL):
                a0, a1 = accs[2 * u], accs[2 * u + 1]
                nums.append(jnp.where(lo, a0, a1))
                dens.append(jnp.where(lo, pltpu.roll(a0, DA_DH, 1), pltpu.roll(a1, DA_DH, 1)))
                maxs.append(jnp.where(lo, probs[2 * u][2], probs[2 * u + 1][2]))
            rows = pl.ds(r0, DA_UNROLL * CHUNK)
            np_ref[rows, :] = jnp.concatenate(nums, axis=0)
            lp_ref[rows, :] = jnp.concatenate(dens, axis=0)
            mp_ref[rows, :] = jnp.concatenate(maxs, axis=0)

        span = DA_UNROLL * CHUNK
        if nb == 1:
            def singles(i, c):
                batch(pl.multiple_of(i * span, span), False, False)
                return c
            lax.fori_loop(0, S // span, singles, 0)
        else:
            def segment(sgi, c, nb=nb):
                base = pl.multiple_of(sgi * (nb * CHUNK), span)
                batch(base, True, False)
                if nb > DA_UNROLL:
                    def inner(n, c2):
                        batch(pl.multiple_of(base + n * span, span), True, True)
                        return c2
                    lax.fori_loop(1, nb // DA_UNROLL, inner, 0)
                return c
            lax.fori_loop(0, dil, segment, 0)

        for rho in range(dil):
            for piece in range(max(L // RB, 1)):
                n_rows = min(RB, L)
                dst = pl.ds(rho + dil * piece * n_rows, n_rows, stride=dil) if dil > 1 else pl.ds(
                    piece * n_rows, n_rows)
                src = pl.ds(rho * L + piece * n_rows, n_rows)
                nn_refs[g][dst, :] = np_ref[src, :]
                ln_refs[g][dst, :] = lp_ref[src, :]
                mn_refs[g][dst, :] = mp_ref[src, :]

    for piece in range(S // RB):
        rows = pl.ds(piece * RB, RB)
        m = jnp.maximum(jnp.maximum(mn_refs[0][rows, :], mn_refs[1][rows, :]), mn_refs[2][rows, :])
        num = jnp.zeros((RB, LANES), F32)
        den = jnp.zeros((RB, LANES), F32)
        for g in range(3):
            e = jnp.exp(mn_refs[g][rows, :] - m)
            num = num + e * nn_refs[g][rows, :]
            den = den + e * ln_refs[g][rows, :]
        o_ref[rows, :] = (num / den).astype(o_ref.dtype)


def _dilated_attention(zb, slab0, wq, wk, B, S):
    T = B * S
    slab = lambda k: pl.BlockSpec((S, LANES), lambda b, p: (b, slab0 + 9 * p + k))
    return pl.pallas_call(
        _da_kernel,
        grid=(B, 2),
        in_specs=[slab(k) for k in range(9)] + [
                  pl.BlockSpec((1, LANES), lambda b, p: (0, 0)),
                  pl.BlockSpec((1, LANES), lambda b, p: (0, 0))],
        out_specs=pl.BlockSpec((S, LANES), lambda b, p: (b, p)),
        out_shape=jax.ShapeDtypeStruct((T, 2 * LANES), BF16),
        scratch_shapes=[pltpu.VMEM((S, LANES), BF16), pltpu.VMEM((S, LANES), BF16),
                        pltpu.VMEM((S, LANES), BF16),
                        pltpu.VMEM((S, LANES), F32), pltpu.VMEM((S, LANES), F32),
                        pltpu.VMEM((S, LANES), F32)] + [pltpu.VMEM((S, LANES), F32)] * 12,
        compiler_params=_cparams("parallel", "arbitrary"),
        name="dilated_attention",
    )(*([zb] * 9), wq, wk)


def _mlstm_kernel(qk_ref, v_ref, og_ref, gate_ref, cw_ref, gb_ref, nw_ref, o_ref,
                  xp_ref, qc_ref, st_ref, e_ref, et_ref):
    S = qk_ref.shape[0]
    n_chunks = S // CHUNK
    W = qk_ref.shape[1]
    PAD = 8
    xp_ref[0:PAD, :] = jnp.zeros((PAD, W), F32)
    RB = 256
    for piece in range(S // RB):
        xp_ref[PAD + piece * RB:PAD + (piece + 1) * RB, :] = qk_ref[piece * RB:(piece + 1) * RB, :].astype(F32)
    for piece in range(S // RB):
        for cs in range(W // LANES):
            cols = slice(cs * LANES, (cs + 1) * LANES)
            acc = cw_ref[0:1, cols] * xp_ref[pl.ds(PAD + piece * RB - 3, RB), cols]
            for i in range(1, 4):
                acc = acc + cw_ref[i:i + 1, cols] * xp_ref[pl.ds(PAD + piece * RB - 3 + i, RB), cols]
            scale = ML_DK ** -0.5 if cs >= W // LANES // 2 else 1.0
            qc_ref[piece * RB:(piece + 1) * RB, cols] = (_silu(acc) * scale).astype(BF16)

    lane = lax.broadcasted_iota(jnp.int32, (1, LANES), 1)
    lo = lane < ML_DK
    row = lax.broadcasted_iota(jnp.int32, (CHUNK, CHUNK), 0)
    col = lax.broadcasted_iota(jnp.int32, (CHUNK, CHUNK), 1)
    causal = col <= row
    tril = causal.astype(F32)
    srow = lax.broadcasted_iota(jnp.int32, (LANES, 1), 0) < ML_DK
    st_ref[...] = jnp.zeros(st_ref.shape, F32)

    is_f = (lane >= ML_HEADS) & (lane < 2 * ML_HEADS)
    for n in range(n_chunks):
        rows = pl.ds(n * CHUNK, CHUNK)
        gp = gate_ref[rows, :] + gb_ref[...]
        logf = jnp.minimum(gp, 0.0) - jnp.log(1.0 + jnp.exp(-jnp.abs(gp)))
        cum = jnp.dot(tril, jnp.where(is_f, logf, 0.0), preferred_element_type=F32,
                      precision=lax.Precision.HIGHEST)
        e = jnp.where(lane < ML_HEADS, gp, cum)
        e_ref[rows, :] = e
        et_ref[n] = e.T[0:2 * ML_HEADS, :]

    def chunk(n, mms):
        r0 = pl.multiple_of(n * CHUNK, CHUNK)
        rows = pl.ds(r0, CHUNK)
        e = e_ref[rows, :]
        et = et_ref[n]
        qhs, khs, vaugs, s_raws, iqs = [], [], [], [], []
        for pair in range(2):
            q2 = qc_ref[rows, pair * LANES:(pair + 1) * LANES]
            k2 = qc_ref[rows, W // 2 + pair * LANES:W // 2 + (pair + 1) * LANES]
            zero = jnp.zeros_like(q2)
            stb = st_ref[pair].astype(BF16)
            for hs in range(2):
                h = 2 * pair + hs
                qh = jnp.where(lo, q2, zero) if hs == 0 else jnp.where(lo, zero, q2)
                kh = jnp.where(lo, k2, zero) if hs == 0 else jnp.where(lo, zero, k2)
                vh = v_ref[rows, h * LANES:(h + 1) * LANES]
                s_raws.append(_dot_nt(qh, k2))
                iqs.append(_dot(qh, stb))
                khs.append(kh)
                vaugs.append(jnp.concatenate([vh, jnp.ones_like(vh)], axis=-1))
        ps, wis, ms_, kws, wcs, new_mms = [], [], [], [], [], []
        for h in range(ML_HEADS):
            mm = mms[h]
            i_b = jnp.broadcast_to(e[:, h:h + 1], (CHUNK, LANES))
            a_b = jnp.broadcast_to(e[:, ML_HEADS + h:ML_HEADS + h + 1], (CHUNK, LANES))
            c_row = et[h:h + 1, :] - et[ML_HEADS + h:ML_HEADS + h + 1, :]
            dmat = jnp.where(causal, a_b + c_row, NEG)
            inter = a_b + mm
            m = jnp.maximum(inter, jnp.max(dmat, axis=-1, keepdims=True))
            ps.append((s_raws[h] * jnp.exp(dmat - m)).astype(BF16))
            wis.append(jnp.exp(inter - m))
            ms_.append(m)
            m_new = m[CHUNK - 1:CHUNK, :]
            a_last = a_b[CHUNK - 1:CHUNK, :]
            wk = jnp.exp(a_last - a_b + i_b - m_new)
            kws.append((khs[h].astype(F32) * wk).astype(BF16))
            wc = jnp.exp(a_last + mm - m_new)
            wcs.append(jnp.concatenate([wc, wc], axis=1))
            new_mms.append(m_new)
        accs = [_dot(ps[h], vaugs[h]) + jnp.concatenate([wis[h], wis[h]], axis=1) * iqs[h]
                for h in range(ML_HEADS)]
        upds = [_dot_tn(kws[h], vaugs[h]) for h in range(ML_HEADS)]
        for pair in range(2):
            h0, h1 = 2 * pair, 2 * pair + 1
            st_ref[pair] = jnp.where(srow, wcs[h0], wcs[h1]) * st_ref[pair] + (upds[h0] + upds[h1])
        for h in range(ML_HEADS):
            num = accs[h][:, :LANES]
            den = accs[h][:, LANES:]
            hv = num / jnp.maximum(jnp.abs(den), jnp.exp(-ms_[h]))
            var = jnp.mean(hv * hv, axis=-1, keepdims=True)
            hv = hv * lax.rsqrt(var + EPS) * nw_ref[:, h * LANES:(h + 1) * LANES]
            og = og_ref[rows, h * LANES:(h + 1) * LANES].astype(F32)
            o_ref[rows, h * LANES:(h + 1) * LANES] = (hv * _sigmoid(og)).astype(o_ref.dtype)
        return tuple(new_mms)

    lax.fori_loop(0, n_chunks, chunk, tuple(jnp.zeros((1, LANES), F32) for _ in range(ML_HEADS)))


def _mlstm(zb, zf, conv_w, gate_bias, nw, B, S, qk_blk, gate_blk):
    T = B * S
    W = ML_HEADS * LANES
    return pl.pallas_call(
        _mlstm_kernel,
        grid=(B,),
        in_specs=[pl.BlockSpec((S, W), lambda b: (b, qk_blk)),
                  pl.BlockSpec((S, W), lambda b: (b, qk_blk + 1)),
                  pl.BlockSpec((S, W), lambda b: (b, qk_blk + 2)),
                  pl.BlockSpec((S, LANES), lambda b: (b, gate_blk)),
                  pl.BlockSpec((4, W), lambda b: (0, 0)),
                  pl.BlockSpec((1, LANES), lambda b: (0, 0)),
                  pl.BlockSpec((1, W), lambda b: (0, 0))],
        out_specs=pl.BlockSpec((S, W), lambda b: (b, 0)),
        out_shape=jax.ShapeDtypeStruct((T, W), BF16),
        scratch_shapes=[pltpu.VMEM((S + 8, W), F32), pltpu.VMEM((S, W), BF16),
                        pltpu.VMEM((2, LANES, 2 * LANES), F32),
                        pltpu.VMEM((S, LANES), F32),
                        pltpu.VMEM((S // CHUNK, 2 * ML_HEADS, LANES), F32)],
        compiler_params=_cparams("parallel"),
        name="mlstm",
    )(zb, zb, zb, zf, conv_w, gate_bias, nw)


def _merge_kernel(x_ref, oret_ref, oda_ref, oml_ref, gr_ref, gd_ref, gm_ref,
                  wr_ref, wd_ref, wm_ref, wo_ref, o_ref):
    y = _sigmoid(gr_ref[...].astype(F32)) * _dot(oret_ref[...], wr_ref[...])
    y = y + _sigmoid(gd_ref[...].astype(F32)) * _dot(oda_ref[...], wd_ref[...])
    y = y + _sigmoid(gm_ref[...].astype(F32)) * _dot(oml_ref[...], wm_ref[...])
    o_ref[...] = x_ref[...] + _dot(y.astype(BF16), wo_ref[...])


def _merge(x, o_ret, o_da, o_ml, zb, g_blk, wr, wd, wm, wo, tm):
    T, D = x.shape
    full = lambda a: pl.BlockSpec(a.shape, lambda i: (0, 0))
    return pl.pallas_call(
        _merge_kernel,
        grid=(T // tm,),
        in_specs=[pl.BlockSpec((tm, D), lambda i: (i, 0)),
                  pl.BlockSpec((tm, o_ret.shape[1]), lambda i: (i, 0)),
                  pl.BlockSpec((tm, o_da.shape[1]), lambda i: (i, 0)),
                  pl.BlockSpec((tm, o_ml.shape[1]), lambda i: (i, 0)),
                  pl.BlockSpec((tm, D), lambda i: (i, g_blk)),
                  pl.BlockSpec((tm, D), lambda i: (i, g_blk + 1)),
                  pl.BlockSpec((tm, D), lambda i: (i, g_blk + 2)),
                  full(wr), full(wd), full(wm), full(wo)],
        out_specs=pl.BlockSpec((tm, D), lambda i: (i, 0)),
        out_shape=jax.ShapeDtypeStruct((T, D), F32),
        compiler_params=_cparams("parallel"),
        name="merge_outproj",
    )(x, o_ret, o_da, o_ml, zb, zb, zb, wr, wd, wm, wo)


def _swiglu_block(h, wg_ref, wu_ref, wd_ref, a_ref):
    tf = wg_ref.shape[1]
    c0 = 0
    while c0 < tf:
        w = min(MXU_N, tf - c0)
        g = _dot(h, wg_ref[:, c0:c0 + w].astype(BF16))
        u = _dot(h, wu_ref[:, c0:c0 + w].astype(BF16))
        a_ref[:, c0:c0 + w] = (_silu(g) * u).astype(BF16)
        c0 += w
    return _dot(a_ref[...], wd_ref[...].astype(BF16))


def _ffn_kernel(x_ref, nw_ref, wg_ref, wu_ref, wd_ref, o_ref, h_ref, acc_ref, a_ref):
    j = pl.program_id(1)

    @pl.when(j == 0)
    def _():
        x = x_ref[...]
        ms = jnp.mean(x * x, axis=-1, keepdims=True)
        h_ref[...] = (x * lax.rsqrt(ms + EPS) * nw_ref[...]).astype(BF16)
        acc_ref[...] = x

    acc_ref[...] += _swiglu_block(h_ref[...], wg_ref, wu_ref, wd_ref, a_ref)

    @pl.when(j == pl.num_programs(1) - 1)
    def _():
        o_ref[...] = acc_ref[...]


def _ffn(x, nw, wg, wu, wd, tm, tf):
    T, D = x.shape
    F = wg.shape[1]
    return pl.pallas_call(
        _ffn_kernel,
        grid=(T // tm, F // tf),
        in_specs=[pl.BlockSpec((tm, D), lambda i, j: (i, 0)),
                  pl.BlockSpec((1, D), lambda i, j: (0, 0)),
                  pl.BlockSpec((D, tf), lambda i, j: (0, j)),
                  pl.BlockSpec((D, tf), lambda i, j: (0, j)),
                  pl.BlockSpec((tf, D), lambda i, j: (j, 0))],
        out_specs=pl.BlockSpec((tm, D), lambda i, j: (i, 0)),
        out_shape=jax.ShapeDtypeStruct((T, D), F32),
        scratch_shapes=[pltpu.VMEM((tm, D), BF16), pltpu.VMEM((tm, D), F32),
                        pltpu.VMEM((tm, tf), BF16)],
        compiler_params=_cparams("parallel", "arbitrary"),
        name="swiglu_ffn",
    )(x, nw, wg, wu, wd)


HI16 = 0xFFFF0000


def _pack_bf16_pairs(a, b):
    ab = pltpu.bitcast(a.astype(BF16).astype(F32), jnp.uint32)
    bb = pltpu.bitcast(b.astype(BF16).astype(F32), jnp.uint32)
    return (ab >> 16) | (bb & jnp.uint32(HI16))


def _unpack_bf16_pairs(p):
    lo = pltpu.bitcast(p << 16, F32)
    hi = pltpu.bitcast(p & jnp.uint32(HI16), F32)
    return lo, hi


def _router_kernel(x_ref, nw_ref, wr_ref, br_ref, ha_ref, hb_ref, sel_ref):
    x = x_ref[...]
    ms = jnp.mean(x * x, axis=-1, keepdims=True)
    h = x * lax.rsqrt(ms + EPS) * nw_ref[...]
    Q = h.shape[1] // 4
    ha_ref[...] = _pack_bf16_pairs(h[:, 0:Q], h[:, Q:2 * Q])
    hb_ref[...] = _pack_bf16_pairs(h[:, 2 * Q:3 * Q], h[:, 3 * Q:4 * Q])
    logits = jnp.dot(h, wr_ref[...], preferred_element_type=F32,
                     precision=lax.Precision.HIGHEST) + br_ref[...]
    lane = lax.broadcasted_iota(jnp.int32, logits.shape, 1).astype(F32)
    logits = jnp.where(lane < N_EXPERTS, logits, NEG)
    m1 = jnp.max(logits, axis=-1, keepdims=True)
    i1 = jnp.min(jnp.where(logits == m1, lane, float(LANES)), axis=-1, keepdims=True)
    rest = jnp.where(lane == i1, NEG, logits)
    m2 = jnp.max(rest, axis=-1, keepdims=True)
    i2 = jnp.min(jnp.where(rest == m2, lane, float(LANES)), axis=-1, keepdims=True)
    e2 = jnp.exp(m2 - m1)
    p1 = 1.0 / (1.0 + e2)
    p2 = e2 / (1.0 + e2)
    sel_ref[...] = jnp.where(lane == 0.0, i1, jnp.where(lane == 1.0, i2,
                             jnp.where(lane == 2.0, p1, jnp.where(lane == 3.0, p2, 0.0))))


def _router(x, nw, wr, br, tm):
    T, D = x.shape
    Q = D // 4
    return pl.pallas_call(
        _router_kernel,
        grid=(T // tm,),
        in_specs=[pl.BlockSpec((tm, D), lambda i: (i, 0)),
                  pl.BlockSpec((1, D), lambda i: (0, 0)),
                  pl.BlockSpec((D, LANES), lambda i: (0, 0)),
                  pl.BlockSpec((1, LANES), lambda i: (0, 0))],
        out_specs=[pl.BlockSpec((tm, Q), lambda i: (i, 0)),
                   pl.BlockSpec((tm, Q), lambda i: (i, 0)),
                   pl.BlockSpec((tm, LANES), lambda i: (i, 0))],
        out_shape=[jax.ShapeDtypeStruct((T, Q), jnp.uint32), jax.ShapeDtypeStruct((T, Q), jnp.uint32),
                   jax.ShapeDtypeStruct((T, LANES), F32)],
        compiler_params=_cparams("parallel"),
        name="moe_router",
    )(x, nw, wr, br)


def _moe_rank_kernel(sel_ref, pos_ref, meta_ref, cnt_ref, offs_ref, carry_ref):
    ph = pl.program_id(0)
    i = pl.program_id(1)
    tm = sel_ref.shape[0]
    lane = lax.broadcasted_iota(jnp.int32, (tm, LANES), 1).astype(F32)
    lane1 = lax.broadcasted_iota(jnp.int32, (1, LANES), 1).astype(F32)
    sel = sel_ref[...]
    i1 = sel[:, 0:1]
    i2 = sel[:, 1:2]
    onehot = jnp.where((lane == i1) | (lane == i2), 1.0, 0.0)
    colsum = jnp.sum(onehot, axis=0, keepdims=True)

    @pl.when((ph == 0) & (i == 0))
    def _():
        cnt_ref[...] = jnp.zeros(cnt_ref.shape, F32)

    @pl.when(ph == 0)
    def _():
        cnt_ref[...] += colsum

    def padded_counts():
        return jnp.floor((cnt_ref[...] + (MOE_TILE - 1.0)) * (1.0 / MOE_TILE)) * MOE_TILE

    @pl.when((ph == 1) & (i == 0))
    def _():
        k = lax.broadcasted_iota(jnp.int32, (LANES, LANES), 0)
        e = lax.broadcasted_iota(jnp.int32, (LANES, LANES), 1)
        upper = (k < e).astype(F32)
        offs_ref[...] = jnp.dot(padded_counts(), upper, preferred_element_type=F32,
                                precision=lax.Precision.HIGHEST)
        carry_ref[...] = jnp.zeros(carry_ref.shape, F32)

    @pl.when(ph == 1)
    def _():
        r = lax.broadcasted_iota(jnp.int32, (tm, tm), 0)
        c = lax.broadcasted_iota(jnp.int32, (tm, tm), 1)
        before = (c < r).astype(BF16)
        rank = _dot(before, onehot.astype(BF16)) + carry_ref[0:1, :]
        row = rank + offs_ref[0:1, :]
        pos1 = jnp.sum(jnp.where(lane == i1, row, 0.0), axis=-1, keepdims=True)
        pos2 = jnp.sum(jnp.where(lane == i2, row, 0.0), axis=-1, keepdims=True)
        both = jnp.where(lane == 0.0, pos1, jnp.where(lane == 1.0, pos2, 0.0))
        pos_ref[...] = both.T[0:8, :].astype(jnp.int32)
        carry_ref[...] += colsum

    @pl.when((ph == 1) & (i == pl.num_programs(1) - 1))
    def _():
        nrow = meta_ref.shape[0]
        padded = padded_counts()[0:1, :]
        offs = offs_ref[0:1, :]
        ends = offs + padded
        start = lax.broadcasted_iota(jnp.int32, (nrow, 1), 0).astype(F32) * MOE_TILE
        is_e = lane1 < N_EXPERTS
        te = jnp.sum(jnp.where(is_e & (ends <= start), 1.0, 0.0), axis=-1, keepdims=True)
        te = jnp.minimum(te, N_EXPERTS - 1.0)
        valid_end = jnp.sum(jnp.where(lane1 == te, offs + cnt_ref[0:1, :], 0.0), axis=-1, keepdims=True)
        nv = jnp.clip(valid_end - start, 0.0, MOE_TILE)
        total = jnp.sum(jnp.where(is_e, padded, 0.0), axis=-1, keepdims=True)
        active = jnp.where(start < total, 1.0, 0.0)
        meta_ref[...] = jnp.where(lane1 == 0.0, te, jnp.where(lane1 == 1.0, nv,
                                  jnp.where(lane1 == 2.0, active, 0.0)))


def _moe_rank(sel, tm, meta_rows):
    T = sel.shape[0]
    nt = T // tm
    return pl.pallas_call(
        _moe_rank_kernel,
        grid=(2, nt),
        in_specs=[pl.BlockSpec((tm, LANES), lambda ph, i: (i, 0))],
        out_specs=[pl.BlockSpec((None, 8, tm), lambda ph, i: (i * ph, 0, 0)),
                   pl.BlockSpec((meta_rows, LANES), lambda ph, i: (0, 0))],
        out_shape=[jax.ShapeDtypeStruct((nt, 8, tm), jnp.int32),
                   jax.ShapeDtypeStruct((meta_rows, LANES), F32)],
        scratch_shapes=[pltpu.VMEM((8, LANES), F32), pltpu.VMEM((8, LANES), F32),
                        pltpu.VMEM((8, LANES), F32)],
        compiler_params=_cparams("arbitrary", "arbitrary"),
        name="moe_rank",
    )(sel)


def _sc_mesh():
    return plsc.VectorSubcoreMesh(core_axis_name="core", subcore_axis_name="subcore")


def _sc_scatter_rows(x, idx, n_rows):
    n_idx = idx.shape[1]
    nt = x.shape[0] // SC_WINDOW
    width = x.shape[1]

    @functools.partial(pl.kernel, out_type=jax.ShapeDtypeStruct((n_rows, width), x.dtype),
                       mesh=_sc_mesh())
    def scatter_kernel(x_hbm, i_hbm, o_hbm):
        def body(x_vmem, i_vmem):
            pltpu.sync_copy(x_vmem, o_hbm.at[i_vmem.at[0]])

        pltpu.emit_pipeline(
            body, grid=(n_idx // SC_WINDOW,),
            in_specs=[pl.BlockSpec((SC_WINDOW, width), lambda i: (i % nt, 0)),
                      pl.BlockSpec((1, SC_WINDOW), lambda i: (0, i))],
            out_specs=[],
            core_axis_name=("core", "subcore"), dimension_semantics=(pltpu.PARALLEL,),
        )(x_hbm, i_hbm)

    return scatter_kernel(x, idx)


def _sc_gather_rows(x, idx):
    n_idx = idx.shape[1]
    width = x.shape[1]

    @functools.partial(pl.kernel, out_type=jax.ShapeDtypeStruct((n_idx, width), x.dtype),
                       mesh=_sc_mesh())
    def gather_kernel(x_hbm, i_hbm, o_hbm):
        def body(i_vmem, o_vmem):
            pltpu.sync_copy(x_hbm.at[i_vmem.at[0]], o_vmem)

        pltpu.emit_pipeline(
            body, grid=(n_idx // SC_WINDOW,),
            in_specs=[pl.BlockSpec((1, SC_WINDOW), lambda i: (0, i))],
            out_specs=[pl.BlockSpec((SC_WINDOW, width), lambda i: (i, 0))],
            core_axis_name=("core", "subcore"), dimension_semantics=(pltpu.PARALLEL,),
        )(i_hbm, o_hbm)

    return gather_kernel(x, idx)


def _moe_group_kernel(te_ref, nv_ref, na_ref, xa_ref, xb_ref, wg_ref, wu_ref, wd_ref,
                      ya_ref, yb_ref, h_ref, acc_ref, a_ref):
    j = pl.program_id(0)
    f = pl.program_id(1)

    @pl.when(j < na_ref[0])
    def _():
        @pl.when(f == 0)
        def _():
            tm = h_ref.shape[0]
            Q = xa_ref.shape[1]
            valid = lax.broadcasted_iota(jnp.int32, (tm, 1), 0) < nv_ref[j]
            for src, c0 in ((xa_ref, 0), (xb_ref, 2 * Q)):
                lo, hi = _unpack_bf16_pairs(src[...])
                h_ref[:, c0:c0 + Q] = jnp.where(valid, lo, 0.0).astype(BF16)
                h_ref[:, c0 + Q:c0 + 2 * Q] = jnp.where(valid, hi, 0.0).astype(BF16)
            acc_ref[...] = jnp.zeros(acc_ref.shape, F32)

        acc_ref[...] += _swiglu_block(h_ref[...], wg_ref, wu_ref, wd_ref, a_ref)

        @pl.when(f == pl.num_programs(1) - 1)
        def _():
            Q = ya_ref.shape[1]
            ya_ref[...] = _pack_bf16_pairs(acc_ref[:, 0:Q], acc_ref[:, Q:2 * Q])
            yb_ref[...] = _pack_bf16_pairs(acc_ref[:, 2 * Q:3 * Q], acc_ref[:, 3 * Q:4 * Q])


def _moe_group(te, nv, na, xa, xb, wg, wu, wd, tf):
    R, Q = xa.shape
    E, D, F = wg.shape
    nf = F // tf
    tile = lambda j, f, te, nv, na: (jnp.minimum(j, na[0] - 1), 0)
    ff = lambda j, f, na: jnp.where(j < na[0], f, nf - 1)
    grid_spec = pltpu.PrefetchScalarGridSpec(
        num_scalar_prefetch=3,
        grid=(R // MOE_TILE, nf),
        in_specs=[pl.BlockSpec((MOE_TILE, Q), tile),
                  pl.BlockSpec((MOE_TILE, Q), tile),
                  pl.BlockSpec((None, D, tf), lambda j, f, te, nv, na: (te[j], 0, ff(j, f, na))),
                  pl.BlockSpec((None, D, tf), lambda j, f, te, nv, na: (te[j], 0, ff(j, f, na))),
                  pl.BlockSpec((None, tf, D), lambda j, f, te, nv, na: (te[j], ff(j, f, na), 0))],
        out_specs=[pl.BlockSpec((MOE_TILE, Q), tile), pl.BlockSpec((MOE_TILE, Q), tile)],
        scratch_shapes=[pltpu.VMEM((MOE_TILE, D), BF16), pltpu.VMEM((MOE_TILE, D), F32),
                        pltpu.VMEM((MOE_TILE, tf), BF16)],
    )
    return pl.pallas_call(
        _moe_group_kernel,
        grid_spec=grid_spec,
        out_shape=[jax.ShapeDtypeStruct((R, Q), jnp.uint32), jax.ShapeDtypeStruct((R, Q), jnp.uint32)],
        compiler_params=_cparams("arbitrary", "arbitrary"),
        name="moe_experts",
    )(te, nv, na, xa, xb, wg, wu, wd)


def _moe_combine_kernel(x_ref, sel_ref, a1_ref, a2_ref, b1_ref, b2_ref, o_ref):
    sel = sel_ref[...]
    p1 = sel[:, 2:3]
    p2 = sel[:, 3:4]
    Q = a1_ref.shape[1]
    for first, second, c0 in ((a1_ref, a2_ref, 0), (b1_ref, b2_ref, 2 * Q)):
        lo1, hi1 = _unpack_bf16_pairs(first[...])
        lo2, hi2 = _unpack_bf16_pairs(second[...])
        o_ref[:, c0:c0 + Q] = x_ref[:, c0:c0 + Q] + (p1 * lo1 + p2 * lo2)
        o_ref[:, c0 + Q:c0 + 2 * Q] = x_ref[:, c0 + Q:c0 + 2 * Q] + (p1 * hi1 + p2 * hi2)


def _moe_combine(x, sel, ga, gb, tm):
    T, D = x.shape
    Q = ga.shape[1]
    nt = T // tm
    return pl.pallas_call(
        _moe_combine_kernel,
        grid=(nt,),
        in_specs=[pl.BlockSpec((tm, D), lambda i: (i, 0)),
                  pl.BlockSpec((tm, LANES), lambda i: (i, 0)),
                  pl.BlockSpec((tm, Q), lambda i: (i, 0)),
                  pl.BlockSpec((tm, Q), lambda i: (i + nt, 0)),
                  pl.BlockSpec((tm, Q), lambda i: (i, 0)),
                  pl.BlockSpec((tm, Q), lambda i: (i + nt, 0))],
        out_specs=pl.BlockSpec((tm, D), lambda i: (i, 0)),
        out_shape=jax.ShapeDtypeStruct((T, D), F32),
        compiler_params=_cparams("parallel"),
        name="moe_combine",
    )(x, sel, ga, ga, gb, gb)


def _moe(x, nw, w_router, b_router, wg, wu, wd):
    T, D = x.shape
    wr = jnp.pad(w_router, ((0, 0), (0, LANES - N_EXPERTS)))
    br = jnp.pad(b_router, (0, LANES - N_EXPERTS)).reshape(1, LANES)
    ha, hb, sel = _router(x, nw, wr, br, 1024)
    n_rows = TOP_K * T + N_EXPERTS * MOE_TILE
    n_tiles = n_rows // MOE_TILE
    pos, meta = _moe_rank(sel, 512, 256)
    idx = jnp.concatenate([pos[:, 0, :].reshape(1, T), pos[:, 1, :].reshape(1, T)], axis=1)
    te = meta[:n_tiles, 0].astype(jnp.int32)
    nv = meta[:n_tiles, 1].astype(jnp.int32)
    na = jnp.sum(meta[:n_tiles, 2]).astype(jnp.int32).reshape(1)
    te = jnp.where(jnp.arange(n_tiles) < na[0], te, te[na[0] - 1])
    xa = _sc_scatter_rows(ha, idx, n_rows)
    xb = _sc_scatter_rows(hb, idx, n_rows)
    ya, yb = _moe_group(te, nv, na, xa, xb, wg, wu, wd, wg.shape[2] // 2)
    ga = _sc_gather_rows(ya, idx)
    gb = _sc_gather_rows(yb, idx)
    return _moe_combine(x, sel, ga, gb, 1024)


def _split_w_in(w_in):
    sizes = (512, 512, 512, 512, 768, 768, 768, 256, 256, 512, 512, 4, 4, 1024, 1024, 1024)
    offs = [0]
    for s in sizes:
        offs.append(offs[-1] + s)
    part = lambda i: w_in[:, offs[i]:offs[i + 1]]
    rq, rk, rv, rg, dq, dk, dv, mq, mk, mv, mo, mi, mf, g_ret, g_da, g_ml = (part(i) for i in range(16))
    cols = []
    for p in range(2):
        for g in range(3):
            for t in (dq, dk, dv):
                cols.append(t[:, g * 256 + p * 128:g * 256 + (p + 1) * 128])
    wb = jnp.concatenate([rq, rk, rv, rg, g_ret, g_da, g_ml, mq, mk, mv, mo] + cols,
                         axis=1).astype(BF16)
    pad = jnp.zeros((w_in.shape[0], LANES - 8), w_in.dtype)
    wf = jnp.concatenate([mi, mf, pad], axis=1).astype(BF16)
    return wb, wf


def _rope_tables(S):
    half = LANES // 2
    inv = jnp.power(ROPE_BASE, -jnp.arange(half, dtype=F32) / half)
    ang = jnp.arange(S, dtype=F32)[:, None] * inv[None, :]
    cos = jnp.cos(ang)
    sin = jnp.sin(ang)
    return jnp.concatenate([cos, cos], axis=1), jnp.concatenate([-sin, sin], axis=1)


def kernel(x, norm1_w, w_in, ret_norm_w, da_q_norm_w, da_k_norm_w, ml_conv_w, ml_i_bias, ml_f_bias,
           ml_norm_w, w_br_ret, w_br_da, w_br_ml, w_out, norm2_w, ffn_w_gate, ffn_w_up, ffn_w_down,
           moe_w_router, moe_b_router, moe_w_gate, moe_w_up, moe_w_down):
    B, S, D = x.shape
    T = B * S
    depth = w_in.shape[0]
    cos, sin = _rope_tables(S)
    xt = x.reshape(T, D)
    for layer in range(depth):
        wb, wf = _split_w_in(w_in[layer])
        nw1 = norm1_w[layer].reshape(1, D)
        zb, zf = _inproj(xt, nw1, wb, wf, 2048, 1280, LANES)
        o_ret = _retention(zb, cos, sin, ret_norm_w[layer].reshape(1, -1), B, S)
        wq = jnp.tile(da_q_norm_w[layer], 2).reshape(1, LANES)
        wk = jnp.tile(da_k_norm_w[layer], 2).reshape(1, LANES)
        o_da = _dilated_attention(zb, 52, wq, wk, B, S)
        gate_bias = jnp.concatenate([ml_i_bias[layer], ml_f_bias[layer],
                                     jnp.zeros((LANES - 2 * ML_HEADS,), F32)]).reshape(1, LANES)
        o_ml = _mlstm(zb, zf, ml_conv_w[layer], gate_bias, ml_norm_w[layer].reshape(1, -1), B, S,
                      qk_blk=10, gate_blk=0)
        xt = _merge(xt, o_ret, o_da, o_ml, zb, 2,
                    w_br_ret[layer].astype(BF16), w_br_da[layer].astype(BF16),
                    w_br_ml[layer].astype(BF16), w_out[layer].astype(BF16), 512)
        nw2 = norm2_w[layer].reshape(1, D)
        j = layer // 2
        if layer % 2 == 0:
            xt = _ffn(xt, nw2, ffn_w_gate[j].astype(BF16), ffn_w_up[j].astype(BF16),
                      ffn_w_down[j].astype(BF16), 1024, 1408)
        else:
            xt = _moe(xt, nw2, moe_w_router[j], moe_b_router[j], moe_w_gate[j].astype(BF16),
                      moe_w_up[j].astype(BF16), moe_w_down[j].astype(BF16))
    return xt.reshape(B, S, D)
```

```python
import functools
import math

import jax
import jax.numpy as jnp
from jax import lax
from jax.experimental import pallas as pl
from jax.experimental.pallas import tpu as pltpu
from jax.experimental.pallas import tpu_sc as plsc

F32 = jnp.float32
BF16 = jnp.bfloat16

EPS = 1e-6
D_MODEL = 1024
CHUNK = 128
LANES = 128
MXU_N = 256
ROPE_BASE = 10000.0
RET_HEADS = 4
DA_GROUPS = ((1, 16), (4, 4), (16, 1))
DA_DH = 64
DA_UNROLL = 4
ML_HEADS = 4
ML_DK = 64
N_EXPERTS = 8
TOP_K = 2
MOE_TILE = 512
SC_WINDOW = 128
VMEM_LIMIT = 56 * 1024 * 1024

NEG = -1e30


def _cparams(*sem):
    return pltpu.CompilerParams(dimension_semantics=sem, vmem_limit_bytes=VMEM_LIMIT)


def _dot(a, b):
    return jnp.dot(a, b, preferred_element_type=F32)


def _dot_nt(a, b):
    return lax.dot_general(a, b, (((1,), (1,)), ((), ())), preferred_element_type=F32)


def _dot_tn(a, b):
    return lax.dot_general(a, b, (((0,), (0,)), ((), ())), preferred_element_type=F32)


def _sigmoid(x):
    return 1.0 / (1.0 + jnp.exp(-x))


def _silu(x):
    return x * _sigmoid(x)


def _inproj_kernel(x_ref, nw_ref, wb_ref, wf_ref, zb_ref, zf_ref, h_ref):
    j = pl.program_id(1)

    @pl.when(j == 0)
    def _():
        rc = 512
        for r in range(x_ref.shape[0] // rc):
            rows = pl.ds(r * rc, rc)
            x = x_ref[rows, :]
            ms = jnp.mean(x * x, axis=-1, keepdims=True)
            h = (x * lax.rsqrt(ms + EPS) * nw_ref[...]).astype(BF16)
            h_ref[rows, :] = h
            zb_ref[rows, :] = _dot(h, wb_ref[...]).astype(zb_ref.dtype)
            zf_ref[rows, :] = _dot(h, wf_ref[...])

    @pl.when(j > 0)
    def _():
        zb_ref[...] = _dot(h_ref[...], wb_ref[...]).astype(zb_ref.dtype)


def _inproj(x, nw, wb, wf, tm, tnb):
    T, D = x.shape
    return pl.pallas_call(
        _inproj_kernel,
        grid=(T // tm, wb.shape[1] // tnb),
        in_specs=[pl.BlockSpec((tm, D), lambda i, j: (i, 0)),
                  pl.BlockSpec((1, D), lambda i, j: (0, 0)),
                  pl.BlockSpec((D, tnb), lambda i, j: (0, j)),
                  pl.BlockSpec(wf.shape, lambda i, j: (0, 0))],
        out_specs=[pl.BlockSpec((tm, tnb), lambda i, j: (i, j)),
                   pl.BlockSpec((tm, wf.shape[1]), lambda i, j: (i, 0))],
        out_shape=[jax.ShapeDtypeStruct((T, wb.shape[1]), BF16),
                   jax.ShapeDtypeStruct((T, wf.shape[1]), F32)],
        scratch_shapes=[pltpu.VMEM((tm, D), BF16)],
        compiler_params=_cparams("parallel", "arbitrary"),
        name="inproj",
    )(x, nw, wb, wf)


def _retention_kernel(z_ref, cos_ref, sin_ref, nw_ref, o_ref, dec_ref, st_ref):
    S = z_ref.shape[0]
    n_chunks = S // CHUNK
    H = RET_HEADS
    HW = H * LANES
    row = lax.broadcasted_iota(jnp.int32, (CHUNK, CHUNK), 0).astype(F32)
    col = lax.broadcasted_iota(jnp.int32, (CHUNK, CHUNK), 1).astype(F32)
    lgs = [math.log1p(-(2.0 ** (-5.0 - h))) for h in range(H)]
    for h, lg in enumerate(lgs):
        rel = row - col
        dec_ref[h] = jnp.where(rel >= 0, jnp.exp(lg * jnp.maximum(rel, 0.0)), 0.0)
        dec_ref[H + h] = jnp.exp(lg * (row + 1.0))
        dec_ref[2 * H + h] = jnp.exp(lg * (CHUNK - 1.0 - row)) * (LANES ** -0.5)
    st_ref[...] = jnp.zeros(st_ref.shape, F32)

    def body(n, carry):
        r0 = pl.multiple_of(n * CHUNK, CHUNK)
        rows = pl.ds(r0, CHUNK)
        cos = cos_ref[rows, :]
        sin = sin_ref[rows, :]
        qbs, vs, ss, iqs, kvs = [], [], [], [], []
        for h in range(H):
            q = z_ref[rows, h * LANES:(h + 1) * LANES].astype(F32)
            k = z_ref[rows, HW + h * LANES:HW + (h + 1) * LANES].astype(F32)
            v = z_ref[rows, 2 * HW + h * LANES:2 * HW + (h + 1) * LANES]
            q = q * cos + pltpu.roll(q, LANES // 2, 1) * sin
            k = k * cos + pltpu.roll(k, LANES // 2, 1) * sin
            qb = q.astype(BF16)
            kb = (k * (LANES ** -0.5)).astype(BF16)
            kd = (k * dec_ref[2 * H + h]).astype(BF16)
            ss.append(_dot_nt(qb, kb))
            iqs.append(_dot(qb, st_ref[h].astype(BF16)))
            kvs.append(_dot_tn(kd, v))
            vs.append(v)
        ps = [(ss[h] * dec_ref[h]).astype(BF16) for h in range(H)]
        os_ = [_dot(ps[h], vs[h]) + iqs[h] * dec_ref[H + h] for h in range(H)]
        for h in range(H):
            st_ref[h] = st_ref[h] * math.exp(lgs[h] * CHUNK) + kvs[h]
            o = os_[h]
            ms = jnp.mean(o * o, axis=-1, keepdims=True)
            o = o * lax.rsqrt(ms + EPS) * nw_ref[:, h * LANES:(h + 1) * LANES]
            g = z_ref[rows, 3 * HW + h * LANES:3 * HW + (h + 1) * LANES].astype(F32)
            o_ref[rows, h * LANES:(h + 1) * LANES] = (o * _silu(g)).astype(o_ref.dtype)
        return carry

    lax.fori_loop(0, n_chunks, body, 0)


def _retention(zb, cos, sin, nw, B, S):
    T = B * S
    W = RET_HEADS * LANES
    return pl.pallas_call(
        _retention_kernel,
        grid=(B,),
        in_specs=[pl.BlockSpec((S, 4 * W), lambda b: (b, 0)),
                  pl.BlockSpec((S, LANES), lambda b: (0, 0)),
                  pl.BlockSpec((S, LANES), lambda b: (0, 0)),
                  pl.BlockSpec((1, W), lambda b: (0, 0))],
        out_specs=pl.BlockSpec((S, W), lambda b: (b, 0)),
        out_shape=jax.ShapeDtypeStruct((T, W), BF16),
        scratch_shapes=[pltpu.VMEM((3 * RET_HEADS, CHUNK, LANES), F32),
                        pltpu.VMEM((RET_HEADS, LANES, LANES), F32)],
        compiler_params=_cparams("parallel"),
        name="retention",
    )(zb, cos, sin, nw)


def _da_kernel(*refs):
    z_refs = refs[:9]
    wq_ref, wk_ref, o_ref, qn_ref, kn_ref, v_ref, np_ref, lp_ref, mp_ref = refs[9:18]
    nn_refs, ln_refs, mn_refs = refs[18:21], refs[21:24], refs[24:27]
    stage_refs = refs[27:30]
    S = o_ref.shape[0]
    lane = lax.broadcasted_iota(jnp.int32, (1, LANES), 1)
    lo = lane < DA_DH
    row = lax.broadcasted_iota(jnp.int32, (CHUNK, CHUNK), 0)
    col = lax.broadcasted_iota(jnp.int32, (CHUNK, CHUNK), 1)
    mask_cur = col <= row
    mask_prev = col >= row

    seg = (lax.broadcasted_iota(jnp.int32, (LANES, LANES), 0) // DA_DH
           == lax.broadcasted_iota(jnp.int32, (LANES, LANES), 1) // DA_DH).astype(BF16) * (1.0 / DA_DH)

    def head_norm(x, w):
        x2 = x * x
        hi = x2.astype(BF16)
        lo_part = (x2 - hi.astype(F32)).astype(BF16)
        ms = _dot(hi, seg) + _dot(lo_part, seg)
        return x * lax.rsqrt(ms + EPS) * w

    RB = 256
    for g, (dil, nb) in enumerate(DA_GROUPS):
        L = S // dil
        zq_ref, zk_ref, zv_ref = z_refs[3 * g:3 * g + 3]
        if dil > 1:
            for piece in range(S // RB):
                rows = pl.ds(piece * RB, RB)
                for src_ref, dst_ref in zip((zq_ref, zk_ref, zv_ref), stage_refs):
                    dst_ref[rows, :] = src_ref[rows, :].astype(F32)
            zq_ref, zk_ref, zv_ref = stage_refs
        for rho in range(dil):
            for piece in range(max(L // RB, 1)):
                n_rows = min(RB, L)
                src = pl.ds(rho + dil * piece * n_rows, n_rows, stride=dil) if dil > 1 else pl.ds(
                    piece * n_rows, n_rows)
                dst = pl.ds(rho * L + piece * n_rows, n_rows)
                q = zq_ref[src, :].astype(F32)
                k = zk_ref[src, :].astype(F32)
                v = zv_ref[src, :]
                qn_ref[dst, :] = (head_norm(q, wq_ref[...]) * (DA_DH ** -0.5)).astype(BF16)
                kn_ref[dst, :] = head_norm(k, wk_ref[...]).astype(BF16)
                v_ref[dst, :] = v.astype(BF16)

        def batch(r0, chained, first_has_prev):
            def kv(rows):
                k = kn_ref[rows, :]
                v = v_ref[rows, :]
                one = jnp.ones_like(v)
                return k, (jnp.where(lo, v, one), jnp.where(lo, one, v))

            prev = kv(pl.ds(r0 - CHUNK, CHUNK)) if first_has_prev else None
            chains = []
            for u in range(DA_UNROLL):
                rows = pl.ds(r0 + u * CHUNK, CHUNK)
                cur = kv(rows)
                p = prev if (chained and (u > 0 or first_has_prev)) else None
                q = qn_ref[rows, :]
                zero = jnp.zeros_like(q)
                for hs, qh in enumerate((jnp.where(lo, q, zero), jnp.where(lo, zero, q))):
                    sc = _dot_nt(qh, cur[0])
                    sp = None if p is None else _dot_nt(qh, p[0])
                    chains.append((sc, sp, cur[1][hs], None if p is None else p[1][hs]))
                prev = cur
            probs = []
            for sc, sp, _, _ in chains:
                sc = jnp.where(mask_cur, sc, NEG)
                m = jnp.max(sc, axis=-1, keepdims=True)
                if sp is not None:
                    sp = jnp.where(mask_prev, sp, NEG)
                    m = jnp.maximum(m, jnp.max(sp, axis=-1, keepdims=True))
                    sp = jnp.exp(sp - m).astype(BF16)
                probs.append((jnp.exp(sc - m).astype(BF16), sp, m))
            accs = []
            for (pc, pp, _), (_, _, vc, vp) in zip(probs, chains):
                acc = _dot(pc, vc)
                if pp is not None:
                    acc = acc + _dot(pp, vp)
                accs.append(acc)
            nums, dens, maxs = [], [], []
            for u in range(DA_UNROLL):
                a0, a1 = accs[2 * u], accs[2 * u + 1]
                nums.append(jnp.where(lo, a0, a1))
                dens.append(jnp.where(lo, pltpu.roll(a0, DA_DH, 1), pltpu.roll(a1, DA_DH, 1)))
                maxs.append(jnp.where(lo, probs[2 * u][2], probs[2 * u + 1][2]))
            rows = pl.ds(r0, DA_UNROLL * CHUNK)
            np_ref[rows, :] = jnp.concatenate(nums, axis=0)
            lp_ref[rows, :] = jnp.concatenate(dens, axis=0)
            mp_ref[rows, :] = jnp.concatenate(maxs, axis=0)

        span = DA_UNROLL * CHUNK
        if nb == 1:
            def singles(i, c):
                batch(pl.multiple_of(i * span, span), False, False)
                return c
            lax.fori_loop(0, S // span, singles, 0)
        else:
            def segment(sgi, c, nb=nb):
                base = pl.multiple_of(sgi * (nb * CHUNK), span)
                batch(base, True, False)
                if nb > DA_UNROLL:
                    def inner(n, c2):
                        batch(pl.multiple_of(base + n * span, span), True, True)
                        return c2
                    lax.fori_loop(1, nb // DA_UNROLL, inner, 0)
                return c
            lax.fori_loop(0, dil, segment, 0)

        for rho in range(dil):
            for piece in range(max(L // RB, 1)):
                n_rows = min(RB, L)
                dst = pl.ds(rho + dil * piece * n_rows, n_rows, stride=dil) if dil > 1 else pl.ds(
                    piece * n_rows, n_rows)
                src = pl.ds(rho * L + piece * n_rows, n_rows)
                nn_refs[g][dst, :] = np_ref[src, :]
                ln_refs[g][dst, :] = lp_ref[src, :]
                mn_refs[g][dst, :] = mp_ref[src, :]

    for piece in range(S // RB):
        rows = pl.ds(piece * RB, RB)
        m = jnp.maximum(jnp.maximum(mn_refs[0][rows, :], mn_refs[1][rows, :]), mn_refs[2][rows, :])
        num = jnp.zeros((RB, LANES), F32)
        den = jnp.zeros((RB, LANES), F32)
        for g in range(3):
            e = jnp.exp(mn_refs[g][rows, :] - m)
            num = num + e * nn_refs[g][rows, :]
            den = den + e * ln_refs[g][rows, :]
        o_ref[rows, :] = (num / den).astype(o_ref.dtype)


def _dilated_attention(zb, slab0, wq, wk, B, S):
    T = B * S
    slab = lambda k: pl.BlockSpec((S, LANES), lambda b, p: (b, slab0 + 9 * p + k))
    return pl.pallas_call(
        _da_kernel,
        grid=(B, 2),
        in_specs=[slab(k) for k in range(9)] + [
                  pl.BlockSpec((1, LANES), lambda b, p: (0, 0)),
                  pl.BlockSpec((1, LANES), lambda b, p: (0, 0))],
        out_specs=pl.BlockSpec((S, LANES), lambda b, p: (b, p)),
        out_shape=jax.ShapeDtypeStruct((T, 2 * LANES), BF16),
        scratch_shapes=[pltpu.VMEM((S, LANES), BF16), pltpu.VMEM((S, LANES), BF16),
                        pltpu.VMEM((S, LANES), BF16),
                        pltpu.VMEM((S, LANES), F32), pltpu.VMEM((S, LANES), F32),
                        pltpu.VMEM((S, LANES), F32)] + [pltpu.VMEM((S, LANES), F32)] * 12,
        compiler_params=_cparams("parallel", "arbitrary"),
        name="dilated_attention",
    )(*([zb] * 9), wq, wk)


def _mlstm_kernel(qk_ref, v_ref, og_ref, gate_ref, cw_ref, gb_ref, nw_ref, o_ref,
                  xp_ref, qc_ref, st_ref, e_ref, et_ref):
    S = qk_ref.shape[0]
    n_chunks = S // CHUNK
    W = qk_ref.shape[1]
    PAD = 8
    xp_ref[0:PAD, :] = jnp.zeros((PAD, W), F32)
    RB = 256
    for piece in range(S // RB):
        xp_ref[PAD + piece * RB:PAD + (piece + 1) * RB, :] = qk_ref[piece * RB:(piece + 1) * RB, :].astype(F32)
    for piece in range(S // RB):
        for cs in range(W // LANES):
            cols = slice(cs * LANES, (cs + 1) * LANES)
            acc = cw_ref[0:1, cols] * xp_ref[pl.ds(PAD + piece * RB - 3, RB), cols]
            for i in range(1, 4):
                acc = acc + cw_ref[i:i + 1, cols] * xp_ref[pl.ds(PAD + piece * RB - 3 + i, RB), cols]
            scale = ML_DK ** -0.5 if cs >= W // LANES // 2 else 1.0
            qc_ref[piece * RB:(piece + 1) * RB, cols] = (_silu(acc) * scale).astype(BF16)

    lane = lax.broadcasted_iota(jnp.int32, (1, LANES), 1)
    lo = lane < ML_DK
    row = lax.broadcasted_iota(jnp.int32, (CHUNK, CHUNK), 0)
    col = lax.broadcasted_iota(jnp.int32, (CHUNK, CHUNK), 1)
    causal = col <= row
    tril = causal.astype(F32)
    srow = lax.broadcasted_iota(jnp.int32, (LANES, 1), 0) < ML_DK
    st_ref[...] = jnp.zeros(st_ref.shape, F32)

    is_f = (lane >= ML_HEADS) & (lane < 2 * ML_HEADS)
    for n in range(n_chunks):
        rows = pl.ds(n * CHUNK, CHUNK)
        gp = gate_ref[rows, :] + gb_ref[...]
        logf = jnp.minimum(gp, 0.0) - jnp.log(1.0 + jnp.exp(-jnp.abs(gp)))
        cum = jnp.dot(tril, jnp.where(is_f, logf, 0.0), preferred_element_type=F32,
                      precision=lax.Precision.HIGHEST)
        e = jnp.where(lane < ML_HEADS, gp, cum)
        e_ref[rows, :] = e
        et_ref[n] = e.T[0:2 * ML_HEADS, :]

    def chunk(n, mms):
        r0 = pl.multiple_of(n * CHUNK, CHUNK)
        rows = pl.ds(r0, CHUNK)
        e = e_ref[rows, :]
        et = et_ref[n]
        qhs, khs, vaugs, s_raws, iqs = [], [], [], [], []
        for pair in range(2):
            q2 = qc_ref[rows, pair * LANES:(pair + 1) * LANES]
            k2 = qc_ref[rows, W // 2 + pair * LANES:W // 2 + (pair + 1) * LANES]
            zero = jnp.zeros_like(q2)
            stb = st_ref[pair].astype(BF16)
            for hs in range(2):
                h = 2 * pair + hs
                qh = jnp.where(lo, q2, zero) if hs == 0 else jnp.where(lo, zero, q2)
                kh = jnp.where(lo, k2, zero) if hs == 0 else jnp.where(lo, zero, k2)
                vh = v_ref[rows, h * LANES:(h + 1) * LANES]
                s_raws.append(_dot_nt(qh, k2))
                iqs.append(_dot(qh, stb))
                khs.append(kh)
                vaugs.append(jnp.concatenate([vh, jnp.ones_like(vh)], axis=-1))
        ps, wis, ms_, kws, wcs, new_mms = [], [], [], [], [], []
        for h in range(ML_HEADS):
            mm = mms[h]
            i_b = jnp.broadcast_to(e[:, h:h + 1], (CHUNK, LANES))
            a_b = jnp.broadcast_to(e[:, ML_HEADS + h:ML_HEADS + h + 1], (CHUNK, LANES))
            c_row = et[h:h + 1, :] - et[ML_HEADS + h:ML_HEADS + h + 1, :]
            dmat = jnp.where(causal, a_b + c_row, NEG)
            inter = a_b + mm
            m = jnp.maximum(inter, jnp.max(dmat, axis=-1, keepdims=True))
            ps.append((s_raws[h] * jnp.exp(dmat - m)).astype(BF16))
            wis.append(jnp.exp(inter - m))
            ms_.append(m)
            m_new = m[CHUNK - 1:CHUNK, :]
            a_last = a_b[CHUNK - 1:CHUNK, :]
            wk = jnp.exp(a_last - a_b + i_b - m_new)
            kws.append((khs[h].astype(F32) * wk).astype(BF16))
            wc = jnp.exp(a_last + mm - m_new)
            wcs.append(jnp.concatenate([wc, wc], axis=1))
            new_mms.append(m_new)
        accs = [_dot(ps[h], vaugs[h]) + jnp.concatenate([wis[h], wis[h]], axis=1) * iqs[h]
                for h in range(ML_HEADS)]
        upds = [_dot_tn(kws[h], vaugs[h]) for h in range(ML_HEADS)]
        for pair in range(2):
            h0, h1 = 2 * pair, 2 * pair + 1
            st_ref[pair] = jnp.where(srow, wcs[h0], wcs[h1]) * st_ref[pair] + (upds[h0] + upds[h1])
        for h in range(ML_HEADS):
            num = accs[h][:, :LANES]
            den = accs[h][:, LANES:]
            hv = num / jnp.maximum(jnp.abs(den), jnp.exp(-ms_[h]))
            var = jnp.mean(hv * hv, axis=-1, keepdims=True)
            hv = hv * lax.rsqrt(var + EPS) * nw_ref[:, h * LANES:(h + 1) * LANES]
            og = og_ref[rows, h * LANES:(h + 1) * LANES].astype(F32)
            o_ref[rows, h * LANES:(h + 1) * LANES] = (hv * _sigmoid(og)).astype(o_ref.dtype)
        return tuple(new_mms)

    lax.fori_loop(0, n_chunks, chunk, tuple(jnp.zeros((1, LANES), F32) for _ in range(ML_HEADS)))


def _mlstm(zb, zf, conv_w, gate_bias, nw, B, S, qk_blk, gate_blk):
    T = B * S
    W = ML_HEADS * LANES
    return pl.pallas_call(
        _mlstm_kernel,
        grid=(B,),
        in_specs=[pl.BlockSpec((S, W), lambda b: (b, qk_blk)),
                  pl.BlockSpec((S, W), lambda b: (b, qk_blk + 1)),
                  pl.BlockSpec((S, W), lambda b: (b, qk_blk + 2)),
                  pl.BlockSpec((S, LANES), lambda b: (b, gate_blk)),
                  pl.BlockSpec((4, W), lambda b: (0, 0)),
                  pl.BlockSpec((1, LANES), lambda b: (0, 0)),
                  pl.BlockSpec((1, W), lambda b: (0, 0))],
        out_specs=pl.BlockSpec((S, W), lambda b: (b, 0)),
        out_shape=jax.ShapeDtypeStruct((T, W), BF16),
        scratch_shapes=[pltpu.VMEM((S + 8, W), F32), pltpu.VMEM((S, W), BF16),
                        pltpu.VMEM((2, LANES, 2 * LANES), F32),
                        pltpu.VMEM((S, LANES), F32),
                        pltpu.VMEM((S // CHUNK, 2 * ML_HEADS, LANES), F32)],
        compiler_params=_cparams("parallel"),
        name="mlstm",
    )(zb, zb, zb, zf, conv_w, gate_bias, nw)


def _merge_kernel(x_ref, oret_ref, oda_ref, oml_ref, gr_ref, gd_ref, gm_ref,
                  wr_ref, wd_ref, wm_ref, wo_ref, o_ref):
    y = _sigmoid(gr_ref[...].astype(F32)) * _dot(oret_ref[...], wr_ref[...])
    y = y + _sigmoid(gd_ref[...].astype(F32)) * _dot(oda_ref[...], wd_ref[...])
    y = y + _sigmoid(gm_ref[...].astype(F32)) * _dot(oml_ref[...], wm_ref[...])
    o_ref[...] = x_ref[...] + _dot(y.astype(BF16), wo_ref[...])


def _merge(x, o_ret, o_da, o_ml, zb, g_blk, wr, wd, wm, wo, tm):
    T, D = x.shape
    full = lambda a: pl.BlockSpec(a.shape, lambda i: (0, 0))
    return pl.pallas_call(
        _merge_kernel,
        grid=(T // tm,),
        in_specs=[pl.BlockSpec((tm, D), lambda i: (i, 0)),
                  pl.BlockSpec((tm, o_ret.shape[1]), lambda i: (i, 0)),
                  pl.BlockSpec((tm, o_da.shape[1]), lambda i: (i, 0)),
                  pl.BlockSpec((tm, o_ml.shape[1]), lambda i: (i, 0)),
                  pl.BlockSpec((tm, D), lambda i: (i, g_blk)),
                  pl.BlockSpec((tm, D), lambda i: (i, g_blk + 1)),
                  pl.BlockSpec((tm, D), lambda i: (i, g_blk + 2)),
                  full(wr), full(wd), full(wm), full(wo)],
        out_specs=pl.BlockSpec((tm, D), lambda i: (i, 0)),
        out_shape=jax.ShapeDtypeStruct((T, D), F32),
        compiler_params=_cparams("parallel"),
        name="merge_outproj",
    )(x, o_ret, o_da, o_ml, zb, zb, zb, wr, wd, wm, wo)


def _swiglu_block(h, wg_ref, wu_ref, wd_ref, a_ref):
    tf = wg_ref.shape[1]
    c0 = 0
    while c0 < tf:
        w = min(MXU_N, tf - c0)
        g = _dot(h, wg_ref[:, c0:c0 + w].astype(BF16))
        u = _dot(h, wu_ref[:, c0:c0 + w].astype(BF16))
        a_ref[:, c0:c0 + w] = (_silu(g) * u).astype(BF16)
        c0 += w
    return _dot(a_ref[...], wd_ref[...].astype(BF16))


def _ffn_kernel(x_ref, nw_ref, wg_ref, wu_ref, wd_ref, o_ref, h_ref, acc_ref, a_ref):
    j = pl.program_id(1)

    @pl.when(j == 0)
    def _():
        x = x_ref[...]
        ms = jnp.mean(x * x, axis=-1, keepdims=True)
        h_ref[...] = (x * lax.rsqrt(ms + EPS) * nw_ref[...]).astype(BF16)
        acc_ref[...] = x

    acc_ref[...] += _swiglu_block(h_ref[...], wg_ref, wu_ref, wd_ref, a_ref)

    @pl.when(j == pl.num_programs(1) - 1)
    def _():
        o_ref[...] = acc_ref[...]


def _ffn(x, nw, wg, wu, wd, tm, tf):
    T, D = x.shape
    F = wg.shape[1]
    return pl.pallas_call(
        _ffn_kernel,
        grid=(T // tm, F // tf),
        in_specs=[pl.BlockSpec((tm, D), lambda i, j: (i, 0)),
                  pl.BlockSpec((1, D), lambda i, j: (0, 0)),
                  pl.BlockSpec((D, tf), lambda i, j: (0, j)),
                  pl.BlockSpec((D, tf), lambda i, j: (0, j)),
                  pl.BlockSpec((tf, D), lambda i, j: (j, 0))],
        out_specs=pl.BlockSpec((tm, D), lambda i, j: (i, 0)),
        out_shape=jax.ShapeDtypeStruct((T, D), F32),
        scratch_shapes=[pltpu.VMEM((tm, D), BF16), pltpu.VMEM((tm, D), F32),
                        pltpu.VMEM((tm, tf), BF16)],
        compiler_params=_cparams("parallel", "arbitrary"),
        name="swiglu_ffn",
    )(x, nw, wg, wu, wd)


HI16 = 0xFFFF0000


def _pack_bf16_pairs(a, b):
    ab = pltpu.bitcast(a.astype(BF16).astype(F32), jnp.uint32)
    bb = pltpu.bitcast(b.astype(BF16).astype(F32), jnp.uint32)
    return (ab >> 16) | (bb & jnp.uint32(HI16))


def _unpack_bf16_pairs(p):
    lo = pltpu.bitcast(p << 16, F32)
    hi = pltpu.bitcast(p & jnp.uint32(HI16), F32)
    return lo, hi


def _router_kernel(x_ref, nw_ref, wr_ref, br_ref, ha_ref, hb_ref, sel_ref):
    x = x_ref[...]
    ms = jnp.mean(x * x, axis=-1, keepdims=True)
    h = x * lax.rsqrt(ms + EPS) * nw_ref[...]
    Q = h.shape[1] // 4
    ha_ref[...] = _pack_bf16_pairs(h[:, 0:Q], h[:, Q:2 * Q])
    hb_ref[...] = _pack_bf16_pairs(h[:, 2 * Q:3 * Q], h[:, 3 * Q:4 * Q])
    logits = jnp.dot(h, wr_ref[...], preferred_element_type=F32,
                     precision=lax.Precision.HIGHEST) + br_ref[...]
    lane = lax.broadcasted_iota(jnp.int32, logits.shape, 1).astype(F32)
    logits = jnp.where(lane < N_EXPERTS, logits, NEG)
    m1 = jnp.max(logits, axis=-1, keepdims=True)
    i1 = jnp.min(jnp.where(logits == m1, lane, float(LANES)), axis=-1, keepdims=True)
    rest = jnp.where(lane == i1, NEG, logits)
    m2 = jnp.max(rest, axis=-1, keepdims=True)
    i2 = jnp.min(jnp.where(rest == m2, lane, float(LANES)), axis=-1, keepdims=True)
    e2 = jnp.exp(m2 - m1)
    p1 = 1.0 / (1.0 + e2)
    p2 = e2 / (1.0 + e2)
    sel_ref[...] = jnp.where(lane == 0.0, i1, jnp.where(lane == 1.0, i2,
                             jnp.where(lane == 2.0, p1, jnp.where(lane == 3.0, p2, 0.0))))


def _router(x, nw, wr, br, tm):
    T, D = x.shape
    Q = D // 4
    return pl.pallas_call(
        _router_kernel,
        grid=(T // tm,),
        in_specs=[pl.BlockSpec((tm, D), lambda i: (i, 0)),
                  pl.BlockSpec((1, D), lambda i: (0, 0)),
                  pl.BlockSpec((D, LANES), lambda i: (0, 0)),
                  pl.BlockSpec((1, LANES), lambda i: (0, 0))],
        out_specs=[pl.BlockSpec((tm, Q), lambda i: (i, 0)),
                   pl.BlockSpec((tm, Q), lambda i: (i, 0)),
                   pl.BlockSpec((tm, LANES), lambda i: (i, 0))],
        out_shape=[jax.ShapeDtypeStruct((T, Q), jnp.uint32), jax.ShapeDtypeStruct((T, Q), jnp.uint32),
                   jax.ShapeDtypeStruct((T, LANES), F32)],
        compiler_params=_cparams("parallel"),
        name="moe_router",
    )(x, nw, wr, br)


def _moe_rank_kernel(sel_ref, pos_ref, meta_ref, cnt_ref, offs_ref, carry_ref, before_ref):
    ph = pl.program_id(0)
    i = pl.program_id(1)
    tm = sel_ref.shape[0]
    lane = lax.broadcasted_iota(jnp.int32, (tm, LANES), 1).astype(F32)
    lane1 = lax.broadcasted_iota(jnp.int32, (1, LANES), 1).astype(F32)
    sel = sel_ref[...]
    i1 = sel[:, 0:1]
    i2 = sel[:, 1:2]
    onehot = jnp.where((lane == i1) | (lane == i2), 1.0, 0.0)
    colsum = jnp.sum(onehot, axis=0, keepdims=True)

    @pl.when((ph == 0) & (i == 0))
    def _():
        cnt_ref[...] = jnp.zeros(cnt_ref.shape, F32)

    @pl.when(ph == 0)
    def _():
        cnt_ref[...] += colsum

    def padded_counts():
        return jnp.floor((cnt_ref[...] + (MOE_TILE - 1.0)) * (1.0 / MOE_TILE)) * MOE_TILE

    @pl.when((ph == 1) & (i == 0))
    def _():
        k = lax.broadcasted_iota(jnp.int32, (LANES, LANES), 0)
        e = lax.broadcasted_iota(jnp.int32, (LANES, LANES), 1)
        upper = (k < e).astype(F32)
        offs_ref[...] = jnp.dot(padded_counts(), upper, preferred_element_type=F32,
                                precision=lax.Precision.HIGHEST)
        carry_ref[...] = jnp.zeros(carry_ref.shape, F32)
        r = lax.broadcasted_iota(jnp.int32, (tm, tm), 0)
        c = lax.broadcasted_iota(jnp.int32, (tm, tm), 1)
        before_ref[...] = (c < r).astype(BF16)

    @pl.when(ph == 1)
    def _():
        rank = _dot(before_ref[...], onehot.astype(BF16)) + carry_ref[0:1, :]
        row = rank + offs_ref[0:1, :]
        pos1 = jnp.sum(jnp.where(lane == i1, row, 0.0), axis=-1, keepdims=True)
        pos2 = jnp.sum(jnp.where(lane == i2, row, 0.0), axis=-1, keepdims=True)
        both = jnp.where(lane == 0.0, pos1, jnp.where(lane == 1.0, pos2, 0.0))
        pos_ref[...] = both.T[0:8, :].astype(jnp.int32)
        carry_ref[...] += colsum

    @pl.when((ph == 1) & (i == pl.num_programs(1) - 1))
    def _():
        nrow = meta_ref.shape[0]
        padded = padded_counts()[0:1, :]
        offs = offs_ref[0:1, :]
        ends = offs + padded
        start = lax.broadcasted_iota(jnp.int32, (nrow, 1), 0).astype(F32) * MOE_TILE
        is_e = lane1 < N_EXPERTS
        te = jnp.sum(jnp.where(is_e & (ends <= start), 1.0, 0.0), axis=-1, keepdims=True)
        te = jnp.minimum(te, N_EXPERTS - 1.0)
        valid_end = jnp.sum(jnp.where(lane1 == te, offs + cnt_ref[0:1, :], 0.0), axis=-1, keepdims=True)
        nv = jnp.clip(valid_end - start, 0.0, MOE_TILE)
        total = jnp.sum(jnp.where(is_e, padded, 0.0), axis=-1, keepdims=True)
        active = jnp.where(start < total, 1.0, 0.0)
        meta_ref[...] = jnp.where(lane1 == 0.0, te, jnp.where(lane1 == 1.0, nv,
                                  jnp.where(lane1 == 2.0, active, 0.0)))


def _moe_rank(sel, tm, meta_rows):
    T = sel.shape[0]
    nt = T // tm
    return pl.pallas_call(
        _moe_rank_kernel,
        grid=(2, nt),
        in_specs=[pl.BlockSpec((tm, LANES), lambda ph, i: (i, 0))],
        out_specs=[pl.BlockSpec((None, 8, tm), lambda ph, i: (i * ph, 0, 0)),
                   pl.BlockSpec((meta_rows, LANES), lambda ph, i: (0, 0))],
        out_shape=[jax.ShapeDtypeStruct((nt, 8, tm), jnp.int32),
                   jax.ShapeDtypeStruct((meta_rows, LANES), F32)],
        scratch_shapes=[pltpu.VMEM((8, LANES), F32), pltpu.VMEM((8, LANES), F32),
                        pltpu.VMEM((8, LANES), F32), pltpu.VMEM((tm, tm), BF16)],
        compiler_params=_cparams("arbitrary", "arbitrary"),
        name="moe_rank",
    )(sel)


def _sc_mesh():
    return plsc.VectorSubcoreMesh(core_axis_name="core", subcore_axis_name="subcore")


def _sc_scatter_rows(x, idx, n_rows):
    n_idx = idx.shape[1]
    nt = x.shape[0] // SC_WINDOW
    width = x.shape[1]

    @functools.partial(pl.kernel, out_type=jax.ShapeDtypeStruct((n_rows, width), x.dtype),
                       mesh=_sc_mesh())
    def scatter_kernel(x_hbm, i_hbm, o_hbm):
        def body(x_vmem, i_vmem):
            pltpu.sync_copy(x_vmem, o_hbm.at[i_vmem.at[0]])

        pltpu.emit_pipeline(
            body, grid=(n_idx // SC_WINDOW,),
            in_specs=[pl.BlockSpec((SC_WINDOW, width), lambda i: (i % nt, 0)),
                      pl.BlockSpec((1, SC_WINDOW), lambda i: (0, i))],
            out_specs=[],
            core_axis_name=("core", "subcore"), dimension_semantics=(pltpu.PARALLEL,),
        )(x_hbm, i_hbm)

    return scatter_kernel(x, idx)


def _sc_gather_rows(x, idx):
    n_idx = idx.shape[1]
    width = x.shape[1]

    @functools.partial(pl.kernel, out_type=jax.ShapeDtypeStruct((n_idx, width), x.dtype),
                       mesh=_sc_mesh())
    def gather_kernel(x_hbm, i_hbm, o_hbm):
        def body(i_vmem, o_vmem):
            pltpu.sync_copy(x_hbm.at[i_vmem.at[0]], o_vmem)

        pltpu.emit_pipeline(
            body, grid=(n_idx // SC_WINDOW,),
            in_specs=[pl.BlockSpec((1, SC_WINDOW), lambda i: (0, i))],
            out_specs=[pl.BlockSpec((SC_WINDOW, width), lambda i: (i, 0))],
            core_axis_name=("core", "subcore"), dimension_semantics=(pltpu.PARALLEL,),
        )(i_hbm, o_hbm)

    return gather_kernel(x, idx)


def _moe_group_kernel(te_ref, nv_ref, na_ref, xa_ref, xb_ref, wg_ref, wu_ref, wd_ref,
                      ya_ref, yb_ref, h_ref, acc_ref, a_ref):
    j = pl.program_id(0)
    f = pl.program_id(1)

    @pl.when(j < na_ref[0])
    def _():
        @pl.when(f == 0)
        def _():
            tm = h_ref.shape[0]
            Q = xa_ref.shape[1]
            valid = lax.broadcasted_iota(jnp.int32, (tm, 1), 0) < nv_ref[j]
            for src, c0 in ((xa_ref, 0), (xb_ref, 2 * Q)):
                lo, hi = _unpack_bf16_pairs(src[...])
                h_ref[:, c0:c0 + Q] = jnp.where(valid, lo, 0.0).astype(BF16)
                h_ref[:, c0 + Q:c0 + 2 * Q] = jnp.where(valid, hi, 0.0).astype(BF16)
            acc_ref[...] = jnp.zeros(acc_ref.shape, F32)

        acc_ref[...] += _swiglu_block(h_ref[...], wg_ref, wu_ref, wd_ref, a_ref)

        @pl.when(f == pl.num_programs(1) - 1)
        def _():
            Q = ya_ref.shape[1]
            ya_ref[...] = _pack_bf16_pairs(acc_ref[:, 0:Q], acc_ref[:, Q:2 * Q])
            yb_ref[...] = _pack_bf16_pairs(acc_ref[:, 2 * Q:3 * Q], acc_ref[:, 3 * Q:4 * Q])


def _moe_group(te, nv, na, xa, xb, wg, wu, wd, tf):
    R, Q = xa.shape
    E, D, F = wg.shape
    nf = F // tf
    tile = lambda j, f, te, nv, na: (jnp.minimum(j, na[0] - 1), 0)
    ff = lambda j, f, na: jnp.where(j < na[0], f, nf - 1)
    grid_spec = pltpu.PrefetchScalarGridSpec(
        num_scalar_prefetch=3,
        grid=(R // MOE_TILE, nf),
        in_specs=[pl.BlockSpec((MOE_TILE, Q), tile),
                  pl.BlockSpec((MOE_TILE, Q), tile),
                  pl.BlockSpec((None, D, tf), lambda j, f, te, nv, na: (te[j], 0, ff(j, f, na))),
                  pl.BlockSpec((None, D, tf), lambda j, f, te, nv, na: (te[j], 0, ff(j, f, na))),
                  pl.BlockSpec((None, tf, D), lambda j, f, te, nv, na: (te[j], ff(j, f, na), 0))],
        out_specs=[pl.BlockSpec((MOE_TILE, Q), tile), pl.BlockSpec((MOE_TILE, Q), tile)],
        scratch_shapes=[pltpu.VMEM((MOE_TILE, D), BF16), pltpu.VMEM((MOE_TILE, D), F32),
                        pltpu.VMEM((MOE_TILE, tf), BF16)],
    )
    return pl.pallas_call(
        _moe_group_kernel,
        grid_spec=grid_spec,
        out_shape=[jax.ShapeDtypeStruct((R, Q), jnp.uint32), jax.ShapeDtypeStruct((R, Q), jnp.uint32)],
        compiler_params=_cparams("arbitrary", "arbitrary"),
        name="moe_experts",
    )(te, nv, na, xa, xb, wg, wu, wd)


def _moe_combine_kernel(x_ref, sel_ref, a1_ref, a2_ref, b1_ref, b2_ref, o_ref):
    sel = sel_ref[...]
    p1 = sel[:, 2:3]
    p2 = sel[:, 3:4]
    Q = a1_ref.shape[1]
    for first, second, c0 in ((a1_ref, a2_ref, 0), (b1_ref, b2_ref, 2 * Q)):
        lo1, hi1 = _unpack_bf16_pairs(first[...])
        lo2, hi2 = _unpack_bf16_pairs(second[...])
        o_ref[:, c0:c0 + Q] = x_ref[:, c0:c0 + Q] + (p1 * lo1 + p2 * lo2)
        o_ref[:, c0 + Q:c0 + 2 * Q] = x_ref[:, c0 + Q:c0 + 2 * Q] + (p1 * hi1 + p2 * hi2)


def _moe_combine(x, sel, ga, gb, tm):
    T, D = x.shape
    Q = ga.shape[1]
    nt = T // tm
    return pl.pallas_call(
        _moe_combine_kernel,
        grid=(nt,),
        in_specs=[pl.BlockSpec((tm, D), lambda i: (i, 0)),
                  pl.BlockSpec((tm, LANES), lambda i: (i, 0)),
                  pl.BlockSpec((tm, Q), lambda i: (i, 0)),
                  pl.BlockSpec((tm, Q), lambda i: (i + nt, 0)),
                  pl.BlockSpec((tm, Q), lambda i: (i, 0)),
                  pl.BlockSpec((tm, Q), lambda i: (i + nt, 0))],
        out_specs=pl.BlockSpec((tm, D), lambda i: (i, 0)),
        out_shape=jax.ShapeDtypeStruct((T, D), F32),
        compiler_params=_cparams("parallel"),
        name="moe_combine",
    )(x, sel, ga, ga, gb, gb)


def _cast_kernel(x_ref, o_ref):
    o_ref[...] = x_ref[...].astype(o_ref.dtype)


def _cast_bf16(w, rows):
    cols = w.shape[-1]
    w2 = w.reshape(-1, cols)
    out = pl.pallas_call(
        _cast_kernel,
        grid=(w2.shape[0] // rows,),
        in_specs=[pl.BlockSpec((rows, cols), lambda i: (i, 0))],
        out_specs=pl.BlockSpec((rows, cols), lambda i: (i, 0)),
        out_shape=jax.ShapeDtypeStruct(w2.shape, BF16),
        compiler_params=_cparams("parallel"),
        name="cast_bf16",
    )(w2)
    return out.reshape(w.shape)


def _moe(x, nw, w_router, b_router, wg, wu, wd):
    T, D = x.shape
    wr = jnp.pad(w_router, ((0, 0), (0, LANES - N_EXPERTS)))
    br = jnp.pad(b_router, (0, LANES - N_EXPERTS)).reshape(1, LANES)
    ha, hb, sel = _router(x, nw, wr, br, 1024)
    n_rows = TOP_K * T + N_EXPERTS * MOE_TILE
    n_tiles = n_rows // MOE_TILE
    pos, meta = _moe_rank(sel, 1024, 256)
    idx = jnp.concatenate([pos[:, 0, :].reshape(1, T), pos[:, 1, :].reshape(1, T)], axis=1)
    te = meta[:n_tiles, 0].astype(jnp.int32)
    nv = meta[:n_tiles, 1].astype(jnp.int32)
    na = jnp.sum(meta[:n_tiles, 2]).astype(jnp.int32).reshape(1)
    te = jnp.where(jnp.arange(n_tiles) < na[0], te, te[na[0] - 1])
    xa = _sc_scatter_rows(ha, idx, n_rows)
    xb = _sc_scatter_rows(hb, idx, n_rows)
    ya, yb = _moe_group(te, nv, na, xa, xb, wg, wu, wd, wg.shape[2] // 2)
    ga = _sc_gather_rows(ya, idx)
    gb = _sc_gather_rows(yb, idx)
    return _moe_combine(x, sel, ga, gb, 1024)


def _split_w_in(w_in):
    sizes = (512, 512, 512, 512, 768, 768, 768, 256, 256, 512, 512, 4, 4, 1024, 1024, 1024)
    offs = [0]
    for s in sizes:
        offs.append(offs[-1] + s)
    part = lambda i: w_in[..., offs[i]:offs[i + 1]]
    rq, rk, rv, rg, dq, dk, dv, mq, mk, mv, mo, mi, mf, g_ret, g_da, g_ml = (part(i) for i in range(16))
    cols = []
    for p in range(2):
        for g in range(3):
            for t in (dq, dk, dv):
                cols.append(t[..., g * 256 + p * 128:g * 256 + (p + 1) * 128])
    wb = jnp.concatenate([rq, rk, rv, rg, g_ret, g_da, g_ml, mq, mk, mv, mo] + cols,
                         axis=-1).astype(BF16)
    pad = jnp.zeros(w_in.shape[:-1] + (LANES - 8,), w_in.dtype)
    wf = jnp.concatenate([mi, mf, pad], axis=-1).astype(BF16)
    return wb, wf


def _rope_tables(S):
    half = LANES // 2
    inv = jnp.power(ROPE_BASE, -jnp.arange(half, dtype=F32) / half)
    ang = jnp.arange(S, dtype=F32)[:, None] * inv[None, :]
    cos = jnp.cos(ang)
    sin = jnp.sin(ang)
    return jnp.concatenate([cos, cos], axis=1), jnp.concatenate([-sin, sin], axis=1)


def kernel(x, norm1_w, w_in, ret_norm_w, da_q_norm_w, da_k_norm_w, ml_conv_w, ml_i_bias, ml_f_bias,
           ml_norm_w, w_br_ret, w_br_da, w_br_ml, w_out, norm2_w, ffn_w_gate, ffn_w_up, ffn_w_down,
           moe_w_router, moe_b_router, moe_w_gate, moe_w_up, moe_w_down):
    B, S, D = x.shape
    T = B * S
    depth = w_in.shape[0]
    cos, sin = _rope_tables(S)
    wb_all, wf_all = _split_w_in(w_in)
    xt = x.reshape(T, D)
    for layer in range(depth):
        wb, wf = wb_all[layer], wf_all[layer]
        nw1 = norm1_w[layer].reshape(1, D)
        zb, zf = _inproj(xt, nw1, wb, wf, 2048, 1280)
        o_ret = _retention(zb, cos, sin, ret_norm_w[layer].reshape(1, -1), B, S)
        wq = jnp.tile(da_q_norm_w[layer], 2).reshape(1, LANES)
        wk = jnp.tile(da_k_norm_w[layer], 2).reshape(1, LANES)
        o_da = _dilated_attention(zb, 52, wq, wk, B, S)
        gate_bias = jnp.concatenate([ml_i_bias[layer], ml_f_bias[layer],
                                     jnp.zeros((LANES - 2 * ML_HEADS,), F32)]).reshape(1, LANES)
        o_ml = _mlstm(zb, zf, ml_conv_w[layer], gate_bias, ml_norm_w[layer].reshape(1, -1), B, S,
                      qk_blk=10, gate_blk=0)
        xt = _merge(xt, o_ret, o_da, o_ml, zb, 2,
                    w_br_ret[layer].astype(BF16), w_br_da[layer].astype(BF16),
                    w_br_ml[layer].astype(BF16), w_out[layer].astype(BF16), 512)
        nw2 = norm2_w[layer].reshape(1, D)
        j = layer // 2
        if layer % 2 == 0:
            xt = _ffn(xt, nw2, ffn_w_gate[j].astype(BF16), ffn_w_up[j].astype(BF16),
                      ffn_w_down[j].astype(BF16), 1024, 1408)
        else:
            xt = _moe(xt, nw2, moe_w_router[j], moe_b_router[j], _cast_bf16(moe_w_gate[j], 512),
                      _cast_bf16(moe_w_up[j], 512), _cast_bf16(moe_w_down[j], 2048))
    return xt.reshape(B, S, D)
```

```python
import functools
import math

import jax
import jax.numpy as jnp
from jax import lax
from jax.experimental import pallas as pl
from jax.experimental.pallas import tpu as pltpu
from jax.experimental.pallas import tpu_sc as plsc

F32 = jnp.float32
BF16 = jnp.bfloat16

EPS = 1e-6
D_MODEL = 1024
CHUNK = 128
LANES = 128
MXU_N = 256
ROPE_BASE = 10000.0
RET_HEADS = 4
DA_GROUPS = ((1, 16), (4, 4), (16, 1))
DA_DH = 64
DA_UNROLL = 4
ML_HEADS = 4
ML_DK = 64
N_EXPERTS = 8
TOP_K = 2
MOE_TILE = 512
SC_WINDOW = 128
VMEM_LIMIT = 56 * 1024 * 1024

NEG = -1e30


def _cparams(*sem):
    return pltpu.CompilerParams(dimension_semantics=sem, vmem_limit_bytes=VMEM_LIMIT)


def _dot(a, b):
    return jnp.dot(a, b, preferred_element_type=F32)


def _dot_nt(a, b):
    return lax.dot_general(a, b, (((1,), (1,)), ((), ())), preferred_element_type=F32)


def _dot_tn(a, b):
    return lax.dot_general(a, b, (((0,), (0,)), ((), ())), preferred_element_type=F32)


def _sigmoid(x):
    return 1.0 / (1.0 + jnp.exp(-x))


def _silu(x):
    return x * _sigmoid(x)


def _inproj_kernel(x_ref, nw_ref, wb_ref, wf_ref, zb_ref, zf_ref, h_ref):
    j = pl.program_id(1)

    @pl.when(j == 0)
    def _():
        rc = 512
        for r in range(x_ref.shape[0] // rc):
            rows = pl.ds(r * rc, rc)
            x = x_ref[rows, :]
            ms = jnp.mean(x * x, axis=-1, keepdims=True)
            h = (x * lax.rsqrt(ms + EPS) * nw_ref[...]).astype(BF16)
            h_ref[rows, :] = h
            zb_ref[rows, :] = _dot(h, wb_ref[...]).astype(zb_ref.dtype)
            zf_ref[rows, :] = _dot(h, wf_ref[...])

    @pl.when(j > 0)
    def _():
        zb_ref[...] = _dot(h_ref[...], wb_ref[...]).astype(zb_ref.dtype)


def _inproj(x, nw, wb, wf, tm, tnb):
    T, D = x.shape
    return pl.pallas_call(
        _inproj_kernel,
        grid=(T // tm, wb.shape[1] // tnb),
        in_specs=[pl.BlockSpec((tm, D), lambda i, j: (i, 0)),
                  pl.BlockSpec((1, D), lambda i, j: (0, 0)),
                  pl.BlockSpec((D, tnb), lambda i, j: (0, j)),
                  pl.BlockSpec(wf.shape, lambda i, j: (0, 0))],
        out_specs=[pl.BlockSpec((tm, tnb), lambda i, j: (i, j)),
                   pl.BlockSpec((tm, wf.shape[1]), lambda i, j: (i, 0))],
        out_shape=[jax.ShapeDtypeStruct((T, wb.shape[1]), BF16),
                   jax.ShapeDtypeStruct((T, wf.shape[1]), F32)],
        scratch_shapes=[pltpu.VMEM((tm, D), BF16)],
        compiler_params=_cparams("parallel", "arbitrary"),
        name="inproj",
    )(x, nw, wb, wf)


def _retention_kernel(z_ref, cos_ref, sin_ref, nw_ref, o_ref, dec_ref, st_ref):
    S = z_ref.shape[0]
    n_chunks = S // CHUNK
    H = RET_HEADS
    HW = H * LANES
    row = lax.broadcasted_iota(jnp.int32, (CHUNK, CHUNK), 0).astype(F32)
    col = lax.broadcasted_iota(jnp.int32, (CHUNK, CHUNK), 1).astype(F32)
    lgs = [math.log1p(-(2.0 ** (-5.0 - h))) for h in range(H)]
    for h, lg in enumerate(lgs):
        rel = row - col
        dec_ref[h] = jnp.where(rel >= 0, jnp.exp(lg * jnp.maximum(rel, 0.0)), 0.0)
        dec_ref[H + h] = jnp.exp(lg * (row + 1.0))
        dec_ref[2 * H + h] = jnp.exp(lg * (CHUNK - 1.0 - row)) * (LANES ** -0.5)
    st_ref[...] = jnp.zeros(st_ref.shape, F32)

    def body(n, carry):
        r0 = pl.multiple_of(n * CHUNK, CHUNK)
        rows = pl.ds(r0, CHUNK)
        cos = cos_ref[rows, :]
        sin = sin_ref[rows, :]
        qbs, vs, ss, iqs, kvs = [], [], [], [], []
        for h in range(H):
            q = z_ref[rows, h * LANES:(h + 1) * LANES].astype(F32)
            k = z_ref[rows, HW + h * LANES:HW + (h + 1) * LANES].astype(F32)
            v = z_ref[rows, 2 * HW + h * LANES:2 * HW + (h + 1) * LANES]
            q = q * cos + pltpu.roll(q, LANES // 2, 1) * sin
            k = k * cos + pltpu.roll(k, LANES // 2, 1) * sin
            qb = q.astype(BF16)
            kb = (k * (LANES ** -0.5)).astype(BF16)
            kd = (k * dec_ref[2 * H + h]).astype(BF16)
            ss.append(_dot_nt(qb, kb))
            iqs.append(_dot(qb, st_ref[h].astype(BF16)))
            kvs.append(_dot_tn(kd, v))
            vs.append(v)
        ps = [(ss[h] * dec_ref[h]).astype(BF16) for h in range(H)]
        os_ = [_dot(ps[h], vs[h]) + iqs[h] * dec_ref[H + h] for h in range(H)]
        for h in range(H):
            st_ref[h] = st_ref[h] * math.exp(lgs[h] * CHUNK) + kvs[h]
            o = os_[h]
            ms = jnp.mean(o * o, axis=-1, keepdims=True)
            o = o * lax.rsqrt(ms + EPS) * nw_ref[:, h * LANES:(h + 1) * LANES]
            g = z_ref[rows, 3 * HW + h * LANES:3 * HW + (h + 1) * LANES].astype(F32)
            o_ref[rows, h * LANES:(h + 1) * LANES] = (o * _silu(g)).astype(o_ref.dtype)
        return carry

    lax.fori_loop(0, n_chunks, body, 0)


def _retention(zb, cos, sin, nw, B, S):
    T = B * S
    W = RET_HEADS * LANES
    return pl.pallas_call(
        _retention_kernel,
        grid=(B,),
        in_specs=[pl.BlockSpec((S, 4 * W), lambda b: (b, 0)),
                  pl.BlockSpec((S, LANES), lambda b: (0, 0)),
                  pl.BlockSpec((S, LANES), lambda b: (0, 0)),
                  pl.BlockSpec((1, W), lambda b: (0, 0))],
        out_specs=pl.BlockSpec((S, W), lambda b: (b, 0)),
        out_shape=jax.ShapeDtypeStruct((T, W), BF16),
        scratch_shapes=[pltpu.VMEM((3 * RET_HEADS, CHUNK, LANES), F32),
                        pltpu.VMEM((RET_HEADS, LANES, LANES), F32)],
        compiler_params=_cparams("parallel"),
        name="retention",
    )(zb, cos, sin, nw)


def _da_kernel(*refs):
    z_refs = refs[:9]
    wq_ref, wk_ref, o_ref, qn_ref, kn_ref, v_ref, np_ref, lp_ref, mp_ref = refs[9:18]
    nn_refs, ln_refs, mn_refs = refs[18:21], refs[21:24], refs[24:27]
    stage_refs = refs[27:30]
    S = o_ref.shape[0]
    lane = lax.broadcasted_iota(jnp.int32, (1, LANES), 1)
    lo = lane < DA_DH
    row = lax.broadcasted_iota(jnp.int32, (CHUNK, CHUNK), 0)
    col = lax.broadcasted_iota(jnp.int32, (CHUNK, CHUNK), 1)
    mask_cur = col <= row
    row2 = lax.broadcasted_iota(jnp.int32, (CHUNK, 2 * CHUNK), 0)
    col2 = lax.broadcasted_iota(jnp.int32, (CHUNK, 2 * CHUNK), 1)
    mask_band = (col2 >= row2) & (col2 <= row2 + CHUNK)

    seg = (lax.broadcasted_iota(jnp.int32, (LANES, LANES), 0) // DA_DH
           == lax.broadcasted_iota(jnp.int32, (LANES, LANES), 1) // DA_DH).astype(BF16) * (1.0 / DA_DH)

    def head_norm(x, w):
        x2 = x * x
        hi = x2.astype(BF16)
        lo_part = (x2 - hi.astype(F32)).astype(BF16)
        ms = _dot(hi, seg) + _dot(lo_part, seg)
        return x * lax.rsqrt(ms + EPS) * w

    RB = 256
    for g, (dil, nb) in enumerate(DA_GROUPS):
        L = S // dil
        zq_ref, zk_ref, zv_ref = z_refs[3 * g:3 * g + 3]
        if dil > 1:
            for piece in range(S // RB):
                rows = pl.ds(piece * RB, RB)
                for src_ref, dst_ref in zip((zq_ref, zk_ref, zv_ref), stage_refs):
                    dst_ref[rows, :] = src_ref[rows, :].astype(F32)
            zq_ref, zk_ref, zv_ref = stage_refs
        for rho in range(dil):
            for piece in range(max(L // RB, 1)):
                n_rows = min(RB, L)
                src = pl.ds(rho + dil * piece * n_rows, n_rows, stride=dil) if dil > 1 else pl.ds(
                    piece * n_rows, n_rows)
                dst = pl.ds(rho * L + piece * n_rows, n_rows)
                q = zq_ref[src, :].astype(F32)
                k = zk_ref[src, :].astype(F32)
                v = zv_ref[src, :]
                qn_ref[dst, :] = (head_norm(q, wq_ref[...]) * (DA_DH ** -0.5)).astype(BF16)
                kn_ref[dst, :] = head_norm(k, wk_ref[...]).astype(BF16)
                v_ref[dst, :] = v.astype(BF16)

        def batch(r0, chained, first_has_prev):
            chains = []
            for u in range(DA_UNROLL):
                rows = pl.ds(r0 + u * CHUNK, CHUNK)
                if chained and (u > 0 or first_has_prev):
                    keys, mask = pl.ds(r0 + (u - 1) * CHUNK, 2 * CHUNK), mask_band
                else:
                    keys, mask = rows, mask_cur
                k = kn_ref[keys, :]
                v = v_ref[keys, :]
                vaug = jnp.concatenate([v, jnp.ones_like(v)], axis=1)
                q = qn_ref[rows, :]
                zero = jnp.zeros_like(q)
                for qh in (jnp.where(lo, q, zero), jnp.where(lo, zero, q)):
                    chains.append((_dot_nt(qh, k), mask, vaug))
            probs = []
            for s, mask, _ in chains:
                s = jnp.where(mask, s, NEG)
                m = jnp.max(s, axis=-1, keepdims=True)
                probs.append((jnp.exp(s - m).astype(BF16), m))
            accs = [_dot(p, vaug) for (p, _), (_, _, vaug) in zip(probs, chains)]
            nums, dens, maxs = [], [], []
            for u in range(DA_UNROLL):
                a0, a1 = accs[2 * u], accs[2 * u + 1]
                nums.append(jnp.where(lo, a0[:, :LANES], a1[:, :LANES]))
                dens.append(jnp.where(lo, a0[:, LANES:], a1[:, LANES:]))
                maxs.append(jnp.where(lo, probs[2 * u][1], probs[2 * u + 1][1]))
            rows = pl.ds(r0, DA_UNROLL * CHUNK)
            np_ref[rows, :] = jnp.concatenate(nums, axis=0)
            lp_ref[rows, :] = jnp.concatenate(dens, axis=0)
            mp_ref[rows, :] = jnp.concatenate(maxs, axis=0)

        span = DA_UNROLL * CHUNK
        if nb == 1:
            def singles(i, c):
                batch(pl.multiple_of(i * span, span), False, False)
                return c
            lax.fori_loop(0, S // span, singles, 0)
        else:
            def segment(sgi, c, nb=nb):
                base = pl.multiple_of(sgi * (nb * CHUNK), span)
                batch(base, True, False)
                if nb > DA_UNROLL:
                    def inner(n, c2):
                        batch(pl.multiple_of(base + n * span, span), True, True)
                        return c2
                    lax.fori_loop(1, nb // DA_UNROLL, inner, 0)
                return c
            lax.fori_loop(0, dil, segment, 0)

        for rho in range(dil):
            for piece in range(max(L // RB, 1)):
                n_rows = min(RB, L)
                dst = pl.ds(rho + dil * piece * n_rows, n_rows, stride=dil) if dil > 1 else pl.ds(
                    piece * n_rows, n_rows)
                src = pl.ds(rho * L + piece * n_rows, n_rows)
                nn_refs[g][dst, :] = np_ref[src, :]
                ln_refs[g][dst, :] = lp_ref[src, :]
                mn_refs[g][dst, :] = mp_ref[src, :]

    for piece in range(S // RB):
        rows = pl.ds(piece * RB, RB)
        m = jnp.maximum(jnp.maximum(mn_refs[0][rows, :], mn_refs[1][rows, :]), mn_refs[2][rows, :])
        num = jnp.zeros((RB, LANES), F32)
        den = jnp.zeros((RB, LANES), F32)
        for g in range(3):
            e = jnp.exp(mn_refs[g][rows, :] - m)
            num = num + e * nn_refs[g][rows, :]
            den = den + e * ln_refs[g][rows, :]
        o_ref[rows, :] = (num / den).astype(o_ref.dtype)


def _dilated_attention(zb, slab0, wq, wk, B, S):
    T = B * S
    slab = lambda k: pl.BlockSpec((S, LANES), lambda b, p: (b, slab0 + 9 * p + k))
    return pl.pallas_call(
        _da_kernel,
        grid=(B, 2),
        in_specs=[slab(k) for k in range(9)] + [
                  pl.BlockSpec((1, LANES), lambda b, p: (0, 0)),
                  pl.BlockSpec((1, LANES), lambda b, p: (0, 0))],
        out_specs=pl.BlockSpec((S, LANES), lambda b, p: (b, p)),
        out_shape=jax.ShapeDtypeStruct((T, 2 * LANES), BF16),
        scratch_shapes=[pltpu.VMEM((S, LANES), BF16), pltpu.VMEM((S, LANES), BF16),
                        pltpu.VMEM((S, LANES), BF16),
                        pltpu.VMEM((S, LANES), F32), pltpu.VMEM((S, LANES), F32),
                        pltpu.VMEM((S, LANES), F32)] + [pltpu.VMEM((S, LANES), F32)] * 12,
        compiler_params=_cparams("parallel", "arbitrary"),
        name="dilated_attention",
    )(*([zb] * 9), wq, wk)


def _mlstm_kernel(qk_ref, v_ref, og_ref, gate_ref, cw_ref, gb_ref, nw_ref, o_ref,
                  xp_ref, qc_ref, st_ref, e_ref, et_ref):
    S = qk_ref.shape[0]
    n_chunks = S // CHUNK
    W = qk_ref.shape[1]
    PAD = 8
    xp_ref[0:PAD, :] = jnp.zeros((PAD, W), F32)
    RB = 256
    for piece in range(S // RB):
        xp_ref[PAD + piece * RB:PAD + (piece + 1) * RB, :] = qk_ref[piece * RB:(piece + 1) * RB, :].astype(F32)
    for piece in range(S // RB):
        for cs in range(W // LANES):
            cols = slice(cs * LANES, (cs + 1) * LANES)
            acc = cw_ref[0:1, cols] * xp_ref[pl.ds(PAD + piece * RB - 3, RB), cols]
            for i in range(1, 4):
                acc = acc + cw_ref[i:i + 1, cols] * xp_ref[pl.ds(PAD + piece * RB - 3 + i, RB), cols]
            scale = ML_DK ** -0.5 if cs >= W // LANES // 2 else 1.0
            qc_ref[piece * RB:(piece + 1) * RB, cols] = (_silu(acc) * scale).astype(BF16)

    lane = lax.broadcasted_iota(jnp.int32, (1, LANES), 1)
    lo = lane < ML_DK
    row = lax.broadcasted_iota(jnp.int32, (CHUNK, CHUNK), 0)
    col = lax.broadcasted_iota(jnp.int32, (CHUNK, CHUNK), 1)
    causal = col <= row
    tril = causal.astype(F32)
    srow = lax.broadcasted_iota(jnp.int32, (LANES, 1), 0) < ML_DK
    st_ref[...] = jnp.zeros(st_ref.shape, F32)

    is_f = (lane >= ML_HEADS) & (lane < 2 * ML_HEADS)
    for n in range(n_chunks):
        rows = pl.ds(n * CHUNK, CHUNK)
        gp = gate_ref[rows, :] + gb_ref[...]
        logf = jnp.minimum(gp, 0.0) - jnp.log(1.0 + jnp.exp(-jnp.abs(gp)))
        cum = jnp.dot(tril, jnp.where(is_f, logf, 0.0), preferred_element_type=F32,
                      precision=lax.Precision.HIGHEST)
        e = jnp.where(lane < ML_HEADS, gp, cum)
        e_ref[rows, :] = e
        et_ref[n] = e.T[0:2 * ML_HEADS, :]

    def chunk(n, mms):
        r0 = pl.multiple_of(n * CHUNK, CHUNK)
        rows = pl.ds(r0, CHUNK)
        e = e_ref[rows, :]
        et = et_ref[n]
        qhs, khs, vaugs, s_raws, iqs = [], [], [], [], []
        for pair in range(2):
            q2 = qc_ref[rows, pair * LANES:(pair + 1) * LANES]
            k2 = qc_ref[rows, W // 2 + pair * LANES:W // 2 + (pair + 1) * LANES]
            zero = jnp.zeros_like(q2)
            stb = st_ref[pair].astype(BF16)
            for hs in range(2):
                h = 2 * pair + hs
                qh = jnp.where(lo, q2, zero) if hs == 0 else jnp.where(lo, zero, q2)
                kh = jnp.where(lo, k2, zero) if hs == 0 else jnp.where(lo, zero, k2)
                vh = v_ref[rows, h * LANES:(h + 1) * LANES]
                s_raws.append(_dot_nt(qh, k2))
                iqs.append(_dot(qh, stb))
                khs.append(kh)
                vaugs.append(jnp.concatenate([vh, jnp.ones_like(vh)], axis=-1))
        ps, wis, ms_, kws, wcs, new_mms = [], [], [], [], [], []
        for h in range(ML_HEADS):
            mm = mms[h]
            i_b = jnp.broadcast_to(e[:, h:h + 1], (CHUNK, LANES))
            a_b = jnp.broadcast_to(e[:, ML_HEADS + h:ML_HEADS + h + 1], (CHUNK, LANES))
            c_row = et[h:h + 1, :] - et[ML_HEADS + h:ML_HEADS + h + 1, :]
            dmat = jnp.where(causal, a_b + c_row, NEG)
            inter = a_b + mm
            m = jnp.maximum(inter, jnp.max(dmat, axis=-1, keepdims=True))
            ps.append((s_raws[h] * jnp.exp(dmat - m)).astype(BF16))
            wis.append(jnp.exp(inter - m))
            ms_.append(m)
            m_new = m[CHUNK - 1:CHUNK, :]
            a_last = a_b[CHUNK - 1:CHUNK, :]
            wk = jnp.exp(a_last - a_b + i_b - m_new)
            kws.append((khs[h].astype(F32) * wk).astype(BF16))
            wc = jnp.exp(a_last + mm - m_new)
            wcs.append(jnp.concatenate([wc, wc], axis=1))
            new_mms.append(m_new)
        accs = [_dot(ps[h], vaugs[h]) + jnp.concatenate([wis[h], wis[h]], axis=1) * iqs[h]
                for h in range(ML_HEADS)]
        upds = [_dot_tn(kws[h], vaugs[h]) for h in range(ML_HEADS)]
        for pair in range(2):
            h0, h1 = 2 * pair, 2 * pair + 1
            st_ref[pair] = jnp.where(srow, wcs[h0], wcs[h1]) * st_ref[pair] + (upds[h0] + upds[h1])
        for h in range(ML_HEADS):
            num = accs[h][:, :LANES]
            den = accs[h][:, LANES:]
            hv = num / jnp.maximum(jnp.abs(den), jnp.exp(-ms_[h]))
            var = jnp.mean(hv * hv, axis=-1, keepdims=True)
            hv = hv * lax.rsqrt(var + EPS) * nw_ref[:, h * LANES:(h + 1) * LANES]
            og = og_ref[rows, h * LANES:(h + 1) * LANES].astype(F32)
            o_ref[rows, h * LANES:(h + 1) * LANES] = (hv * _sigmoid(og)).astype(o_ref.dtype)
        return tuple(new_mms)

    lax.fori_loop(0, n_chunks, chunk, tuple(jnp.zeros((1, LANES), F32) for _ in range(ML_HEADS)))


def _mlstm(zb, zf, conv_w, gate_bias, nw, B, S, qk_blk, gate_blk):
    T = B * S
    W = ML_HEADS * LANES
    return pl.pallas_call(
        _mlstm_kernel,
        grid=(B,),
        in_specs=[pl.BlockSpec((S, W), lambda b: (b, qk_blk)),
                  pl.BlockSpec((S, W), lambda b: (b, qk_blk + 1)),
                  pl.BlockSpec((S, W), lambda b: (b, qk_blk + 2)),
                  pl.BlockSpec((S, LANES), lambda b: (b, gate_blk)),
                  pl.BlockSpec((4, W), lambda b: (0, 0)),
                  pl.BlockSpec((1, LANES), lambda b: (0, 0)),
                  pl.BlockSpec((1, W), lambda b: (0, 0))],
        out_specs=pl.BlockSpec((S, W), lambda b: (b, 0)),
        out_shape=jax.ShapeDtypeStruct((T, W), BF16),
        scratch_shapes=[pltpu.VMEM((S + 8, W), F32), pltpu.VMEM((S, W), BF16),
                        pltpu.VMEM((2, LANES, 2 * LANES), F32),
                        pltpu.VMEM((S, LANES), F32),
                        pltpu.VMEM((S // CHUNK, 2 * ML_HEADS, LANES), F32)],
        compiler_params=_cparams("parallel"),
        name="mlstm",
    )(zb, zb, zb, zf, conv_w, gate_bias, nw)


def _merge_kernel(x_ref, oret_ref, oda_ref, oml_ref, gr_ref, gd_ref, gm_ref,
                  wr_ref, wd_ref, wm_ref, wo_ref, o_ref):
    y = _sigmoid(gr_ref[...].astype(F32)) * _dot(oret_ref[...], wr_ref[...])
    y = y + _sigmoid(gd_ref[...].astype(F32)) * _dot(oda_ref[...], wd_ref[...])
    y = y + _sigmoid(gm_ref[...].astype(F32)) * _dot(oml_ref[...], wm_ref[...])
    o_ref[...] = x_ref[...] + _dot(y.astype(BF16), wo_ref[...])


def _merge(x, o_ret, o_da, o_ml, zb, g_blk, wr, wd, wm, wo, tm):
    T, D = x.shape
    full = lambda a: pl.BlockSpec(a.shape, lambda i: (0, 0))
    return pl.pallas_call(
        _merge_kernel,
        grid=(T // tm,),
        in_specs=[pl.BlockSpec((tm, D), lambda i: (i, 0)),
                  pl.BlockSpec((tm, o_ret.shape[1]), lambda i: (i, 0)),
                  pl.BlockSpec((tm, o_da.shape[1]), lambda i: (i, 0)),
                  pl.BlockSpec((tm, o_ml.shape[1]), lambda i: (i, 0)),
                  pl.BlockSpec((tm, D), lambda i: (i, g_blk)),
                  pl.BlockSpec((tm, D), lambda i: (i, g_blk + 1)),
                  pl.BlockSpec((tm, D), lambda i: (i, g_blk + 2)),
                  full(wr), full(wd), full(wm), full(wo)],
        out_specs=pl.BlockSpec((tm, D), lambda i: (i, 0)),
        out_shape=jax.ShapeDtypeStruct((T, D), F32),
        compiler_params=_cparams("parallel"),
        name="merge_outproj",
    )(x, o_ret, o_da, o_ml, zb, zb, zb, wr, wd, wm, wo)


def _swiglu_block(h, wg_ref, wu_ref, wd_ref, a_ref):
    tf = wg_ref.shape[1]
    c0 = 0
    while c0 < tf:
        w = min(MXU_N, tf - c0)
        g = _dot(h, wg_ref[:, c0:c0 + w].astype(BF16))
        u = _dot(h, wu_ref[:, c0:c0 + w].astype(BF16))
        a_ref[:, c0:c0 + w] = (_silu(g) * u).astype(BF16)
        c0 += w
    return _dot(a_ref[...], wd_ref[...].astype(BF16))


def _ffn_kernel(x_ref, nw_ref, wg_ref, wu_ref, wd_ref, o_ref, h_ref, acc_ref, a_ref):
    j = pl.program_id(1)

    @pl.when(j == 0)
    def _():
        x = x_ref[...]
        ms = jnp.mean(x * x, axis=-1, keepdims=True)
        h_ref[...] = (x * lax.rsqrt(ms + EPS) * nw_ref[...]).astype(BF16)
        acc_ref[...] = x

    acc_ref[...] += _swiglu_block(h_ref[...], wg_ref, wu_ref, wd_ref, a_ref)

    @pl.when(j == pl.num_programs(1) - 1)
    def _():
        o_ref[...] = acc_ref[...]


def _ffn(x, nw, wg, wu, wd, tm, tf):
    T, D = x.shape
    F = wg.shape[1]
    return pl.pallas_call(
        _ffn_kernel,
        grid=(T // tm, F // tf),
        in_specs=[pl.BlockSpec((tm, D), lambda i, j: (i, 0)),
                  pl.BlockSpec((1, D), lambda i, j: (0, 0)),
                  pl.BlockSpec((D, tf), lambda i, j: (0, j)),
                  pl.BlockSpec((D, tf), lambda i, j: (0, j)),
                  pl.BlockSpec((tf, D), lambda i, j: (j, 0))],
        out_specs=pl.BlockSpec((tm, D), lambda i, j: (i, 0)),
        out_shape=jax.ShapeDtypeStruct((T, D), F32),
        scratch_shapes=[pltpu.VMEM((tm, D), BF16), pltpu.VMEM((tm, D), F32),
                        pltpu.VMEM((tm, tf), BF16)],
        compiler_params=_cparams("parallel", "arbitrary"),
        name="swiglu_ffn",
    )(x, nw, wg, wu, wd)


HI16 = 0xFFFF0000


def _pack_bf16_pairs(a, b):
    ab = pltpu.bitcast(a.astype(BF16).astype(F32), jnp.uint32)
    bb = pltpu.bitcast(b.astype(BF16).astype(F32), jnp.uint32)
    return (ab >> 16) | (bb & jnp.uint32(HI16))


def _unpack_bf16_pairs(p):
    lo = pltpu.bitcast(p << 16, F32)
    hi = pltpu.bitcast(p & jnp.uint32(HI16), F32)
    return lo, hi


def _router_kernel(x_ref, nw_ref, wr_ref, br_ref, ha_ref, hb_ref, sel_ref):
    x = x_ref[...]
    ms = jnp.mean(x * x, axis=-1, keepdims=True)
    h = x * lax.rsqrt(ms + EPS) * nw_ref[...]
    Q = h.shape[1] // 4
    ha_ref[...] = _pack_bf16_pairs(h[:, 0:Q], h[:, Q:2 * Q])
    hb_ref[...] = _pack_bf16_pairs(h[:, 2 * Q:3 * Q], h[:, 3 * Q:4 * Q])
    logits = jnp.dot(h, wr_ref[...], preferred_element_type=F32,
                     precision=lax.Precision.HIGHEST) + br_ref[...]
    lane = lax.broadcasted_iota(jnp.int32, logits.shape, 1).astype(F32)
    logits = jnp.where(lane < N_EXPERTS, logits, NEG)
    m1 = jnp.max(logits, axis=-1, keepdims=True)
    i1 = jnp.min(jnp.where(logits == m1, lane, float(LANES)), axis=-1, keepdims=True)
    rest = jnp.where(lane == i1, NEG, logits)
    m2 = jnp.max(rest, axis=-1, keepdims=True)
    i2 = jnp.min(jnp.where(rest == m2, lane, float(LANES)), axis=-1, keepdims=True)
    e2 = jnp.exp(m2 - m1)
    p1 = 1.0 / (1.0 + e2)
    p2 = e2 / (1.0 + e2)
    sel_ref[...] = jnp.where(lane == 0.0, i1, jnp.where(lane == 1.0, i2,
                             jnp.where(lane == 2.0, p1, jnp.where(lane == 3.0, p2, 0.0))))


def _router(x, nw, wr, br, tm):
    T, D = x.shape
    Q = D // 4
    return pl.pallas_call(
        _router_kernel,
        grid=(T // tm,),
        in_specs=[pl.BlockSpec((tm, D), lambda i: (i, 0)),
                  pl.BlockSpec((1, D), lambda i: (0, 0)),
                  pl.BlockSpec((D, LANES), lambda i: (0, 0)),
                  pl.BlockSpec((1, LANES), lambda i: (0, 0))],
        out_specs=[pl.BlockSpec((tm, Q), lambda i: (i, 0)),
                   pl.BlockSpec((tm, Q), lambda i: (i, 0)),
                   pl.BlockSpec((tm, LANES), lambda i: (i, 0))],
        out_shape=[jax.ShapeDtypeStruct((T, Q), jnp.uint32), jax.ShapeDtypeStruct((T, Q), jnp.uint32),
                   jax.ShapeDtypeStruct((T, LANES), F32)],
        compiler_params=_cparams("parallel"),
        name="moe_router",
    )(x, nw, wr, br)


def _moe_rank_kernel(sel_ref, pos_ref, meta_ref, cnt_ref, offs_ref, carry_ref, before_ref):
    ph = pl.program_id(0)
    i = pl.program_id(1)
    tm = sel_ref.shape[0]
    lane = lax.broadcasted_iota(jnp.int32, (tm, LANES), 1).astype(F32)
    lane1 = lax.broadcasted_iota(jnp.int32, (1, LANES), 1).astype(F32)
    sel = sel_ref[...]
    i1 = sel[:, 0:1]
    i2 = sel[:, 1:2]
    onehot = jnp.where((lane == i1) | (lane == i2), 1.0, 0.0)
    colsum = jnp.sum(onehot, axis=0, keepdims=True)

    @pl.when((ph == 0) & (i == 0))
    def _():
        cnt_ref[...] = jnp.zeros(cnt_ref.shape, F32)

    @pl.when(ph == 0)
    def _():
        cnt_ref[...] += colsum

    def padded_counts():
        return jnp.floor((cnt_ref[...] + (MOE_TILE - 1.0)) * (1.0 / MOE_TILE)) * MOE_TILE

    @pl.when((ph == 1) & (i == 0))
    def _():
        k = lax.broadcasted_iota(jnp.int32, (LANES, LANES), 0)
        e = lax.broadcasted_iota(jnp.int32, (LANES, LANES), 1)
        upper = (k < e).astype(F32)
        offs_ref[...] = jnp.dot(padded_counts(), upper, preferred_element_type=F32,
                                precision=lax.Precision.HIGHEST)
        carry_ref[...] = jnp.zeros(carry_ref.shape, F32)
        r = lax.broadcasted_iota(jnp.int32, (tm, tm), 0)
        c = lax.broadcasted_iota(jnp.int32, (tm, tm), 1)
        before_ref[...] = (c < r).astype(BF16)

    @pl.when(ph == 1)
    def _():
        rank = _dot(before_ref[...], onehot.astype(BF16)) + carry_ref[0:1, :]
        row = rank + offs_ref[0:1, :]
        pos1 = jnp.sum(jnp.where(lane == i1, row, 0.0), axis=-1, keepdims=True)
        pos2 = jnp.sum(jnp.where(lane == i2, row, 0.0), axis=-1, keepdims=True)
        both = jnp.where(lane == 0.0, pos1, jnp.where(lane == 1.0, pos2, 0.0))
        pos_ref[...] = both.T[0:8, :].astype(jnp.int32)
        carry_ref[...] += colsum

    @pl.when((ph == 1) & (i == pl.num_programs(1) - 1))
    def _():
        nrow = meta_ref.shape[0]
        padded = padded_counts()[0:1, :]
        offs = offs_ref[0:1, :]
        ends = offs + padded
        start = lax.broadcasted_iota(jnp.int32, (nrow, 1), 0).astype(F32) * MOE_TILE
        is_e = lane1 < N_EXPERTS
        te = jnp.sum(jnp.where(is_e & (ends <= start), 1.0, 0.0), axis=-1, keepdims=True)
        te = jnp.minimum(te, N_EXPERTS - 1.0)
        valid_end = jnp.sum(jnp.where(lane1 == te, offs + cnt_ref[0:1, :], 0.0), axis=-1, keepdims=True)
        nv = jnp.clip(valid_end - start, 0.0, MOE_TILE)
        total = jnp.sum(jnp.where(is_e, padded, 0.0), axis=-1, keepdims=True)
        active = jnp.where(start < total, 1.0, 0.0)
        meta_ref[...] = jnp.where(lane1 == 0.0, te, jnp.where(lane1 == 1.0, nv,
                                  jnp.where(lane1 == 2.0, active, 0.0)))


def _moe_rank(sel, tm, meta_rows):
    T = sel.shape[0]
    nt = T // tm
    return pl.pallas_call(
        _moe_rank_kernel,
        grid=(2, nt),
        in_specs=[pl.BlockSpec((tm, LANES), lambda ph, i: (i, 0))],
        out_specs=[pl.BlockSpec((None, 8, tm), lambda ph, i: (i * ph, 0, 0)),
                   pl.BlockSpec((meta_rows, LANES), lambda ph, i: (0, 0))],
        out_shape=[jax.ShapeDtypeStruct((nt, 8, tm), jnp.int32),
                   jax.ShapeDtypeStruct((meta_rows, LANES), F32)],
        scratch_shapes=[pltpu.VMEM((8, LANES), F32), pltpu.VMEM((8, LANES), F32),
                        pltpu.VMEM((8, LANES), F32), pltpu.VMEM((tm, tm), BF16)],
        compiler_params=_cparams("arbitrary", "arbitrary"),
        name="moe_rank",
    )(sel)


def _sc_mesh():
    return plsc.VectorSubcoreMesh(core_axis_name="core", subcore_axis_name="subcore")


def _sc_scatter_rows(x, idx, n_rows):
    n_idx = idx.shape[1]
    nt = x.shape[0] // SC_WINDOW
    width = x.shape[1]

    @functools.partial(pl.kernel, out_type=jax.ShapeDtypeStruct((n_rows, width), x.dtype),
                       mesh=_sc_mesh())
    def scatter_kernel(x_hbm, i_hbm, o_hbm):
        def body(x_vmem, i_vmem):
            pltpu.sync_copy(x_vmem, o_hbm.at[i_vmem.at[0]])

        pltpu.emit_pipeline(
            body, grid=(n_idx // SC_WINDOW,),
            in_specs=[pl.BlockSpec((SC_WINDOW, width), lambda i: (i % nt, 0)),
                      pl.BlockSpec((1, SC_WINDOW), lambda i: (0, i))],
            out_specs=[],
            core_axis_name=("core", "subcore"), dimension_semantics=(pltpu.PARALLEL,),
        )(x_hbm, i_hbm)

    return scatter_kernel(x, idx)


def _sc_gather_rows(x, idx):
    n_idx = idx.shape[1]
    width = x.shape[1]

    @functools.partial(pl.kernel, out_type=jax.ShapeDtypeStruct((n_idx, width), x.dtype),
                       mesh=_sc_mesh())
    def gather_kernel(x_hbm, i_hbm, o_hbm):
        def body(i_vmem, o_vmem):
            pltpu.sync_copy(x_hbm.at[i_vmem.at[0]], o_vmem)

        pltpu.emit_pipeline(
            body, grid=(n_idx // SC_WINDOW,),
            in_specs=[pl.BlockSpec((1, SC_WINDOW), lambda i: (0, i))],
            out_specs=[pl.BlockSpec((SC_WINDOW, width), lambda i: (i, 0))],
            core_axis_name=("core", "subcore"), dimension_semantics=(pltpu.PARALLEL,),
        )(i_hbm, o_hbm)

    return gather_kernel(x, idx)


def _moe_group_kernel(te_ref, nv_ref, na_ref, xa_ref, xb_ref, wg_ref, wu_ref, wd_ref,
                      ya_ref, yb_ref, h_ref, acc_ref, a_ref):
    j = pl.program_id(0)
    f = pl.program_id(1)

    @pl.when(j < na_ref[0])
    def _():
        @pl.when(f == 0)
        def _():
            tm = h_ref.shape[0]
            Q = xa_ref.shape[1]
            valid = lax.broadcasted_iota(jnp.int32, (tm, 1), 0) < nv_ref[j]
            for src, c0 in ((xa_ref, 0), (xb_ref, 2 * Q)):
                lo, hi = _unpack_bf16_pairs(src[...])
                h_ref[:, c0:c0 + Q] = jnp.where(valid, lo, 0.0).astype(BF16)
                h_ref[:, c0 + Q:c0 + 2 * Q] = jnp.where(valid, hi, 0.0).astype(BF16)
            acc_ref[...] = jnp.zeros(acc_ref.shape, F32)

        acc_ref[...] += _swiglu_block(h_ref[...], wg_ref, wu_ref, wd_ref, a_ref)

        @pl.when(f == pl.num_programs(1) - 1)
        def _():
            Q = ya_ref.shape[1]
            ya_ref[...] = _pack_bf16_pairs(acc_ref[:, 0:Q], acc_ref[:, Q:2 * Q])
            yb_ref[...] = _pack_bf16_pairs(acc_ref[:, 2 * Q:3 * Q], acc_ref[:, 3 * Q:4 * Q])


def _moe_group(te, nv, na, xa, xb, wg, wu, wd, tf):
    R, Q = xa.shape
    E, D, F = wg.shape
    nf = F // tf
    tile = lambda j, f, te, nv, na: (jnp.minimum(j, na[0] - 1), 0)
    ff = lambda j, f, na: jnp.where(j < na[0], f, nf - 1)
    grid_spec = pltpu.PrefetchScalarGridSpec(
        num_scalar_prefetch=3,
        grid=(R // MOE_TILE, nf),
        in_specs=[pl.BlockSpec((MOE_TILE, Q), tile),
                  pl.BlockSpec((MOE_TILE, Q), tile),
                  pl.BlockSpec((None, D, tf), lambda j, f, te, nv, na: (te[j], 0, ff(j, f, na))),
                  pl.BlockSpec((None, D, tf), lambda j, f, te, nv, na: (te[j], 0, ff(j, f, na))),
                  pl.BlockSpec((None, tf, D), lambda j, f, te, nv, na: (te[j], ff(j, f, na), 0))],
        out_specs=[pl.BlockSpec((MOE_TILE, Q), tile), pl.BlockSpec((MOE_TILE, Q), tile)],
        scratch_shapes=[pltpu.VMEM((MOE_TILE, D), BF16), pltpu.VMEM((MOE_TILE, D), F32),
                        pltpu.VMEM((MOE_TILE, tf), BF16)],
    )
    return pl.pallas_call(
        _moe_group_kernel,
        grid_spec=grid_spec,
        out_shape=[jax.ShapeDtypeStruct((R, Q), jnp.uint32), jax.ShapeDtypeStruct((R, Q), jnp.uint32)],
        compiler_params=_cparams("arbitrary", "arbitrary"),
        name="moe_experts",
    )(te, nv, na, xa, xb, wg, wu, wd)


def _moe_combine_kernel(x_ref, sel_ref, a1_ref, a2_ref, b1_ref, b2_ref, o_ref):
    sel = sel_ref[...]
    p1 = sel[:, 2:3]
    p2 = sel[:, 3:4]
    Q = a1_ref.shape[1]
    for first, second, c0 in ((a1_ref, a2_ref, 0), (b1_ref, b2_ref, 2 * Q)):
        lo1, hi1 = _unpack_bf16_pairs(first[...])
        lo2, hi2 = _unpack_bf16_pairs(second[...])
        o_ref[:, c0:c0 + Q] = x_ref[:, c0:c0 + Q] + (p1 * lo1 + p2 * lo2)
        o_ref[:, c0 + Q:c0 + 2 * Q] = x_ref[:, c0 + Q:c0 + 2 * Q] + (p1 * hi1 + p2 * hi2)


def _moe_combine(x, sel, ga, gb, tm):
    T, D = x.shape
    Q = ga.shape[1]
    nt = T // tm
    return pl.pallas_call(
        _moe_combine_kernel,
        grid=(nt,),
        in_specs=[pl.BlockSpec((tm, D), lambda i: (i, 0)),
                  pl.BlockSpec((tm, LANES), lambda i: (i, 0)),
                  pl.BlockSpec((tm, Q), lambda i: (i, 0)),
                  pl.BlockSpec((tm, Q), lambda i: (i + nt, 0)),
                  pl.BlockSpec((tm, Q), lambda i: (i, 0)),
                  pl.BlockSpec((tm, Q), lambda i: (i + nt, 0))],
        out_specs=pl.BlockSpec((tm, D), lambda i: (i, 0)),
        out_shape=jax.ShapeDtypeStruct((T, D), F32),
        compiler_params=_cparams("parallel"),
        name="moe_combine",
    )(x, sel, ga, ga, gb, gb)


def _cast_kernel(x_ref, o_ref):
    o_ref[...] = x_ref[...].astype(o_ref.dtype)


def _cast_bf16(w, rows):
    cols = w.shape[-1]
    w2 = w.reshape(-1, cols)
    out = pl.pallas_call(
        _cast_kernel,
        grid=(w2.shape[0] // rows,),
        in_specs=[pl.BlockSpec((rows, cols), lambda i: (i, 0))],
        out_specs=pl.BlockSpec((rows, cols), lambda i: (i, 0)),
        out_shape=jax.ShapeDtypeStruct(w2.shape, BF16),
        compiler_params=_cparams("parallel"),
        name="cast_bf16",
    )(w2)
    return out.reshape(w.shape)


def _moe(x, nw, w_router, b_router, wg, wu, wd):
    T, D = x.shape
    wr = jnp.pad(w_router, ((0, 0), (0, LANES - N_EXPERTS)))
    br = jnp.pad(b_router, (0, LANES - N_EXPERTS)).reshape(1, LANES)
    ha, hb, sel = _router(x, nw, wr, br, 1024)
    n_rows = TOP_K * T + N_EXPERTS * MOE_TILE
    n_tiles = n_rows // MOE_TILE
    pos, meta = _moe_rank(sel, 1024, 256)
    idx = jnp.concatenate([pos[:, 0, :].reshape(1, T), pos[:, 1, :].reshape(1, T)], axis=1)
    te = meta[:n_tiles, 0].astype(jnp.int32)
    nv = meta[:n_tiles, 1].astype(jnp.int32)
    na = jnp.sum(meta[:n_tiles, 2]).astype(jnp.int32).reshape(1)
    te = jnp.where(jnp.arange(n_tiles) < na[0], te, te[na[0] - 1])
    xa = _sc_scatter_rows(ha, idx, n_rows)
    xb = _sc_scatter_rows(hb, idx, n_rows)
    ya, yb = _moe_group(te, nv, na, xa, xb, wg, wu, wd, wg.shape[2] // 2)
    ga = _sc_gather_rows(ya, idx)
    gb = _sc_gather_rows(yb, idx)
    return _moe_combine(x, sel, ga, gb, 1024)


def _split_w_in(w_in):
    sizes = (512, 512, 512, 512, 768, 768, 768, 256, 256, 512, 512, 4, 4, 1024, 1024, 1024)
    offs = [0]
    for s in sizes:
        offs.append(offs[-1] + s)
    part = lambda i: w_in[..., offs[i]:offs[i + 1]]
    rq, rk, rv, rg, dq, dk, dv, mq, mk, mv, mo, mi, mf, g_ret, g_da, g_ml = (part(i) for i in range(16))
    cols = []
    for p in range(2):
        for g in range(3):
            for t in (dq, dk, dv):
                cols.append(t[..., g * 256 + p * 128:g * 256 + (p + 1) * 128])
    wb = jnp.concatenate([rq, rk, rv, rg, g_ret, g_da, g_ml, mq, mk, mv, mo] + cols,
                         axis=-1).astype(BF16)
    pad = jnp.zeros(w_in.shape[:-1] + (LANES - 8,), w_in.dtype)
    wf = jnp.concatenate([mi, mf, pad], axis=-1).astype(BF16)
    return wb, wf


def _rope_tables(S):
    half = LANES // 2
    inv = jnp.power(ROPE_BASE, -jnp.arange(half, dtype=F32) / half)
    ang = jnp.arange(S, dtype=F32)[:, None] * inv[None, :]
    cos = jnp.cos(ang)
    sin = jnp.sin(ang)
    return jnp.concatenate([cos, cos], axis=1), jnp.concatenate([-sin, sin], axis=1)


def kernel(x, norm1_w, w_in, ret_norm_w, da_q_norm_w, da_k_norm_w, ml_conv_w, ml_i_bias, ml_f_bias,
           ml_norm_w, w_br_ret, w_br_da, w_br_ml, w_out, norm2_w, ffn_w_gate, ffn_w_up, ffn_w_down,
           moe_w_router, moe_b_router, moe_w_gate, moe_w_up, moe_w_down):
    B, S, D = x.shape
    T = B * S
    depth = w_in.shape[0]
    cos, sin = _rope_tables(S)
    wb_all, wf_all = _split_w_in(w_in)
    xt = x.reshape(T, D)
    for layer in range(depth):
        wb, wf = wb_all[layer], wf_all[layer]
        nw1 = norm1_w[layer].reshape(1, D)
        zb, zf = _inproj(xt, nw1, wb, wf, 2048, 1280)
        o_ret = _retention(zb, cos, sin, ret_norm_w[layer].reshape(1, -1), B, S)
        wq = jnp.tile(da_q_norm_w[layer], 2).reshape(1, LANES)
        wk = jnp.tile(da_k_norm_w[layer], 2).reshape(1, LANES)
        o_da = _dilated_attention(zb, 52, wq, wk, B, S)
        gate_bias = jnp.concatenate([ml_i_bias[layer], ml_f_bias[layer],
                                     jnp.zeros((LANES - 2 * ML_HEADS,), F32)]).reshape(1, LANES)
        o_ml = _mlstm(zb, zf, ml_conv_w[layer], gate_bias, ml_norm_w[layer].reshape(1, -1), B, S,
                      qk_blk=10, gate_blk=0)
        xt = _merge(xt, o_ret, o_da, o_ml, zb, 2,
                    w_br_ret[layer].astype(BF16), w_br_da[layer].astype(BF16),
                    w_br_ml[layer].astype(BF16), w_out[layer].astype(BF16), 512)
        nw2 = norm2_w[layer].reshape(1, D)
        j = layer // 2
        if layer % 2 == 0:
            xt = _ffn(xt, nw2, ffn_w_gate[j].astype(BF16), ffn_w_up[j].astype(BF16),
                      ffn_w_down[j].astype(BF16), 1024, 1408)
        else:
            xt = _moe(xt, nw2, moe_w_router[j], moe_b_router[j], _cast_bf16(moe_w_gate[j], 512),
                      _cast_bf16(moe_w_up[j], 512), _cast_bf16(moe_w_down[j], 2048))
    return xt.reshape(B, S, D)
```

```python
import functools
import math

import jax
import jax.numpy as jnp
from jax import lax
from jax.experimental import pallas as pl
from jax.experimental.pallas import tpu as pltpu
from jax.experimental.pallas import tpu_sc as plsc

F32 = jnp.float32
BF16 = jnp.bfloat16

EPS = 1e-6
D_MODEL = 1024
CHUNK = 128
LANES = 128
MXU_N = 256
ROPE_BASE = 10000.0
RET_HEADS = 4
DA_GROUPS = ((1, 16), (4, 4), (16, 1))
DA_DH = 64
DA_UNROLL = 4
ML_HEADS = 4
ML_DK = 64
ML_SEQS = 2
N_EXPERTS = 8
TOP_K = 2
MOE_TILE = 512
SC_WINDOW = 128
VMEM_LIMIT = 56 * 1024 * 1024

NEG = -1e30


def _cparams(*sem):
    return pltpu.CompilerParams(dimension_semantics=sem, vmem_limit_bytes=VMEM_LIMIT)


def _dot(a, b):
    return jnp.dot(a, b, preferred_element_type=F32)


def _dot_nt(a, b):
    return lax.dot_general(a, b, (((1,), (1,)), ((), ())), preferred_element_type=F32)


def _dot_tn(a, b):
    return lax.dot_general(a, b, (((0,), (0,)), ((), ())), preferred_element_type=F32)


def _sigmoid(x):
    return 1.0 / (1.0 + jnp.exp(-x))


def _silu(x):
    return x * _sigmoid(x)


def _inproj_kernel(conv_blk, x_ref, nw_ref, wb_ref, wf_ref, cw_ref, zb_ref, zf_ref, h_ref, cv_ref):
    j = pl.program_id(1)
    tm = x_ref.shape[0]

    @pl.when(j == 0)
    def _():
        rc = 512
        for r in range(tm // rc):
            rows = pl.ds(r * rc, rc)
            x = x_ref[rows, :]
            ms = jnp.mean(x * x, axis=-1, keepdims=True)
            h = (x * lax.rsqrt(ms + EPS) * nw_ref[...]).astype(BF16)
            h_ref[rows, :] = h
            zb_ref[rows, :] = _dot(h, wb_ref[...]).astype(zb_ref.dtype)
            zf_ref[rows, :] = _dot(h, wf_ref[...])

    @pl.when((j > 0) & (j != conv_blk))
    def _():
        zb_ref[...] = _dot(h_ref[...], wb_ref[...]).astype(zb_ref.dtype)

    @pl.when(j == conv_blk)
    def _():
        h = h_ref[...]
        cw = cw_ref.shape[1]
        pad = cv_ref.shape[0] - tm
        cv_ref[0:pad, :] = jnp.zeros((pad, cw), F32)
        cv_ref[pad:pad + tm, :] = _dot(h, wb_ref[:, 0:cw])
        taps = cw_ref.shape[0]
        rb = 256

        def conv(piece, cs):
            cols = slice(cs * LANES, (cs + 1) * LANES)
            r0 = pad + piece * rb - (taps - 1)
            acc = cw_ref[0:1, cols] * cv_ref[pl.ds(r0, rb), cols]
            for i in range(1, taps):
                acc = acc + cw_ref[i:i + 1, cols] * cv_ref[pl.ds(r0 + i, rb), cols]
            scale = ML_DK ** -0.5 if cs >= cw // LANES // 2 else 1.0
            zb_ref[piece * rb:(piece + 1) * rb, cols] = (_silu(acc) * scale).astype(zb_ref.dtype)

        todo = [(piece, cs) for piece in range(tm // rb) for cs in range(cw // LANES)]
        chunks = list(range(cw, zb_ref.shape[1], MXU_N))
        per = -(-len(todo) // len(chunks))
        for n, c0 in enumerate(chunks):
            c1 = min(c0 + MXU_N, zb_ref.shape[1])
            zb_ref[:, c0:c1] = _dot(h, wb_ref[:, c0:c1]).astype(zb_ref.dtype)
            for piece, cs in todo[n * per:(n + 1) * per]:
                conv(piece, cs)


def _inproj(x, nw, wb, wf, conv_w, conv_blk, tm, tnb):
    T, D = x.shape
    return pl.pallas_call(
        functools.partial(_inproj_kernel, conv_blk),
        grid=(T // tm, wb.shape[1] // tnb),
        in_specs=[pl.BlockSpec((tm, D), lambda i, j: (i, 0)),
                  pl.BlockSpec((1, D), lambda i, j: (0, 0)),
                  pl.BlockSpec((D, tnb), lambda i, j: (0, j)),
                  pl.BlockSpec(wf.shape, lambda i, j: (0, 0)),
                  pl.BlockSpec(conv_w.shape, lambda i, j: (0, 0))],
        out_specs=[pl.BlockSpec((tm, tnb), lambda i, j: (i, j)),
                   pl.BlockSpec((tm, wf.shape[1]), lambda i, j: (i, 0))],
        out_shape=[jax.ShapeDtypeStruct((T, wb.shape[1]), BF16),
                   jax.ShapeDtypeStruct((T, wf.shape[1]), F32)],
        scratch_shapes=[pltpu.VMEM((tm, D), BF16), pltpu.VMEM((tm + 8, conv_w.shape[1]), F32)],
        compiler_params=_cparams("parallel", "arbitrary"),
        name="inproj",
    )(x, nw, wb, wf, conv_w)


def _retention_kernel(z_ref, cos_ref, sin_ref, nw_ref, o_ref, dec_ref, st_ref):
    S = z_ref.shape[0]
    n_chunks = S // CHUNK
    H = RET_HEADS
    HW = H * LANES
    row = lax.broadcasted_iota(jnp.int32, (CHUNK, CHUNK), 0).astype(F32)
    col = lax.broadcasted_iota(jnp.int32, (CHUNK, CHUNK), 1).astype(F32)
    lgs = [math.log1p(-(2.0 ** (-5.0 - h))) for h in range(H)]
    for h, lg in enumerate(lgs):
        rel = row - col
        dec_ref[h] = jnp.where(rel >= 0, jnp.exp(lg * jnp.maximum(rel, 0.0)), 0.0)
        dec_ref[H + h] = jnp.exp(lg * (row + 1.0))
        dec_ref[2 * H + h] = jnp.exp(lg * (CHUNK - 1.0 - row)) * (LANES ** -0.5)
    st_ref[...] = jnp.zeros(st_ref.shape, F32)

    def body(n, carry):
        r0 = pl.multiple_of(n * CHUNK, CHUNK)
        rows = pl.ds(r0, CHUNK)
        cos = cos_ref[rows, :]
        sin = sin_ref[rows, :]
        qbs, vs, ss, iqs, kvs = [], [], [], [], []
        for h in range(H):
            q = z_ref[rows, h * LANES:(h + 1) * LANES].astype(F32)
            k = z_ref[rows, HW + h * LANES:HW + (h + 1) * LANES].astype(F32)
            v = z_ref[rows, 2 * HW + h * LANES:2 * HW + (h + 1) * LANES]
            q = q * cos + pltpu.roll(q, LANES // 2, 1) * sin
            k = k * cos + pltpu.roll(k, LANES // 2, 1) * sin
            qb = q.astype(BF16)
            kb = (k * (LANES ** -0.5)).astype(BF16)
            kd = (k * dec_ref[2 * H + h]).astype(BF16)
            ss.append(_dot_nt(qb, kb))
            iqs.append(_dot(qb, st_ref[h].astype(BF16)))
            kvs.append(_dot_tn(kd, v))
            vs.append(v)
        ps = [(ss[h] * dec_ref[h]).astype(BF16) for h in range(H)]
        os_ = [_dot(ps[h], vs[h]) + iqs[h] * dec_ref[H + h] for h in range(H)]
        for h in range(H):
            st_ref[h] = st_ref[h] * math.exp(lgs[h] * CHUNK) + kvs[h]
            o = os_[h]
            ms = jnp.mean(o * o, axis=-1, keepdims=True)
            o = o * lax.rsqrt(ms + EPS) * nw_ref[:, h * LANES:(h + 1) * LANES]
            g = z_ref[rows, 3 * HW + h * LANES:3 * HW + (h + 1) * LANES].astype(F32)
            o_ref[rows, h * LANES:(h + 1) * LANES] = (o * _silu(g)).astype(o_ref.dtype)
        return carry

    lax.fori_loop(0, n_chunks, body, 0)


def _retention(zb, cos, sin, nw, B, S):
    T = B * S
    W = RET_HEADS * LANES
    return pl.pallas_call(
        _retention_kernel,
        grid=(B,),
        in_specs=[pl.BlockSpec((S, 4 * W), lambda b: (b, 0)),
                  pl.BlockSpec((S, LANES), lambda b: (0, 0)),
                  pl.BlockSpec((S, LANES), lambda b: (0, 0)),
                  pl.BlockSpec((1, W), lambda b: (0, 0))],
        out_specs=pl.BlockSpec((S, W), lambda b: (b, 0)),
        out_shape=jax.ShapeDtypeStruct((T, W), BF16),
        scratch_shapes=[pltpu.VMEM((3 * RET_HEADS, CHUNK, LANES), F32),
                        pltpu.VMEM((RET_HEADS, LANES, LANES), F32)],
        compiler_params=_cparams("parallel"),
        name="retention",
    )(zb, cos, sin, nw)


def _da_kernel(*refs):
    z_refs = refs[:9]
    wq_ref, wk_ref, o_ref, qn_ref, kn_ref, v_ref, np_ref, lp_ref, mp_ref = refs[9:18]
    nn_refs, ln_refs, mn_refs = refs[18:21], refs[21:24], refs[24:27]
    stage_refs = refs[27:30]
    S = o_ref.shape[0]
    lane = lax.broadcasted_iota(jnp.int32, (1, LANES), 1)
    lo = lane < DA_DH
    row = lax.broadcasted_iota(jnp.int32, (CHUNK, CHUNK), 0)
    col = lax.broadcasted_iota(jnp.int32, (CHUNK, CHUNK), 1)
    mask_cur = col <= row
    row2 = lax.broadcasted_iota(jnp.int32, (CHUNK, 2 * CHUNK), 0)
    col2 = lax.broadcasted_iota(jnp.int32, (CHUNK, 2 * CHUNK), 1)
    mask_band = (col2 >= row2) & (col2 <= row2 + CHUNK)

    seg = (lax.broadcasted_iota(jnp.int32, (LANES, LANES), 0) // DA_DH
           == lax.broadcasted_iota(jnp.int32, (LANES, LANES), 1) // DA_DH).astype(BF16) * (1.0 / DA_DH)

    def head_norm(x, w):
        x2 = x * x
        hi = x2.astype(BF16)
        lo_part = (x2 - hi.astype(F32)).astype(BF16)
        ms = _dot(hi, seg) + _dot(lo_part, seg)
        return x * lax.rsqrt(ms + EPS) * w

    RB = 256
    for g, (dil, nb) in enumerate(DA_GROUPS):
        L = S // dil
        zq_ref, zk_ref, zv_ref = z_refs[3 * g:3 * g + 3]
        if dil > 1:
            for piece in range(S // RB):
                rows = pl.ds(piece * RB, RB)
                for src_ref, dst_ref in zip((zq_ref, zk_ref, zv_ref), stage_refs):
                    dst_ref[rows, :] = src_ref[rows, :].astype(F32)
            zq_ref, zk_ref, zv_ref = stage_refs
        for rho in range(dil):
            for piece in range(max(L // RB, 1)):
                n_rows = min(RB, L)
                src = pl.ds(rho + dil * piece * n_rows, n_rows, stride=dil) if dil > 1 else pl.ds(
                    piece * n_rows, n_rows)
                dst = pl.ds(rho * L + piece * n_rows, n_rows)
                q = zq_ref[src, :].astype(F32)
                k = zk_ref[src, :].astype(F32)
                v = zv_ref[src, :]
                qn_ref[dst, :] = (head_norm(q, wq_ref[...]) * (DA_DH ** -0.5)).astype(BF16)
                kn_ref[dst, :] = head_norm(k, wk_ref[...]).astype(BF16)
                v_ref[dst, :] = v.astype(BF16)

        def batch(r0, chained, first_has_prev):
            chains = []
            for u in range(DA_UNROLL):
                rows = pl.ds(r0 + u * CHUNK, CHUNK)
                if chained and (u > 0 or first_has_prev):
                    keys, mask = pl.ds(r0 + (u - 1) * CHUNK, 2 * CHUNK), mask_band
                else:
                    keys, mask = rows, mask_cur
                k = kn_ref[keys, :]
                v = v_ref[keys, :]
                vaug = jnp.concatenate([v, jnp.ones_like(v)], axis=1)
                q = qn_ref[rows, :]
                zero = jnp.zeros_like(q)
                for qh in (jnp.where(lo, q, zero), jnp.where(lo, zero, q)):
                    chains.append((_dot_nt(qh, k), mask, vaug))
            probs = []
            for s, mask, _ in chains:
                s = jnp.where(mask, s, NEG)
                m = jnp.max(s, axis=-1, keepdims=True)
                probs.append((jnp.exp(s - m).astype(BF16), m))
            accs = [_dot(p, vaug) for (p, _), (_, _, vaug) in zip(probs, chains)]
            nums, dens, maxs = [], [], []
            for u in range(DA_UNROLL):
                a0, a1 = accs[2 * u], accs[2 * u + 1]
                nums.append(jnp.where(lo, a0[:, :LANES], a1[:, :LANES]))
                dens.append(jnp.where(lo, a0[:, LANES:], a1[:, LANES:]))
                maxs.append(jnp.where(lo, probs[2 * u][1], probs[2 * u + 1][1]))
            rows = pl.ds(r0, DA_UNROLL * CHUNK)
            np_ref[rows, :] = jnp.concatenate(nums, axis=0)
            lp_ref[rows, :] = jnp.concatenate(dens, axis=0)
            mp_ref[rows, :] = jnp.concatenate(maxs, axis=0)

        span = DA_UNROLL * CHUNK
        if nb == 1:
            def singles(i, c):
                batch(pl.multiple_of(i * span, span), False, False)
                return c
            lax.fori_loop(0, S // span, singles, 0)
        else:
            def segment(sgi, c, nb=nb):
                base = pl.multiple_of(sgi * (nb * CHUNK), span)
                batch(base, True, False)
                if nb > DA_UNROLL:
                    def inner(n, c2):
                        batch(pl.multiple_of(base + n * span, span), True, True)
                        return c2
                    lax.fori_loop(1, nb // DA_UNROLL, inner, 0)
                return c
            lax.fori_loop(0, dil, segment, 0)

        for rho in range(dil):
            for piece in range(max(L // RB, 1)):
                n_rows = min(RB, L)
                dst = pl.ds(rho + dil * piece * n_rows, n_rows, stride=dil) if dil > 1 else pl.ds(
                    piece * n_rows, n_rows)
                src = pl.ds(rho * L + piece * n_rows, n_rows)
                nn_refs[g][dst, :] = np_ref[src, :]
                ln_refs[g][dst, :] = lp_ref[src, :]
                mn_refs[g][dst, :] = mp_ref[src, :]

    for piece in range(S // RB):
        rows = pl.ds(piece * RB, RB)
        m = jnp.maximum(jnp.maximum(mn_refs[0][rows, :], mn_refs[1][rows, :]), mn_refs[2][rows, :])
        num = jnp.zeros((RB, LANES), F32)
        den = jnp.zeros((RB, LANES), F32)
        for g in range(3):
            e = jnp.exp(mn_refs[g][rows, :] - m)
            num = num + e * nn_refs[g][rows, :]
            den = den + e * ln_refs[g][rows, :]
        o_ref[rows, :] = (num / den).astype(o_ref.dtype)


def _dilated_attention(zb, slab0, wq, wk, B, S):
    T = B * S
    slab = lambda k: pl.BlockSpec((S, LANES), lambda b, p: (b, slab0 + 9 * p + k))
    return pl.pallas_call(
        _da_kernel,
        grid=(B, 2),
        in_specs=[slab(k) for k in range(9)] + [
                  pl.BlockSpec((1, LANES), lambda b, p: (0, 0)),
                  pl.BlockSpec((1, LANES), lambda b, p: (0, 0))],
        out_specs=pl.BlockSpec((S, LANES), lambda b, p: (b, p)),
        out_shape=jax.ShapeDtypeStruct((T, 2 * LANES), BF16),
        scratch_shapes=[pltpu.VMEM((S, LANES), BF16), pltpu.VMEM((S, LANES), BF16),
                        pltpu.VMEM((S, LANES), BF16),
                        pltpu.VMEM((S, LANES), F32), pltpu.VMEM((S, LANES), F32),
                        pltpu.VMEM((S, LANES), F32)] + [pltpu.VMEM((S, LANES), F32)] * 12,
        compiler_params=_cparams("parallel", "arbitrary"),
        name="dilated_attention",
    )(*([zb] * 9), wq, wk)


def _mlstm_kernel(qk_ref, v_ref, og_ref, gate_ref, gb_ref, nw_ref, o_ref, st_ref, e_ref, et_ref):
    S = qk_ref.shape[0] // ML_SEQS
    n_chunks = S // CHUNK
    W = qk_ref.shape[1]
    lane = lax.broadcasted_iota(jnp.int32, (1, LANES), 1)
    lo = lane < ML_DK
    row = lax.broadcasted_iota(jnp.int32, (CHUNK, CHUNK), 0)
    col = lax.broadcasted_iota(jnp.int32, (CHUNK, CHUNK), 1)
    causal = col <= row
    tril = causal.astype(F32)
    srow = lax.broadcasted_iota(jnp.int32, (LANES, 1), 0) < ML_DK
    st_ref[...] = jnp.zeros(st_ref.shape, F32)

    is_f = (lane >= ML_HEADS) & (lane < 2 * ML_HEADS)
    for n in range(ML_SEQS * n_chunks):
        rows = pl.ds(n * CHUNK, CHUNK)
        gp = gate_ref[rows, :] + gb_ref[...]
        logf = jnp.minimum(gp, 0.0) - jnp.log(1.0 + jnp.exp(-jnp.abs(gp)))
        cum = jnp.dot(tril, jnp.where(is_f, logf, 0.0), preferred_element_type=F32,
                      precision=lax.Precision.HIGHEST)
        e = jnp.where(lane < ML_HEADS, gp, cum)
        e_ref[rows, :] = e
        et_ref[n] = e.T[0:2 * ML_HEADS, :]

    units = [(sq, h) for sq in range(ML_SEQS) for h in range(ML_HEADS)]

    def chunk(n, mms):
        rows_of = [pl.ds(pl.multiple_of(sq * S + n * CHUNK, CHUNK), CHUNK) for sq in range(ML_SEQS)]
        es = [e_ref[rows_of[sq], :] for sq in range(ML_SEQS)]
        ets = [et_ref[sq * n_chunks + n] for sq in range(ML_SEQS)]
        khs, vaugs, s_raws, iqs = [], [], [], []
        for sq in range(ML_SEQS):
            rows = rows_of[sq]
            for pair in range(2):
                q2 = qk_ref[rows, pair * LANES:(pair + 1) * LANES]
                k2 = qk_ref[rows, W // 2 + pair * LANES:W // 2 + (pair + 1) * LANES]
                zero = jnp.zeros_like(q2)
                stb = st_ref[2 * sq + pair].astype(BF16)
                for hs in range(2):
                    h = 2 * pair + hs
                    qh = jnp.where(lo, q2, zero) if hs == 0 else jnp.where(lo, zero, q2)
                    kh = jnp.where(lo, k2, zero) if hs == 0 else jnp.where(lo, zero, k2)
                    vh = v_ref[rows, h * LANES:(h + 1) * LANES]
                    s_raws.append(_dot_nt(qh, k2))
                    iqs.append(_dot(qh, stb))
                    khs.append(kh)
                    vaugs.append(jnp.concatenate([vh, jnp.ones_like(vh)], axis=-1))
        ps, wis, ms_, kws, wcs, new_mms = [], [], [], [], [], []
        for u, (sq, h) in enumerate(units):
            e, et = es[sq], ets[sq]
            mm = mms[u]
            i_b = jnp.broadcast_to(e[:, h:h + 1], (CHUNK, LANES))
            a_b = jnp.broadcast_to(e[:, ML_HEADS + h:ML_HEADS + h + 1], (CHUNK, LANES))
            c_row = et[h:h + 1, :] - et[ML_HEADS + h:ML_HEADS + h + 1, :]
            dmat = jnp.where(causal, a_b + c_row, NEG)
            inter = a_b + mm
            m = jnp.maximum(inter, jnp.max(dmat, axis=-1, keepdims=True))
            ps.append((s_raws[u] * jnp.exp(dmat - m)).astype(BF16))
            wis.append(jnp.exp(inter - m))
            ms_.append(m)
            m_new = m[CHUNK - 1:CHUNK, :]
            a_last = a_b[CHUNK - 1:CHUNK, :]
            wk = jnp.exp(a_last - a_b + i_b - m_new)
            kws.append((khs[u].astype(F32) * wk).astype(BF16))
            wc = jnp.exp(a_last + mm - m_new)
            wcs.append(jnp.concatenate([wc, wc], axis=1))
            new_mms.append(m_new)
        accs = [_dot(ps[u], vaugs[u]) + jnp.concatenate([wis[u], wis[u]], axis=1) * iqs[u]
                for u in range(len(units))]
        upds = [_dot_tn(kws[u], vaugs[u]) for u in range(len(units))]
        for sp in range(2 * ML_SEQS):
            u0, u1 = 2 * sp, 2 * sp + 1
            st_ref[sp] = jnp.where(srow, wcs[u0], wcs[u1]) * st_ref[sp] + (upds[u0] + upds[u1])
        for u, (sq, h) in enumerate(units):
            num = accs[u][:, :LANES]
            den = accs[u][:, LANES:]
            hv = num / jnp.maximum(jnp.abs(den), jnp.exp(-ms_[u]))
            var = jnp.mean(hv * hv, axis=-1, keepdims=True)
            hv = hv * lax.rsqrt(var + EPS) * nw_ref[:, h * LANES:(h + 1) * LANES]
            og = og_ref[rows_of[sq], h * LANES:(h + 1) * LANES].astype(F32)
            o_ref[rows_of[sq], h * LANES:(h + 1) * LANES] = (hv * _sigmoid(og)).astype(o_ref.dtype)
        return tuple(new_mms)

    lax.fori_loop(0, n_chunks, chunk, tuple(jnp.zeros((1, LANES), F32) for _ in units))


def _mlstm(zb, zf, gate_bias, nw, B, S, qk_blk, gate_blk):
    T = B * S
    W = ML_HEADS * LANES
    R = ML_SEQS * S
    return pl.pallas_call(
        _mlstm_kernel,
        grid=(B // ML_SEQS,),
        in_specs=[pl.BlockSpec((R, W), lambda b: (b, qk_blk)),
                  pl.BlockSpec((R, W), lambda b: (b, qk_blk + 1)),
                  pl.BlockSpec((R, W), lambda b: (b, qk_blk + 2)),
                  pl.BlockSpec((R, LANES), lambda b: (b, gate_blk)),
                  pl.BlockSpec((1, LANES), lambda b: (0, 0)),
                  pl.BlockSpec((1, W), lambda b: (0, 0))],
        out_specs=pl.BlockSpec((R, W), lambda b: (b, 0)),
        out_shape=jax.ShapeDtypeStruct((T, W), BF16),
        scratch_shapes=[pltpu.VMEM((2 * ML_SEQS, LANES, 2 * LANES), F32),
                        pltpu.VMEM((R, LANES), F32),
                        pltpu.VMEM((R // CHUNK, 2 * ML_HEADS, LANES), F32)],
        compiler_params=_cparams("parallel"),
        name="mlstm",
    )(zb, zb, zb, zf, gate_bias, nw)


def _merge_kernel(x_ref, oret_ref, oda_ref, oml_ref, gr_ref, gd_ref, gm_ref,
                  wr_ref, wd_ref, wm_ref, wo_ref, o_ref):
    y = _sigmoid(gr_ref[...].astype(F32)) * _dot(oret_ref[...], wr_ref[...])
    y = y + _sigmoid(gd_ref[...].astype(F32)) * _dot(oda_ref[...], wd_ref[...])
    y = y + _sigmoid(gm_ref[...].astype(F32)) * _dot(oml_ref[...], wm_ref[...])
    o_ref[...] = x_ref[...] + _dot(y.astype(BF16), wo_ref[...])


def _merge(x, o_ret, o_da, o_ml, zb, g_blk, wr, wd, wm, wo, tm):
    T, D = x.shape
    full = lambda a: pl.BlockSpec(a.shape, lambda i: (0, 0))
    return pl.pallas_call(
        _merge_kernel,
        grid=(T // tm,),
        in_specs=[pl.BlockSpec((tm, D), lambda i: (i, 0)),
                  pl.BlockSpec((tm, o_ret.shape[1]), lambda i: (i, 0)),
                  pl.BlockSpec((tm, o_da.shape[1]), lambda i: (i, 0)),
                  pl.BlockSpec((tm, o_ml.shape[1]), lambda i: (i, 0)),
                  pl.BlockSpec((tm, D), lambda i: (i, g_blk)),
                  pl.BlockSpec((tm, D), lambda i: (i, g_blk + 1)),
                  pl.BlockSpec((tm, D), lambda i: (i, g_blk + 2)),
                  full(wr), full(wd), full(wm), full(wo)],
        out_specs=pl.BlockSpec((tm, D), lambda i: (i, 0)),
        out_shape=jax.ShapeDtypeStruct((T, D), F32),
        compiler_params=_cparams("parallel"),
        name="merge_outproj",
    )(x, o_ret, o_da, o_ml, zb, zb, zb, wr, wd, wm, wo)


def _swiglu_block(h, wg_ref, wu_ref, wd_ref, a_ref):
    tf = wg_ref.shape[1]
    c0 = 0
    while c0 < tf:
        w = min(MXU_N, tf - c0)
        g = _dot(h, wg_ref[:, c0:c0 + w].astype(BF16))
        u = _dot(h, wu_ref[:, c0:c0 + w].astype(BF16))
        a_ref[:, c0:c0 + w] = (_silu(g) * u).astype(BF16)
        c0 += w
    return _dot(a_ref[...], wd_ref[...].astype(BF16))


def _ffn_kernel(x_ref, nw_ref, wg_ref, wu_ref, wd_ref, o_ref, h_ref, acc_ref, a_ref):
    j = pl.program_id(1)

    @pl.when(j == 0)
    def _():
        x = x_ref[...]
        ms = jnp.mean(x * x, axis=-1, keepdims=True)
        h_ref[...] = (x * lax.rsqrt(ms + EPS) * nw_ref[...]).astype(BF16)
        acc_ref[...] = x

    acc_ref[...] += _swiglu_block(h_ref[...], wg_ref, wu_ref, wd_ref, a_ref)

    @pl.when(j == pl.num_programs(1) - 1)
    def _():
        o_ref[...] = acc_ref[...]


def _ffn(x, nw, wg, wu, wd, tm, tf):
    T, D = x.shape
    F = wg.shape[1]
    return pl.pallas_call(
        _ffn_kernel,
        grid=(T // tm, F // tf),
        in_specs=[pl.BlockSpec((tm, D), lambda i, j: (i, 0)),
                  pl.BlockSpec((1, D), lambda i, j: (0, 0)),
                  pl.BlockSpec((D, tf), lambda i, j: (0, j)),
                  pl.BlockSpec((D, tf), lambda i, j: (0, j)),
                  pl.BlockSpec((tf, D), lambda i, j: (j, 0))],
        out_specs=pl.BlockSpec((tm, D), lambda i, j: (i, 0)),
        out_shape=jax.ShapeDtypeStruct((T, D), F32),
        scratch_shapes=[pltpu.VMEM((tm, D), BF16), pltpu.VMEM((tm, D), F32),
                        pltpu.VMEM((tm, tf), BF16)],
        compiler_params=_cparams("parallel", "arbitrary"),
        name="swiglu_ffn",
    )(x, nw, wg, wu, wd)


HI16 = 0xFFFF0000


def _pack_bf16_pairs(a, b):
    ab = pltpu.bitcast(a.astype(BF16).astype(F32), jnp.uint32)
    bb = pltpu.bitcast(b.astype(BF16).astype(F32), jnp.uint32)
    return (ab >> 16) | (bb & jnp.uint32(HI16))


def _unpack_bf16_pairs(p):
    lo = pltpu.bitcast(p << 16, F32)
    hi = pltpu.bitcast(p & jnp.uint32(HI16), F32)
    return lo, hi


def _router_kernel(x_ref, nw_ref, wr_ref, br_ref, ha_ref, hb_ref, sel_ref):
    x = x_ref[...]
    ms = jnp.mean(x * x, axis=-1, keepdims=True)
    h = x * lax.rsqrt(ms + EPS) * nw_ref[...]
    Q = h.shape[1] // 4
    ha_ref[...] = _pack_bf16_pairs(h[:, 0:Q], h[:, Q:2 * Q])
    hb_ref[...] = _pack_bf16_pairs(h[:, 2 * Q:3 * Q], h[:, 3 * Q:4 * Q])
    logits = jnp.dot(h, wr_ref[...], preferred_element_type=F32,
                     precision=lax.Precision.HIGHEST) + br_ref[...]
    lane = lax.broadcasted_iota(jnp.int32, logits.shape, 1).astype(F32)
    logits = jnp.where(lane < N_EXPERTS, logits, NEG)
    m1 = jnp.max(logits, axis=-1, keepdims=True)
    i1 = jnp.min(jnp.where(logits == m1, lane, float(LANES)), axis=-1, keepdims=True)
    rest = jnp.where(lane == i1, NEG, logits)
    m2 = jnp.max(rest, axis=-1, keepdims=True)
    i2 = jnp.min(jnp.where(rest == m2, lane, float(LANES)), axis=-1, keepdims=True)
    e2 = jnp.exp(m2 - m1)
    p1 = 1.0 / (1.0 + e2)
    p2 = e2 / (1.0 + e2)
    sel_ref[...] = jnp.where(lane == 0.0, i1, jnp.where(lane == 1.0, i2,
                             jnp.where(lane == 2.0, p1, jnp.where(lane == 3.0, p2, 0.0))))


def _router(x, nw, wr, br, tm):
    T, D = x.shape
    Q = D // 4
    return pl.pallas_call(
        _router_kernel,
        grid=(T // tm,),
        in_specs=[pl.BlockSpec((tm, D), lambda i: (i, 0)),
                  pl.BlockSpec((1, D), lambda i: (0, 0)),
                  pl.BlockSpec((D, LANES), lambda i: (0, 0)),
                  pl.BlockSpec((1, LANES), lambda i: (0, 0))],
        out_specs=[pl.BlockSpec((tm, Q), lambda i: (i, 0)),
                   pl.BlockSpec((tm, Q), lambda i: (i, 0)),
                   pl.BlockSpec((tm, LANES), lambda i: (i, 0))],
        out_shape=[jax.ShapeDtypeStruct((T, Q), jnp.uint32), jax.ShapeDtypeStruct((T, Q), jnp.uint32),
                   jax.ShapeDtypeStruct((T, LANES), F32)],
        compiler_params=_cparams("parallel"),
        name="moe_router",
    )(x, nw, wr, br)


def _moe_rank_kernel(sel_ref, pos_ref, meta_ref, cnt_ref, offs_ref, carry_ref, before_ref):
    ph = pl.program_id(0)
    i = pl.program_id(1)
    tm = sel_ref.shape[0]
    lane = lax.broadcasted_iota(jnp.int32, (tm, LANES), 1).astype(F32)
    lane1 = lax.broadcasted_iota(jnp.int32, (1, LANES), 1).astype(F32)
    sel = sel_ref[...]
    i1 = sel[:, 0:1]
    i2 = sel[:, 1:2]
    onehot = jnp.where((lane == i1) | (lane == i2), 1.0, 0.0)
    colsum = jnp.sum(onehot, axis=0, keepdims=True)

    @pl.when((ph == 0) & (i == 0))
    def _():
        cnt_ref[...] = jnp.zeros(cnt_ref.shape, F32)

    @pl.when(ph == 0)
    def _():
        cnt_ref[...] += colsum

    def padded_counts():
        return jnp.floor((cnt_ref[...] + (MOE_TILE - 1.0)) * (1.0 / MOE_TILE)) * MOE_TILE

    @pl.when((ph == 1) & (i == 0))
    def _():
        k = lax.broadcasted_iota(jnp.int32, (LANES, LANES), 0)
        e = lax.broadcasted_iota(jnp.int32, (LANES, LANES), 1)
        upper = (k < e).astype(F32)
        offs_ref[...] = jnp.dot(padded_counts(), upper, preferred_element_type=F32,
                                precision=lax.Precision.HIGHEST)
        carry_ref[...] = jnp.zeros(carry_ref.shape, F32)
        r = lax.broadcasted_iota(jnp.int32, (tm, tm), 0)
        c = lax.broadcasted_iota(jnp.int32, (tm, tm), 1)
        before_ref[...] = (c < r).astype(BF16)

    @pl.when(ph == 1)
    def _():
        rank = _dot(before_ref[...], onehot.astype(BF16)) + carry_ref[0:1, :]
        row = rank + offs_ref[0:1, :]
        pos1 = jnp.sum(jnp.where(lane == i1, row, 0.0), axis=-1, keepdims=True)
        pos2 = jnp.sum(jnp.where(lane == i2, row, 0.0), axis=-1, keepdims=True)
        both = jnp.where(lane == 0.0, pos1, jnp.where(lane == 1.0, pos2, 0.0))
        pos_ref[...] = both.T[0:8, :].astype(jnp.int32)
        carry_ref[...] += colsum

    @pl.when((ph == 1) & (i == pl.num_programs(1) - 1))
    def _():
        nrow = meta_ref.shape[0]
        padded = padded_counts()[0:1, :]
        offs = offs_ref[0:1, :]
        ends = offs + padded
        start = lax.broadcasted_iota(jnp.int32, (nrow, 1), 0).astype(F32) * MOE_TILE
        is_e = lane1 < N_EXPERTS
        te = jnp.sum(jnp.where(is_e & (ends <= start), 1.0, 0.0), axis=-1, keepdims=True)
        te = jnp.minimum(te, N_EXPERTS - 1.0)
        valid_end = jnp.sum(jnp.where(lane1 == te, offs + cnt_ref[0:1, :], 0.0), axis=-1, keepdims=True)
        nv = jnp.clip(valid_end - start, 0.0, MOE_TILE)
        total = jnp.sum(jnp.where(is_e, padded, 0.0), axis=-1, keepdims=True)
        active = jnp.where(start < total, 1.0, 0.0)
        meta_ref[...] = jnp.where(lane1 == 0.0, te, jnp.where(lane1 == 1.0, nv,
                                  jnp.where(lane1 == 2.0, active, 0.0)))


def _moe_rank(sel, tm, meta_rows):
    T = sel.shape[0]
    nt = T // tm
    return pl.pallas_call(
        _moe_rank_kernel,
        grid=(2, nt),
        in_specs=[pl.BlockSpec((tm, LANES), lambda ph, i: (i, 0))],
        out_specs=[pl.BlockSpec((None, 8, tm), lambda ph, i: (i * ph, 0, 0)),
                   pl.BlockSpec((meta_rows, LANES), lambda ph, i: (0, 0))],
        out_shape=[jax.ShapeDtypeStruct((nt, 8, tm), jnp.int32),
                   jax.ShapeDtypeStruct((meta_rows, LANES), F32)],
        scratch_shapes=[pltpu.VMEM((8, LANES), F32), pltpu.VMEM((8, LANES), F32),
                        pltpu.VMEM((8, LANES), F32), pltpu.VMEM((tm, tm), BF16)],
        compiler_params=_cparams("arbitrary", "arbitrary"),
        name="moe_rank",
    )(sel)


def _sc_mesh():
    return plsc.VectorSubcoreMesh(core_axis_name="core", subcore_axis_name="subcore")


def _sc_scatter_rows(x, idx, n_rows):
    n_idx = idx.shape[1]
    nt = x.shape[0] // SC_WINDOW
    width = x.shape[1]

    @functools.partial(pl.kernel, out_type=jax.ShapeDtypeStruct((n_rows, width), x.dtype),
                       mesh=_sc_mesh())
    def scatter_kernel(x_hbm, i_hbm, o_hbm):
        def body(x_vmem, i_vmem):
            pltpu.sync_copy(x_vmem, o_hbm.at[i_vmem.at[0]])

        pltpu.emit_pipeline(
            body, grid=(n_idx // SC_WINDOW,),
            in_specs=[pl.BlockSpec((SC_WINDOW, width), lambda i: (i % nt, 0)),
                      pl.BlockSpec((1, SC_WINDOW), lambda i: (0, i))],
            out_specs=[],
            core_axis_name=("core", "subcore"), dimension_semantics=(pltpu.PARALLEL,),
        )(x_hbm, i_hbm)

    return scatter_kernel(x, idx)


def _sc_gather_rows(x, idx):
    n_idx = idx.shape[1]
    width = x.shape[1]

    @functools.partial(pl.kernel, out_type=jax.ShapeDtypeStruct((n_idx, width), x.dtype),
                       mesh=_sc_mesh())
    def gather_kernel(x_hbm, i_hbm, o_hbm):
        def body(i_vmem, o_vmem):
            pltpu.sync_copy(x_hbm.at[i_vmem.at[0]], o_vmem)

        pltpu.emit_pipeline(
            body, grid=(n_idx // SC_WINDOW,),
            in_specs=[pl.BlockSpec((1, SC_WINDOW), lambda i: (0, i))],
            out_specs=[pl.BlockSpec((SC_WINDOW, width), lambda i: (i, 0))],
            core_axis_name=("core", "subcore"), dimension_semantics=(pltpu.PARALLEL,),
        )(i_hbm, o_hbm)

    return gather_kernel(x, idx)


def _moe_group_kernel(te_ref, nv_ref, na_ref, xa_ref, xb_ref, wg_ref, wu_ref, wd_ref,
                      ya_ref, yb_ref, h_ref, acc_ref, a_ref):
    j = pl.program_id(0)
    f = pl.program_id(1)

    @pl.when(j < na_ref[0])
    def _():
        @pl.when(f == 0)
        def _():
            tm = h_ref.shape[0]
            Q = xa_ref.shape[1]
            valid = lax.broadcasted_iota(jnp.int32, (tm, 1), 0) < nv_ref[j]
            for src, c0 in ((xa_ref, 0), (xb_ref, 2 * Q)):
                lo, hi = _unpack_bf16_pairs(src[...])
                h_ref[:, c0:c0 + Q] = jnp.where(valid, lo, 0.0).astype(BF16)
                h_ref[:, c0 + Q:c0 + 2 * Q] = jnp.where(valid, hi, 0.0).astype(BF16)
            acc_ref[...] = jnp.zeros(acc_ref.shape, F32)

        acc_ref[...] += _swiglu_block(h_ref[...], wg_ref, wu_ref, wd_ref, a_ref)

        @pl.when(f == pl.num_programs(1) - 1)
        def _():
            Q = ya_ref.shape[1]
            ya_ref[...] = _pack_bf16_pairs(acc_ref[:, 0:Q], acc_ref[:, Q:2 * Q])
            yb_ref[...] = _pack_bf16_pairs(acc_ref[:, 2 * Q:3 * Q], acc_ref[:, 3 * Q:4 * Q])


def _moe_group(te, nv, na, xa, xb, wg, wu, wd, tf):
    R, Q = xa.shape
    E, D, F = wg.shape
    nf = F // tf
    tile = lambda j, f, te, nv, na: (jnp.minimum(j, na[0] - 1), 0)
    ff = lambda j, f, na: jnp.where(j < na[0], f, nf - 1)
    grid_spec = pltpu.PrefetchScalarGridSpec(
        num_scalar_prefetch=3,
        grid=(R // MOE_TILE, nf),
        in_specs=[pl.BlockSpec((MOE_TILE, Q), tile),
                  pl.BlockSpec((MOE_TILE, Q), tile),
                  pl.BlockSpec((None, D, tf), lambda j, f, te, nv, na: (te[j], 0, ff(j, f, na))),
                  pl.BlockSpec((None, D, tf), lambda j, f, te, nv, na: (te[j], 0, ff(j, f, na))),
                  pl.BlockSpec((None, tf, D), lambda j, f, te, nv, na: (te[j], ff(j, f, na), 0))],
        out_specs=[pl.BlockSpec((MOE_TILE, Q), tile), pl.BlockSpec((MOE_TILE, Q), tile)],
        scratch_shapes=[pltpu.VMEM((MOE_TILE, D), BF16), pltpu.VMEM((MOE_TILE, D), F32),
                        pltpu.VMEM((MOE_TILE, tf), BF16)],
    )
    return pl.pallas_call(
        _moe_group_kernel,
        grid_spec=grid_spec,
        out_shape=[jax.ShapeDtypeStruct((R, Q), jnp.uint32), jax.ShapeDtypeStruct((R, Q), jnp.uint32)],
        compiler_params=_cparams("arbitrary", "arbitrary"),
        name="moe_experts",
    )(te, nv, na, xa, xb, wg, wu, wd)


def _moe_combine_kernel(x_ref, sel_ref, a1_ref, a2_ref, b1_ref, b2_ref, o_ref):
    sel = sel_ref[...]
    p1 = sel[:, 2:3]
    p2 = sel[:, 3:4]
    Q = a1_ref.shape[1]
    for first, second, c0 in ((a1_ref, a2_ref, 0), (b1_ref, b2_ref, 2 * Q)):
        lo1, hi1 = _unpack_bf16_pairs(first[...])
        lo2, hi2 = _unpack_bf16_pairs(second[...])
        o_ref[:, c0:c0 + Q] = x_ref[:, c0:c0 + Q] + (p1 * lo1 + p2 * lo2)
        o_ref[:, c0 + Q:c0 + 2 * Q] = x_ref[:, c0 + Q:c0 + 2 * Q] + (p1 * hi1 + p2 * hi2)


def _moe_combine(x, sel, ga, gb, tm):
    T, D = x.shape
    Q = ga.shape[1]
    nt = T // tm
    return pl.pallas_call(
        _moe_combine_kernel,
        grid=(nt,),
        in_specs=[pl.BlockSpec((tm, D), lambda i: (i, 0)),
                  pl.BlockSpec((tm, LANES), lambda i: (i, 0)),
                  pl.BlockSpec((tm, Q), lambda i: (i, 0)),
                  pl.BlockSpec((tm, Q), lambda i: (i + nt, 0)),
                  pl.BlockSpec((tm, Q), lambda i: (i, 0)),
                  pl.BlockSpec((tm, Q), lambda i: (i + nt, 0))],
        out_specs=pl.BlockSpec((tm, D), lambda i: (i, 0)),
        out_shape=jax.ShapeDtypeStruct((T, D), F32),
        compiler_params=_cparams("parallel"),
        name="moe_combine",
    )(x, sel, ga, ga, gb, gb)


def _cast_kernel(x_ref, o_ref):
    o_ref[...] = x_ref[...].astype(o_ref.dtype)


def _cast_bf16(w, rows):
    cols = w.shape[-1]
    w2 = w.reshape(-1, cols)
    out = pl.pallas_call(
        _cast_kernel,
        grid=(w2.shape[0] // rows,),
        in_specs=[pl.BlockSpec((rows, cols), lambda i: (i, 0))],
        out_specs=pl.BlockSpec((rows, cols), lambda i: (i, 0)),
        out_shape=jax.ShapeDtypeStruct(w2.shape, BF16),
        compiler_params=_cparams("parallel"),
        name="cast_bf16",
    )(w2)
    return out.reshape(w.shape)


def _moe(x, nw, w_router, b_router, wg, wu, wd):
    T, D = x.shape
    wr = jnp.pad(w_router, ((0, 0), (0, LANES - N_EXPERTS)))
    br = jnp.pad(b_router, (0, LANES - N_EXPERTS)).reshape(1, LANES)
    ha, hb, sel = _router(x, nw, wr, br, 1024)
    n_rows = TOP_K * T + N_EXPERTS * MOE_TILE
    n_tiles = n_rows // MOE_TILE
    pos, meta = _moe_rank(sel, 1024, 256)
    idx = jnp.concatenate([pos[:, 0, :].reshape(1, T), pos[:, 1, :].reshape(1, T)], axis=1)
    te = meta[:n_tiles, 0].astype(jnp.int32)
    nv = meta[:n_tiles, 1].astype(jnp.int32)
    na = jnp.sum(meta[:n_tiles, 2]).astype(jnp.int32).reshape(1)
    te = jnp.where(jnp.arange(n_tiles) < na[0], te, te[na[0] - 1])
    xa = _sc_scatter_rows(ha, idx, n_rows)
    xb = _sc_scatter_rows(hb, idx, n_rows)
    ya, yb = _moe_group(te, nv, na, xa, xb, wg, wu, wd, wg.shape[2] // 2)
    ga = _sc_gather_rows(ya, idx)
    gb = _sc_gather_rows(yb, idx)
    return _moe_combine(x, sel, ga, gb, 1024)


def _split_w_in(w_in):
    sizes = (512, 512, 512, 512, 768, 768, 768, 256, 256, 512, 512, 4, 4, 1024, 1024, 1024)
    offs = [0]
    for s in sizes:
        offs.append(offs[-1] + s)
    part = lambda i: w_in[..., offs[i]:offs[i + 1]]
    rq, rk, rv, rg, dq, dk, dv, mq, mk, mv, mo, mi, mf, g_ret, g_da, g_ml = (part(i) for i in range(16))
    cols = []
    for p in range(2):
        for g in range(3):
            for t in (dq, dk, dv):
                cols.append(t[..., g * 256 + p * 128:g * 256 + (p + 1) * 128])
    wb = jnp.concatenate([rq, rk, rv, rg, g_ret, g_da, g_ml, mq, mk, mv, mo] + cols,
                         axis=-1).astype(BF16)
    pad = jnp.zeros(w_in.shape[:-1] + (LANES - 8,), w_in.dtype)
    wf = jnp.concatenate([mi, mf, pad], axis=-1).astype(BF16)
    return wb, wf


def _rope_tables(S):
    half = LANES // 2
    inv = jnp.power(ROPE_BASE, -jnp.arange(half, dtype=F32) / half)
    ang = jnp.arange(S, dtype=F32)[:, None] * inv[None, :]
    cos = jnp.cos(ang)
    sin = jnp.sin(ang)
    return jnp.concatenate([cos, cos], axis=1), jnp.concatenate([-sin, sin], axis=1)


def kernel(x, norm1_w, w_in, ret_norm_w, da_q_norm_w, da_k_norm_w, ml_conv_w, ml_i_bias, ml_f_bias,
           ml_norm_w, w_br_ret, w_br_da, w_br_ml, w_out, norm2_w, ffn_w_gate, ffn_w_up, ffn_w_down,
           moe_w_router, moe_b_router, moe_w_gate, moe_w_up, moe_w_down):
    B, S, D = x.shape
    T = B * S
    depth = w_in.shape[0]
    cos, sin = _rope_tables(S)
    wb_all, wf_all = _split_w_in(w_in)
    xt = x.reshape(T, D)
    for layer in range(depth):
        wb, wf = wb_all[layer], wf_all[layer]
        nw1 = norm1_w[layer].reshape(1, D)
        zb, zf = _inproj(xt, nw1, wb, wf, ml_conv_w[layer], 4, S, 1280)
        o_ret = _retention(zb, cos, sin, ret_norm_w[layer].reshape(1, -1), B, S)
        wq = jnp.tile(da_q_norm_w[layer], 2).reshape(1, LANES)
        wk = jnp.tile(da_k_norm_w[layer], 2).reshape(1, LANES)
        o_da = _dilated_attention(zb, 52, wq, wk, B, S)
        gate_bias = jnp.concatenate([ml_i_bias[layer], ml_f_bias[layer],
                                     jnp.zeros((LANES - 2 * ML_HEADS,), F32)]).reshape(1, LANES)
        o_ml = _mlstm(zb, zf, gate_bias, ml_norm_w[layer].reshape(1, -1), B, S,
                      qk_blk=10, gate_blk=0)
        xt = _merge(xt, o_ret, o_da, o_ml, zb, 2,
                    w_br_ret[layer].astype(BF16), w_br_da[layer].astype(BF16),
                    w_br_ml[layer].astype(BF16), w_out[layer].astype(BF16), 512)
        nw2 = norm2_w[layer].reshape(1, D)
        j = layer // 2
        if layer % 2 == 0:
            xt = _ffn(xt, nw2, ffn_w_gate[j].astype(BF16), ffn_w_up[j].astype(BF16),
                      ffn_w_down[j].astype(BF16), 1024, 1408)
        else:
            xt = _moe(xt, nw2, moe_w_router[j], moe_b_router[j], _cast_bf16(moe_w_gate[j], 512),
                      _cast_bf16(moe_w_up[j], 512), _cast_bf16(moe_w_down[j], 2048))
    return xt.reshape(B, S, D)
```

```python
import functools
import math

import jax
import jax.numpy as jnp
from jax import lax
from jax.experimental import pallas as pl
from jax.experimental.pallas import tpu as pltpu
from jax.experimental.pallas import tpu_sc as plsc

F32 = jnp.float32
BF16 = jnp.bfloat16

EPS = 1e-6
D_MODEL = 1024
CHUNK = 128
LANES = 128
MXU_N = 256
ROPE_BASE = 10000.0
RET_HEADS = 4
RET_SEQS = 2
DA_GROUPS = ((1, 16), (4, 4), (16, 1))
DA_DH = 64
DA_UNROLL = 4
ML_HEADS = 4
ML_DK = 64
ML_SEQS = 2
N_EXPERTS = 8
TOP_K = 2
MOE_TILE = 512
SC_WINDOW = 128
VMEM_LIMIT = 56 * 1024 * 1024

NEG = -1e30


def _cparams(*sem):
    return pltpu.CompilerParams(dimension_semantics=sem, vmem_limit_bytes=VMEM_LIMIT)


def _dot(a, b):
    return jnp.dot(a, b, preferred_element_type=F32)


def _dot_nt(a, b):
    return lax.dot_general(a, b, (((1,), (1,)), ((), ())), preferred_element_type=F32)


def _dot_tn(a, b):
    return lax.dot_general(a, b, (((0,), (0,)), ((), ())), preferred_element_type=F32)


def _sigmoid(x):
    return 1.0 / (1.0 + jnp.exp(-x))


def _silu(x):
    return x * _sigmoid(x)


def _inproj_kernel(conv_blk, x_ref, nw_ref, wb_ref, wf_ref, cw_ref, zb_ref, zf_ref, h_ref, cv_ref):
    j = pl.program_id(1)
    tm = x_ref.shape[0]

    @pl.when(j == 0)
    def _():
        rc = 512
        for r in range(tm // rc):
            rows = pl.ds(r * rc, rc)
            x = x_ref[rows, :]
            ms = jnp.mean(x * x, axis=-1, keepdims=True)
            h = (x * lax.rsqrt(ms + EPS) * nw_ref[...]).astype(BF16)
            h_ref[rows, :] = h
            zb_ref[rows, :] = _dot(h, wb_ref[...]).astype(zb_ref.dtype)
            zf_ref[rows, :] = _dot(h, wf_ref[...])

    @pl.when((j > 0) & (j != conv_blk))
    def _():
        zb_ref[...] = _dot(h_ref[...], wb_ref[...]).astype(zb_ref.dtype)

    @pl.when(j == conv_blk)
    def _():
        h = h_ref[...]
        cw = cw_ref.shape[1]
        pad = cv_ref.shape[0] - tm
        cv_ref[0:pad, :] = jnp.zeros((pad, cw), F32)
        cv_ref[pad:pad + tm, :] = _dot(h, wb_ref[:, 0:cw])
        taps = cw_ref.shape[0]
        rb = 256

        def conv(piece, cs):
            cols = slice(cs * LANES, (cs + 1) * LANES)
            r0 = pad + piece * rb - (taps - 1)
            acc = cw_ref[0:1, cols] * cv_ref[pl.ds(r0, rb), cols]
            for i in range(1, taps):
                acc = acc + cw_ref[i:i + 1, cols] * cv_ref[pl.ds(r0 + i, rb), cols]
            scale = ML_DK ** -0.5 if cs >= cw // LANES // 2 else 1.0
            zb_ref[piece * rb:(piece + 1) * rb, cols] = (_silu(acc) * scale).astype(zb_ref.dtype)

        todo = [(piece, cs) for piece in range(tm // rb) for cs in range(cw // LANES)]
        chunks = list(range(cw, zb_ref.shape[1], MXU_N))
        per = -(-len(todo) // len(chunks))
        for n, c0 in enumerate(chunks):
            c1 = min(c0 + MXU_N, zb_ref.shape[1])
            zb_ref[:, c0:c1] = _dot(h, wb_ref[:, c0:c1]).astype(zb_ref.dtype)
            for piece, cs in todo[n * per:(n + 1) * per]:
                conv(piece, cs)


def _inproj(x, nw, wb, wf, conv_w, conv_blk, tm, tnb):
    T, D = x.shape
    return pl.pallas_call(
        functools.partial(_inproj_kernel, conv_blk),
        grid=(T // tm, wb.shape[1] // tnb),
        in_specs=[pl.BlockSpec((tm, D), lambda i, j: (i, 0)),
                  pl.BlockSpec((1, D), lambda i, j: (0, 0)),
                  pl.BlockSpec((D, tnb), lambda i, j: (0, j)),
                  pl.BlockSpec(wf.shape, lambda i, j: (0, 0)),
                  pl.BlockSpec(conv_w.shape, lambda i, j: (0, 0))],
        out_specs=[pl.BlockSpec((tm, tnb), lambda i, j: (i, j)),
                   pl.BlockSpec((tm, wf.shape[1]), lambda i, j: (i, 0))],
        out_shape=[jax.ShapeDtypeStruct((T, wb.shape[1]), BF16),
                   jax.ShapeDtypeStruct((T, wf.shape[1]), F32)],
        scratch_shapes=[pltpu.VMEM((tm, D), BF16), pltpu.VMEM((tm + 8, conv_w.shape[1]), F32)],
        compiler_params=_cparams("parallel", "arbitrary"),
        name="inproj",
    )(x, nw, wb, wf, conv_w)


def _retention_kernel(z_ref, cos_ref, sin_ref, nw_ref, o_ref, dec_ref, st_ref):
    S = z_ref.shape[0] // RET_SEQS
    n_chunks = S // CHUNK
    H = RET_HEADS
    HW = H * LANES
    row = lax.broadcasted_iota(jnp.int32, (CHUNK, CHUNK), 0).astype(F32)
    col = lax.broadcasted_iota(jnp.int32, (CHUNK, CHUNK), 1).astype(F32)
    lgs = [math.log1p(-(2.0 ** (-5.0 - h))) for h in range(H)]
    for h, lg in enumerate(lgs):
        rel = row - col
        dec_ref[h] = jnp.where(rel >= 0, jnp.exp(lg * jnp.maximum(rel, 0.0)), 0.0)
        dec_ref[H + h] = jnp.exp(lg * (row + 1.0))
        dec_ref[2 * H + h] = jnp.exp(lg * (CHUNK - 1.0 - row)) * (LANES ** -0.5)
    st_ref[...] = jnp.zeros(st_ref.shape, F32)
    units = [(sq, h) for sq in range(RET_SEQS) for h in range(H)]

    def body(n, carry):
        r0 = pl.multiple_of(n * CHUNK, CHUNK)
        cos = cos_ref[pl.ds(r0, CHUNK), :]
        sin = sin_ref[pl.ds(r0, CHUNK), :]
        rows_of = [pl.ds(pl.multiple_of(sq * S + n * CHUNK, CHUNK), CHUNK) for sq in range(RET_SEQS)]
        vs, ss, iqs, kvs = [], [], [], []
        for u, (sq, h) in enumerate(units):
            rows = rows_of[sq]
            q = z_ref[rows, h * LANES:(h + 1) * LANES].astype(F32)
            k = z_ref[rows, HW + h * LANES:HW + (h + 1) * LANES].astype(F32)
            v = z_ref[rows, 2 * HW + h * LANES:2 * HW + (h + 1) * LANES]
            q = q * cos + pltpu.roll(q, LANES // 2, 1) * sin
            k = k * cos + pltpu.roll(k, LANES // 2, 1) * sin
            qb = q.astype(BF16)
            kb = (k * (LANES ** -0.5)).astype(BF16)
            kd = (k * dec_ref[2 * H + h]).astype(BF16)
            ss.append(_dot_nt(qb, kb))
            iqs.append(_dot(qb, st_ref[u].astype(BF16)))
            kvs.append(_dot_tn(kd, v))
            vs.append(v)
        ps = [(ss[u] * dec_ref[h]).astype(BF16) for u, (_, h) in enumerate(units)]
        os_ = [_dot(ps[u], vs[u]) + iqs[u] * dec_ref[H + h] for u, (_, h) in enumerate(units)]
        for u, (sq, h) in enumerate(units):
            rows = rows_of[sq]
            st_ref[u] = st_ref[u] * math.exp(lgs[h] * CHUNK) + kvs[u]
            o = os_[u]
            ms = jnp.mean(o * o, axis=-1, keepdims=True)
            o = o * lax.rsqrt(ms + EPS) * nw_ref[:, h * LANES:(h + 1) * LANES]
            g = z_ref[rows, 3 * HW + h * LANES:3 * HW + (h + 1) * LANES].astype(F32)
            o_ref[rows, h * LANES:(h + 1) * LANES] = (o * _silu(g)).astype(o_ref.dtype)
        return carry

    lax.fori_loop(0, n_chunks, body, 0)


def _retention(zb, cos, sin, nw, B, S):
    T = B * S
    W = RET_HEADS * LANES
    R = RET_SEQS * S
    return pl.pallas_call(
        _retention_kernel,
        grid=(B // RET_SEQS,),
        in_specs=[pl.BlockSpec((R, 4 * W), lambda b: (b, 0)),
                  pl.BlockSpec((S, LANES), lambda b: (0, 0)),
                  pl.BlockSpec((S, LANES), lambda b: (0, 0)),
                  pl.BlockSpec((1, W), lambda b: (0, 0))],
        out_specs=pl.BlockSpec((R, W), lambda b: (b, 0)),
        out_shape=jax.ShapeDtypeStruct((T, W), BF16),
        scratch_shapes=[pltpu.VMEM((3 * RET_HEADS, CHUNK, LANES), F32),
                        pltpu.VMEM((RET_SEQS * RET_HEADS, LANES, LANES), F32)],
        compiler_params=_cparams("parallel"),
        name="retention",
    )(zb, cos, sin, nw)


def _da_kernel(*refs):
    z_refs = refs[:9]
    wq_ref, wk_ref, o_ref, qn_ref, kn_ref, v_ref, np_ref, lp_ref, mp_ref = refs[9:18]
    nn_refs, ln_refs, mn_refs = refs[18:21], refs[21:24], refs[24:27]
    stage_refs = refs[27:30]
    S = o_ref.shape[0]
    lane = lax.broadcasted_iota(jnp.int32, (1, LANES), 1)
    lo = lane < DA_DH
    row = lax.broadcasted_iota(jnp.int32, (CHUNK, CHUNK), 0)
    col = lax.broadcasted_iota(jnp.int32, (CHUNK, CHUNK), 1)
    mask_cur = col <= row
    row2 = lax.broadcasted_iota(jnp.int32, (CHUNK, 2 * CHUNK), 0)
    col2 = lax.broadcasted_iota(jnp.int32, (CHUNK, 2 * CHUNK), 1)
    mask_band = (col2 >= row2) & (col2 <= row2 + CHUNK)

    seg = (lax.broadcasted_iota(jnp.int32, (LANES, LANES), 0) // DA_DH
           == lax.broadcasted_iota(jnp.int32, (LANES, LANES), 1) // DA_DH).astype(BF16) * (1.0 / DA_DH)

    def head_norm(x, w):
        x2 = x * x
        hi = x2.astype(BF16)
        lo_part = (x2 - hi.astype(F32)).astype(BF16)
        ms = _dot(hi, seg) + _dot(lo_part, seg)
        return x * lax.rsqrt(ms + EPS) * w

    RB = 256
    for g, (dil, nb) in enumerate(DA_GROUPS):
        L = S // dil
        zq_ref, zk_ref, zv_ref = z_refs[3 * g:3 * g + 3]
        if dil > 1:
            for piece in range(S // RB):
                rows = pl.ds(piece * RB, RB)
                for src_ref, dst_ref in zip((zq_ref, zk_ref, zv_ref), stage_refs):
                    dst_ref[rows, :] = src_ref[rows, :].astype(F32)
            zq_ref, zk_ref, zv_ref = stage_refs
        for rho in range(dil):
            for piece in range(max(L // RB, 1)):
                n_rows = min(RB, L)
                src = pl.ds(rho + dil * piece * n_rows, n_rows, stride=dil) if dil > 1 else pl.ds(
                    piece * n_rows, n_rows)
                dst = pl.ds(rho * L + piece * n_rows, n_rows)
                q = zq_ref[src, :].astype(F32)
                k = zk_ref[src, :].astype(F32)
                v = zv_ref[src, :]
                qn_ref[dst, :] = (head_norm(q, wq_ref[...]) * (DA_DH ** -0.5)).astype(BF16)
                kn_ref[dst, :] = head_norm(k, wk_ref[...]).astype(BF16)
                v_ref[dst, :] = v.astype(BF16)

        def batch(r0, chained, first_has_prev):
            chains = []
            for u in range(DA_UNROLL):
                rows = pl.ds(r0 + u * CHUNK, CHUNK)
                if chained and (u > 0 or first_has_prev):
                    keys, mask = pl.ds(r0 + (u - 1) * CHUNK, 2 * CHUNK), mask_band
                else:
                    keys, mask = rows, mask_cur
                k = kn_ref[keys, :]
                v = v_ref[keys, :]
                vaug = jnp.concatenate([v, jnp.ones_like(v)], axis=1)
                q = qn_ref[rows, :]
                zero = jnp.zeros_like(q)
                for qh in (jnp.where(lo, q, zero), jnp.where(lo, zero, q)):
                    chains.append((_dot_nt(qh, k), mask, vaug))
            probs = []
            for s, mask, _ in chains:
                s = jnp.where(mask, s, NEG)
                m = jnp.max(s, axis=-1, keepdims=True)
                probs.append((jnp.exp(s - m).astype(BF16), m))
            accs = [_dot(p, vaug) for (p, _), (_, _, vaug) in zip(probs, chains)]
            nums, dens, maxs = [], [], []
            for u in range(DA_UNROLL):
                a0, a1 = accs[2 * u], accs[2 * u + 1]
                nums.append(jnp.where(lo, a0[:, :LANES], a1[:, :LANES]))
                dens.append(jnp.where(lo, a0[:, LANES:], a1[:, LANES:]))
                maxs.append(jnp.where(lo, probs[2 * u][1], probs[2 * u + 1][1]))
            rows = pl.ds(r0, DA_UNROLL * CHUNK)
            np_ref[rows, :] = jnp.concatenate(nums, axis=0)
            lp_ref[rows, :] = jnp.concatenate(dens, axis=0)
            mp_ref[rows, :] = jnp.concatenate(maxs, axis=0)

        span = DA_UNROLL * CHUNK
        if nb == 1:
            def singles(i, c):
                batch(pl.multiple_of(i * span, span), False, False)
                return c
            lax.fori_loop(0, S // span, singles, 0)
        else:
            def segment(sgi, c, nb=nb):
                base = pl.multiple_of(sgi * (nb * CHUNK), span)
                batch(base, True, False)
                if nb > DA_UNROLL:
                    def inner(n, c2):
                        batch(pl.multiple_of(base + n * span, span), True, True)
                        return c2
                    lax.fori_loop(1, nb // DA_UNROLL, inner, 0)
                return c
            lax.fori_loop(0, dil, segment, 0)

        for rho in range(dil):
            for piece in range(max(L // RB, 1)):
                n_rows = min(RB, L)
                dst = pl.ds(rho + dil * piece * n_rows, n_rows, stride=dil) if dil > 1 else pl.ds(
                    piece * n_rows, n_rows)
                src = pl.ds(rho * L + piece * n_rows, n_rows)
                nn_refs[g][dst, :] = np_ref[src, :]
                ln_refs[g][dst, :] = lp_ref[src, :]
                mn_refs[g][dst, :] = mp_ref[src, :]

    for piece in range(S // RB):
        rows = pl.ds(piece * RB, RB)
        m = jnp.maximum(jnp.maximum(mn_refs[0][rows, :], mn_refs[1][rows, :]), mn_refs[2][rows, :])
        num = jnp.zeros((RB, LANES), F32)
        den = jnp.zeros((RB, LANES), F32)
        for g in range(3):
            e = jnp.exp(mn_refs[g][rows, :] - m)
            num = num + e * nn_refs[g][rows, :]
            den = den + e * ln_refs[g][rows, :]
        o_ref[rows, :] = (num / den).astype(o_ref.dtype)


def _dilated_attention(zb, slab0, wq, wk, B, S):
    T = B * S
    slab = lambda k: pl.BlockSpec((S, LANES), lambda b, p: (b, slab0 + 9 * p + k))
    return pl.pallas_call(
        _da_kernel,
        grid=(B, 2),
        in_specs=[slab(k) for k in range(9)] + [
                  pl.BlockSpec((1, LANES), lambda b, p: (0, 0)),
                  pl.BlockSpec((1, LANES), lambda b, p: (0, 0))],
        out_specs=pl.BlockSpec((S, LANES), lambda b, p: (b, p)),
        out_shape=jax.ShapeDtypeStruct((T, 2 * LANES), BF16),
        scratch_shapes=[pltpu.VMEM((S, LANES), BF16), pltpu.VMEM((S, LANES), BF16),
                        pltpu.VMEM((S, LANES), BF16),
                        pltpu.VMEM((S, LANES), F32), pltpu.VMEM((S, LANES), F32),
                        pltpu.VMEM((S, LANES), F32)] + [pltpu.VMEM((S, LANES), F32)] * 12,
        compiler_params=_cparams("parallel", "arbitrary"),
        name="dilated_attention",
    )(*([zb] * 9), wq, wk)


def _mlstm_kernel(qk_ref, v_ref, og_ref, gate_ref, gb_ref, nw_ref, o_ref, st_ref, e_ref, et_ref):
    S = qk_ref.shape[0] // ML_SEQS
    n_chunks = S // CHUNK
    W = qk_ref.shape[1]
    lane = lax.broadcasted_iota(jnp.int32, (1, LANES), 1)
    lo = lane < ML_DK
    row = lax.broadcasted_iota(jnp.int32, (CHUNK, CHUNK), 0)
    col = lax.broadcasted_iota(jnp.int32, (CHUNK, CHUNK), 1)
    causal = col <= row
    tril = causal.astype(F32)
    srow = lax.broadcasted_iota(jnp.int32, (LANES, 1), 0) < ML_DK
    st_ref[...] = jnp.zeros(st_ref.shape, F32)

    is_f = (lane >= ML_HEADS) & (lane < 2 * ML_HEADS)
    for n in range(ML_SEQS * n_chunks):
        rows = pl.ds(n * CHUNK, CHUNK)
        gp = gate_ref[rows, :] + gb_ref[...]
        logf = jnp.minimum(gp, 0.0) - jnp.log(1.0 + jnp.exp(-jnp.abs(gp)))
        cum = jnp.dot(tril, jnp.where(is_f, logf, 0.0), preferred_element_type=F32,
                      precision=lax.Precision.HIGHEST)
        e = jnp.where(lane < ML_HEADS, gp, cum)
        e_ref[rows, :] = e
        et_ref[n] = e.T[0:2 * ML_HEADS, :]

    units = [(sq, h) for sq in range(ML_SEQS) for h in range(ML_HEADS)]

    def chunk(n, mms):
        rows_of = [pl.ds(pl.multiple_of(sq * S + n * CHUNK, CHUNK), CHUNK) for sq in range(ML_SEQS)]
        es = [e_ref[rows_of[sq], :] for sq in range(ML_SEQS)]
        ets = [et_ref[sq * n_chunks + n] for sq in range(ML_SEQS)]
        khs, vaugs, s_raws, iqs = [], [], [], []
        for sq in range(ML_SEQS):
            rows = rows_of[sq]
            for pair in range(2):
                q2 = qk_ref[rows, pair * LANES:(pair + 1) * LANES]
                k2 = qk_ref[rows, W // 2 + pair * LANES:W // 2 + (pair + 1) * LANES]
                zero = jnp.zeros_like(q2)
                stb = st_ref[2 * sq + pair].astype(BF16)
                for hs in range(2):
                    h = 2 * pair + hs
                    qh = jnp.where(lo, q2, zero) if hs == 0 else jnp.where(lo, zero, q2)
                    kh = jnp.where(lo, k2, zero) if hs == 0 else jnp.where(lo, zero, k2)
                    vh = v_ref[rows, h * LANES:(h + 1) * LANES]
                    s_raws.append(_dot_nt(qh, k2))
                    iqs.append(_dot(qh, stb))
                    khs.append(kh)
                    vaugs.append(jnp.concatenate([vh, jnp.ones_like(vh)], axis=-1))
        ps, wis, ms_, kws, wcs, new_mms = [], [], [], [], [], []
        for u, (sq, h) in enumerate(units):
            e, et = es[sq], ets[sq]
            mm = mms[u]
            i_b = jnp.broadcast_to(e[:, h:h + 1], (CHUNK, LANES))
            a_b = jnp.broadcast_to(e[:, ML_HEADS + h:ML_HEADS + h + 1], (CHUNK, LANES))
            c_row = et[h:h + 1, :] - et[ML_HEADS + h:ML_HEADS + h + 1, :]
            dmat = jnp.where(causal, a_b + c_row, NEG)
            inter = a_b + mm
            m = jnp.maximum(inter, jnp.max(dmat, axis=-1, keepdims=True))
            ps.append((s_raws[u] * jnp.exp(dmat - m)).astype(BF16))
            wis.append(jnp.exp(inter - m))
            ms_.append(m)
            m_new = m[CHUNK - 1:CHUNK, :]
            a_last = a_b[CHUNK - 1:CHUNK, :]
            wk = jnp.exp(a_last - a_b + i_b - m_new)
            kws.append((khs[u].astype(F32) * wk).astype(BF16))
            wc = jnp.exp(a_last + mm - m_new)
            wcs.append(jnp.concatenate([wc, wc], axis=1))
            new_mms.append(m_new)
        accs = [_dot(ps[u], vaugs[u]) + jnp.concatenate([wis[u], wis[u]], axis=1) * iqs[u]
                for u in range(len(units))]
        upds = [_dot_tn(kws[u], vaugs[u]) for u in range(len(units))]
        for sp in range(2 * ML_SEQS):
            u0, u1 = 2 * sp, 2 * sp + 1
            st_ref[sp] = jnp.where(srow, wcs[u0], wcs[u1]) * st_ref[sp] + (upds[u0] + upds[u1])
        for u, (sq, h) in enumerate(units):
            num = accs[u][:, :LANES]
            den = accs[u][:, LANES:]
            hv = num / jnp.maximum(jnp.abs(den), jnp.exp(-ms_[u]))
            var = jnp.mean(hv * hv, axis=-1, keepdims=True)
            hv = hv * lax.rsqrt(var + EPS) * nw_ref[:, h * LANES:(h + 1) * LANES]
            og = og_ref[rows_of[sq], h * LANES:(h + 1) * LANES].astype(F32)
            o_ref[rows_of[sq], h * LANES:(h + 1) * LANES] = (hv * _sigmoid(og)).astype(o_ref.dtype)
        return tuple(new_mms)

    lax.fori_loop(0, n_chunks, chunk, tuple(jnp.zeros((1, LANES), F32) for _ in units))


def _mlstm(zb, zf, gate_bias, nw, B, S, qk_blk, gate_blk):
    T = B * S
    W = ML_HEADS * LANES
    R = ML_SEQS * S
    return pl.pallas_call(
        _mlstm_kernel,
        grid=(B // ML_SEQS,),
        in_specs=[pl.BlockSpec((R, W), lambda b: (b, qk_blk)),
                  pl.BlockSpec((R, W), lambda b: (b, qk_blk + 1)),
                  pl.BlockSpec((R, W), lambda b: (b, qk_blk + 2)),
                  pl.BlockSpec((R, LANES), lambda b: (b, gate_blk)),
                  pl.BlockSpec((1, LANES), lambda b: (0, 0)),
                  pl.BlockSpec((1, W), lambda b: (0, 0))],
        out_specs=pl.BlockSpec((R, W), lambda b: (b, 0)),
        out_shape=jax.ShapeDtypeStruct((T, W), BF16),
        scratch_shapes=[pltpu.VMEM((2 * ML_SEQS, LANES, 2 * LANES), F32),
                        pltpu.VMEM((R, LANES), F32),
                        pltpu.VMEM((R // CHUNK, 2 * ML_HEADS, LANES), F32)],
        compiler_params=_cparams("parallel"),
        name="mlstm",
    )(zb, zb, zb, zf, gate_bias, nw)


def _merge_kernel(x_ref, oret_ref, oda_ref, oml_ref, gr_ref, gd_ref, gm_ref,
                  wr_ref, wd_ref, wm_ref, wo_ref, o_ref):
    y = _sigmoid(gr_ref[...].astype(F32)) * _dot(oret_ref[...], wr_ref[...])
    y = y + _sigmoid(gd_ref[...].astype(F32)) * _dot(oda_ref[...], wd_ref[...])
    y = y + _sigmoid(gm_ref[...].astype(F32)) * _dot(oml_ref[...], wm_ref[...])
    o_ref[...] = x_ref[...] + _dot(y.astype(BF16), wo_ref[...])


def _merge(x, o_ret, o_da, o_ml, zb, g_blk, wr, wd, wm, wo, tm):
    T, D = x.shape
    full = lambda a: pl.BlockSpec(a.shape, lambda i: (0, 0))
    return pl.pallas_call(
        _merge_kernel,
        grid=(T // tm,),
        in_specs=[pl.BlockSpec((tm, D), lambda i: (i, 0)),
                  pl.BlockSpec((tm, o_ret.shape[1]), lambda i: (i, 0)),
                  pl.BlockSpec((tm, o_da.shape[1]), lambda i: (i, 0)),
                  pl.BlockSpec((tm, o_ml.shape[1]), lambda i: (i, 0)),
                  pl.BlockSpec((tm, D), lambda i: (i, g_blk)),
                  pl.BlockSpec((tm, D), lambda i: (i, g_blk + 1)),
                  pl.BlockSpec((tm, D), lambda i: (i, g_blk + 2)),
                  full(wr), full(wd), full(wm), full(wo)],
        out_specs=pl.BlockSpec((tm, D), lambda i: (i, 0)),
        out_shape=jax.ShapeDtypeStruct((T, D), F32),
        compiler_params=_cparams("parallel"),
        name="merge_outproj",
    )(x, o_ret, o_da, o_ml, zb, zb, zb, wr, wd, wm, wo)


def _swiglu_block(h, wg_ref, wu_ref, wd_ref, a_ref):
    tf = wg_ref.shape[1]
    c0 = 0
    while c0 < tf:
        w = min(MXU_N, tf - c0)
        g = _dot(h, wg_ref[:, c0:c0 + w].astype(BF16))
        u = _dot(h, wu_ref[:, c0:c0 + w].astype(BF16))
        a_ref[:, c0:c0 + w] = (_silu(g) * u).astype(BF16)
        c0 += w
    return _dot(a_ref[...], wd_ref[...].astype(BF16))


def _ffn_kernel(x_ref, nw_ref, wg_ref, wu_ref, wd_ref, o_ref, h_ref, acc_ref, a_ref):
    j = pl.program_id(1)

    @pl.when(j == 0)
    def _():
        x = x_ref[...]
        ms = jnp.mean(x * x, axis=-1, keepdims=True)
        h_ref[...] = (x * lax.rsqrt(ms + EPS) * nw_ref[...]).astype(BF16)
        acc_ref[...] = x

    acc_ref[...] += _swiglu_block(h_ref[...], wg_ref, wu_ref, wd_ref, a_ref)

    @pl.when(j == pl.num_programs(1) - 1)
    def _():
        o_ref[...] = acc_ref[...]


def _ffn(x, nw, wg, wu, wd, tm, tf):
    T, D = x.shape
    F = wg.shape[1]
    return pl.pallas_call(
        _ffn_kernel,
        grid=(T // tm, F // tf),
        in_specs=[pl.BlockSpec((tm, D), lambda i, j: (i, 0)),
                  pl.BlockSpec((1, D), lambda i, j: (0, 0)),
                  pl.BlockSpec((D, tf), lambda i, j: (0, j)),
                  pl.BlockSpec((D, tf), lambda i, j: (0, j)),
                  pl.BlockSpec((tf, D), lambda i, j: (j, 0))],
        out_specs=pl.BlockSpec((tm, D), lambda i, j: (i, 0)),
        out_shape=jax.ShapeDtypeStruct((T, D), F32),
        scratch_shapes=[pltpu.VMEM((tm, D), BF16), pltpu.VMEM((tm, D), F32),
                        pltpu.VMEM((tm, tf), BF16)],
        compiler_params=_cparams("parallel", "arbitrary"),
        name="swiglu_ffn",
    )(x, nw, wg, wu, wd)


HI16 = 0xFFFF0000


def _pack_bf16_pairs(a, b):
    ab = pltpu.bitcast(a.astype(BF16).astype(F32), jnp.uint32)
    bb = pltpu.bitcast(b.astype(BF16).astype(F32), jnp.uint32)
    return (ab >> 16) | (bb & jnp.uint32(HI16))


def _unpack_bf16_pairs(p):
    lo = pltpu.bitcast(p << 16, F32)
    hi = pltpu.bitcast(p & jnp.uint32(HI16), F32)
    return lo, hi


def _router_kernel(x_ref, nw_ref, wr_ref, br_ref, ha_ref, hb_ref, sel_ref):
    x = x_ref[...]
    ms = jnp.mean(x * x, axis=-1, keepdims=True)
    h = x * lax.rsqrt(ms + EPS) * nw_ref[...]
    Q = h.shape[1] // 4
    ha_ref[...] = _pack_bf16_pairs(h[:, 0:Q], h[:, Q:2 * Q])
    hb_ref[...] = _pack_bf16_pairs(h[:, 2 * Q:3 * Q], h[:, 3 * Q:4 * Q])
    w = wr_ref[...]
    h_hi = h.astype(BF16)
    h_lo = (h - h_hi.astype(F32)).astype(BF16)
    w_hi = w.astype(BF16)
    w_lo = (w - w_hi.astype(F32)).astype(BF16)
    logits = _dot(h_hi, w_hi) + _dot(h_lo, w_hi) + _dot(h_hi, w_lo) + br_ref[...]
    lane = lax.broadcasted_iota(jnp.int32, logits.shape, 1).astype(F32)
    logits = jnp.where(lane < N_EXPERTS, logits, NEG)
    m1 = jnp.max(logits, axis=-1, keepdims=True)
    i1 = jnp.min(jnp.where(logits == m1, lane, float(LANES)), axis=-1, keepdims=True)
    rest = jnp.where(lane == i1, NEG, logits)
    m2 = jnp.max(rest, axis=-1, keepdims=True)
    i2 = jnp.min(jnp.where(rest == m2, lane, float(LANES)), axis=-1, keepdims=True)
    e2 = jnp.exp(m2 - m1)
    p1 = 1.0 / (1.0 + e2)
    p2 = e2 / (1.0 + e2)
    sel_ref[...] = jnp.where(lane == 0.0, i1, jnp.where(lane == 1.0, i2,
                             jnp.where(lane == 2.0, p1, jnp.where(lane == 3.0, p2, 0.0))))


def _router(x, nw, wr, br, tm):
    T, D = x.shape
    Q = D // 4
    return pl.pallas_call(
        _router_kernel,
        grid=(T // tm,),
        in_specs=[pl.BlockSpec((tm, D), lambda i: (i, 0)),
                  pl.BlockSpec((1, D), lambda i: (0, 0)),
                  pl.BlockSpec((D, LANES), lambda i: (0, 0)),
                  pl.BlockSpec((1, LANES), lambda i: (0, 0))],
        out_specs=[pl.BlockSpec((tm, Q), lambda i: (i, 0)),
                   pl.BlockSpec((tm, Q), lambda i: (i, 0)),
                   pl.BlockSpec((tm, LANES), lambda i: (i, 0))],
        out_shape=[jax.ShapeDtypeStruct((T, Q), jnp.uint32), jax.ShapeDtypeStruct((T, Q), jnp.uint32),
                   jax.ShapeDtypeStruct((T, LANES), F32)],
        compiler_params=_cparams("parallel"),
        name="moe_router",
    )(x, nw, wr, br)


def _moe_rank_kernel(sel_ref, pos_ref, meta_ref, cnt_ref, offs_ref, carry_ref, before_ref):
    ph = pl.program_id(0)
    i = pl.program_id(1)
    tm = sel_ref.shape[0]
    lane = lax.broadcasted_iota(jnp.int32, (tm, LANES), 1).astype(F32)
    lane1 = lax.broadcasted_iota(jnp.int32, (1, LANES), 1).astype(F32)
    sel = sel_ref[...]
    i1 = sel[:, 0:1]
    i2 = sel[:, 1:2]
    onehot = jnp.where((lane == i1) | (lane == i2), 1.0, 0.0)
    colsum = jnp.sum(onehot, axis=0, keepdims=True)

    @pl.when((ph == 0) & (i == 0))
    def _():
        cnt_ref[...] = jnp.zeros(cnt_ref.shape, F32)

    @pl.when(ph == 0)
    def _():
        cnt_ref[...] += colsum

    def padded_counts():
        return jnp.floor((cnt_ref[...] + (MOE_TILE - 1.0)) * (1.0 / MOE_TILE)) * MOE_TILE

    @pl.when((ph == 1) & (i == 0))
    def _():
        k = lax.broadcasted_iota(jnp.int32, (LANES, LANES), 0)
        e = lax.broadcasted_iota(jnp.int32, (LANES, LANES), 1)
        upper = (k < e).astype(F32)
        offs_ref[...] = jnp.dot(padded_counts(), upper, preferred_element_type=F32,
                                precision=lax.Precision.HIGHEST)
        carry_ref[...] = jnp.zeros(carry_ref.shape, F32)
        r = lax.broadcasted_iota(jnp.int32, (tm, tm), 0)
        c = lax.broadcasted_iota(jnp.int32, (tm, tm), 1)
        before_ref[...] = (c < r).astype(BF16)

    @pl.when(ph == 1)
    def _():
        rank = _dot(before_ref[...], onehot.astype(BF16)) + carry_ref[0:1, :]
        row = rank + offs_ref[0:1, :]
        pos1 = jnp.sum(jnp.where(lane == i1, row, 0.0), axis=-1, keepdims=True)
        pos2 = jnp.sum(jnp.where(lane == i2, row, 0.0), axis=-1, keepdims=True)
        both = jnp.where(lane == 0.0, pos1, jnp.where(lane == 1.0, pos2, 0.0))
        pos_ref[...] = both.T[0:8, :].astype(jnp.int32)
        carry_ref[...] += colsum

    @pl.when((ph == 1) & (i == pl.num_programs(1) - 1))
    def _():
        nrow = meta_ref.shape[0]
        padded = padded_counts()[0:1, :]
        offs = offs_ref[0:1, :]
        ends = offs + padded
        start = lax.broadcasted_iota(jnp.int32, (nrow, 1), 0).astype(F32) * MOE_TILE
        is_e = lane1 < N_EXPERTS
        te = jnp.sum(jnp.where(is_e & (ends <= start), 1.0, 0.0), axis=-1, keepdims=True)
        te = jnp.minimum(te, N_EXPERTS - 1.0)
        valid_end = jnp.sum(jnp.where(lane1 == te, offs + cnt_ref[0:1, :], 0.0), axis=-1, keepdims=True)
        nv = jnp.clip(valid_end - start, 0.0, MOE_TILE)
        total = jnp.sum(jnp.where(is_e, padded, 0.0), axis=-1, keepdims=True)
        active = jnp.where(start < total, 1.0, 0.0)
        meta_ref[...] = jnp.where(lane1 == 0.0, te, jnp.where(lane1 == 1.0, nv,
                                  jnp.where(lane1 == 2.0, active, 0.0)))


def _moe_rank(sel, tm, meta_rows):
    T = sel.shape[0]
    nt = T // tm
    return pl.pallas_call(
        _moe_rank_kernel,
        grid=(2, nt),
        in_specs=[pl.BlockSpec((tm, LANES), lambda ph, i: (i, 0))],
        out_specs=[pl.BlockSpec((None, 8, tm), lambda ph, i: (i * ph, 0, 0)),
                   pl.BlockSpec((meta_rows, LANES), lambda ph, i: (0, 0))],
        out_shape=[jax.ShapeDtypeStruct((nt, 8, tm), jnp.int32),
                   jax.ShapeDtypeStruct((meta_rows, LANES), F32)],
        scratch_shapes=[pltpu.VMEM((8, LANES), F32), pltpu.VMEM((8, LANES), F32),
                        pltpu.VMEM((8, LANES), F32), pltpu.VMEM((tm, tm), BF16)],
        compiler_params=_cparams("arbitrary", "arbitrary"),
        name="moe_rank",
    )(sel)


def _sc_mesh():
    return plsc.VectorSubcoreMesh(core_axis_name="core", subcore_axis_name="subcore")


def _sc_scatter_rows(x, idx, n_rows):
    n_idx = idx.shape[1]
    nt = x.shape[0] // SC_WINDOW
    width = x.shape[1]

    @functools.partial(pl.kernel, out_type=jax.ShapeDtypeStruct((n_rows, width), x.dtype),
                       mesh=_sc_mesh())
    def scatter_kernel(x_hbm, i_hbm, o_hbm):
        def body(x_vmem, i_vmem):
            pltpu.sync_copy(x_vmem, o_hbm.at[i_vmem.at[0]])

        pltpu.emit_pipeline(
            body, grid=(n_idx // SC_WINDOW,),
            in_specs=[pl.BlockSpec((SC_WINDOW, width), lambda i: (i % nt, 0)),
                      pl.BlockSpec((1, SC_WINDOW), lambda i: (0, i))],
            out_specs=[],
            core_axis_name=("core", "subcore"), dimension_semantics=(pltpu.PARALLEL,),
        )(x_hbm, i_hbm)

    return scatter_kernel(x, idx)


def _sc_gather_rows(x, idx):
    n_idx = idx.shape[1]
    width = x.shape[1]

    @functools.partial(pl.kernel, out_type=jax.ShapeDtypeStruct((n_idx, width), x.dtype),
                       mesh=_sc_mesh())
    def gather_kernel(x_hbm, i_hbm, o_hbm):
        def body(i_vmem, o_vmem):
            pltpu.sync_copy(x_hbm.at[i_vmem.at[0]], o_vmem)

        pltpu.emit_pipeline(
            body, grid=(n_idx // SC_WINDOW,),
            in_specs=[pl.BlockSpec((1, SC_WINDOW), lambda i: (0, i))],
            out_specs=[pl.BlockSpec((SC_WINDOW, width), lambda i: (i, 0))],
            core_axis_name=("core", "subcore"), dimension_semantics=(pltpu.PARALLEL,),
        )(i_hbm, o_hbm)

    return gather_kernel(x, idx)


def _moe_group_kernel(te_ref, nv_ref, na_ref, xa_ref, xb_ref, wg_ref, wu_ref, wd_ref,
                      ya_ref, yb_ref, h_ref, acc_ref, a_ref):
    j = pl.program_id(0)
    f = pl.program_id(1)

    @pl.when(j < na_ref[0])
    def _():
        @pl.when(f == 0)
        def _():
            tm = h_ref.shape[0]
            Q = xa_ref.shape[1]
            valid = lax.broadcasted_iota(jnp.int32, (tm, 1), 0) < nv_ref[j]
            for src, c0 in ((xa_ref, 0), (xb_ref, 2 * Q)):
                lo, hi = _unpack_bf16_pairs(src[...])
                h_ref[:, c0:c0 + Q] = jnp.where(valid, lo, 0.0).astype(BF16)
                h_ref[:, c0 + Q:c0 + 2 * Q] = jnp.where(valid, hi, 0.0).astype(BF16)
            acc_ref[...] = jnp.zeros(acc_ref.shape, F32)

        acc_ref[...] += _swiglu_block(h_ref[...], wg_ref, wu_ref, wd_ref, a_ref)

        @pl.when(f == pl.num_programs(1) - 1)
        def _():
            Q = ya_ref.shape[1]
            ya_ref[...] = _pack_bf16_pairs(acc_ref[:, 0:Q], acc_ref[:, Q:2 * Q])
            yb_ref[...] = _pack_bf16_pairs(acc_ref[:, 2 * Q:3 * Q], acc_ref[:, 3 * Q:4 * Q])


def _moe_group(te, nv, na, xa, xb, wg, wu, wd, tf):
    R, Q = xa.shape
    E, D, F = wg.shape
    nf = F // tf
    tile = lambda j, f, te, nv, na: (jnp.minimum(j, na[0] - 1), 0)
    ff = lambda j, f, na: jnp.where(j < na[0], f, nf - 1)
    grid_spec = pltpu.PrefetchScalarGridSpec(
        num_scalar_prefetch=3,
        grid=(R // MOE_TILE, nf),
        in_specs=[pl.BlockSpec((MOE_TILE, Q), tile),
                  pl.BlockSpec((MOE_TILE, Q), tile),
                  pl.BlockSpec((None, D, tf), lambda j, f, te, nv, na: (te[j], 0, ff(j, f, na))),
                  pl.BlockSpec((None, D, tf), lambda j, f, te, nv, na: (te[j], 0, ff(j, f, na))),
                  pl.BlockSpec((None, tf, D), lambda j, f, te, nv, na: (te[j], ff(j, f, na), 0))],
        out_specs=[pl.BlockSpec((MOE_TILE, Q), tile), pl.BlockSpec((MOE_TILE, Q), tile)],
        scratch_shapes=[pltpu.VMEM((MOE_TILE, D), BF16), pltpu.VMEM((MOE_TILE, D), F32),
                        pltpu.VMEM((MOE_TILE, tf), BF16)],
    )
    return pl.pallas_call(
        _moe_group_kernel,
        grid_spec=grid_spec,
        out_shape=[jax.ShapeDtypeStruct((R, Q), jnp.uint32), jax.ShapeDtypeStruct((R, Q), jnp.uint32)],
        compiler_params=_cparams("arbitrary", "arbitrary"),
        name="moe_experts",
    )(te, nv, na, xa, xb, wg, wu, wd)


def _moe_combine_kernel(x_ref, sel_ref, g1_ref, g2_ref, *rest):
    o_ref = rest[-1]
    sel = sel_ref[...]
    p1 = sel[:, 2:3]
    p2 = sel[:, 3:4]
    Q = g1_ref.shape[1]
    lo1, hi1 = _unpack_bf16_pairs(g1_ref[...])
    lo2, hi2 = _unpack_bf16_pairs(g2_ref[...])
    o_ref[:, 0:Q] = x_ref[:, 0:Q] + (p1 * lo1 + p2 * lo2)
    o_ref[:, Q:2 * Q] = x_ref[:, Q:2 * Q] + (p1 * hi1 + p2 * hi2)


def _moe_combine_half(x, sel, g, half, partial_out, tm):
    T, D = x.shape
    Q = g.shape[1]
    nt = T // tm
    in_specs = [pl.BlockSpec((tm, 2 * Q), lambda i: (i, half)),
                pl.BlockSpec((tm, LANES), lambda i: (i, 0)),
                pl.BlockSpec((tm, Q), lambda i: (i, 0)),
                pl.BlockSpec((tm, Q), lambda i: (i + nt, 0))]
    args = [x, sel, g, g]
    aliases = {}
    if partial_out is not None:
        in_specs.append(pl.BlockSpec(memory_space=pl.ANY))
        args.append(partial_out)
        aliases = {4: 0}
    return pl.pallas_call(
        _moe_combine_kernel,
        grid=(nt,),
        in_specs=in_specs,
        out_specs=pl.BlockSpec((tm, 2 * Q), lambda i: (i, half)),
        out_shape=jax.ShapeDtypeStruct((T, D), F32),
        input_output_aliases=aliases,
        compiler_params=_cparams("parallel"),
        name="moe_combine",
    )(*args)


def _cast_kernel(x_ref, o_ref):
    o_ref[...] = x_ref[...].astype(o_ref.dtype)


def _cast_bf16(w, rows):
    cols = w.shape[-1]
    w2 = w.reshape(-1, cols)
    out = pl.pallas_call(
        _cast_kernel,
        grid=(w2.shape[0] // rows,),
        in_specs=[pl.BlockSpec((rows, cols), lambda i: (i, 0))],
        out_specs=pl.BlockSpec((rows, cols), lambda i: (i, 0)),
        out_shape=jax.ShapeDtypeStruct(w2.shape, BF16),
        compiler_params=_cparams("parallel"),
        name="cast_bf16",
    )(w2)
    return out.reshape(w.shape)


def _moe(x, nw, w_router, b_router, wg, wu, wd):
    T, D = x.shape
    wr = jnp.pad(w_router, ((0, 0), (0, LANES - N_EXPERTS)))
    br = jnp.pad(b_router, (0, LANES - N_EXPERTS)).reshape(1, LANES)
    ha, hb, sel = _router(x, nw, wr, br, 1024)
    n_rows = TOP_K * T + N_EXPERTS * MOE_TILE
    n_tiles = n_rows // MOE_TILE
    pos, meta = _moe_rank(sel, 1024, 256)
    idx = jnp.concatenate([pos[:, 0, :].reshape(1, T), pos[:, 1, :].reshape(1, T)], axis=1)
    te = meta[:n_tiles, 0].astype(jnp.int32)
    nv = meta[:n_tiles, 1].astype(jnp.int32)
    na = jnp.sum(meta[:n_tiles, 2]).astype(jnp.int32).reshape(1)
    te = jnp.where(jnp.arange(n_tiles) < na[0], te, te[na[0] - 1])
    xa = _sc_scatter_rows(ha, idx, n_rows)
    xb = _sc_scatter_rows(hb, idx, n_rows)
    ya, yb = _moe_group(te, nv, na, xa, xb, wg, wu, wd, wg.shape[2] // 2)
    ga = _sc_gather_rows(ya, idx)
    gb = _sc_gather_rows(yb, idx)
    out = _moe_combine_half(x, sel, ga, 0, None, 1024)
    return _moe_combine_half(x, sel, gb, 1, out, 1024)


def _split_w_in(w_in):
    sizes = (512, 512, 512, 512, 768, 768, 768, 256, 256, 512, 512, 4, 4, 1024, 1024, 1024)
    offs = [0]
    for s in sizes:
        offs.append(offs[-1] + s)
    part = lambda i: w_in[..., offs[i]:offs[i + 1]]
    rq, rk, rv, rg, dq, dk, dv, mq, mk, mv, mo, mi, mf, g_ret, g_da, g_ml = (part(i) for i in range(16))
    cols = []
    for p in range(2):
        for g in range(3):
            for t in (dq, dk, dv):
                cols.append(t[..., g * 256 + p * 128:g * 256 + (p + 1) * 128])
    wb = jnp.concatenate([rq, rk, rv, rg, g_ret, g_da, g_ml, mq, mk, mv, mo] + cols,
                         axis=-1).astype(BF16)
    pad = jnp.zeros(w_in.shape[:-1] + (LANES - 8,), w_in.dtype)
    wf = jnp.concatenate([mi, mf, pad], axis=-1).astype(BF16)
    return wb, wf


def _rope_tables(S):
    half = LANES // 2
    inv = jnp.power(ROPE_BASE, -jnp.arange(half, dtype=F32) / half)
    ang = jnp.arange(S, dtype=F32)[:, None] * inv[None, :]
    cos = jnp.cos(ang)
    sin = jnp.sin(ang)
    return jnp.concatenate([cos, cos], axis=1), jnp.concatenate([-sin, sin], axis=1)


def kernel(x, norm1_w, w_in, ret_norm_w, da_q_norm_w, da_k_norm_w, ml_conv_w, ml_i_bias, ml_f_bias,
           ml_norm_w, w_br_ret, w_br_da, w_br_ml, w_out, norm2_w, ffn_w_gate, ffn_w_up, ffn_w_down,
           moe_w_router, moe_b_router, moe_w_gate, moe_w_up, moe_w_down):
    B, S, D = x.shape
    T = B * S
    depth = w_in.shape[0]
    cos, sin = _rope_tables(S)
    wb_all, wf_all = _split_w_in(w_in)
    xt = x.reshape(T, D)
    for layer in range(depth):
        wb, wf = wb_all[layer], wf_all[layer]
        nw1 = norm1_w[layer].reshape(1, D)
        zb, zf = _inproj(xt, nw1, wb, wf, ml_conv_w[layer], 4, S, 1280)
        o_ret = _retention(zb, cos, sin, ret_norm_w[layer].reshape(1, -1), B, S)
        wq = jnp.tile(da_q_norm_w[layer], 2).reshape(1, LANES)
        wk = jnp.tile(da_k_norm_w[layer], 2).reshape(1, LANES)
        o_da = _dilated_attention(zb, 52, wq, wk, B, S)
        gate_bias = jnp.concatenate([ml_i_bias[layer], ml_f_bias[layer],
                                     jnp.zeros((LANES - 2 * ML_HEADS,), F32)]).reshape(1, LANES)
        o_ml = _mlstm(zb, zf, gate_bias, ml_norm_w[layer].reshape(1, -1), B, S,
                      qk_blk=10, gate_blk=0)
        xt = _merge(xt, o_ret, o_da, o_ml, zb, 2,
                    w_br_ret[layer].astype(BF16), w_br_da[layer].astype(BF16),
                    w_br_ml[layer].astype(BF16), w_out[layer].astype(BF16), 512)
        nw2 = norm2_w[layer].reshape(1, D)
        j = layer // 2
        if layer % 2 == 0:
            xt = _ffn(xt, nw2, ffn_w_gate[j].astype(BF16), ffn_w_up[j].astype(BF16),
                      ffn_w_down[j].astype(BF16), 1024, 1408)
        else:
            xt = _moe(xt, nw2, moe_w_router[j], moe_b_router[j], _cast_bf16(moe_w_gate[j], 512),
                      _cast_bf16(moe_w_up[j], 512), _cast_bf16(moe_w_down[j], 2048))
    return xt.reshape(B, S, D)
```

```python
import functools
import math

import jax
import jax.numpy as jnp
from jax import lax
from jax.experimental import pallas as pl
from jax.experimental.pallas import tpu as pltpu
from jax.experimental.pallas import tpu_sc as plsc

F32 = jnp.float32
BF16 = jnp.bfloat16

EPS = 1e-6
D_MODEL = 1024
CHUNK = 128
LANES = 128
MXU_N = 256
ROPE_BASE = 10000.0
RET_HEADS = 4
RET_SEQS = 2
DA_GROUPS = ((1, 16), (4, 4), (16, 1))
DA_DH = 64
DA_UNROLL = 4
DA_MAX_STRIDE = 4
ML_HEADS = 4
ML_DK = 64
ML_SEQS = 2
N_EXPERTS = 8
TOP_K = 2
MOE_TILE = 512
SC_WINDOW = 128
VMEM_LIMIT = 56 * 1024 * 1024

NEG = -1e30


def _cparams(*sem):
    return pltpu.CompilerParams(dimension_semantics=sem, vmem_limit_bytes=VMEM_LIMIT)


def _dot(a, b):
    return jnp.dot(a, b, preferred_element_type=F32)


def _dot_nt(a, b):
    return lax.dot_general(a, b, (((1,), (1,)), ((), ())), preferred_element_type=F32)


def _dot_tn(a, b):
    return lax.dot_general(a, b, (((0,), (0,)), ((), ())), preferred_element_type=F32)


def _sigmoid(x):
    return 1.0 / (1.0 + jnp.exp(-x))


def _silu(x):
    return x * _sigmoid(x)


def _inproj_kernel(conv_blk, x_ref, nw_ref, wb_ref, wf_ref, cw_ref, zb_ref, zf_ref, h_ref, cv_ref):
    j = pl.program_id(1)
    tm = x_ref.shape[0]

    @pl.when(j == 0)
    def _():
        rc = 512
        for r in range(tm // rc):
            rows = pl.ds(r * rc, rc)
            x = x_ref[rows, :]
            ms = jnp.mean(x * x, axis=-1, keepdims=True)
            h = (x * lax.rsqrt(ms + EPS) * nw_ref[...]).astype(BF16)
            h_ref[rows, :] = h
            zb_ref[rows, :] = _dot(h, wb_ref[...]).astype(zb_ref.dtype)
            zf_ref[rows, :] = _dot(h, wf_ref[...])

    @pl.when((j > 0) & (j != conv_blk))
    def _():
        zb_ref[...] = _dot(h_ref[...], wb_ref[...]).astype(zb_ref.dtype)

    @pl.when(j == conv_blk)
    def _():
        h = h_ref[...]
        cw = cw_ref.shape[1]
        pad = cv_ref.shape[0] - tm
        cv_ref[0:pad, :] = jnp.zeros((pad, cw), F32)
        cv_ref[pad:pad + tm, :] = _dot(h, wb_ref[:, 0:cw])
        taps = cw_ref.shape[0]
        rb = 256

        def conv(piece, cs):
            cols = slice(cs * LANES, (cs + 1) * LANES)
            r0 = pad + piece * rb - (taps - 1)
            acc = cw_ref[0:1, cols] * cv_ref[pl.ds(r0, rb), cols]
            for i in range(1, taps):
                acc = acc + cw_ref[i:i + 1, cols] * cv_ref[pl.ds(r0 + i, rb), cols]
            scale = ML_DK ** -0.5 if cs >= cw // LANES // 2 else 1.0
            zb_ref[piece * rb:(piece + 1) * rb, cols] = (_silu(acc) * scale).astype(zb_ref.dtype)

        todo = [(piece, cs) for piece in range(tm // rb) for cs in range(cw // LANES)]
        chunks = list(range(cw, zb_ref.shape[1], MXU_N))
        per = -(-len(todo) // len(chunks))
        for n, c0 in enumerate(chunks):
            c1 = min(c0 + MXU_N, zb_ref.shape[1])
            zb_ref[:, c0:c1] = _dot(h, wb_ref[:, c0:c1]).astype(zb_ref.dtype)
            for piece, cs in todo[n * per:(n + 1) * per]:
                conv(piece, cs)


def _inproj(x, nw, wb, wf, conv_w, conv_blk, tm, tnb):
    T, D = x.shape
    return pl.pallas_call(
        functools.partial(_inproj_kernel, conv_blk),
        grid=(T // tm, wb.shape[1] // tnb),
        in_specs=[pl.BlockSpec((tm, D), lambda i, j: (i, 0)),
                  pl.BlockSpec((1, D), lambda i, j: (0, 0)),
                  pl.BlockSpec((D, tnb), lambda i, j: (0, j)),
                  pl.BlockSpec(wf.shape, lambda i, j: (0, 0)),
                  pl.BlockSpec(conv_w.shape, lambda i, j: (0, 0))],
        out_specs=[pl.BlockSpec((tm, tnb), lambda i, j: (i, j)),
                   pl.BlockSpec((tm, wf.shape[1]), lambda i, j: (i, 0))],
        out_shape=[jax.ShapeDtypeStruct((T, wb.shape[1]), BF16),
                   jax.ShapeDtypeStruct((T, wf.shape[1]), F32)],
        scratch_shapes=[pltpu.VMEM((tm, D), BF16), pltpu.VMEM((tm + 8, conv_w.shape[1]), F32)],
        compiler_params=_cparams("parallel", "arbitrary"),
        name="inproj",
    )(x, nw, wb, wf, conv_w)


def _retention_kernel(z_ref, cos_ref, sin_ref, nw_ref, o_ref, dec_ref, st_ref):
    S = z_ref.shape[0] // RET_SEQS
    n_chunks = S // CHUNK
    H = RET_HEADS
    HW = H * LANES
    row = lax.broadcasted_iota(jnp.int32, (CHUNK, CHUNK), 0).astype(F32)
    col = lax.broadcasted_iota(jnp.int32, (CHUNK, CHUNK), 1).astype(F32)
    lgs = [math.log1p(-(2.0 ** (-5.0 - h))) for h in range(H)]
    for h, lg in enumerate(lgs):
        rel = row - col
        dec_ref[h] = jnp.where(rel >= 0, jnp.exp(lg * jnp.maximum(rel, 0.0)), 0.0)
        dec_ref[H + h] = jnp.exp(lg * (row + 1.0))
        dec_ref[2 * H + h] = jnp.exp(lg * (CHUNK - 1.0 - row)) * (LANES ** -0.5)
    st_ref[...] = jnp.zeros(st_ref.shape, F32)
    units = [(sq, h) for sq in range(RET_SEQS) for h in range(H)]

    def body(n, carry):
        r0 = pl.multiple_of(n * CHUNK, CHUNK)
        cos = cos_ref[pl.ds(r0, CHUNK), :]
        sin = sin_ref[pl.ds(r0, CHUNK), :]
        rows_of = [pl.ds(pl.multiple_of(sq * S + n * CHUNK, CHUNK), CHUNK) for sq in range(RET_SEQS)]
        vs, ss, iqs, kvs = [], [], [], []
        for u, (sq, h) in enumerate(units):
            rows = rows_of[sq]
            q = z_ref[rows, h * LANES:(h + 1) * LANES].astype(F32)
            k = z_ref[rows, HW + h * LANES:HW + (h + 1) * LANES].astype(F32)
            v = z_ref[rows, 2 * HW + h * LANES:2 * HW + (h + 1) * LANES]
            q = q * cos + pltpu.roll(q, LANES // 2, 1) * sin
            k = k * cos + pltpu.roll(k, LANES // 2, 1) * sin
            qb = q.astype(BF16)
            kb = (k * (LANES ** -0.5)).astype(BF16)
            kd = (k * dec_ref[2 * H + h]).astype(BF16)
            ss.append(_dot_nt(qb, kb))
            iqs.append(_dot(qb, st_ref[u].astype(BF16)))
            kvs.append(_dot_tn(kd, v))
            vs.append(v)
        ps = [(ss[u] * dec_ref[h]).astype(BF16) for u, (_, h) in enumerate(units)]
        os_ = [_dot(ps[u], vs[u]) + iqs[u] * dec_ref[H + h] for u, (_, h) in enumerate(units)]
        for u, (sq, h) in enumerate(units):
            rows = rows_of[sq]
            st_ref[u] = st_ref[u] * math.exp(lgs[h] * CHUNK) + kvs[u]
            o = os_[u]
            ms = jnp.mean(o * o, axis=-1, keepdims=True)
            o = o * lax.rsqrt(ms + EPS) * nw_ref[:, h * LANES:(h + 1) * LANES]
            g = z_ref[rows, 3 * HW + h * LANES:3 * HW + (h + 1) * LANES].astype(F32)
            o_ref[rows, h * LANES:(h + 1) * LANES] = (o * _silu(g)).astype(o_ref.dtype)
        return carry

    lax.fori_loop(0, n_chunks, body, 0)


def _retention(zb, cos, sin, nw, B, S):
    T = B * S
    W = RET_HEADS * LANES
    R = RET_SEQS * S
    return pl.pallas_call(
        _retention_kernel,
        grid=(B // RET_SEQS,),
        in_specs=[pl.BlockSpec((R, 4 * W), lambda b: (b, 0)),
                  pl.BlockSpec((S, LANES), lambda b: (0, 0)),
                  pl.BlockSpec((S, LANES), lambda b: (0, 0)),
                  pl.BlockSpec((1, W), lambda b: (0, 0))],
        out_specs=pl.BlockSpec((R, W), lambda b: (b, 0)),
        out_shape=jax.ShapeDtypeStruct((T, W), BF16),
        scratch_shapes=[pltpu.VMEM((3 * RET_HEADS, CHUNK, LANES), F32),
                        pltpu.VMEM((RET_SEQS * RET_HEADS, LANES, LANES), F32)],
        compiler_params=_cparams("parallel"),
        name="retention",
    )(zb, cos, sin, nw)


def _da_kernel(*refs):
    z_refs = refs[:9]
    wq_ref, wk_ref, o_ref, qn_ref, kn_ref, v_ref, np_ref, lp_ref, mp_ref = refs[9:18]
    nn_refs, ln_refs, mn_refs = refs[18:21], refs[21:24], refs[24:27]
    stage_refs = refs[27:30]
    S = o_ref.shape[0]
    lane = lax.broadcasted_iota(jnp.int32, (1, LANES), 1)
    lo = lane < DA_DH
    row = lax.broadcasted_iota(jnp.int32, (CHUNK, CHUNK), 0)
    col = lax.broadcasted_iota(jnp.int32, (CHUNK, CHUNK), 1)
    mask_cur = col <= row
    row2 = lax.broadcasted_iota(jnp.int32, (CHUNK, 2 * CHUNK), 0)
    col2 = lax.broadcasted_iota(jnp.int32, (CHUNK, 2 * CHUNK), 1)
    mask_band = (col2 >= row2) & (col2 <= row2 + CHUNK)

    seg = (lax.broadcasted_iota(jnp.int32, (LANES, LANES), 0) // DA_DH
           == lax.broadcasted_iota(jnp.int32, (LANES, LANES), 1) // DA_DH).astype(BF16) * (1.0 / DA_DH)

    def head_norm(x, w):
        x2 = x * x
        hi = x2.astype(BF16)
        lo_part = (x2 - hi.astype(F32)).astype(BF16)
        ms = _dot(hi, seg) + _dot(lo_part, seg)
        return x * lax.rsqrt(ms + EPS) * w

    RB = 256
    for g, (dil, nb) in enumerate(DA_GROUPS):
        L = S // dil
        zq_ref, zk_ref, zv_ref = z_refs[3 * g:3 * g + 3]
        if dil > 1:
            for piece in range(S // RB):
                rows = pl.ds(piece * RB, RB)
                for src_ref, dst_ref in zip((zq_ref, zk_ref, zv_ref), stage_refs):
                    dst_ref[rows, :] = src_ref[rows, :].astype(F32)
            zq_ref, zk_ref, zv_ref = stage_refs
        first = min(dil, DA_MAX_STRIDE)
        if dil > first:
            mid_refs = (nn_refs[g], ln_refs[g], mn_refs[g])
            sub = S // first
            for b in range(first):
                for piece in range(sub // RB):
                    src = pl.ds(b + first * piece * RB, RB, stride=first)
                    dst = pl.ds(b * sub + piece * RB, RB)
                    for src_ref, dst_ref in zip((zq_ref, zk_ref, zv_ref), mid_refs):
                        dst_ref[dst, :] = src_ref[src, :]
            zq_ref, zk_ref, zv_ref = mid_refs
        rest = dil // first

        def source_rows(rho, piece, n_rows):
            if dil == 1:
                return pl.ds(piece * n_rows, n_rows)
            b, a = rho % first, rho // first
            if rest == 1:
                return pl.ds(b + first * piece * n_rows, n_rows, stride=first)
            return pl.ds(b * (S // first) + a + rest * piece * n_rows, n_rows, stride=rest)

        for rho in range(dil):
            for piece in range(max(L // RB, 1)):
                n_rows = min(RB, L)
                src = source_rows(rho, piece, n_rows)
                dst = pl.ds(rho * L + piece * n_rows, n_rows)
                q = zq_ref[src, :].astype(F32)
                k = zk_ref[src, :].astype(F32)
                v = zv_ref[src, :]
                qn_ref[dst, :] = (head_norm(q, wq_ref[...]) * (DA_DH ** -0.5)).astype(BF16)
                kn_ref[dst, :] = head_norm(k, wk_ref[...]).astype(BF16)
                v_ref[dst, :] = v.astype(BF16)

        def batch(r0, chained, first_has_prev):
            chains = []
            for u in range(DA_UNROLL):
                rows = pl.ds(r0 + u * CHUNK, CHUNK)
                if chained and (u > 0 or first_has_prev):
                    keys, mask = pl.ds(r0 + (u - 1) * CHUNK, 2 * CHUNK), mask_band
                else:
                    keys, mask = rows, mask_cur
                k = kn_ref[keys, :]
                v = v_ref[keys, :]
                vaug = jnp.concatenate([v, jnp.ones_like(v)], axis=1)
                q = qn_ref[rows, :]
                zero = jnp.zeros_like(q)
                for qh in (jnp.where(lo, q, zero), jnp.where(lo, zero, q)):
                    chains.append((_dot_nt(qh, k), mask, vaug))
            probs = []
            for s, mask, _ in chains:
                s = jnp.where(mask, s, NEG)
                m = jnp.max(s, axis=-1, keepdims=True)
                probs.append((jnp.exp(s - m).astype(BF16), m))
            accs = [_dot(p, vaug) for (p, _), (_, _, vaug) in zip(probs, chains)]
            nums, dens, maxs = [], [], []
            for u in range(DA_UNROLL):
                a0, a1 = accs[2 * u], accs[2 * u + 1]
                nums.append(jnp.where(lo, a0[:, :LANES], a1[:, :LANES]))
                dens.append(jnp.where(lo, a0[:, LANES:], a1[:, LANES:]))
                maxs.append(jnp.where(lo, probs[2 * u][1], probs[2 * u + 1][1]))
            rows = pl.ds(r0, DA_UNROLL * CHUNK)
            np_ref[rows, :] = jnp.concatenate(nums, axis=0)
            lp_ref[rows, :] = jnp.concatenate(dens, axis=0)
            mp_ref[rows, :] = jnp.concatenate(maxs, axis=0)

        span = DA_UNROLL * CHUNK
        if nb == 1:
            def singles(i, c):
                batch(pl.multiple_of(i * span, span), False, False)
                return c
            lax.fori_loop(0, S // span, singles, 0)
        else:
            def segment(sgi, c, nb=nb):
                base = pl.multiple_of(sgi * (nb * CHUNK), span)
                batch(base, True, False)
                if nb > DA_UNROLL:
                    def inner(n, c2):
                        batch(pl.multiple_of(base + n * span, span), True, True)
                        return c2
                    lax.fori_loop(1, nb // DA_UNROLL, inner, 0)
                return c
            lax.fori_loop(0, dil, segment, 0)

        outs = ((np_ref, nn_refs[g]), (lp_ref, ln_refs[g]), (mp_ref, mn_refs[g]))
        for rho in range(dil):
            for piece in range(max(L // RB, 1)):
                n_rows = min(RB, L)
                dst = source_rows(rho, piece, n_rows)
                src = pl.ds(rho * L + piece * n_rows, n_rows)
                for k_out, (perm_ref, nat_ref) in enumerate(outs):
                    (stage_refs[k_out] if rest > 1 else nat_ref)[dst, :] = perm_ref[src, :]
        if rest > 1:
            sub = S // first
            for b in range(first):
                for piece in range(sub // RB):
                    dst = pl.ds(b + first * piece * RB, RB, stride=first)
                    src = pl.ds(b * sub + piece * RB, RB)
                    for k_out, (_, nat_ref) in enumerate(outs):
                        nat_ref[dst, :] = stage_refs[k_out][src, :]

    for piece in range(S // RB):
        rows = pl.ds(piece * RB, RB)
        m = jnp.maximum(jnp.maximum(mn_refs[0][rows, :], mn_refs[1][rows, :]), mn_refs[2][rows, :])
        num = jnp.zeros((RB, LANES), F32)
        den = jnp.zeros((RB, LANES), F32)
        for g in range(3):
            e = jnp.exp(mn_refs[g][rows, :] - m)
            num = num + e * nn_refs[g][rows, :]
            den = den + e * ln_refs[g][rows, :]
        o_ref[rows, :] = (num / den).astype(o_ref.dtype)


def _dilated_attention(zb, slab0, wq, wk, B, S):
    T = B * S
    slab = lambda k: pl.BlockSpec((S, LANES), lambda b, p: (b, slab0 + 9 * p + k))
    return pl.pallas_call(
        _da_kernel,
        grid=(B, 2),
        in_specs=[slab(k) for k in range(9)] + [
                  pl.BlockSpec((1, LANES), lambda b, p: (0, 0)),
                  pl.BlockSpec((1, LANES), lambda b, p: (0, 0))],
        out_specs=pl.BlockSpec((S, LANES), lambda b, p: (b, p)),
        out_shape=jax.ShapeDtypeStruct((T, 2 * LANES), BF16),
        scratch_shapes=[pltpu.VMEM((S, LANES), BF16), pltpu.VMEM((S, LANES), BF16),
                        pltpu.VMEM((S, LANES), BF16),
                        pltpu.VMEM((S, LANES), F32), pltpu.VMEM((S, LANES), F32),
                        pltpu.VMEM((S, LANES), F32)] + [pltpu.VMEM((S, LANES), F32)] * 12,
        compiler_params=_cparams("parallel", "arbitrary"),
        name="dilated_attention",
    )(*([zb] * 9), wq, wk)


def _mlstm_kernel(qk_ref, v_ref, og_ref, gate_ref, gb_ref, nw_ref, o_ref, st_ref, e_ref, et_ref):
    S = qk_ref.shape[0] // ML_SEQS
    n_chunks = S // CHUNK
    W = qk_ref.shape[1]
    lane = lax.broadcasted_iota(jnp.int32, (1, LANES), 1)
    lo = lane < ML_DK
    row = lax.broadcasted_iota(jnp.int32, (CHUNK, CHUNK), 0)
    col = lax.broadcasted_iota(jnp.int32, (CHUNK, CHUNK), 1)
    causal = col <= row
    tril = causal.astype(F32)
    srow = lax.broadcasted_iota(jnp.int32, (LANES, 1), 0) < ML_DK
    st_ref[...] = jnp.zeros(st_ref.shape, F32)

    is_f = (lane >= ML_HEADS) & (lane < 2 * ML_HEADS)
    for n in range(ML_SEQS * n_chunks):
        rows = pl.ds(n * CHUNK, CHUNK)
        gp = gate_ref[rows, :] + gb_ref[...]
        logf = jnp.minimum(gp, 0.0) - jnp.log(1.0 + jnp.exp(-jnp.abs(gp)))
        cum = jnp.dot(tril, jnp.where(is_f, logf, 0.0), preferred_element_type=F32,
                      precision=lax.Precision.HIGHEST)
        e = jnp.where(lane < ML_HEADS, gp, cum)
        e_ref[rows, :] = e
        et_ref[n] = e.T[0:2 * ML_HEADS, :]

    units = [(sq, h) for sq in range(ML_SEQS) for h in range(ML_HEADS)]

    def chunk(n, mms):
        rows_of = [pl.ds(pl.multiple_of(sq * S + n * CHUNK, CHUNK), CHUNK) for sq in range(ML_SEQS)]
        es = [e_ref[rows_of[sq], :] for sq in range(ML_SEQS)]
        ets = [et_ref[sq * n_chunks + n] for sq in range(ML_SEQS)]
        khs, vaugs, s_raws, iqs = [], [], [], []
        for sq in range(ML_SEQS):
            rows = rows_of[sq]
            for pair in range(2):
                q2 = qk_ref[rows, pair * LANES:(pair + 1) * LANES]
                k2 = qk_ref[rows, W // 2 + pair * LANES:W // 2 + (pair + 1) * LANES]
                zero = jnp.zeros_like(q2)
                stb = st_ref[2 * sq + pair].astype(BF16)
                for hs in range(2):
                    h = 2 * pair + hs
                    qh = jnp.where(lo, q2, zero) if hs == 0 else jnp.where(lo, zero, q2)
                    kh = jnp.where(lo, k2, zero) if hs == 0 else jnp.where(lo, zero, k2)
                    vh = v_ref[rows, h * LANES:(h + 1) * LANES]
                    s_raws.append(_dot_nt(qh, k2))
                    iqs.append(_dot(qh, stb))
                    khs.append(kh)
                    vaugs.append(jnp.concatenate([vh, jnp.ones_like(vh)], axis=-1))
        ps, wis, ms_, kws, wcs, new_mms = [], [], [], [], [], []
        for u, (sq, h) in enumerate(units):
            e, et = es[sq], ets[sq]
            mm = mms[u]
            i_b = jnp.broadcast_to(e[:, h:h + 1], (CHUNK, LANES))
            a_b = jnp.broadcast_to(e[:, ML_HEADS + h:ML_HEADS + h + 1], (CHUNK, LANES))
            c_row = et[h:h + 1, :] - et[ML_HEADS + h:ML_HEADS + h + 1, :]
            dmat = jnp.where(causal, a_b + c_row, NEG)
            inter = a_b + mm
            m = jnp.maximum(inter, jnp.max(dmat, axis=-1, keepdims=True))
            ps.append((s_raws[u] * jnp.exp(dmat - m)).astype(BF16))
            wis.append(jnp.exp(inter - m))
            ms_.append(m)
            m_new = m[CHUNK - 1:CHUNK, :]
            a_last = a_b[CHUNK - 1:CHUNK, :]
            wk = jnp.exp(a_last - a_b + i_b - m_new)
            kws.append((khs[u].astype(F32) * wk).astype(BF16))
            wc = jnp.exp(a_last + mm - m_new)
            wcs.append(jnp.concatenate([wc, wc], axis=1))
            new_mms.append(m_new)
        accs = [_dot(ps[u], vaugs[u]) + jnp.concatenate([wis[u], wis[u]], axis=1) * iqs[u]
                for u in range(len(units))]
        upds = [_dot_tn(kws[u], vaugs[u]) for u in range(len(units))]
        for sp in range(2 * ML_SEQS):
            u0, u1 = 2 * sp, 2 * sp + 1
            st_ref[sp] = jnp.where(srow, wcs[u0], wcs[u1]) * st_ref[sp] + (upds[u0] + upds[u1])
        for u, (sq, h) in enumerate(units):
            num = accs[u][:, :LANES]
            den = accs[u][:, LANES:]
            hv = num / jnp.maximum(jnp.abs(den), jnp.exp(-ms_[u]))
            var = jnp.mean(hv * hv, axis=-1, keepdims=True)
            hv = hv * lax.rsqrt(var + EPS) * nw_ref[:, h * LANES:(h + 1) * LANES]
            og = og_ref[rows_of[sq], h * LANES:(h + 1) * LANES].astype(F32)
            o_ref[rows_of[sq], h * LANES:(h + 1) * LANES] = (hv * _sigmoid(og)).astype(o_ref.dtype)
        return tuple(new_mms)

    lax.fori_loop(0, n_chunks, chunk, tuple(jnp.zeros((1, LANES), F32) for _ in units))


def _mlstm(zb, zf, gate_bias, nw, B, S, qk_blk, gate_blk):
    T = B * S
    W = ML_HEADS * LANES
    R = ML_SEQS * S
    return pl.pallas_call(
        _mlstm_kernel,
        grid=(B // ML_SEQS,),
        in_specs=[pl.BlockSpec((R, W), lambda b: (b, qk_blk)),
                  pl.BlockSpec((R, W), lambda b: (b, qk_blk + 1)),
                  pl.BlockSpec((R, W), lambda b: (b, qk_blk + 2)),
                  pl.BlockSpec((R, LANES), lambda b: (b, gate_blk)),
                  pl.BlockSpec((1, LANES), lambda b: (0, 0)),
                  pl.BlockSpec((1, W), lambda b: (0, 0))],
        out_specs=pl.BlockSpec((R, W), lambda b: (b, 0)),
        out_shape=jax.ShapeDtypeStruct((T, W), BF16),
        scratch_shapes=[pltpu.VMEM((2 * ML_SEQS, LANES, 2 * LANES), F32),
                        pltpu.VMEM((R, LANES), F32),
                        pltpu.VMEM((R // CHUNK, 2 * ML_HEADS, LANES), F32)],
        compiler_params=_cparams("parallel"),
        name="mlstm",
    )(zb, zb, zb, zf, gate_bias, nw)


def _merge_kernel(x_ref, oret_ref, oda_ref, oml_ref, gr_ref, gd_ref, gm_ref,
                  wr_ref, wd_ref, wm_ref, wo_ref, o_ref):
    y = _sigmoid(gr_ref[...].astype(F32)) * _dot(oret_ref[...], wr_ref[...])
    y = y + _sigmoid(gd_ref[...].astype(F32)) * _dot(oda_ref[...], wd_ref[...])
    y = y + _sigmoid(gm_ref[...].astype(F32)) * _dot(oml_ref[...], wm_ref[...])
    o_ref[...] = x_ref[...] + _dot(y.astype(BF16), wo_ref[...])


def _merge(x, o_ret, o_da, o_ml, zb, g_blk, wr, wd, wm, wo, tm):
    T, D = x.shape
    full = lambda a: pl.BlockSpec(a.shape, lambda i: (0, 0))
    return pl.pallas_call(
        _merge_kernel,
        grid=(T // tm,),
        in_specs=[pl.BlockSpec((tm, D), lambda i: (i, 0)),
                  pl.BlockSpec((tm, o_ret.shape[1]), lambda i: (i, 0)),
                  pl.BlockSpec((tm, o_da.shape[1]), lambda i: (i, 0)),
                  pl.BlockSpec((tm, o_ml.shape[1]), lambda i: (i, 0)),
                  pl.BlockSpec((tm, D), lambda i: (i, g_blk)),
                  pl.BlockSpec((tm, D), lambda i: (i, g_blk + 1)),
                  pl.BlockSpec((tm, D), lambda i: (i, g_blk + 2)),
                  full(wr), full(wd), full(wm), full(wo)],
        out_specs=pl.BlockSpec((tm, D), lambda i: (i, 0)),
        out_shape=jax.ShapeDtypeStruct((T, D), F32),
        compiler_params=_cparams("parallel"),
        name="merge_outproj",
    )(x, o_ret, o_da, o_ml, zb, zb, zb, wr, wd, wm, wo)


def _swiglu_block(h, wg_ref, wu_ref, wd_ref, a_ref):
    tf = wg_ref.shape[1]
    c0 = 0
    while c0 < tf:
        w = min(MXU_N, tf - c0)
        g = _dot(h, wg_ref[:, c0:c0 + w].astype(BF16))
        u = _dot(h, wu_ref[:, c0:c0 + w].astype(BF16))
        a_ref[:, c0:c0 + w] = (_silu(g) * u).astype(BF16)
        c0 += w
    return _dot(a_ref[...], wd_ref[...].astype(BF16))


def _ffn_kernel(x_ref, nw_ref, wg_ref, wu_ref, wd_ref, o_ref, h_ref, acc_ref, a_ref):
    j = pl.program_id(1)

    @pl.when(j == 0)
    def _():
        x = x_ref[...]
        ms = jnp.mean(x * x, axis=-1, keepdims=True)
        h_ref[...] = (x * lax.rsqrt(ms + EPS) * nw_ref[...]).astype(BF16)
        acc_ref[...] = x

    acc_ref[...] += _swiglu_block(h_ref[...], wg_ref, wu_ref, wd_ref, a_ref)

    @pl.when(j == pl.num_programs(1) - 1)
    def _():
        o_ref[...] = acc_ref[...]


def _ffn(x, nw, wg, wu, wd, tm, tf):
    T, D = x.shape
    F = wg.shape[1]
    return pl.pallas_call(
        _ffn_kernel,
        grid=(T // tm, F // tf),
        in_specs=[pl.BlockSpec((tm, D), lambda i, j: (i, 0)),
                  pl.BlockSpec((1, D), lambda i, j: (0, 0)),
                  pl.BlockSpec((D, tf), lambda i, j: (0, j)),
                  pl.BlockSpec((D, tf), lambda i, j: (0, j)),
                  pl.BlockSpec((tf, D), lambda i, j: (j, 0))],
        out_specs=pl.BlockSpec((tm, D), lambda i, j: (i, 0)),
        out_shape=jax.ShapeDtypeStruct((T, D), F32),
        scratch_shapes=[pltpu.VMEM((tm, D), BF16), pltpu.VMEM((tm, D), F32),
                        pltpu.VMEM((tm, tf), BF16)],
        compiler_params=_cparams("parallel", "arbitrary"),
        name="swiglu_ffn",
    )(x, nw, wg, wu, wd)


HI16 = 0xFFFF0000


def _pack_bf16_pairs(a, b):
    ab = pltpu.bitcast(a.astype(BF16).astype(F32), jnp.uint32)
    bb = pltpu.bitcast(b.astype(BF16).astype(F32), jnp.uint32)
    return (ab >> 16) | (bb & jnp.uint32(HI16))


def _unpack_bf16_pairs(p):
    lo = pltpu.bitcast(p << 16, F32)
    hi = pltpu.bitcast(p & jnp.uint32(HI16), F32)
    return lo, hi


def _router_kernel(x_ref, nw_ref, wr_ref, br_ref, ha_ref, hb_ref, sel_ref):
    x = x_ref[...]
    ms = jnp.mean(x * x, axis=-1, keepdims=True)
    h = x * lax.rsqrt(ms + EPS) * nw_ref[...]
    Q = h.shape[1] // 4
    ha_ref[...] = _pack_bf16_pairs(h[:, 0:Q], h[:, Q:2 * Q])
    hb_ref[...] = _pack_bf16_pairs(h[:, 2 * Q:3 * Q], h[:, 3 * Q:4 * Q])
    w = wr_ref[...]
    h_hi = h.astype(BF16)
    h_lo = (h - h_hi.astype(F32)).astype(BF16)
    w_hi = w.astype(BF16)
    w_lo = (w - w_hi.astype(F32)).astype(BF16)
    logits = _dot(h_hi, w_hi) + _dot(h_lo, w_hi) + _dot(h_hi, w_lo) + br_ref[...]
    lane = lax.broadcasted_iota(jnp.int32, logits.shape, 1).astype(F32)
    logits = jnp.where(lane < N_EXPERTS, logits, NEG)
    m1 = jnp.max(logits, axis=-1, keepdims=True)
    i1 = jnp.min(jnp.where(logits == m1, lane, float(LANES)), axis=-1, keepdims=True)
    rest = jnp.where(lane == i1, NEG, logits)
    m2 = jnp.max(rest, axis=-1, keepdims=True)
    i2 = jnp.min(jnp.where(rest == m2, lane, float(LANES)), axis=-1, keepdims=True)
    e2 = jnp.exp(m2 - m1)
    p1 = 1.0 / (1.0 + e2)
    p2 = e2 / (1.0 + e2)
    sel_ref[...] = jnp.where(lane == 0.0, i1, jnp.where(lane == 1.0, i2,
                             jnp.where(lane == 2.0, p1, jnp.where(lane == 3.0, p2, 0.0))))


def _router(x, nw, wr, br, tm):
    T, D = x.shape
    Q = D // 4
    return pl.pallas_call(
        _router_kernel,
        grid=(T // tm,),
        in_specs=[pl.BlockSpec((tm, D), lambda i: (i, 0)),
                  pl.BlockSpec((1, D), lambda i: (0, 0)),
                  pl.BlockSpec((D, LANES), lambda i: (0, 0)),
                  pl.BlockSpec((1, LANES), lambda i: (0, 0))],
        out_specs=[pl.BlockSpec((tm, Q), lambda i: (i, 0)),
                   pl.BlockSpec((tm, Q), lambda i: (i, 0)),
                   pl.BlockSpec((tm, LANES), lambda i: (i, 0))],
        out_shape=[jax.ShapeDtypeStruct((T, Q), jnp.uint32), jax.ShapeDtypeStruct((T, Q), jnp.uint32),
                   jax.ShapeDtypeStruct((T, LANES), F32)],
        compiler_params=_cparams("parallel"),
        name="moe_router",
    )(x, nw, wr, br)


def _moe_rank_kernel(sel_ref, pos_ref, meta_ref, cnt_ref, offs_ref, carry_ref, before_ref):
    ph = pl.program_id(0)
    i = pl.program_id(1)
    tm = sel_ref.shape[0]
    lane = lax.broadcasted_iota(jnp.int32, (tm, LANES), 1).astype(F32)
    lane1 = lax.broadcasted_iota(jnp.int32, (1, LANES), 1).astype(F32)
    sel = sel_ref[...]
    i1 = sel[:, 0:1]
    i2 = sel[:, 1:2]
    onehot = jnp.where((lane == i1) | (lane == i2), 1.0, 0.0)
    colsum = jnp.sum(onehot, axis=0, keepdims=True)

    @pl.when((ph == 0) & (i == 0))
    def _():
        cnt_ref[...] = jnp.zeros(cnt_ref.shape, F32)

    @pl.when(ph == 0)
    def _():
        cnt_ref[...] += colsum

    def padded_counts():
        return jnp.floor((cnt_ref[...] + (MOE_TILE - 1.0)) * (1.0 / MOE_TILE)) * MOE_TILE

    @pl.when((ph == 1) & (i == 0))
    def _():
        k = lax.broadcasted_iota(jnp.int32, (LANES, LANES), 0)
        e = lax.broadcasted_iota(jnp.int32, (LANES, LANES), 1)
        upper = (k < e).astype(F32)
        offs_ref[...] = jnp.dot(padded_counts(), upper, preferred_element_type=F32,
                                precision=lax.Precision.HIGHEST)
        carry_ref[...] = jnp.zeros(carry_ref.shape, F32)
        r = lax.broadcasted_iota(jnp.int32, (tm, tm), 0)
        c = lax.broadcasted_iota(jnp.int32, (tm, tm), 1)
        before_ref[...] = (c < r).astype(BF16)

    @pl.when(ph == 1)
    def _():
        rank = _dot(before_ref[...], onehot.astype(BF16)) + carry_ref[0:1, :]
        row = rank + offs_ref[0:1, :]
        pos1 = jnp.sum(jnp.where(lane == i1, row, 0.0), axis=-1, keepdims=True)
        pos2 = jnp.sum(jnp.where(lane == i2, row, 0.0), axis=-1, keepdims=True)
        both = jnp.where(lane == 0.0, pos1, jnp.where(lane == 1.0, pos2, 0.0))
        pos_ref[...] = both.T[0:8, :].astype(jnp.int32)
        carry_ref[...] += colsum

    @pl.when((ph == 1) & (i == pl.num_programs(1) - 1))
    def _():
        nrow = meta_ref.shape[0]
        padded = padded_counts()[0:1, :]
        offs = offs_ref[0:1, :]
        ends = offs + padded
        start = lax.broadcasted_iota(jnp.int32, (nrow, 1), 0).astype(F32) * MOE_TILE
        is_e = lane1 < N_EXPERTS
        te = jnp.sum(jnp.where(is_e & (ends <= start), 1.0, 0.0), axis=-1, keepdims=True)
        te = jnp.minimum(te, N_EXPERTS - 1.0)
        valid_end = jnp.sum(jnp.where(lane1 == te, offs + cnt_ref[0:1, :], 0.0), axis=-1, keepdims=True)
        nv = jnp.clip(valid_end - start, 0.0, MOE_TILE)
        total = jnp.sum(jnp.where(is_e, padded, 0.0), axis=-1, keepdims=True)
        active = jnp.where(start < total, 1.0, 0.0)
        meta_ref[...] = jnp.where(lane1 == 0.0, te, jnp.where(lane1 == 1.0, nv,
                                  jnp.where(lane1 == 2.0, active, 0.0)))


def _moe_rank(sel, tm, meta_rows):
    T = sel.shape[0]
    nt = T // tm
    return pl.pallas_call(
        _moe_rank_kernel,
        grid=(2, nt),
        in_specs=[pl.BlockSpec((tm, LANES), lambda ph, i: (i, 0))],
        out_specs=[pl.BlockSpec((None, 8, tm), lambda ph, i: (i * ph, 0, 0)),
                   pl.BlockSpec((meta_rows, LANES), lambda ph, i: (0, 0))],
        out_shape=[jax.ShapeDtypeStruct((nt, 8, tm), jnp.int32),
                   jax.ShapeDtypeStruct((meta_rows, LANES), F32)],
        scratch_shapes=[pltpu.VMEM((8, LANES), F32), pltpu.VMEM((8, LANES), F32),
                        pltpu.VMEM((8, LANES), F32), pltpu.VMEM((tm, tm), BF16)],
        compiler_params=_cparams("arbitrary", "arbitrary"),
        name="moe_rank",
    )(sel)


def _sc_mesh():
    return plsc.VectorSubcoreMesh(core_axis_name="core", subcore_axis_name="subcore")


def _sc_scatter_rows(x, idx, n_rows):
    n_idx = idx.shape[1]
    nt = x.shape[0] // SC_WINDOW
    width = x.shape[1]

    @functools.partial(pl.kernel, out_type=jax.ShapeDtypeStruct((n_rows, width), x.dtype),
                       mesh=_sc_mesh())
    def scatter_kernel(x_hbm, i_hbm, o_hbm):
        def body(x_vmem, i_vmem):
            pltpu.sync_copy(x_vmem, o_hbm.at[i_vmem.at[0]])

        pltpu.emit_pipeline(
            body, grid=(n_idx // SC_WINDOW,),
            in_specs=[pl.BlockSpec((SC_WINDOW, width), lambda i: (i % nt, 0)),
                      pl.BlockSpec((1, SC_WINDOW), lambda i: (0, i))],
            out_specs=[],
            core_axis_name=("core", "subcore"), dimension_semantics=(pltpu.PARALLEL,),
        )(x_hbm, i_hbm)

    return scatter_kernel(x, idx)


def _sc_gather_rows(x, idx):
    n_idx = idx.shape[1]
    width = x.shape[1]

    @functools.partial(pl.kernel, out_type=jax.ShapeDtypeStruct((n_idx, width), x.dtype),
                       mesh=_sc_mesh())
    def gather_kernel(x_hbm, i_hbm, o_hbm):
        def body(i_vmem, o_vmem):
            pltpu.sync_copy(x_hbm.at[i_vmem.at[0]], o_vmem)

        pltpu.emit_pipeline(
            body, grid=(n_idx // SC_WINDOW,),
            in_specs=[pl.BlockSpec((1, SC_WINDOW), lambda i: (0, i))],
            out_specs=[pl.BlockSpec((SC_WINDOW, width), lambda i: (i, 0))],
            core_axis_name=("core", "subcore"), dimension_semantics=(pltpu.PARALLEL,),
        )(i_hbm, o_hbm)

    return gather_kernel(x, idx)


def _moe_group_kernel(te_ref, nv_ref, na_ref, xa_ref, xb_ref, wg_ref, wu_ref, wd_ref,
                      ya_ref, yb_ref, h_ref, acc_ref, a_ref):
    j = pl.program_id(0)
    f = pl.program_id(1)

    @pl.when(j < na_ref[0])
    def _():
        @pl.when(f == 0)
        def _():
            tm = h_ref.shape[0]
            Q = xa_ref.shape[1]
            valid = lax.broadcasted_iota(jnp.int32, (tm, 1), 0) < nv_ref[j]
            for src, c0 in ((xa_ref, 0), (xb_ref, 2 * Q)):
                lo, hi = _unpack_bf16_pairs(src[...])
                h_ref[:, c0:c0 + Q] = jnp.where(valid, lo, 0.0).astype(BF16)
                h_ref[:, c0 + Q:c0 + 2 * Q] = jnp.where(valid, hi, 0.0).astype(BF16)
            acc_ref[...] = jnp.zeros(acc_ref.shape, F32)

        acc_ref[...] += _swiglu_block(h_ref[...], wg_ref, wu_ref, wd_ref, a_ref)

        @pl.when(f == pl.num_programs(1) - 1)
        def _():
            Q = ya_ref.shape[1]
            ya_ref[...] = _pack_bf16_pairs(acc_ref[:, 0:Q], acc_ref[:, Q:2 * Q])
            yb_ref[...] = _pack_bf16_pairs(acc_ref[:, 2 * Q:3 * Q], acc_ref[:, 3 * Q:4 * Q])


def _moe_group(te, nv, na, xa, xb, wg, wu, wd, tf):
    R, Q = xa.shape
    E, D, F = wg.shape
    nf = F // tf
    tile = lambda j, f, te, nv, na: (jnp.minimum(j, na[0] - 1), 0)
    ff = lambda j, f, na: jnp.where(j < na[0], f, nf - 1)
    grid_spec = pltpu.PrefetchScalarGridSpec(
        num_scalar_prefetch=3,
        grid=(R // MOE_TILE, nf),
        in_specs=[pl.BlockSpec((MOE_TILE, Q), tile),
                  pl.BlockSpec((MOE_TILE, Q), tile),
                  pl.BlockSpec((None, D, tf), lambda j, f, te, nv, na: (te[j], 0, ff(j, f, na))),
                  pl.BlockSpec((None, D, tf), lambda j, f, te, nv, na: (te[j], 0, ff(j, f, na))),
                  pl.BlockSpec((None, tf, D), lambda j, f, te, nv, na: (te[j], ff(j, f, na), 0))],
        out_specs=[pl.BlockSpec((MOE_TILE, Q), tile), pl.BlockSpec((MOE_TILE, Q), tile)],
        scratch_shapes=[pltpu.VMEM((MOE_TILE, D), BF16), pltpu.VMEM((MOE_TILE, D), F32),
                        pltpu.VMEM((MOE_TILE, tf), BF16)],
    )
    return pl.pallas_call(
        _moe_group_kernel,
        grid_spec=grid_spec,
        out_shape=[jax.ShapeDtypeStruct((R, Q), jnp.uint32), jax.ShapeDtypeStruct((R, Q), jnp.uint32)],
        compiler_params=_cparams("arbitrary", "arbitrary"),
        name="moe_experts",
    )(te, nv, na, xa, xb, wg, wu, wd)


def _moe_combine_kernel(x_ref, sel_ref, g1_ref, g2_ref, *rest):
    o_ref = rest[-1]
    sel = sel_ref[...]
    p1 = sel[:, 2:3]
    p2 = sel[:, 3:4]
    Q = g1_ref.shape[1]
    lo1, hi1 = _unpack_bf16_pairs(g1_ref[...])
    lo2, hi2 = _unpack_bf16_pairs(g2_ref[...])
    o_ref[:, 0:Q] = x_ref[:, 0:Q] + (p1 * lo1 + p2 * lo2)
    o_ref[:, Q:2 * Q] = x_ref[:, Q:2 * Q] + (p1 * hi1 + p2 * hi2)


def _moe_combine_half(x, sel, g, half, partial_out, tm):
    T, D = x.shape
    Q = g.shape[1]
    nt = T // tm
    in_specs = [pl.BlockSpec((tm, 2 * Q), lambda i: (i, half)),
                pl.BlockSpec((tm, LANES), lambda i: (i, 0)),
                pl.BlockSpec((tm, Q), lambda i: (i, 0)),
                pl.BlockSpec((tm, Q), lambda i: (i + nt, 0))]
    args = [x, sel, g, g]
    aliases = {}
    if partial_out is not None:
        in_specs.append(pl.BlockSpec(memory_space=pl.ANY))
        args.append(partial_out)
        aliases = {4: 0}
    return pl.pallas_call(
        _moe_combine_kernel,
        grid=(nt,),
        in_specs=in_specs,
        out_specs=pl.BlockSpec((tm, 2 * Q), lambda i: (i, half)),
        out_shape=jax.ShapeDtypeStruct((T, D), F32),
        input_output_aliases=aliases,
        compiler_params=_cparams("parallel"),
        name="moe_combine",
    )(*args)


def _cast_kernel(x_ref, o_ref):
    o_ref[...] = x_ref[...].astype(o_ref.dtype)


def _cast_bf16(w, rows):
    cols = w.shape[-1]
    w2 = w.reshape(-1, cols)
    out = pl.pallas_call(
        _cast_kernel,
        grid=(w2.shape[0] // rows,),
        in_specs=[pl.BlockSpec((rows, cols), lambda i: (i, 0))],
        out_specs=pl.BlockSpec((rows, cols), lambda i: (i, 0)),
        out_shape=jax.ShapeDtypeStruct(w2.shape, BF16),
        compiler_params=_cparams("parallel"),
        name="cast_bf16",
    )(w2)
    return out.reshape(w.shape)


def _moe(x, nw, w_router, b_router, wg, wu, wd):
    T, D = x.shape
    wr = jnp.pad(w_router, ((0, 0), (0, LANES - N_EXPERTS)))
    br = jnp.pad(b_router, (0, LANES - N_EXPERTS)).reshape(1, LANES)
    ha, hb, sel = _router(x, nw, wr, br, 1024)
    n_rows = TOP_K * T + N_EXPERTS * MOE_TILE
    n_tiles = n_rows // MOE_TILE
    pos, meta = _moe_rank(sel, 1024, 256)
    idx = jnp.concatenate([pos[:, 0, :].reshape(1, T), pos[:, 1, :].reshape(1, T)], axis=1)
    te = meta[:n_tiles, 0].astype(jnp.int32)
    nv = meta[:n_tiles, 1].astype(jnp.int32)
    na = jnp.sum(meta[:n_tiles, 2]).astype(jnp.int32).reshape(1)
    te = jnp.where(jnp.arange(n_tiles) < na[0], te, te[na[0] - 1])
    xa = _sc_scatter_rows(ha, idx, n_rows)
    xb = _sc_scatter_rows(hb, idx, n_rows)
    ya, yb = _moe_group(te, nv, na, xa, xb, wg, wu, wd, wg.shape[2] // 2)
    ga = _sc_gather_rows(ya, idx)
    gb = _sc_gather_rows(yb, idx)
    out = _moe_combine_half(x, sel, ga, 0, None, 1024)
    return _moe_combine_half(x, sel, gb, 1, out, 1024)


def _split_w_in(w_in):
    sizes = (512, 512, 512, 512, 768, 768, 768, 256, 256, 512, 512, 4, 4, 1024, 1024, 1024)
    offs = [0]
    for s in sizes:
        offs.append(offs[-1] + s)
    part = lambda i: w_in[..., offs[i]:offs[i + 1]]
    rq, rk, rv, rg, dq, dk, dv, mq, mk, mv, mo, mi, mf, g_ret, g_da, g_ml = (part(i) for i in range(16))
    cols = []
    for p in range(2):
        for g in range(3):
            for t in (dq, dk, dv):
                cols.append(t[..., g * 256 + p * 128:g * 256 + (p + 1) * 128])
    wb = jnp.concatenate([rq, rk, rv, rg, g_ret, g_da, g_ml, mq, mk, mv, mo] + cols,
                         axis=-1).astype(BF16)
    pad = jnp.zeros(w_in.shape[:-1] + (LANES - 8,), w_in.dtype)
    wf = jnp.concatenate([mi, mf, pad], axis=-1).astype(BF16)
    return wb, wf


def _rope_tables(S):
    half = LANES // 2
    inv = jnp.power(ROPE_BASE, -jnp.arange(half, dtype=F32) / half)
    ang = jnp.arange(S, dtype=F32)[:, None] * inv[None, :]
    cos = jnp.cos(ang)
    sin = jnp.sin(ang)
    return jnp.concatenate([cos, cos], axis=1), jnp.concatenate([-sin, sin], axis=1)


def kernel(x, norm1_w, w_in, ret_norm_w, da_q_norm_w, da_k_norm_w, ml_conv_w, ml_i_bias, ml_f_bias,
           ml_norm_w, w_br_ret, w_br_da, w_br_ml, w_out, norm2_w, ffn_w_gate, ffn_w_up, ffn_w_down,
           moe_w_router, moe_b_router, moe_w_gate, moe_w_up, moe_w_down):
    B, S, D = x.shape
    T = B * S
    depth = w_in.shape[0]
    cos, sin = _rope_tables(S)
    wb_all, wf_all = _split_w_in(w_in)
    xt = x.reshape(T, D)
    for layer in range(depth):
        wb, wf = wb_all[layer], wf_all[layer]
        nw1 = norm1_w[layer].reshape(1, D)
        zb, zf = _inproj(xt, nw1, wb, wf, ml_conv_w[layer], 4, S, 1280)
        o_ret = _retention(zb, cos, sin, ret_norm_w[layer].reshape(1, -1), B, S)
        wq = jnp.tile(da_q_norm_w[layer], 2).reshape(1, LANES)
        wk = jnp.tile(da_k_norm_w[layer], 2).reshape(1, LANES)
        o_da = _dilated_attention(zb, 52, wq, wk, B, S)
        gate_bias = jnp.concatenate([ml_i_bias[layer], ml_f_bias[layer],
                                     jnp.zeros((LANES - 2 * ML_HEADS,), F32)]).reshape(1, LANES)
        o_ml = _mlstm(zb, zf, gate_bias, ml_norm_w[layer].reshape(1, -1), B, S,
                      qk_blk=10, gate_blk=0)
        xt = _merge(xt, o_ret, o_da, o_ml, zb, 2,
                    w_br_ret[layer].astype(BF16), w_br_da[layer].astype(BF16),
                    w_br_ml[layer].astype(BF16), w_out[layer].astype(BF16), 512)
        nw2 = norm2_w[layer].reshape(1, D)
        j = layer // 2
        if layer % 2 == 0:
            xt = _ffn(xt, nw2, ffn_w_gate[j].astype(BF16), ffn_w_up[j].astype(BF16),
                      ffn_w_down[j].astype(BF16), 512, 2816)
        else:
            xt = _moe(xt, nw2, moe_w_router[j], moe_b_router[j], _cast_bf16(moe_w_gate[j], 512),
                      _cast_bf16(moe_w_up[j], 512), _cast_bf16(moe_w_down[j], 2048))
    return xt.reshape(B, S, D)
```

```python
import functools
import math

import jax
import jax.numpy as jnp
from jax import lax
from jax.experimental import pallas as pl
from jax.experimental.pallas import tpu as pltpu
from jax.experimental.pallas import tpu_sc as plsc

F32 = jnp.float32
BF16 = jnp.bfloat16

EPS = 1e-6
D_MODEL = 1024
CHUNK = 128
LANES = 128
MXU_N = 256
ROPE_BASE = 10000.0
RET_HEADS = 4
RET_SEQS = 2
DA_GROUPS = ((1, 16), (4, 4), (16, 1))
DA_DH = 64
DA_UNROLL = 4
DA_MAX_STRIDE = 4
ML_HEADS = 4
ML_DK = 64
ML_SEQS = 2
N_EXPERTS = 8
TOP_K = 2
MOE_TILE = 512
SC_WINDOW = 128
VMEM_LIMIT = 56 * 1024 * 1024

NEG = -1e30


def _cparams(*sem):
    return pltpu.CompilerParams(dimension_semantics=sem, vmem_limit_bytes=VMEM_LIMIT)


def _dot(a, b):
    return jnp.dot(a, b, preferred_element_type=F32)


def _dot_nt(a, b):
    return lax.dot_general(a, b, (((1,), (1,)), ((), ())), preferred_element_type=F32)


def _dot_tn(a, b):
    return lax.dot_general(a, b, (((0,), (0,)), ((), ())), preferred_element_type=F32)


def _sigmoid(x):
    return 1.0 / (1.0 + jnp.exp(-x))


def _silu(x):
    return x * _sigmoid(x)


def _inproj_kernel(conv_blk, x_ref, nw_ref, wb_ref, wf_ref, cw_ref, zb_ref, zf_ref, h_ref, cv_ref):
    j = pl.program_id(1)
    tm = x_ref.shape[0]

    @pl.when(j == 0)
    def _():
        rc = 512
        for r in range(tm // rc):
            rows = pl.ds(r * rc, rc)
            x = x_ref[rows, :]
            ms = jnp.mean(x * x, axis=-1, keepdims=True)
            h = (x * lax.rsqrt(ms + EPS) * nw_ref[...]).astype(BF16)
            h_ref[rows, :] = h
            zb_ref[rows, :] = _dot(h, wb_ref[...]).astype(zb_ref.dtype)
            zf_ref[rows, :] = _dot(h, wf_ref[...])

    @pl.when((j > 0) & (j != conv_blk))
    def _():
        zb_ref[...] = _dot(h_ref[...], wb_ref[...]).astype(zb_ref.dtype)

    @pl.when(j == conv_blk)
    def _():
        h = h_ref[...]
        cw = cw_ref.shape[1]
        pad = cv_ref.shape[0] - tm
        cv_ref[0:pad, :] = jnp.zeros((pad, cw), F32)
        cv_ref[pad:pad + tm, :] = _dot(h, wb_ref[:, 0:cw])
        taps = cw_ref.shape[0]
        rb = 256

        def conv(piece, cs):
            cols = slice(cs * LANES, (cs + 1) * LANES)
            r0 = pad + piece * rb - (taps - 1)
            acc = cw_ref[0:1, cols] * cv_ref[pl.ds(r0, rb), cols]
            for i in range(1, taps):
                acc = acc + cw_ref[i:i + 1, cols] * cv_ref[pl.ds(r0 + i, rb), cols]
            scale = ML_DK ** -0.5 if cs >= cw // LANES // 2 else 1.0
            zb_ref[piece * rb:(piece + 1) * rb, cols] = (_silu(acc) * scale).astype(zb_ref.dtype)

        todo = [(piece, cs) for piece in range(tm // rb) for cs in range(cw // LANES)]
        chunks = list(range(cw, zb_ref.shape[1], MXU_N))
        per = -(-len(todo) // len(chunks))
        for n, c0 in enumerate(chunks):
            c1 = min(c0 + MXU_N, zb_ref.shape[1])
            zb_ref[:, c0:c1] = _dot(h, wb_ref[:, c0:c1]).astype(zb_ref.dtype)
            for piece, cs in todo[n * per:(n + 1) * per]:
                conv(piece, cs)


def _inproj(x, nw, wb, wf, conv_w, conv_blk, tm, tnb):
    T, D = x.shape
    return pl.pallas_call(
        functools.partial(_inproj_kernel, conv_blk),
        grid=(T // tm, wb.shape[1] // tnb),
        in_specs=[pl.BlockSpec((tm, D), lambda i, j: (i, 0)),
                  pl.BlockSpec((1, D), lambda i, j: (0, 0)),
                  pl.BlockSpec((D, tnb), lambda i, j: (0, j)),
                  pl.BlockSpec(wf.shape, lambda i, j: (0, 0)),
                  pl.BlockSpec(conv_w.shape, lambda i, j: (0, 0))],
        out_specs=[pl.BlockSpec((tm, tnb), lambda i, j: (i, j)),
                   pl.BlockSpec((tm, wf.shape[1]), lambda i, j: (i, 0))],
        out_shape=[jax.ShapeDtypeStruct((T, wb.shape[1]), BF16),
                   jax.ShapeDtypeStruct((T, wf.shape[1]), F32)],
        scratch_shapes=[pltpu.VMEM((tm, D), BF16), pltpu.VMEM((tm + 8, conv_w.shape[1]), F32)],
        compiler_params=_cparams("parallel", "arbitrary"),
        name="inproj",
    )(x, nw, wb, wf, conv_w)


def _retention_kernel(z_ref, cos_ref, sin_ref, nw_ref, o_ref, dec_ref, st_ref):
    S = z_ref.shape[0] // RET_SEQS
    n_chunks = S // CHUNK
    H = RET_HEADS
    HW = H * LANES
    row = lax.broadcasted_iota(jnp.int32, (CHUNK, CHUNK), 0).astype(F32)
    col = lax.broadcasted_iota(jnp.int32, (CHUNK, CHUNK), 1).astype(F32)
    lgs = [math.log1p(-(2.0 ** (-5.0 - h))) for h in range(H)]
    for h, lg in enumerate(lgs):
        rel = row - col
        dec_ref[h] = jnp.where(rel >= 0, jnp.exp(lg * jnp.maximum(rel, 0.0)), 0.0)
        dec_ref[H + h] = jnp.exp(lg * (row + 1.0))
        dec_ref[2 * H + h] = jnp.exp(lg * (CHUNK - 1.0 - row)) * (LANES ** -0.5)
    st_ref[...] = jnp.zeros(st_ref.shape, F32)
    units = [(sq, h) for sq in range(RET_SEQS) for h in range(H)]

    def body(n, carry):
        r0 = pl.multiple_of(n * CHUNK, CHUNK)
        cos = cos_ref[pl.ds(r0, CHUNK), :]
        sin = sin_ref[pl.ds(r0, CHUNK), :]
        rows_of = [pl.ds(pl.multiple_of(sq * S + n * CHUNK, CHUNK), CHUNK) for sq in range(RET_SEQS)]
        vs, ss, iqs, kvs = [], [], [], []
        for u, (sq, h) in enumerate(units):
            rows = rows_of[sq]
            q = z_ref[rows, h * LANES:(h + 1) * LANES].astype(F32)
            k = z_ref[rows, HW + h * LANES:HW + (h + 1) * LANES].astype(F32)
            v = z_ref[rows, 2 * HW + h * LANES:2 * HW + (h + 1) * LANES]
            q = q * cos + pltpu.roll(q, LANES // 2, 1) * sin
            k = k * cos + pltpu.roll(k, LANES // 2, 1) * sin
            qb = q.astype(BF16)
            kb = (k * (LANES ** -0.5)).astype(BF16)
            kd = (k * dec_ref[2 * H + h]).astype(BF16)
            ss.append(_dot_nt(qb, kb))
            iqs.append(_dot(qb, st_ref[u].astype(BF16)))
            kvs.append(_dot_tn(kd, v))
            vs.append(v)
        ps = [(ss[u] * dec_ref[h]).astype(BF16) for u, (_, h) in enumerate(units)]
        os_ = [_dot(ps[u], vs[u]) + iqs[u] * dec_ref[H + h] for u, (_, h) in enumerate(units)]
        for u, (sq, h) in enumerate(units):
            rows = rows_of[sq]
            st_ref[u] = st_ref[u] * math.exp(lgs[h] * CHUNK) + kvs[u]
            o = os_[u]
            ms = jnp.mean(o * o, axis=-1, keepdims=True)
            o = o * lax.rsqrt(ms + EPS) * nw_ref[:, h * LANES:(h + 1) * LANES]
            g = z_ref[rows, 3 * HW + h * LANES:3 * HW + (h + 1) * LANES].astype(F32)
            o_ref[rows, h * LANES:(h + 1) * LANES] = (o * _silu(g)).astype(o_ref.dtype)
        return carry

    lax.fori_loop(0, n_chunks, body, 0)


def _retention(zb, cos, sin, nw, B, S):
    T = B * S
    W = RET_HEADS * LANES
    R = RET_SEQS * S
    return pl.pallas_call(
        _retention_kernel,
        grid=(B // RET_SEQS,),
        in_specs=[pl.BlockSpec((R, 4 * W), lambda b: (b, 0)),
                  pl.BlockSpec((S, LANES), lambda b: (0, 0)),
                  pl.BlockSpec((S, LANES), lambda b: (0, 0)),
                  pl.BlockSpec((1, W), lambda b: (0, 0))],
        out_specs=pl.BlockSpec((R, W), lambda b: (b, 0)),
        out_shape=jax.ShapeDtypeStruct((T, W), BF16),
        scratch_shapes=[pltpu.VMEM((3 * RET_HEADS, CHUNK, LANES), F32),
                        pltpu.VMEM((RET_SEQS * RET_HEADS, LANES, LANES), F32)],
        compiler_params=_cparams("parallel"),
        name="retention",
    )(zb, cos, sin, nw)


def _da_kernel(*refs):
    z_refs = refs[:9]
    wq_ref, wk_ref, o_ref, qn_ref, kn_ref, v_ref, np_ref, lp_ref, mp_ref = refs[9:18]
    nn_refs, ln_refs, mn_refs = refs[18:21], refs[21:24], refs[24:27]
    stage_refs = refs[27:30]
    S = o_ref.shape[0]
    lane = lax.broadcasted_iota(jnp.int32, (1, LANES), 1)
    lo = lane < DA_DH
    row = lax.broadcasted_iota(jnp.int32, (CHUNK, CHUNK), 0)
    col = lax.broadcasted_iota(jnp.int32, (CHUNK, CHUNK), 1)
    mask_cur = col <= row
    row2 = lax.broadcasted_iota(jnp.int32, (CHUNK, 2 * CHUNK), 0)
    col2 = lax.broadcasted_iota(jnp.int32, (CHUNK, 2 * CHUNK), 1)
    mask_band = (col2 >= row2) & (col2 <= row2 + CHUNK)

    seg = (lax.broadcasted_iota(jnp.int32, (LANES, LANES), 0) // DA_DH
           == lax.broadcasted_iota(jnp.int32, (LANES, LANES), 1) // DA_DH).astype(BF16) * (1.0 / DA_DH)

    def head_norm(x, w):
        x2 = x * x
        hi = x2.astype(BF16)
        lo_part = (x2 - hi.astype(F32)).astype(BF16)
        ms = _dot(hi, seg) + _dot(lo_part, seg)
        return x * lax.rsqrt(ms + EPS) * w

    RB = 256
    for g, (dil, nb) in enumerate(DA_GROUPS):
        L = S // dil
        zq_ref, zk_ref, zv_ref = z_refs[3 * g:3 * g + 3]
        if dil > 1:
            for piece in range(S // RB):
                rows = pl.ds(piece * RB, RB)
                for src_ref, dst_ref in zip((zq_ref, zk_ref, zv_ref), stage_refs):
                    dst_ref[rows, :] = src_ref[rows, :].astype(F32)
            zq_ref, zk_ref, zv_ref = stage_refs
        first = min(dil, DA_MAX_STRIDE)
        if dil > first:
            mid_refs = (nn_refs[g], ln_refs[g], mn_refs[g])
            sub = S // first
            for b in range(first):
                for piece in range(sub // RB):
                    src = pl.ds(b + first * piece * RB, RB, stride=first)
                    dst = pl.ds(b * sub + piece * RB, RB)
                    for src_ref, dst_ref in zip((zq_ref, zk_ref, zv_ref), mid_refs):
                        dst_ref[dst, :] = src_ref[src, :]
            zq_ref, zk_ref, zv_ref = mid_refs
        rest = dil // first

        def source_rows(rho, piece, n_rows):
            if dil == 1:
                return pl.ds(piece * n_rows, n_rows)
            b, a = rho % first, rho // first
            if rest == 1:
                return pl.ds(b + first * piece * n_rows, n_rows, stride=first)
            return pl.ds(b * (S // first) + a + rest * piece * n_rows, n_rows, stride=rest)

        for rho in range(dil):
            for piece in range(max(L // RB, 1)):
                n_rows = min(RB, L)
                src = source_rows(rho, piece, n_rows)
                dst = pl.ds(rho * L + piece * n_rows, n_rows)
                q = zq_ref[src, :].astype(F32)
                k = zk_ref[src, :].astype(F32)
                v = zv_ref[src, :]
                qn_ref[dst, :] = (head_norm(q, wq_ref[...]) * (DA_DH ** -0.5)).astype(BF16)
                kn_ref[dst, :] = head_norm(k, wk_ref[...]).astype(BF16)
                v_ref[dst, :] = v.astype(BF16)

        def batch(r0, chained, first_has_prev):
            chains = []
            for u in range(DA_UNROLL):
                rows = pl.ds(r0 + u * CHUNK, CHUNK)
                if chained and (u > 0 or first_has_prev):
                    keys, mask = pl.ds(r0 + (u - 1) * CHUNK, 2 * CHUNK), mask_band
                else:
                    keys, mask = rows, mask_cur
                k = kn_ref[keys, :]
                v = v_ref[keys, :]
                vaug = jnp.concatenate([v, jnp.ones_like(v)], axis=1)
                q = qn_ref[rows, :]
                zero = jnp.zeros_like(q)
                for qh in (jnp.where(lo, q, zero), jnp.where(lo, zero, q)):
                    chains.append((_dot_nt(qh, k), mask, vaug))
            probs = []
            for s, mask, _ in chains:
                s = jnp.where(mask, s, NEG)
                m = jnp.max(s, axis=-1, keepdims=True)
                probs.append((jnp.exp(s - m).astype(BF16), m))
            accs = [_dot(p, vaug) for (p, _), (_, _, vaug) in zip(probs, chains)]
            nums, dens, maxs = [], [], []
            for u in range(DA_UNROLL):
                a0, a1 = accs[2 * u], accs[2 * u + 1]
                nums.append(jnp.where(lo, a0[:, :LANES], a1[:, :LANES]))
                dens.append(jnp.where(lo, a0[:, LANES:], a1[:, LANES:]))
                maxs.append(jnp.where(lo, probs[2 * u][1], probs[2 * u + 1][1]))
            rows = pl.ds(r0, DA_UNROLL * CHUNK)
            np_ref[rows, :] = jnp.concatenate(nums, axis=0)
            lp_ref[rows, :] = jnp.concatenate(dens, axis=0)
            mp_ref[rows, :] = jnp.concatenate(maxs, axis=0)

        span = DA_UNROLL * CHUNK
        if nb == 1:
            def singles(i, c):
                batch(pl.multiple_of(i * span, span), False, False)
                return c
            lax.fori_loop(0, S // span, singles, 0)
        else:
            def segment(sgi, c, nb=nb):
                base = pl.multiple_of(sgi * (nb * CHUNK), span)
                batch(base, True, False)
                if nb > DA_UNROLL:
                    def inner(n, c2):
                        batch(pl.multiple_of(base + n * span, span), True, True)
                        return c2
                    lax.fori_loop(1, nb // DA_UNROLL, inner, 0)
                return c
            lax.fori_loop(0, dil, segment, 0)

        outs = ((np_ref, nn_refs[g]), (lp_ref, ln_refs[g]), (mp_ref, mn_refs[g]))
        for rho in range(dil):
            for piece in range(max(L // RB, 1)):
                n_rows = min(RB, L)
                dst = source_rows(rho, piece, n_rows)
                src = pl.ds(rho * L + piece * n_rows, n_rows)
                for k_out, (perm_ref, nat_ref) in enumerate(outs):
                    (stage_refs[k_out] if rest > 1 else nat_ref)[dst, :] = perm_ref[src, :]
        if rest > 1:
            sub = S // first
            for b in range(first):
                for piece in range(sub // RB):
                    dst = pl.ds(b + first * piece * RB, RB, stride=first)
                    src = pl.ds(b * sub + piece * RB, RB)
                    for k_out, (_, nat_ref) in enumerate(outs):
                        nat_ref[dst, :] = stage_refs[k_out][src, :]

    for piece in range(S // RB):
        rows = pl.ds(piece * RB, RB)
        m = jnp.maximum(jnp.maximum(mn_refs[0][rows, :], mn_refs[1][rows, :]), mn_refs[2][rows, :])
        num = jnp.zeros((RB, LANES), F32)
        den = jnp.zeros((RB, LANES), F32)
        for g in range(3):
            e = jnp.exp(mn_refs[g][rows, :] - m)
            num = num + e * nn_refs[g][rows, :]
            den = den + e * ln_refs[g][rows, :]
        o_ref[rows, :] = (num / den).astype(o_ref.dtype)


def _dilated_attention(zb, slab0, wq, wk, B, S):
    T = B * S
    slab = lambda k: pl.BlockSpec((S, LANES), lambda b, p: (b, slab0 + 9 * p + k))
    return pl.pallas_call(
        _da_kernel,
        grid=(B, 2),
        in_specs=[slab(k) for k in range(9)] + [
                  pl.BlockSpec((1, LANES), lambda b, p: (0, 0)),
                  pl.BlockSpec((1, LANES), lambda b, p: (0, 0))],
        out_specs=pl.BlockSpec((S, LANES), lambda b, p: (b, p)),
        out_shape=jax.ShapeDtypeStruct((T, 2 * LANES), BF16),
        scratch_shapes=[pltpu.VMEM((S, LANES), BF16), pltpu.VMEM((S, LANES), BF16),
                        pltpu.VMEM((S, LANES), BF16),
                        pltpu.VMEM((S, LANES), F32), pltpu.VMEM((S, LANES), F32),
                        pltpu.VMEM((S, LANES), F32)] + [pltpu.VMEM((S, LANES), F32)] * 12,
        compiler_params=_cparams("parallel", "arbitrary"),
        name="dilated_attention",
    )(*([zb] * 9), wq, wk)


def _mlstm_kernel(qk_ref, v_ref, og_ref, gate_ref, gb_ref, nw_ref, o_ref, st_ref, e_ref, et_ref):
    S = qk_ref.shape[0] // ML_SEQS
    n_chunks = S // CHUNK
    W = qk_ref.shape[1]
    lane = lax.broadcasted_iota(jnp.int32, (1, LANES), 1)
    lo = lane < ML_DK
    row = lax.broadcasted_iota(jnp.int32, (CHUNK, CHUNK), 0)
    col = lax.broadcasted_iota(jnp.int32, (CHUNK, CHUNK), 1)
    causal = col <= row
    tril = causal.astype(F32)
    srow = lax.broadcasted_iota(jnp.int32, (LANES, 1), 0) < ML_DK
    st_ref[...] = jnp.zeros(st_ref.shape, F32)

    is_f = (lane >= ML_HEADS) & (lane < 2 * ML_HEADS)
    for n in range(ML_SEQS * n_chunks):
        rows = pl.ds(n * CHUNK, CHUNK)
        gp = gate_ref[rows, :] + gb_ref[...]
        logf = jnp.minimum(gp, 0.0) - jnp.log(1.0 + jnp.exp(-jnp.abs(gp)))
        cum = jnp.dot(tril, jnp.where(is_f, logf, 0.0), preferred_element_type=F32,
                      precision=lax.Precision.HIGHEST)
        e = jnp.where(lane < ML_HEADS, gp, cum)
        e_ref[rows, :] = e
        et_ref[n] = e.T[0:2 * ML_HEADS, :]

    units = [(sq, h) for sq in range(ML_SEQS) for h in range(ML_HEADS)]

    def chunk(n, mms):
        rows_of = [pl.ds(pl.multiple_of(sq * S + n * CHUNK, CHUNK), CHUNK) for sq in range(ML_SEQS)]
        es = [e_ref[rows_of[sq], :] for sq in range(ML_SEQS)]
        ets = [et_ref[sq * n_chunks + n] for sq in range(ML_SEQS)]
        khs, vaugs, s_raws, iqs = [], [], [], []
        for sq in range(ML_SEQS):
            rows = rows_of[sq]
            for pair in range(2):
                q2 = qk_ref[rows, pair * LANES:(pair + 1) * LANES]
                k2 = qk_ref[rows, W // 2 + pair * LANES:W // 2 + (pair + 1) * LANES]
                zero = jnp.zeros_like(q2)
                stb = st_ref[2 * sq + pair].astype(BF16)
                for hs in range(2):
                    h = 2 * pair + hs
                    qh = jnp.where(lo, q2, zero) if hs == 0 else jnp.where(lo, zero, q2)
                    kh = jnp.where(lo, k2, zero) if hs == 0 else jnp.where(lo, zero, k2)
                    vh = v_ref[rows, h * LANES:(h + 1) * LANES]
                    s_raws.append(_dot_nt(qh, k2))
                    iqs.append(_dot(qh, stb))
                    khs.append(kh)
                    vaugs.append(jnp.concatenate([vh, jnp.ones_like(vh)], axis=-1))
        ps, wis, ms_, kws, wcs, new_mms = [], [], [], [], [], []
        for u, (sq, h) in enumerate(units):
            e, et = es[sq], ets[sq]
            mm = mms[u]
            i_b = jnp.broadcast_to(e[:, h:h + 1], (CHUNK, LANES))
            a_b = jnp.broadcast_to(e[:, ML_HEADS + h:ML_HEADS + h + 1], (CHUNK, LANES))
            c_row = et[h:h + 1, :] - et[ML_HEADS + h:ML_HEADS + h + 1, :]
            dmat = jnp.where(causal, a_b + c_row, NEG)
            inter = a_b + mm
            m = jnp.maximum(inter, jnp.max(dmat, axis=-1, keepdims=True))
            ps.append((s_raws[u] * jnp.exp(dmat - m)).astype(BF16))
            wis.append(jnp.exp(inter - m))
            ms_.append(m)
            m_new = m[CHUNK - 1:CHUNK, :]
            a_last = a_b[CHUNK - 1:CHUNK, :]
            wk = jnp.exp(a_last - a_b + i_b - m_new)
            kws.append((khs[u].astype(F32) * wk).astype(BF16))
            wc = jnp.exp(a_last + mm - m_new)
            wcs.append(jnp.concatenate([wc, wc], axis=1))
            new_mms.append(m_new)
        accs = [_dot(ps[u], vaugs[u]) + jnp.concatenate([wis[u], wis[u]], axis=1) * iqs[u]
                for u in range(len(units))]
        upds = [_dot_tn(kws[u], vaugs[u]) for u in range(len(units))]
        for sp in range(2 * ML_SEQS):
            u0, u1 = 2 * sp, 2 * sp + 1
            st_ref[sp] = jnp.where(srow, wcs[u0], wcs[u1]) * st_ref[sp] + (upds[u0] + upds[u1])
        for u, (sq, h) in enumerate(units):
            num = accs[u][:, :LANES]
            den = accs[u][:, LANES:]
            hv = num / jnp.maximum(jnp.abs(den), jnp.exp(-ms_[u]))
            var = jnp.mean(hv * hv, axis=-1, keepdims=True)
            hv = hv * lax.rsqrt(var + EPS) * nw_ref[:, h * LANES:(h + 1) * LANES]
            og = og_ref[rows_of[sq], h * LANES:(h + 1) * LANES].astype(F32)
            o_ref[rows_of[sq], h * LANES:(h + 1) * LANES] = (hv * _sigmoid(og)).astype(o_ref.dtype)
        return tuple(new_mms)

    lax.fori_loop(0, n_chunks, chunk, tuple(jnp.zeros((1, LANES), F32) for _ in units))


def _mlstm(zb, zf, gate_bias, nw, B, S, qk_blk, gate_blk):
    T = B * S
    W = ML_HEADS * LANES
    R = ML_SEQS * S
    return pl.pallas_call(
        _mlstm_kernel,
        grid=(B // ML_SEQS,),
        in_specs=[pl.BlockSpec((R, W), lambda b: (b, qk_blk)),
                  pl.BlockSpec((R, W), lambda b: (b, qk_blk + 1)),
                  pl.BlockSpec((R, W), lambda b: (b, qk_blk + 2)),
                  pl.BlockSpec((R, LANES), lambda b: (b, gate_blk)),
                  pl.BlockSpec((1, LANES), lambda b: (0, 0)),
                  pl.BlockSpec((1, W), lambda b: (0, 0))],
        out_specs=pl.BlockSpec((R, W), lambda b: (b, 0)),
        out_shape=jax.ShapeDtypeStruct((T, W), BF16),
        scratch_shapes=[pltpu.VMEM((2 * ML_SEQS, LANES, 2 * LANES), F32),
                        pltpu.VMEM((R, LANES), F32),
                        pltpu.VMEM((R // CHUNK, 2 * ML_HEADS, LANES), F32)],
        compiler_params=_cparams("parallel"),
        name="mlstm",
    )(zb, zb, zb, zf, gate_bias, nw)


def _merge_kernel(x_ref, oret_ref, oda_ref, oml_ref, gr_ref, gd_ref, gm_ref,
                  wr_ref, wd_ref, wm_ref, wo_ref, o_ref):
    y = _sigmoid(gr_ref[...].astype(F32)) * _dot(oret_ref[...], wr_ref[...])
    y = y + _sigmoid(gd_ref[...].astype(F32)) * _dot(oda_ref[...], wd_ref[...])
    y = y + _sigmoid(gm_ref[...].astype(F32)) * _dot(oml_ref[...], wm_ref[...])
    o_ref[...] = x_ref[...] + _dot(y.astype(BF16), wo_ref[...])


def _merge(x, o_ret, o_da, o_ml, zb, g_blk, wr, wd, wm, wo, tm):
    T, D = x.shape
    full = lambda a: pl.BlockSpec(a.shape, lambda i: (0, 0))
    return pl.pallas_call(
        _merge_kernel,
        grid=(T // tm,),
        in_specs=[pl.BlockSpec((tm, D), lambda i: (i, 0)),
                  pl.BlockSpec((tm, o_ret.shape[1]), lambda i: (i, 0)),
                  pl.BlockSpec((tm, o_da.shape[1]), lambda i: (i, 0)),
                  pl.BlockSpec((tm, o_ml.shape[1]), lambda i: (i, 0)),
                  pl.BlockSpec((tm, D), lambda i: (i, g_blk)),
                  pl.BlockSpec((tm, D), lambda i: (i, g_blk + 1)),
                  pl.BlockSpec((tm, D), lambda i: (i, g_blk + 2)),
                  full(wr), full(wd), full(wm), full(wo)],
        out_specs=pl.BlockSpec((tm, D), lambda i: (i, 0)),
        out_shape=jax.ShapeDtypeStruct((T, D), F32),
        compiler_params=_cparams("parallel"),
        name="merge_outproj",
    )(x, o_ret, o_da, o_ml, zb, zb, zb, wr, wd, wm, wo)


def _swiglu_block(h, wg_ref, wu_ref, wd_ref, a_ref):
    tf = wg_ref.shape[1]
    c0 = 0
    while c0 < tf:
        w = min(MXU_N, tf - c0)
        g = _dot(h, wg_ref[:, c0:c0 + w].astype(BF16))
        u = _dot(h, wu_ref[:, c0:c0 + w].astype(BF16))
        a_ref[:, c0:c0 + w] = (_silu(g) * u).astype(BF16)
        c0 += w
    return _dot(a_ref[...], wd_ref[...].astype(BF16))


def _ffn_kernel(x_ref, nw_ref, wg_ref, wu_ref, wd_ref, o_ref, h_ref, acc_ref, a_ref):
    j = pl.program_id(1)

    @pl.when(j == 0)
    def _():
        x = x_ref[...]
        ms = jnp.mean(x * x, axis=-1, keepdims=True)
        h_ref[...] = (x * lax.rsqrt(ms + EPS) * nw_ref[...]).astype(BF16)
        acc_ref[...] = x

    acc_ref[...] += _swiglu_block(h_ref[...], wg_ref, wu_ref, wd_ref, a_ref)

    @pl.when(j == pl.num_programs(1) - 1)
    def _():
        o_ref[...] = acc_ref[...]


def _ffn(x, nw, wg, wu, wd, tm, tf):
    T, D = x.shape
    F = wg.shape[1]
    return pl.pallas_call(
        _ffn_kernel,
        grid=(T // tm, F // tf),
        in_specs=[pl.BlockSpec((tm, D), lambda i, j: (i, 0)),
                  pl.BlockSpec((1, D), lambda i, j: (0, 0)),
                  pl.BlockSpec((D, tf), lambda i, j: (0, j)),
                  pl.BlockSpec((D, tf), lambda i, j: (0, j)),
                  pl.BlockSpec((tf, D), lambda i, j: (j, 0))],
        out_specs=pl.BlockSpec((tm, D), lambda i, j: (i, 0)),
        out_shape=jax.ShapeDtypeStruct((T, D), F32),
        scratch_shapes=[pltpu.VMEM((tm, D), BF16), pltpu.VMEM((tm, D), F32),
                        pltpu.VMEM((tm, tf), BF16)],
        compiler_params=_cparams("parallel", "arbitrary"),
        name="swiglu_ffn",
    )(x, nw, wg, wu, wd)


HI16 = 0xFFFF0000


def _pack_bf16_pairs(a, b):
    ab = pltpu.bitcast(a.astype(BF16).astype(F32), jnp.uint32)
    bb = pltpu.bitcast(b.astype(BF16).astype(F32), jnp.uint32)
    return (ab >> 16) | (bb & jnp.uint32(HI16))


def _unpack_bf16_pairs(p):
    lo = pltpu.bitcast(p << 16, F32)
    hi = pltpu.bitcast(p & jnp.uint32(HI16), F32)
    return lo, hi


def _router_kernel(x_ref, nw_ref, wr_ref, br_ref, ha_ref, hb_ref, sel_ref):
    x = x_ref[...]
    ms = jnp.mean(x * x, axis=-1, keepdims=True)
    h = x * lax.rsqrt(ms + EPS) * nw_ref[...]
    Q = h.shape[1] // 4
    ha_ref[...] = _pack_bf16_pairs(h[:, 0:Q], h[:, Q:2 * Q])
    hb_ref[...] = _pack_bf16_pairs(h[:, 2 * Q:3 * Q], h[:, 3 * Q:4 * Q])
    w = wr_ref[...]
    h_hi = h.astype(BF16)
    h_lo = (h - h_hi.astype(F32)).astype(BF16)
    w_hi = w.astype(BF16)
    w_lo = (w - w_hi.astype(F32)).astype(BF16)
    logits = _dot(h_hi, w_hi) + _dot(h_lo, w_hi) + _dot(h_hi, w_lo) + br_ref[...]
    lane = lax.broadcasted_iota(jnp.int32, logits.shape, 1).astype(F32)
    logits = jnp.where(lane < N_EXPERTS, logits, NEG)
    m1 = jnp.max(logits, axis=-1, keepdims=True)
    i1 = jnp.min(jnp.where(logits == m1, lane, float(LANES)), axis=-1, keepdims=True)
    rest = jnp.where(lane == i1, NEG, logits)
    m2 = jnp.max(rest, axis=-1, keepdims=True)
    i2 = jnp.min(jnp.where(rest == m2, lane, float(LANES)), axis=-1, keepdims=True)
    e2 = jnp.exp(m2 - m1)
    p1 = 1.0 / (1.0 + e2)
    p2 = e2 / (1.0 + e2)
    sel_ref[...] = jnp.where(lane == 0.0, i1, jnp.where(lane == 1.0, i2,
                             jnp.where(lane == 2.0, p1, jnp.where(lane == 3.0, p2, 0.0))))


def _router(x, nw, wr, br, tm):
    T, D = x.shape
    Q = D // 4
    return pl.pallas_call(
        _router_kernel,
        grid=(T // tm,),
        in_specs=[pl.BlockSpec((tm, D), lambda i: (i, 0)),
                  pl.BlockSpec((1, D), lambda i: (0, 0)),
                  pl.BlockSpec((D, LANES), lambda i: (0, 0)),
                  pl.BlockSpec((1, LANES), lambda i: (0, 0))],
        out_specs=[pl.BlockSpec((tm, Q), lambda i: (i, 0)),
                   pl.BlockSpec((tm, Q), lambda i: (i, 0)),
                   pl.BlockSpec((tm, LANES), lambda i: (i, 0))],
        out_shape=[jax.ShapeDtypeStruct((T, Q), jnp.uint32), jax.ShapeDtypeStruct((T, Q), jnp.uint32),
                   jax.ShapeDtypeStruct((T, LANES), F32)],
        compiler_params=_cparams("parallel"),
        name="moe_router",
    )(x, nw, wr, br)


def _moe_rank_kernel(sel_ref, pos_ref, meta_ref, cnt_ref, offs_ref, carry_ref, before_ref):
    ph = pl.program_id(0)
    i = pl.program_id(1)
    tm = sel_ref.shape[0]
    lane = lax.broadcasted_iota(jnp.int32, (tm, LANES), 1).astype(F32)
    lane1 = lax.broadcasted_iota(jnp.int32, (1, LANES), 1).astype(F32)
    sel = sel_ref[...]
    i1 = sel[:, 0:1]
    i2 = sel[:, 1:2]
    onehot = jnp.where((lane == i1) | (lane == i2), 1.0, 0.0)
    colsum = jnp.sum(onehot, axis=0, keepdims=True)

    @pl.when((ph == 0) & (i == 0))
    def _():
        cnt_ref[...] = jnp.zeros(cnt_ref.shape, F32)

    @pl.when(ph == 0)
    def _():
        cnt_ref[...] += colsum

    def padded_counts():
        return jnp.floor((cnt_ref[...] + (MOE_TILE - 1.0)) * (1.0 / MOE_TILE)) * MOE_TILE

    @pl.when((ph == 1) & (i == 0))
    def _():
        k = lax.broadcasted_iota(jnp.int32, (LANES, LANES), 0)
        e = lax.broadcasted_iota(jnp.int32, (LANES, LANES), 1)
        upper = (k < e).astype(F32)
        offs_ref[...] = jnp.dot(padded_counts(), upper, preferred_element_type=F32,
                                precision=lax.Precision.HIGHEST)
        carry_ref[...] = jnp.zeros(carry_ref.shape, F32)
        r = lax.broadcasted_iota(jnp.int32, (tm, tm), 0)
        c = lax.broadcasted_iota(jnp.int32, (tm, tm), 1)
        before_ref[...] = (c < r).astype(BF16)

    @pl.when(ph == 1)
    def _():
        rank = _dot(before_ref[...], onehot.astype(BF16)) + carry_ref[0:1, :]
        row = rank + offs_ref[0:1, :]
        pos1 = jnp.sum(jnp.where(lane == i1, row, 0.0), axis=-1, keepdims=True)
        pos2 = jnp.sum(jnp.where(lane == i2, row, 0.0), axis=-1, keepdims=True)
        both = jnp.where(lane == 0.0, pos1, jnp.where(lane == 1.0, pos2, 0.0))
        pos_ref[...] = both.T[0:8, :].astype(jnp.int32)
        carry_ref[...] += colsum

    @pl.when((ph == 1) & (i == pl.num_programs(1) - 1))
    def _():
        nrow = meta_ref.shape[0]
        padded = padded_counts()[0:1, :]
        offs = offs_ref[0:1, :]
        ends = offs + padded
        start = lax.broadcasted_iota(jnp.int32, (nrow, 1), 0).astype(F32) * MOE_TILE
        is_e = lane1 < N_EXPERTS
        te = jnp.sum(jnp.where(is_e & (ends <= start), 1.0, 0.0), axis=-1, keepdims=True)
        te = jnp.minimum(te, N_EXPERTS - 1.0)
        valid_end = jnp.sum(jnp.where(lane1 == te, offs + cnt_ref[0:1, :], 0.0), axis=-1, keepdims=True)
        nv = jnp.clip(valid_end - start, 0.0, MOE_TILE)
        total = jnp.sum(jnp.where(is_e, padded, 0.0), axis=-1, keepdims=True)
        active = jnp.where(start < total, 1.0, 0.0)
        meta_ref[...] = jnp.where(lane1 == 0.0, te, jnp.where(lane1 == 1.0, nv,
                                  jnp.where(lane1 == 2.0, active, 0.0)))


def _moe_rank(sel, tm, meta_rows):
    T = sel.shape[0]
    nt = T // tm
    return pl.pallas_call(
        _moe_rank_kernel,
        grid=(2, nt),
        in_specs=[pl.BlockSpec((tm, LANES), lambda ph, i: (i, 0))],
        out_specs=[pl.BlockSpec((None, 8, tm), lambda ph, i: (i * ph, 0, 0)),
                   pl.BlockSpec((meta_rows, LANES), lambda ph, i: (0, 0))],
        out_shape=[jax.ShapeDtypeStruct((nt, 8, tm), jnp.int32),
                   jax.ShapeDtypeStruct((meta_rows, LANES), F32)],
        scratch_shapes=[pltpu.VMEM((8, LANES), F32), pltpu.VMEM((8, LANES), F32),
                        pltpu.VMEM((8, LANES), F32), pltpu.VMEM((tm, tm), BF16)],
        compiler_params=_cparams("arbitrary", "arbitrary"),
        name="moe_rank",
    )(sel)


def _sc_mesh():
    return plsc.VectorSubcoreMesh(core_axis_name="core", subcore_axis_name="subcore")


def _sc_scatter_rows(x, idx, n_rows):
    n_idx = idx.shape[1]
    nt = x.shape[0] // SC_WINDOW
    width = x.shape[1]

    @functools.partial(pl.kernel, out_type=jax.ShapeDtypeStruct((n_rows, width), x.dtype),
                       mesh=_sc_mesh())
    def scatter_kernel(x_hbm, i_hbm, o_hbm):
        def body(x_vmem, i_vmem):
            pltpu.sync_copy(x_vmem, o_hbm.at[i_vmem.at[0]])

        pltpu.emit_pipeline(
            body, grid=(n_idx // SC_WINDOW,),
            in_specs=[pl.BlockSpec((SC_WINDOW, width), lambda i: (i % nt, 0)),
                      pl.BlockSpec((1, SC_WINDOW), lambda i: (0, i))],
            out_specs=[],
            core_axis_name=("core", "subcore"), dimension_semantics=(pltpu.PARALLEL,),
        )(x_hbm, i_hbm)

    return scatter_kernel(x, idx)


def _sc_gather_rows(x, idx):
    n_idx = idx.shape[1]
    width = x.shape[1]

    @functools.partial(pl.kernel, out_type=jax.ShapeDtypeStruct((n_idx, width), x.dtype),
                       mesh=_sc_mesh())
    def gather_kernel(x_hbm, i_hbm, o_hbm):
        def body(i_vmem, o_vmem):
            pltpu.sync_copy(x_hbm.at[i_vmem.at[0]], o_vmem)

        pltpu.emit_pipeline(
            body, grid=(n_idx // SC_WINDOW,),
            in_specs=[pl.BlockSpec((1, SC_WINDOW), lambda i: (0, i))],
            out_specs=[pl.BlockSpec((SC_WINDOW, width), lambda i: (i, 0))],
            core_axis_name=("core", "subcore"), dimension_semantics=(pltpu.PARALLEL,),
        )(i_hbm, o_hbm)

    return gather_kernel(x, idx)


def _moe_group_kernel(te_ref, nv_ref, na_ref, xa_ref, xb_ref, wg_ref, wu_ref, wd_ref,
                      ya_ref, yb_ref, h_ref, a_ref):
    j = pl.program_id(0)

    @pl.when(j < na_ref[0])
    def _():
        tm = h_ref.shape[0]
        Q = xa_ref.shape[1]
        valid = lax.broadcasted_iota(jnp.int32, (tm, 1), 0) < nv_ref[j]
        for src, c0 in ((xa_ref, 0), (xb_ref, 2 * Q)):
            lo, hi = _unpack_bf16_pairs(src[...])
            h_ref[:, c0:c0 + Q] = jnp.where(valid, lo, 0.0).astype(BF16)
            h_ref[:, c0 + Q:c0 + 2 * Q] = jnp.where(valid, hi, 0.0).astype(BF16)
        y = _swiglu_block(h_ref[...], wg_ref, wu_ref, wd_ref, a_ref)
        ya_ref[...] = _pack_bf16_pairs(y[:, 0:Q], y[:, Q:2 * Q])
        yb_ref[...] = _pack_bf16_pairs(y[:, 2 * Q:3 * Q], y[:, 3 * Q:4 * Q])


def _moe_group(te, nv, na, xa, xb, wg, wu, wd):
    R, Q = xa.shape
    E, D, F = wg.shape
    tile = lambda j, te, nv, na: (jnp.minimum(j, na[0] - 1), 0)
    expert = lambda j, te, nv, na: (te[j], 0, 0)
    once = pl.Buffered(1)
    grid_spec = pltpu.PrefetchScalarGridSpec(
        num_scalar_prefetch=3,
        grid=(R // MOE_TILE,),
        in_specs=[pl.BlockSpec((MOE_TILE, Q), tile),
                  pl.BlockSpec((MOE_TILE, Q), tile),
                  pl.BlockSpec((None, D, F), expert, pipeline_mode=once),
                  pl.BlockSpec((None, D, F), expert, pipeline_mode=once),
                  pl.BlockSpec((None, F, D), expert, pipeline_mode=once)],
        out_specs=[pl.BlockSpec((MOE_TILE, Q), tile), pl.BlockSpec((MOE_TILE, Q), tile)],
        scratch_shapes=[pltpu.VMEM((MOE_TILE, D), BF16), pltpu.VMEM((MOE_TILE, F), BF16)],
    )
    return pl.pallas_call(
        _moe_group_kernel,
        grid_spec=grid_spec,
        out_shape=[jax.ShapeDtypeStruct((R, Q), jnp.uint32), jax.ShapeDtypeStruct((R, Q), jnp.uint32)],
        compiler_params=_cparams("arbitrary"),
        name="moe_experts",
    )(te, nv, na, xa, xb, wg, wu, wd)


def _moe_combine_kernel(x_ref, sel_ref, g1_ref, g2_ref, *rest):
    o_ref = rest[-1]
    sel = sel_ref[...]
    p1 = sel[:, 2:3]
    p2 = sel[:, 3:4]
    Q = g1_ref.shape[1]
    lo1, hi1 = _unpack_bf16_pairs(g1_ref[...])
    lo2, hi2 = _unpack_bf16_pairs(g2_ref[...])
    o_ref[:, 0:Q] = x_ref[:, 0:Q] + (p1 * lo1 + p2 * lo2)
    o_ref[:, Q:2 * Q] = x_ref[:, Q:2 * Q] + (p1 * hi1 + p2 * hi2)


def _moe_combine_half(x, sel, g, half, partial_out, tm):
    T, D = x.shape
    Q = g.shape[1]
    nt = T // tm
    in_specs = [pl.BlockSpec((tm, 2 * Q), lambda i: (i, half)),
                pl.BlockSpec((tm, LANES), lambda i: (i, 0)),
                pl.BlockSpec((tm, Q), lambda i: (i, 0)),
                pl.BlockSpec((tm, Q), lambda i: (i + nt, 0))]
    args = [x, sel, g, g]
    aliases = {}
    if partial_out is not None:
        in_specs.append(pl.BlockSpec(memory_space=pl.ANY))
        args.append(partial_out)
        aliases = {4: 0}
    return pl.pallas_call(
        _moe_combine_kernel,
        grid=(nt,),
        in_specs=in_specs,
        out_specs=pl.BlockSpec((tm, 2 * Q), lambda i: (i, half)),
        out_shape=jax.ShapeDtypeStruct((T, D), F32),
        input_output_aliases=aliases,
        compiler_params=_cparams("parallel"),
        name="moe_combine",
    )(*args)


def _cast_kernel(x_ref, o_ref):
    o_ref[...] = x_ref[...].astype(o_ref.dtype)


def _cast_bf16(w, rows):
    cols = w.shape[-1]
    w2 = w.reshape(-1, cols)
    out = pl.pallas_call(
        _cast_kernel,
        grid=(w2.shape[0] // rows,),
        in_specs=[pl.BlockSpec((rows, cols), lambda i: (i, 0))],
        out_specs=pl.BlockSpec((rows, cols), lambda i: (i, 0)),
        out_shape=jax.ShapeDtypeStruct(w2.shape, BF16),
        compiler_params=_cparams("parallel"),
        name="cast_bf16",
    )(w2)
    return out.reshape(w.shape)


def _moe(x, nw, w_router, b_router, wg, wu, wd):
    T, D = x.shape
    wr = jnp.pad(w_router, ((0, 0), (0, LANES - N_EXPERTS)))
    br = jnp.pad(b_router, (0, LANES - N_EXPERTS)).reshape(1, LANES)
    ha, hb, sel = _router(x, nw, wr, br, 1024)
    n_rows = TOP_K * T + N_EXPERTS * MOE_TILE
    n_tiles = n_rows // MOE_TILE
    pos, meta = _moe_rank(sel, 1024, 256)
    idx = jnp.concatenate([pos[:, 0, :].reshape(1, T), pos[:, 1, :].reshape(1, T)], axis=1)
    te = meta[:n_tiles, 0].astype(jnp.int32)
    nv = meta[:n_tiles, 1].astype(jnp.int32)
    na = jnp.sum(meta[:n_tiles, 2]).astype(jnp.int32).reshape(1)
    te = jnp.where(jnp.arange(n_tiles) < na[0], te, te[na[0] - 1])
    xa = _sc_scatter_rows(ha, idx, n_rows)
    xb = _sc_scatter_rows(hb, idx, n_rows)
    ya, yb = _moe_group(te, nv, na, xa, xb, wg, wu, wd)
    ga = _sc_gather_rows(ya, idx)
    gb = _sc_gather_rows(yb, idx)
    out = _moe_combine_half(x, sel, ga, 0, None, 1024)
    return _moe_combine_half(x, sel, gb, 1, out, 1024)


def _split_w_in(w_in):
    sizes = (512, 512, 512, 512, 768, 768, 768, 256, 256, 512, 512, 4, 4, 1024, 1024, 1024)
    offs = [0]
    for s in sizes:
        offs.append(offs[-1] + s)
    part = lambda i: w_in[..., offs[i]:offs[i + 1]]
    rq, rk, rv, rg, dq, dk, dv, mq, mk, mv, mo, mi, mf, g_ret, g_da, g_ml = (part(i) for i in range(16))
    cols = []
    for p in range(2):
        for g in range(3):
            for t in (dq, dk, dv):
                cols.append(t[..., g * 256 + p * 128:g * 256 + (p + 1) * 128])
    wb = jnp.concatenate([rq, rk, rv, rg, g_ret, g_da, g_ml, mq, mk, mv, mo] + cols,
                         axis=-1).astype(BF16)
    pad = jnp.zeros(w_in.shape[:-1] + (LANES - 8,), w_in.dtype)
    wf = jnp.concatenate([mi, mf, pad], axis=-1).astype(BF16)
    return wb, wf


def _rope_tables(S):
    half = LANES // 2
    inv = jnp.power(ROPE_BASE, -jnp.arange(half, dtype=F32) / half)
    ang = jnp.arange(S, dtype=F32)[:, None] * inv[None, :]
    cos = jnp.cos(ang)
    sin = jnp.sin(ang)
    return jnp.concatenate([cos, cos], axis=1), jnp.concatenate([-sin, sin], axis=1)


def kernel(x, norm1_w, w_in, ret_norm_w, da_q_norm_w, da_k_norm_w, ml_conv_w, ml_i_bias, ml_f_bias,
           ml_norm_w, w_br_ret, w_br_da, w_br_ml, w_out, norm2_w, ffn_w_gate, ffn_w_up, ffn_w_down,
           moe_w_router, moe_b_router, moe_w_gate, moe_w_up, moe_w_down):
    B, S, D = x.shape
    T = B * S
    depth = w_in.shape[0]
    cos, sin = _rope_tables(S)
    wb_all, wf_all = _split_w_in(w_in)
    xt = x.reshape(T, D)
    for layer in range(depth):
        wb, wf = wb_all[layer], wf_all[layer]
        nw1 = norm1_w[layer].reshape(1, D)
        zb, zf = _inproj(xt, nw1, wb, wf, ml_conv_w[layer], 4, S, 1280)
        o_ret = _retention(zb, cos, sin, ret_norm_w[layer].reshape(1, -1), B, S)
        wq = jnp.tile(da_q_norm_w[layer], 2).reshape(1, LANES)
        wk = jnp.tile(da_k_norm_w[layer], 2).reshape(1, LANES)
        o_da = _dilated_attention(zb, 52, wq, wk, B, S)
        gate_bias = jnp.concatenate([ml_i_bias[layer], ml_f_bias[layer],
                                     jnp.zeros((LANES - 2 * ML_HEADS,), F32)]).reshape(1, LANES)
        o_ml = _mlstm(zb, zf, gate_bias, ml_norm_w[layer].reshape(1, -1), B, S,
                      qk_blk=10, gate_blk=0)
        xt = _merge(xt, o_ret, o_da, o_ml, zb, 2,
                    w_br_ret[layer].astype(BF16), w_br_da[layer].astype(BF16),
                    w_br_ml[layer].astype(BF16), w_out[layer].astype(BF16), 512)
        nw2 = norm2_w[layer].reshape(1, D)
        j = layer // 2
        if layer % 2 == 0:
            xt = _ffn(xt, nw2, ffn_w_gate[j].astype(BF16), ffn_w_up[j].astype(BF16),
                      ffn_w_down[j].astype(BF16), 512, 2816)
        else:
            xt = _moe(xt, nw2, moe_w_router[j], moe_b_router[j], _cast_bf16(moe_w_gate[j], 512),
                      _cast_bf16(moe_w_up[j], 512), _cast_bf16(moe_w_down[j], 2048))
    return xt.reshape(B, S, D)
```

```python
import functools
import math

import jax
import jax.numpy as jnp
from jax import lax
from jax.experimental import pallas as pl
from jax.experimental.pallas import tpu as pltpu
from jax.experimental.pallas import tpu_sc as plsc

F32 = jnp.float32
BF16 = jnp.bfloat16

EPS = 1e-6
D_MODEL = 1024
CHUNK = 128
LANES = 128
MXU_N = 256
ROPE_BASE = 10000.0
RET_HEADS = 4
RET_SEQS = 2
DA_GROUPS = ((1, 16), (4, 4), (16, 1))
DA_DH = 64
DA_UNROLL = 4
DA_MAX_STRIDE = 4
ML_HEADS = 4
ML_DK = 64
ML_SEQS = 2
N_EXPERTS = 8
TOP_K = 2
MOE_TILE = 512
SC_WINDOW = 128
INPROJ_TN = 1280
ROW_TILE = 512
WIDE_ROW_TILE = 1024
OFF_GATES = 2048
OFF_MLQK = OFF_GATES + 3 * D_MODEL
OFF_DA = OFF_MLQK + 3 * ML_HEADS * LANES
VMEM_LIMIT = 56 * 1024 * 1024

NEG = -1e30


def _cparams(*sem):
    return pltpu.CompilerParams(dimension_semantics=sem, vmem_limit_bytes=VMEM_LIMIT)


def _dot(a, b):
    return jnp.dot(a, b, preferred_element_type=F32)


def _dot_nt(a, b):
    return lax.dot_general(a, b, (((1,), (1,)), ((), ())), preferred_element_type=F32)


def _dot_tn(a, b):
    return lax.dot_general(a, b, (((0,), (0,)), ((), ())), preferred_element_type=F32)


def _sigmoid(x):
    return 1.0 / (1.0 + jnp.exp(-x))


def _silu(x):
    return x * _sigmoid(x)


def _inproj_kernel(conv_blk, x_ref, nw_ref, wb_ref, wf_ref, cw_ref, zb_ref, zf_ref, h_ref, cv_ref):
    j = pl.program_id(1)
    tm = x_ref.shape[0]

    @pl.when(j == 0)
    def _():
        rc = 512
        for r in range(tm // rc):
            rows = pl.ds(r * rc, rc)
            x = x_ref[rows, :]
            ms = jnp.mean(x * x, axis=-1, keepdims=True)
            h = (x * lax.rsqrt(ms + EPS) * nw_ref[...]).astype(BF16)
            h_ref[rows, :] = h
            zb_ref[rows, :] = _dot(h, wb_ref[...]).astype(zb_ref.dtype)
            zf_ref[rows, :] = _dot(h, wf_ref[...])

    @pl.when((j > 0) & (j != conv_blk))
    def _():
        zb_ref[...] = _dot(h_ref[...], wb_ref[...]).astype(zb_ref.dtype)

    @pl.when(j == conv_blk)
    def _():
        h = h_ref[...]
        cw = cw_ref.shape[1]
        pad = cv_ref.shape[0] - tm
        cv_ref[0:pad, :] = jnp.zeros((pad, cw), F32)
        cv_ref[pad:pad + tm, :] = _dot(h, wb_ref[:, 0:cw])
        taps = cw_ref.shape[0]
        rb = 256

        def conv(piece, cs):
            cols = slice(cs * LANES, (cs + 1) * LANES)
            r0 = pad + piece * rb - (taps - 1)
            acc = cw_ref[0:1, cols] * cv_ref[pl.ds(r0, rb), cols]
            for i in range(1, taps):
                acc = acc + cw_ref[i:i + 1, cols] * cv_ref[pl.ds(r0 + i, rb), cols]
            scale = ML_DK ** -0.5 if cs >= cw // LANES // 2 else 1.0
            zb_ref[piece * rb:(piece + 1) * rb, cols] = (_silu(acc) * scale).astype(zb_ref.dtype)

        todo = [(piece, cs) for piece in range(tm // rb) for cs in range(cw // LANES)]
        chunks = list(range(cw, zb_ref.shape[1], MXU_N))
        per = -(-len(todo) // len(chunks))
        for n, c0 in enumerate(chunks):
            c1 = min(c0 + MXU_N, zb_ref.shape[1])
            zb_ref[:, c0:c1] = _dot(h, wb_ref[:, c0:c1]).astype(zb_ref.dtype)
            for piece, cs in todo[n * per:(n + 1) * per]:
                conv(piece, cs)


def _inproj(x, nw, wb, wf, conv_w, conv_blk, tm, tnb):
    T, D = x.shape
    return pl.pallas_call(
        functools.partial(_inproj_kernel, conv_blk),
        grid=(T // tm, wb.shape[1] // tnb),
        in_specs=[pl.BlockSpec((tm, D), lambda i, j: (i, 0)),
                  pl.BlockSpec((1, D), lambda i, j: (0, 0)),
                  pl.BlockSpec((D, tnb), lambda i, j: (0, j)),
                  pl.BlockSpec(wf.shape, lambda i, j: (0, 0)),
                  pl.BlockSpec(conv_w.shape, lambda i, j: (0, 0))],
        out_specs=[pl.BlockSpec((tm, tnb), lambda i, j: (i, j)),
                   pl.BlockSpec((tm, wf.shape[1]), lambda i, j: (i, 0))],
        out_shape=[jax.ShapeDtypeStruct((T, wb.shape[1]), BF16),
                   jax.ShapeDtypeStruct((T, wf.shape[1]), F32)],
        scratch_shapes=[pltpu.VMEM((tm, D), BF16), pltpu.VMEM((tm + 8, conv_w.shape[1]), F32)],
        compiler_params=_cparams("parallel", "arbitrary"),
        name="inproj",
    )(x, nw, wb, wf, conv_w)


def _retention_kernel(z_ref, cos_ref, sin_ref, nw_ref, o_ref, dec_ref, st_ref):
    S = z_ref.shape[0] // RET_SEQS
    n_chunks = S // CHUNK
    H = RET_HEADS
    HW = H * LANES
    row = lax.broadcasted_iota(jnp.int32, (CHUNK, CHUNK), 0).astype(F32)
    col = lax.broadcasted_iota(jnp.int32, (CHUNK, CHUNK), 1).astype(F32)
    lgs = [math.log1p(-(2.0 ** (-5.0 - h))) for h in range(H)]
    for h, lg in enumerate(lgs):
        rel = row - col
        dec_ref[h] = jnp.where(rel >= 0, jnp.exp(lg * jnp.maximum(rel, 0.0)), 0.0)
        dec_ref[H + h] = jnp.exp(lg * (row + 1.0))
        dec_ref[2 * H + h] = jnp.exp(lg * (CHUNK - 1.0 - row)) * (LANES ** -0.5)
    st_ref[...] = jnp.zeros(st_ref.shape, F32)
    units = [(sq, h) for sq in range(RET_SEQS) for h in range(H)]

    def body(n, carry):
        r0 = pl.multiple_of(n * CHUNK, CHUNK)
        cos = cos_ref[pl.ds(r0, CHUNK), :]
        sin = sin_ref[pl.ds(r0, CHUNK), :]
        rows_of = [pl.ds(pl.multiple_of(sq * S + n * CHUNK, CHUNK), CHUNK) for sq in range(RET_SEQS)]
        vs, ss, iqs, kvs = [], [], [], []
        for u, (sq, h) in enumerate(units):
            rows = rows_of[sq]
            q = z_ref[rows, h * LANES:(h + 1) * LANES].astype(F32)
            k = z_ref[rows, HW + h * LANES:HW + (h + 1) * LANES].astype(F32)
            v = z_ref[rows, 2 * HW + h * LANES:2 * HW + (h + 1) * LANES]
            q = q * cos + pltpu.roll(q, LANES // 2, 1) * sin
            k = k * cos + pltpu.roll(k, LANES // 2, 1) * sin
            qb = q.astype(BF16)
            kb = (k * (LANES ** -0.5)).astype(BF16)
            kd = (k * dec_ref[2 * H + h]).astype(BF16)
            ss.append(_dot_nt(qb, kb))
            iqs.append(_dot(qb, st_ref[u].astype(BF16)))
            kvs.append(_dot_tn(kd, v))
            vs.append(v)
        ps = [(ss[u] * dec_ref[h]).astype(BF16) for u, (_, h) in enumerate(units)]
        os_ = [_dot(ps[u], vs[u]) + iqs[u] * dec_ref[H + h] for u, (_, h) in enumerate(units)]
        for u, (sq, h) in enumerate(units):
            rows = rows_of[sq]
            st_ref[u] = st_ref[u] * math.exp(lgs[h] * CHUNK) + kvs[u]
            o = os_[u]
            ms = jnp.mean(o * o, axis=-1, keepdims=True)
            o = o * lax.rsqrt(ms + EPS) * nw_ref[:, h * LANES:(h + 1) * LANES]
            g = z_ref[rows, 3 * HW + h * LANES:3 * HW + (h + 1) * LANES].astype(F32)
            o_ref[rows, h * LANES:(h + 1) * LANES] = (o * _silu(g)).astype(o_ref.dtype)
        return carry

    lax.fori_loop(0, n_chunks, body, 0)


def _retention(zb, cos, sin, nw, B, S):
    T = B * S
    W = RET_HEADS * LANES
    R = RET_SEQS * S
    return pl.pallas_call(
        _retention_kernel,
        grid=(B // RET_SEQS,),
        in_specs=[pl.BlockSpec((R, 4 * W), lambda b: (b, 0)),
                  pl.BlockSpec((S, LANES), lambda b: (0, 0)),
                  pl.BlockSpec((S, LANES), lambda b: (0, 0)),
                  pl.BlockSpec((1, W), lambda b: (0, 0))],
        out_specs=pl.BlockSpec((R, W), lambda b: (b, 0)),
        out_shape=jax.ShapeDtypeStruct((T, W), BF16),
        scratch_shapes=[pltpu.VMEM((3 * RET_HEADS, CHUNK, LANES), F32),
                        pltpu.VMEM((RET_SEQS * RET_HEADS, LANES, LANES), F32)],
        compiler_params=_cparams("parallel"),
        name="retention",
    )(zb, cos, sin, nw)


def _da_kernel(*refs):
    z_refs = refs[:9]
    wq_ref, wk_ref, o_ref, qn_ref, kn_ref, v_ref, np_ref, lp_ref, mp_ref = refs[9:18]
    nn_refs, ln_refs, mn_refs = refs[18:21], refs[21:24], refs[24:27]
    stage_refs = refs[27:30]
    S = o_ref.shape[0]
    lane = lax.broadcasted_iota(jnp.int32, (1, LANES), 1)
    lo = lane < DA_DH
    row = lax.broadcasted_iota(jnp.int32, (CHUNK, CHUNK), 0)
    col = lax.broadcasted_iota(jnp.int32, (CHUNK, CHUNK), 1)
    mask_cur = col <= row
    row2 = lax.broadcasted_iota(jnp.int32, (CHUNK, 2 * CHUNK), 0)
    col2 = lax.broadcasted_iota(jnp.int32, (CHUNK, 2 * CHUNK), 1)
    mask_band = (col2 >= row2) & (col2 <= row2 + CHUNK)

    seg = (lax.broadcasted_iota(jnp.int32, (LANES, LANES), 0) // DA_DH
           == lax.broadcasted_iota(jnp.int32, (LANES, LANES), 1) // DA_DH).astype(BF16) * (1.0 / DA_DH)

    def head_norm(x, w):
        x2 = x * x
        hi = x2.astype(BF16)
        lo_part = (x2 - hi.astype(F32)).astype(BF16)
        ms = _dot(hi, seg) + _dot(lo_part, seg)
        return x * lax.rsqrt(ms + EPS) * w

    RB = 256
    for g, (dil, nb) in enumerate(DA_GROUPS):
        L = S // dil
        zq_ref, zk_ref, zv_ref = z_refs[3 * g:3 * g + 3]
        if dil > 1:
            for piece in range(S // RB):
                rows = pl.ds(piece * RB, RB)
                for src_ref, dst_ref in zip((zq_ref, zk_ref, zv_ref), stage_refs):
                    dst_ref[rows, :] = src_ref[rows, :].astype(F32)
            zq_ref, zk_ref, zv_ref = stage_refs
        first = min(dil, DA_MAX_STRIDE)
        if dil > first:
            mid_refs = (nn_refs[g], ln_refs[g], mn_refs[g])
            sub = S // first
            for b in range(first):
                for piece in range(sub // RB):
                    src = pl.ds(b + first * piece * RB, RB, stride=first)
                    dst = pl.ds(b * sub + piece * RB, RB)
                    for src_ref, dst_ref in zip((zq_ref, zk_ref, zv_ref), mid_refs):
                        dst_ref[dst, :] = src_ref[src, :]
            zq_ref, zk_ref, zv_ref = mid_refs
        rest = dil // first

        def source_rows(rho, piece, n_rows):
            if dil == 1:
                return pl.ds(piece * n_rows, n_rows)
            b, a = rho % first, rho // first
            if rest == 1:
                return pl.ds(b + first * piece * n_rows, n_rows, stride=first)
            return pl.ds(b * (S // first) + a + rest * piece * n_rows, n_rows, stride=rest)

        for rho in range(dil):
            for piece in range(max(L // RB, 1)):
                n_rows = min(RB, L)
                src = source_rows(rho, piece, n_rows)
                dst = pl.ds(rho * L + piece * n_rows, n_rows)
                q = zq_ref[src, :].astype(F32)
                k = zk_ref[src, :].astype(F32)
                v = zv_ref[src, :]
                qn_ref[dst, :] = (head_norm(q, wq_ref[...]) * (DA_DH ** -0.5)).astype(BF16)
                kn_ref[dst, :] = head_norm(k, wk_ref[...]).astype(BF16)
                v_ref[dst, :] = v.astype(BF16)

        def batch(r0, chained, first_has_prev):
            chains = []
            for u in range(DA_UNROLL):
                rows = pl.ds(r0 + u * CHUNK, CHUNK)
                if chained and (u > 0 or first_has_prev):
                    keys, mask = pl.ds(r0 + (u - 1) * CHUNK, 2 * CHUNK), mask_band
                else:
                    keys, mask = rows, mask_cur
                k = kn_ref[keys, :]
                v = v_ref[keys, :]
                vaug = jnp.concatenate([v, jnp.ones_like(v)], axis=1)
                q = qn_ref[rows, :]
                zero = jnp.zeros_like(q)
                for qh in (jnp.where(lo, q, zero), jnp.where(lo, zero, q)):
                    chains.append((_dot_nt(qh, k), mask, vaug))
            probs = []
            for s, mask, _ in chains:
                s = jnp.where(mask, s, NEG)
                m = jnp.max(s, axis=-1, keepdims=True)
                probs.append((jnp.exp(s - m).astype(BF16), m))
            accs = [_dot(p, vaug) for (p, _), (_, _, vaug) in zip(probs, chains)]
            nums, dens, maxs = [], [], []
            for u in range(DA_UNROLL):
                a0, a1 = accs[2 * u], accs[2 * u + 1]
                nums.append(jnp.where(lo, a0[:, :LANES], a1[:, :LANES]))
                dens.append(jnp.where(lo, a0[:, LANES:], a1[:, LANES:]))
                maxs.append(jnp.where(lo, probs[2 * u][1], probs[2 * u + 1][1]))
            rows = pl.ds(r0, DA_UNROLL * CHUNK)
            np_ref[rows, :] = jnp.concatenate(nums, axis=0)
            lp_ref[rows, :] = jnp.concatenate(dens, axis=0)
            mp_ref[rows, :] = jnp.concatenate(maxs, axis=0)

        span = DA_UNROLL * CHUNK
        if nb == 1:
            def singles(i, c):
                batch(pl.multiple_of(i * span, span), False, False)
                return c
            lax.fori_loop(0, S // span, singles, 0)
        else:
            def segment(sgi, c, nb=nb):
                base = pl.multiple_of(sgi * (nb * CHUNK), span)
                batch(base, True, False)
                if nb > DA_UNROLL:
                    def inner(n, c2):
                        batch(pl.multiple_of(base + n * span, span), True, True)
                        return c2
                    lax.fori_loop(1, nb // DA_UNROLL, inner, 0)
                return c
            lax.fori_loop(0, dil, segment, 0)

        outs = ((np_ref, nn_refs[g]), (lp_ref, ln_refs[g]), (mp_ref, mn_refs[g]))
        for rho in range(dil):
            for piece in range(max(L // RB, 1)):
                n_rows = min(RB, L)
                dst = source_rows(rho, piece, n_rows)
                src = pl.ds(rho * L + piece * n_rows, n_rows)
                for k_out, (perm_ref, nat_ref) in enumerate(outs):
                    (stage_refs[k_out] if rest > 1 else nat_ref)[dst, :] = perm_ref[src, :]
        if rest > 1:
            sub = S // first
            for b in range(first):
                for piece in range(sub // RB):
                    dst = pl.ds(b + first * piece * RB, RB, stride=first)
                    src = pl.ds(b * sub + piece * RB, RB)
                    for k_out, (_, nat_ref) in enumerate(outs):
                        nat_ref[dst, :] = stage_refs[k_out][src, :]

    for piece in range(S // RB):
        rows = pl.ds(piece * RB, RB)
        m = jnp.maximum(jnp.maximum(mn_refs[0][rows, :], mn_refs[1][rows, :]), mn_refs[2][rows, :])
        num = jnp.zeros((RB, LANES), F32)
        den = jnp.zeros((RB, LANES), F32)
        for g in range(3):
            e = jnp.exp(mn_refs[g][rows, :] - m)
            num = num + e * nn_refs[g][rows, :]
            den = den + e * ln_refs[g][rows, :]
        o_ref[rows, :] = (num / den).astype(o_ref.dtype)


def _dilated_attention(zb, slab0, wq, wk, B, S):
    T = B * S
    slab = lambda k: pl.BlockSpec((S, LANES), lambda b, p: (b, slab0 + 9 * p + k))
    return pl.pallas_call(
        _da_kernel,
        grid=(B, 2),
        in_specs=[slab(k) for k in range(9)] + [
                  pl.BlockSpec((1, LANES), lambda b, p: (0, 0)),
                  pl.BlockSpec((1, LANES), lambda b, p: (0, 0))],
        out_specs=pl.BlockSpec((S, LANES), lambda b, p: (b, p)),
        out_shape=jax.ShapeDtypeStruct((T, 2 * LANES), BF16),
        scratch_shapes=[pltpu.VMEM((S, LANES), BF16), pltpu.VMEM((S, LANES), BF16),
                        pltpu.VMEM((S, LANES), BF16),
                        pltpu.VMEM((S, LANES), F32), pltpu.VMEM((S, LANES), F32),
                        pltpu.VMEM((S, LANES), F32)] + [pltpu.VMEM((S, LANES), F32)] * 12,
        compiler_params=_cparams("parallel", "arbitrary"),
        name="dilated_attention",
    )(*([zb] * 9), wq, wk)


def _mlstm_kernel(qk_ref, v_ref, og_ref, gate_ref, gb_ref, nw_ref, o_ref, st_ref, e_ref, et_ref):
    S = qk_ref.shape[0] // ML_SEQS
    n_chunks = S // CHUNK
    W = qk_ref.shape[1]
    lane = lax.broadcasted_iota(jnp.int32, (1, LANES), 1)
    lo = lane < ML_DK
    row = lax.broadcasted_iota(jnp.int32, (CHUNK, CHUNK), 0)
    col = lax.broadcasted_iota(jnp.int32, (CHUNK, CHUNK), 1)
    causal = col <= row
    tril = causal.astype(F32)
    srow = lax.broadcasted_iota(jnp.int32, (LANES, 1), 0) < ML_DK
    st_ref[...] = jnp.zeros(st_ref.shape, F32)

    is_f = (lane >= ML_HEADS) & (lane < 2 * ML_HEADS)
    for n in range(ML_SEQS * n_chunks):
        rows = pl.ds(n * CHUNK, CHUNK)
        gp = gate_ref[rows, :] + gb_ref[...]
        logf = jnp.minimum(gp, 0.0) - jnp.log(1.0 + jnp.exp(-jnp.abs(gp)))
        cum = jnp.dot(tril, jnp.where(is_f, logf, 0.0), preferred_element_type=F32,
                      precision=lax.Precision.HIGHEST)
        e = jnp.where(lane < ML_HEADS, gp, cum)
        e_ref[rows, :] = e
        et_ref[n] = e.T[0:2 * ML_HEADS, :]

    units = [(sq, h) for sq in range(ML_SEQS) for h in range(ML_HEADS)]

    def chunk(n, mms):
        rows_of = [pl.ds(pl.multiple_of(sq * S + n * CHUNK, CHUNK), CHUNK) for sq in range(ML_SEQS)]
        es = [e_ref[rows_of[sq], :] for sq in range(ML_SEQS)]
        ets = [et_ref[sq * n_chunks + n] for sq in range(ML_SEQS)]
        khs, vaugs, s_raws, iqs = [], [], [], []
        for sq in range(ML_SEQS):
            rows = rows_of[sq]
            for pair in range(2):
                q2 = qk_ref[rows, pair * LANES:(pair + 1) * LANES]
                k2 = qk_ref[rows, W // 2 + pair * LANES:W // 2 + (pair + 1) * LANES]
                zero = jnp.zeros_like(q2)
                stb = st_ref[2 * sq + pair].astype(BF16)
                for hs in range(2):
                    h = 2 * pair + hs
                    qh = jnp.where(lo, q2, zero) if hs == 0 else jnp.where(lo, zero, q2)
                    kh = jnp.where(lo, k2, zero) if hs == 0 else jnp.where(lo, zero, k2)
                    vh = v_ref[rows, h * LANES:(h + 1) * LANES]
                    s_raws.append(_dot_nt(qh, k2))
                    iqs.append(_dot(qh, stb))
                    khs.append(kh)
                    vaugs.append(jnp.concatenate([vh, jnp.ones_like(vh)], axis=-1))
        ps, wis, ms_, kws, wcs, new_mms = [], [], [], [], [], []
        for u, (sq, h) in enumerate(units):
            e, et = es[sq], ets[sq]
            mm = mms[u]
            i_b = jnp.broadcast_to(e[:, h:h + 1], (CHUNK, LANES))
            a_b = jnp.broadcast_to(e[:, ML_HEADS + h:ML_HEADS + h + 1], (CHUNK, LANES))
            c_row = et[h:h + 1, :] - et[ML_HEADS + h:ML_HEADS + h + 1, :]
            dmat = jnp.where(causal, a_b + c_row, NEG)
            inter = a_b + mm
            m = jnp.maximum(inter, jnp.max(dmat, axis=-1, keepdims=True))
            ps.append((s_raws[u] * jnp.exp(dmat - m)).astype(BF16))
            wis.append(jnp.exp(inter - m))
            ms_.append(m)
            m_new = m[CHUNK - 1:CHUNK, :]
            a_last = a_b[CHUNK - 1:CHUNK, :]
            wk = jnp.exp(a_last - a_b + i_b - m_new)
            kws.append((khs[u].astype(F32) * wk).astype(BF16))
            wc = jnp.exp(a_last + mm - m_new)
            wcs.append(jnp.concatenate([wc, wc], axis=1))
            new_mms.append(m_new)
        accs = [_dot(ps[u], vaugs[u]) + jnp.concatenate([wis[u], wis[u]], axis=1) * iqs[u]
                for u in range(len(units))]
        upds = [_dot_tn(kws[u], vaugs[u]) for u in range(len(units))]
        for sp in range(2 * ML_SEQS):
            u0, u1 = 2 * sp, 2 * sp + 1
            st_ref[sp] = jnp.where(srow, wcs[u0], wcs[u1]) * st_ref[sp] + (upds[u0] + upds[u1])
        for u, (sq, h) in enumerate(units):
            num = accs[u][:, :LANES]
            den = accs[u][:, LANES:]
            hv = num / jnp.maximum(jnp.abs(den), jnp.exp(-ms_[u]))
            var = jnp.mean(hv * hv, axis=-1, keepdims=True)
            hv = hv * lax.rsqrt(var + EPS) * nw_ref[:, h * LANES:(h + 1) * LANES]
            og = og_ref[rows_of[sq], h * LANES:(h + 1) * LANES].astype(F32)
            o_ref[rows_of[sq], h * LANES:(h + 1) * LANES] = (hv * _sigmoid(og)).astype(o_ref.dtype)
        return tuple(new_mms)

    lax.fori_loop(0, n_chunks, chunk, tuple(jnp.zeros((1, LANES), F32) for _ in units))


def _mlstm(zb, zf, gate_bias, nw, B, S, qk_blk, gate_blk):
    T = B * S
    W = ML_HEADS * LANES
    R = ML_SEQS * S
    return pl.pallas_call(
        _mlstm_kernel,
        grid=(B // ML_SEQS,),
        in_specs=[pl.BlockSpec((R, W), lambda b: (b, qk_blk)),
                  pl.BlockSpec((R, W), lambda b: (b, qk_blk + 1)),
                  pl.BlockSpec((R, W), lambda b: (b, qk_blk + 2)),
                  pl.BlockSpec((R, LANES), lambda b: (b, gate_blk)),
                  pl.BlockSpec((1, LANES), lambda b: (0, 0)),
                  pl.BlockSpec((1, W), lambda b: (0, 0))],
        out_specs=pl.BlockSpec((R, W), lambda b: (b, 0)),
        out_shape=jax.ShapeDtypeStruct((T, W), BF16),
        scratch_shapes=[pltpu.VMEM((2 * ML_SEQS, LANES, 2 * LANES), F32),
                        pltpu.VMEM((R, LANES), F32),
                        pltpu.VMEM((R // CHUNK, 2 * ML_HEADS, LANES), F32)],
        compiler_params=_cparams("parallel"),
        name="mlstm",
    )(zb, zb, zb, zf, gate_bias, nw)


def _merge_kernel(x_ref, oret_ref, oda_ref, oml_ref, gr_ref, gd_ref, gm_ref,
                  wr_ref, wd_ref, wm_ref, wo_ref, o_ref):
    y = _sigmoid(gr_ref[...].astype(F32)) * _dot(oret_ref[...], wr_ref[...])
    y = y + _sigmoid(gd_ref[...].astype(F32)) * _dot(oda_ref[...], wd_ref[...])
    y = y + _sigmoid(gm_ref[...].astype(F32)) * _dot(oml_ref[...], wm_ref[...])
    o_ref[...] = x_ref[...] + _dot(y.astype(BF16), wo_ref[...])


def _merge(x, o_ret, o_da, o_ml, zb, g_blk, wr, wd, wm, wo, tm):
    T, D = x.shape
    full = lambda a: pl.BlockSpec(a.shape, lambda i: (0, 0))
    return pl.pallas_call(
        _merge_kernel,
        grid=(T // tm,),
        in_specs=[pl.BlockSpec((tm, D), lambda i: (i, 0)),
                  pl.BlockSpec((tm, o_ret.shape[1]), lambda i: (i, 0)),
                  pl.BlockSpec((tm, o_da.shape[1]), lambda i: (i, 0)),
                  pl.BlockSpec((tm, o_ml.shape[1]), lambda i: (i, 0)),
                  pl.BlockSpec((tm, D), lambda i: (i, g_blk)),
                  pl.BlockSpec((tm, D), lambda i: (i, g_blk + 1)),
                  pl.BlockSpec((tm, D), lambda i: (i, g_blk + 2)),
                  full(wr), full(wd), full(wm), full(wo)],
        out_specs=pl.BlockSpec((tm, D), lambda i: (i, 0)),
        out_shape=jax.ShapeDtypeStruct((T, D), F32),
        compiler_params=_cparams("parallel"),
        name="merge_outproj",
    )(x, o_ret, o_da, o_ml, zb, zb, zb, wr, wd, wm, wo)


def _swiglu_block(h, wg_ref, wu_ref, wd_ref, a_ref):
    tf = wg_ref.shape[1]
    c0 = 0
    while c0 < tf:
        w = min(MXU_N, tf - c0)
        g = _dot(h, wg_ref[:, c0:c0 + w].astype(BF16))
        u = _dot(h, wu_ref[:, c0:c0 + w].astype(BF16))
        a_ref[:, c0:c0 + w] = (_silu(g) * u).astype(BF16)
        c0 += w
    return _dot(a_ref[...], wd_ref[...].astype(BF16))


def _ffn_kernel(x_ref, nw_ref, wg_ref, wu_ref, wd_ref, o_ref, h_ref, acc_ref, a_ref):
    j = pl.program_id(1)

    @pl.when(j == 0)
    def _():
        x = x_ref[...]
        ms = jnp.mean(x * x, axis=-1, keepdims=True)
        h_ref[...] = (x * lax.rsqrt(ms + EPS) * nw_ref[...]).astype(BF16)
        acc_ref[...] = x

    acc_ref[...] += _swiglu_block(h_ref[...], wg_ref, wu_ref, wd_ref, a_ref)

    @pl.when(j == pl.num_programs(1) - 1)
    def _():
        o_ref[...] = acc_ref[...]


def _ffn(x, nw, wg, wu, wd, tm, tf):
    T, D = x.shape
    F = wg.shape[1]
    return pl.pallas_call(
        _ffn_kernel,
        grid=(T // tm, F // tf),
        in_specs=[pl.BlockSpec((tm, D), lambda i, j: (i, 0)),
                  pl.BlockSpec((1, D), lambda i, j: (0, 0)),
                  pl.BlockSpec((D, tf), lambda i, j: (0, j)),
                  pl.BlockSpec((D, tf), lambda i, j: (0, j)),
                  pl.BlockSpec((tf, D), lambda i, j: (j, 0))],
        out_specs=pl.BlockSpec((tm, D), lambda i, j: (i, 0)),
        out_shape=jax.ShapeDtypeStruct((T, D), F32),
        scratch_shapes=[pltpu.VMEM((tm, D), BF16), pltpu.VMEM((tm, D), F32),
                        pltpu.VMEM((tm, tf), BF16)],
        compiler_params=_cparams("parallel", "arbitrary"),
        name="swiglu_ffn",
    )(x, nw, wg, wu, wd)


HI16 = 0xFFFF0000


def _pack_bf16_pairs(a, b):
    ab = pltpu.bitcast(a.astype(BF16).astype(F32), jnp.uint32)
    bb = pltpu.bitcast(b.astype(BF16).astype(F32), jnp.uint32)
    return (ab >> 16) | (bb & jnp.uint32(HI16))


def _unpack_bf16_pairs(p):
    lo = pltpu.bitcast(p << 16, F32)
    hi = pltpu.bitcast(p & jnp.uint32(HI16), F32)
    return lo, hi


def _router_kernel(x_ref, nw_ref, wr_ref, br_ref, ha_ref, hb_ref, sel_ref):
    x = x_ref[...]
    ms = jnp.mean(x * x, axis=-1, keepdims=True)
    h = x * lax.rsqrt(ms + EPS) * nw_ref[...]
    Q = h.shape[1] // 4
    ha_ref[...] = _pack_bf16_pairs(h[:, 0:Q], h[:, Q:2 * Q])
    hb_ref[...] = _pack_bf16_pairs(h[:, 2 * Q:3 * Q], h[:, 3 * Q:4 * Q])
    w = wr_ref[...]
    h_hi = h.astype(BF16)
    h_lo = (h - h_hi.astype(F32)).astype(BF16)
    w_hi = w.astype(BF16)
    w_lo = (w - w_hi.astype(F32)).astype(BF16)
    logits = _dot(h_hi, w_hi) + _dot(h_lo, w_hi) + _dot(h_hi, w_lo) + br_ref[...]
    lane = lax.broadcasted_iota(jnp.int32, logits.shape, 1).astype(F32)
    logits = jnp.where(lane < N_EXPERTS, logits, NEG)
    m1 = jnp.max(logits, axis=-1, keepdims=True)
    i1 = jnp.min(jnp.where(logits == m1, lane, float(LANES)), axis=-1, keepdims=True)
    rest = jnp.where(lane == i1, NEG, logits)
    m2 = jnp.max(rest, axis=-1, keepdims=True)
    i2 = jnp.min(jnp.where(rest == m2, lane, float(LANES)), axis=-1, keepdims=True)
    e2 = jnp.exp(m2 - m1)
    p1 = 1.0 / (1.0 + e2)
    p2 = e2 / (1.0 + e2)
    sel_ref[...] = jnp.where(lane == 0.0, i1, jnp.where(lane == 1.0, i2,
                             jnp.where(lane == 2.0, p1, jnp.where(lane == 3.0, p2, 0.0))))


def _router(x, nw, wr, br, tm):
    T, D = x.shape
    Q = D // 4
    return pl.pallas_call(
        _router_kernel,
        grid=(T // tm,),
        in_specs=[pl.BlockSpec((tm, D), lambda i: (i, 0)),
                  pl.BlockSpec((1, D), lambda i: (0, 0)),
                  pl.BlockSpec((D, LANES), lambda i: (0, 0)),
                  pl.BlockSpec((1, LANES), lambda i: (0, 0))],
        out_specs=[pl.BlockSpec((tm, Q), lambda i: (i, 0)),
                   pl.BlockSpec((tm, Q), lambda i: (i, 0)),
                   pl.BlockSpec((tm, LANES), lambda i: (i, 0))],
        out_shape=[jax.ShapeDtypeStruct((T, Q), jnp.uint32), jax.ShapeDtypeStruct((T, Q), jnp.uint32),
                   jax.ShapeDtypeStruct((T, LANES), F32)],
        compiler_params=_cparams("parallel"),
        name="moe_router",
    )(x, nw, wr, br)


def _moe_rank_kernel(sel_ref, pos_ref, meta_ref, cnt_ref, offs_ref, carry_ref, before_ref):
    ph = pl.program_id(0)
    i = pl.program_id(1)
    tm = sel_ref.shape[0]
    lane = lax.broadcasted_iota(jnp.int32, (tm, LANES), 1).astype(F32)
    lane1 = lax.broadcasted_iota(jnp.int32, (1, LANES), 1).astype(F32)
    sel = sel_ref[...]
    i1 = sel[:, 0:1]
    i2 = sel[:, 1:2]
    onehot = jnp.where((lane == i1) | (lane == i2), 1.0, 0.0)
    colsum = jnp.sum(onehot, axis=0, keepdims=True)

    @pl.when((ph == 0) & (i == 0))
    def _():
        cnt_ref[...] = jnp.zeros(cnt_ref.shape, F32)

    @pl.when(ph == 0)
    def _():
        cnt_ref[...] += colsum

    def padded_counts():
        return jnp.floor((cnt_ref[...] + (MOE_TILE - 1.0)) * (1.0 / MOE_TILE)) * MOE_TILE

    @pl.when((ph == 1) & (i == 0))
    def _():
        k = lax.broadcasted_iota(jnp.int32, (LANES, LANES), 0)
        e = lax.broadcasted_iota(jnp.int32, (LANES, LANES), 1)
        upper = (k < e).astype(F32)
        offs_ref[...] = jnp.dot(padded_counts(), upper, preferred_element_type=F32,
                                precision=lax.Precision.HIGHEST)
        carry_ref[...] = jnp.zeros(carry_ref.shape, F32)
        r = lax.broadcasted_iota(jnp.int32, (tm, tm), 0)
        c = lax.broadcasted_iota(jnp.int32, (tm, tm), 1)
        before_ref[...] = (c < r).astype(BF16)

    @pl.when(ph == 1)
    def _():
        rank = _dot(before_ref[...], onehot.astype(BF16)) + carry_ref[0:1, :]
        row = rank + offs_ref[0:1, :]
        pos1 = jnp.sum(jnp.where(lane == i1, row, 0.0), axis=-1, keepdims=True)
        pos2 = jnp.sum(jnp.where(lane == i2, row, 0.0), axis=-1, keepdims=True)
        both = jnp.where(lane == 0.0, pos1, jnp.where(lane == 1.0, pos2, 0.0))
        pos_ref[...] = both.T[0:8, :].astype(jnp.int32)
        carry_ref[...] += colsum

    @pl.when((ph == 1) & (i == pl.num_programs(1) - 1))
    def _():
        nrow = meta_ref.shape[0]
        padded = padded_counts()[0:1, :]
        offs = offs_ref[0:1, :]
        ends = offs + padded
        start = lax.broadcasted_iota(jnp.int32, (nrow, 1), 0).astype(F32) * MOE_TILE
        is_e = lane1 < N_EXPERTS
        te = jnp.sum(jnp.where(is_e & (ends <= start), 1.0, 0.0), axis=-1, keepdims=True)
        te = jnp.minimum(te, N_EXPERTS - 1.0)
        valid_end = jnp.sum(jnp.where(lane1 == te, offs + cnt_ref[0:1, :], 0.0), axis=-1, keepdims=True)
        nv = jnp.clip(valid_end - start, 0.0, MOE_TILE)
        total = jnp.sum(jnp.where(is_e, padded, 0.0), axis=-1, keepdims=True)
        active = jnp.where(start < total, 1.0, 0.0)
        meta_ref[...] = jnp.where(lane1 == 0.0, te, jnp.where(lane1 == 1.0, nv,
                                  jnp.where(lane1 == 2.0, active, 0.0)))


def _moe_rank(sel, tm, meta_rows):
    T = sel.shape[0]
    nt = T // tm
    return pl.pallas_call(
        _moe_rank_kernel,
        grid=(2, nt),
        in_specs=[pl.BlockSpec((tm, LANES), lambda ph, i: (i, 0))],
        out_specs=[pl.BlockSpec((None, 8, tm), lambda ph, i: (i * ph, 0, 0)),
                   pl.BlockSpec((meta_rows, LANES), lambda ph, i: (0, 0))],
        out_shape=[jax.ShapeDtypeStruct((nt, 8, tm), jnp.int32),
                   jax.ShapeDtypeStruct((meta_rows, LANES), F32)],
        scratch_shapes=[pltpu.VMEM((8, LANES), F32), pltpu.VMEM((8, LANES), F32),
                        pltpu.VMEM((8, LANES), F32), pltpu.VMEM((tm, tm), BF16)],
        compiler_params=_cparams("arbitrary", "arbitrary"),
        name="moe_rank",
    )(sel)


def _sc_mesh():
    return plsc.VectorSubcoreMesh(core_axis_name="core", subcore_axis_name="subcore")


def _sc_scatter_rows(x, idx, n_rows):
    n_idx = idx.shape[1]
    nt = x.shape[0] // SC_WINDOW
    width = x.shape[1]

    @functools.partial(pl.kernel, out_type=jax.ShapeDtypeStruct((n_rows, width), x.dtype),
                       mesh=_sc_mesh())
    def scatter_kernel(x_hbm, i_hbm, o_hbm):
        def body(x_vmem, i_vmem):
            pltpu.sync_copy(x_vmem, o_hbm.at[i_vmem.at[0]])

        pltpu.emit_pipeline(
            body, grid=(n_idx // SC_WINDOW,),
            in_specs=[pl.BlockSpec((SC_WINDOW, width), lambda i: (i % nt, 0)),
                      pl.BlockSpec((1, SC_WINDOW), lambda i: (0, i))],
            out_specs=[],
            core_axis_name=("core", "subcore"), dimension_semantics=(pltpu.PARALLEL,),
        )(x_hbm, i_hbm)

    return scatter_kernel(x, idx)


def _sc_gather_rows(x, idx):
    n_idx = idx.shape[1]
    width = x.shape[1]

    @functools.partial(pl.kernel, out_type=jax.ShapeDtypeStruct((n_idx, width), x.dtype),
                       mesh=_sc_mesh())
    def gather_kernel(x_hbm, i_hbm, o_hbm):
        def body(i_vmem, o_vmem):
            pltpu.sync_copy(x_hbm.at[i_vmem.at[0]], o_vmem)

        pltpu.emit_pipeline(
            body, grid=(n_idx // SC_WINDOW,),
            in_specs=[pl.BlockSpec((1, SC_WINDOW), lambda i: (0, i))],
            out_specs=[pl.BlockSpec((SC_WINDOW, width), lambda i: (i, 0))],
            core_axis_name=("core", "subcore"), dimension_semantics=(pltpu.PARALLEL,),
        )(i_hbm, o_hbm)

    return gather_kernel(x, idx)


def _moe_group_kernel(te_ref, nv_ref, na_ref, xa_ref, xb_ref, wg_ref, wu_ref, wd_ref,
                      ya_ref, yb_ref, h_ref, a_ref):
    j = pl.program_id(0)

    @pl.when(j < na_ref[0])
    def _():
        tm = h_ref.shape[0]
        Q = xa_ref.shape[1]
        valid = lax.broadcasted_iota(jnp.int32, (tm, 1), 0) < nv_ref[j]
        for src, c0 in ((xa_ref, 0), (xb_ref, 2 * Q)):
            lo, hi = _unpack_bf16_pairs(src[...])
            h_ref[:, c0:c0 + Q] = jnp.where(valid, lo, 0.0).astype(BF16)
            h_ref[:, c0 + Q:c0 + 2 * Q] = jnp.where(valid, hi, 0.0).astype(BF16)
        y = _swiglu_block(h_ref[...], wg_ref, wu_ref, wd_ref, a_ref)
        ya_ref[...] = _pack_bf16_pairs(y[:, 0:Q], y[:, Q:2 * Q])
        yb_ref[...] = _pack_bf16_pairs(y[:, 2 * Q:3 * Q], y[:, 3 * Q:4 * Q])


def _moe_group(te, nv, na, xa, xb, wg, wu, wd):
    R, Q = xa.shape
    E, D, F = wg.shape
    tile = lambda j, te, nv, na: (jnp.minimum(j, na[0] - 1), 0)
    expert = lambda j, te, nv, na: (te[j], 0, 0)
    once = pl.Buffered(1)
    grid_spec = pltpu.PrefetchScalarGridSpec(
        num_scalar_prefetch=3,
        grid=(R // MOE_TILE,),
        in_specs=[pl.BlockSpec((MOE_TILE, Q), tile),
                  pl.BlockSpec((MOE_TILE, Q), tile),
                  pl.BlockSpec((None, D, F), expert),
                  pl.BlockSpec((None, D, F), expert),
                  pl.BlockSpec((None, F, D), expert, pipeline_mode=once)],
        out_specs=[pl.BlockSpec((MOE_TILE, Q), tile), pl.BlockSpec((MOE_TILE, Q), tile)],
        scratch_shapes=[pltpu.VMEM((MOE_TILE, D), BF16), pltpu.VMEM((MOE_TILE, F), BF16)],
    )
    return pl.pallas_call(
        _moe_group_kernel,
        grid_spec=grid_spec,
        out_shape=[jax.ShapeDtypeStruct((R, Q), jnp.uint32), jax.ShapeDtypeStruct((R, Q), jnp.uint32)],
        compiler_params=_cparams("arbitrary"),
        name="moe_experts",
    )(te, nv, na, xa, xb, wg, wu, wd)


def _moe_combine_kernel(x_ref, sel_ref, g1_ref, g2_ref, *rest):
    o_ref = rest[-1]
    sel = sel_ref[...]
    p1 = sel[:, 2:3]
    p2 = sel[:, 3:4]
    Q = g1_ref.shape[1]
    lo1, hi1 = _unpack_bf16_pairs(g1_ref[...])
    lo2, hi2 = _unpack_bf16_pairs(g2_ref[...])
    o_ref[:, 0:Q] = x_ref[:, 0:Q] + (p1 * lo1 + p2 * lo2)
    o_ref[:, Q:2 * Q] = x_ref[:, Q:2 * Q] + (p1 * hi1 + p2 * hi2)


def _moe_combine_half(x, sel, g, half, partial_out, tm):
    T, D = x.shape
    Q = g.shape[1]
    nt = T // tm
    in_specs = [pl.BlockSpec((tm, 2 * Q), lambda i: (i, half)),
                pl.BlockSpec((tm, LANES), lambda i: (i, 0)),
                pl.BlockSpec((tm, Q), lambda i: (i, 0)),
                pl.BlockSpec((tm, Q), lambda i: (i + nt, 0))]
    args = [x, sel, g, g]
    aliases = {}
    if partial_out is not None:
        in_specs.append(pl.BlockSpec(memory_space=pl.ANY))
        args.append(partial_out)
        aliases = {4: 0}
    return pl.pallas_call(
        _moe_combine_kernel,
        grid=(nt,),
        in_specs=in_specs,
        out_specs=pl.BlockSpec((tm, 2 * Q), lambda i: (i, half)),
        out_shape=jax.ShapeDtypeStruct((T, D), F32),
        input_output_aliases=aliases,
        compiler_params=_cparams("parallel"),
        name="moe_combine",
    )(*args)


def _cast_kernel(x_ref, o_ref):
    o_ref[...] = x_ref[...].astype(o_ref.dtype)


def _cast_bf16(w, rows):
    cols = w.shape[-1]
    w2 = w.reshape(-1, cols)
    out = pl.pallas_call(
        _cast_kernel,
        grid=(w2.shape[0] // rows,),
        in_specs=[pl.BlockSpec((rows, cols), lambda i: (i, 0))],
        out_specs=pl.BlockSpec((rows, cols), lambda i: (i, 0)),
        out_shape=jax.ShapeDtypeStruct(w2.shape, BF16),
        compiler_params=_cparams("parallel"),
        name="cast_bf16",
    )(w2)
    return out.reshape(w.shape)


def _moe(x, nw, w_router, b_router, wg, wu, wd):
    T, D = x.shape
    wr = jnp.pad(w_router, ((0, 0), (0, LANES - N_EXPERTS)))
    br = jnp.pad(b_router, (0, LANES - N_EXPERTS)).reshape(1, LANES)
    ha, hb, sel = _router(x, nw, wr, br, WIDE_ROW_TILE)
    n_rows = TOP_K * T + N_EXPERTS * MOE_TILE
    n_tiles = n_rows // MOE_TILE
    pos, meta = _moe_rank(sel, WIDE_ROW_TILE, 256)
    idx = jnp.concatenate([pos[:, 0, :].reshape(1, T), pos[:, 1, :].reshape(1, T)], axis=1)
    te = meta[:n_tiles, 0].astype(jnp.int32)
    nv = meta[:n_tiles, 1].astype(jnp.int32)
    na = jnp.sum(meta[:n_tiles, 2]).astype(jnp.int32).reshape(1)
    te = jnp.where(jnp.arange(n_tiles) < na[0], te, te[na[0] - 1])
    xa = _sc_scatter_rows(ha, idx, n_rows)
    xb = _sc_scatter_rows(hb, idx, n_rows)
    ya, yb = _moe_group(te, nv, na, xa, xb, wg, wu, wd)
    ga = _sc_gather_rows(ya, idx)
    gb = _sc_gather_rows(yb, idx)
    out = _moe_combine_half(x, sel, ga, 0, None, WIDE_ROW_TILE)
    return _moe_combine_half(x, sel, gb, 1, out, WIDE_ROW_TILE)


def _split_w_in(w_in):
    sizes = (512, 512, 512, 512, 768, 768, 768, 256, 256, 512, 512, 4, 4, 1024, 1024, 1024)
    offs = [0]
    for s in sizes:
        offs.append(offs[-1] + s)
    part = lambda i: w_in[..., offs[i]:offs[i + 1]]
    rq, rk, rv, rg, dq, dk, dv, mq, mk, mv, mo, mi, mf, g_ret, g_da, g_ml = (part(i) for i in range(16))
    cols = []
    for p in range(2):
        for g in range(3):
            for t in (dq, dk, dv):
                cols.append(t[..., g * 256 + p * 128:g * 256 + (p + 1) * 128])
    wb = jnp.concatenate([rq, rk, rv, rg, g_ret, g_da, g_ml, mq, mk, mv, mo] + cols,
                         axis=-1).astype(BF16)
    pad = jnp.zeros(w_in.shape[:-1] + (LANES - 8,), w_in.dtype)
    wf = jnp.concatenate([mi, mf, pad], axis=-1).astype(BF16)
    return wb, wf


def _rope_tables(S):
    half = LANES // 2
    inv = jnp.power(ROPE_BASE, -jnp.arange(half, dtype=F32) / half)
    ang = jnp.arange(S, dtype=F32)[:, None] * inv[None, :]
    cos = jnp.cos(ang)
    sin = jnp.sin(ang)
    return jnp.concatenate([cos, cos], axis=1), jnp.concatenate([-sin, sin], axis=1)


def kernel(x, norm1_w, w_in, ret_norm_w, da_q_norm_w, da_k_norm_w, ml_conv_w, ml_i_bias, ml_f_bias,
           ml_norm_w, w_br_ret, w_br_da, w_br_ml, w_out, norm2_w, ffn_w_gate, ffn_w_up, ffn_w_down,
           moe_w_router, moe_b_router, moe_w_gate, moe_w_up, moe_w_down):
    B, S, D = x.shape
    T = B * S
    depth = w_in.shape[0]
    cos, sin = _rope_tables(S)
    wb_all, wf_all = _split_w_in(w_in)
    xt = x.reshape(T, D)
    for layer in range(depth):
        wb, wf = wb_all[layer], wf_all[layer]
        nw1 = norm1_w[layer].reshape(1, D)
        assert OFF_MLQK % INPROJ_TN == 0
        zb, zf = _inproj(xt, nw1, wb, wf, ml_conv_w[layer], OFF_MLQK // INPROJ_TN, S, INPROJ_TN)
        o_ret = _retention(zb, cos, sin, ret_norm_w[layer].reshape(1, -1), B, S)
        wq = jnp.tile(da_q_norm_w[layer], 2).reshape(1, LANES)
        wk = jnp.tile(da_k_norm_w[layer], 2).reshape(1, LANES)
        o_da = _dilated_attention(zb, OFF_DA // LANES, wq, wk, B, S)
        gate_bias = jnp.concatenate([ml_i_bias[layer], ml_f_bias[layer],
                                     jnp.zeros((LANES - 2 * ML_HEADS,), F32)]).reshape(1, LANES)
        o_ml = _mlstm(zb, zf, gate_bias, ml_norm_w[layer].reshape(1, -1), B, S,
                      qk_blk=OFF_MLQK // (ML_HEADS * LANES), gate_blk=0)
        xt = _merge(xt, o_ret, o_da, o_ml, zb, OFF_GATES // D_MODEL,
                    w_br_ret[layer].astype(BF16), w_br_da[layer].astype(BF16),
                    w_br_ml[layer].astype(BF16), w_out[layer].astype(BF16), ROW_TILE)
        nw2 = norm2_w[layer].reshape(1, D)
        j = layer // 2
        if layer % 2 == 0:
            xt = _ffn(xt, nw2, ffn_w_gate[j].astype(BF16), ffn_w_up[j].astype(BF16),
                      ffn_w_down[j].astype(BF16), ROW_TILE, ffn_w_gate.shape[-1])
        else:
            xt = _moe(xt, nw2, moe_w_router[j], moe_b_router[j], _cast_bf16(moe_w_gate[j], 512),
                      _cast_bf16(moe_w_up[j], 512), _cast_bf16(moe_w_down[j], 2048))
    return xt.reshape(B, S, D)
```

```python
import functools
import math

import jax
import jax.numpy as jnp
from jax import lax
from jax.experimental import pallas as pl
from jax.experimental.pallas import tpu as pltpu
from jax.experimental.pallas import tpu_sc as plsc

F32 = jnp.float32
BF16 = jnp.bfloat16

EPS = 1e-6
D_MODEL = 1024
CHUNK = 128
LANES = 128
MXU_N = 256
ROPE_BASE = 10000.0
RET_HEADS = 4
RET_SEQS = 2
DA_GROUPS = ((1, 16), (4, 4), (16, 1))
DA_DH = 64
DA_UNROLL = 4
DA_MAX_STRIDE = 4
ML_HEADS = 4
ML_DK = 64
ML_SEQS = 2
N_EXPERTS = 8
TOP_K = 2
MOE_TILE = 512
SC_WINDOW = 128
INPROJ_TN = 1280
ROW_TILE = 512
WIDE_ROW_TILE = 1024
OFF_GATES = 2048
OFF_MLQK = OFF_GATES + 3 * D_MODEL
OFF_DA = OFF_MLQK + 3 * ML_HEADS * LANES
VMEM_LIMIT = 56 * 1024 * 1024

NEG = -1e30


def _cparams(*sem):
    return pltpu.CompilerParams(dimension_semantics=sem, vmem_limit_bytes=VMEM_LIMIT)


def _dot(a, b):
    return jnp.dot(a, b, preferred_element_type=F32)


def _dot_nt(a, b):
    return lax.dot_general(a, b, (((1,), (1,)), ((), ())), preferred_element_type=F32)


def _dot_tn(a, b):
    return lax.dot_general(a, b, (((0,), (0,)), ((), ())), preferred_element_type=F32)


def _sigmoid(x):
    return 1.0 / (1.0 + jnp.exp(-x))


def _silu(x):
    return x * _sigmoid(x)


def _inproj_kernel(conv_blk, x_ref, nw_ref, wb_ref, wf_ref, cw_ref, zb_ref, zf_ref, h_ref, cv_ref):
    j = pl.program_id(1)
    tm = x_ref.shape[0]

    @pl.when(j == 0)
    def _():
        rc = 512
        for r in range(tm // rc):
            rows = pl.ds(r * rc, rc)
            x = x_ref[rows, :]
            ms = jnp.mean(x * x, axis=-1, keepdims=True)
            h = (x * lax.rsqrt(ms + EPS) * nw_ref[...]).astype(BF16)
            h_ref[rows, :] = h
            zb_ref[rows, :] = _dot(h, wb_ref[...]).astype(zb_ref.dtype)
            zf_ref[rows, :] = _dot(h, wf_ref[...])

    @pl.when((j > 0) & (j != conv_blk))
    def _():
        zb_ref[...] = _dot(h_ref[...], wb_ref[...]).astype(zb_ref.dtype)

    @pl.when(j == conv_blk)
    def _():
        h = h_ref[...]
        cw = cw_ref.shape[1]
        pad = cv_ref.shape[0] - tm
        cv_ref[0:pad, :] = jnp.zeros((pad, cw), F32)
        cv_ref[pad:pad + tm, :] = _dot(h, wb_ref[:, 0:cw])
        taps = cw_ref.shape[0]
        rb = 256

        def conv(piece, cs):
            cols = slice(cs * LANES, (cs + 1) * LANES)
            r0 = pad + piece * rb - (taps - 1)
            acc = cw_ref[0:1, cols] * cv_ref[pl.ds(r0, rb), cols]
            for i in range(1, taps):
                acc = acc + cw_ref[i:i + 1, cols] * cv_ref[pl.ds(r0 + i, rb), cols]
            scale = ML_DK ** -0.5 if cs >= cw // LANES // 2 else 1.0
            zb_ref[piece * rb:(piece + 1) * rb, cols] = (_silu(acc) * scale).astype(zb_ref.dtype)

        todo = [(piece, cs) for piece in range(tm // rb) for cs in range(cw // LANES)]
        chunks = list(range(cw, zb_ref.shape[1], MXU_N))
        per = -(-len(todo) // len(chunks))
        for n, c0 in enumerate(chunks):
            c1 = min(c0 + MXU_N, zb_ref.shape[1])
            zb_ref[:, c0:c1] = _dot(h, wb_ref[:, c0:c1]).astype(zb_ref.dtype)
            for piece, cs in todo[n * per:(n + 1) * per]:
                conv(piece, cs)


def _inproj(x, nw, wb, wf, conv_w, conv_blk, tm, tnb):
    T, D = x.shape
    return pl.pallas_call(
        functools.partial(_inproj_kernel, conv_blk),
        grid=(T // tm, wb.shape[1] // tnb),
        in_specs=[pl.BlockSpec((tm, D), lambda i, j: (i, 0)),
                  pl.BlockSpec((1, D), lambda i, j: (0, 0)),
                  pl.BlockSpec((D, tnb), lambda i, j: (0, j)),
                  pl.BlockSpec(wf.shape, lambda i, j: (0, 0)),
                  pl.BlockSpec(conv_w.shape, lambda i, j: (0, 0))],
        out_specs=[pl.BlockSpec((tm, tnb), lambda i, j: (i, j)),
                   pl.BlockSpec((tm, wf.shape[1]), lambda i, j: (i, 0))],
        out_shape=[jax.ShapeDtypeStruct((T, wb.shape[1]), BF16),
                   jax.ShapeDtypeStruct((T, wf.shape[1]), F32)],
        scratch_shapes=[pltpu.VMEM((tm, D), BF16), pltpu.VMEM((tm + 8, conv_w.shape[1]), F32)],
        compiler_params=_cparams("parallel", "arbitrary"),
        name="inproj",
    )(x, nw, wb, wf, conv_w)


def _retention_kernel(z_ref, cos_ref, sin_ref, nw_ref, o_ref, dec_ref, st_ref):
    S = z_ref.shape[0] // RET_SEQS
    n_chunks = S // CHUNK
    H = RET_HEADS
    HW = H * LANES
    row = lax.broadcasted_iota(jnp.int32, (CHUNK, CHUNK), 0).astype(F32)
    col = lax.broadcasted_iota(jnp.int32, (CHUNK, CHUNK), 1).astype(F32)
    lgs = [math.log1p(-(2.0 ** (-5.0 - h))) for h in range(H)]
    for h, lg in enumerate(lgs):
        rel = row - col
        dec_ref[h] = jnp.where(rel >= 0, jnp.exp(lg * jnp.maximum(rel, 0.0)), 0.0)
        dec_ref[H + h] = jnp.exp(lg * (row + 1.0))
        dec_ref[2 * H + h] = jnp.exp(lg * (CHUNK - 1.0 - row)) * (LANES ** -0.5)
    st_ref[...] = jnp.zeros(st_ref.shape, F32)
    units = [(sq, h) for sq in range(RET_SEQS) for h in range(H)]

    def body(n, carry):
        r0 = pl.multiple_of(n * CHUNK, CHUNK)
        cos = cos_ref[pl.ds(r0, CHUNK), :]
        sin = sin_ref[pl.ds(r0, CHUNK), :]
        rows_of = [pl.ds(pl.multiple_of(sq * S + n * CHUNK, CHUNK), CHUNK) for sq in range(RET_SEQS)]
        vs, ss, iqs, kvs = [], [], [], []
        for u, (sq, h) in enumerate(units):
            rows = rows_of[sq]
            q = z_ref[rows, h * LANES:(h + 1) * LANES].astype(F32)
            k = z_ref[rows, HW + h * LANES:HW + (h + 1) * LANES].astype(F32)
            v = z_ref[rows, 2 * HW + h * LANES:2 * HW + (h + 1) * LANES]
            q = q * cos + pltpu.roll(q, LANES // 2, 1) * sin
            k = k * cos + pltpu.roll(k, LANES // 2, 1) * sin
            qb = q.astype(BF16)
            kb = (k * (LANES ** -0.5)).astype(BF16)
            kd = (k * dec_ref[2 * H + h]).astype(BF16)
            ss.append(_dot_nt(qb, kb))
            iqs.append(_dot(qb, st_ref[u].astype(BF16)))
            kvs.append(_dot_tn(kd, v))
            vs.append(v)
        ps = [(ss[u] * dec_ref[h]).astype(BF16) for u, (_, h) in enumerate(units)]
        os_ = [_dot(ps[u], vs[u]) + iqs[u] * dec_ref[H + h] for u, (_, h) in enumerate(units)]
        for u, (sq, h) in enumerate(units):
            rows = rows_of[sq]
            st_ref[u] = st_ref[u] * math.exp(lgs[h] * CHUNK) + kvs[u]
            o = os_[u]
            ms = jnp.mean(o * o, axis=-1, keepdims=True)
            o = o * lax.rsqrt(ms + EPS) * nw_ref[:, h * LANES:(h + 1) * LANES]
            g = z_ref[rows, 3 * HW + h * LANES:3 * HW + (h + 1) * LANES].astype(F32)
            o_ref[rows, h * LANES:(h + 1) * LANES] = (o * _silu(g)).astype(o_ref.dtype)
        return carry

    lax.fori_loop(0, n_chunks, body, 0)


def _retention(zb, cos, sin, nw, B, S):
    T = B * S
    W = RET_HEADS * LANES
    R = RET_SEQS * S
    return pl.pallas_call(
        _retention_kernel,
        grid=(B // RET_SEQS,),
        in_specs=[pl.BlockSpec((R, 4 * W), lambda b: (b, 0)),
                  pl.BlockSpec((S, LANES), lambda b: (0, 0)),
                  pl.BlockSpec((S, LANES), lambda b: (0, 0)),
                  pl.BlockSpec((1, W), lambda b: (0, 0))],
        out_specs=pl.BlockSpec((R, W), lambda b: (b, 0)),
        out_shape=jax.ShapeDtypeStruct((T, W), BF16),
        scratch_shapes=[pltpu.VMEM((3 * RET_HEADS, CHUNK, LANES), F32),
                        pltpu.VMEM((RET_SEQS * RET_HEADS, LANES, LANES), F32)],
        compiler_params=_cparams("parallel"),
        name="retention",
    )(zb, cos, sin, nw)


def _da_kernel(*refs):
    z_refs = refs[:9]
    wq_ref, wk_ref, o_ref, qn_ref, kn_ref, v_ref, np_ref, lp_ref, mp_ref = refs[9:18]
    nn_refs, ln_refs, mn_refs = refs[18:21], refs[21:24], refs[24:27]
    stage_refs = refs[27:30]
    S = o_ref.shape[0]
    lane = lax.broadcasted_iota(jnp.int32, (1, LANES), 1)
    lo = lane < DA_DH
    row = lax.broadcasted_iota(jnp.int32, (CHUNK, CHUNK), 0)
    col = lax.broadcasted_iota(jnp.int32, (CHUNK, CHUNK), 1)
    mask_cur = col <= row
    row2 = lax.broadcasted_iota(jnp.int32, (CHUNK, 2 * CHUNK), 0)
    col2 = lax.broadcasted_iota(jnp.int32, (CHUNK, 2 * CHUNK), 1)
    mask_band = (col2 >= row2) & (col2 <= row2 + CHUNK)

    seg = (lax.broadcasted_iota(jnp.int32, (LANES, LANES), 0) // DA_DH
           == lax.broadcasted_iota(jnp.int32, (LANES, LANES), 1) // DA_DH).astype(BF16) * (1.0 / DA_DH)

    def head_norm(x, w):
        x2 = x * x
        hi = x2.astype(BF16)
        lo_part = (x2 - hi.astype(F32)).astype(BF16)
        ms = _dot(hi, seg) + _dot(lo_part, seg)
        return x * lax.rsqrt(ms + EPS) * w

    RB = 256
    for g, (dil, nb) in enumerate(DA_GROUPS):
        L = S // dil
        zq_ref, zk_ref, zv_ref = z_refs[3 * g:3 * g + 3]
        if dil > 1:
            for piece in range(S // RB):
                rows = pl.ds(piece * RB, RB)
                for src_ref, dst_ref in zip((zq_ref, zk_ref, zv_ref), stage_refs):
                    dst_ref[rows, :] = src_ref[rows, :].astype(F32)
            zq_ref, zk_ref, zv_ref = stage_refs
        first = min(dil, DA_MAX_STRIDE)
        if dil > first:
            mid_refs = (nn_refs[g], ln_refs[g], mn_refs[g])
            sub = S // first
            for b in range(first):
                for piece in range(sub // RB):
                    src = pl.ds(b + first * piece * RB, RB, stride=first)
                    dst = pl.ds(b * sub + piece * RB, RB)
                    for src_ref, dst_ref in zip((zq_ref, zk_ref, zv_ref), mid_refs):
                        dst_ref[dst, :] = src_ref[src, :]
            zq_ref, zk_ref, zv_ref = mid_refs
        rest = dil // first

        def source_rows(rho, piece, n_rows):
            if dil == 1:
                return pl.ds(piece * n_rows, n_rows)
            b, a = rho % first, rho // first
            if rest == 1:
                return pl.ds(b + first * piece * n_rows, n_rows, stride=first)
            return pl.ds(b * (S // first) + a + rest * piece * n_rows, n_rows, stride=rest)

        for rho in range(dil):
            for piece in range(max(L // RB, 1)):
                n_rows = min(RB, L)
                src = source_rows(rho, piece, n_rows)
                dst = pl.ds(rho * L + piece * n_rows, n_rows)
                q = zq_ref[src, :].astype(F32)
                k = zk_ref[src, :].astype(F32)
                v = zv_ref[src, :]
                qn_ref[dst, :] = (head_norm(q, wq_ref[...]) * (DA_DH ** -0.5)).astype(BF16)
                kn_ref[dst, :] = head_norm(k, wk_ref[...]).astype(BF16)
                v_ref[dst, :] = v.astype(BF16)

        def batch(r0, chained, first_has_prev):
            chains = []
            for u in range(DA_UNROLL):
                rows = pl.ds(r0 + u * CHUNK, CHUNK)
                if chained and (u > 0 or first_has_prev):
                    keys, mask = pl.ds(r0 + (u - 1) * CHUNK, 2 * CHUNK), mask_band
                else:
                    keys, mask = rows, mask_cur
                k = kn_ref[keys, :]
                v = v_ref[keys, :]
                vaug = jnp.concatenate([v, jnp.ones_like(v)], axis=1)
                q = qn_ref[rows, :]
                zero = jnp.zeros_like(q)
                for qh in (jnp.where(lo, q, zero), jnp.where(lo, zero, q)):
                    chains.append((_dot_nt(qh, k), mask, vaug))
            probs = []
            for s, mask, _ in chains:
                s = jnp.where(mask, s, NEG)
                m = jnp.max(s, axis=-1, keepdims=True)
                probs.append((jnp.exp(s - m).astype(BF16), m))
            accs = [_dot(p, vaug) for (p, _), (_, _, vaug) in zip(probs, chains)]
            nums, dens, maxs = [], [], []
            for u in range(DA_UNROLL):
                a0, a1 = accs[2 * u], accs[2 * u + 1]
                nums.append(jnp.where(lo, a0[:, :LANES], a1[:, :LANES]))
                dens.append(jnp.where(lo, a0[:, LANES:], a1[:, LANES:]))
                maxs.append(jnp.where(lo, probs[2 * u][1], probs[2 * u + 1][1]))
            rows = pl.ds(r0, DA_UNROLL * CHUNK)
            np_ref[rows, :] = jnp.concatenate(nums, axis=0)
            lp_ref[rows, :] = jnp.concatenate(dens, axis=0)
            mp_ref[rows, :] = jnp.concatenate(maxs, axis=0)

        span = DA_UNROLL * CHUNK
        if nb == 1:
            def singles(i, c):
                batch(pl.multiple_of(i * span, span), False, False)
                return c
            lax.fori_loop(0, S // span, singles, 0)
        else:
            def segment(sgi, c, nb=nb):
                base = pl.multiple_of(sgi * (nb * CHUNK), span)
                batch(base, True, False)
                if nb > DA_UNROLL:
                    def inner(n, c2):
                        batch(pl.multiple_of(base + n * span, span), True, True)
                        return c2
                    lax.fori_loop(1, nb // DA_UNROLL, inner, 0)
                return c
            lax.fori_loop(0, dil, segment, 0)

        outs = ((np_ref, nn_refs[g]), (lp_ref, ln_refs[g]), (mp_ref, mn_refs[g]))
        for rho in range(dil):
            for piece in range(max(L // RB, 1)):
                n_rows = min(RB, L)
                dst = source_rows(rho, piece, n_rows)
                src = pl.ds(rho * L + piece * n_rows, n_rows)
                for k_out, (perm_ref, nat_ref) in enumerate(outs):
                    (stage_refs[k_out] if rest > 1 else nat_ref)[dst, :] = perm_ref[src, :]
        if rest > 1:
            sub = S // first
            for b in range(first):
                for piece in range(sub // RB):
                    dst = pl.ds(b + first * piece * RB, RB, stride=first)
                    src = pl.ds(b * sub + piece * RB, RB)
                    for k_out, (_, nat_ref) in enumerate(outs):
                        nat_ref[dst, :] = stage_refs[k_out][src, :]

    for piece in range(S // RB):
        rows = pl.ds(piece * RB, RB)
        m = jnp.maximum(jnp.maximum(mn_refs[0][rows, :], mn_refs[1][rows, :]), mn_refs[2][rows, :])
        num = jnp.zeros((RB, LANES), F32)
        den = jnp.zeros((RB, LANES), F32)
        for g in range(3):
            e = jnp.exp(mn_refs[g][rows, :] - m)
            num = num + e * nn_refs[g][rows, :]
            den = den + e * ln_refs[g][rows, :]
        o_ref[rows, :] = (num / den).astype(o_ref.dtype)


def _dilated_attention(zb, slab0, wq, wk, B, S):
    T = B * S
    slab = lambda k: pl.BlockSpec((S, LANES), lambda b, p: (b, slab0 + 9 * p + k))
    return pl.pallas_call(
        _da_kernel,
        grid=(B, 2),
        in_specs=[slab(k) for k in range(9)] + [
                  pl.BlockSpec((1, LANES), lambda b, p: (0, 0)),
                  pl.BlockSpec((1, LANES), lambda b, p: (0, 0))],
        out_specs=pl.BlockSpec((S, LANES), lambda b, p: (b, p)),
        out_shape=jax.ShapeDtypeStruct((T, 2 * LANES), BF16),
        scratch_shapes=[pltpu.VMEM((S, LANES), BF16), pltpu.VMEM((S, LANES), BF16),
                        pltpu.VMEM((S, LANES), BF16),
                        pltpu.VMEM((S, LANES), F32), pltpu.VMEM((S, LANES), F32),
                        pltpu.VMEM((S, LANES), F32)] + [pltpu.VMEM((S, LANES), F32)] * 12,
        compiler_params=_cparams("parallel", "arbitrary"),
        name="dilated_attention",
    )(*([zb] * 9), wq, wk)


def _mlstm_kernel(qk_ref, v_ref, og_ref, gate_ref, gb_ref, nw_ref, o_ref, st_ref, e_ref, et_ref):
    S = qk_ref.shape[0] // ML_SEQS
    n_chunks = S // CHUNK
    W = qk_ref.shape[1]
    lane = lax.broadcasted_iota(jnp.int32, (1, LANES), 1)
    lo = lane < ML_DK
    row = lax.broadcasted_iota(jnp.int32, (CHUNK, CHUNK), 0)
    col = lax.broadcasted_iota(jnp.int32, (CHUNK, CHUNK), 1)
    causal = col <= row
    tril = causal.astype(F32)
    srow = lax.broadcasted_iota(jnp.int32, (LANES, 1), 0) < ML_DK
    st_ref[...] = jnp.zeros(st_ref.shape, F32)

    is_f = (lane >= ML_HEADS) & (lane < 2 * ML_HEADS)
    for n in range(ML_SEQS * n_chunks):
        rows = pl.ds(n * CHUNK, CHUNK)
        gp = gate_ref[rows, :] + gb_ref[...]
        logf = jnp.minimum(gp, 0.0) - jnp.log(1.0 + jnp.exp(-jnp.abs(gp)))
        cum = jnp.dot(tril, jnp.where(is_f, logf, 0.0), preferred_element_type=F32,
                      precision=lax.Precision.HIGHEST)
        e = jnp.where(lane < ML_HEADS, gp, cum)
        e_ref[rows, :] = e
        et_ref[n] = e.T[0:2 * ML_HEADS, :]

    units = [(sq, h) for sq in range(ML_SEQS) for h in range(ML_HEADS)]

    def chunk(n, mms):
        rows_of = [pl.ds(pl.multiple_of(sq * S + n * CHUNK, CHUNK), CHUNK) for sq in range(ML_SEQS)]
        es = [e_ref[rows_of[sq], :] for sq in range(ML_SEQS)]
        ets = [et_ref[sq * n_chunks + n] for sq in range(ML_SEQS)]
        khs, vaugs, s_raws, iqs = [], [], [], []
        for sq in range(ML_SEQS):
            rows = rows_of[sq]
            for pair in range(2):
                q2 = qk_ref[rows, pair * LANES:(pair + 1) * LANES]
                k2 = qk_ref[rows, W // 2 + pair * LANES:W // 2 + (pair + 1) * LANES]
                zero = jnp.zeros_like(q2)
                stb = st_ref[2 * sq + pair].astype(BF16)
                for hs in range(2):
                    h = 2 * pair + hs
                    qh = jnp.where(lo, q2, zero) if hs == 0 else jnp.where(lo, zero, q2)
                    kh = jnp.where(lo, k2, zero) if hs == 0 else jnp.where(lo, zero, k2)
                    vh = v_ref[rows, h * LANES:(h + 1) * LANES]
                    s_raws.append(_dot_nt(qh, k2))
                    iqs.append(_dot(qh, stb))
                    khs.append(kh)
                    vaugs.append(jnp.concatenate([vh, jnp.ones_like(vh)], axis=-1))
        ps, wis, ms_, kws, wcs, new_mms = [], [], [], [], [], []
        for u, (sq, h) in enumerate(units):
            e, et = es[sq], ets[sq]
            mm = mms[u]
            i_b = jnp.broadcast_to(e[:, h:h + 1], (CHUNK, LANES))
            a_b = jnp.broadcast_to(e[:, ML_HEADS + h:ML_HEADS + h + 1], (CHUNK, LANES))
            c_row = et[h:h + 1, :] - et[ML_HEADS + h:ML_HEADS + h + 1, :]
            dmat = jnp.where(causal, a_b + c_row, NEG)
            inter = a_b + mm
            m = jnp.maximum(inter, jnp.max(dmat, axis=-1, keepdims=True))
            ps.append((s_raws[u] * jnp.exp(dmat - m)).astype(BF16))
            wis.append(jnp.exp(inter - m))
            ms_.append(m)
            m_new = m[CHUNK - 1:CHUNK, :]
            a_last = a_b[CHUNK - 1:CHUNK, :]
            wk = jnp.exp(a_last - a_b + i_b - m_new)
            kws.append((khs[u].astype(F32) * wk).astype(BF16))
            wc = jnp.exp(a_last + mm - m_new)
            wcs.append(jnp.concatenate([wc, wc], axis=1))
            new_mms.append(m_new)
        accs = [_dot(ps[u], vaugs[u]) + jnp.concatenate([wis[u], wis[u]], axis=1) * iqs[u]
                for u in range(len(units))]
        upds = [_dot_tn(kws[u], vaugs[u]) for u in range(len(units))]
        for sp in range(2 * ML_SEQS):
            u0, u1 = 2 * sp, 2 * sp + 1
            st_ref[sp] = jnp.where(srow, wcs[u0], wcs[u1]) * st_ref[sp] + (upds[u0] + upds[u1])
        for u, (sq, h) in enumerate(units):
            num = accs[u][:, :LANES]
            den = accs[u][:, LANES:]
            hv = num / jnp.maximum(jnp.abs(den), jnp.exp(-ms_[u]))
            var = jnp.mean(hv * hv, axis=-1, keepdims=True)
            hv = hv * lax.rsqrt(var + EPS) * nw_ref[:, h * LANES:(h + 1) * LANES]
            og = og_ref[rows_of[sq], h * LANES:(h + 1) * LANES].astype(F32)
            o_ref[rows_of[sq], h * LANES:(h + 1) * LANES] = (hv * _sigmoid(og)).astype(o_ref.dtype)
        return tuple(new_mms)

    lax.fori_loop(0, n_chunks, chunk, tuple(jnp.zeros((1, LANES), F32) for _ in units))


def _mlstm(zb, zf, gate_bias, nw, B, S, qk_blk, gate_blk):
    T = B * S
    W = ML_HEADS * LANES
    R = ML_SEQS * S
    return pl.pallas_call(
        _mlstm_kernel,
        grid=(B // ML_SEQS,),
        in_specs=[pl.BlockSpec((R, W), lambda b: (b, qk_blk)),
                  pl.BlockSpec((R, W), lambda b: (b, qk_blk + 1)),
                  pl.BlockSpec((R, W), lambda b: (b, qk_blk + 2)),
                  pl.BlockSpec((R, LANES), lambda b: (b, gate_blk)),
                  pl.BlockSpec((1, LANES), lambda b: (0, 0)),
                  pl.BlockSpec((1, W), lambda b: (0, 0))],
        out_specs=pl.BlockSpec((R, W), lambda b: (b, 0)),
        out_shape=jax.ShapeDtypeStruct((T, W), BF16),
        scratch_shapes=[pltpu.VMEM((2 * ML_SEQS, LANES, 2 * LANES), F32),
                        pltpu.VMEM((R, LANES), F32),
                        pltpu.VMEM((R // CHUNK, 2 * ML_HEADS, LANES), F32)],
        compiler_params=_cparams("parallel"),
        name="mlstm",
    )(zb, zb, zb, zf, gate_bias, nw)


def _merge_value(x_ref, oret_ref, oda_ref, oml_ref, gr_ref, gd_ref, gm_ref,
                 wr_ref, wd_ref, wm_ref, wo_ref):
    y = _sigmoid(gr_ref[...].astype(F32)) * _dot(oret_ref[...], wr_ref[...])
    y = y + _sigmoid(gd_ref[...].astype(F32)) * _dot(oda_ref[...], wd_ref[...])
    y = y + _sigmoid(gm_ref[...].astype(F32)) * _dot(oml_ref[...], wm_ref[...])
    return x_ref[...] + _dot(y.astype(BF16), wo_ref[...])


def _merge_specs(x, o_ret, o_da, o_ml, g_blk, tm):
    D = x.shape[1]
    return [pl.BlockSpec((tm, D), lambda i: (i, 0)),
            pl.BlockSpec((tm, o_ret.shape[1]), lambda i: (i, 0)),
            pl.BlockSpec((tm, o_da.shape[1]), lambda i: (i, 0)),
            pl.BlockSpec((tm, o_ml.shape[1]), lambda i: (i, 0)),
            pl.BlockSpec((tm, D), lambda i: (i, g_blk)),
            pl.BlockSpec((tm, D), lambda i: (i, g_blk + 1)),
            pl.BlockSpec((tm, D), lambda i: (i, g_blk + 2))]


def _resident(a):
    return pl.BlockSpec(a.shape, lambda i: (0,) * a.ndim, pipeline_mode=pl.Buffered(1))


def _swiglu_block(h, wg_ref, wu_ref, wd_ref, a_ref):
    tf = wg_ref.shape[1]
    c0 = 0
    while c0 < tf:
        w = min(MXU_N, tf - c0)
        g = _dot(h, wg_ref[:, c0:c0 + w].astype(BF16))
        u = _dot(h, wu_ref[:, c0:c0 + w].astype(BF16))
        a_ref[:, c0:c0 + w] = (_silu(g) * u).astype(BF16)
        c0 += w
    return _dot(a_ref[...], wd_ref[...].astype(BF16))


def _merge_ffn_kernel(*refs):
    merge_refs, (nw_ref, wg_ref, wu_ref, wdn_ref, o_ref, a_ref) = refs[:11], refs[11:]
    x = _merge_value(*merge_refs)
    ms = jnp.mean(x * x, axis=-1, keepdims=True)
    h = (x * lax.rsqrt(ms + EPS) * nw_ref[...]).astype(BF16)
    o_ref[...] = x + _swiglu_block(h, wg_ref, wu_ref, wdn_ref, a_ref)


def _merge_ffn(x, o_ret, o_da, o_ml, zb, g_blk, wr, wd, wm, wo, nw, wg, wu, wdn, tm):
    T, D = x.shape
    weights = (wr, wd, wm, wo, nw, wg, wu, wdn)
    return pl.pallas_call(
        _merge_ffn_kernel,
        grid=(T // tm,),
        in_specs=_merge_specs(x, o_ret, o_da, o_ml, g_blk, tm) + [_resident(a) for a in weights],
        out_specs=pl.BlockSpec((tm, D), lambda i: (i, 0)),
        out_shape=jax.ShapeDtypeStruct((T, D), F32),
        scratch_shapes=[pltpu.VMEM((tm, wg.shape[1]), BF16)],
        compiler_params=_cparams("parallel"),
        name="merge_ffn",
    )(x, o_ret, o_da, o_ml, zb, zb, zb, *weights)


HI16 = 0xFFFF0000


def _pack_bf16_pairs(a, b):
    ab = pltpu.bitcast(a.astype(BF16).astype(F32), jnp.uint32)
    bb = pltpu.bitcast(b.astype(BF16).astype(F32), jnp.uint32)
    return (ab >> 16) | (bb & jnp.uint32(HI16))


def _unpack_bf16_pairs(p):
    lo = pltpu.bitcast(p << 16, F32)
    hi = pltpu.bitcast(p & jnp.uint32(HI16), F32)
    return lo, hi


def _merge_router_kernel(*refs):
    merge_refs, (nw_ref, wr_ref, br_ref, xo_ref, ha_ref, hb_ref, sel_ref) = refs[:11], refs[11:]
    x = _merge_value(*merge_refs)
    xo_ref[...] = x
    ms = jnp.mean(x * x, axis=-1, keepdims=True)
    h = x * lax.rsqrt(ms + EPS) * nw_ref[...]
    Q = h.shape[1] // 4
    ha_ref[...] = _pack_bf16_pairs(h[:, 0:Q], h[:, Q:2 * Q])
    hb_ref[...] = _pack_bf16_pairs(h[:, 2 * Q:3 * Q], h[:, 3 * Q:4 * Q])
    w = wr_ref[...]
    h_hi = h.astype(BF16)
    h_lo = (h - h_hi.astype(F32)).astype(BF16)
    w_hi = w.astype(BF16)
    w_lo = (w - w_hi.astype(F32)).astype(BF16)
    logits = _dot(h_hi, w_hi) + _dot(h_lo, w_hi) + _dot(h_hi, w_lo) + br_ref[...]
    lane = lax.broadcasted_iota(jnp.int32, logits.shape, 1).astype(F32)
    logits = jnp.where(lane < N_EXPERTS, logits, NEG)
    m1 = jnp.max(logits, axis=-1, keepdims=True)
    i1 = jnp.min(jnp.where(logits == m1, lane, float(LANES)), axis=-1, keepdims=True)
    rest = jnp.where(lane == i1, NEG, logits)
    m2 = jnp.max(rest, axis=-1, keepdims=True)
    i2 = jnp.min(jnp.where(rest == m2, lane, float(LANES)), axis=-1, keepdims=True)
    e2 = jnp.exp(m2 - m1)
    p1 = 1.0 / (1.0 + e2)
    p2 = e2 / (1.0 + e2)
    sel_ref[...] = jnp.where(lane == 0.0, i1, jnp.where(lane == 1.0, i2,
                             jnp.where(lane == 2.0, p1, jnp.where(lane == 3.0, p2, 0.0))))


def _merge_router(x, o_ret, o_da, o_ml, zb, g_blk, wr, wd, wm, wo, nw, w_router, b_router, tm):
    T, D = x.shape
    Q = D // 4
    wrt = jnp.pad(w_router, ((0, 0), (0, LANES - N_EXPERTS)))
    brt = jnp.pad(b_router, (0, LANES - N_EXPERTS)).reshape(1, LANES)
    weights = (wr, wd, wm, wo, nw, wrt, brt)
    return pl.pallas_call(
        _merge_router_kernel,
        grid=(T // tm,),
        in_specs=_merge_specs(x, o_ret, o_da, o_ml, g_blk, tm) + [_resident(a) for a in weights],
        out_specs=[pl.BlockSpec((tm, D), lambda i: (i, 0)),
                   pl.BlockSpec((tm, Q), lambda i: (i, 0)),
                   pl.BlockSpec((tm, Q), lambda i: (i, 0)),
                   pl.BlockSpec((tm, LANES), lambda i: (i, 0))],
        out_shape=[jax.ShapeDtypeStruct((T, D), F32),
                   jax.ShapeDtypeStruct((T, Q), jnp.uint32), jax.ShapeDtypeStruct((T, Q), jnp.uint32),
                   jax.ShapeDtypeStruct((T, LANES), F32)],
        compiler_params=_cparams("parallel"),
        name="merge_router",
    )(x, o_ret, o_da, o_ml, zb, zb, zb, *weights)


def _moe_rank_kernel(sel_ref, pos_ref, meta_ref, cnt_ref, offs_ref, carry_ref, before_ref):
    ph = pl.program_id(0)
    i = pl.program_id(1)
    tm = sel_ref.shape[0]
    lane = lax.broadcasted_iota(jnp.int32, (tm, LANES), 1).astype(F32)
    lane1 = lax.broadcasted_iota(jnp.int32, (1, LANES), 1).astype(F32)
    sel = sel_ref[...]
    i1 = sel[:, 0:1]
    i2 = sel[:, 1:2]
    onehot = jnp.where((lane == i1) | (lane == i2), 1.0, 0.0)
    colsum = jnp.sum(onehot, axis=0, keepdims=True)

    @pl.when((ph == 0) & (i == 0))
    def _():
        cnt_ref[...] = jnp.zeros(cnt_ref.shape, F32)

    @pl.when(ph == 0)
    def _():
        cnt_ref[...] += colsum

    def padded_counts():
        return jnp.floor((cnt_ref[...] + (MOE_TILE - 1.0)) * (1.0 / MOE_TILE)) * MOE_TILE

    @pl.when((ph == 1) & (i == 0))
    def _():
        k = lax.broadcasted_iota(jnp.int32, (LANES, LANES), 0)
        e = lax.broadcasted_iota(jnp.int32, (LANES, LANES), 1)
        upper = (k < e).astype(F32)
        offs_ref[...] = jnp.dot(padded_counts(), upper, preferred_element_type=F32,
                                precision=lax.Precision.HIGHEST)
        carry_ref[...] = jnp.zeros(carry_ref.shape, F32)
        r = lax.broadcasted_iota(jnp.int32, (tm, tm), 0)
        c = lax.broadcasted_iota(jnp.int32, (tm, tm), 1)
        before_ref[...] = (c < r).astype(BF16)

    @pl.when(ph == 1)
    def _():
        rank = _dot(before_ref[...], onehot.astype(BF16)) + carry_ref[0:1, :]
        row = rank + offs_ref[0:1, :]
        pos1 = jnp.sum(jnp.where(lane == i1, row, 0.0), axis=-1, keepdims=True)
        pos2 = jnp.sum(jnp.where(lane == i2, row, 0.0), axis=-1, keepdims=True)
        both = jnp.where(lane == 0.0, pos1, jnp.where(lane == 1.0, pos2, 0.0))
        pos_ref[...] = both.T[0:8, :].astype(jnp.int32)
        carry_ref[...] += colsum

    @pl.when((ph == 1) & (i == pl.num_programs(1) - 1))
    def _():
        nrow = meta_ref.shape[0]
        padded = padded_counts()[0:1, :]
        offs = offs_ref[0:1, :]
        ends = offs + padded
        start = lax.broadcasted_iota(jnp.int32, (nrow, 1), 0).astype(F32) * MOE_TILE
        is_e = lane1 < N_EXPERTS
        te = jnp.sum(jnp.where(is_e & (ends <= start), 1.0, 0.0), axis=-1, keepdims=True)
        te = jnp.minimum(te, N_EXPERTS - 1.0)
        valid_end = jnp.sum(jnp.where(lane1 == te, offs + cnt_ref[0:1, :], 0.0), axis=-1, keepdims=True)
        nv = jnp.clip(valid_end - start, 0.0, MOE_TILE)
        total = jnp.sum(jnp.where(is_e, padded, 0.0), axis=-1, keepdims=True)
        active = jnp.where(start < total, 1.0, 0.0)
        meta_ref[...] = jnp.where(lane1 == 0.0, te, jnp.where(lane1 == 1.0, nv,
                                  jnp.where(lane1 == 2.0, active, 0.0)))


def _moe_rank(sel, tm, meta_rows):
    T = sel.shape[0]
    nt = T // tm
    return pl.pallas_call(
        _moe_rank_kernel,
        grid=(2, nt),
        in_specs=[pl.BlockSpec((tm, LANES), lambda ph, i: (i, 0))],
        out_specs=[pl.BlockSpec((None, 8, tm), lambda ph, i: (i * ph, 0, 0)),
                   pl.BlockSpec((meta_rows, LANES), lambda ph, i: (0, 0))],
        out_shape=[jax.ShapeDtypeStruct((nt, 8, tm), jnp.int32),
                   jax.ShapeDtypeStruct((meta_rows, LANES), F32)],
        scratch_shapes=[pltpu.VMEM((8, LANES), F32), pltpu.VMEM((8, LANES), F32),
                        pltpu.VMEM((8, LANES), F32), pltpu.VMEM((tm, tm), BF16)],
        compiler_params=_cparams("arbitrary", "arbitrary"),
        name="moe_rank",
    )(sel)


def _sc_mesh():
    return plsc.VectorSubcoreMesh(core_axis_name="core", subcore_axis_name="subcore")


def _sc_scatter_rows(x, idx, n_rows):
    n_idx = idx.shape[1]
    nt = x.shape[0] // SC_WINDOW
    width = x.shape[1]

    @functools.partial(pl.kernel, out_type=jax.ShapeDtypeStruct((n_rows, width), x.dtype),
                       mesh=_sc_mesh())
    def scatter_kernel(x_hbm, i_hbm, o_hbm):
        def body(x_vmem, i_vmem):
            pltpu.sync_copy(x_vmem, o_hbm.at[i_vmem.at[0]])

        pltpu.emit_pipeline(
            body, grid=(n_idx // SC_WINDOW,),
            in_specs=[pl.BlockSpec((SC_WINDOW, width), lambda i: (i % nt, 0)),
                      pl.BlockSpec((1, SC_WINDOW), lambda i: (0, i))],
            out_specs=[],
            core_axis_name=("core", "subcore"), dimension_semantics=(pltpu.PARALLEL,),
        )(x_hbm, i_hbm)

    return scatter_kernel(x, idx)


def _sc_gather_rows(x, idx):
    n_idx = idx.shape[1]
    width = x.shape[1]

    @functools.partial(pl.kernel, out_type=jax.ShapeDtypeStruct((n_idx, width), x.dtype),
                       mesh=_sc_mesh())
    def gather_kernel(x_hbm, i_hbm, o_hbm):
        def body(i_vmem, o_vmem):
            pltpu.sync_copy(x_hbm.at[i_vmem.at[0]], o_vmem)

        pltpu.emit_pipeline(
            body, grid=(n_idx // SC_WINDOW,),
            in_specs=[pl.BlockSpec((1, SC_WINDOW), lambda i: (0, i))],
            out_specs=[pl.BlockSpec((SC_WINDOW, width), lambda i: (i, 0))],
            core_axis_name=("core", "subcore"), dimension_semantics=(pltpu.PARALLEL,),
        )(i_hbm, o_hbm)

    return gather_kernel(x, idx)


def _moe_group_kernel(te_ref, nv_ref, na_ref, xa_ref, xb_ref, wg_ref, wu_ref, wd_ref,
                      ya_ref, yb_ref, h_ref, a_ref):
    j = pl.program_id(0)

    @pl.when(j < na_ref[0])
    def _():
        tm = h_ref.shape[0]
        Q = xa_ref.shape[1]
        valid = lax.broadcasted_iota(jnp.int32, (tm, 1), 0) < nv_ref[j]
        for src, c0 in ((xa_ref, 0), (xb_ref, 2 * Q)):
            lo, hi = _unpack_bf16_pairs(src[...])
            h_ref[:, c0:c0 + Q] = jnp.where(valid, lo, 0.0).astype(BF16)
            h_ref[:, c0 + Q:c0 + 2 * Q] = jnp.where(valid, hi, 0.0).astype(BF16)
        y = _swiglu_block(h_ref[...], wg_ref, wu_ref, wd_ref, a_ref)
        ya_ref[...] = _pack_bf16_pairs(y[:, 0:Q], y[:, Q:2 * Q])
        yb_ref[...] = _pack_bf16_pairs(y[:, 2 * Q:3 * Q], y[:, 3 * Q:4 * Q])


def _moe_group(te, nv, na, xa, xb, wg, wu, wd):
    R, Q = xa.shape
    E, D, F = wg.shape
    tile = lambda j, te, nv, na: (jnp.minimum(j, na[0] - 1), 0)
    expert = lambda j, te, nv, na: (te[j], 0, 0)
    once = pl.Buffered(1)
    grid_spec = pltpu.PrefetchScalarGridSpec(
        num_scalar_prefetch=3,
        grid=(R // MOE_TILE,),
        in_specs=[pl.BlockSpec((MOE_TILE, Q), tile),
                  pl.BlockSpec((MOE_TILE, Q), tile),
                  pl.BlockSpec((None, D, F), expert),
                  pl.BlockSpec((None, D, F), expert),
                  pl.BlockSpec((None, F, D), expert, pipeline_mode=once)],
        out_specs=[pl.BlockSpec((MOE_TILE, Q), tile), pl.BlockSpec((MOE_TILE, Q), tile)],
        scratch_shapes=[pltpu.VMEM((MOE_TILE, D), BF16), pltpu.VMEM((MOE_TILE, F), BF16)],
    )
    return pl.pallas_call(
        _moe_group_kernel,
        grid_spec=grid_spec,
        out_shape=[jax.ShapeDtypeStruct((R, Q), jnp.uint32), jax.ShapeDtypeStruct((R, Q), jnp.uint32)],
        compiler_params=_cparams("arbitrary"),
        name="moe_experts",
    )(te, nv, na, xa, xb, wg, wu, wd)


def _moe_combine_kernel(x_ref, sel_ref, g1_ref, g2_ref, *rest):
    o_ref = rest[-1]
    sel = sel_ref[...]
    p1 = sel[:, 2:3]
    p2 = sel[:, 3:4]
    Q = g1_ref.shape[1]
    lo1, hi1 = _unpack_bf16_pairs(g1_ref[...])
    lo2, hi2 = _unpack_bf16_pairs(g2_ref[...])
    o_ref[:, 0:Q] = x_ref[:, 0:Q] + (p1 * lo1 + p2 * lo2)
    o_ref[:, Q:2 * Q] = x_ref[:, Q:2 * Q] + (p1 * hi1 + p2 * hi2)


def _moe_combine_half(x, sel, g, half, partial_out, tm):
    T, D = x.shape
    Q = g.shape[1]
    nt = T // tm
    in_specs = [pl.BlockSpec((tm, 2 * Q), lambda i: (i, half)),
                pl.BlockSpec((tm, LANES), lambda i: (i, 0)),
                pl.BlockSpec((tm, Q), lambda i: (i, 0)),
                pl.BlockSpec((tm, Q), lambda i: (i + nt, 0))]
    args = [x, sel, g, g]
    aliases = {}
    if partial_out is not None:
        in_specs.append(pl.BlockSpec(memory_space=pl.ANY))
        args.append(partial_out)
        aliases = {4: 0}
    return pl.pallas_call(
        _moe_combine_kernel,
        grid=(nt,),
        in_specs=in_specs,
        out_specs=pl.BlockSpec((tm, 2 * Q), lambda i: (i, half)),
        out_shape=jax.ShapeDtypeStruct((T, D), F32),
        input_output_aliases=aliases,
        compiler_params=_cparams("parallel"),
        name="moe_combine",
    )(*args)


def _cast_kernel(x_ref, o_ref):
    o_ref[...] = x_ref[...].astype(o_ref.dtype)


def _cast_bf16(w, rows):
    cols = w.shape[-1]
    w2 = w.reshape(-1, cols)
    out = pl.pallas_call(
        _cast_kernel,
        grid=(w2.shape[0] // rows,),
        in_specs=[pl.BlockSpec((rows, cols), lambda i: (i, 0))],
        out_specs=pl.BlockSpec((rows, cols), lambda i: (i, 0)),
        out_shape=jax.ShapeDtypeStruct(w2.shape, BF16),
        compiler_params=_cparams("parallel"),
        name="cast_bf16",
    )(w2)
    return out.reshape(w.shape)


def _moe(x, ha, hb, sel, wg, wu, wd):
    T, D = x.shape
    n_rows = TOP_K * T + N_EXPERTS * MOE_TILE
    n_tiles = n_rows // MOE_TILE
    pos, meta = _moe_rank(sel, WIDE_ROW_TILE, 256)
    idx = jnp.concatenate([pos[:, 0, :].reshape(1, T), pos[:, 1, :].reshape(1, T)], axis=1)
    te = meta[:n_tiles, 0].astype(jnp.int32)
    nv = meta[:n_tiles, 1].astype(jnp.int32)
    na = jnp.sum(meta[:n_tiles, 2]).astype(jnp.int32).reshape(1)
    te = jnp.where(jnp.arange(n_tiles) < na[0], te, te[na[0] - 1])
    xa = _sc_scatter_rows(ha, idx, n_rows)
    xb = _sc_scatter_rows(hb, idx, n_rows)
    ya, yb = _moe_group(te, nv, na, xa, xb, wg, wu, wd)
    ga = _sc_gather_rows(ya, idx)
    gb = _sc_gather_rows(yb, idx)
    out = _moe_combine_half(x, sel, ga, 0, None, WIDE_ROW_TILE)
    return _moe_combine_half(x, sel, gb, 1, out, WIDE_ROW_TILE)


def _split_w_in(w_in):
    sizes = (512, 512, 512, 512, 768, 768, 768, 256, 256, 512, 512, 4, 4, 1024, 1024, 1024)
    offs = [0]
    for s in sizes:
        offs.append(offs[-1] + s)
    part = lambda i: w_in[..., offs[i]:offs[i + 1]]
    rq, rk, rv, rg, dq, dk, dv, mq, mk, mv, mo, mi, mf, g_ret, g_da, g_ml = (part(i) for i in range(16))
    cols = []
    for p in range(2):
        for g in range(3):
            for t in (dq, dk, dv):
                cols.append(t[..., g * 256 + p * 128:g * 256 + (p + 1) * 128])
    wb = jnp.concatenate([rq, rk, rv, rg, g_ret, g_da, g_ml, mq, mk, mv, mo] + cols,
                         axis=-1).astype(BF16)
    pad = jnp.zeros(w_in.shape[:-1] + (LANES - 8,), w_in.dtype)
    wf = jnp.concatenate([mi, mf, pad], axis=-1).astype(BF16)
    return wb, wf


def _rope_tables(S):
    half = LANES // 2
    inv = jnp.power(ROPE_BASE, -jnp.arange(half, dtype=F32) / half)
    ang = jnp.arange(S, dtype=F32)[:, None] * inv[None, :]
    cos = jnp.cos(ang)
    sin = jnp.sin(ang)
    return jnp.concatenate([cos, cos], axis=1), jnp.concatenate([-sin, sin], axis=1)


def kernel(x, norm1_w, w_in, ret_norm_w, da_q_norm_w, da_k_norm_w, ml_conv_w, ml_i_bias, ml_f_bias,
           ml_norm_w, w_br_ret, w_br_da, w_br_ml, w_out, norm2_w, ffn_w_gate, ffn_w_up, ffn_w_down,
           moe_w_router, moe_b_router, moe_w_gate, moe_w_up, moe_w_down):
    B, S, D = x.shape
    T = B * S
    depth = w_in.shape[0]
    cos, sin = _rope_tables(S)
    wb_all, wf_all = _split_w_in(w_in)
    xt = x.reshape(T, D)
    for layer in range(depth):
        wb, wf = wb_all[layer], wf_all[layer]
        nw1 = norm1_w[layer].reshape(1, D)
        assert OFF_MLQK % INPROJ_TN == 0
        zb, zf = _inproj(xt, nw1, wb, wf, ml_conv_w[layer], OFF_MLQK // INPROJ_TN, S, INPROJ_TN)
        o_ret = _retention(zb, cos, sin, ret_norm_w[layer].reshape(1, -1), B, S)
        wq = jnp.tile(da_q_norm_w[layer], 2).reshape(1, LANES)
        wk = jnp.tile(da_k_norm_w[layer], 2).reshape(1, LANES)
        o_da = _dilated_attention(zb, OFF_DA // LANES, wq, wk, B, S)
        gate_bias = jnp.concatenate([ml_i_bias[layer], ml_f_bias[layer],
                                     jnp.zeros((LANES - 2 * ML_HEADS,), F32)]).reshape(1, LANES)
        o_ml = _mlstm(zb, zf, gate_bias, ml_norm_w[layer].reshape(1, -1), B, S,
                      qk_blk=OFF_MLQK // (ML_HEADS * LANES), gate_blk=0)
        merge_args = (xt, o_ret, o_da, o_ml, zb, OFF_GATES // D_MODEL,
                      w_br_ret[layer].astype(BF16), w_br_da[layer].astype(BF16),
                      w_br_ml[layer].astype(BF16), w_out[layer].astype(BF16),
                      norm2_w[layer].reshape(1, D))
        j = layer // 2
        if layer % 2 == 0:
            xt = _merge_ffn(*merge_args, ffn_w_gate[j].astype(BF16), ffn_w_up[j].astype(BF16),
                            ffn_w_down[j].astype(BF16), ROW_TILE)
        else:
            xt, ha, hb, sel = _merge_router(*merge_args, moe_w_router[j], moe_b_router[j], ROW_TILE)
            xt = _moe(xt, ha, hb, sel, _cast_bf16(moe_w_gate[j], 512), _cast_bf16(moe_w_up[j], 512),
                      _cast_bf16(moe_w_down[j], 2048))
    return xt.reshape(B, S, D)
```

```python
import functools
import math

import jax
import jax.numpy as jnp
from jax import lax
from jax.experimental import pallas as pl
from jax.experimental.pallas import tpu as pltpu
from jax.experimental.pallas import tpu_sc as plsc

F32 = jnp.float32
BF16 = jnp.bfloat16

EPS = 1e-6
D_MODEL = 1024
CHUNK = 128
LANES = 128
MXU_N = 256
ROPE_BASE = 10000.0
RET_HEADS = 4
RET_SEQS = 2
DA_GROUPS = ((1, 16), (4, 4), (16, 1))
DA_DH = 64
DA_UNROLL = 4
DA_MAX_STRIDE = 4
ML_HEADS = 4
ML_DK = 64
ML_SEQS = 2
N_EXPERTS = 8
TOP_K = 2
MOE_TILE = 512
SC_WINDOW = 128
INPROJ_TN = 1024
ROW_TILE = 512
WIDE_ROW_TILE = 1024
OFF_MLQK = 4 * RET_HEADS * LANES
OFF_DA = OFF_MLQK + 3 * ML_HEADS * LANES
VMEM_LIMIT = 56 * 1024 * 1024

NEG = -1e30


def _cparams(*sem):
    return pltpu.CompilerParams(dimension_semantics=sem, vmem_limit_bytes=VMEM_LIMIT)


def _dot(a, b):
    return jnp.dot(a, b, preferred_element_type=F32)


def _dot_nt(a, b):
    return lax.dot_general(a, b, (((1,), (1,)), ((), ())), preferred_element_type=F32)


def _dot_tn(a, b):
    return lax.dot_general(a, b, (((0,), (0,)), ((), ())), preferred_element_type=F32)


def _sigmoid(x):
    return 1.0 / (1.0 + jnp.exp(-x))


def _silu(x):
    return x * _sigmoid(x)


def _inproj_kernel(conv_blk, x_ref, nw_ref, wb_ref, wf_ref, cw_ref, zb_ref, zf_ref, h_ref, cv_ref):
    j = pl.program_id(1)
    tm = x_ref.shape[0]

    @pl.when(j == 0)
    def _():
        rc = 512
        for r in range(tm // rc):
            rows = pl.ds(r * rc, rc)
            x = x_ref[rows, :]
            ms = jnp.mean(x * x, axis=-1, keepdims=True)
            h = (x * lax.rsqrt(ms + EPS) * nw_ref[...]).astype(BF16)
            h_ref[rows, :] = h
            zb_ref[rows, :] = _dot(h, wb_ref[...]).astype(zb_ref.dtype)
            zf_ref[rows, :] = _dot(h, wf_ref[...])

    @pl.when((j > 0) & (j != conv_blk))
    def _():
        zb_ref[...] = _dot(h_ref[...], wb_ref[...]).astype(zb_ref.dtype)

    @pl.when(j == conv_blk)
    def _():
        h = h_ref[...]
        cw = cw_ref.shape[1]
        pad = cv_ref.shape[0] - tm
        cv_ref[0:pad, :] = jnp.zeros((pad, cw), F32)
        cv_ref[pad:pad + tm, :] = _dot(h, wb_ref[:, 0:cw])
        taps = cw_ref.shape[0]
        rb = 256

        def conv(piece, cs):
            cols = slice(cs * LANES, (cs + 1) * LANES)
            r0 = pad + piece * rb - (taps - 1)
            acc = cw_ref[0:1, cols] * cv_ref[pl.ds(r0, rb), cols]
            for i in range(1, taps):
                acc = acc + cw_ref[i:i + 1, cols] * cv_ref[pl.ds(r0 + i, rb), cols]
            scale = ML_DK ** -0.5 if cs >= cw // LANES // 2 else 1.0
            zb_ref[piece * rb:(piece + 1) * rb, cols] = (_silu(acc) * scale).astype(zb_ref.dtype)

        todo = [(piece, cs) for piece in range(tm // rb) for cs in range(cw // LANES)]
        chunks = list(range(cw, zb_ref.shape[1], MXU_N))
        per = -(-len(todo) // len(chunks))
        for n, c0 in enumerate(chunks):
            c1 = min(c0 + MXU_N, zb_ref.shape[1])
            zb_ref[:, c0:c1] = _dot(h, wb_ref[:, c0:c1]).astype(zb_ref.dtype)
            for piece, cs in todo[n * per:(n + 1) * per]:
                conv(piece, cs)


def _inproj(x, nw, wb, wf, conv_w, conv_blk, tm, tnb):
    T, D = x.shape
    return pl.pallas_call(
        functools.partial(_inproj_kernel, conv_blk),
        grid=(T // tm, wb.shape[1] // tnb),
        in_specs=[pl.BlockSpec((tm, D), lambda i, j: (i, 0)),
                  pl.BlockSpec((1, D), lambda i, j: (0, 0)),
                  pl.BlockSpec((D, tnb), lambda i, j: (0, j)),
                  pl.BlockSpec(wf.shape, lambda i, j: (0, 0)),
                  pl.BlockSpec(conv_w.shape, lambda i, j: (0, 0))],
        out_specs=[pl.BlockSpec((tm, tnb), lambda i, j: (i, j)),
                   pl.BlockSpec((tm, wf.shape[1]), lambda i, j: (i, 0))],
        out_shape=[jax.ShapeDtypeStruct((T, wb.shape[1]), BF16),
                   jax.ShapeDtypeStruct((T, wf.shape[1]), F32)],
        scratch_shapes=[pltpu.VMEM((tm, D), BF16), pltpu.VMEM((tm + 8, conv_w.shape[1]), F32)],
        compiler_params=_cparams("parallel", "arbitrary"),
        name="inproj",
    )(x, nw, wb, wf, conv_w)


def _retention_kernel(z_ref, cos_ref, sin_ref, nw_ref, o_ref, dec_ref, st_ref):
    S = z_ref.shape[0] // RET_SEQS
    n_chunks = S // CHUNK
    H = RET_HEADS
    HW = H * LANES
    row = lax.broadcasted_iota(jnp.int32, (CHUNK, CHUNK), 0).astype(F32)
    col = lax.broadcasted_iota(jnp.int32, (CHUNK, CHUNK), 1).astype(F32)
    lgs = [math.log1p(-(2.0 ** (-5.0 - h))) for h in range(H)]
    for h, lg in enumerate(lgs):
        rel = row - col
        dec_ref[h] = jnp.where(rel >= 0, jnp.exp(lg * jnp.maximum(rel, 0.0)), 0.0)
        dec_ref[H + h] = jnp.exp(lg * (row + 1.0))
        dec_ref[2 * H + h] = jnp.exp(lg * (CHUNK - 1.0 - row)) * (LANES ** -0.5)
    st_ref[...] = jnp.zeros(st_ref.shape, F32)
    units = [(sq, h) for sq in range(RET_SEQS) for h in range(H)]

    def body(n, carry):
        r0 = pl.multiple_of(n * CHUNK, CHUNK)
        cos = cos_ref[pl.ds(r0, CHUNK), :]
        sin = sin_ref[pl.ds(r0, CHUNK), :]
        rows_of = [pl.ds(pl.multiple_of(sq * S + n * CHUNK, CHUNK), CHUNK) for sq in range(RET_SEQS)]
        vs, ss, iqs, kvs = [], [], [], []
        for u, (sq, h) in enumerate(units):
            rows = rows_of[sq]
            q = z_ref[rows, h * LANES:(h + 1) * LANES].astype(F32)
            k = z_ref[rows, HW + h * LANES:HW + (h + 1) * LANES].astype(F32)
            v = z_ref[rows, 2 * HW + h * LANES:2 * HW + (h + 1) * LANES]
            q = q * cos + pltpu.roll(q, LANES // 2, 1) * sin
            k = k * cos + pltpu.roll(k, LANES // 2, 1) * sin
            qb = q.astype(BF16)
            kb = (k * (LANES ** -0.5)).astype(BF16)
            kd = (k * dec_ref[2 * H + h]).astype(BF16)
            ss.append(_dot_nt(qb, kb))
            iqs.append(_dot(qb, st_ref[u].astype(BF16)))
            kvs.append(_dot_tn(kd, v))
            vs.append(v)
        ps = [(ss[u] * dec_ref[h]).astype(BF16) for u, (_, h) in enumerate(units)]
        os_ = [_dot(ps[u], vs[u]) + iqs[u] * dec_ref[H + h] for u, (_, h) in enumerate(units)]
        for u, (sq, h) in enumerate(units):
            rows = rows_of[sq]
            st_ref[u] = st_ref[u] * math.exp(lgs[h] * CHUNK) + kvs[u]
            o = os_[u]
            ms = jnp.mean(o * o, axis=-1, keepdims=True)
            o = o * lax.rsqrt(ms + EPS) * nw_ref[:, h * LANES:(h + 1) * LANES]
            g = z_ref[rows, 3 * HW + h * LANES:3 * HW + (h + 1) * LANES].astype(F32)
            o_ref[rows, h * LANES:(h + 1) * LANES] = (o * _silu(g)).astype(o_ref.dtype)
        return carry

    lax.fori_loop(0, n_chunks, body, 0)


def _retention(zb, cos, sin, nw, B, S):
    T = B * S
    W = RET_HEADS * LANES
    R = RET_SEQS * S
    return pl.pallas_call(
        _retention_kernel,
        grid=(B // RET_SEQS,),
        in_specs=[pl.BlockSpec((R, 4 * W), lambda b: (b, 0)),
                  pl.BlockSpec((S, LANES), lambda b: (0, 0)),
                  pl.BlockSpec((S, LANES), lambda b: (0, 0)),
                  pl.BlockSpec((1, W), lambda b: (0, 0))],
        out_specs=pl.BlockSpec((R, W), lambda b: (b, 0)),
        out_shape=jax.ShapeDtypeStruct((T, W), BF16),
        scratch_shapes=[pltpu.VMEM((3 * RET_HEADS, CHUNK, LANES), F32),
                        pltpu.VMEM((RET_SEQS * RET_HEADS, LANES, LANES), F32)],
        compiler_params=_cparams("parallel"),
        name="retention",
    )(zb, cos, sin, nw)


def _da_kernel(*refs):
    z_refs = refs[:9]
    wq_ref, wk_ref, o_ref, qn_ref, kn_ref, v_ref, np_ref, lp_ref, mp_ref = refs[9:18]
    nn_refs, ln_refs, mn_refs = refs[18:21], refs[21:24], refs[24:27]
    stage_refs = refs[27:30]
    S = o_ref.shape[0]
    lane = lax.broadcasted_iota(jnp.int32, (1, LANES), 1)
    lo = lane < DA_DH
    row = lax.broadcasted_iota(jnp.int32, (CHUNK, CHUNK), 0)
    col = lax.broadcasted_iota(jnp.int32, (CHUNK, CHUNK), 1)
    mask_cur = col <= row
    row2 = lax.broadcasted_iota(jnp.int32, (CHUNK, 2 * CHUNK), 0)
    col2 = lax.broadcasted_iota(jnp.int32, (CHUNK, 2 * CHUNK), 1)
    mask_band = (col2 >= row2) & (col2 <= row2 + CHUNK)

    seg = (lax.broadcasted_iota(jnp.int32, (LANES, LANES), 0) // DA_DH
           == lax.broadcasted_iota(jnp.int32, (LANES, LANES), 1) // DA_DH).astype(BF16) * (1.0 / DA_DH)

    def head_norm(x, w):
        x2 = x * x
        hi = x2.astype(BF16)
        lo_part = (x2 - hi.astype(F32)).astype(BF16)
        ms = _dot(hi, seg) + _dot(lo_part, seg)
        return x * lax.rsqrt(ms + EPS) * w

    RB = 256
    for g, (dil, nb) in enumerate(DA_GROUPS):
        L = S // dil
        zq_ref, zk_ref, zv_ref = z_refs[3 * g:3 * g + 3]
        if dil > 1:
            for piece in range(S // RB):
                rows = pl.ds(piece * RB, RB)
                for src_ref, dst_ref in zip((zq_ref, zk_ref, zv_ref), stage_refs):
                    dst_ref[rows, :] = src_ref[rows, :].astype(F32)
            zq_ref, zk_ref, zv_ref = stage_refs
        first = min(dil, DA_MAX_STRIDE)
        if dil > first:
            mid_refs = (nn_refs[g], ln_refs[g], mn_refs[g])
            sub = S // first
            for b in range(first):
                for piece in range(sub // RB):
                    src = pl.ds(b + first * piece * RB, RB, stride=first)
                    dst = pl.ds(b * sub + piece * RB, RB)
                    for src_ref, dst_ref in zip((zq_ref, zk_ref, zv_ref), mid_refs):
                        dst_ref[dst, :] = src_ref[src, :]
            zq_ref, zk_ref, zv_ref = mid_refs
        rest = dil // first

        def source_rows(rho, piece, n_rows):
            if dil == 1:
                return pl.ds(piece * n_rows, n_rows)
            b, a = rho % first, rho // first
            if rest == 1:
                return pl.ds(b + first * piece * n_rows, n_rows, stride=first)
            return pl.ds(b * (S // first) + a + rest * piece * n_rows, n_rows, stride=rest)

        for rho in range(dil):
            for piece in range(max(L // RB, 1)):
                n_rows = min(RB, L)
                src = source_rows(rho, piece, n_rows)
                dst = pl.ds(rho * L + piece * n_rows, n_rows)
                q = zq_ref[src, :].astype(F32)
                k = zk_ref[src, :].astype(F32)
                v = zv_ref[src, :]
                qn_ref[dst, :] = (head_norm(q, wq_ref[...]) * (DA_DH ** -0.5)).astype(BF16)
                kn_ref[dst, :] = head_norm(k, wk_ref[...]).astype(BF16)
                v_ref[dst, :] = v.astype(BF16)

        def batch(r0, chained, first_has_prev):
            chains = []
            for u in range(DA_UNROLL):
                rows = pl.ds(r0 + u * CHUNK, CHUNK)
                if chained and (u > 0 or first_has_prev):
                    keys, mask = pl.ds(r0 + (u - 1) * CHUNK, 2 * CHUNK), mask_band
                else:
                    keys, mask = rows, mask_cur
                k = kn_ref[keys, :]
                v = v_ref[keys, :]
                vaug = jnp.concatenate([v, jnp.ones_like(v)], axis=1)
                q = qn_ref[rows, :]
                zero = jnp.zeros_like(q)
                for qh in (jnp.where(lo, q, zero), jnp.where(lo, zero, q)):
                    chains.append((_dot_nt(qh, k), mask, vaug))
            probs = []
            for s, mask, _ in chains:
                s = jnp.where(mask, s, NEG)
                m = jnp.max(s, axis=-1, keepdims=True)
                probs.append((jnp.exp(s - m).astype(BF16), m))
            accs = [_dot(p, vaug) for (p, _), (_, _, vaug) in zip(probs, chains)]
            nums, dens, maxs = [], [], []
            for u in range(DA_UNROLL):
                a0, a1 = accs[2 * u], accs[2 * u + 1]
                nums.append(jnp.where(lo, a0[:, :LANES], a1[:, :LANES]))
                dens.append(jnp.where(lo, a0[:, LANES:], a1[:, LANES:]))
                maxs.append(jnp.where(lo, probs[2 * u][1], probs[2 * u + 1][1]))
            rows = pl.ds(r0, DA_UNROLL * CHUNK)
            np_ref[rows, :] = jnp.concatenate(nums, axis=0)
            lp_ref[rows, :] = jnp.concatenate(dens, axis=0)
            mp_ref[rows, :] = jnp.concatenate(maxs, axis=0)

        span = DA_UNROLL * CHUNK
        if nb == 1:
            def singles(i, c):
                batch(pl.multiple_of(i * span, span), False, False)
                return c
            lax.fori_loop(0, S // span, singles, 0)
        else:
            def segment(sgi, c, nb=nb):
                base = pl.multiple_of(sgi * (nb * CHUNK), span)
                batch(base, True, False)
                if nb > DA_UNROLL:
                    def inner(n, c2):
                        batch(pl.multiple_of(base + n * span, span), True, True)
                        return c2
                    lax.fori_loop(1, nb // DA_UNROLL, inner, 0)
                return c
            lax.fori_loop(0, dil, segment, 0)

        outs = ((np_ref, nn_refs[g]), (lp_ref, ln_refs[g]), (mp_ref, mn_refs[g]))
        for rho in range(dil):
            for piece in range(max(L // RB, 1)):
                n_rows = min(RB, L)
                dst = source_rows(rho, piece, n_rows)
                src = pl.ds(rho * L + piece * n_rows, n_rows)
                for k_out, (perm_ref, nat_ref) in enumerate(outs):
                    (stage_refs[k_out] if rest > 1 else nat_ref)[dst, :] = perm_ref[src, :]
        if rest > 1:
            sub = S // first
            for b in range(first):
                for piece in range(sub // RB):
                    dst = pl.ds(b + first * piece * RB, RB, stride=first)
                    src = pl.ds(b * sub + piece * RB, RB)
                    for k_out, (_, nat_ref) in enumerate(outs):
                        nat_ref[dst, :] = stage_refs[k_out][src, :]

    for piece in range(S // RB):
        rows = pl.ds(piece * RB, RB)
        m = jnp.maximum(jnp.maximum(mn_refs[0][rows, :], mn_refs[1][rows, :]), mn_refs[2][rows, :])
        num = jnp.zeros((RB, LANES), F32)
        den = jnp.zeros((RB, LANES), F32)
        for g in range(3):
            e = jnp.exp(mn_refs[g][rows, :] - m)
            num = num + e * nn_refs[g][rows, :]
            den = den + e * ln_refs[g][rows, :]
        o_ref[rows, :] = (num / den).astype(o_ref.dtype)


def _dilated_attention(zb, slab0, wq, wk, B, S):
    T = B * S
    slab = lambda k: pl.BlockSpec((S, LANES), lambda b, p: (b, slab0 + 9 * p + k))
    return pl.pallas_call(
        _da_kernel,
        grid=(B, 2),
        in_specs=[slab(k) for k in range(9)] + [
                  pl.BlockSpec((1, LANES), lambda b, p: (0, 0)),
                  pl.BlockSpec((1, LANES), lambda b, p: (0, 0))],
        out_specs=pl.BlockSpec((S, LANES), lambda b, p: (b, p)),
        out_shape=jax.ShapeDtypeStruct((T, 2 * LANES), BF16),
        scratch_shapes=[pltpu.VMEM((S, LANES), BF16), pltpu.VMEM((S, LANES), BF16),
                        pltpu.VMEM((S, LANES), BF16),
                        pltpu.VMEM((S, LANES), F32), pltpu.VMEM((S, LANES), F32),
                        pltpu.VMEM((S, LANES), F32)] + [pltpu.VMEM((S, LANES), F32)] * 12,
        compiler_params=_cparams("parallel", "arbitrary"),
        name="dilated_attention",
    )(*([zb] * 9), wq, wk)


def _mlstm_kernel(qk_ref, v_ref, og_ref, gate_ref, gb_ref, nw_ref, o_ref, st_ref, e_ref, et_ref):
    S = qk_ref.shape[0] // ML_SEQS
    n_chunks = S // CHUNK
    W = qk_ref.shape[1]
    lane = lax.broadcasted_iota(jnp.int32, (1, LANES), 1)
    lo = lane < ML_DK
    row = lax.broadcasted_iota(jnp.int32, (CHUNK, CHUNK), 0)
    col = lax.broadcasted_iota(jnp.int32, (CHUNK, CHUNK), 1)
    causal = col <= row
    tril = causal.astype(F32)
    srow = lax.broadcasted_iota(jnp.int32, (LANES, 1), 0) < ML_DK
    st_ref[...] = jnp.zeros(st_ref.shape, F32)

    is_f = (lane >= ML_HEADS) & (lane < 2 * ML_HEADS)
    for n in range(ML_SEQS * n_chunks):
        rows = pl.ds(n * CHUNK, CHUNK)
        gp = gate_ref[rows, :] + gb_ref[...]
        logf = jnp.minimum(gp, 0.0) - jnp.log(1.0 + jnp.exp(-jnp.abs(gp)))
        cum = jnp.dot(tril, jnp.where(is_f, logf, 0.0), preferred_element_type=F32,
                      precision=lax.Precision.HIGHEST)
        e = jnp.where(lane < ML_HEADS, gp, cum)
        e_ref[rows, :] = e
        et_ref[n] = e.T[0:2 * ML_HEADS, :]

    units = [(sq, h) for sq in range(ML_SEQS) for h in range(ML_HEADS)]

    def chunk(n, mms):
        rows_of = [pl.ds(pl.multiple_of(sq * S + n * CHUNK, CHUNK), CHUNK) for sq in range(ML_SEQS)]
        es = [e_ref[rows_of[sq], :] for sq in range(ML_SEQS)]
        ets = [et_ref[sq * n_chunks + n] for sq in range(ML_SEQS)]
        khs, vaugs, s_raws, iqs = [], [], [], []
        for sq in range(ML_SEQS):
            rows = rows_of[sq]
            for pair in range(2):
                q2 = qk_ref[rows, pair * LANES:(pair + 1) * LANES]
                k2 = qk_ref[rows, W // 2 + pair * LANES:W // 2 + (pair + 1) * LANES]
                zero = jnp.zeros_like(q2)
                stb = st_ref[2 * sq + pair].astype(BF16)
                for hs in range(2):
                    h = 2 * pair + hs
                    qh = jnp.where(lo, q2, zero) if hs == 0 else jnp.where(lo, zero, q2)
                    kh = jnp.where(lo, k2, zero) if hs == 0 else jnp.where(lo, zero, k2)
                    vh = v_ref[rows, h * LANES:(h + 1) * LANES]
                    s_raws.append(_dot_nt(qh, k2))
                    iqs.append(_dot(qh, stb))
                    khs.append(kh)
                    vaugs.append(jnp.concatenate([vh, jnp.ones_like(vh)], axis=-1))
        ps, wis, ms_, kws, wcs, new_mms = [], [], [], [], [], []
        for u, (sq, h) in enumerate(units):
            e, et = es[sq], ets[sq]
            mm = mms[u]
            i_b = jnp.broadcast_to(e[:, h:h + 1], (CHUNK, LANES))
            a_b = jnp.broadcast_to(e[:, ML_HEADS + h:ML_HEADS + h + 1], (CHUNK, LANES))
            c_row = et[h:h + 1, :] - et[ML_HEADS + h:ML_HEADS + h + 1, :]
            dmat = jnp.where(causal, a_b + c_row, NEG)
            inter = a_b + mm
            m = jnp.maximum(inter, jnp.max(dmat, axis=-1, keepdims=True))
            ps.append((s_raws[u] * jnp.exp(dmat - m)).astype(BF16))
            wis.append(jnp.exp(inter - m))
            ms_.append(m)
            m_new = m[CHUNK - 1:CHUNK, :]
            a_last = a_b[CHUNK - 1:CHUNK, :]
            wk = jnp.exp(a_last - a_b + i_b - m_new)
            kws.append((khs[u].astype(F32) * wk).astype(BF16))
            wc = jnp.exp(a_last + mm - m_new)
            wcs.append(jnp.concatenate([wc, wc], axis=1))
            new_mms.append(m_new)
        accs = [_dot(ps[u], vaugs[u]) + jnp.concatenate([wis[u], wis[u]], axis=1) * iqs[u]
                for u in range(len(units))]
        upds = [_dot_tn(kws[u], vaugs[u]) for u in range(len(units))]
        for sp in range(2 * ML_SEQS):
            u0, u1 = 2 * sp, 2 * sp + 1
            st_ref[sp] = jnp.where(srow, wcs[u0], wcs[u1]) * st_ref[sp] + (upds[u0] + upds[u1])
        for u, (sq, h) in enumerate(units):
            num = accs[u][:, :LANES]
            den = accs[u][:, LANES:]
            hv = num / jnp.maximum(jnp.abs(den), jnp.exp(-ms_[u]))
            var = jnp.mean(hv * hv, axis=-1, keepdims=True)
            hv = hv * lax.rsqrt(var + EPS) * nw_ref[:, h * LANES:(h + 1) * LANES]
            og = og_ref[rows_of[sq], h * LANES:(h + 1) * LANES].astype(F32)
            o_ref[rows_of[sq], h * LANES:(h + 1) * LANES] = (hv * _sigmoid(og)).astype(o_ref.dtype)
        return tuple(new_mms)

    lax.fori_loop(0, n_chunks, chunk, tuple(jnp.zeros((1, LANES), F32) for _ in units))


def _mlstm(zb, zf, gate_bias, nw, B, S, qk_blk, gate_blk):
    T = B * S
    W = ML_HEADS * LANES
    R = ML_SEQS * S
    return pl.pallas_call(
        _mlstm_kernel,
        grid=(B // ML_SEQS,),
        in_specs=[pl.BlockSpec((R, W), lambda b: (b, qk_blk)),
                  pl.BlockSpec((R, W), lambda b: (b, qk_blk + 1)),
                  pl.BlockSpec((R, W), lambda b: (b, qk_blk + 2)),
                  pl.BlockSpec((R, LANES), lambda b: (b, gate_blk)),
                  pl.BlockSpec((1, LANES), lambda b: (0, 0)),
                  pl.BlockSpec((1, W), lambda b: (0, 0))],
        out_specs=pl.BlockSpec((R, W), lambda b: (b, 0)),
        out_shape=jax.ShapeDtypeStruct((T, W), BF16),
        scratch_shapes=[pltpu.VMEM((2 * ML_SEQS, LANES, 2 * LANES), F32),
                        pltpu.VMEM((R, LANES), F32),
                        pltpu.VMEM((R // CHUNK, 2 * ML_HEADS, LANES), F32)],
        compiler_params=_cparams("parallel"),
        name="mlstm",
    )(zb, zb, zb, zf, gate_bias, nw)


N_MERGE_REFS = 10


def _merge_value(x_ref, oret_ref, oda_ref, oml_ref, nw1_ref, wg_ref, wr_ref, wd_ref, wm_ref, wo_ref):
    x = x_ref[...]
    D = x.shape[1]
    ms = jnp.mean(x * x, axis=-1, keepdims=True)
    h = (x * lax.rsqrt(ms + EPS) * nw1_ref[...]).astype(BF16)
    y = None
    for n, (o_ref, w_ref) in enumerate(((oret_ref, wr_ref), (oda_ref, wd_ref), (oml_ref, wm_ref))):
        gate = _sigmoid(_dot(h, wg_ref[:, n * D:(n + 1) * D]))
        term = gate * _dot(o_ref[...], w_ref[...])
        y = term if y is None else y + term
    return x + _dot(y.astype(BF16), wo_ref[...])


def _merge_specs(x, o_ret, o_da, o_ml, tm):
    return [pl.BlockSpec((tm, a.shape[1]), lambda i: (i, 0)) for a in (x, o_ret, o_da, o_ml)]


def _resident(a):
    return pl.BlockSpec(a.shape, lambda i: (0,) * a.ndim, pipeline_mode=pl.Buffered(1))


def _swiglu_block(h, wg_ref, wu_ref, wd_ref, a_ref):
    tf = wg_ref.shape[1]
    c0 = 0
    while c0 < tf:
        w = min(MXU_N, tf - c0)
        g = _dot(h, wg_ref[:, c0:c0 + w].astype(BF16))
        u = _dot(h, wu_ref[:, c0:c0 + w].astype(BF16))
        a_ref[:, c0:c0 + w] = (_silu(g) * u).astype(BF16)
        c0 += w
    return _dot(a_ref[...], wd_ref[...].astype(BF16))


def _merge_ffn_kernel(*refs):
    merge_refs, (nw_ref, wg_ref, wu_ref, wdn_ref, o_ref, a_ref) = refs[:N_MERGE_REFS], refs[N_MERGE_REFS:]
    x = _merge_value(*merge_refs)
    ms = jnp.mean(x * x, axis=-1, keepdims=True)
    h = (x * lax.rsqrt(ms + EPS) * nw_ref[...]).astype(BF16)
    o_ref[...] = x + _swiglu_block(h, wg_ref, wu_ref, wdn_ref, a_ref)


def _merge_ffn(x, o_ret, o_da, o_ml, nw1, wgates, wr, wd, wm, wo, nw, wg, wu, wdn, tm):
    T, D = x.shape
    weights = (nw1, wgates, wr, wd, wm, wo, nw, wg, wu, wdn)
    return pl.pallas_call(
        _merge_ffn_kernel,
        grid=(T // tm,),
        in_specs=_merge_specs(x, o_ret, o_da, o_ml, tm) + [_resident(a) for a in weights],
        out_specs=pl.BlockSpec((tm, D), lambda i: (i, 0)),
        out_shape=jax.ShapeDtypeStruct((T, D), F32),
        scratch_shapes=[pltpu.VMEM((tm, wg.shape[1]), BF16)],
        compiler_params=_cparams("parallel"),
        name="merge_ffn",
    )(x, o_ret, o_da, o_ml, *weights)


HI16 = 0xFFFF0000


def _pack_bf16_pairs(a, b):
    ab = pltpu.bitcast(a.astype(BF16).astype(F32), jnp.uint32)
    bb = pltpu.bitcast(b.astype(BF16).astype(F32), jnp.uint32)
    return (ab >> 16) | (bb & jnp.uint32(HI16))


def _unpack_bf16_pairs(p):
    lo = pltpu.bitcast(p << 16, F32)
    hi = pltpu.bitcast(p & jnp.uint32(HI16), F32)
    return lo, hi


def _merge_router_kernel(*refs):
    merge_refs, (nw_ref, wr_ref, br_ref, xo_ref, ha_ref, hb_ref, sel_ref) = refs[:N_MERGE_REFS], refs[N_MERGE_REFS:]
    x = _merge_value(*merge_refs)
    xo_ref[...] = x
    ms = jnp.mean(x * x, axis=-1, keepdims=True)
    h = x * lax.rsqrt(ms + EPS) * nw_ref[...]
    Q = h.shape[1] // 4
    ha_ref[...] = _pack_bf16_pairs(h[:, 0:Q], h[:, Q:2 * Q])
    hb_ref[...] = _pack_bf16_pairs(h[:, 2 * Q:3 * Q], h[:, 3 * Q:4 * Q])
    w = wr_ref[...]
    h_hi = h.astype(BF16)
    h_lo = (h - h_hi.astype(F32)).astype(BF16)
    w_hi = w.astype(BF16)
    w_lo = (w - w_hi.astype(F32)).astype(BF16)
    logits = _dot(h_hi, w_hi) + _dot(h_lo, w_hi) + _dot(h_hi, w_lo) + br_ref[...]
    lane = lax.broadcasted_iota(jnp.int32, logits.shape, 1).astype(F32)
    logits = jnp.where(lane < N_EXPERTS, logits, NEG)
    m1 = jnp.max(logits, axis=-1, keepdims=True)
    i1 = jnp.min(jnp.where(logits == m1, lane, float(LANES)), axis=-1, keepdims=True)
    rest = jnp.where(lane == i1, NEG, logits)
    m2 = jnp.max(rest, axis=-1, keepdims=True)
    i2 = jnp.min(jnp.where(rest == m2, lane, float(LANES)), axis=-1, keepdims=True)
    e2 = jnp.exp(m2 - m1)
    p1 = 1.0 / (1.0 + e2)
    p2 = e2 / (1.0 + e2)
    sel_ref[...] = jnp.where(lane == 0.0, i1, jnp.where(lane == 1.0, i2,
                             jnp.where(lane == 2.0, p1, jnp.where(lane == 3.0, p2, 0.0))))


def _merge_router(x, o_ret, o_da, o_ml, nw1, wgates, wr, wd, wm, wo, nw, w_router, b_router, tm):
    T, D = x.shape
    Q = D // 4
    wrt = jnp.pad(w_router, ((0, 0), (0, LANES - N_EXPERTS)))
    brt = jnp.pad(b_router, (0, LANES - N_EXPERTS)).reshape(1, LANES)
    weights = (nw1, wgates, wr, wd, wm, wo, nw, wrt, brt)
    return pl.pallas_call(
        _merge_router_kernel,
        grid=(T // tm,),
        in_specs=_merge_specs(x, o_ret, o_da, o_ml, tm) + [_resident(a) for a in weights],
        out_specs=[pl.BlockSpec((tm, D), lambda i: (i, 0)),
                   pl.BlockSpec((tm, Q), lambda i: (i, 0)),
                   pl.BlockSpec((tm, Q), lambda i: (i, 0)),
                   pl.BlockSpec((tm, LANES), lambda i: (i, 0))],
        out_shape=[jax.ShapeDtypeStruct((T, D), F32),
                   jax.ShapeDtypeStruct((T, Q), jnp.uint32), jax.ShapeDtypeStruct((T, Q), jnp.uint32),
                   jax.ShapeDtypeStruct((T, LANES), F32)],
        compiler_params=_cparams("parallel"),
        name="merge_router",
    )(x, o_ret, o_da, o_ml, *weights)


def _moe_rank_kernel(sel_ref, pos_ref, meta_ref, cnt_ref, offs_ref, carry_ref, before_ref):
    ph = pl.program_id(0)
    i = pl.program_id(1)
    tm = sel_ref.shape[0]
    lane = lax.broadcasted_iota(jnp.int32, (tm, LANES), 1).astype(F32)
    lane1 = lax.broadcasted_iota(jnp.int32, (1, LANES), 1).astype(F32)
    sel = sel_ref[...]
    i1 = sel[:, 0:1]
    i2 = sel[:, 1:2]
    onehot = jnp.where((lane == i1) | (lane == i2), 1.0, 0.0)
    colsum = jnp.sum(onehot, axis=0, keepdims=True)

    @pl.when((ph == 0) & (i == 0))
    def _():
        cnt_ref[...] = jnp.zeros(cnt_ref.shape, F32)

    @pl.when(ph == 0)
    def _():
        cnt_ref[...] += colsum

    def padded_counts():
        return jnp.floor((cnt_ref[...] + (MOE_TILE - 1.0)) * (1.0 / MOE_TILE)) * MOE_TILE

    @pl.when((ph == 1) & (i == 0))
    def _():
        k = lax.broadcasted_iota(jnp.int32, (LANES, LANES), 0)
        e = lax.broadcasted_iota(jnp.int32, (LANES, LANES), 1)
        upper = (k < e).astype(F32)
        offs_ref[...] = jnp.dot(padded_counts(), upper, preferred_element_type=F32,
                                precision=lax.Precision.HIGHEST)
        carry_ref[...] = jnp.zeros(carry_ref.shape, F32)
        r = lax.broadcasted_iota(jnp.int32, (tm, tm), 0)
        c = lax.broadcasted_iota(jnp.int32, (tm, tm), 1)
        before_ref[...] = (c < r).astype(BF16)

    @pl.when(ph == 1)
    def _():
        rank = _dot(before_ref[...], onehot.astype(BF16)) + carry_ref[0:1, :]
        row = rank + offs_ref[0:1, :]
        pos1 = jnp.sum(jnp.where(lane == i1, row, 0.0), axis=-1, keepdims=True)
        pos2 = jnp.sum(jnp.where(lane == i2, row, 0.0), axis=-1, keepdims=True)
        both = jnp.where(lane == 0.0, pos1, jnp.where(lane == 1.0, pos2, 0.0))
        pos_ref[...] = both.T[0:8, :].astype(jnp.int32)
        carry_ref[...] += colsum

    @pl.when((ph == 1) & (i == pl.num_programs(1) - 1))
    def _():
        nrow = meta_ref.shape[0]
        padded = padded_counts()[0:1, :]
        offs = offs_ref[0:1, :]
        ends = offs + padded
        start = lax.broadcasted_iota(jnp.int32, (nrow, 1), 0).astype(F32) * MOE_TILE
        is_e = lane1 < N_EXPERTS
        te = jnp.sum(jnp.where(is_e & (ends <= start), 1.0, 0.0), axis=-1, keepdims=True)
        te = jnp.minimum(te, N_EXPERTS - 1.0)
        valid_end = jnp.sum(jnp.where(lane1 == te, offs + cnt_ref[0:1, :], 0.0), axis=-1, keepdims=True)
        nv = jnp.clip(valid_end - start, 0.0, MOE_TILE)
        total = jnp.sum(jnp.where(is_e, padded, 0.0), axis=-1, keepdims=True)
        active = jnp.where(start < total, 1.0, 0.0)
        meta_ref[...] = jnp.where(lane1 == 0.0, te, jnp.where(lane1 == 1.0, nv,
                                  jnp.where(lane1 == 2.0, active, 0.0)))


def _moe_rank(sel, tm, meta_rows):
    T = sel.shape[0]
    nt = T // tm
    return pl.pallas_call(
        _moe_rank_kernel,
        grid=(2, nt),
        in_specs=[pl.BlockSpec((tm, LANES), lambda ph, i: (i, 0))],
        out_specs=[pl.BlockSpec((None, 8, tm), lambda ph, i: (i * ph, 0, 0)),
                   pl.BlockSpec((meta_rows, LANES), lambda ph, i: (0, 0))],
        out_shape=[jax.ShapeDtypeStruct((nt, 8, tm), jnp.int32),
                   jax.ShapeDtypeStruct((meta_rows, LANES), F32)],
        scratch_shapes=[pltpu.VMEM((8, LANES), F32), pltpu.VMEM((8, LANES), F32),
                        pltpu.VMEM((8, LANES), F32), pltpu.VMEM((tm, tm), BF16)],
        compiler_params=_cparams("arbitrary", "arbitrary"),
        name="moe_rank",
    )(sel)


def _sc_mesh():
    return plsc.VectorSubcoreMesh(core_axis_name="core", subcore_axis_name="subcore")


def _sc_scatter_rows(x, idx, n_rows):
    n_idx = idx.shape[1]
    nt = x.shape[0] // SC_WINDOW
    width = x.shape[1]

    @functools.partial(pl.kernel, out_type=jax.ShapeDtypeStruct((n_rows, width), x.dtype),
                       mesh=_sc_mesh())
    def scatter_kernel(x_hbm, i_hbm, o_hbm):
        def body(x_vmem, i_vmem):
            pltpu.sync_copy(x_vmem, o_hbm.at[i_vmem.at[0]])

        pltpu.emit_pipeline(
            body, grid=(n_idx // SC_WINDOW,),
            in_specs=[pl.BlockSpec((SC_WINDOW, width), lambda i: (i % nt, 0)),
                      pl.BlockSpec((1, SC_WINDOW), lambda i: (0, i))],
            out_specs=[],
            core_axis_name=("core", "subcore"), dimension_semantics=(pltpu.PARALLEL,),
        )(x_hbm, i_hbm)

    return scatter_kernel(x, idx)


def _sc_gather_rows(x, idx):
    n_idx = idx.shape[1]
    width = x.shape[1]

    @functools.partial(pl.kernel, out_type=jax.ShapeDtypeStruct((n_idx, width), x.dtype),
                       mesh=_sc_mesh())
    def gather_kernel(x_hbm, i_hbm, o_hbm):
        def body(i_vmem, o_vmem):
            pltpu.sync_copy(x_hbm.at[i_vmem.at[0]], o_vmem)

        pltpu.emit_pipeline(
            body, grid=(n_idx // SC_WINDOW,),
            in_specs=[pl.BlockSpec((1, SC_WINDOW), lambda i: (0, i))],
            out_specs=[pl.BlockSpec((SC_WINDOW, width), lambda i: (i, 0))],
            core_axis_name=("core", "subcore"), dimension_semantics=(pltpu.PARALLEL,),
        )(i_hbm, o_hbm)

    return gather_kernel(x, idx)


def _moe_group_kernel(te_ref, nv_ref, na_ref, xa_ref, xb_ref, wg_ref, wu_ref, wd_ref,
                      ya_ref, yb_ref, h_ref, a_ref):
    j = pl.program_id(0)

    @pl.when(j < na_ref[0])
    def _():
        tm = h_ref.shape[0]
        Q = xa_ref.shape[1]
        valid = lax.broadcasted_iota(jnp.int32, (tm, 1), 0) < nv_ref[j]
        for src, c0 in ((xa_ref, 0), (xb_ref, 2 * Q)):
            lo, hi = _unpack_bf16_pairs(src[...])
            h_ref[:, c0:c0 + Q] = jnp.where(valid, lo, 0.0).astype(BF16)
            h_ref[:, c0 + Q:c0 + 2 * Q] = jnp.where(valid, hi, 0.0).astype(BF16)
        y = _swiglu_block(h_ref[...], wg_ref, wu_ref, wd_ref, a_ref)
        ya_ref[...] = _pack_bf16_pairs(y[:, 0:Q], y[:, Q:2 * Q])
        yb_ref[...] = _pack_bf16_pairs(y[:, 2 * Q:3 * Q], y[:, 3 * Q:4 * Q])


def _moe_group(te, nv, na, xa, xb, wg, wu, wd):
    R, Q = xa.shape
    E, D, F = wg.shape
    tile = lambda j, te, nv, na: (jnp.minimum(j, na[0] - 1), 0)
    expert = lambda j, te, nv, na: (te[j], 0, 0)
    once = pl.Buffered(1)
    grid_spec = pltpu.PrefetchScalarGridSpec(
        num_scalar_prefetch=3,
        grid=(R // MOE_TILE,),
        in_specs=[pl.BlockSpec((MOE_TILE, Q), tile),
                  pl.BlockSpec((MOE_TILE, Q), tile),
                  pl.BlockSpec((None, D, F), expert),
                  pl.BlockSpec((None, D, F), expert),
                  pl.BlockSpec((None, F, D), expert, pipeline_mode=once)],
        out_specs=[pl.BlockSpec((MOE_TILE, Q), tile), pl.BlockSpec((MOE_TILE, Q), tile)],
        scratch_shapes=[pltpu.VMEM((MOE_TILE, D), BF16), pltpu.VMEM((MOE_TILE, F), BF16)],
    )
    return pl.pallas_call(
        _moe_group_kernel,
        grid_spec=grid_spec,
        out_shape=[jax.ShapeDtypeStruct((R, Q), jnp.uint32), jax.ShapeDtypeStruct((R, Q), jnp.uint32)],
        compiler_params=_cparams("arbitrary"),
        name="moe_experts",
    )(te, nv, na, xa, xb, wg, wu, wd)


def _moe_combine_kernel(x_ref, sel_ref, g1_ref, g2_ref, *rest):
    o_ref = rest[-1]
    sel = sel_ref[...]
    p1 = sel[:, 2:3]
    p2 = sel[:, 3:4]
    Q = g1_ref.shape[1]
    lo1, hi1 = _unpack_bf16_pairs(g1_ref[...])
    lo2, hi2 = _unpack_bf16_pairs(g2_ref[...])
    o_ref[:, 0:Q] = x_ref[:, 0:Q] + (p1 * lo1 + p2 * lo2)
    o_ref[:, Q:2 * Q] = x_ref[:, Q:2 * Q] + (p1 * hi1 + p2 * hi2)


def _moe_combine_half(x, sel, g, half, partial_out, tm):
    T, D = x.shape
    Q = g.shape[1]
    nt = T // tm
    in_specs = [pl.BlockSpec((tm, 2 * Q), lambda i: (i, half)),
                pl.BlockSpec((tm, LANES), lambda i: (i, 0)),
                pl.BlockSpec((tm, Q), lambda i: (i, 0)),
                pl.BlockSpec((tm, Q), lambda i: (i + nt, 0))]
    args = [x, sel, g, g]
    aliases = {}
    if partial_out is not None:
        in_specs.append(pl.BlockSpec(memory_space=pl.ANY))
        args.append(partial_out)
        aliases = {4: 0}
    return pl.pallas_call(
        _moe_combine_kernel,
        grid=(nt,),
        in_specs=in_specs,
        out_specs=pl.BlockSpec((tm, 2 * Q), lambda i: (i, half)),
        out_shape=jax.ShapeDtypeStruct((T, D), F32),
        input_output_aliases=aliases,
        compiler_params=_cparams("parallel"),
        name="moe_combine",
    )(*args)


def _cast_kernel(x_ref, o_ref):
    o_ref[...] = x_ref[...].astype(o_ref.dtype)


def _cast_bf16(w, rows):
    cols = w.shape[-1]
    w2 = w.reshape(-1, cols)
    out = pl.pallas_call(
        _cast_kernel,
        grid=(w2.shape[0] // rows,),
        in_specs=[pl.BlockSpec((rows, cols), lambda i: (i, 0))],
        out_specs=pl.BlockSpec((rows, cols), lambda i: (i, 0)),
        out_shape=jax.ShapeDtypeStruct(w2.shape, BF16),
        compiler_params=_cparams("parallel"),
        name="cast_bf16",
    )(w2)
    return out.reshape(w.shape)


def _moe(x, ha, hb, sel, wg, wu, wd):
    T, D = x.shape
    n_rows = TOP_K * T + N_EXPERTS * MOE_TILE
    n_tiles = n_rows // MOE_TILE
    pos, meta = _moe_rank(sel, WIDE_ROW_TILE, 256)
    idx = jnp.concatenate([pos[:, 0, :].reshape(1, T), pos[:, 1, :].reshape(1, T)], axis=1)
    te = meta[:n_tiles, 0].astype(jnp.int32)
    nv = meta[:n_tiles, 1].astype(jnp.int32)
    na = jnp.sum(meta[:n_tiles, 2]).astype(jnp.int32).reshape(1)
    te = jnp.where(jnp.arange(n_tiles) < na[0], te, te[na[0] - 1])
    xa = _sc_scatter_rows(ha, idx, n_rows)
    xb = _sc_scatter_rows(hb, idx, n_rows)
    ya, yb = _moe_group(te, nv, na, xa, xb, wg, wu, wd)
    ga = _sc_gather_rows(ya, idx)
    gb = _sc_gather_rows(yb, idx)
    out = _moe_combine_half(x, sel, ga, 0, None, WIDE_ROW_TILE)
    return _moe_combine_half(x, sel, gb, 1, out, WIDE_ROW_TILE)


def _split_w_in(w_in):
    sizes = (512, 512, 512, 512, 768, 768, 768, 256, 256, 512, 512, 4, 4, 1024, 1024, 1024)
    offs = [0]
    for s in sizes:
        offs.append(offs[-1] + s)
    part = lambda i: w_in[..., offs[i]:offs[i + 1]]
    rq, rk, rv, rg, dq, dk, dv, mq, mk, mv, mo, mi, mf, g_ret, g_da, g_ml = (part(i) for i in range(16))
    cols = []
    for p in range(2):
        for g in range(3):
            for t in (dq, dk, dv):
                cols.append(t[..., g * 256 + p * 128:g * 256 + (p + 1) * 128])
    used = OFF_DA + len(cols) * LANES
    zpad = jnp.zeros(w_in.shape[:-1] + (-used % INPROJ_TN,), w_in.dtype)
    wb = jnp.concatenate([rq, rk, rv, rg, mq, mk, mv, mo] + cols + [zpad], axis=-1).astype(BF16)
    wgates = jnp.concatenate([g_ret, g_da, g_ml], axis=-1).astype(BF16)
    pad = jnp.zeros(w_in.shape[:-1] + (LANES - 8,), w_in.dtype)
    wf = jnp.concatenate([mi, mf, pad], axis=-1).astype(BF16)
    return wb, wf, wgates


def _rope_tables(S):
    half = LANES // 2
    inv = jnp.power(ROPE_BASE, -jnp.arange(half, dtype=F32) / half)
    ang = jnp.arange(S, dtype=F32)[:, None] * inv[None, :]
    cos = jnp.cos(ang)
    sin = jnp.sin(ang)
    return jnp.concatenate([cos, cos], axis=1), jnp.concatenate([-sin, sin], axis=1)


def kernel(x, norm1_w, w_in, ret_norm_w, da_q_norm_w, da_k_norm_w, ml_conv_w, ml_i_bias, ml_f_bias,
           ml_norm_w, w_br_ret, w_br_da, w_br_ml, w_out, norm2_w, ffn_w_gate, ffn_w_up, ffn_w_down,
           moe_w_router, moe_b_router, moe_w_gate, moe_w_up, moe_w_down):
    B, S, D = x.shape
    T = B * S
    depth = w_in.shape[0]
    cos, sin = _rope_tables(S)
    wb_all, wf_all, wgates_all = _split_w_in(w_in)
    xt = x.reshape(T, D)
    for layer in range(depth):
        wb, wf = wb_all[layer], wf_all[layer]
        nw1 = norm1_w[layer].reshape(1, D)
        assert OFF_MLQK % INPROJ_TN == 0
        zb, zf = _inproj(xt, nw1, wb, wf, ml_conv_w[layer], OFF_MLQK // INPROJ_TN, S, INPROJ_TN)
        o_ret = _retention(zb, cos, sin, ret_norm_w[layer].reshape(1, -1), B, S)
        wq = jnp.tile(da_q_norm_w[layer], 2).reshape(1, LANES)
        wk = jnp.tile(da_k_norm_w[layer], 2).reshape(1, LANES)
        o_da = _dilated_attention(zb, OFF_DA // LANES, wq, wk, B, S)
        gate_bias = jnp.concatenate([ml_i_bias[layer], ml_f_bias[layer],
                                     jnp.zeros((LANES - 2 * ML_HEADS,), F32)]).reshape(1, LANES)
        o_ml = _mlstm(zb, zf, gate_bias, ml_norm_w[layer].reshape(1, -1), B, S,
                      qk_blk=OFF_MLQK // (ML_HEADS * LANES), gate_blk=0)
        merge_args = (xt, o_ret, o_da, o_ml, nw1, wgates_all[layer],
                      w_br_ret[layer].astype(BF16), w_br_da[layer].astype(BF16),
                      w_br_ml[layer].astype(BF16), w_out[layer].astype(BF16),
                      norm2_w[layer].reshape(1, D))
        j = layer // 2
        if layer % 2 == 0:
            xt = _merge_ffn(*merge_args, ffn_w_gate[j].astype(BF16), ffn_w_up[j].astype(BF16),
                            ffn_w_down[j].astype(BF16), ROW_TILE)
        else:
            xt, ha, hb, sel = _merge_router(*merge_args, moe_w_router[j], moe_b_router[j], ROW_TILE)
            xt = _moe(xt, ha, hb, sel, _cast_bf16(moe_w_gate[j], 512), _cast_bf16(moe_w_up[j], 512),
                      _cast_bf16(moe_w_down[j], 2048))
    return xt.reshape(B, S, D)
```

```python
import functools
import math

import jax
import jax.numpy as jnp
from jax import lax
from jax.experimental import pallas as pl
from jax.experimental.pallas import tpu as pltpu
from jax.experimental.pallas import tpu_sc as plsc

F32 = jnp.float32
BF16 = jnp.bfloat16

EPS = 1e-6
D_MODEL = 1024
CHUNK = 128
LANES = 128
MXU_N = 256
ROPE_BASE = 10000.0
RET_HEADS = 4
RET_SEQS = 2
DA_GROUPS = ((1, 16), (4, 4), (16, 1))
DA_DH = 64
DA_UNROLL = 4
DA_MAX_STRIDE = 4
ML_HEADS = 4
ML_DK = 64
ML_SEQS = 2
N_EXPERTS = 8
TOP_K = 2
MOE_TILE = 512
SC_WINDOW = 128
INPROJ_TN = 1280
ROW_TILE = 512
WIDE_ROW_TILE = 1024
OFF_GATES = 2048
OFF_MLQK = OFF_GATES + 3 * D_MODEL
OFF_DA = OFF_MLQK + 3 * ML_HEADS * LANES
VMEM_LIMIT = 56 * 1024 * 1024

NEG = -1e30


def _cparams(*sem):
    return pltpu.CompilerParams(dimension_semantics=sem, vmem_limit_bytes=VMEM_LIMIT)


def _dot(a, b):
    return jnp.dot(a, b, preferred_element_type=F32)


def _dot_nt(a, b):
    return lax.dot_general(a, b, (((1,), (1,)), ((), ())), preferred_element_type=F32)


def _dot_tn(a, b):
    return lax.dot_general(a, b, (((0,), (0,)), ((), ())), preferred_element_type=F32)


def _sigmoid(x):
    return 1.0 / (1.0 + jnp.exp(-x))


def _silu(x):
    return x * _sigmoid(x)


def _inproj_kernel(conv_blk, x_ref, nw_ref, wb_ref, wf_ref, cw_ref, zb_ref, zf_ref, h_ref, cv_ref):
    j = pl.program_id(1)
    tm = x_ref.shape[0]

    @pl.when(j == 0)
    def _():
        rc = 512
        for r in range(tm // rc):
            rows = pl.ds(r * rc, rc)
            x = x_ref[rows, :]
            ms = jnp.mean(x * x, axis=-1, keepdims=True)
            h = (x * lax.rsqrt(ms + EPS) * nw_ref[...]).astype(BF16)
            h_ref[rows, :] = h
            zb_ref[rows, :] = _dot(h, wb_ref[...]).astype(zb_ref.dtype)
            zf_ref[rows, :] = _dot(h, wf_ref[...])

    @pl.when((j > 0) & (j != conv_blk))
    def _():
        zb_ref[...] = _dot(h_ref[...], wb_ref[...]).astype(zb_ref.dtype)

    @pl.when(j == conv_blk)
    def _():
        h = h_ref[...]
        cw = cw_ref.shape[1]
        pad = cv_ref.shape[0] - tm
        cv_ref[0:pad, :] = jnp.zeros((pad, cw), F32)
        cv_ref[pad:pad + tm, :] = _dot(h, wb_ref[:, 0:cw])
        taps = cw_ref.shape[0]
        rb = 256

        def conv(piece, cs):
            cols = slice(cs * LANES, (cs + 1) * LANES)
            r0 = pad + piece * rb - (taps - 1)
            acc = cw_ref[0:1, cols] * cv_ref[pl.ds(r0, rb), cols]
            for i in range(1, taps):
                acc = acc + cw_ref[i:i + 1, cols] * cv_ref[pl.ds(r0 + i, rb), cols]
            scale = ML_DK ** -0.5 if cs >= cw // LANES // 2 else 1.0
            zb_ref[piece * rb:(piece + 1) * rb, cols] = (_silu(acc) * scale).astype(zb_ref.dtype)

        todo = [(piece, cs) for piece in range(tm // rb) for cs in range(cw // LANES)]
        chunks = list(range(cw, zb_ref.shape[1], MXU_N))
        per = -(-len(todo) // len(chunks))
        for n, c0 in enumerate(chunks):
            c1 = min(c0 + MXU_N, zb_ref.shape[1])
            zb_ref[:, c0:c1] = _dot(h, wb_ref[:, c0:c1]).astype(zb_ref.dtype)
            for piece, cs in todo[n * per:(n + 1) * per]:
                conv(piece, cs)


def _inproj(x, nw, wb, wf, conv_w, conv_blk, tm, tnb):
    T, D = x.shape
    return pl.pallas_call(
        functools.partial(_inproj_kernel, conv_blk),
        grid=(T // tm, wb.shape[1] // tnb),
        in_specs=[pl.BlockSpec((tm, D), lambda i, j: (i, 0)),
                  pl.BlockSpec((1, D), lambda i, j: (0, 0)),
                  pl.BlockSpec((D, tnb), lambda i, j: (0, j)),
                  pl.BlockSpec(wf.shape, lambda i, j: (0, 0)),
                  pl.BlockSpec(conv_w.shape, lambda i, j: (0, 0))],
        out_specs=[pl.BlockSpec((tm, tnb), lambda i, j: (i, j)),
                   pl.BlockSpec((tm, wf.shape[1]), lambda i, j: (i, 0))],
        out_shape=[jax.ShapeDtypeStruct((T, wb.shape[1]), BF16),
                   jax.ShapeDtypeStruct((T, wf.shape[1]), F32)],
        scratch_shapes=[pltpu.VMEM((tm, D), BF16), pltpu.VMEM((tm + 8, conv_w.shape[1]), F32)],
        compiler_params=_cparams("parallel", "arbitrary"),
        name="inproj",
    )(x, nw, wb, wf, conv_w)


def _retention_kernel(z_ref, cos_ref, sin_ref, nw_ref, o_ref, dec_ref, st_ref):
    S = z_ref.shape[0] // RET_SEQS
    n_chunks = S // CHUNK
    H = RET_HEADS
    HW = H * LANES
    row = lax.broadcasted_iota(jnp.int32, (CHUNK, CHUNK), 0).astype(F32)
    col = lax.broadcasted_iota(jnp.int32, (CHUNK, CHUNK), 1).astype(F32)
    lgs = [math.log1p(-(2.0 ** (-5.0 - h))) for h in range(H)]
    for h, lg in enumerate(lgs):
        rel = row - col
        dec_ref[h] = jnp.where(rel >= 0, jnp.exp(lg * jnp.maximum(rel, 0.0)), 0.0)
        dec_ref[H + h] = jnp.exp(lg * (row + 1.0))
        dec_ref[2 * H + h] = jnp.exp(lg * (CHUNK - 1.0 - row)) * (LANES ** -0.5)
    st_ref[...] = jnp.zeros(st_ref.shape, F32)
    units = [(sq, h) for sq in range(RET_SEQS) for h in range(H)]

    def body(n, carry):
        r0 = pl.multiple_of(n * CHUNK, CHUNK)
        cos = cos_ref[pl.ds(r0, CHUNK), :]
        sin = sin_ref[pl.ds(r0, CHUNK), :]
        rows_of = [pl.ds(pl.multiple_of(sq * S + n * CHUNK, CHUNK), CHUNK) for sq in range(RET_SEQS)]
        vs, ss, iqs, kvs = [], [], [], []
        for u, (sq, h) in enumerate(units):
            rows = rows_of[sq]
            q = z_ref[rows, h * LANES:(h + 1) * LANES].astype(F32)
            k = z_ref[rows, HW + h * LANES:HW + (h + 1) * LANES].astype(F32)
            v = z_ref[rows, 2 * HW + h * LANES:2 * HW + (h + 1) * LANES]
            q = q * cos + pltpu.roll(q, LANES // 2, 1) * sin
            k = k * cos + pltpu.roll(k, LANES // 2, 1) * sin
            qb = q.astype(BF16)
            kb = (k * (LANES ** -0.5)).astype(BF16)
            kd = (k * dec_ref[2 * H + h]).astype(BF16)
            ss.append(_dot_nt(qb, kb))
            iqs.append(_dot(qb, st_ref[u].astype(BF16)))
            kvs.append(_dot_tn(kd, v))
            vs.append(v)
        ps = [(ss[u] * dec_ref[h]).astype(BF16) for u, (_, h) in enumerate(units)]
        os_ = [_dot(ps[u], vs[u]) + iqs[u] * dec_ref[H + h] for u, (_, h) in enumerate(units)]
        for u, (sq, h) in enumerate(units):
            rows = rows_of[sq]
            st_ref[u] = st_ref[u] * math.exp(lgs[h] * CHUNK) + kvs[u]
            o = os_[u]
            ms = jnp.mean(o * o, axis=-1, keepdims=True)
            o = o * lax.rsqrt(ms + EPS) * nw_ref[:, h * LANES:(h + 1) * LANES]
            g = z_ref[rows, 3 * HW + h * LANES:3 * HW + (h + 1) * LANES].astype(F32)
            o_ref[rows, h * LANES:(h + 1) * LANES] = (o * _silu(g)).astype(o_ref.dtype)
        return carry

    lax.fori_loop(0, n_chunks, body, 0)


def _retention(zb, cos, sin, nw, B, S):
    T = B * S
    W = RET_HEADS * LANES
    R = RET_SEQS * S
    return pl.pallas_call(
        _retention_kernel,
        grid=(B // RET_SEQS,),
        in_specs=[pl.BlockSpec((R, 4 * W), lambda b: (b, 0)),
                  pl.BlockSpec((S, LANES), lambda b: (0, 0)),
                  pl.BlockSpec((S, LANES), lambda b: (0, 0)),
                  pl.BlockSpec((1, W), lambda b: (0, 0))],
        out_specs=pl.BlockSpec((R, W), lambda b: (b, 0)),
        out_shape=jax.ShapeDtypeStruct((T, W), BF16),
        scratch_shapes=[pltpu.VMEM((3 * RET_HEADS, CHUNK, LANES), F32),
                        pltpu.VMEM((RET_SEQS * RET_HEADS, LANES, LANES), F32)],
        compiler_params=_cparams("parallel"),
        name="retention",
    )(zb, cos, sin, nw)


def _da_kernel(*refs):
    z_refs = refs[:9]
    wq_ref, wk_ref, o_ref, qn_ref, kn_ref, v_ref, np_ref, lp_ref, mp_ref = refs[9:18]
    nn_refs, ln_refs, mn_refs = refs[18:21], refs[21:24], refs[24:27]
    stage_refs = refs[27:30]
    S = o_ref.shape[0]
    lane = lax.broadcasted_iota(jnp.int32, (1, LANES), 1)
    lo = lane < DA_DH
    row = lax.broadcasted_iota(jnp.int32, (CHUNK, CHUNK), 0)
    col = lax.broadcasted_iota(jnp.int32, (CHUNK, CHUNK), 1)
    mask_cur = col <= row
    row2 = lax.broadcasted_iota(jnp.int32, (CHUNK, 2 * CHUNK), 0)
    col2 = lax.broadcasted_iota(jnp.int32, (CHUNK, 2 * CHUNK), 1)
    mask_band = (col2 >= row2) & (col2 <= row2 + CHUNK)

    seg = (lax.broadcasted_iota(jnp.int32, (LANES, LANES), 0) // DA_DH
           == lax.broadcasted_iota(jnp.int32, (LANES, LANES), 1) // DA_DH).astype(BF16) * (1.0 / DA_DH)

    def head_norm(x, w):
        x2 = x * x
        hi = x2.astype(BF16)
        lo_part = (x2 - hi.astype(F32)).astype(BF16)
        ms = _dot(hi, seg) + _dot(lo_part, seg)
        return x * lax.rsqrt(ms + EPS) * w

    RB = 256
    for g, (dil, nb) in enumerate(DA_GROUPS):
        L = S // dil
        zq_ref, zk_ref, zv_ref = z_refs[3 * g:3 * g + 3]
        if dil > 1:
            for piece in range(S // RB):
                rows = pl.ds(piece * RB, RB)
                for src_ref, dst_ref in zip((zq_ref, zk_ref, zv_ref), stage_refs):
                    dst_ref[rows, :] = src_ref[rows, :].astype(F32)
            zq_ref, zk_ref, zv_ref = stage_refs
        first = min(dil, DA_MAX_STRIDE)
        if dil > first:
            mid_refs = (nn_refs[g], ln_refs[g], mn_refs[g])
            sub = S // first
            for b in range(first):
                for piece in range(sub // RB):
                    src = pl.ds(b + first * piece * RB, RB, stride=first)
                    dst = pl.ds(b * sub + piece * RB, RB)
                    for src_ref, dst_ref in zip((zq_ref, zk_ref, zv_ref), mid_refs):
                        dst_ref[dst, :] = src_ref[src, :]
            zq_ref, zk_ref, zv_ref = mid_refs
        rest = dil // first

        def source_rows(rho, piece, n_rows):
            if dil == 1:
                return pl.ds(piece * n_rows, n_rows)
            b, a = rho % first, rho // first
            if rest == 1:
                return pl.ds(b + first * piece * n_rows, n_rows, stride=first)
            return pl.ds(b * (S // first) + a + rest * piece * n_rows, n_rows, stride=rest)

        for rho in range(dil):
            for piece in range(max(L // RB, 1)):
                n_rows = min(RB, L)
                src = source_rows(rho, piece, n_rows)
                dst = pl.ds(rho * L + piece * n_rows, n_rows)
                q = zq_ref[src, :].astype(F32)
                k = zk_ref[src, :].astype(F32)
                v = zv_ref[src, :]
                qn_ref[dst, :] = (head_norm(q, wq_ref[...]) * (DA_DH ** -0.5)).astype(BF16)
                kn_ref[dst, :] = head_norm(k, wk_ref[...]).astype(BF16)
                v_ref[dst, :] = v.astype(BF16)

        def batch(r0, chained, first_has_prev):
            chains = []
            for u in range(DA_UNROLL):
                rows = pl.ds(r0 + u * CHUNK, CHUNK)
                if chained and (u > 0 or first_has_prev):
                    keys, mask = pl.ds(r0 + (u - 1) * CHUNK, 2 * CHUNK), mask_band
                else:
                    keys, mask = rows, mask_cur
                k = kn_ref[keys, :]
                v = v_ref[keys, :]
                vaug = jnp.concatenate([v, jnp.ones_like(v)], axis=1)
                q = qn_ref[rows, :]
                zero = jnp.zeros_like(q)
                for qh in (jnp.where(lo, q, zero), jnp.where(lo, zero, q)):
                    chains.append((_dot_nt(qh, k), mask, vaug))
            probs = []
            for s, mask, _ in chains:
                s = jnp.where(mask, s, NEG)
                m = jnp.max(s, axis=-1, keepdims=True)
                probs.append((jnp.exp(s - m).astype(BF16), m))
            accs = [_dot(p, vaug) for (p, _), (_, _, vaug) in zip(probs, chains)]
            nums, dens, maxs = [], [], []
            for u in range(DA_UNROLL):
                a0, a1 = accs[2 * u], accs[2 * u + 1]
                nums.append(jnp.where(lo, a0[:, :LANES], a1[:, :LANES]))
                dens.append(jnp.where(lo, a0[:, LANES:], a1[:, LANES:]))
                maxs.append(jnp.where(lo, probs[2 * u][1], probs[2 * u + 1][1]))
            rows = pl.ds(r0, DA_UNROLL * CHUNK)
            np_ref[rows, :] = jnp.concatenate(nums, axis=0)
            lp_ref[rows, :] = jnp.concatenate(dens, axis=0)
            mp_ref[rows, :] = jnp.concatenate(maxs, axis=0)

        span = DA_UNROLL * CHUNK
        if nb == 1:
            def singles(i, c):
                batch(pl.multiple_of(i * span, span), False, False)
                return c
            lax.fori_loop(0, S // span, singles, 0)
        else:
            def segment(sgi, c, nb=nb):
                base = pl.multiple_of(sgi * (nb * CHUNK), span)
                batch(base, True, False)
                if nb > DA_UNROLL:
                    def inner(n, c2):
                        batch(pl.multiple_of(base + n * span, span), True, True)
                        return c2
                    lax.fori_loop(1, nb // DA_UNROLL, inner, 0)
                return c
            lax.fori_loop(0, dil, segment, 0)

        outs = ((np_ref, nn_refs[g]), (lp_ref, ln_refs[g]), (mp_ref, mn_refs[g]))
        for rho in range(dil):
            for piece in range(max(L // RB, 1)):
                n_rows = min(RB, L)
                dst = source_rows(rho, piece, n_rows)
                src = pl.ds(rho * L + piece * n_rows, n_rows)
                for k_out, (perm_ref, nat_ref) in enumerate(outs):
                    (stage_refs[k_out] if rest > 1 else nat_ref)[dst, :] = perm_ref[src, :]
        if rest > 1:
            sub = S // first
            for b in range(first):
                for piece in range(sub // RB):
                    dst = pl.ds(b + first * piece * RB, RB, stride=first)
                    src = pl.ds(b * sub + piece * RB, RB)
                    for k_out, (_, nat_ref) in enumerate(outs):
                        nat_ref[dst, :] = stage_refs[k_out][src, :]

    for piece in range(S // RB):
        rows = pl.ds(piece * RB, RB)
        m = jnp.maximum(jnp.maximum(mn_refs[0][rows, :], mn_refs[1][rows, :]), mn_refs[2][rows, :])
        num = jnp.zeros((RB, LANES), F32)
        den = jnp.zeros((RB, LANES), F32)
        for g in range(3):
            e = jnp.exp(mn_refs[g][rows, :] - m)
            num = num + e * nn_refs[g][rows, :]
            den = den + e * ln_refs[g][rows, :]
        o_ref[rows, :] = (num / den).astype(o_ref.dtype)


def _dilated_attention(zb, slab0, wq, wk, B, S):
    T = B * S
    slab = lambda k: pl.BlockSpec((S, LANES), lambda b, p: (b, slab0 + 9 * p + k))
    return pl.pallas_call(
        _da_kernel,
        grid=(B, 2),
        in_specs=[slab(k) for k in range(9)] + [
                  pl.BlockSpec((1, LANES), lambda b, p: (0, 0)),
                  pl.BlockSpec((1, LANES), lambda b, p: (0, 0))],
        out_specs=pl.BlockSpec((S, LANES), lambda b, p: (b, p)),
        out_shape=jax.ShapeDtypeStruct((T, 2 * LANES), BF16),
        scratch_shapes=[pltpu.VMEM((S, LANES), BF16), pltpu.VMEM((S, LANES), BF16),
                        pltpu.VMEM((S, LANES), BF16),
                        pltpu.VMEM((S, LANES), F32), pltpu.VMEM((S, LANES), F32),
                        pltpu.VMEM((S, LANES), F32)] + [pltpu.VMEM((S, LANES), F32)] * 12,
        compiler_params=_cparams("parallel", "arbitrary"),
        name="dilated_attention",
    )(*([zb] * 9), wq, wk)


def _mlstm_kernel(qk_ref, v_ref, og_ref, gate_ref, gb_ref, nw_ref, o_ref, st_ref, e_ref, et_ref):
    S = qk_ref.shape[0] // ML_SEQS
    n_chunks = S // CHUNK
    W = qk_ref.shape[1]
    lane = lax.broadcasted_iota(jnp.int32, (1, LANES), 1)
    lo = lane < ML_DK
    row = lax.broadcasted_iota(jnp.int32, (CHUNK, CHUNK), 0)
    col = lax.broadcasted_iota(jnp.int32, (CHUNK, CHUNK), 1)
    causal = col <= row
    tril = causal.astype(F32)
    srow = lax.broadcasted_iota(jnp.int32, (LANES, 1), 0) < ML_DK
    st_ref[...] = jnp.zeros(st_ref.shape, F32)

    is_f = (lane >= ML_HEADS) & (lane < 2 * ML_HEADS)
    for n in range(ML_SEQS * n_chunks):
        rows = pl.ds(n * CHUNK, CHUNK)
        gp = gate_ref[rows, :] + gb_ref[...]
        logf = jnp.minimum(gp, 0.0) - jnp.log(1.0 + jnp.exp(-jnp.abs(gp)))
        cum = jnp.dot(tril, jnp.where(is_f, logf, 0.0), preferred_element_type=F32,
                      precision=lax.Precision.HIGHEST)
        e = jnp.where(lane < ML_HEADS, gp, cum)
        e_ref[rows, :] = e
        et_ref[n] = e.T[0:2 * ML_HEADS, :]

    units = [(sq, h) for sq in range(ML_SEQS) for h in range(ML_HEADS)]

    def chunk(n, mms):
        rows_of = [pl.ds(pl.multiple_of(sq * S + n * CHUNK, CHUNK), CHUNK) for sq in range(ML_SEQS)]
        es = [e_ref[rows_of[sq], :] for sq in range(ML_SEQS)]
        ets = [et_ref[sq * n_chunks + n] for sq in range(ML_SEQS)]
        khs, vaugs, s_raws, iqs = [], [], [], []
        for sq in range(ML_SEQS):
            rows = rows_of[sq]
            for pair in range(2):
                q2 = qk_ref[rows, pair * LANES:(pair + 1) * LANES]
                k2 = qk_ref[rows, W // 2 + pair * LANES:W // 2 + (pair + 1) * LANES]
                zero = jnp.zeros_like(q2)
                stb = st_ref[2 * sq + pair].astype(BF16)
                for hs in range(2):
                    h = 2 * pair + hs
                    qh = jnp.where(lo, q2, zero) if hs == 0 else jnp.where(lo, zero, q2)
                    kh = jnp.where(lo, k2, zero) if hs == 0 else jnp.where(lo, zero, k2)
                    vh = v_ref[rows, h * LANES:(h + 1) * LANES]
                    s_raws.append(_dot_nt(qh, k2))
                    iqs.append(_dot(qh, stb))
                    khs.append(kh)
                    vaugs.append(jnp.concatenate([vh, jnp.ones_like(vh)], axis=-1))
        ps, wis, ms_, kws, wcs, new_mms = [], [], [], [], [], []
        for u, (sq, h) in enumerate(units):
            e, et = es[sq], ets[sq]
            mm = mms[u]
            i_b = jnp.broadcast_to(e[:, h:h + 1], (CHUNK, LANES))
            a_b = jnp.broadcast_to(e[:, ML_HEADS + h:ML_HEADS + h + 1], (CHUNK, LANES))
            c_row = et[h:h + 1, :] - et[ML_HEADS + h:ML_HEADS + h + 1, :]
            dmat = jnp.where(causal, a_b + c_row, NEG)
            inter = a_b + mm
            m = jnp.maximum(inter, jnp.max(dmat, axis=-1, keepdims=True))
            ps.append((s_raws[u] * jnp.exp(dmat - m)).astype(BF16))
            wis.append(jnp.exp(inter - m))
            ms_.append(m)
            m_new = m[CHUNK - 1:CHUNK, :]
            a_last = a_b[CHUNK - 1:CHUNK, :]
            wk = jnp.exp(a_last - a_b + i_b - m_new)
            kws.append((khs[u].astype(F32) * wk).astype(BF16))
            wc = jnp.exp(a_last + mm - m_new)
            wcs.append(jnp.concatenate([wc, wc], axis=1))
            new_mms.append(m_new)
        accs = [_dot(ps[u], vaugs[u]) + jnp.concatenate([wis[u], wis[u]], axis=1) * iqs[u]
                for u in range(len(units))]
        upds = [_dot_tn(kws[u], vaugs[u]) for u in range(len(units))]
        for sp in range(2 * ML_SEQS):
            u0, u1 = 2 * sp, 2 * sp + 1
            st_ref[sp] = jnp.where(srow, wcs[u0], wcs[u1]) * st_ref[sp] + (upds[u0] + upds[u1])
        for u, (sq, h) in enumerate(units):
            num = accs[u][:, :LANES]
            den = accs[u][:, LANES:]
            hv = num / jnp.maximum(jnp.abs(den), jnp.exp(-ms_[u]))
            var = jnp.mean(hv * hv, axis=-1, keepdims=True)
            hv = hv * lax.rsqrt(var + EPS) * nw_ref[:, h * LANES:(h + 1) * LANES]
            og = og_ref[rows_of[sq], h * LANES:(h + 1) * LANES].astype(F32)
            o_ref[rows_of[sq], h * LANES:(h + 1) * LANES] = (hv * _sigmoid(og)).astype(o_ref.dtype)
        return tuple(new_mms)

    lax.fori_loop(0, n_chunks, chunk, tuple(jnp.zeros((1, LANES), F32) for _ in units))


def _mlstm(zb, zf, gate_bias, nw, B, S, qk_blk, gate_blk):
    T = B * S
    W = ML_HEADS * LANES
    R = ML_SEQS * S
    return pl.pallas_call(
        _mlstm_kernel,
        grid=(B // ML_SEQS,),
        in_specs=[pl.BlockSpec((R, W), lambda b: (b, qk_blk)),
                  pl.BlockSpec((R, W), lambda b: (b, qk_blk + 1)),
                  pl.BlockSpec((R, W), lambda b: (b, qk_blk + 2)),
                  pl.BlockSpec((R, LANES), lambda b: (b, gate_blk)),
                  pl.BlockSpec((1, LANES), lambda b: (0, 0)),
                  pl.BlockSpec((1, W), lambda b: (0, 0))],
        out_specs=pl.BlockSpec((R, W), lambda b: (b, 0)),
        out_shape=jax.ShapeDtypeStruct((T, W), BF16),
        scratch_shapes=[pltpu.VMEM((2 * ML_SEQS, LANES, 2 * LANES), F32),
                        pltpu.VMEM((R, LANES), F32),
                        pltpu.VMEM((R // CHUNK, 2 * ML_HEADS, LANES), F32)],
        compiler_params=_cparams("parallel"),
        name="mlstm",
    )(zb, zb, zb, zf, gate_bias, nw)


def _merge_value(x_ref, oret_ref, oda_ref, oml_ref, gr_ref, gd_ref, gm_ref,
                 wr_ref, wd_ref, wm_ref, wo_ref):
    y = _sigmoid(gr_ref[...].astype(F32)) * _dot(oret_ref[...], wr_ref[...])
    y = y + _sigmoid(gd_ref[...].astype(F32)) * _dot(oda_ref[...], wd_ref[...])
    y = y + _sigmoid(gm_ref[...].astype(F32)) * _dot(oml_ref[...], wm_ref[...])
    return x_ref[...] + _dot(y.astype(BF16), wo_ref[...])


def _merge_specs(x, o_ret, o_da, o_ml, g_blk, tm):
    D = x.shape[1]
    return [pl.BlockSpec((tm, D), lambda i: (i, 0)),
            pl.BlockSpec((tm, o_ret.shape[1]), lambda i: (i, 0)),
            pl.BlockSpec((tm, o_da.shape[1]), lambda i: (i, 0)),
            pl.BlockSpec((tm, o_ml.shape[1]), lambda i: (i, 0)),
            pl.BlockSpec((tm, D), lambda i: (i, g_blk)),
            pl.BlockSpec((tm, D), lambda i: (i, g_blk + 1)),
            pl.BlockSpec((tm, D), lambda i: (i, g_blk + 2))]


def _resident(a):
    return pl.BlockSpec(a.shape, lambda i: (0,) * a.ndim, pipeline_mode=pl.Buffered(1))


def _swiglu_block(h, wg_ref, wu_ref, wd_ref, a_ref):
    tf = wg_ref.shape[1]
    c0 = 0
    while c0 < tf:
        w = min(MXU_N, tf - c0)
        g = _dot(h, wg_ref[:, c0:c0 + w].astype(BF16))
        u = _dot(h, wu_ref[:, c0:c0 + w].astype(BF16))
        a_ref[:, c0:c0 + w] = (_silu(g) * u).astype(BF16)
        c0 += w
    return _dot(a_ref[...], wd_ref[...].astype(BF16))


def _merge_ffn_kernel(*refs):
    merge_refs, (nw_ref, wg_ref, wu_ref, wdn_ref, o_ref, a_ref) = refs[:11], refs[11:]
    x = _merge_value(*merge_refs)
    ms = jnp.mean(x * x, axis=-1, keepdims=True)
    h = (x * lax.rsqrt(ms + EPS) * nw_ref[...]).astype(BF16)
    o_ref[...] = x + _swiglu_block(h, wg_ref, wu_ref, wdn_ref, a_ref)


def _merge_ffn(x, o_ret, o_da, o_ml, zb, g_blk, wr, wd, wm, wo, nw, wg, wu, wdn, tm):
    T, D = x.shape
    weights = (wr, wd, wm, wo, nw, wg, wu, wdn)
    return pl.pallas_call(
        _merge_ffn_kernel,
        grid=(T // tm,),
        in_specs=_merge_specs(x, o_ret, o_da, o_ml, g_blk, tm) + [_resident(a) for a in weights],
        out_specs=pl.BlockSpec((tm, D), lambda i: (i, 0)),
        out_shape=jax.ShapeDtypeStruct((T, D), F32),
        scratch_shapes=[pltpu.VMEM((tm, wg.shape[1]), BF16)],
        compiler_params=_cparams("parallel"),
        name="merge_ffn",
    )(x, o_ret, o_da, o_ml, zb, zb, zb, *weights)


HI16 = 0xFFFF0000


def _pack_bf16_pairs(a, b):
    ab = pltpu.bitcast(a.astype(BF16).astype(F32), jnp.uint32)
    bb = pltpu.bitcast(b.astype(BF16).astype(F32), jnp.uint32)
    return (ab >> 16) | (bb & jnp.uint32(HI16))


def _unpack_bf16_pairs(p):
    lo = pltpu.bitcast(p << 16, F32)
    hi = pltpu.bitcast(p & jnp.uint32(HI16), F32)
    return lo, hi


def _merge_router_kernel(*refs):
    merge_refs, (nw_ref, wr_ref, br_ref, xo_ref, ha_ref, hb_ref, sel_ref, cnt_ref) = refs[:11], refs[11:]
    x = _merge_value(*merge_refs)
    xo_ref[...] = x
    ms = jnp.mean(x * x, axis=-1, keepdims=True)
    h = x * lax.rsqrt(ms + EPS) * nw_ref[...]
    Q = h.shape[1] // 4
    ha_ref[...] = _pack_bf16_pairs(h[:, 0:Q], h[:, Q:2 * Q])
    hb_ref[...] = _pack_bf16_pairs(h[:, 2 * Q:3 * Q], h[:, 3 * Q:4 * Q])
    w = wr_ref[...]
    h_hi = h.astype(BF16)
    h_lo = (h - h_hi.astype(F32)).astype(BF16)
    w_hi = w.astype(BF16)
    w_lo = (w - w_hi.astype(F32)).astype(BF16)
    logits = _dot(h_hi, w_hi) + _dot(h_lo, w_hi) + _dot(h_hi, w_lo) + br_ref[...]
    lane = lax.broadcasted_iota(jnp.int32, logits.shape, 1).astype(F32)
    logits = jnp.where(lane < N_EXPERTS, logits, NEG)
    m1 = jnp.max(logits, axis=-1, keepdims=True)
    i1 = jnp.min(jnp.where(logits == m1, lane, float(LANES)), axis=-1, keepdims=True)
    rest = jnp.where(lane == i1, NEG, logits)
    m2 = jnp.max(rest, axis=-1, keepdims=True)
    i2 = jnp.min(jnp.where(rest == m2, lane, float(LANES)), axis=-1, keepdims=True)
    e2 = jnp.exp(m2 - m1)
    p1 = 1.0 / (1.0 + e2)
    p2 = e2 / (1.0 + e2)
    sel_ref[...] = jnp.where(lane == 0.0, i1, jnp.where(lane == 1.0, i2,
                             jnp.where(lane == 2.0, p1, jnp.where(lane == 3.0, p2, 0.0))))
    chosen = jnp.where((lane == i1) | (lane == i2), 1.0, 0.0)
    cnt_ref[...] = jnp.broadcast_to(jnp.sum(chosen, axis=0, keepdims=True), cnt_ref.shape)


def _merge_router(x, o_ret, o_da, o_ml, zb, g_blk, wr, wd, wm, wo, nw, w_router, b_router, tm):
    T, D = x.shape
    Q = D // 4
    wrt = jnp.pad(w_router, ((0, 0), (0, LANES - N_EXPERTS)))
    brt = jnp.pad(b_router, (0, LANES - N_EXPERTS)).reshape(1, LANES)
    weights = (wr, wd, wm, wo, nw, wrt, brt)
    return pl.pallas_call(
        _merge_router_kernel,
        grid=(T // tm,),
        in_specs=_merge_specs(x, o_ret, o_da, o_ml, g_blk, tm) + [_resident(a) for a in weights],
        out_specs=[pl.BlockSpec((tm, D), lambda i: (i, 0)),
                   pl.BlockSpec((tm, Q), lambda i: (i, 0)),
                   pl.BlockSpec((tm, Q), lambda i: (i, 0)),
                   pl.BlockSpec((tm, LANES), lambda i: (i, 0)),
                   pl.BlockSpec((None, 8, LANES), lambda i: (i, 0, 0))],
        out_shape=[jax.ShapeDtypeStruct((T, D), F32),
                   jax.ShapeDtypeStruct((T, Q), jnp.uint32), jax.ShapeDtypeStruct((T, Q), jnp.uint32),
                   jax.ShapeDtypeStruct((T, LANES), F32),
                   jax.ShapeDtypeStruct((T // tm, 8, LANES), F32)],
        compiler_params=_cparams("parallel"),
        name="merge_router",
    )(x, o_ret, o_da, o_ml, zb, zb, zb, *weights)


def _moe_rank_kernel(sel_ref, tcnt_ref, pos_ref, meta_ref, cnt_ref, offs_ref, carry_ref, before_ref):
    i = pl.program_id(0)
    tm = sel_ref.shape[0]
    lane = lax.broadcasted_iota(jnp.int32, (tm, LANES), 1).astype(F32)
    lane1 = lax.broadcasted_iota(jnp.int32, (1, LANES), 1).astype(F32)
    sel = sel_ref[...]
    i1 = sel[:, 0:1]
    i2 = sel[:, 1:2]
    onehot = jnp.where((lane == i1) | (lane == i2), 1.0, 0.0)
    colsum = jnp.sum(onehot, axis=0, keepdims=True)

    def padded_counts():
        return jnp.floor((cnt_ref[...] + (MOE_TILE - 1.0)) * (1.0 / MOE_TILE)) * MOE_TILE

    @pl.when(i == 0)
    def _():
        cnt_ref[...] = jnp.sum(tcnt_ref[...], axis=0)
        k = lax.broadcasted_iota(jnp.int32, (LANES, LANES), 0)
        e = lax.broadcasted_iota(jnp.int32, (LANES, LANES), 1)
        upper = (k < e).astype(F32)
        offs_ref[...] = jnp.dot(padded_counts(), upper, preferred_element_type=F32,
                                precision=lax.Precision.HIGHEST)
        carry_ref[...] = jnp.zeros(carry_ref.shape, F32)
        r = lax.broadcasted_iota(jnp.int32, (tm, tm), 0)
        c = lax.broadcasted_iota(jnp.int32, (tm, tm), 1)
        before_ref[...] = (c < r).astype(BF16)

    rank = _dot(before_ref[...], onehot.astype(BF16)) + carry_ref[0:1, :]
    row = rank + offs_ref[0:1, :]
    pos1 = jnp.sum(jnp.where(lane == i1, row, 0.0), axis=-1, keepdims=True)
    pos2 = jnp.sum(jnp.where(lane == i2, row, 0.0), axis=-1, keepdims=True)
    both = jnp.where(lane == 0.0, pos1, jnp.where(lane == 1.0, pos2, 0.0))
    pos_ref[...] = both.T[0:8, :].astype(jnp.int32)
    carry_ref[...] += colsum

    @pl.when(i == pl.num_programs(0) - 1)
    def _():
        nrow = meta_ref.shape[0]
        padded = padded_counts()[0:1, :]
        offs = offs_ref[0:1, :]
        ends = offs + padded
        start = lax.broadcasted_iota(jnp.int32, (nrow, 1), 0).astype(F32) * MOE_TILE
        is_e = lane1 < N_EXPERTS
        te = jnp.sum(jnp.where(is_e & (ends <= start), 1.0, 0.0), axis=-1, keepdims=True)
        te = jnp.minimum(te, N_EXPERTS - 1.0)
        valid_end = jnp.sum(jnp.where(lane1 == te, offs + cnt_ref[0:1, :], 0.0), axis=-1, keepdims=True)
        nv = jnp.clip(valid_end - start, 0.0, MOE_TILE)
        total = jnp.sum(jnp.where(is_e, padded, 0.0), axis=-1, keepdims=True)
        active = jnp.where(start < total, 1.0, 0.0)
        meta_ref[...] = jnp.where(lane1 == 0.0, te, jnp.where(lane1 == 1.0, nv,
                                  jnp.where(lane1 == 2.0, active, 0.0)))


def _moe_rank(sel, tcnt, tm, meta_rows):
    T = sel.shape[0]
    nt = T // tm
    return pl.pallas_call(
        _moe_rank_kernel,
        grid=(nt,),
        in_specs=[pl.BlockSpec((tm, LANES), lambda i: (i, 0)),
                  pl.BlockSpec(tcnt.shape, lambda i: (0, 0, 0))],
        out_specs=[pl.BlockSpec((None, 8, tm), lambda i: (i, 0, 0)),
                   pl.BlockSpec((meta_rows, LANES), lambda i: (0, 0))],
        out_shape=[jax.ShapeDtypeStruct((nt, 8, tm), jnp.int32),
                   jax.ShapeDtypeStruct((meta_rows, LANES), F32)],
        scratch_shapes=[pltpu.VMEM((8, LANES), F32), pltpu.VMEM((8, LANES), F32),
                        pltpu.VMEM((8, LANES), F32), pltpu.VMEM((tm, tm), BF16)],
        compiler_params=_cparams("arbitrary"),
        name="moe_rank",
    )(sel, tcnt)


def _sc_mesh():
    return plsc.VectorSubcoreMesh(core_axis_name="core", subcore_axis_name="subcore")


def _sc_scatter_rows(x, idx, n_rows):
    n_idx = idx.shape[1]
    nt = x.shape[0] // SC_WINDOW
    width = x.shape[1]

    @functools.partial(pl.kernel, out_type=jax.ShapeDtypeStruct((n_rows, width), x.dtype),
                       mesh=_sc_mesh())
    def scatter_kernel(x_hbm, i_hbm, o_hbm):
        def body(x_vmem, i_vmem):
            pltpu.sync_copy(x_vmem, o_hbm.at[i_vmem.at[0]])

        pltpu.emit_pipeline(
            body, grid=(n_idx // SC_WINDOW,),
            in_specs=[pl.BlockSpec((SC_WINDOW, width), lambda i: (i % nt, 0)),
                      pl.BlockSpec((1, SC_WINDOW), lambda i: (0, i))],
            out_specs=[],
            core_axis_name=("core", "subcore"), dimension_semantics=(pltpu.PARALLEL,),
        )(x_hbm, i_hbm)

    return scatter_kernel(x, idx)


def _sc_gather_rows(x, idx):
    n_idx = idx.shape[1]
    width = x.shape[1]

    @functools.partial(pl.kernel, out_type=jax.ShapeDtypeStruct((n_idx, width), x.dtype),
                       mesh=_sc_mesh())
    def gather_kernel(x_hbm, i_hbm, o_hbm):
        def body(i_vmem, o_vmem):
            pltpu.sync_copy(x_hbm.at[i_vmem.at[0]], o_vmem)

        pltpu.emit_pipeline(
            body, grid=(n_idx // SC_WINDOW,),
            in_specs=[pl.BlockSpec((1, SC_WINDOW), lambda i: (0, i))],
            out_specs=[pl.BlockSpec((SC_WINDOW, width), lambda i: (i, 0))],
            core_axis_name=("core", "subcore"), dimension_semantics=(pltpu.PARALLEL,),
        )(i_hbm, o_hbm)

    return gather_kernel(x, idx)


def _moe_group_kernel(te_ref, nv_ref, na_ref, xa_ref, xb_ref, wg_ref, wu_ref, wd_ref,
                      ya_ref, yb_ref, h_ref, a_ref):
    j = pl.program_id(0)

    @pl.when(j < na_ref[0])
    def _():
        tm = h_ref.shape[0]
        Q = xa_ref.shape[1]
        valid = lax.broadcasted_iota(jnp.int32, (tm, 1), 0) < nv_ref[j]
        for src, c0 in ((xa_ref, 0), (xb_ref, 2 * Q)):
            lo, hi = _unpack_bf16_pairs(src[...])
            h_ref[:, c0:c0 + Q] = jnp.where(valid, lo, 0.0).astype(BF16)
            h_ref[:, c0 + Q:c0 + 2 * Q] = jnp.where(valid, hi, 0.0).astype(BF16)
        y = _swiglu_block(h_ref[...], wg_ref, wu_ref, wd_ref, a_ref)
        ya_ref[...] = _pack_bf16_pairs(y[:, 0:Q], y[:, Q:2 * Q])
        yb_ref[...] = _pack_bf16_pairs(y[:, 2 * Q:3 * Q], y[:, 3 * Q:4 * Q])


def _moe_group(te, nv, na, xa, xb, wg, wu, wd):
    R, Q = xa.shape
    E, D, F = wg.shape
    tile = lambda j, te, nv, na: (jnp.minimum(j, na[0] - 1), 0)
    expert = lambda j, te, nv, na: (te[j], 0, 0)
    once = pl.Buffered(1)
    grid_spec = pltpu.PrefetchScalarGridSpec(
        num_scalar_prefetch=3,
        grid=(R // MOE_TILE,),
        in_specs=[pl.BlockSpec((MOE_TILE, Q), tile),
                  pl.BlockSpec((MOE_TILE, Q), tile),
                  pl.BlockSpec((None, D, F), expert),
                  pl.BlockSpec((None, D, F), expert),
                  pl.BlockSpec((None, F, D), expert, pipeline_mode=once)],
        out_specs=[pl.BlockSpec((MOE_TILE, Q), tile), pl.BlockSpec((MOE_TILE, Q), tile)],
        scratch_shapes=[pltpu.VMEM((MOE_TILE, D), BF16), pltpu.VMEM((MOE_TILE, F), BF16)],
    )
    return pl.pallas_call(
        _moe_group_kernel,
        grid_spec=grid_spec,
        out_shape=[jax.ShapeDtypeStruct((R, Q), jnp.uint32), jax.ShapeDtypeStruct((R, Q), jnp.uint32)],
        compiler_params=_cparams("arbitrary"),
        name="moe_experts",
    )(te, nv, na, xa, xb, wg, wu, wd)


def _moe_combine_kernel(x_ref, sel_ref, g1_ref, g2_ref, *rest):
    o_ref = rest[-1]
    sel = sel_ref[...]
    p1 = sel[:, 2:3]
    p2 = sel[:, 3:4]
    Q = g1_ref.shape[1]
    lo1, hi1 = _unpack_bf16_pairs(g1_ref[...])
    lo2, hi2 = _unpack_bf16_pairs(g2_ref[...])
    o_ref[:, 0:Q] = x_ref[:, 0:Q] + (p1 * lo1 + p2 * lo2)
    o_ref[:, Q:2 * Q] = x_ref[:, Q:2 * Q] + (p1 * hi1 + p2 * hi2)


def _moe_combine_half(x, sel, g, half, partial_out, tm):
    T, D = x.shape
    Q = g.shape[1]
    nt = T // tm
    in_specs = [pl.BlockSpec((tm, 2 * Q), lambda i: (i, half)),
                pl.BlockSpec((tm, LANES), lambda i: (i, 0)),
                pl.BlockSpec((tm, Q), lambda i: (i, 0)),
                pl.BlockSpec((tm, Q), lambda i: (i + nt, 0))]
    args = [x, sel, g, g]
    aliases = {}
    if partial_out is not None:
        in_specs.append(pl.BlockSpec(memory_space=pl.ANY))
        args.append(partial_out)
        aliases = {4: 0}
    return pl.pallas_call(
        _moe_combine_kernel,
        grid=(nt,),
        in_specs=in_specs,
        out_specs=pl.BlockSpec((tm, 2 * Q), lambda i: (i, half)),
        out_shape=jax.ShapeDtypeStruct((T, D), F32),
        input_output_aliases=aliases,
        compiler_params=_cparams("parallel"),
        name="moe_combine",
    )(*args)


def _cast_kernel(x_ref, o_ref):
    o_ref[...] = x_ref[...].astype(o_ref.dtype)


def _cast_bf16(w, rows):
    cols = w.shape[-1]
    w2 = w.reshape(-1, cols)
    out = pl.pallas_call(
        _cast_kernel,
        grid=(w2.shape[0] // rows,),
        in_specs=[pl.BlockSpec((rows, cols), lambda i: (i, 0))],
        out_specs=pl.BlockSpec((rows, cols), lambda i: (i, 0)),
        out_shape=jax.ShapeDtypeStruct(w2.shape, BF16),
        compiler_params=_cparams("parallel"),
        name="cast_bf16",
    )(w2)
    return out.reshape(w.shape)


def _moe(x, ha, hb, sel, tcnt, wg, wu, wd):
    T, D = x.shape
    n_rows = TOP_K * T + N_EXPERTS * MOE_TILE
    n_tiles = n_rows // MOE_TILE
    pos, meta = _moe_rank(sel, tcnt, WIDE_ROW_TILE, 256)
    idx = jnp.concatenate([pos[:, 0, :].reshape(1, T), pos[:, 1, :].reshape(1, T)], axis=1)
    te = meta[:n_tiles, 0].astype(jnp.int32)
    nv = meta[:n_tiles, 1].astype(jnp.int32)
    na = jnp.sum(meta[:n_tiles, 2]).astype(jnp.int32).reshape(1)
    te = jnp.where(jnp.arange(n_tiles) < na[0], te, te[na[0] - 1])
    xa = _sc_scatter_rows(ha, idx, n_rows)
    xb = _sc_scatter_rows(hb, idx, n_rows)
    ya, yb = _moe_group(te, nv, na, xa, xb, wg, wu, wd)
    ga = _sc_gather_rows(ya, idx)
    gb = _sc_gather_rows(yb, idx)
    out = _moe_combine_half(x, sel, ga, 0, None, WIDE_ROW_TILE)
    return _moe_combine_half(x, sel, gb, 1, out, WIDE_ROW_TILE)


def _split_w_in(w_in):
    sizes = (512, 512, 512, 512, 768, 768, 768, 256, 256, 512, 512, 4, 4, 1024, 1024, 1024)
    offs = [0]
    for s in sizes:
        offs.append(offs[-1] + s)
    part = lambda i: w_in[..., offs[i]:offs[i + 1]]
    rq, rk, rv, rg, dq, dk, dv, mq, mk, mv, mo, mi, mf, g_ret, g_da, g_ml = (part(i) for i in range(16))
    cols = []
    for p in range(2):
        for g in range(3):
            for t in (dq, dk, dv):
                cols.append(t[..., g * 256 + p * 128:g * 256 + (p + 1) * 128])
    wb = jnp.concatenate([rq, rk, rv, rg, g_ret, g_da, g_ml, mq, mk, mv, mo] + cols,
                         axis=-1).astype(BF16)
    pad = jnp.zeros(w_in.shape[:-1] + (LANES - 8,), w_in.dtype)
    wf = jnp.concatenate([mi, mf, pad], axis=-1).astype(BF16)
    return wb, wf


def _rope_tables(S):
    half = LANES // 2
    inv = jnp.power(ROPE_BASE, -jnp.arange(half, dtype=F32) / half)
    ang = jnp.arange(S, dtype=F32)[:, None] * inv[None, :]
    cos = jnp.cos(ang)
    sin = jnp.sin(ang)
    return jnp.concatenate([cos, cos], axis=1), jnp.concatenate([-sin, sin], axis=1)


def kernel(x, norm1_w, w_in, ret_norm_w, da_q_norm_w, da_k_norm_w, ml_conv_w, ml_i_bias, ml_f_bias,
           ml_norm_w, w_br_ret, w_br_da, w_br_ml, w_out, norm2_w, ffn_w_gate, ffn_w_up, ffn_w_down,
           moe_w_router, moe_b_router, moe_w_gate, moe_w_up, moe_w_down):
    B, S, D = x.shape
    T = B * S
    depth = w_in.shape[0]
    cos, sin = _rope_tables(S)
    wb_all, wf_all = _split_w_in(w_in)
    xt = x.reshape(T, D)
    for layer in range(depth):
        wb, wf = wb_all[layer], wf_all[layer]
        nw1 = norm1_w[layer].reshape(1, D)
        assert OFF_MLQK % INPROJ_TN == 0
        zb, zf = _inproj(xt, nw1, wb, wf, ml_conv_w[layer], OFF_MLQK // INPROJ_TN, S, INPROJ_TN)
        o_ret = _retention(zb, cos, sin, ret_norm_w[layer].reshape(1, -1), B, S)
        wq = jnp.tile(da_q_norm_w[layer], 2).reshape(1, LANES)
        wk = jnp.tile(da_k_norm_w[layer], 2).reshape(1, LANES)
        o_da = _dilated_attention(zb, OFF_DA // LANES, wq, wk, B, S)
        gate_bias = jnp.concatenate([ml_i_bias[layer], ml_f_bias[layer],
                                     jnp.zeros((LANES - 2 * ML_HEADS,), F32)]).reshape(1, LANES)
        o_ml = _mlstm(zb, zf, gate_bias, ml_norm_w[layer].reshape(1, -1), B, S,
                      qk_blk=OFF_MLQK // (ML_HEADS * LANES), gate_blk=0)
        merge_args = (xt, o_ret, o_da, o_ml, zb, OFF_GATES // D_MODEL,
                      w_br_ret[layer].astype(BF16), w_br_da[layer].astype(BF16),
                      w_br_ml[layer].astype(BF16), w_out[layer].astype(BF16),
                      norm2_w[layer].reshape(1, D))
        j = layer // 2
        if layer % 2 == 0:
            xt = _merge_ffn(*merge_args, ffn_w_gate[j].astype(BF16), ffn_w_up[j].astype(BF16),
                            ffn_w_down[j].astype(BF16), ROW_TILE)
        else:
            xt, ha, hb, sel, tcnt = _merge_router(*merge_args, moe_w_router[j], moe_b_router[j],
                                                  ROW_TILE)
            xt = _moe(xt, ha, hb, sel, tcnt, _cast_bf16(moe_w_gate[j], 512), _cast_bf16(moe_w_up[j], 512),
                      _cast_bf16(moe_w_down[j], 2048))
    return xt.reshape(B, S, D)
```

```python
import functools
import math

import jax
import jax.numpy as jnp
from jax import lax
from jax.experimental import pallas as pl
from jax.experimental.pallas import tpu as pltpu
from jax.experimental.pallas import tpu_sc as plsc

F32 = jnp.float32
BF16 = jnp.bfloat16

EPS = 1e-6
D_MODEL = 1024
CHUNK = 128
LANES = 128
MXU_N = 256
ROPE_BASE = 10000.0
RET_HEADS = 4
RET_SEQS = 2
DA_GROUPS = ((1, 16), (4, 4), (16, 1))
DA_DH = 64
DA_UNROLL = 4
DA_MAX_STRIDE = 4
ML_HEADS = 4
ML_DK = 64
ML_SEQS = 2
N_EXPERTS = 8
TOP_K = 2
MOE_TILE = 512
SC_WINDOW = 128
INPROJ_TN = 1280
ROW_TILE = 512
WIDE_ROW_TILE = 1024
OFF_GATES = 2048
OFF_MLQK = OFF_GATES + 3 * D_MODEL
OFF_DA = OFF_MLQK + 3 * ML_HEADS * LANES
VMEM_LIMIT = 56 * 1024 * 1024

NEG = -1e30


def _cparams(*sem):
    return pltpu.CompilerParams(dimension_semantics=sem, vmem_limit_bytes=VMEM_LIMIT)


def _dot(a, b):
    return jnp.dot(a, b, preferred_element_type=F32)


def _dot_nt(a, b):
    return lax.dot_general(a, b, (((1,), (1,)), ((), ())), preferred_element_type=F32)


def _dot_tn(a, b):
    return lax.dot_general(a, b, (((0,), (0,)), ((), ())), preferred_element_type=F32)


def _sigmoid(x):
    return 1.0 / (1.0 + jnp.exp(-x))


def _silu(x):
    return x * _sigmoid(x)


def _inproj_kernel(conv_blk, x_ref, nw_ref, wb_ref, wf_ref, cw_ref, zb_ref, zf_ref, h_ref, cv_ref):
    j = pl.program_id(1)
    tm = x_ref.shape[0]

    @pl.when(j == 0)
    def _():
        rc = 512
        for r in range(tm // rc):
            rows = pl.ds(r * rc, rc)
            x = x_ref[rows, :]
            ms = jnp.mean(x * x, axis=-1, keepdims=True)
            h = (x * lax.rsqrt(ms + EPS) * nw_ref[...]).astype(BF16)
            h_ref[rows, :] = h
            zb_ref[rows, :] = _dot(h, wb_ref[...]).astype(zb_ref.dtype)
            zf_ref[rows, :] = _dot(h, wf_ref[...])

    @pl.when((j > 0) & (j != conv_blk))
    def _():
        zb_ref[...] = _dot(h_ref[...], wb_ref[...]).astype(zb_ref.dtype)

    @pl.when(j == conv_blk)
    def _():
        h = h_ref[...]
        cw = cw_ref.shape[1]
        pad = cv_ref.shape[0] - tm
        cv_ref[0:pad, :] = jnp.zeros((pad, cw), F32)
        cv_ref[pad:pad + tm, :] = _dot(h, wb_ref[:, 0:cw])
        taps = cw_ref.shape[0]
        rb = 256

        def conv(piece, cs):
            cols = slice(cs * LANES, (cs + 1) * LANES)
            r0 = pad + piece * rb - (taps - 1)
            acc = cw_ref[0:1, cols] * cv_ref[pl.ds(r0, rb), cols]
            for i in range(1, taps):
                acc = acc + cw_ref[i:i + 1, cols] * cv_ref[pl.ds(r0 + i, rb), cols]
            scale = ML_DK ** -0.5 if cs >= cw // LANES // 2 else 1.0
            zb_ref[piece * rb:(piece + 1) * rb, cols] = (_silu(acc) * scale).astype(zb_ref.dtype)

        todo = [(piece, cs) for piece in range(tm // rb) for cs in range(cw // LANES)]
        chunks = list(range(cw, zb_ref.shape[1], MXU_N))
        per = -(-len(todo) // len(chunks))
        for n, c0 in enumerate(chunks):
            c1 = min(c0 + MXU_N, zb_ref.shape[1])
            zb_ref[:, c0:c1] = _dot(h, wb_ref[:, c0:c1]).astype(zb_ref.dtype)
            for piece, cs in todo[n * per:(n + 1) * per]:
                conv(piece, cs)


def _inproj(x, nw, wb, wf, conv_w, conv_blk, tm, tnb):
    T, D = x.shape
    return pl.pallas_call(
        functools.partial(_inproj_kernel, conv_blk),
        grid=(T // tm, wb.shape[1] // tnb),
        in_specs=[pl.BlockSpec((tm, D), lambda i, j: (i, 0)),
                  pl.BlockSpec((1, D), lambda i, j: (0, 0)),
                  pl.BlockSpec((D, tnb), lambda i, j: (0, j)),
                  pl.BlockSpec(wf.shape, lambda i, j: (0, 0)),
                  pl.BlockSpec(conv_w.shape, lambda i, j: (0, 0))],
        out_specs=[pl.BlockSpec((tm, tnb), lambda i, j: (i, j)),
                   pl.BlockSpec((tm, wf.shape[1]), lambda i, j: (i, 0))],
        out_shape=[jax.ShapeDtypeStruct((T, wb.shape[1]), BF16),
                   jax.ShapeDtypeStruct((T, wf.shape[1]), F32)],
        scratch_shapes=[pltpu.VMEM((tm, D), BF16), pltpu.VMEM((tm + 8, conv_w.shape[1]), F32)],
        compiler_params=_cparams("parallel", "arbitrary"),
        name="inproj",
    )(x, nw, wb, wf, conv_w)


def _retention_kernel(z_ref, cos_ref, sin_ref, nw_ref, o_ref, dec_ref, st_ref):
    S = z_ref.shape[0] // RET_SEQS
    n_chunks = S // CHUNK
    H = RET_HEADS
    HW = H * LANES
    row = lax.broadcasted_iota(jnp.int32, (CHUNK, CHUNK), 0).astype(F32)
    col = lax.broadcasted_iota(jnp.int32, (CHUNK, CHUNK), 1).astype(F32)
    lgs = [math.log1p(-(2.0 ** (-5.0 - h))) for h in range(H)]
    for h, lg in enumerate(lgs):
        rel = row - col
        dec_ref[h] = jnp.where(rel >= 0, jnp.exp(lg * jnp.maximum(rel, 0.0)), 0.0)
        dec_ref[H + h] = jnp.exp(lg * (row + 1.0))
        dec_ref[2 * H + h] = jnp.exp(lg * (CHUNK - 1.0 - row)) * (LANES ** -0.5)
    st_ref[...] = jnp.zeros(st_ref.shape, F32)
    units = [(sq, h) for sq in range(RET_SEQS) for h in range(H)]

    def body(n, carry):
        r0 = pl.multiple_of(n * CHUNK, CHUNK)
        cos = cos_ref[pl.ds(r0, CHUNK), :]
        sin = sin_ref[pl.ds(r0, CHUNK), :]
        rows_of = [pl.ds(pl.multiple_of(sq * S + n * CHUNK, CHUNK), CHUNK) for sq in range(RET_SEQS)]
        vs, ss, iqs, kvs = [], [], [], []
        for u, (sq, h) in enumerate(units):
            rows = rows_of[sq]
            q = z_ref[rows, h * LANES:(h + 1) * LANES].astype(F32)
            k = z_ref[rows, HW + h * LANES:HW + (h + 1) * LANES].astype(F32)
            v = z_ref[rows, 2 * HW + h * LANES:2 * HW + (h + 1) * LANES]
            q = q * cos + pltpu.roll(q, LANES // 2, 1) * sin
            k = k * cos + pltpu.roll(k, LANES // 2, 1) * sin
            qb = q.astype(BF16)
            kb = (k * (LANES ** -0.5)).astype(BF16)
            kd = (k * dec_ref[2 * H + h]).astype(BF16)
            ss.append(_dot_nt(qb, kb))
            iqs.append(_dot(qb, st_ref[u].astype(BF16)))
            kvs.append(_dot_tn(kd, v))
            vs.append(v)
        ps = [(ss[u] * dec_ref[h]).astype(BF16) for u, (_, h) in enumerate(units)]
        os_ = [_dot(ps[u], vs[u]) + iqs[u] * dec_ref[H + h] for u, (_, h) in enumerate(units)]
        for u, (sq, h) in enumerate(units):
            rows = rows_of[sq]
            st_ref[u] = st_ref[u] * math.exp(lgs[h] * CHUNK) + kvs[u]
            o = os_[u]
            ms = jnp.mean(o * o, axis=-1, keepdims=True)
            o = o * lax.rsqrt(ms + EPS) * nw_ref[:, h * LANES:(h + 1) * LANES]
            g = z_ref[rows, 3 * HW + h * LANES:3 * HW + (h + 1) * LANES].astype(F32)
            o_ref[rows, h * LANES:(h + 1) * LANES] = (o * _silu(g)).astype(o_ref.dtype)
        return carry

    lax.fori_loop(0, n_chunks, body, 0)


def _retention(zb, cos, sin, nw, B, S):
    T = B * S
    W = RET_HEADS * LANES
    R = RET_SEQS * S
    return pl.pallas_call(
        _retention_kernel,
        grid=(B // RET_SEQS,),
        in_specs=[pl.BlockSpec((R, 4 * W), lambda b: (b, 0)),
                  pl.BlockSpec((S, LANES), lambda b: (0, 0)),
                  pl.BlockSpec((S, LANES), lambda b: (0, 0)),
                  pl.BlockSpec((1, W), lambda b: (0, 0))],
        out_specs=pl.BlockSpec((R, W), lambda b: (b, 0)),
        out_shape=jax.ShapeDtypeStruct((T, W), BF16),
        scratch_shapes=[pltpu.VMEM((3 * RET_HEADS, CHUNK, LANES), F32),
                        pltpu.VMEM((RET_SEQS * RET_HEADS, LANES, LANES), F32)],
        compiler_params=_cparams("parallel"),
        name="retention",
    )(zb, cos, sin, nw)


def _da_kernel(*refs):
    z_refs = refs[:9]
    wq_ref, wk_ref, o_ref, qn_ref, kn_ref, v_ref, np_ref, lp_ref, mp_ref = refs[9:18]
    nn_refs, ln_refs, mn_refs = refs[18:21], refs[21:24], refs[24:27]
    stage_refs = refs[27:30]
    S = o_ref.shape[0]
    lane = lax.broadcasted_iota(jnp.int32, (1, LANES), 1)
    lo = lane < DA_DH
    row = lax.broadcasted_iota(jnp.int32, (CHUNK, CHUNK), 0)
    col = lax.broadcasted_iota(jnp.int32, (CHUNK, CHUNK), 1)
    mask_cur = col <= row
    row2 = lax.broadcasted_iota(jnp.int32, (CHUNK, 2 * CHUNK), 0)
    col2 = lax.broadcasted_iota(jnp.int32, (CHUNK, 2 * CHUNK), 1)
    mask_band = (col2 >= row2) & (col2 <= row2 + CHUNK)

    seg = (lax.broadcasted_iota(jnp.int32, (LANES, LANES), 0) // DA_DH
           == lax.broadcasted_iota(jnp.int32, (LANES, LANES), 1) // DA_DH).astype(BF16) * (1.0 / DA_DH)

    def head_norm(x, w):
        x2 = x * x
        hi = x2.astype(BF16)
        lo_part = (x2 - hi.astype(F32)).astype(BF16)
        ms = _dot(hi, seg) + _dot(lo_part, seg)
        return x * lax.rsqrt(ms + EPS) * w

    RB = 256
    for g, (dil, nb) in enumerate(DA_GROUPS):
        L = S // dil
        zq_ref, zk_ref, zv_ref = z_refs[3 * g:3 * g + 3]
        if dil > 1:
            for piece in range(S // RB):
                rows = pl.ds(piece * RB, RB)
                for src_ref, dst_ref in zip((zq_ref, zk_ref, zv_ref), stage_refs):
                    dst_ref[rows, :] = src_ref[rows, :].astype(F32)
            zq_ref, zk_ref, zv_ref = stage_refs
        first = min(dil, DA_MAX_STRIDE)
        if dil > first:
            mid_refs = (nn_refs[g], ln_refs[g], mn_refs[g])
            sub = S // first
            for b in range(first):
                for piece in range(sub // RB):
                    src = pl.ds(b + first * piece * RB, RB, stride=first)
                    dst = pl.ds(b * sub + piece * RB, RB)
                    for src_ref, dst_ref in zip((zq_ref, zk_ref, zv_ref), mid_refs):
                        dst_ref[dst, :] = src_ref[src, :]
            zq_ref, zk_ref, zv_ref = mid_refs
        rest = dil // first

        def source_rows(rho, piece, n_rows):
            if dil == 1:
                return pl.ds(piece * n_rows, n_rows)
            b, a = rho % first, rho // first
            if rest == 1:
                return pl.ds(b + first * piece * n_rows, n_rows, stride=first)
            return pl.ds(b * (S // first) + a + rest * piece * n_rows, n_rows, stride=rest)

        for rho in range(dil):
            for piece in range(max(L // RB, 1)):
                n_rows = min(RB, L)
                src = source_rows(rho, piece, n_rows)
                dst = pl.ds(rho * L + piece * n_rows, n_rows)
                q = zq_ref[src, :].astype(F32)
                k = zk_ref[src, :].astype(F32)
                v = zv_ref[src, :]
                qn_ref[dst, :] = (head_norm(q, wq_ref[...]) * (DA_DH ** -0.5)).astype(BF16)
                kn_ref[dst, :] = head_norm(k, wk_ref[...]).astype(BF16)
                v_ref[dst, :] = v.astype(BF16)

        def batch(r0, chained, first_has_prev):
            chains = []
            for u in range(DA_UNROLL):
                rows = pl.ds(r0 + u * CHUNK, CHUNK)
                if chained and (u > 0 or first_has_prev):
                    keys, mask = pl.ds(r0 + (u - 1) * CHUNK, 2 * CHUNK), mask_band
                else:
                    keys, mask = rows, mask_cur
                k = kn_ref[keys, :]
                v = v_ref[keys, :]
                vaug = jnp.concatenate([v, jnp.ones_like(v)], axis=1)
                q = qn_ref[rows, :]
                zero = jnp.zeros_like(q)
                for qh in (jnp.where(lo, q, zero), jnp.where(lo, zero, q)):
                    chains.append((_dot_nt(qh, k), mask, vaug))
            probs = []
            for s, mask, _ in chains:
                s = jnp.where(mask, s, NEG)
                m = jnp.max(s, axis=-1, keepdims=True)
                probs.append((jnp.exp(s - m).astype(BF16), m))
            accs = [_dot(p, vaug) for (p, _), (_, _, vaug) in zip(probs, chains)]
            nums, dens, maxs = [], [], []
            for u in range(DA_UNROLL):
                a0, a1 = accs[2 * u], accs[2 * u + 1]
                nums.append(jnp.where(lo, a0[:, :LANES], a1[:, :LANES]))
                dens.append(jnp.where(lo, a0[:, LANES:], a1[:, LANES:]))
                maxs.append(jnp.where(lo, probs[2 * u][1], probs[2 * u + 1][1]))
            rows = pl.ds(r0, DA_UNROLL * CHUNK)
            np_ref[rows, :] = jnp.concatenate(nums, axis=0)
            lp_ref[rows, :] = jnp.concatenate(dens, axis=0)
            mp_ref[rows, :] = jnp.concatenate(maxs, axis=0)

        span = DA_UNROLL * CHUNK
        if nb == 1:
            def singles(i, c):
                batch(pl.multiple_of(i * span, span), False, False)
                return c
            lax.fori_loop(0, S // span, singles, 0)
        else:
            def segment(sgi, c, nb=nb):
                base = pl.multiple_of(sgi * (nb * CHUNK), span)
                batch(base, True, False)
                if nb > DA_UNROLL:
                    def inner(n, c2):
                        batch(pl.multiple_of(base + n * span, span), True, True)
                        return c2
                    lax.fori_loop(1, nb // DA_UNROLL, inner, 0)
                return c
            lax.fori_loop(0, dil, segment, 0)

        outs = ((np_ref, nn_refs[g]), (lp_ref, ln_refs[g]), (mp_ref, mn_refs[g]))
        for rho in range(dil):
            for piece in range(max(L // RB, 1)):
                n_rows = min(RB, L)
                dst = source_rows(rho, piece, n_rows)
                src = pl.ds(rho * L + piece * n_rows, n_rows)
                for k_out, (perm_ref, nat_ref) in enumerate(outs):
                    (stage_refs[k_out] if rest > 1 else nat_ref)[dst, :] = perm_ref[src, :]
        if rest > 1:
            sub = S // first
            for b in range(first):
                for piece in range(sub // RB):
                    dst = pl.ds(b + first * piece * RB, RB, stride=first)
                    src = pl.ds(b * sub + piece * RB, RB)
                    for k_out, (_, nat_ref) in enumerate(outs):
                        nat_ref[dst, :] = stage_refs[k_out][src, :]

    for piece in range(S // RB):
        rows = pl.ds(piece * RB, RB)
        m = jnp.maximum(jnp.maximum(mn_refs[0][rows, :], mn_refs[1][rows, :]), mn_refs[2][rows, :])
        num = jnp.zeros((RB, LANES), F32)
        den = jnp.zeros((RB, LANES), F32)
        for g in range(3):
            e = jnp.exp(mn_refs[g][rows, :] - m)
            num = num + e * nn_refs[g][rows, :]
            den = den + e * ln_refs[g][rows, :]
        o_ref[rows, :] = (num / den).astype(o_ref.dtype)


def _dilated_attention(zb, slab0, wq, wk, B, S):
    T = B * S
    slab = lambda k: pl.BlockSpec((S, LANES), lambda b, p: (b, slab0 + 9 * p + k))
    return pl.pallas_call(
        _da_kernel,
        grid=(B, 2),
        in_specs=[slab(k) for k in range(9)] + [
                  pl.BlockSpec((1, LANES), lambda b, p: (0, 0)),
                  pl.BlockSpec((1, LANES), lambda b, p: (0, 0))],
        out_specs=pl.BlockSpec((S, LANES), lambda b, p: (b, p)),
        out_shape=jax.ShapeDtypeStruct((T, 2 * LANES), BF16),
        scratch_shapes=[pltpu.VMEM((S, LANES), BF16), pltpu.VMEM((S, LANES), BF16),
                        pltpu.VMEM((S, LANES), BF16),
                        pltpu.VMEM((S, LANES), F32), pltpu.VMEM((S, LANES), F32),
                        pltpu.VMEM((S, LANES), F32)] + [pltpu.VMEM((S, LANES), F32)] * 12,
        compiler_params=_cparams("parallel", "arbitrary"),
        name="dilated_attention",
    )(*([zb] * 9), wq, wk)


def _mlstm_kernel(qk_ref, v_ref, og_ref, gate_ref, gb_ref, nw_ref, o_ref, st_ref, e_ref, et_ref):
    S = qk_ref.shape[0] // ML_SEQS
    n_chunks = S // CHUNK
    W = qk_ref.shape[1]
    lane = lax.broadcasted_iota(jnp.int32, (1, LANES), 1)
    lo = lane < ML_DK
    row = lax.broadcasted_iota(jnp.int32, (CHUNK, CHUNK), 0)
    col = lax.broadcasted_iota(jnp.int32, (CHUNK, CHUNK), 1)
    causal = col <= row
    tril = causal.astype(F32)
    srow = lax.broadcasted_iota(jnp.int32, (LANES, 1), 0) < ML_DK
    st_ref[...] = jnp.zeros(st_ref.shape, F32)

    is_f = (lane >= ML_HEADS) & (lane < 2 * ML_HEADS)
    for n in range(ML_SEQS * n_chunks):
        rows = pl.ds(n * CHUNK, CHUNK)
        gp = gate_ref[rows, :] + gb_ref[...]
        logf = jnp.minimum(gp, 0.0) - jnp.log(1.0 + jnp.exp(-jnp.abs(gp)))
        cum = jnp.dot(tril, jnp.where(is_f, logf, 0.0), preferred_element_type=F32,
                      precision=lax.Precision.HIGHEST)
        e = jnp.where(lane < ML_HEADS, gp, cum)
        e_ref[rows, :] = e
        et_ref[n] = e.T[0:2 * ML_HEADS, :]

    units = [(sq, h) for sq in range(ML_SEQS) for h in range(ML_HEADS)]

    def chunk(n, mms):
        rows_of = [pl.ds(pl.multiple_of(sq * S + n * CHUNK, CHUNK), CHUNK) for sq in range(ML_SEQS)]
        es = [e_ref[rows_of[sq], :] for sq in range(ML_SEQS)]
        ets = [et_ref[sq * n_chunks + n] for sq in range(ML_SEQS)]
        khs, vaugs, s_raws, iqs = [], [], [], []
        for sq in range(ML_SEQS):
            rows = rows_of[sq]
            for pair in range(2):
                q2 = qk_ref[rows, pair * LANES:(pair + 1) * LANES]
                k2 = qk_ref[rows, W // 2 + pair * LANES:W // 2 + (pair + 1) * LANES]
                zero = jnp.zeros_like(q2)
                stb = st_ref[2 * sq + pair].astype(BF16)
                for hs in range(2):
                    h = 2 * pair + hs
                    qh = jnp.where(lo, q2, zero) if hs == 0 else jnp.where(lo, zero, q2)
                    kh = jnp.where(lo, k2, zero) if hs == 0 else jnp.where(lo, zero, k2)
                    vh = v_ref[rows, h * LANES:(h + 1) * LANES]
                    s_raws.append(_dot_nt(qh, k2))
                    iqs.append(_dot(qh, stb))
                    khs.append(kh)
                    vaugs.append(jnp.concatenate([vh, jnp.ones_like(vh)], axis=-1))
        ps, wis, ms_, kws, wcs, new_mms = [], [], [], [], [], []
        for u, (sq, h) in enumerate(units):
            e, et = es[sq], ets[sq]
            mm = mms[u]
            i_b = jnp.broadcast_to(e[:, h:h + 1], (CHUNK, LANES))
            a_b = jnp.broadcast_to(e[:, ML_HEADS + h:ML_HEADS + h + 1], (CHUNK, LANES))
            c_row = et[h:h + 1, :] - et[ML_HEADS + h:ML_HEADS + h + 1, :]
            dmat = jnp.where(causal, a_b + c_row, NEG)
            inter = a_b + mm
            m = jnp.maximum(inter, jnp.max(dmat, axis=-1, keepdims=True))
            ps.append((s_raws[u] * jnp.exp(dmat - m)).astype(BF16))
            wis.append(jnp.exp(inter - m))
            ms_.append(m)
            m_new = m[CHUNK - 1:CHUNK, :]
            a_last = a_b[CHUNK - 1:CHUNK, :]
            wk = jnp.exp(a_last - a_b + i_b - m_new)
            kws.append((khs[u].astype(F32) * wk).astype(BF16))
            wc = jnp.exp(a_last + mm - m_new)
            wcs.append(jnp.concatenate([wc, wc], axis=1))
            new_mms.append(m_new)
        accs = [_dot(ps[u], vaugs[u]) + jnp.concatenate([wis[u], wis[u]], axis=1) * iqs[u]
                for u in range(len(units))]
        upds = [_dot_tn(kws[u], vaugs[u]) for u in range(len(units))]
        for sp in range(2 * ML_SEQS):
            u0, u1 = 2 * sp, 2 * sp + 1
            st_ref[sp] = jnp.where(srow, wcs[u0], wcs[u1]) * st_ref[sp] + (upds[u0] + upds[u1])
        for u, (sq, h) in enumerate(units):
            num = accs[u][:, :LANES]
            den = accs[u][:, LANES:]
            hv = num / jnp.maximum(jnp.abs(den), jnp.exp(-ms_[u]))
            var = jnp.mean(hv * hv, axis=-1, keepdims=True)
            hv = hv * lax.rsqrt(var + EPS) * nw_ref[:, h * LANES:(h + 1) * LANES]
            og = og_ref[rows_of[sq], h * LANES:(h + 1) * LANES].astype(F32)
            o_ref[rows_of[sq], h * LANES:(h + 1) * LANES] = (hv * _sigmoid(og)).astype(o_ref.dtype)
        return tuple(new_mms)

    lax.fori_loop(0, n_chunks, chunk, tuple(jnp.zeros((1, LANES), F32) for _ in units))


def _mlstm(zb, zf, gate_bias, nw, B, S, qk_blk, gate_blk):
    T = B * S
    W = ML_HEADS * LANES
    R = ML_SEQS * S
    return pl.pallas_call(
        _mlstm_kernel,
        grid=(B // ML_SEQS,),
        in_specs=[pl.BlockSpec((R, W), lambda b: (b, qk_blk)),
                  pl.BlockSpec((R, W), lambda b: (b, qk_blk + 1)),
                  pl.BlockSpec((R, W), lambda b: (b, qk_blk + 2)),
                  pl.BlockSpec((R, LANES), lambda b: (b, gate_blk)),
                  pl.BlockSpec((1, LANES), lambda b: (0, 0)),
                  pl.BlockSpec((1, W), lambda b: (0, 0))],
        out_specs=pl.BlockSpec((R, W), lambda b: (b, 0)),
        out_shape=jax.ShapeDtypeStruct((T, W), BF16),
        scratch_shapes=[pltpu.VMEM((2 * ML_SEQS, LANES, 2 * LANES), F32),
                        pltpu.VMEM((R, LANES), F32),
                        pltpu.VMEM((R // CHUNK, 2 * ML_HEADS, LANES), F32)],
        compiler_params=_cparams("parallel"),
        name="mlstm",
    )(zb, zb, zb, zf, gate_bias, nw)


def _merge_value(x_ref, oret_ref, oda_ref, oml_ref, gr_ref, gd_ref, gm_ref,
                 wr_ref, wd_ref, wm_ref, wo_ref):
    y = _sigmoid(gr_ref[...].astype(F32)) * _dot(oret_ref[...], wr_ref[...])
    y = y + _sigmoid(gd_ref[...].astype(F32)) * _dot(oda_ref[...], wd_ref[...])
    y = y + _sigmoid(gm_ref[...].astype(F32)) * _dot(oml_ref[...], wm_ref[...])
    return x_ref[...] + _dot(y.astype(BF16), wo_ref[...])


def _merge_specs(x, o_ret, o_da, o_ml, g_blk, tm):
    D = x.shape[1]
    return [pl.BlockSpec((tm, D), lambda i: (i, 0)),
            pl.BlockSpec((tm, o_ret.shape[1]), lambda i: (i, 0)),
            pl.BlockSpec((tm, o_da.shape[1]), lambda i: (i, 0)),
            pl.BlockSpec((tm, o_ml.shape[1]), lambda i: (i, 0)),
            pl.BlockSpec((tm, D), lambda i: (i, g_blk)),
            pl.BlockSpec((tm, D), lambda i: (i, g_blk + 1)),
            pl.BlockSpec((tm, D), lambda i: (i, g_blk + 2))]


def _resident(a):
    return pl.BlockSpec(a.shape, lambda i: (0,) * a.ndim, pipeline_mode=pl.Buffered(1))


def _swiglu_block(h, wg_ref, wu_ref, wd_ref, a_ref):
    tf = wg_ref.shape[1]
    c0 = 0
    while c0 < tf:
        w = min(MXU_N, tf - c0)
        g = _dot(h, wg_ref[:, c0:c0 + w].astype(BF16))
        u = _dot(h, wu_ref[:, c0:c0 + w].astype(BF16))
        a_ref[:, c0:c0 + w] = (_silu(g) * u).astype(BF16)
        c0 += w
    return _dot(a_ref[...], wd_ref[...].astype(BF16))


def _merge_ffn_kernel(*refs):
    merge_refs, (nw_ref, wg_ref, wu_ref, wdn_ref, o_ref, a_ref) = refs[:11], refs[11:]
    x = _merge_value(*merge_refs)
    ms = jnp.mean(x * x, axis=-1, keepdims=True)
    h = (x * lax.rsqrt(ms + EPS) * nw_ref[...]).astype(BF16)
    o_ref[...] = x + _swiglu_block(h, wg_ref, wu_ref, wdn_ref, a_ref)


def _merge_ffn(x, o_ret, o_da, o_ml, zb, g_blk, wr, wd, wm, wo, nw, wg, wu, wdn, tm):
    T, D = x.shape
    weights = (wr, wd, wm, wo, nw, wg, wu, wdn)
    return pl.pallas_call(
        _merge_ffn_kernel,
        grid=(T // tm,),
        in_specs=_merge_specs(x, o_ret, o_da, o_ml, g_blk, tm) + [_resident(a) for a in weights],
        out_specs=pl.BlockSpec((tm, D), lambda i: (i, 0)),
        out_shape=jax.ShapeDtypeStruct((T, D), F32),
        scratch_shapes=[pltpu.VMEM((tm, wg.shape[1]), BF16)],
        compiler_params=_cparams("parallel"),
        name="merge_ffn",
    )(x, o_ret, o_da, o_ml, zb, zb, zb, *weights)


HI16 = 0xFFFF0000


def _pack_bf16_pairs(a, b):
    ab = pltpu.bitcast(a.astype(BF16).astype(F32), jnp.uint32)
    bb = pltpu.bitcast(b.astype(BF16).astype(F32), jnp.uint32)
    return (ab >> 16) | (bb & jnp.uint32(HI16))


def _unpack_bf16_pairs(p):
    lo = pltpu.bitcast(p << 16, F32)
    hi = pltpu.bitcast(p & jnp.uint32(HI16), F32)
    return lo, hi


def _merge_router_kernel(*refs):
    merge_refs, (nw_ref, wr_ref, br_ref, xo_ref, ha_ref, hb_ref, sel_ref, cnt_ref) = refs[:11], refs[11:]
    x = _merge_value(*merge_refs)
    xo_ref[...] = x
    ms = jnp.mean(x * x, axis=-1, keepdims=True)
    h = x * lax.rsqrt(ms + EPS) * nw_ref[...]
    Q = h.shape[1] // 4
    ha_ref[...] = _pack_bf16_pairs(h[:, 0:Q], h[:, Q:2 * Q])
    hb_ref[...] = _pack_bf16_pairs(h[:, 2 * Q:3 * Q], h[:, 3 * Q:4 * Q])
    w = wr_ref[...]
    h_hi = h.astype(BF16)
    h_lo = (h - h_hi.astype(F32)).astype(BF16)
    w_hi = w.astype(BF16)
    w_lo = (w - w_hi.astype(F32)).astype(BF16)
    logits = _dot(h_hi, w_hi) + _dot(h_lo, w_hi) + _dot(h_hi, w_lo) + br_ref[...]
    lane = lax.broadcasted_iota(jnp.int32, logits.shape, 1).astype(F32)
    logits = jnp.where(lane < N_EXPERTS, logits, NEG)
    m1 = jnp.max(logits, axis=-1, keepdims=True)
    i1 = jnp.min(jnp.where(logits == m1, lane, float(LANES)), axis=-1, keepdims=True)
    rest = jnp.where(lane == i1, NEG, logits)
    m2 = jnp.max(rest, axis=-1, keepdims=True)
    i2 = jnp.min(jnp.where(rest == m2, lane, float(LANES)), axis=-1, keepdims=True)
    e2 = jnp.exp(m2 - m1)
    p1 = 1.0 / (1.0 + e2)
    p2 = e2 / (1.0 + e2)
    sel_ref[...] = jnp.where(lane == 0.0, i1, jnp.where(lane == 1.0, i2,
                             jnp.where(lane == 2.0, p1, jnp.where(lane == 3.0, p2, 0.0))))
    chosen = jnp.where((lane == i1) | (lane == i2), 1.0, 0.0)
    cnt_ref[...] = jnp.broadcast_to(jnp.sum(chosen, axis=0, keepdims=True), cnt_ref.shape)


def _merge_router(x, o_ret, o_da, o_ml, zb, g_blk, wr, wd, wm, wo, nw, w_router, b_router, tm):
    T, D = x.shape
    Q = D // 4
    wrt = jnp.pad(w_router, ((0, 0), (0, LANES - N_EXPERTS)))
    brt = jnp.pad(b_router, (0, LANES - N_EXPERTS)).reshape(1, LANES)
    weights = (wr, wd, wm, wo, nw, wrt, brt)
    return pl.pallas_call(
        _merge_router_kernel,
        grid=(T // tm,),
        in_specs=_merge_specs(x, o_ret, o_da, o_ml, g_blk, tm) + [_resident(a) for a in weights],
        out_specs=[pl.BlockSpec((tm, D), lambda i: (i, 0)),
                   pl.BlockSpec((tm, Q), lambda i: (i, 0)),
                   pl.BlockSpec((tm, Q), lambda i: (i, 0)),
                   pl.BlockSpec((tm, LANES), lambda i: (i, 0)),
                   pl.BlockSpec((None, 8, LANES), lambda i: (i, 0, 0))],
        out_shape=[jax.ShapeDtypeStruct((T, D), F32),
                   jax.ShapeDtypeStruct((T, Q), jnp.uint32), jax.ShapeDtypeStruct((T, Q), jnp.uint32),
                   jax.ShapeDtypeStruct((T, LANES), F32),
                   jax.ShapeDtypeStruct((T // tm, 8, LANES), F32)],
        compiler_params=_cparams("parallel"),
        name="merge_router",
    )(x, o_ret, o_da, o_ml, zb, zb, zb, *weights)


def _moe_rank_kernel(sel_ref, tcnt_ref, pos_ref, meta_ref, cnt_ref, offs_ref, carry_ref, before_ref):
    i = pl.program_id(0)
    tm = sel_ref.shape[0]
    lane = lax.broadcasted_iota(jnp.int32, (tm, LANES), 1).astype(F32)
    lane1 = lax.broadcasted_iota(jnp.int32, (1, LANES), 1).astype(F32)
    sel = sel_ref[...]
    i1 = sel[:, 0:1]
    i2 = sel[:, 1:2]
    onehot = jnp.where((lane == i1) | (lane == i2), 1.0, 0.0)
    colsum = jnp.sum(onehot, axis=0, keepdims=True)

    def padded_counts():
        return jnp.floor((cnt_ref[...] + (MOE_TILE - 1.0)) * (1.0 / MOE_TILE)) * MOE_TILE

    @pl.when(i == 0)
    def _():
        cnt_ref[...] = jnp.sum(tcnt_ref[...], axis=0)
        k = lax.broadcasted_iota(jnp.int32, (LANES, LANES), 0)
        e = lax.broadcasted_iota(jnp.int32, (LANES, LANES), 1)
        upper = (k < e).astype(F32)
        offs_ref[...] = jnp.dot(padded_counts(), upper, preferred_element_type=F32,
                                precision=lax.Precision.HIGHEST)
        carry_ref[...] = jnp.zeros(carry_ref.shape, F32)
        r = lax.broadcasted_iota(jnp.int32, (tm, tm), 0)
        c = lax.broadcasted_iota(jnp.int32, (tm, tm), 1)
        before_ref[...] = (c < r).astype(BF16)

    rank = _dot(before_ref[...], onehot.astype(BF16)) + carry_ref[0:1, :]
    row = rank + offs_ref[0:1, :]
    pos1 = jnp.sum(jnp.where(lane == i1, row, 0.0), axis=-1, keepdims=True)
    pos2 = jnp.sum(jnp.where(lane == i2, row, 0.0), axis=-1, keepdims=True)
    both = jnp.where(lane == 0.0, pos1, jnp.where(lane == 1.0, pos2, 0.0))
    pos_ref[...] = both.T[0:8, :].astype(jnp.int32)
    carry_ref[...] += colsum

    @pl.when(i == pl.num_programs(0) - 1)
    def _():
        nrow = meta_ref.shape[0]
        padded = padded_counts()[0:1, :]
        offs = offs_ref[0:1, :]
        ends = offs + padded
        start = lax.broadcasted_iota(jnp.int32, (nrow, 1), 0).astype(F32) * MOE_TILE
        is_e = lane1 < N_EXPERTS
        te = jnp.sum(jnp.where(is_e & (ends <= start), 1.0, 0.0), axis=-1, keepdims=True)
        te = jnp.minimum(te, N_EXPERTS - 1.0)
        valid_end = jnp.sum(jnp.where(lane1 == te, offs + cnt_ref[0:1, :], 0.0), axis=-1, keepdims=True)
        nv = jnp.clip(valid_end - start, 0.0, MOE_TILE)
        total = jnp.sum(jnp.where(is_e, padded, 0.0), axis=-1, keepdims=True)
        active = jnp.where(start < total, 1.0, 0.0)
        meta_ref[...] = jnp.where(lane1 == 0.0, te, jnp.where(lane1 == 1.0, nv,
                                  jnp.where(lane1 == 2.0, active, 0.0)))


def _moe_rank(sel, tcnt, tm, meta_rows):
    T = sel.shape[0]
    nt = T // tm
    return pl.pallas_call(
        _moe_rank_kernel,
        grid=(nt,),
        in_specs=[pl.BlockSpec((tm, LANES), lambda i: (i, 0)),
                  pl.BlockSpec(tcnt.shape, lambda i: (0, 0, 0))],
        out_specs=[pl.BlockSpec((None, 8, tm), lambda i: (i, 0, 0)),
                   pl.BlockSpec((meta_rows, LANES), lambda i: (0, 0))],
        out_shape=[jax.ShapeDtypeStruct((nt, 8, tm), jnp.int32),
                   jax.ShapeDtypeStruct((meta_rows, LANES), F32)],
        scratch_shapes=[pltpu.VMEM((8, LANES), F32), pltpu.VMEM((8, LANES), F32),
                        pltpu.VMEM((8, LANES), F32), pltpu.VMEM((tm, tm), BF16)],
        compiler_params=_cparams("arbitrary"),
        name="moe_rank",
    )(sel, tcnt)


def _sc_mesh():
    return plsc.VectorSubcoreMesh(core_axis_name="core", subcore_axis_name="subcore")


def _sc_scatter_rows(x, idx, n_rows):
    nt = x.shape[0] // SC_WINDOW
    width = x.shape[1]
    assert idx.shape[1] == TOP_K * x.shape[0]

    @functools.partial(pl.kernel, out_type=jax.ShapeDtypeStruct((n_rows, width), x.dtype),
                       mesh=_sc_mesh())
    def scatter_kernel(x_hbm, i1_hbm, i2_hbm, o_hbm):
        def body(x_vmem, i1_vmem, i2_vmem):
            pltpu.sync_copy(x_vmem, o_hbm.at[i1_vmem.at[0]])
            pltpu.sync_copy(x_vmem, o_hbm.at[i2_vmem.at[0]])

        pltpu.emit_pipeline(
            body, grid=(nt,),
            in_specs=[pl.BlockSpec((SC_WINDOW, width), lambda i: (i, 0)),
                      pl.BlockSpec((1, SC_WINDOW), lambda i: (0, i)),
                      pl.BlockSpec((1, SC_WINDOW), lambda i: (0, i + nt))],
            out_specs=[],
            core_axis_name=("core", "subcore"), dimension_semantics=(pltpu.PARALLEL,),
        )(x_hbm, i1_hbm, i2_hbm)

    return scatter_kernel(x, idx, idx)


def _sc_gather_rows(x, idx):
    n_idx = idx.shape[1]
    width = x.shape[1]

    @functools.partial(pl.kernel, out_type=jax.ShapeDtypeStruct((n_idx, width), x.dtype),
                       mesh=_sc_mesh())
    def gather_kernel(x_hbm, i_hbm, o_hbm):
        def body(i_vmem, o_vmem):
            pltpu.sync_copy(x_hbm.at[i_vmem.at[0]], o_vmem)

        pltpu.emit_pipeline(
            body, grid=(n_idx // SC_WINDOW,),
            in_specs=[pl.BlockSpec((1, SC_WINDOW), lambda i: (0, i))],
            out_specs=[pl.BlockSpec((SC_WINDOW, width), lambda i: (i, 0))],
            core_axis_name=("core", "subcore"), dimension_semantics=(pltpu.PARALLEL,),
        )(i_hbm, o_hbm)

    return gather_kernel(x, idx)


def _moe_group_kernel(te_ref, nv_ref, na_ref, xa_ref, xb_ref, wg_ref, wu_ref, wd_ref,
                      ya_ref, yb_ref, h_ref, a_ref):
    j = pl.program_id(0)

    @pl.when(j < na_ref[0])
    def _():
        tm = h_ref.shape[0]
        Q = xa_ref.shape[1]
        valid = lax.broadcasted_iota(jnp.int32, (tm, 1), 0) < nv_ref[j]
        for src, c0 in ((xa_ref, 0), (xb_ref, 2 * Q)):
            lo, hi = _unpack_bf16_pairs(src[...])
            h_ref[:, c0:c0 + Q] = jnp.where(valid, lo, 0.0).astype(BF16)
            h_ref[:, c0 + Q:c0 + 2 * Q] = jnp.where(valid, hi, 0.0).astype(BF16)
        y = _swiglu_block(h_ref[...], wg_ref, wu_ref, wd_ref, a_ref)
        ya_ref[...] = _pack_bf16_pairs(y[:, 0:Q], y[:, Q:2 * Q])
        yb_ref[...] = _pack_bf16_pairs(y[:, 2 * Q:3 * Q], y[:, 3 * Q:4 * Q])


def _moe_group(te, nv, na, xa, xb, wg, wu, wd):
    R, Q = xa.shape
    E, D, F = wg.shape
    tile = lambda j, te, nv, na: (jnp.minimum(j, na[0] - 1), 0)
    expert = lambda j, te, nv, na: (te[j], 0, 0)
    once = pl.Buffered(1)
    grid_spec = pltpu.PrefetchScalarGridSpec(
        num_scalar_prefetch=3,
        grid=(R // MOE_TILE,),
        in_specs=[pl.BlockSpec((MOE_TILE, Q), tile),
                  pl.BlockSpec((MOE_TILE, Q), tile),
                  pl.BlockSpec((None, D, F), expert),
                  pl.BlockSpec((None, D, F), expert),
                  pl.BlockSpec((None, F, D), expert, pipeline_mode=once)],
        out_specs=[pl.BlockSpec((MOE_TILE, Q), tile), pl.BlockSpec((MOE_TILE, Q), tile)],
        scratch_shapes=[pltpu.VMEM((MOE_TILE, D), BF16), pltpu.VMEM((MOE_TILE, F), BF16)],
    )
    return pl.pallas_call(
        _moe_group_kernel,
        grid_spec=grid_spec,
        out_shape=[jax.ShapeDtypeStruct((R, Q), jnp.uint32), jax.ShapeDtypeStruct((R, Q), jnp.uint32)],
        compiler_params=_cparams("arbitrary"),
        name="moe_experts",
    )(te, nv, na, xa, xb, wg, wu, wd)


def _moe_combine_kernel(x_ref, sel_ref, g1_ref, g2_ref, *rest):
    o_ref = rest[-1]
    sel = sel_ref[...]
    p1 = sel[:, 2:3]
    p2 = sel[:, 3:4]
    Q = g1_ref.shape[1]
    lo1, hi1 = _unpack_bf16_pairs(g1_ref[...])
    lo2, hi2 = _unpack_bf16_pairs(g2_ref[...])
    o_ref[:, 0:Q] = x_ref[:, 0:Q] + (p1 * lo1 + p2 * lo2)
    o_ref[:, Q:2 * Q] = x_ref[:, Q:2 * Q] + (p1 * hi1 + p2 * hi2)


def _moe_combine_half(x, sel, g, half, partial_out, tm):
    T, D = x.shape
    Q = g.shape[1]
    nt = T // tm
    in_specs = [pl.BlockSpec((tm, 2 * Q), lambda i: (i, half)),
                pl.BlockSpec((tm, LANES), lambda i: (i, 0)),
                pl.BlockSpec((tm, Q), lambda i: (i, 0)),
                pl.BlockSpec((tm, Q), lambda i: (i + nt, 0))]
    args = [x, sel, g, g]
    aliases = {}
    if partial_out is not None:
        in_specs.append(pl.BlockSpec(memory_space=pl.ANY))
        args.append(partial_out)
        aliases = {4: 0}
    return pl.pallas_call(
        _moe_combine_kernel,
        grid=(nt,),
        in_specs=in_specs,
        out_specs=pl.BlockSpec((tm, 2 * Q), lambda i: (i, half)),
        out_shape=jax.ShapeDtypeStruct((T, D), F32),
        input_output_aliases=aliases,
        compiler_params=_cparams("parallel"),
        name="moe_combine",
    )(*args)


def _cast_kernel(x_ref, o_ref):
    o_ref[...] = x_ref[...].astype(o_ref.dtype)


def _cast_bf16(w, rows):
    cols = w.shape[-1]
    w2 = w.reshape(-1, cols)
    out = pl.pallas_call(
        _cast_kernel,
        grid=(w2.shape[0] // rows,),
        in_specs=[pl.BlockSpec((rows, cols), lambda i: (i, 0))],
        out_specs=pl.BlockSpec((rows, cols), lambda i: (i, 0)),
        out_shape=jax.ShapeDtypeStruct(w2.shape, BF16),
        compiler_params=_cparams("parallel"),
        name="cast_bf16",
    )(w2)
    return out.reshape(w.shape)


def _moe(x, ha, hb, sel, tcnt, wg, wu, wd):
    T, D = x.shape
    n_rows = TOP_K * T + N_EXPERTS * MOE_TILE
    n_tiles = n_rows // MOE_TILE
    pos, meta = _moe_rank(sel, tcnt, WIDE_ROW_TILE, 256)
    idx = jnp.concatenate([pos[:, 0, :].reshape(1, T), pos[:, 1, :].reshape(1, T)], axis=1)
    te = meta[:n_tiles, 0].astype(jnp.int32)
    nv = meta[:n_tiles, 1].astype(jnp.int32)
    na = jnp.sum(meta[:n_tiles, 2]).astype(jnp.int32).reshape(1)
    te = jnp.where(jnp.arange(n_tiles) < na[0], te, te[na[0] - 1])
    xa = _sc_scatter_rows(ha, idx, n_rows)
    xb = _sc_scatter_rows(hb, idx, n_rows)
    ya, yb = _moe_group(te, nv, na, xa, xb, wg, wu, wd)
    ga = _sc_gather_rows(ya, idx)
    gb = _sc_gather_rows(yb, idx)
    out = _moe_combine_half(x, sel, ga, 0, None, WIDE_ROW_TILE)
    return _moe_combine_half(x, sel, gb, 1, out, WIDE_ROW_TILE)


def _split_w_in(w_in):
    sizes = (512, 512, 512, 512, 768, 768, 768, 256, 256, 512, 512, 4, 4, 1024, 1024, 1024)
    offs = [0]
    for s in sizes:
        offs.append(offs[-1] + s)
    part = lambda i: w_in[..., offs[i]:offs[i + 1]]
    rq, rk, rv, rg, dq, dk, dv, mq, mk, mv, mo, mi, mf, g_ret, g_da, g_ml = (part(i) for i in range(16))
    cols = []
    for p in range(2):
        for g in range(3):
            for t in (dq, dk, dv):
                cols.append(t[..., g * 256 + p * 128:g * 256 + (p + 1) * 128])
    wb = jnp.concatenate([rq, rk, rv, rg, g_ret, g_da, g_ml, mq, mk, mv, mo] + cols,
                         axis=-1).astype(BF16)
    pad = jnp.zeros(w_in.shape[:-1] + (LANES - 8,), w_in.dtype)
    wf = jnp.concatenate([mi, mf, pad], axis=-1).astype(BF16)
    return wb, wf


def _rope_tables(S):
    half = LANES // 2
    inv = jnp.power(ROPE_BASE, -jnp.arange(half, dtype=F32) / half)
    ang = jnp.arange(S, dtype=F32)[:, None] * inv[None, :]
    cos = jnp.cos(ang)
    sin = jnp.sin(ang)
    return jnp.concatenate([cos, cos], axis=1), jnp.concatenate([-sin, sin], axis=1)


def kernel(x, norm1_w, w_in, ret_norm_w, da_q_norm_w, da_k_norm_w, ml_conv_w, ml_i_bias, ml_f_bias,
           ml_norm_w, w_br_ret, w_br_da, w_br_ml, w_out, norm2_w, ffn_w_gate, ffn_w_up, ffn_w_down,
           moe_w_router, moe_b_router, moe_w_gate, moe_w_up, moe_w_down):
    B, S, D = x.shape
    T = B * S
    depth = w_in.shape[0]
    cos, sin = _rope_tables(S)
    wb_all, wf_all = _split_w_in(w_in)
    xt = x.reshape(T, D)
    for layer in range(depth):
        wb, wf = wb_all[layer], wf_all[layer]
        nw1 = norm1_w[layer].reshape(1, D)
        assert OFF_MLQK % INPROJ_TN == 0
        zb, zf = _inproj(xt, nw1, wb, wf, ml_conv_w[layer], OFF_MLQK // INPROJ_TN, S, INPROJ_TN)
        o_ret = _retention(zb, cos, sin, ret_norm_w[layer].reshape(1, -1), B, S)
        wq = jnp.tile(da_q_norm_w[layer], 2).reshape(1, LANES)
        wk = jnp.tile(da_k_norm_w[layer], 2).reshape(1, LANES)
        o_da = _dilated_attention(zb, OFF_DA // LANES, wq, wk, B, S)
        gate_bias = jnp.concatenate([ml_i_bias[layer], ml_f_bias[layer],
                                     jnp.zeros((LANES - 2 * ML_HEADS,), F32)]).reshape(1, LANES)
        o_ml = _mlstm(zb, zf, gate_bias, ml_norm_w[layer].reshape(1, -1), B, S,
                      qk_blk=OFF_MLQK // (ML_HEADS * LANES), gate_blk=0)
        merge_args = (xt, o_ret, o_da, o_ml, zb, OFF_GATES // D_MODEL,
                      w_br_ret[layer].astype(BF16), w_br_da[layer].astype(BF16),
                      w_br_ml[layer].astype(BF16), w_out[layer].astype(BF16),
                      norm2_w[layer].reshape(1, D))
        j = layer // 2
        if layer % 2 == 0:
            xt = _merge_ffn(*merge_args, ffn_w_gate[j].astype(BF16), ffn_w_up[j].astype(BF16),
                            ffn_w_down[j].astype(BF16), ROW_TILE)
        else:
            xt, ha, hb, sel, tcnt = _merge_router(*merge_args, moe_w_router[j], moe_b_router[j],
                                                  ROW_TILE)
            xt = _moe(xt, ha, hb, sel, tcnt, _cast_bf16(moe_w_gate[j], 512), _cast_bf16(moe_w_up[j], 512),
                      _cast_bf16(moe_w_down[j], 2048))
    return xt.reshape(B, S, D)
```

```python
import functools
import math

import jax
import jax.numpy as jnp
from jax import lax
from jax.experimental import pallas as pl
from jax.experimental.pallas import tpu as pltpu
from jax.experimental.pallas import tpu_sc as plsc

F32 = jnp.float32
BF16 = jnp.bfloat16

EPS = 1e-6
D_MODEL = 1024
CHUNK = 128
LANES = 128
MXU_N = 256
ROPE_BASE = 10000.0
RET_HEADS = 4
RET_SEQS = 2
DA_GROUPS = ((1, 16), (4, 4), (16, 1))
DA_DH = 64
DA_UNROLL = 4
DA_MAX_STRIDE = 4
ML_HEADS = 4
ML_DK = 64
ML_SEQS = 2
N_EXPERTS = 8
TOP_K = 2
MOE_TILE = 512
SC_WINDOW = 128
INPROJ_TN = 1280
ROW_TILE = 512
WIDE_ROW_TILE = 1024
OFF_GATES = 2048
OFF_MLQK = OFF_GATES + 3 * D_MODEL
OFF_DA = OFF_MLQK + 3 * ML_HEADS * LANES
VMEM_LIMIT = 56 * 1024 * 1024

NEG = -1e30


def _cparams(*sem):
    return pltpu.CompilerParams(dimension_semantics=sem, vmem_limit_bytes=VMEM_LIMIT)


def _dot(a, b):
    return jnp.dot(a, b, preferred_element_type=F32)


def _dot_nt(a, b):
    return lax.dot_general(a, b, (((1,), (1,)), ((), ())), preferred_element_type=F32)


def _dot_tn(a, b):
    return lax.dot_general(a, b, (((0,), (0,)), ((), ())), preferred_element_type=F32)


def _sigmoid(x):
    return 1.0 / (1.0 + jnp.exp(-x))


def _silu(x):
    return x * _sigmoid(x)


def _inproj_kernel(conv_blk, x_ref, nw_ref, wb_ref, wf_ref, cw_ref, zb_ref, zf_ref, h_ref, cv_ref):
    j = pl.program_id(1)
    tm = x_ref.shape[0]

    @pl.when(j == 0)
    def _():
        rc = 512
        for r in range(tm // rc):
            rows = pl.ds(r * rc, rc)
            x = x_ref[rows, :]
            ms = jnp.mean(x * x, axis=-1, keepdims=True)
            h = (x * lax.rsqrt(ms + EPS) * nw_ref[...]).astype(BF16)
            h_ref[rows, :] = h
            zb_ref[rows, :] = _dot(h, wb_ref[...]).astype(zb_ref.dtype)
            zf_ref[rows, :] = _dot(h, wf_ref[...])

    @pl.when((j > 0) & (j != conv_blk))
    def _():
        zb_ref[...] = _dot(h_ref[...], wb_ref[...]).astype(zb_ref.dtype)

    @pl.when(j == conv_blk)
    def _():
        h = h_ref[...]
        cw = cw_ref.shape[1]
        pad = cv_ref.shape[0] - tm
        cv_ref[0:pad, :] = jnp.zeros((pad, cw), F32)
        cv_ref[pad:pad + tm, :] = _dot(h, wb_ref[:, 0:cw])
        taps = cw_ref.shape[0]
        rb = 256

        def conv(piece, cs):
            cols = slice(cs * LANES, (cs + 1) * LANES)
            r0 = pad + piece * rb - (taps - 1)
            acc = cw_ref[0:1, cols] * cv_ref[pl.ds(r0, rb), cols]
            for i in range(1, taps):
                acc = acc + cw_ref[i:i + 1, cols] * cv_ref[pl.ds(r0 + i, rb), cols]
            scale = ML_DK ** -0.5 if cs >= cw // LANES // 2 else 1.0
            zb_ref[piece * rb:(piece + 1) * rb, cols] = (_silu(acc) * scale).astype(zb_ref.dtype)

        todo = [(piece, cs) for piece in range(tm // rb) for cs in range(cw // LANES)]
        chunks = list(range(cw, zb_ref.shape[1], MXU_N))
        per = -(-len(todo) // len(chunks))
        for n, c0 in enumerate(chunks):
            c1 = min(c0 + MXU_N, zb_ref.shape[1])
            zb_ref[:, c0:c1] = _dot(h, wb_ref[:, c0:c1]).astype(zb_ref.dtype)
            for piece, cs in todo[n * per:(n + 1) * per]:
                conv(piece, cs)


def _inproj(x, nw, wb, wf, conv_w, conv_blk, tm, tnb):
    T, D = x.shape
    return pl.pallas_call(
        functools.partial(_inproj_kernel, conv_blk),
        grid=(T // tm, wb.shape[1] // tnb),
        in_specs=[pl.BlockSpec((tm, D), lambda i, j: (i, 0)),
                  pl.BlockSpec((1, D), lambda i, j: (0, 0)),
                  pl.BlockSpec((D, tnb), lambda i, j: (0, j)),
                  pl.BlockSpec(wf.shape, lambda i, j: (0, 0)),
                  pl.BlockSpec(conv_w.shape, lambda i, j: (0, 0))],
        out_specs=[pl.BlockSpec((tm, tnb), lambda i, j: (i, j)),
                   pl.BlockSpec((tm, wf.shape[1]), lambda i, j: (i, 0))],
        out_shape=[jax.ShapeDtypeStruct((T, wb.shape[1]), BF16),
                   jax.ShapeDtypeStruct((T, wf.shape[1]), F32)],
        scratch_shapes=[pltpu.VMEM((tm, D), BF16), pltpu.VMEM((tm + 8, conv_w.shape[1]), F32)],
        compiler_params=_cparams("parallel", "arbitrary"),
        name="inproj",
    )(x, nw, wb, wf, conv_w)


def _retention_kernel(z_ref, cos_ref, sin_ref, nw_ref, o_ref, dec_ref, st_ref):
    S = z_ref.shape[0] // RET_SEQS
    n_chunks = S // CHUNK
    H = RET_HEADS
    HW = H * LANES
    row = lax.broadcasted_iota(jnp.int32, (CHUNK, CHUNK), 0).astype(F32)
    col = lax.broadcasted_iota(jnp.int32, (CHUNK, CHUNK), 1).astype(F32)
    lgs = [math.log1p(-(2.0 ** (-5.0 - h))) for h in range(H)]
    for h, lg in enumerate(lgs):
        rel = row - col
        dec_ref[h] = jnp.where(rel >= 0, jnp.exp(lg * jnp.maximum(rel, 0.0)), 0.0)
        dec_ref[H + h] = jnp.exp(lg * (row + 1.0))
        dec_ref[2 * H + h] = jnp.exp(lg * (CHUNK - 1.0 - row)) * (LANES ** -0.5)
    st_ref[...] = jnp.zeros(st_ref.shape, F32)
    units = [(sq, h) for sq in range(RET_SEQS) for h in range(H)]

    def body(n, carry):
        r0 = pl.multiple_of(n * CHUNK, CHUNK)
        cos = cos_ref[pl.ds(r0, CHUNK), :]
        sin = sin_ref[pl.ds(r0, CHUNK), :]
        rows_of = [pl.ds(pl.multiple_of(sq * S + n * CHUNK, CHUNK), CHUNK) for sq in range(RET_SEQS)]
        vs, ss, iqs, kvs = [], [], [], []
        for u, (sq, h) in enumerate(units):
            rows = rows_of[sq]
            q = z_ref[rows, h * LANES:(h + 1) * LANES].astype(F32)
            k = z_ref[rows, HW + h * LANES:HW + (h + 1) * LANES].astype(F32)
            v = z_ref[rows, 2 * HW + h * LANES:2 * HW + (h + 1) * LANES]
            q = q * cos + pltpu.roll(q, LANES // 2, 1) * sin
            k = k * cos + pltpu.roll(k, LANES // 2, 1) * sin
            qb = q.astype(BF16)
            kb = (k * (LANES ** -0.5)).astype(BF16)
            kd = (k * dec_ref[2 * H + h]).astype(BF16)
            ss.append(_dot_nt(qb, kb))
            iqs.append(_dot(qb, st_ref[u].astype(BF16)))
            kvs.append(_dot_tn(kd, v))
            vs.append(v)
        ps = [(ss[u] * dec_ref[h]).astype(BF16) for u, (_, h) in enumerate(units)]
        os_ = [_dot(ps[u], vs[u]) + iqs[u] * dec_ref[H + h] for u, (_, h) in enumerate(units)]
        for u, (sq, h) in enumerate(units):
            rows = rows_of[sq]
            st_ref[u] = st_ref[u] * math.exp(lgs[h] * CHUNK) + kvs[u]
            o = os_[u]
            ms = jnp.mean(o * o, axis=-1, keepdims=True)
            o = o * lax.rsqrt(ms + EPS) * nw_ref[:, h * LANES:(h + 1) * LANES]
            g = z_ref[rows, 3 * HW + h * LANES:3 * HW + (h + 1) * LANES].astype(F32)
            o_ref[rows, h * LANES:(h + 1) * LANES] = (o * _silu(g)).astype(o_ref.dtype)
        return carry

    lax.fori_loop(0, n_chunks, body, 0)


def _retention(zb, cos, sin, nw, B, S):
    T = B * S
    W = RET_HEADS * LANES
    R = RET_SEQS * S
    return pl.pallas_call(
        _retention_kernel,
        grid=(B // RET_SEQS,),
        in_specs=[pl.BlockSpec((R, 4 * W), lambda b: (b, 0)),
                  pl.BlockSpec((S, LANES), lambda b: (0, 0)),
                  pl.BlockSpec((S, LANES), lambda b: (0, 0)),
                  pl.BlockSpec((1, W), lambda b: (0, 0))],
        out_specs=pl.BlockSpec((R, W), lambda b: (b, 0)),
        out_shape=jax.ShapeDtypeStruct((T, W), BF16),
        scratch_shapes=[pltpu.VMEM((3 * RET_HEADS, CHUNK, LANES), F32),
                        pltpu.VMEM((RET_SEQS * RET_HEADS, LANES, LANES), F32)],
        compiler_params=_cparams("parallel"),
        name="retention",
    )(zb, cos, sin, nw)


def _da_kernel(*refs):
    z_refs = refs[:9]
    wq_ref, wk_ref, o_ref, qn_ref, kn_ref, v_ref, op_ref, zp_ref = refs[9:17]
    on_refs, zn_refs, spare_refs = refs[17:20], refs[20:23], refs[23:26]
    stage_refs = refs[26:29]
    S = o_ref.shape[0]
    lane = lax.broadcasted_iota(jnp.int32, (1, LANES), 1)
    lo = lane < DA_DH
    row = lax.broadcasted_iota(jnp.int32, (CHUNK, CHUNK), 0)
    col = lax.broadcasted_iota(jnp.int32, (CHUNK, CHUNK), 1)
    mask_cur = col <= row
    row2 = lax.broadcasted_iota(jnp.int32, (CHUNK, 2 * CHUNK), 0)
    col2 = lax.broadcasted_iota(jnp.int32, (CHUNK, 2 * CHUNK), 1)
    mask_band = (col2 >= row2) & (col2 <= row2 + CHUNK)

    seg = (lax.broadcasted_iota(jnp.int32, (LANES, LANES), 0) // DA_DH
           == lax.broadcasted_iota(jnp.int32, (LANES, LANES), 1) // DA_DH).astype(BF16) * (1.0 / DA_DH)

    def head_norm(x, w):
        x2 = x * x
        hi = x2.astype(BF16)
        lo_part = (x2 - hi.astype(F32)).astype(BF16)
        ms = _dot(hi, seg) + _dot(lo_part, seg)
        return x * lax.rsqrt(ms + EPS) * w

    RB = 256
    for g, (dil, nb) in enumerate(DA_GROUPS):
        L = S // dil
        zq_ref, zk_ref, zv_ref = z_refs[3 * g:3 * g + 3]
        if dil > 1:
            for piece in range(S // RB):
                rows = pl.ds(piece * RB, RB)
                for src_ref, dst_ref in zip((zq_ref, zk_ref, zv_ref), stage_refs):
                    dst_ref[rows, :] = src_ref[rows, :].astype(F32)
            zq_ref, zk_ref, zv_ref = stage_refs
        first = min(dil, DA_MAX_STRIDE)
        if dil > first:
            mid_refs = (on_refs[g], zn_refs[g], spare_refs[g])
            sub = S // first
            for b in range(first):
                for piece in range(sub // RB):
                    src = pl.ds(b + first * piece * RB, RB, stride=first)
                    dst = pl.ds(b * sub + piece * RB, RB)
                    for src_ref, dst_ref in zip((zq_ref, zk_ref, zv_ref), mid_refs):
                        dst_ref[dst, :] = src_ref[src, :]
            zq_ref, zk_ref, zv_ref = mid_refs
        rest = dil // first

        def source_rows(rho, piece, n_rows):
            if dil == 1:
                return pl.ds(piece * n_rows, n_rows)
            b, a = rho % first, rho // first
            if rest == 1:
                return pl.ds(b + first * piece * n_rows, n_rows, stride=first)
            return pl.ds(b * (S // first) + a + rest * piece * n_rows, n_rows, stride=rest)

        for rho in range(dil):
            for piece in range(max(L // RB, 1)):
                n_rows = min(RB, L)
                src = source_rows(rho, piece, n_rows)
                dst = pl.ds(rho * L + piece * n_rows, n_rows)
                q = zq_ref[src, :].astype(F32)
                k = zk_ref[src, :].astype(F32)
                v = zv_ref[src, :]
                qn_ref[dst, :] = (head_norm(q, wq_ref[...]) * (DA_DH ** -0.5)).astype(BF16)
                kn_ref[dst, :] = head_norm(k, wk_ref[...]).astype(BF16)
                v_ref[dst, :] = v.astype(BF16)

        def batch(r0, chained, first_has_prev):
            chains = []
            for u in range(DA_UNROLL):
                rows = pl.ds(r0 + u * CHUNK, CHUNK)
                if chained and (u > 0 or first_has_prev):
                    keys, mask = pl.ds(r0 + (u - 1) * CHUNK, 2 * CHUNK), mask_band
                else:
                    keys, mask = rows, mask_cur
                k = kn_ref[keys, :]
                v = v_ref[keys, :]
                vaug = jnp.concatenate([v, jnp.ones_like(v)], axis=1)
                q = qn_ref[rows, :]
                zero = jnp.zeros_like(q)
                for qh in (jnp.where(lo, q, zero), jnp.where(lo, zero, q)):
                    chains.append((_dot_nt(qh, k), mask, vaug))
            probs = []
            for s, mask, _ in chains:
                s = jnp.where(mask, s, NEG)
                m = jnp.max(s, axis=-1, keepdims=True)
                probs.append((jnp.exp(s - m).astype(BF16), m))
            accs = [_dot(p, vaug) for (p, _), (_, _, vaug) in zip(probs, chains)]
            normed, lses = [], []
            for u in range(DA_UNROLL):
                a0, a1 = accs[2 * u], accs[2 * u + 1]
                den = jnp.where(lo, a0[:, LANES:], a1[:, LANES:])
                normed.append(jnp.where(lo, a0[:, :LANES], a1[:, :LANES]) / den)
                lses.append(jnp.where(lo, probs[2 * u][1], probs[2 * u + 1][1]) + jnp.log(den))
            rows = pl.ds(r0, DA_UNROLL * CHUNK)
            op_ref[rows, :] = jnp.concatenate(normed, axis=0)
            zp_ref[rows, :] = jnp.concatenate(lses, axis=0)

        span = DA_UNROLL * CHUNK
        if nb == 1:
            def singles(i, c):
                batch(pl.multiple_of(i * span, span), False, False)
                return c
            lax.fori_loop(0, S // span, singles, 0)
        else:
            def segment(sgi, c, nb=nb):
                base = pl.multiple_of(sgi * (nb * CHUNK), span)
                batch(base, True, False)
                if nb > DA_UNROLL:
                    def inner(n, c2):
                        batch(pl.multiple_of(base + n * span, span), True, True)
                        return c2
                    lax.fori_loop(1, nb // DA_UNROLL, inner, 0)
                return c
            lax.fori_loop(0, dil, segment, 0)

        outs = ((op_ref, on_refs[g]), (zp_ref, zn_refs[g]))
        for rho in range(dil):
            for piece in range(max(L // RB, 1)):
                n_rows = min(RB, L)
                dst = source_rows(rho, piece, n_rows)
                src = pl.ds(rho * L + piece * n_rows, n_rows)
                for k_out, (perm_ref, nat_ref) in enumerate(outs):
                    (stage_refs[k_out] if rest > 1 else nat_ref)[dst, :] = perm_ref[src, :]
        if rest > 1:
            sub = S // first
            for b in range(first):
                for piece in range(sub // RB):
                    dst = pl.ds(b + first * piece * RB, RB, stride=first)
                    src = pl.ds(b * sub + piece * RB, RB)
                    for k_out, (_, nat_ref) in enumerate(outs):
                        nat_ref[dst, :] = stage_refs[k_out][src, :]

    for piece in range(S // RB):
        rows = pl.ds(piece * RB, RB)
        m = jnp.maximum(jnp.maximum(zn_refs[0][rows, :], zn_refs[1][rows, :]), zn_refs[2][rows, :])
        num = jnp.zeros((RB, LANES), F32)
        den = jnp.zeros((RB, LANES), F32)
        for g in range(3):
            e = jnp.exp(zn_refs[g][rows, :] - m)
            num = num + e * on_refs[g][rows, :]
            den = den + e
        o_ref[rows, :] = (num / den).astype(o_ref.dtype)


def _dilated_attention(zb, slab0, wq, wk, B, S):
    T = B * S
    slab = lambda k: pl.BlockSpec((S, LANES), lambda b, p: (b, slab0 + 9 * p + k))
    return pl.pallas_call(
        _da_kernel,
        grid=(B, 2),
        in_specs=[slab(k) for k in range(9)] + [
                  pl.BlockSpec((1, LANES), lambda b, p: (0, 0)),
                  pl.BlockSpec((1, LANES), lambda b, p: (0, 0))],
        out_specs=pl.BlockSpec((S, LANES), lambda b, p: (b, p)),
        out_shape=jax.ShapeDtypeStruct((T, 2 * LANES), BF16),
        scratch_shapes=[pltpu.VMEM((S, LANES), BF16), pltpu.VMEM((S, LANES), BF16),
                        pltpu.VMEM((S, LANES), BF16),
                        pltpu.VMEM((S, LANES), F32), pltpu.VMEM((S, LANES), F32)]
                       + [pltpu.VMEM((S, LANES), F32)] * 12,
        compiler_params=_cparams("parallel", "arbitrary"),
        name="dilated_attention",
    )(*([zb] * 9), wq, wk)


def _mlstm_kernel(qk_ref, v_ref, og_ref, gate_ref, gb_ref, nw_ref, o_ref, st_ref, e_ref, et_ref):
    S = qk_ref.shape[0] // ML_SEQS
    n_chunks = S // CHUNK
    W = qk_ref.shape[1]
    lane = lax.broadcasted_iota(jnp.int32, (1, LANES), 1)
    lo = lane < ML_DK
    row = lax.broadcasted_iota(jnp.int32, (CHUNK, CHUNK), 0)
    col = lax.broadcasted_iota(jnp.int32, (CHUNK, CHUNK), 1)
    causal = col <= row
    tril = causal.astype(F32)
    srow = lax.broadcasted_iota(jnp.int32, (LANES, 1), 0) < ML_DK
    st_ref[...] = jnp.zeros(st_ref.shape, F32)

    is_f = (lane >= ML_HEADS) & (lane < 2 * ML_HEADS)
    for n in range(ML_SEQS * n_chunks):
        rows = pl.ds(n * CHUNK, CHUNK)
        gp = gate_ref[rows, :] + gb_ref[...]
        logf = jnp.minimum(gp, 0.0) - jnp.log(1.0 + jnp.exp(-jnp.abs(gp)))
        cum = jnp.dot(tril, jnp.where(is_f, logf, 0.0), preferred_element_type=F32,
                      precision=lax.Precision.HIGHEST)
        e = jnp.where(lane < ML_HEADS, gp, cum)
        e_ref[rows, :] = e
        et_ref[n] = e.T[0:2 * ML_HEADS, :]

    units = [(sq, h) for sq in range(ML_SEQS) for h in range(ML_HEADS)]

    def chunk(n, mms):
        rows_of = [pl.ds(pl.multiple_of(sq * S + n * CHUNK, CHUNK), CHUNK) for sq in range(ML_SEQS)]
        es = [e_ref[rows_of[sq], :] for sq in range(ML_SEQS)]
        ets = [et_ref[sq * n_chunks + n] for sq in range(ML_SEQS)]
        khs, vaugs, s_raws, iqs = [], [], [], []
        for sq in range(ML_SEQS):
            rows = rows_of[sq]
            for pair in range(2):
                q2 = qk_ref[rows, pair * LANES:(pair + 1) * LANES]
                k2 = qk_ref[rows, W // 2 + pair * LANES:W // 2 + (pair + 1) * LANES]
                zero = jnp.zeros_like(q2)
                stb = st_ref[2 * sq + pair].astype(BF16)
                for hs in range(2):
                    h = 2 * pair + hs
                    qh = jnp.where(lo, q2, zero) if hs == 0 else jnp.where(lo, zero, q2)
                    kh = jnp.where(lo, k2, zero) if hs == 0 else jnp.where(lo, zero, k2)
                    vh = v_ref[rows, h * LANES:(h + 1) * LANES]
                    s_raws.append(_dot_nt(qh, k2))
                    iqs.append(_dot(qh, stb))
                    khs.append(kh)
                    vaugs.append(jnp.concatenate([vh, jnp.ones_like(vh)], axis=-1))
        ps, wis, ms_, kws, wcs, new_mms = [], [], [], [], [], []
        for u, (sq, h) in enumerate(units):
            e, et = es[sq], ets[sq]
            mm = mms[u]
            i_b = jnp.broadcast_to(e[:, h:h + 1], (CHUNK, LANES))
            a_b = jnp.broadcast_to(e[:, ML_HEADS + h:ML_HEADS + h + 1], (CHUNK, LANES))
            c_row = et[h:h + 1, :] - et[ML_HEADS + h:ML_HEADS + h + 1, :]
            dmat = jnp.where(causal, a_b + c_row, NEG)
            inter = a_b + mm
            m = jnp.maximum(inter, jnp.max(dmat, axis=-1, keepdims=True))
            ps.append((s_raws[u] * jnp.exp(dmat - m)).astype(BF16))
            wis.append(jnp.exp(inter - m))
            ms_.append(m)
            m_new = m[CHUNK - 1:CHUNK, :]
            a_last = a_b[CHUNK - 1:CHUNK, :]
            wk = jnp.exp(a_last - a_b + i_b - m_new)
            kws.append((khs[u].astype(F32) * wk).astype(BF16))
            wc = jnp.exp(a_last + mm - m_new)
            wcs.append(jnp.concatenate([wc, wc], axis=1))
            new_mms.append(m_new)
        accs = [_dot(ps[u], vaugs[u]) + jnp.concatenate([wis[u], wis[u]], axis=1) * iqs[u]
                for u in range(len(units))]
        upds = [_dot_tn(kws[u], vaugs[u]) for u in range(len(units))]
        for sp in range(2 * ML_SEQS):
            u0, u1 = 2 * sp, 2 * sp + 1
            st_ref[sp] = jnp.where(srow, wcs[u0], wcs[u1]) * st_ref[sp] + (upds[u0] + upds[u1])
        for u, (sq, h) in enumerate(units):
            num = accs[u][:, :LANES]
            den = accs[u][:, LANES:]
            hv = num / jnp.maximum(jnp.abs(den), jnp.exp(-ms_[u]))
            var = jnp.mean(hv * hv, axis=-1, keepdims=True)
            hv = hv * lax.rsqrt(var + EPS) * nw_ref[:, h * LANES:(h + 1) * LANES]
            og = og_ref[rows_of[sq], h * LANES:(h + 1) * LANES].astype(F32)
            o_ref[rows_of[sq], h * LANES:(h + 1) * LANES] = (hv * _sigmoid(og)).astype(o_ref.dtype)
        return tuple(new_mms)

    lax.fori_loop(0, n_chunks, chunk, tuple(jnp.zeros((1, LANES), F32) for _ in units))


def _mlstm(zb, zf, gate_bias, nw, B, S, qk_blk, gate_blk):
    T = B * S
    W = ML_HEADS * LANES
    R = ML_SEQS * S
    return pl.pallas_call(
        _mlstm_kernel,
        grid=(B // ML_SEQS,),
        in_specs=[pl.BlockSpec((R, W), lambda b: (b, qk_blk)),
                  pl.BlockSpec((R, W), lambda b: (b, qk_blk + 1)),
                  pl.BlockSpec((R, W), lambda b: (b, qk_blk + 2)),
                  pl.BlockSpec((R, LANES), lambda b: (b, gate_blk)),
                  pl.BlockSpec((1, LANES), lambda b: (0, 0)),
                  pl.BlockSpec((1, W), lambda b: (0, 0))],
        out_specs=pl.BlockSpec((R, W), lambda b: (b, 0)),
        out_shape=jax.ShapeDtypeStruct((T, W), BF16),
        scratch_shapes=[pltpu.VMEM((2 * ML_SEQS, LANES, 2 * LANES), F32),
                        pltpu.VMEM((R, LANES), F32),
                        pltpu.VMEM((R // CHUNK, 2 * ML_HEADS, LANES), F32)],
        compiler_params=_cparams("parallel"),
        name="mlstm",
    )(zb, zb, zb, zf, gate_bias, nw)


def _merge_value(x_ref, oret_ref, oda_ref, oml_ref, gr_ref, gd_ref, gm_ref,
                 wr_ref, wd_ref, wm_ref, wo_ref):
    y = _sigmoid(gr_ref[...].astype(F32)) * _dot(oret_ref[...], wr_ref[...])
    y = y + _sigmoid(gd_ref[...].astype(F32)) * _dot(oda_ref[...], wd_ref[...])
    y = y + _sigmoid(gm_ref[...].astype(F32)) * _dot(oml_ref[...], wm_ref[...])
    return x_ref[...] + _dot(y.astype(BF16), wo_ref[...])


def _merge_specs(x, o_ret, o_da, o_ml, g_blk, tm):
    D = x.shape[1]
    return [pl.BlockSpec((tm, D), lambda i: (i, 0)),
            pl.BlockSpec((tm, o_ret.shape[1]), lambda i: (i, 0)),
            pl.BlockSpec((tm, o_da.shape[1]), lambda i: (i, 0)),
            pl.BlockSpec((tm, o_ml.shape[1]), lambda i: (i, 0)),
            pl.BlockSpec((tm, D), lambda i: (i, g_blk)),
            pl.BlockSpec((tm, D), lambda i: (i, g_blk + 1)),
            pl.BlockSpec((tm, D), lambda i: (i, g_blk + 2))]


def _resident(a):
    return pl.BlockSpec(a.shape, lambda i: (0,) * a.ndim, pipeline_mode=pl.Buffered(1))


def _swiglu_block(h, wg_ref, wu_ref, wd_ref, a_ref):
    tf = wg_ref.shape[1]
    c0 = 0
    while c0 < tf:
        w = min(MXU_N, tf - c0)
        g = _dot(h, wg_ref[:, c0:c0 + w].astype(BF16))
        u = _dot(h, wu_ref[:, c0:c0 + w].astype(BF16))
        a_ref[:, c0:c0 + w] = (_silu(g) * u).astype(BF16)
        c0 += w
    return _dot(a_ref[...], wd_ref[...].astype(BF16))


def _merge_ffn_kernel(*refs):
    merge_refs, (nw_ref, wg_ref, wu_ref, wdn_ref, o_ref, a_ref) = refs[:11], refs[11:]
    x = _merge_value(*merge_refs)
    ms = jnp.mean(x * x, axis=-1, keepdims=True)
    h = (x * lax.rsqrt(ms + EPS) * nw_ref[...]).astype(BF16)
    o_ref[...] = x + _swiglu_block(h, wg_ref, wu_ref, wdn_ref, a_ref)


def _merge_ffn(x, o_ret, o_da, o_ml, zb, g_blk, wr, wd, wm, wo, nw, wg, wu, wdn, tm):
    T, D = x.shape
    weights = (wr, wd, wm, wo, nw, wg, wu, wdn)
    return pl.pallas_call(
        _merge_ffn_kernel,
        grid=(T // tm,),
        in_specs=_merge_specs(x, o_ret, o_da, o_ml, g_blk, tm) + [_resident(a) for a in weights],
        out_specs=pl.BlockSpec((tm, D), lambda i: (i, 0)),
        out_shape=jax.ShapeDtypeStruct((T, D), F32),
        scratch_shapes=[pltpu.VMEM((tm, wg.shape[1]), BF16)],
        compiler_params=_cparams("parallel"),
        name="merge_ffn",
    )(x, o_ret, o_da, o_ml, zb, zb, zb, *weights)


HI16 = 0xFFFF0000


def _pack_bf16_pairs(a, b):
    ab = pltpu.bitcast(a.astype(BF16).astype(F32), jnp.uint32)
    bb = pltpu.bitcast(b.astype(BF16).astype(F32), jnp.uint32)
    return (ab >> 16) | (bb & jnp.uint32(HI16))


def _unpack_bf16_pairs(p):
    lo = pltpu.bitcast(p << 16, F32)
    hi = pltpu.bitcast(p & jnp.uint32(HI16), F32)
    return lo, hi


def _merge_router_kernel(*refs):
    merge_refs, (nw_ref, wr_ref, br_ref, xo_ref, ha_ref, hb_ref, sel_ref, cnt_ref) = refs[:11], refs[11:]
    x = _merge_value(*merge_refs)
    xo_ref[...] = x
    ms = jnp.mean(x * x, axis=-1, keepdims=True)
    h = x * lax.rsqrt(ms + EPS) * nw_ref[...]
    Q = h.shape[1] // 4
    ha_ref[...] = _pack_bf16_pairs(h[:, 0:Q], h[:, Q:2 * Q])
    hb_ref[...] = _pack_bf16_pairs(h[:, 2 * Q:3 * Q], h[:, 3 * Q:4 * Q])
    w = wr_ref[...]
    h_hi = h.astype(BF16)
    h_lo = (h - h_hi.astype(F32)).astype(BF16)
    w_hi = w.astype(BF16)
    w_lo = (w - w_hi.astype(F32)).astype(BF16)
    logits = _dot(h_hi, w_hi) + _dot(h_lo, w_hi) + _dot(h_hi, w_lo) + br_ref[...]
    lane = lax.broadcasted_iota(jnp.int32, logits.shape, 1).astype(F32)
    logits = jnp.where(lane < N_EXPERTS, logits, NEG)
    m1 = jnp.max(logits, axis=-1, keepdims=True)
    i1 = jnp.min(jnp.where(logits == m1, lane, float(LANES)), axis=-1, keepdims=True)
    rest = jnp.where(lane == i1, NEG, logits)
    m2 = jnp.max(rest, axis=-1, keepdims=True)
    i2 = jnp.min(jnp.where(rest == m2, lane, float(LANES)), axis=-1, keepdims=True)
    e2 = jnp.exp(m2 - m1)
    p1 = 1.0 / (1.0 + e2)
    p2 = e2 / (1.0 + e2)
    sel_ref[...] = jnp.where(lane == 0.0, i1, jnp.where(lane == 1.0, i2,
                             jnp.where(lane == 2.0, p1, jnp.where(lane == 3.0, p2, 0.0))))
    chosen = jnp.where((lane == i1) | (lane == i2), 1.0, 0.0)
    cnt_ref[...] = jnp.broadcast_to(jnp.sum(chosen, axis=0, keepdims=True), cnt_ref.shape)


def _merge_router(x, o_ret, o_da, o_ml, zb, g_blk, wr, wd, wm, wo, nw, w_router, b_router, tm):
    T, D = x.shape
    Q = D // 4
    wrt = jnp.pad(w_router, ((0, 0), (0, LANES - N_EXPERTS)))
    brt = jnp.pad(b_router, (0, LANES - N_EXPERTS)).reshape(1, LANES)
    weights = (wr, wd, wm, wo, nw, wrt, brt)
    return pl.pallas_call(
        _merge_router_kernel,
        grid=(T // tm,),
        in_specs=_merge_specs(x, o_ret, o_da, o_ml, g_blk, tm) + [_resident(a) for a in weights],
        out_specs=[pl.BlockSpec((tm, D), lambda i: (i, 0)),
                   pl.BlockSpec((tm, Q), lambda i: (i, 0)),
                   pl.BlockSpec((tm, Q), lambda i: (i, 0)),
                   pl.BlockSpec((tm, LANES), lambda i: (i, 0)),
                   pl.BlockSpec((None, 8, LANES), lambda i: (i, 0, 0))],
        out_shape=[jax.ShapeDtypeStruct((T, D), F32),
                   jax.ShapeDtypeStruct((T, Q), jnp.uint32), jax.ShapeDtypeStruct((T, Q), jnp.uint32),
                   jax.ShapeDtypeStruct((T, LANES), F32),
                   jax.ShapeDtypeStruct((T // tm, 8, LANES), F32)],
        compiler_params=_cparams("parallel"),
        name="merge_router",
    )(x, o_ret, o_da, o_ml, zb, zb, zb, *weights)


def _moe_rank_kernel(sel_ref, tcnt_ref, pos_ref, meta_ref, cnt_ref, offs_ref, carry_ref, before_ref):
    i = pl.program_id(0)
    tm = sel_ref.shape[0]
    lane = lax.broadcasted_iota(jnp.int32, (tm, LANES), 1).astype(F32)
    lane1 = lax.broadcasted_iota(jnp.int32, (1, LANES), 1).astype(F32)
    sel = sel_ref[...]
    i1 = sel[:, 0:1]
    i2 = sel[:, 1:2]
    onehot = jnp.where((lane == i1) | (lane == i2), 1.0, 0.0)
    colsum = jnp.sum(onehot, axis=0, keepdims=True)

    def padded_counts():
        return jnp.floor((cnt_ref[...] + (MOE_TILE - 1.0)) * (1.0 / MOE_TILE)) * MOE_TILE

    @pl.when(i == 0)
    def _():
        cnt_ref[...] = jnp.sum(tcnt_ref[...], axis=0)
        k = lax.broadcasted_iota(jnp.int32, (LANES, LANES), 0)
        e = lax.broadcasted_iota(jnp.int32, (LANES, LANES), 1)
        upper = (k < e).astype(F32)
        offs_ref[...] = jnp.dot(padded_counts(), upper, preferred_element_type=F32,
                                precision=lax.Precision.HIGHEST)
        carry_ref[...] = jnp.zeros(carry_ref.shape, F32)
        r = lax.broadcasted_iota(jnp.int32, (tm, tm), 0)
        c = lax.broadcasted_iota(jnp.int32, (tm, tm), 1)
        before_ref[...] = (c < r).astype(BF16)

    rank = _dot(before_ref[...], onehot.astype(BF16)) + carry_ref[0:1, :]
    row = rank + offs_ref[0:1, :]
    pos1 = jnp.sum(jnp.where(lane == i1, row, 0.0), axis=-1, keepdims=True)
    pos2 = jnp.sum(jnp.where(lane == i2, row, 0.0), axis=-1, keepdims=True)
    both = jnp.where(lane == 0.0, pos1, jnp.where(lane == 1.0, pos2, 0.0))
    pos_ref[...] = both.T[0:8, :].astype(jnp.int32)
    carry_ref[...] += colsum

    @pl.when(i == pl.num_programs(0) - 1)
    def _():
        nrow = meta_ref.shape[0]
        padded = padded_counts()[0:1, :]
        offs = offs_ref[0:1, :]
        ends = offs + padded
        start = lax.broadcasted_iota(jnp.int32, (nrow, 1), 0).astype(F32) * MOE_TILE
        is_e = lane1 < N_EXPERTS
        te = jnp.sum(jnp.where(is_e & (ends <= start), 1.0, 0.0), axis=-1, keepdims=True)
        te = jnp.minimum(te, N_EXPERTS - 1.0)
        valid_end = jnp.sum(jnp.where(lane1 == te, offs + cnt_ref[0:1, :], 0.0), axis=-1, keepdims=True)
        nv = jnp.clip(valid_end - start, 0.0, MOE_TILE)
        total = jnp.sum(jnp.where(is_e, padded, 0.0), axis=-1, keepdims=True)
        active = jnp.where(start < total, 1.0, 0.0)
        meta_ref[...] = jnp.where(lane1 == 0.0, te, jnp.where(lane1 == 1.0, nv,
                                  jnp.where(lane1 == 2.0, active, 0.0)))


def _moe_rank(sel, tcnt, tm, meta_rows):
    T = sel.shape[0]
    nt = T // tm
    return pl.pallas_call(
        _moe_rank_kernel,
        grid=(nt,),
        in_specs=[pl.BlockSpec((tm, LANES), lambda i: (i, 0)),
                  pl.BlockSpec(tcnt.shape, lambda i: (0, 0, 0))],
        out_specs=[pl.BlockSpec((None, 8, tm), lambda i: (i, 0, 0)),
                   pl.BlockSpec((meta_rows, LANES), lambda i: (0, 0))],
        out_shape=[jax.ShapeDtypeStruct((nt, 8, tm), jnp.int32),
                   jax.ShapeDtypeStruct((meta_rows, LANES), F32)],
        scratch_shapes=[pltpu.VMEM((8, LANES), F32), pltpu.VMEM((8, LANES), F32),
                        pltpu.VMEM((8, LANES), F32), pltpu.VMEM((tm, tm), BF16)],
        compiler_params=_cparams("arbitrary"),
        name="moe_rank",
    )(sel, tcnt)


def _sc_mesh():
    return plsc.VectorSubcoreMesh(core_axis_name="core", subcore_axis_name="subcore")


def _sc_scatter_rows(x, idx, n_rows):
    n_idx = idx.shape[1]
    nt = x.shape[0] // SC_WINDOW
    width = x.shape[1]

    @functools.partial(pl.kernel, out_type=jax.ShapeDtypeStruct((n_rows, width), x.dtype),
                       mesh=_sc_mesh())
    def scatter_kernel(x_hbm, i_hbm, o_hbm):
        def body(x_vmem, i_vmem):
            pltpu.sync_copy(x_vmem, o_hbm.at[i_vmem.at[0]])

        pltpu.emit_pipeline(
            body, grid=(n_idx // SC_WINDOW,),
            in_specs=[pl.BlockSpec((SC_WINDOW, width), lambda i: (i % nt, 0)),
                      pl.BlockSpec((1, SC_WINDOW), lambda i: (0, i))],
            out_specs=[],
            core_axis_name=("core", "subcore"), dimension_semantics=(pltpu.PARALLEL,),
        )(x_hbm, i_hbm)

    return scatter_kernel(x, idx)


def _sc_gather_rows(x, idx):
    n_idx = idx.shape[1]
    width = x.shape[1]

    @functools.partial(pl.kernel, out_type=jax.ShapeDtypeStruct((n_idx, width), x.dtype),
                       mesh=_sc_mesh())
    def gather_kernel(x_hbm, i_hbm, o_hbm):
        def body(i_vmem, o_vmem):
            pltpu.sync_copy(x_hbm.at[i_vmem.at[0]], o_vmem)

        pltpu.emit_pipeline(
            body, grid=(n_idx // SC_WINDOW,),
            in_specs=[pl.BlockSpec((1, SC_WINDOW), lambda i: (0, i))],
            out_specs=[pl.BlockSpec((SC_WINDOW, width), lambda i: (i, 0))],
            core_axis_name=("core", "subcore"), dimension_semantics=(pltpu.PARALLEL,),
        )(i_hbm, o_hbm)

    return gather_kernel(x, idx)


def _moe_group_kernel(te_ref, nv_ref, na_ref, xa_ref, xb_ref, wg_ref, wu_ref, wd_ref,
                      ya_ref, yb_ref, h_ref, a_ref):
    j = pl.program_id(0)

    @pl.when(j < na_ref[0])
    def _():
        tm = h_ref.shape[0]
        Q = xa_ref.shape[1]
        valid = lax.broadcasted_iota(jnp.int32, (tm, 1), 0) < nv_ref[j]
        for src, c0 in ((xa_ref, 0), (xb_ref, 2 * Q)):
            lo, hi = _unpack_bf16_pairs(src[...])
            h_ref[:, c0:c0 + Q] = jnp.where(valid, lo, 0.0).astype(BF16)
            h_ref[:, c0 + Q:c0 + 2 * Q] = jnp.where(valid, hi, 0.0).astype(BF16)
        y = _swiglu_block(h_ref[...], wg_ref, wu_ref, wd_ref, a_ref)
        ya_ref[...] = _pack_bf16_pairs(y[:, 0:Q], y[:, Q:2 * Q])
        yb_ref[...] = _pack_bf16_pairs(y[:, 2 * Q:3 * Q], y[:, 3 * Q:4 * Q])


def _moe_group(te, nv, na, xa, xb, wg, wu, wd):
    R, Q = xa.shape
    E, D, F = wg.shape
    tile = lambda j, te, nv, na: (jnp.minimum(j, na[0] - 1), 0)
    expert = lambda j, te, nv, na: (te[j], 0, 0)
    once = pl.Buffered(1)
    grid_spec = pltpu.PrefetchScalarGridSpec(
        num_scalar_prefetch=3,
        grid=(R // MOE_TILE,),
        in_specs=[pl.BlockSpec((MOE_TILE, Q), tile),
                  pl.BlockSpec((MOE_TILE, Q), tile),
                  pl.BlockSpec((None, D, F), expert),
                  pl.BlockSpec((None, D, F), expert),
                  pl.BlockSpec((None, F, D), expert, pipeline_mode=once)],
        out_specs=[pl.BlockSpec((MOE_TILE, Q), tile), pl.BlockSpec((MOE_TILE, Q), tile)],
        scratch_shapes=[pltpu.VMEM((MOE_TILE, D), BF16), pltpu.VMEM((MOE_TILE, F), BF16)],
    )
    return pl.pallas_call(
        _moe_group_kernel,
        grid_spec=grid_spec,
        out_shape=[jax.ShapeDtypeStruct((R, Q), jnp.uint32), jax.ShapeDtypeStruct((R, Q), jnp.uint32)],
        compiler_params=_cparams("arbitrary"),
        name="moe_experts",
    )(te, nv, na, xa, xb, wg, wu, wd)


def _moe_combine_kernel(x_ref, sel_ref, g1_ref, g2_ref, *rest):
    o_ref = rest[-1]
    sel = sel_ref[...]
    p1 = sel[:, 2:3]
    p2 = sel[:, 3:4]
    Q = g1_ref.shape[1]
    lo1, hi1 = _unpack_bf16_pairs(g1_ref[...])
    lo2, hi2 = _unpack_bf16_pairs(g2_ref[...])
    o_ref[:, 0:Q] = x_ref[:, 0:Q] + (p1 * lo1 + p2 * lo2)
    o_ref[:, Q:2 * Q] = x_ref[:, Q:2 * Q] + (p1 * hi1 + p2 * hi2)


def _moe_combine_half(x, sel, g, half, partial_out, tm):
    T, D = x.shape
    Q = g.shape[1]
    nt = T // tm
    in_specs = [pl.BlockSpec((tm, 2 * Q), lambda i: (i, half)),
                pl.BlockSpec((tm, LANES), lambda i: (i, 0)),
                pl.BlockSpec((tm, Q), lambda i: (i, 0)),
                pl.BlockSpec((tm, Q), lambda i: (i + nt, 0))]
    args = [x, sel, g, g]
    aliases = {}
    if partial_out is not None:
        in_specs.append(pl.BlockSpec(memory_space=pl.ANY))
        args.append(partial_out)
        aliases = {4: 0}
    return pl.pallas_call(
        _moe_combine_kernel,
        grid=(nt,),
        in_specs=in_specs,
        out_specs=pl.BlockSpec((tm, 2 * Q), lambda i: (i, half)),
        out_shape=jax.ShapeDtypeStruct((T, D), F32),
        input_output_aliases=aliases,
        compiler_params=_cparams("parallel"),
        name="moe_combine",
    )(*args)


def _cast_kernel(x_ref, o_ref):
    o_ref[...] = x_ref[...].astype(o_ref.dtype)


def _cast_bf16(w, rows):
    cols = w.shape[-1]
    w2 = w.reshape(-1, cols)
    out = pl.pallas_call(
        _cast_kernel,
        grid=(w2.shape[0] // rows,),
        in_specs=[pl.BlockSpec((rows, cols), lambda i: (i, 0))],
        out_specs=pl.BlockSpec((rows, cols), lambda i: (i, 0)),
        out_shape=jax.ShapeDtypeStruct(w2.shape, BF16),
        compiler_params=_cparams("parallel"),
        name="cast_bf16",
    )(w2)
    return out.reshape(w.shape)


def _moe(x, ha, hb, sel, tcnt, wg, wu, wd):
    T, D = x.shape
    n_rows = TOP_K * T + N_EXPERTS * MOE_TILE
    n_tiles = n_rows // MOE_TILE
    pos, meta = _moe_rank(sel, tcnt, WIDE_ROW_TILE, 256)
    idx = jnp.concatenate([pos[:, 0, :].reshape(1, T), pos[:, 1, :].reshape(1, T)], axis=1)
    te = meta[:n_tiles, 0].astype(jnp.int32)
    nv = meta[:n_tiles, 1].astype(jnp.int32)
    na = jnp.sum(meta[:n_tiles, 2]).astype(jnp.int32).reshape(1)
    te = jnp.where(jnp.arange(n_tiles) < na[0], te, te[na[0] - 1])
    xa = _sc_scatter_rows(ha, idx, n_rows)
    xb = _sc_scatter_rows(hb, idx, n_rows)
    ya, yb = _moe_group(te, nv, na, xa, xb, wg, wu, wd)
    ga = _sc_gather_rows(ya, idx)
    gb = _sc_gather_rows(yb, idx)
    out = _moe_combine_half(x, sel, ga, 0, None, 2 * WIDE_ROW_TILE)
    return _moe_combine_half(x, sel, gb, 1, out, 2 * WIDE_ROW_TILE)


def _split_w_in(w_in):
    sizes = (512, 512, 512, 512, 768, 768, 768, 256, 256, 512, 512, 4, 4, 1024, 1024, 1024)
    offs = [0]
    for s in sizes:
        offs.append(offs[-1] + s)
    part = lambda i: w_in[..., offs[i]:offs[i + 1]]
    rq, rk, rv, rg, dq, dk, dv, mq, mk, mv, mo, mi, mf, g_ret, g_da, g_ml = (part(i) for i in range(16))
    cols = []
    for p in range(2):
        for g in range(3):
            for t in (dq, dk, dv):
                cols.append(t[..., g * 256 + p * 128:g * 256 + (p + 1) * 128])
    wb = jnp.concatenate([rq, rk, rv, rg, g_ret, g_da, g_ml, mq, mk, mv, mo] + cols,
                         axis=-1).astype(BF16)
    pad = jnp.zeros(w_in.shape[:-1] + (LANES - 8,), w_in.dtype)
    wf = jnp.concatenate([mi, mf, pad], axis=-1).astype(BF16)
    return wb, wf


def _rope_tables(S):
    half = LANES // 2
    inv = jnp.power(ROPE_BASE, -jnp.arange(half, dtype=F32) / half)
    ang = jnp.arange(S, dtype=F32)[:, None] * inv[None, :]
    cos = jnp.cos(ang)
    sin = jnp.sin(ang)
    return jnp.concatenate([cos, cos], axis=1), jnp.concatenate([-sin, sin], axis=1)


def kernel(x, norm1_w, w_in, ret_norm_w, da_q_norm_w, da_k_norm_w, ml_conv_w, ml_i_bias, ml_f_bias,
           ml_norm_w, w_br_ret, w_br_da, w_br_ml, w_out, norm2_w, ffn_w_gate, ffn_w_up, ffn_w_down,
           moe_w_router, moe_b_router, moe_w_gate, moe_w_up, moe_w_down):
    B, S, D = x.shape
    T = B * S
    depth = w_in.shape[0]
    cos, sin = _rope_tables(S)
    wb_all, wf_all = _split_w_in(w_in)
    xt = x.reshape(T, D)
    for layer in range(depth):
        wb, wf = wb_all[layer], wf_all[layer]
        nw1 = norm1_w[layer].reshape(1, D)
        assert OFF_MLQK % INPROJ_TN == 0
        zb, zf = _inproj(xt, nw1, wb, wf, ml_conv_w[layer], OFF_MLQK // INPROJ_TN, S, INPROJ_TN)
        o_ret = _retention(zb, cos, sin, ret_norm_w[layer].reshape(1, -1), B, S)
        wq = jnp.tile(da_q_norm_w[layer], 2).reshape(1, LANES)
        wk = jnp.tile(da_k_norm_w[layer], 2).reshape(1, LANES)
        o_da = _dilated_attention(zb, OFF_DA // LANES, wq, wk, B, S)
        gate_bias = jnp.concatenate([ml_i_bias[layer], ml_f_bias[layer],
                                     jnp.zeros((LANES - 2 * ML_HEADS,), F32)]).reshape(1, LANES)
        o_ml = _mlstm(zb, zf, gate_bias, ml_norm_w[layer].reshape(1, -1), B, S,
                      qk_blk=OFF_MLQK // (ML_HEADS * LANES), gate_blk=0)
        merge_args = (xt, o_ret, o_da, o_ml, zb, OFF_GATES // D_MODEL,
                      w_br_ret[layer].astype(BF16), w_br_da[layer].astype(BF16),
                      w_br_ml[layer].astype(BF16), w_out[layer].astype(BF16),
                      norm2_w[layer].reshape(1, D))
        j = layer // 2
        if layer % 2 == 0:
            xt = _merge_ffn(*merge_args, ffn_w_gate[j].astype(BF16), ffn_w_up[j].astype(BF16),
                            ffn_w_down[j].astype(BF16), ROW_TILE)
        else:
            xt, ha, hb, sel, tcnt = _merge_router(*merge_args, moe_w_router[j], moe_b_router[j],
                                                  ROW_TILE)
            xt = _moe(xt, ha, hb, sel, tcnt, _cast_bf16(moe_w_gate[j], 512), _cast_bf16(moe_w_up[j], 512),
                      _cast_bf16(moe_w_down[j], 2048))
    return xt.reshape(B, S, D)
```

```python
import functools
import math

import jax
import jax.numpy as jnp
from jax import lax
from jax.experimental import pallas as pl
from jax.experimental.pallas import tpu as pltpu
from jax.experimental.pallas import tpu_sc as plsc

F32 = jnp.float32
BF16 = jnp.bfloat16

EPS = 1e-6
D_MODEL = 1024
CHUNK = 128
LANES = 128
MXU_N = 256
ROPE_BASE = 10000.0
RET_HEADS = 4
RET_SEQS = 2
DA_GROUPS = ((1, 16), (4, 4), (16, 1))
DA_DH = 64
DA_UNROLL = 4
DA_MAX_STRIDE = 4
ML_HEADS = 4
ML_DK = 64
ML_SEQS = 2
N_EXPERTS = 8
TOP_K = 2
MOE_TILE = 512
SC_WINDOW = 128
INPROJ_TN = 1280
ROW_TILE = 512
WIDE_ROW_TILE = 1024
OFF_GATES = 2048
OFF_MLQK = OFF_GATES + 3 * D_MODEL
OFF_DA = OFF_MLQK + 3 * ML_HEADS * LANES
VMEM_LIMIT = 56 * 1024 * 1024

NEG = -1e30


def _cparams(*sem):
    return pltpu.CompilerParams(dimension_semantics=sem, vmem_limit_bytes=VMEM_LIMIT)


def _dot(a, b):
    return jnp.dot(a, b, preferred_element_type=F32)


def _dot_nt(a, b):
    return lax.dot_general(a, b, (((1,), (1,)), ((), ())), preferred_element_type=F32)


def _dot_tn(a, b):
    return lax.dot_general(a, b, (((0,), (0,)), ((), ())), preferred_element_type=F32)


def _sigmoid(x):
    return 1.0 / (1.0 + jnp.exp(-x))


def _silu(x):
    return x * _sigmoid(x)


def _inproj_kernel(conv_blk, x_ref, nw_ref, wb_ref, wf_ref, cw_ref, zb_ref, zf_ref, h_ref, cv_ref):
    j = pl.program_id(1)
    tm = x_ref.shape[0]

    @pl.when(j == 0)
    def _():
        rc = 512
        for r in range(tm // rc):
            rows = pl.ds(r * rc, rc)
            x = x_ref[rows, :]
            ms = jnp.mean(x * x, axis=-1, keepdims=True)
            h = (x * lax.rsqrt(ms + EPS) * nw_ref[...]).astype(BF16)
            h_ref[rows, :] = h
            zb_ref[rows, :] = _dot(h, wb_ref[...]).astype(zb_ref.dtype)
            zf_ref[rows, :] = _dot(h, wf_ref[...])

    @pl.when((j > 0) & (j != conv_blk))
    def _():
        zb_ref[...] = _dot(h_ref[...], wb_ref[...]).astype(zb_ref.dtype)

    @pl.when(j == conv_blk)
    def _():
        h = h_ref[...]
        cw = cw_ref.shape[1]
        pad = cv_ref.shape[0] - tm
        cv_ref[0:pad, :] = jnp.zeros((pad, cw), F32)
        cv_ref[pad:pad + tm, :] = _dot(h, wb_ref[:, 0:cw])
        taps = cw_ref.shape[0]
        rb = 256

        def conv(piece, cs):
            cols = slice(cs * LANES, (cs + 1) * LANES)
            xe = cv_ref[pl.ds(piece * rb, rb + pad), cols]
            acc = cw_ref[0:1, cols] * xe
            for i in range(1, taps):
                acc = cw_ref[i:i + 1, cols] * xe + pltpu.roll(acc, 1, 0)
            acc = acc[pad:, :]
            scale = ML_DK ** -0.5 if cs >= cw // LANES // 2 else 1.0
            zb_ref[piece * rb:(piece + 1) * rb, cols] = (_silu(acc) * scale).astype(zb_ref.dtype)

        todo = [(piece, cs) for piece in range(tm // rb) for cs in range(cw // LANES)]
        chunks = list(range(cw, zb_ref.shape[1], MXU_N))
        per = -(-len(todo) // len(chunks))
        for n, c0 in enumerate(chunks):
            c1 = min(c0 + MXU_N, zb_ref.shape[1])
            zb_ref[:, c0:c1] = _dot(h, wb_ref[:, c0:c1]).astype(zb_ref.dtype)
            for piece, cs in todo[n * per:(n + 1) * per]:
                conv(piece, cs)


def _inproj(x, nw, wb, wf, conv_w, conv_blk, tm, tnb):
    T, D = x.shape
    return pl.pallas_call(
        functools.partial(_inproj_kernel, conv_blk),
        grid=(T // tm, wb.shape[1] // tnb),
        in_specs=[pl.BlockSpec((tm, D), lambda i, j: (i, 0)),
                  pl.BlockSpec((1, D), lambda i, j: (0, 0)),
                  pl.BlockSpec((D, tnb), lambda i, j: (0, j)),
                  pl.BlockSpec(wf.shape, lambda i, j: (0, 0)),
                  pl.BlockSpec(conv_w.shape, lambda i, j: (0, 0))],
        out_specs=[pl.BlockSpec((tm, tnb), lambda i, j: (i, j)),
                   pl.BlockSpec((tm, wf.shape[1]), lambda i, j: (i, 0))],
        out_shape=[jax.ShapeDtypeStruct((T, wb.shape[1]), BF16),
                   jax.ShapeDtypeStruct((T, wf.shape[1]), F32)],
        scratch_shapes=[pltpu.VMEM((tm, D), BF16), pltpu.VMEM((tm + 8, conv_w.shape[1]), F32)],
        compiler_params=_cparams("parallel", "arbitrary"),
        name="inproj",
    )(x, nw, wb, wf, conv_w)


def _retention_kernel(z_ref, cos_ref, sin_ref, nw_ref, o_ref, dec_ref, st_ref):
    S = z_ref.shape[0] // RET_SEQS
    n_chunks = S // CHUNK
    H = RET_HEADS
    HW = H * LANES
    row = lax.broadcasted_iota(jnp.int32, (CHUNK, CHUNK), 0).astype(F32)
    col = lax.broadcasted_iota(jnp.int32, (CHUNK, CHUNK), 1).astype(F32)
    lgs = [math.log1p(-(2.0 ** (-5.0 - h))) for h in range(H)]
    for h, lg in enumerate(lgs):
        rel = row - col
        dec_ref[h] = jnp.where(rel >= 0, jnp.exp(lg * jnp.maximum(rel, 0.0)), 0.0)
        dec_ref[H + h] = jnp.exp(lg * (row + 1.0))
        dec_ref[2 * H + h] = jnp.exp(lg * (CHUNK - 1.0 - row)) * (LANES ** -0.5)
    st_ref[...] = jnp.zeros(st_ref.shape, F32)
    units = [(sq, h) for sq in range(RET_SEQS) for h in range(H)]

    def body(n, carry):
        r0 = pl.multiple_of(n * CHUNK, CHUNK)
        cos = cos_ref[pl.ds(r0, CHUNK), :]
        sin = sin_ref[pl.ds(r0, CHUNK), :]
        rows_of = [pl.ds(pl.multiple_of(sq * S + n * CHUNK, CHUNK), CHUNK) for sq in range(RET_SEQS)]
        vs, ss, iqs, kvs = [], [], [], []
        for u, (sq, h) in enumerate(units):
            rows = rows_of[sq]
            q = z_ref[rows, h * LANES:(h + 1) * LANES].astype(F32)
            k = z_ref[rows, HW + h * LANES:HW + (h + 1) * LANES].astype(F32)
            v = z_ref[rows, 2 * HW + h * LANES:2 * HW + (h + 1) * LANES]
            q = q * cos + pltpu.roll(q, LANES // 2, 1) * sin
            k = k * cos + pltpu.roll(k, LANES // 2, 1) * sin
            qb = q.astype(BF16)
            kb = (k * (LANES ** -0.5)).astype(BF16)
            kd = (k * dec_ref[2 * H + h]).astype(BF16)
            ss.append(_dot_nt(qb, kb))
            iqs.append(_dot(qb, st_ref[u].astype(BF16)))
            kvs.append(_dot_tn(kd, v))
            vs.append(v)
        ps = [(ss[u] * dec_ref[h]).astype(BF16) for u, (_, h) in enumerate(units)]
        os_ = [_dot(ps[u], vs[u]) + iqs[u] * dec_ref[H + h] for u, (_, h) in enumerate(units)]
        for u, (sq, h) in enumerate(units):
            rows = rows_of[sq]
            st_ref[u] = st_ref[u] * math.exp(lgs[h] * CHUNK) + kvs[u]
            o = os_[u]
            ms = jnp.mean(o * o, axis=-1, keepdims=True)
            o = o * lax.rsqrt(ms + EPS) * nw_ref[:, h * LANES:(h + 1) * LANES]
            g = z_ref[rows, 3 * HW + h * LANES:3 * HW + (h + 1) * LANES].astype(F32)
            o_ref[rows, h * LANES:(h + 1) * LANES] = (o * _silu(g)).astype(o_ref.dtype)
        return carry

    lax.fori_loop(0, n_chunks, body, 0)


def _retention(zb, cos, sin, nw, B, S):
    T = B * S
    W = RET_HEADS * LANES
    R = RET_SEQS * S
    return pl.pallas_call(
        _retention_kernel,
        grid=(B // RET_SEQS,),
        in_specs=[pl.BlockSpec((R, 4 * W), lambda b: (b, 0)),
                  pl.BlockSpec((S, LANES), lambda b: (0, 0)),
                  pl.BlockSpec((S, LANES), lambda b: (0, 0)),
                  pl.BlockSpec((1, W), lambda b: (0, 0))],
        out_specs=pl.BlockSpec((R, W), lambda b: (b, 0)),
        out_shape=jax.ShapeDtypeStruct((T, W), BF16),
        scratch_shapes=[pltpu.VMEM((3 * RET_HEADS, CHUNK, LANES), F32),
                        pltpu.VMEM((RET_SEQS * RET_HEADS, LANES, LANES), F32)],
        compiler_params=_cparams("parallel"),
        name="retention",
    )(zb, cos, sin, nw)


def _da_kernel(*refs):
    z_refs = refs[:9]
    wq_ref, wk_ref, o_ref, qn_ref, kn_ref, v_ref, op_ref, zp_ref = refs[9:17]
    on_refs, zn_refs, spare_refs = refs[17:20], refs[20:23], refs[23:26]
    stage_refs = refs[26:29]
    S = o_ref.shape[0]
    lane = lax.broadcasted_iota(jnp.int32, (1, LANES), 1)
    lo = lane < DA_DH
    row = lax.broadcasted_iota(jnp.int32, (CHUNK, CHUNK), 0)
    col = lax.broadcasted_iota(jnp.int32, (CHUNK, CHUNK), 1)
    mask_cur = col <= row
    row2 = lax.broadcasted_iota(jnp.int32, (CHUNK, 2 * CHUNK), 0)
    col2 = lax.broadcasted_iota(jnp.int32, (CHUNK, 2 * CHUNK), 1)
    mask_band = (col2 >= row2) & (col2 <= row2 + CHUNK)

    seg = (lax.broadcasted_iota(jnp.int32, (LANES, LANES), 0) // DA_DH
           == lax.broadcasted_iota(jnp.int32, (LANES, LANES), 1) // DA_DH).astype(BF16) * (1.0 / DA_DH)

    def head_norm(x, w):
        x2 = x * x
        hi = x2.astype(BF16)
        lo_part = (x2 - hi.astype(F32)).astype(BF16)
        ms = _dot(hi, seg) + _dot(lo_part, seg)
        return x * lax.rsqrt(ms + EPS) * w

    RB = 256
    for g, (dil, nb) in enumerate(DA_GROUPS):
        L = S // dil
        zq_ref, zk_ref, zv_ref = z_refs[3 * g:3 * g + 3]
        if dil > 1:
            for piece in range(S // RB):
                rows = pl.ds(piece * RB, RB)
                for src_ref, dst_ref in zip((zq_ref, zk_ref, zv_ref), stage_refs):
                    dst_ref[rows, :] = src_ref[rows, :].astype(F32)
            zq_ref, zk_ref, zv_ref = stage_refs
        first = min(dil, DA_MAX_STRIDE)
        if dil > first:
            mid_refs = (on_refs[g], zn_refs[g], spare_refs[g])
            sub = S // first
            for b in range(first):
                for piece in range(sub // RB):
                    src = pl.ds(b + first * piece * RB, RB, stride=first)
                    dst = pl.ds(b * sub + piece * RB, RB)
                    for src_ref, dst_ref in zip((zq_ref, zk_ref, zv_ref), mid_refs):
                        dst_ref[dst, :] = src_ref[src, :]
            zq_ref, zk_ref, zv_ref = mid_refs
        rest = dil // first

        def source_rows(rho, piece, n_rows):
            if dil == 1:
                return pl.ds(piece * n_rows, n_rows)
            b, a = rho % first, rho // first
            if rest == 1:
                return pl.ds(b + first * piece * n_rows, n_rows, stride=first)
            return pl.ds(b * (S // first) + a + rest * piece * n_rows, n_rows, stride=rest)

        for rho in range(dil):
            for piece in range(max(L // RB, 1)):
                n_rows = min(RB, L)
                src = source_rows(rho, piece, n_rows)
                dst = pl.ds(rho * L + piece * n_rows, n_rows)
                q = zq_ref[src, :].astype(F32)
                k = zk_ref[src, :].astype(F32)
                v = zv_ref[src, :]
                qn_ref[dst, :] = (head_norm(q, wq_ref[...]) * (DA_DH ** -0.5)).astype(BF16)
                kn_ref[dst, :] = head_norm(k, wk_ref[...]).astype(BF16)
                v_ref[dst, :] = v.astype(BF16)

        def batch(r0, chained, first_has_prev):
            chains = []
            for u in range(DA_UNROLL):
                rows = pl.ds(r0 + u * CHUNK, CHUNK)
                if chained and (u > 0 or first_has_prev):
                    keys, mask = pl.ds(r0 + (u - 1) * CHUNK, 2 * CHUNK), mask_band
                else:
                    keys, mask = rows, mask_cur
                k = kn_ref[keys, :]
                v = v_ref[keys, :]
                vaug = jnp.concatenate([v, jnp.ones_like(v)], axis=1)
                q = qn_ref[rows, :]
                zero = jnp.zeros_like(q)
                for qh in (jnp.where(lo, q, zero), jnp.where(lo, zero, q)):
                    chains.append((_dot_nt(qh, k), mask, vaug))
            probs = []
            for s, mask, _ in chains:
                s = jnp.where(mask, s, NEG)
                m = jnp.max(s, axis=-1, keepdims=True)
                probs.append((jnp.exp(s - m).astype(BF16), m))
            accs = [_dot(p, vaug) for (p, _), (_, _, vaug) in zip(probs, chains)]
            normed, lses = [], []
            for u in range(DA_UNROLL):
                a0, a1 = accs[2 * u], accs[2 * u + 1]
                den = jnp.where(lo, a0[:, LANES:], a1[:, LANES:])
                normed.append(jnp.where(lo, a0[:, :LANES], a1[:, :LANES]) / den)
                lses.append(jnp.where(lo, probs[2 * u][1], probs[2 * u + 1][1]) + jnp.log(den))
            rows = pl.ds(r0, DA_UNROLL * CHUNK)
            op_ref[rows, :] = jnp.concatenate(normed, axis=0)
            zp_ref[rows, :] = jnp.concatenate(lses, axis=0)

        span = DA_UNROLL * CHUNK
        if nb == 1:
            def singles(i, c):
                batch(pl.multiple_of(i * span, span), False, False)
                return c
            lax.fori_loop(0, S // span, singles, 0)
        else:
            def segment(sgi, c, nb=nb):
                base = pl.multiple_of(sgi * (nb * CHUNK), span)
                batch(base, True, False)
                if nb > DA_UNROLL:
                    def inner(n, c2):
                        batch(pl.multiple_of(base + n * span, span), True, True)
                        return c2
                    lax.fori_loop(1, nb // DA_UNROLL, inner, 0)
                return c
            lax.fori_loop(0, dil, segment, 0)

        outs = ((op_ref, on_refs[g]), (zp_ref, zn_refs[g]))
        for rho in range(dil):
            for piece in range(max(L // RB, 1)):
                n_rows = min(RB, L)
                dst = source_rows(rho, piece, n_rows)
                src = pl.ds(rho * L + piece * n_rows, n_rows)
                for k_out, (perm_ref, nat_ref) in enumerate(outs):
                    (stage_refs[k_out] if rest > 1 else nat_ref)[dst, :] = perm_ref[src, :]
        if rest > 1:
            sub = S // first
            for b in range(first):
                for piece in range(sub // RB):
                    dst = pl.ds(b + first * piece * RB, RB, stride=first)
                    src = pl.ds(b * sub + piece * RB, RB)
                    for k_out, (_, nat_ref) in enumerate(outs):
                        nat_ref[dst, :] = stage_refs[k_out][src, :]

    for piece in range(S // RB):
        rows = pl.ds(piece * RB, RB)
        m = jnp.maximum(jnp.maximum(zn_refs[0][rows, :], zn_refs[1][rows, :]), zn_refs[2][rows, :])
        num = jnp.zeros((RB, LANES), F32)
        den = jnp.zeros((RB, LANES), F32)
        for g in range(3):
            e = jnp.exp(zn_refs[g][rows, :] - m)
            num = num + e * on_refs[g][rows, :]
            den = den + e
        o_ref[rows, :] = (num / den).astype(o_ref.dtype)


def _dilated_attention(zb, slab0, wq, wk, B, S):
    T = B * S
    slab = lambda k: pl.BlockSpec((S, LANES), lambda b, p: (b, slab0 + 9 * p + k))
    return pl.pallas_call(
        _da_kernel,
        grid=(B, 2),
        in_specs=[slab(k) for k in range(9)] + [
                  pl.BlockSpec((1, LANES), lambda b, p: (0, 0)),
                  pl.BlockSpec((1, LANES), lambda b, p: (0, 0))],
        out_specs=pl.BlockSpec((S, LANES), lambda b, p: (b, p)),
        out_shape=jax.ShapeDtypeStruct((T, 2 * LANES), BF16),
        scratch_shapes=[pltpu.VMEM((S, LANES), BF16), pltpu.VMEM((S, LANES), BF16),
                        pltpu.VMEM((S, LANES), BF16),
                        pltpu.VMEM((S, LANES), F32), pltpu.VMEM((S, LANES), F32)]
                       + [pltpu.VMEM((S, LANES), F32)] * 12,
        compiler_params=_cparams("parallel", "arbitrary"),
        name="dilated_attention",
    )(*([zb] * 9), wq, wk)


def _mlstm_kernel(qk_ref, v_ref, og_ref, gate_ref, gb_ref, nw_ref, o_ref, st_ref, e_ref, et_ref):
    S = qk_ref.shape[0] // ML_SEQS
    n_chunks = S // CHUNK
    W = qk_ref.shape[1]
    lane = lax.broadcasted_iota(jnp.int32, (1, LANES), 1)
    lo = lane < ML_DK
    row = lax.broadcasted_iota(jnp.int32, (CHUNK, CHUNK), 0)
    col = lax.broadcasted_iota(jnp.int32, (CHUNK, CHUNK), 1)
    causal = col <= row
    tril = causal.astype(F32)
    srow = lax.broadcasted_iota(jnp.int32, (LANES, 1), 0) < ML_DK
    st_ref[...] = jnp.zeros(st_ref.shape, F32)

    is_f = (lane >= ML_HEADS) & (lane < 2 * ML_HEADS)
    for n in range(ML_SEQS * n_chunks):
        rows = pl.ds(n * CHUNK, CHUNK)
        gp = gate_ref[rows, :] + gb_ref[...]
        logf = jnp.minimum(gp, 0.0) - jnp.log(1.0 + jnp.exp(-jnp.abs(gp)))
        cum = jnp.dot(tril, jnp.where(is_f, logf, 0.0), preferred_element_type=F32,
                      precision=lax.Precision.HIGHEST)
        e = jnp.where(lane < ML_HEADS, gp, cum)
        e_ref[rows, :] = e
        et_ref[n] = e.T[0:2 * ML_HEADS, :]

    units = [(sq, h) for sq in range(ML_SEQS) for h in range(ML_HEADS)]

    def chunk(n, mms):
        rows_of = [pl.ds(pl.multiple_of(sq * S + n * CHUNK, CHUNK), CHUNK) for sq in range(ML_SEQS)]
        es = [e_ref[rows_of[sq], :] for sq in range(ML_SEQS)]
        ets = [et_ref[sq * n_chunks + n] for sq in range(ML_SEQS)]
        khs, vaugs, s_raws, iqs = [], [], [], []
        for sq in range(ML_SEQS):
            rows = rows_of[sq]
            for pair in range(2):
                q2 = qk_ref[rows, pair * LANES:(pair + 1) * LANES]
                k2 = qk_ref[rows, W // 2 + pair * LANES:W // 2 + (pair + 1) * LANES]
                zero = jnp.zeros_like(q2)
                stb = st_ref[2 * sq + pair].astype(BF16)
                for hs in range(2):
                    h = 2 * pair + hs
                    qh = jnp.where(lo, q2, zero) if hs == 0 else jnp.where(lo, zero, q2)
                    kh = jnp.where(lo, k2, zero) if hs == 0 else jnp.where(lo, zero, k2)
                    vh = v_ref[rows, h * LANES:(h + 1) * LANES]
                    s_raws.append(_dot_nt(qh, k2))
                    iqs.append(_dot(qh, stb))
                    khs.append(kh)
                    vaugs.append(jnp.concatenate([vh, jnp.ones_like(vh)], axis=-1))
        ps, wis, ms_, kws, wcs, new_mms = [], [], [], [], [], []
        for u, (sq, h) in enumerate(units):
            e, et = es[sq], ets[sq]
            mm = mms[u]
            i_b = jnp.broadcast_to(e[:, h:h + 1], (CHUNK, LANES))
            a_b = jnp.broadcast_to(e[:, ML_HEADS + h:ML_HEADS + h + 1], (CHUNK, LANES))
            c_row = et[h:h + 1, :] - et[ML_HEADS + h:ML_HEADS + h + 1, :]
            dmat = jnp.where(causal, a_b + c_row, NEG)
            inter = a_b + mm
            m = jnp.maximum(inter, jnp.max(dmat, axis=-1, keepdims=True))
            ps.append((s_raws[u] * jnp.exp(dmat - m)).astype(BF16))
            wis.append(jnp.exp(inter - m))
            ms_.append(m)
            m_new = m[CHUNK - 1:CHUNK, :]
            a_last = a_b[CHUNK - 1:CHUNK, :]
            wk = jnp.exp(a_last - a_b + i_b - m_new)
            kws.append((khs[u].astype(F32) * wk).astype(BF16))
            wc = jnp.exp(a_last + mm - m_new)
            wcs.append(jnp.concatenate([wc, wc], axis=1))
            new_mms.append(m_new)
        accs = [_dot(ps[u], vaugs[u]) + jnp.concatenate([wis[u], wis[u]], axis=1) * iqs[u]
                for u in range(len(units))]
        upds = [_dot_tn(kws[u], vaugs[u]) for u in range(len(units))]
        for sp in range(2 * ML_SEQS):
            u0, u1 = 2 * sp, 2 * sp + 1
            st_ref[sp] = jnp.where(srow, wcs[u0], wcs[u1]) * st_ref[sp] + (upds[u0] + upds[u1])
        for u, (sq, h) in enumerate(units):
            num = accs[u][:, :LANES]
            den = accs[u][:, LANES:]
            hv = num / jnp.maximum(jnp.abs(den), jnp.exp(-ms_[u]))
            var = jnp.mean(hv * hv, axis=-1, keepdims=True)
            hv = hv * lax.rsqrt(var + EPS) * nw_ref[:, h * LANES:(h + 1) * LANES]
            og = og_ref[rows_of[sq], h * LANES:(h + 1) * LANES].astype(F32)
            o_ref[rows_of[sq], h * LANES:(h + 1) * LANES] = (hv * _sigmoid(og)).astype(o_ref.dtype)
        return tuple(new_mms)

    lax.fori_loop(0, n_chunks, chunk, tuple(jnp.zeros((1, LANES), F32) for _ in units))


def _mlstm(zb, zf, gate_bias, nw, B, S, qk_blk, gate_blk):
    T = B * S
    W = ML_HEADS * LANES
    R = ML_SEQS * S
    return pl.pallas_call(
        _mlstm_kernel,
        grid=(B // ML_SEQS,),
        in_specs=[pl.BlockSpec((R, W), lambda b: (b, qk_blk)),
                  pl.BlockSpec((R, W), lambda b: (b, qk_blk + 1)),
                  pl.BlockSpec((R, W), lambda b: (b, qk_blk + 2)),
                  pl.BlockSpec((R, LANES), lambda b: (b, gate_blk)),
                  pl.BlockSpec((1, LANES), lambda b: (0, 0)),
                  pl.BlockSpec((1, W), lambda b: (0, 0))],
        out_specs=pl.BlockSpec((R, W), lambda b: (b, 0)),
        out_shape=jax.ShapeDtypeStruct((T, W), BF16),
        scratch_shapes=[pltpu.VMEM((2 * ML_SEQS, LANES, 2 * LANES), F32),
                        pltpu.VMEM((R, LANES), F32),
                        pltpu.VMEM((R // CHUNK, 2 * ML_HEADS, LANES), F32)],
        compiler_params=_cparams("parallel"),
        name="mlstm",
    )(zb, zb, zb, zf, gate_bias, nw)


def _merge_value(x_ref, oret_ref, oda_ref, oml_ref, gr_ref, gd_ref, gm_ref,
                 wr_ref, wd_ref, wm_ref, wo_ref):
    y = _sigmoid(gr_ref[...].astype(F32)) * _dot(oret_ref[...], wr_ref[...])
    y = y + _sigmoid(gd_ref[...].astype(F32)) * _dot(oda_ref[...], wd_ref[...])
    y = y + _sigmoid(gm_ref[...].astype(F32)) * _dot(oml_ref[...], wm_ref[...])
    return x_ref[...] + _dot(y.astype(BF16), wo_ref[...])


def _merge_specs(x, o_ret, o_da, o_ml, g_blk, tm):
    D = x.shape[1]
    return [pl.BlockSpec((tm, D), lambda i: (i, 0)),
            pl.BlockSpec((tm, o_ret.shape[1]), lambda i: (i, 0)),
            pl.BlockSpec((tm, o_da.shape[1]), lambda i: (i, 0)),
            pl.BlockSpec((tm, o_ml.shape[1]), lambda i: (i, 0)),
            pl.BlockSpec((tm, D), lambda i: (i, g_blk)),
            pl.BlockSpec((tm, D), lambda i: (i, g_blk + 1)),
            pl.BlockSpec((tm, D), lambda i: (i, g_blk + 2))]


def _resident(a):
    return pl.BlockSpec(a.shape, lambda i: (0,) * a.ndim, pipeline_mode=pl.Buffered(1))


def _swiglu_block(h, wg_ref, wu_ref, wd_ref, a_ref):
    tf = wg_ref.shape[1]
    c0 = 0
    while c0 < tf:
        w = min(MXU_N, tf - c0)
        g = _dot(h, wg_ref[:, c0:c0 + w].astype(BF16))
        u = _dot(h, wu_ref[:, c0:c0 + w].astype(BF16))
        a_ref[:, c0:c0 + w] = (_silu(g) * u).astype(BF16)
        c0 += w
    return _dot(a_ref[...], wd_ref[...].astype(BF16))


def _merge_ffn_kernel(*refs):
    merge_refs, (nw_ref, wg_ref, wu_ref, wdn_ref, o_ref, a_ref) = refs[:11], refs[11:]
    x = _merge_value(*merge_refs)
    ms = jnp.mean(x * x, axis=-1, keepdims=True)
    h = (x * lax.rsqrt(ms + EPS) * nw_ref[...]).astype(BF16)
    o_ref[...] = x + _swiglu_block(h, wg_ref, wu_ref, wdn_ref, a_ref)


def _merge_ffn(x, o_ret, o_da, o_ml, zb, g_blk, wr, wd, wm, wo, nw, wg, wu, wdn, tm):
    T, D = x.shape
    weights = (wr, wd, wm, wo, nw, wg, wu, wdn)
    return pl.pallas_call(
        _merge_ffn_kernel,
        grid=(T // tm,),
        in_specs=_merge_specs(x, o_ret, o_da, o_ml, g_blk, tm) + [_resident(a) for a in weights],
        out_specs=pl.BlockSpec((tm, D), lambda i: (i, 0)),
        out_shape=jax.ShapeDtypeStruct((T, D), F32),
        scratch_shapes=[pltpu.VMEM((tm, wg.shape[1]), BF16)],
        compiler_params=_cparams("parallel"),
        name="merge_ffn",
    )(x, o_ret, o_da, o_ml, zb, zb, zb, *weights)


HI16 = 0xFFFF0000


def _pack_bf16_pairs(a, b):
    ab = pltpu.bitcast(a.astype(BF16).astype(F32), jnp.uint32)
    bb = pltpu.bitcast(b.astype(BF16).astype(F32), jnp.uint32)
    return (ab >> 16) | (bb & jnp.uint32(HI16))


def _unpack_bf16_pairs(p):
    lo = pltpu.bitcast(p << 16, F32)
    hi = pltpu.bitcast(p & jnp.uint32(HI16), F32)
    return lo, hi


def _merge_router_kernel(*refs):
    merge_refs, (nw_ref, wr_ref, br_ref, xo_ref, ha_ref, hb_ref, sel_ref, cnt_ref) = refs[:11], refs[11:]
    x = _merge_value(*merge_refs)
    xo_ref[...] = x
    ms = jnp.mean(x * x, axis=-1, keepdims=True)
    h = x * lax.rsqrt(ms + EPS) * nw_ref[...]
    Q = h.shape[1] // 4
    ha_ref[...] = _pack_bf16_pairs(h[:, 0:Q], h[:, Q:2 * Q])
    hb_ref[...] = _pack_bf16_pairs(h[:, 2 * Q:3 * Q], h[:, 3 * Q:4 * Q])
    w = wr_ref[...]
    h_hi = h.astype(BF16)
    h_lo = (h - h_hi.astype(F32)).astype(BF16)
    w_hi = w.astype(BF16)
    w_lo = (w - w_hi.astype(F32)).astype(BF16)
    logits = _dot(h_hi, w_hi) + _dot(h_lo, w_hi) + _dot(h_hi, w_lo) + br_ref[...]
    lane = lax.broadcasted_iota(jnp.int32, logits.shape, 1).astype(F32)
    logits = jnp.where(lane < N_EXPERTS, logits, NEG)
    m1 = jnp.max(logits, axis=-1, keepdims=True)
    i1 = jnp.min(jnp.where(logits == m1, lane, float(LANES)), axis=-1, keepdims=True)
    rest = jnp.where(lane == i1, NEG, logits)
    m2 = jnp.max(rest, axis=-1, keepdims=True)
    i2 = jnp.min(jnp.where(rest == m2, lane, float(LANES)), axis=-1, keepdims=True)
    e2 = jnp.exp(m2 - m1)
    p1 = 1.0 / (1.0 + e2)
    p2 = e2 / (1.0 + e2)
    sel_ref[...] = jnp.where(lane == 0.0, i1, jnp.where(lane == 1.0, i2,
                             jnp.where(lane == 2.0, p1, jnp.where(lane == 3.0, p2, 0.0))))
    chosen = jnp.where((lane == i1) | (lane == i2), 1.0, 0.0)
    cnt_ref[...] = jnp.broadcast_to(jnp.sum(chosen, axis=0, keepdims=True), cnt_ref.shape)


def _merge_router(x, o_ret, o_da, o_ml, zb, g_blk, wr, wd, wm, wo, nw, w_router, b_router, tm):
    T, D = x.shape
    Q = D // 4
    wrt = jnp.pad(w_router, ((0, 0), (0, LANES - N_EXPERTS)))
    brt = jnp.pad(b_router, (0, LANES - N_EXPERTS)).reshape(1, LANES)
    weights = (wr, wd, wm, wo, nw, wrt, brt)
    return pl.pallas_call(
        _merge_router_kernel,
        grid=(T // tm,),
        in_specs=_merge_specs(x, o_ret, o_da, o_ml, g_blk, tm) + [_resident(a) for a in weights],
        out_specs=[pl.BlockSpec((tm, D), lambda i: (i, 0)),
                   pl.BlockSpec((tm, Q), lambda i: (i, 0)),
                   pl.BlockSpec((tm, Q), lambda i: (i, 0)),
                   pl.BlockSpec((tm, LANES), lambda i: (i, 0)),
                   pl.BlockSpec((None, 8, LANES), lambda i: (i, 0, 0))],
        out_shape=[jax.ShapeDtypeStruct((T, D), F32),
                   jax.ShapeDtypeStruct((T, Q), jnp.uint32), jax.ShapeDtypeStruct((T, Q), jnp.uint32),
                   jax.ShapeDtypeStruct((T, LANES), F32),
                   jax.ShapeDtypeStruct((T // tm, 8, LANES), F32)],
        compiler_params=_cparams("parallel"),
        name="merge_router",
    )(x, o_ret, o_da, o_ml, zb, zb, zb, *weights)


def _moe_rank_kernel(sel_ref, tcnt_ref, pos_ref, meta_ref, cnt_ref, offs_ref, carry_ref, before_ref):
    i = pl.program_id(0)
    tm = sel_ref.shape[0]
    lane = lax.broadcasted_iota(jnp.int32, (tm, LANES), 1).astype(F32)
    lane1 = lax.broadcasted_iota(jnp.int32, (1, LANES), 1).astype(F32)
    sel = sel_ref[...]
    i1 = sel[:, 0:1]
    i2 = sel[:, 1:2]
    onehot = jnp.where((lane == i1) | (lane == i2), 1.0, 0.0)
    colsum = jnp.sum(onehot, axis=0, keepdims=True)

    def padded_counts():
        return jnp.floor((cnt_ref[...] + (MOE_TILE - 1.0)) * (1.0 / MOE_TILE)) * MOE_TILE

    @pl.when(i == 0)
    def _():
        cnt_ref[...] = jnp.sum(tcnt_ref[...], axis=0)
        k = lax.broadcasted_iota(jnp.int32, (LANES, LANES), 0)
        e = lax.broadcasted_iota(jnp.int32, (LANES, LANES), 1)
        upper = (k < e).astype(F32)
        offs_ref[...] = jnp.dot(padded_counts(), upper, preferred_element_type=F32,
                                precision=lax.Precision.HIGHEST)
        carry_ref[...] = jnp.zeros(carry_ref.shape, F32)
        r = lax.broadcasted_iota(jnp.int32, (tm, tm), 0)
        c = lax.broadcasted_iota(jnp.int32, (tm, tm), 1)
        before_ref[...] = (c < r).astype(BF16)

    rank = _dot(before_ref[...], onehot.astype(BF16)) + carry_ref[0:1, :]
    row = rank + offs_ref[0:1, :]
    pos1 = jnp.sum(jnp.where(lane == i1, row, 0.0), axis=-1, keepdims=True)
    pos2 = jnp.sum(jnp.where(lane == i2, row, 0.0), axis=-1, keepdims=True)
    both = jnp.where(lane == 0.0, pos1, jnp.where(lane == 1.0, pos2, 0.0))
    pos_ref[...] = both.T[0:8, :].astype(jnp.int32)
    carry_ref[...] += colsum

    @pl.when(i == pl.num_programs(0) - 1)
    def _():
        nrow = meta_ref.shape[0]
        padded = padded_counts()[0:1, :]
        offs = offs_ref[0:1, :]
        ends = offs + padded
        start = lax.broadcasted_iota(jnp.int32, (nrow, 1), 0).astype(F32) * MOE_TILE
        is_e = lane1 < N_EXPERTS
        te = jnp.sum(jnp.where(is_e & (ends <= start), 1.0, 0.0), axis=-1, keepdims=True)
        te = jnp.minimum(te, N_EXPERTS - 1.0)
        valid_end = jnp.sum(jnp.where(lane1 == te, offs + cnt_ref[0:1, :], 0.0), axis=-1, keepdims=True)
        nv = jnp.clip(valid_end - start, 0.0, MOE_TILE)
        total = jnp.sum(jnp.where(is_e, padded, 0.0), axis=-1, keepdims=True)
        active = jnp.where(start < total, 1.0, 0.0)
        meta_ref[...] = jnp.where(lane1 == 0.0, te, jnp.where(lane1 == 1.0, nv,
                                  jnp.where(lane1 == 2.0, active, 0.0)))


def _moe_rank(sel, tcnt, tm, meta_rows):
    T = sel.shape[0]
    nt = T // tm
    return pl.pallas_call(
        _moe_rank_kernel,
        grid=(nt,),
        in_specs=[pl.BlockSpec((tm, LANES), lambda i: (i, 0)),
                  pl.BlockSpec(tcnt.shape, lambda i: (0, 0, 0))],
        out_specs=[pl.BlockSpec((None, 8, tm), lambda i: (i, 0, 0)),
                   pl.BlockSpec((meta_rows, LANES), lambda i: (0, 0))],
        out_shape=[jax.ShapeDtypeStruct((nt, 8, tm), jnp.int32),
                   jax.ShapeDtypeStruct((meta_rows, LANES), F32)],
        scratch_shapes=[pltpu.VMEM((8, LANES), F32), pltpu.VMEM((8, LANES), F32),
                        pltpu.VMEM((8, LANES), F32), pltpu.VMEM((tm, tm), BF16)],
        compiler_params=_cparams("arbitrary"),
        name="moe_rank",
    )(sel, tcnt)


def _sc_mesh():
    return plsc.VectorSubcoreMesh(core_axis_name="core", subcore_axis_name="subcore")


def _sc_scatter_rows(x, idx, n_rows):
    n_idx = idx.shape[1]
    nt = x.shape[0] // SC_WINDOW
    width = x.shape[1]

    @functools.partial(pl.kernel, out_type=jax.ShapeDtypeStruct((n_rows, width), x.dtype),
                       mesh=_sc_mesh())
    def scatter_kernel(x_hbm, i_hbm, o_hbm):
        def body(x_vmem, i_vmem):
            pltpu.sync_copy(x_vmem, o_hbm.at[i_vmem.at[0]])

        pltpu.emit_pipeline(
            body, grid=(n_idx // SC_WINDOW,),
            in_specs=[pl.BlockSpec((SC_WINDOW, width), lambda i: (i % nt, 0)),
                      pl.BlockSpec((1, SC_WINDOW), lambda i: (0, i))],
            out_specs=[],
            core_axis_name=("core", "subcore"), dimension_semantics=(pltpu.PARALLEL,),
        )(x_hbm, i_hbm)

    return scatter_kernel(x, idx)


def _sc_gather_rows(x, idx):
    n_idx = idx.shape[1]
    width = x.shape[1]

    @functools.partial(pl.kernel, out_type=jax.ShapeDtypeStruct((n_idx, width), x.dtype),
                       mesh=_sc_mesh())
    def gather_kernel(x_hbm, i_hbm, o_hbm):
        def body(i_vmem, o_vmem):
            pltpu.sync_copy(x_hbm.at[i_vmem.at[0]], o_vmem)

        pltpu.emit_pipeline(
            body, grid=(n_idx // SC_WINDOW,),
            in_specs=[pl.BlockSpec((1, SC_WINDOW), lambda i: (0, i))],
            out_specs=[pl.BlockSpec((SC_WINDOW, width), lambda i: (i, 0))],
            core_axis_name=("core", "subcore"), dimension_semantics=(pltpu.PARALLEL,),
        )(i_hbm, o_hbm)

    return gather_kernel(x, idx)


def _moe_group_kernel(te_ref, nv_ref, na_ref, xa_ref, xb_ref, wg_ref, wu_ref, wd_ref,
                      ya_ref, yb_ref, h_ref, a_ref):
    j = pl.program_id(0)

    @pl.when(j < na_ref[0])
    def _():
        tm = h_ref.shape[0]
        Q = xa_ref.shape[1]
        valid = lax.broadcasted_iota(jnp.int32, (tm, 1), 0) < nv_ref[j]
        for src, c0 in ((xa_ref, 0), (xb_ref, 2 * Q)):
            lo, hi = _unpack_bf16_pairs(src[...])
            h_ref[:, c0:c0 + Q] = jnp.where(valid, lo, 0.0).astype(BF16)
            h_ref[:, c0 + Q:c0 + 2 * Q] = jnp.where(valid, hi, 0.0).astype(BF16)
        y = _swiglu_block(h_ref[...], wg_ref, wu_ref, wd_ref, a_ref)
        ya_ref[...] = _pack_bf16_pairs(y[:, 0:Q], y[:, Q:2 * Q])
        yb_ref[...] = _pack_bf16_pairs(y[:, 2 * Q:3 * Q], y[:, 3 * Q:4 * Q])


def _moe_group(te, nv, na, xa, xb, wg, wu, wd):
    R, Q = xa.shape
    E, D, F = wg.shape
    tile = lambda j, te, nv, na: (jnp.minimum(j, na[0] - 1), 0)
    expert = lambda j, te, nv, na: (te[j], 0, 0)
    once = pl.Buffered(1)
    grid_spec = pltpu.PrefetchScalarGridSpec(
        num_scalar_prefetch=3,
        grid=(R // MOE_TILE,),
        in_specs=[pl.BlockSpec((MOE_TILE, Q), tile),
                  pl.BlockSpec((MOE_TILE, Q), tile),
                  pl.BlockSpec((None, D, F), expert),
                  pl.BlockSpec((None, D, F), expert),
                  pl.BlockSpec((None, F, D), expert, pipeline_mode=once)],
        out_specs=[pl.BlockSpec((MOE_TILE, Q), tile), pl.BlockSpec((MOE_TILE, Q), tile)],
        scratch_shapes=[pltpu.VMEM((MOE_TILE, D), BF16), pltpu.VMEM((MOE_TILE, F), BF16)],
    )
    return pl.pallas_call(
        _moe_group_kernel,
        grid_spec=grid_spec,
        out_shape=[jax.ShapeDtypeStruct((R, Q), jnp.uint32), jax.ShapeDtypeStruct((R, Q), jnp.uint32)],
        compiler_params=_cparams("arbitrary"),
        name="moe_experts",
    )(te, nv, na, xa, xb, wg, wu, wd)


def _moe_combine_kernel(x_ref, sel_ref, g1_ref, g2_ref, *rest):
    o_ref = rest[-1]
    sel = sel_ref[...]
    p1 = sel[:, 2:3]
    p2 = sel[:, 3:4]
    Q = g1_ref.shape[1]
    lo1, hi1 = _unpack_bf16_pairs(g1_ref[...])
    lo2, hi2 = _unpack_bf16_pairs(g2_ref[...])
    o_ref[:, 0:Q] = x_ref[:, 0:Q] + (p1 * lo1 + p2 * lo2)
    o_ref[:, Q:2 * Q] = x_ref[:, Q:2 * Q] + (p1 * hi1 + p2 * hi2)


def _moe_combine_half(x, sel, g, half, partial_out, tm):
    T, D = x.shape
    Q = g.shape[1]
    nt = T // tm
    in_specs = [pl.BlockSpec((tm, 2 * Q), lambda i: (i, half)),
                pl.BlockSpec((tm, LANES), lambda i: (i, 0)),
                pl.BlockSpec((tm, Q), lambda i: (i, 0)),
                pl.BlockSpec((tm, Q), lambda i: (i + nt, 0))]
    args = [x, sel, g, g]
    aliases = {}
    if partial_out is not None:
        in_specs.append(pl.BlockSpec(memory_space=pl.ANY))
        args.append(partial_out)
        aliases = {4: 0}
    return pl.pallas_call(
        _moe_combine_kernel,
        grid=(nt,),
        in_specs=in_specs,
        out_specs=pl.BlockSpec((tm, 2 * Q), lambda i: (i, half)),
        out_shape=jax.ShapeDtypeStruct((T, D), F32),
        input_output_aliases=aliases,
        compiler_params=_cparams("parallel"),
        name="moe_combine",
    )(*args)


def _cast_kernel(x_ref, o_ref):
    o_ref[...] = x_ref[...].astype(o_ref.dtype)


def _cast_bf16(w, rows):
    cols = w.shape[-1]
    w2 = w.reshape(-1, cols)
    out = pl.pallas_call(
        _cast_kernel,
        grid=(w2.shape[0] // rows,),
        in_specs=[pl.BlockSpec((rows, cols), lambda i: (i, 0))],
        out_specs=pl.BlockSpec((rows, cols), lambda i: (i, 0)),
        out_shape=jax.ShapeDtypeStruct(w2.shape, BF16),
        compiler_params=_cparams("parallel"),
        name="cast_bf16",
    )(w2)
    return out.reshape(w.shape)


def _moe(x, ha, hb, sel, tcnt, wg, wu, wd):
    T, D = x.shape
    n_rows = TOP_K * T + N_EXPERTS * MOE_TILE
    n_tiles = n_rows // MOE_TILE
    pos, meta = _moe_rank(sel, tcnt, WIDE_ROW_TILE, 256)
    idx = jnp.concatenate([pos[:, 0, :].reshape(1, T), pos[:, 1, :].reshape(1, T)], axis=1)
    te = meta[:n_tiles, 0].astype(jnp.int32)
    nv = meta[:n_tiles, 1].astype(jnp.int32)
    na = jnp.sum(meta[:n_tiles, 2]).astype(jnp.int32).reshape(1)
    te = jnp.where(jnp.arange(n_tiles) < na[0], te, te[na[0] - 1])
    xa = _sc_scatter_rows(ha, idx, n_rows)
    xb = _sc_scatter_rows(hb, idx, n_rows)
    ya, yb = _moe_group(te, nv, na, xa, xb, wg, wu, wd)
    ga = _sc_gather_rows(ya, idx)
    gb = _sc_gather_rows(yb, idx)
    out = _moe_combine_half(x, sel, ga, 0, None, 2 * WIDE_ROW_TILE)
    return _moe_combine_half(x, sel, gb, 1, out, 2 * WIDE_ROW_TILE)


def _split_w_in(w_in):
    sizes = (512, 512, 512, 512, 768, 768, 768, 256, 256, 512, 512, 4, 4, 1024, 1024, 1024)
    offs = [0]
    for s in sizes:
        offs.append(offs[-1] + s)
    part = lambda i: w_in[..., offs[i]:offs[i + 1]]
    rq, rk, rv, rg, dq, dk, dv, mq, mk, mv, mo, mi, mf, g_ret, g_da, g_ml = (part(i) for i in range(16))
    cols = []
    for p in range(2):
        for g in range(3):
            for t in (dq, dk, dv):
                cols.append(t[..., g * 256 + p * 128:g * 256 + (p + 1) * 128])
    wb = jnp.concatenate([rq, rk, rv, rg, g_ret, g_da, g_ml, mq, mk, mv, mo] + cols,
                         axis=-1).astype(BF16)
    pad = jnp.zeros(w_in.shape[:-1] + (LANES - 8,), w_in.dtype)
    wf = jnp.concatenate([mi, mf, pad], axis=-1).astype(BF16)
    return wb, wf


def _rope_tables(S):
    half = LANES // 2
    inv = jnp.power(ROPE_BASE, -jnp.arange(half, dtype=F32) / half)
    ang = jnp.arange(S, dtype=F32)[:, None] * inv[None, :]
    cos = jnp.cos(ang)
    sin = jnp.sin(ang)
    return jnp.concatenate([cos, cos], axis=1), jnp.concatenate([-sin, sin], axis=1)


def kernel(x, norm1_w, w_in, ret_norm_w, da_q_norm_w, da_k_norm_w, ml_conv_w, ml_i_bias, ml_f_bias,
           ml_norm_w, w_br_ret, w_br_da, w_br_ml, w_out, norm2_w, ffn_w_gate, ffn_w_up, ffn_w_down,
           moe_w_router, moe_b_router, moe_w_gate, moe_w_up, moe_w_down):
    B, S, D = x.shape
    T = B * S
    depth = w_in.shape[0]
    cos, sin = _rope_tables(S)
    wb_all, wf_all = _split_w_in(w_in)
    xt = x.reshape(T, D)
    for layer in range(depth):
        wb, wf = wb_all[layer], wf_all[layer]
        nw1 = norm1_w[layer].reshape(1, D)
        assert OFF_MLQK % INPROJ_TN == 0
        zb, zf = _inproj(xt, nw1, wb, wf, ml_conv_w[layer], OFF_MLQK // INPROJ_TN, S, INPROJ_TN)
        o_ret = _retention(zb, cos, sin, ret_norm_w[layer].reshape(1, -1), B, S)
        wq = jnp.tile(da_q_norm_w[layer], 2).reshape(1, LANES)
        wk = jnp.tile(da_k_norm_w[layer], 2).reshape(1, LANES)
        o_da = _dilated_attention(zb, OFF_DA // LANES, wq, wk, B, S)
        gate_bias = jnp.concatenate([ml_i_bias[layer], ml_f_bias[layer],
                                     jnp.zeros((LANES - 2 * ML_HEADS,), F32)]).reshape(1, LANES)
        o_ml = _mlstm(zb, zf, gate_bias, ml_norm_w[layer].reshape(1, -1), B, S,
                      qk_blk=OFF_MLQK // (ML_HEADS * LANES), gate_blk=0)
        merge_args = (xt, o_ret, o_da, o_ml, zb, OFF_GATES // D_MODEL,
                      w_br_ret[layer].astype(BF16), w_br_da[layer].astype(BF16),
                      w_br_ml[layer].astype(BF16), w_out[layer].astype(BF16),
                      norm2_w[layer].reshape(1, D))
        j = layer // 2
        if layer % 2 == 0:
            xt = _merge_ffn(*merge_args, ffn_w_gate[j].astype(BF16), ffn_w_up[j].astype(BF16),
                            ffn_w_down[j].astype(BF16), ROW_TILE)
        else:
            xt, ha, hb, sel, tcnt = _merge_router(*merge_args, moe_w_router[j], moe_b_router[j],
                                                  ROW_TILE)
            xt = _moe(xt, ha, hb, sel, tcnt, _cast_bf16(moe_w_gate[j], 512), _cast_bf16(moe_w_up[j], 512),
                      _cast_bf16(moe_w_down[j], 2048))
    return xt.reshape(B, S, D)
```

```python
import functools
import math

import jax
import jax.numpy as jnp
from jax import lax
from jax.experimental import pallas as pl
from jax.experimental.pallas import tpu as pltpu
from jax.experimental.pallas import tpu_sc as plsc

F32 = jnp.float32
BF16 = jnp.bfloat16

EPS = 1e-6
D_MODEL = 1024
CHUNK = 128
LANES = 128
MXU_N = 256
ROPE_BASE = 10000.0
RET_HEADS = 4
RET_SEQS = 2
DA_GROUPS = ((1, 16), (4, 4), (16, 1))
DA_DH = 64
DA_UNROLL = 4
DA_MAX_STRIDE = 4
ML_HEADS = 4
ML_DK = 64
ML_SEQS = 2
N_EXPERTS = 8
TOP_K = 2
MOE_TILE = 512
SC_WINDOW = 128
INPROJ_TN = 1280
ROW_TILE = 512
WIDE_ROW_TILE = 1024
OFF_GATES = 2048
OFF_MLQK = OFF_GATES + 3 * D_MODEL
OFF_DA = OFF_MLQK + 3 * ML_HEADS * LANES
VMEM_LIMIT = 56 * 1024 * 1024

NEG = -1e30


def _cparams(*sem):
    return pltpu.CompilerParams(dimension_semantics=sem, vmem_limit_bytes=VMEM_LIMIT)


def _dot(a, b):
    return jnp.dot(a, b, preferred_element_type=F32)


def _dot_nt(a, b):
    return lax.dot_general(a, b, (((1,), (1,)), ((), ())), preferred_element_type=F32)


def _dot_tn(a, b):
    return lax.dot_general(a, b, (((0,), (0,)), ((), ())), preferred_element_type=F32)


def _sigmoid(x):
    return 1.0 / (1.0 + jnp.exp(-x))


def _silu(x):
    return x * _sigmoid(x)


def _inproj_kernel(conv_blk, x_ref, nw_ref, wb_ref, wf_ref, cw_ref, zb_ref, zf_ref, h_ref, cv_ref):
    j = pl.program_id(1)
    tm = x_ref.shape[0]

    @pl.when(j == 0)
    def _():
        rc = 512
        for r in range(tm // rc):
            rows = pl.ds(r * rc, rc)
            x = x_ref[rows, :]
            ms = jnp.mean(x * x, axis=-1, keepdims=True)
            h = (x * lax.rsqrt(ms + EPS) * nw_ref[...]).astype(BF16)
            h_ref[rows, :] = h
            zb_ref[rows, :] = _dot(h, wb_ref[...]).astype(zb_ref.dtype)
            zf_ref[rows, :] = _dot(h, wf_ref[...])

    @pl.when((j > 0) & (j != conv_blk))
    def _():
        zb_ref[...] = _dot(h_ref[...], wb_ref[...]).astype(zb_ref.dtype)

    @pl.when(j == conv_blk)
    def _():
        h = h_ref[...]
        cw = cw_ref.shape[1]
        pad = cv_ref.shape[0] - tm
        cv_ref[0:pad, :] = jnp.zeros((pad, cw), F32)
        cv_ref[pad:pad + tm, :] = _dot(h, wb_ref[:, 0:cw])
        taps = cw_ref.shape[0]
        rb = 256

        def conv(piece, cs):
            cols = slice(cs * LANES, (cs + 1) * LANES)
            xe = cv_ref[pl.ds(piece * rb, rb + pad), cols]
            acc = cw_ref[0:1, cols] * xe
            for i in range(1, taps):
                acc = cw_ref[i:i + 1, cols] * xe + pltpu.roll(acc, 1, 0)
            acc = acc[pad:, :]
            scale = ML_DK ** -0.5 if cs >= cw // LANES // 2 else 1.0
            zb_ref[piece * rb:(piece + 1) * rb, cols] = (_silu(acc) * scale).astype(zb_ref.dtype)

        todo = [(piece, cs) for piece in range(tm // rb) for cs in range(cw // LANES)]
        chunks = list(range(cw, zb_ref.shape[1], MXU_N))
        per = -(-len(todo) // len(chunks))
        for n, c0 in enumerate(chunks):
            c1 = min(c0 + MXU_N, zb_ref.shape[1])
            zb_ref[:, c0:c1] = _dot(h, wb_ref[:, c0:c1]).astype(zb_ref.dtype)
            for piece, cs in todo[n * per:(n + 1) * per]:
                conv(piece, cs)


def _inproj(x, nw, wb, wf, conv_w, conv_blk, tm, tnb):
    T, D = x.shape
    return pl.pallas_call(
        functools.partial(_inproj_kernel, conv_blk),
        grid=(T // tm, wb.shape[1] // tnb),
        in_specs=[pl.BlockSpec((tm, D), lambda i, j: (i, 0)),
                  pl.BlockSpec((1, D), lambda i, j: (0, 0)),
                  pl.BlockSpec((D, tnb), lambda i, j: (0, j)),
                  pl.BlockSpec(wf.shape, lambda i, j: (0, 0)),
                  pl.BlockSpec(conv_w.shape, lambda i, j: (0, 0))],
        out_specs=[pl.BlockSpec((tm, tnb), lambda i, j: (i, j)),
                   pl.BlockSpec((tm, wf.shape[1]), lambda i, j: (i, 0))],
        out_shape=[jax.ShapeDtypeStruct((T, wb.shape[1]), BF16),
                   jax.ShapeDtypeStruct((T, wf.shape[1]), F32)],
        scratch_shapes=[pltpu.VMEM((tm, D), BF16), pltpu.VMEM((tm + 8, conv_w.shape[1]), F32)],
        compiler_params=_cparams("parallel", "arbitrary"),
        name="inproj",
    )(x, nw, wb, wf, conv_w)


def _retention_kernel(z_ref, cos_ref, sin_ref, nw_ref, o_ref, dec_ref, st_ref):
    S = z_ref.shape[0] // RET_SEQS
    n_chunks = S // CHUNK
    H = RET_HEADS
    HW = H * LANES
    row = lax.broadcasted_iota(jnp.int32, (CHUNK, CHUNK), 0).astype(F32)
    col = lax.broadcasted_iota(jnp.int32, (CHUNK, CHUNK), 1).astype(F32)
    lgs = [math.log1p(-(2.0 ** (-5.0 - h))) for h in range(H)]
    for h, lg in enumerate(lgs):
        rel = row - col
        dec_ref[h] = jnp.where(rel >= 0, jnp.exp(lg * jnp.maximum(rel, 0.0)), 0.0)
        dec_ref[H + h] = jnp.exp(lg * (row + 1.0))
        dec_ref[2 * H + h] = jnp.exp(lg * (CHUNK - 1.0 - row)) * (LANES ** -0.5)
    st_ref[...] = jnp.zeros(st_ref.shape, F32)
    units = [(sq, h) for sq in range(RET_SEQS) for h in range(H)]

    def body(n, carry):
        r0 = pl.multiple_of(n * CHUNK, CHUNK)
        cos = cos_ref[pl.ds(r0, CHUNK), :]
        sin = sin_ref[pl.ds(r0, CHUNK), :]
        rows_of = [pl.ds(pl.multiple_of(sq * S + n * CHUNK, CHUNK), CHUNK) for sq in range(RET_SEQS)]
        vs, ss, iqs, kvs = [], [], [], []
        for u, (sq, h) in enumerate(units):
            rows = rows_of[sq]
            q = z_ref[rows, h * LANES:(h + 1) * LANES].astype(F32)
            k = z_ref[rows, HW + h * LANES:HW + (h + 1) * LANES].astype(F32)
            v = z_ref[rows, 2 * HW + h * LANES:2 * HW + (h + 1) * LANES]
            q = q * cos + pltpu.roll(q, LANES // 2, 1) * sin
            k = k * cos + pltpu.roll(k, LANES // 2, 1) * sin
            qb = q.astype(BF16)
            kb = (k * (LANES ** -0.5)).astype(BF16)
            kd = (k * dec_ref[2 * H + h]).astype(BF16)
            ss.append(_dot_nt(qb, kb))
            iqs.append(_dot(qb, st_ref[u].astype(BF16)))
            kvs.append(_dot_tn(kd, v))
            vs.append(v)
        ps = [(ss[u] * dec_ref[h]).astype(BF16) for u, (_, h) in enumerate(units)]
        os_ = [_dot(ps[u], vs[u]) + iqs[u] * dec_ref[H + h] for u, (_, h) in enumerate(units)]
        for u, (sq, h) in enumerate(units):
            rows = rows_of[sq]
            st_ref[u] = st_ref[u] * math.exp(lgs[h] * CHUNK) + kvs[u]
            o = os_[u]
            ms = jnp.mean(o * o, axis=-1, keepdims=True)
            o = o * lax.rsqrt(ms + EPS) * nw_ref[:, h * LANES:(h + 1) * LANES]
            g = z_ref[rows, 3 * HW + h * LANES:3 * HW + (h + 1) * LANES].astype(F32)
            o_ref[rows, h * LANES:(h + 1) * LANES] = (o * _silu(g)).astype(o_ref.dtype)
        return carry

    lax.fori_loop(0, n_chunks, body, 0)


def _retention(zb, cos, sin, nw, B, S):
    T = B * S
    W = RET_HEADS * LANES
    R = RET_SEQS * S
    return pl.pallas_call(
        _retention_kernel,
        grid=(B // RET_SEQS,),
        in_specs=[pl.BlockSpec((R, 4 * W), lambda b: (b, 0)),
                  pl.BlockSpec((S, LANES), lambda b: (0, 0)),
                  pl.BlockSpec((S, LANES), lambda b: (0, 0)),
                  pl.BlockSpec((1, W), lambda b: (0, 0))],
        out_specs=pl.BlockSpec((R, W), lambda b: (b, 0)),
        out_shape=jax.ShapeDtypeStruct((T, W), BF16),
        scratch_shapes=[pltpu.VMEM((3 * RET_HEADS, CHUNK, LANES), F32),
                        pltpu.VMEM((RET_SEQS * RET_HEADS, LANES, LANES), F32)],
        compiler_params=_cparams("parallel"),
        name="retention",
    )(zb, cos, sin, nw)


def _da_kernel(*refs):
    z_refs = refs[:9]
    wq_ref, wk_ref, o_ref, qn_ref, kn_ref, v_ref, op_ref, zp_ref = refs[9:17]
    on_refs, zn_refs, spare_refs = refs[17:20], refs[20:23], refs[23:26]
    stage_refs = refs[26:29]
    S = o_ref.shape[0]
    lane = lax.broadcasted_iota(jnp.int32, (1, LANES), 1)
    lo = lane < DA_DH
    row = lax.broadcasted_iota(jnp.int32, (CHUNK, CHUNK), 0)
    col = lax.broadcasted_iota(jnp.int32, (CHUNK, CHUNK), 1)
    mask_cur = col <= row
    row2 = lax.broadcasted_iota(jnp.int32, (CHUNK, 2 * CHUNK), 0)
    col2 = lax.broadcasted_iota(jnp.int32, (CHUNK, 2 * CHUNK), 1)
    mask_band = (col2 >= row2) & (col2 <= row2 + CHUNK)

    seg = (lax.broadcasted_iota(jnp.int32, (LANES, LANES), 0) // DA_DH
           == lax.broadcasted_iota(jnp.int32, (LANES, LANES), 1) // DA_DH).astype(BF16) * (1.0 / DA_DH)

    def head_norm(x, w):
        x2 = x * x
        hi = x2.astype(BF16)
        lo_part = (x2 - hi.astype(F32)).astype(BF16)
        ms = _dot(hi, seg) + _dot(lo_part, seg)
        return x * lax.rsqrt(ms + EPS) * w

    RB = 256
    for g, (dil, nb) in enumerate(DA_GROUPS):
        L = S // dil
        zq_ref, zk_ref, zv_ref = z_refs[3 * g:3 * g + 3]
        if dil > 1:
            for piece in range(S // RB):
                rows = pl.ds(piece * RB, RB)
                for src_ref, dst_ref in zip((zq_ref, zk_ref, zv_ref), stage_refs):
                    dst_ref[rows, :] = src_ref[rows, :].astype(F32)
            zq_ref, zk_ref, zv_ref = stage_refs
        first = min(dil, DA_MAX_STRIDE)
        if dil > first:
            mid_refs = (on_refs[g], zn_refs[g], spare_refs[g])
            sub = S // first
            for b in range(first):
                for piece in range(sub // RB):
                    src = pl.ds(b + first * piece * RB, RB, stride=first)
                    dst = pl.ds(b * sub + piece * RB, RB)
                    for src_ref, dst_ref in zip((zq_ref, zk_ref, zv_ref), mid_refs):
                        dst_ref[dst, :] = src_ref[src, :]
            zq_ref, zk_ref, zv_ref = mid_refs
        rest = dil // first

        def source_rows(rho, piece, n_rows):
            if dil == 1:
                return pl.ds(piece * n_rows, n_rows)
            b, a = rho % first, rho // first
            if rest == 1:
                return pl.ds(b + first * piece * n_rows, n_rows, stride=first)
            return pl.ds(b * (S // first) + a + rest * piece * n_rows, n_rows, stride=rest)

        for rho in range(dil):
            for piece in range(max(L // RB, 1)):
                n_rows = min(RB, L)
                src = source_rows(rho, piece, n_rows)
                dst = pl.ds(rho * L + piece * n_rows, n_rows)
                q = zq_ref[src, :].astype(F32)
                k = zk_ref[src, :].astype(F32)
                v = zv_ref[src, :]
                qn_ref[dst, :] = (head_norm(q, wq_ref[...]) * (DA_DH ** -0.5)).astype(BF16)
                kn_ref[dst, :] = head_norm(k, wk_ref[...]).astype(BF16)
                v_ref[dst, :] = v.astype(BF16)

        def batch(r0, chained, first_has_prev):
            chains = []
            for u in range(DA_UNROLL):
                rows = pl.ds(r0 + u * CHUNK, CHUNK)
                if chained and (u > 0 or first_has_prev):
                    keys, mask = pl.ds(r0 + (u - 1) * CHUNK, 2 * CHUNK), mask_band
                else:
                    keys, mask = rows, mask_cur
                k = kn_ref[keys, :]
                v = v_ref[keys, :]
                vaug = jnp.concatenate([v, jnp.ones_like(v)], axis=1)
                q = qn_ref[rows, :]
                zero = jnp.zeros_like(q)
                for qh in (jnp.where(lo, q, zero), jnp.where(lo, zero, q)):
                    chains.append((_dot_nt(qh, k), mask, vaug))
            probs = []
            for s, mask, _ in chains:
                s = jnp.where(mask, s, NEG)
                m = jnp.max(s, axis=-1, keepdims=True)
                probs.append((jnp.exp(s - m).astype(BF16), m))
            accs = [_dot(p, vaug) for (p, _), (_, _, vaug) in zip(probs, chains)]
            normed, lses = [], []
            for u in range(DA_UNROLL):
                a0, a1 = accs[2 * u], accs[2 * u + 1]
                den = jnp.where(lo, a0[:, LANES:], a1[:, LANES:])
                normed.append(jnp.where(lo, a0[:, :LANES], a1[:, :LANES]) / den)
                lses.append(jnp.where(lo, probs[2 * u][1], probs[2 * u + 1][1]) + jnp.log(den))
            rows = pl.ds(r0, DA_UNROLL * CHUNK)
            op_ref[rows, :] = jnp.concatenate(normed, axis=0)
            zp_ref[rows, :] = jnp.concatenate(lses, axis=0)

        span = DA_UNROLL * CHUNK
        if nb == 1:
            def singles(i, c):
                batch(pl.multiple_of(i * span, span), False, False)
                return c
            lax.fori_loop(0, S // span, singles, 0)
        else:
            def segment(sgi, c, nb=nb):
                base = pl.multiple_of(sgi * (nb * CHUNK), span)
                batch(base, True, False)
                if nb > DA_UNROLL:
                    def inner(n, c2):
                        batch(pl.multiple_of(base + n * span, span), True, True)
                        return c2
                    lax.fori_loop(1, nb // DA_UNROLL, inner, 0)
                return c
            lax.fori_loop(0, dil, segment, 0)

        outs = ((op_ref, on_refs[g]), (zp_ref, zn_refs[g]))
        for rho in range(dil):
            for piece in range(max(L // RB, 1)):
                n_rows = min(RB, L)
                dst = source_rows(rho, piece, n_rows)
                src = pl.ds(rho * L + piece * n_rows, n_rows)
                for k_out, (perm_ref, nat_ref) in enumerate(outs):
                    (stage_refs[k_out] if rest > 1 else nat_ref)[dst, :] = perm_ref[src, :]
        if rest > 1:
            sub = S // first
            for b in range(first):
                for piece in range(sub // RB):
                    dst = pl.ds(b + first * piece * RB, RB, stride=first)
                    src = pl.ds(b * sub + piece * RB, RB)
                    for k_out, (_, nat_ref) in enumerate(outs):
                        nat_ref[dst, :] = stage_refs[k_out][src, :]

    for piece in range(S // RB):
        rows = pl.ds(piece * RB, RB)
        m = jnp.maximum(jnp.maximum(zn_refs[0][rows, :], zn_refs[1][rows, :]), zn_refs[2][rows, :])
        num = jnp.zeros((RB, LANES), F32)
        den = jnp.zeros((RB, LANES), F32)
        for g in range(3):
            e = jnp.exp(zn_refs[g][rows, :] - m)
            num = num + e * on_refs[g][rows, :]
            den = den + e
        o_ref[rows, :] = (num / den).astype(o_ref.dtype)


def _dilated_attention(zb, slab0, wq, wk, B, S):
    T = B * S
    slab = lambda k: pl.BlockSpec((S, LANES), lambda b, p: (b, slab0 + 9 * p + k))
    return pl.pallas_call(
        _da_kernel,
        grid=(B, 2),
        in_specs=[slab(k) for k in range(9)] + [
                  pl.BlockSpec((1, LANES), lambda b, p: (0, 0)),
                  pl.BlockSpec((1, LANES), lambda b, p: (0, 0))],
        out_specs=pl.BlockSpec((S, LANES), lambda b, p: (b, p)),
        out_shape=jax.ShapeDtypeStruct((T, 2 * LANES), BF16),
        scratch_shapes=[pltpu.VMEM((S, LANES), BF16), pltpu.VMEM((S, LANES), BF16),
                        pltpu.VMEM((S, LANES), BF16),
                        pltpu.VMEM((S, LANES), F32), pltpu.VMEM((S, LANES), F32)]
                       + [pltpu.VMEM((S, LANES), F32)] * 12,
        compiler_params=_cparams("parallel", "arbitrary"),
        name="dilated_attention",
    )(*([zb] * 9), wq, wk)


def _mlstm_kernel(qk_ref, v_ref, og_ref, gate_ref, gb_ref, nw_ref, o_ref, st_ref, e_ref, et_ref):
    S = qk_ref.shape[0] // ML_SEQS
    n_chunks = S // CHUNK
    W = qk_ref.shape[1]
    lane = lax.broadcasted_iota(jnp.int32, (1, LANES), 1)
    lo = lane < ML_DK
    row = lax.broadcasted_iota(jnp.int32, (CHUNK, CHUNK), 0)
    col = lax.broadcasted_iota(jnp.int32, (CHUNK, CHUNK), 1)
    causal = col <= row
    tril = causal.astype(BF16)
    srow = lax.broadcasted_iota(jnp.int32, (LANES, 1), 0) < ML_DK
    st_ref[...] = jnp.zeros(st_ref.shape, F32)

    is_f = (lane >= ML_HEADS) & (lane < 2 * ML_HEADS)
    for n in range(ML_SEQS * n_chunks):
        rows = pl.ds(n * CHUNK, CHUNK)
        gp = gate_ref[rows, :] + gb_ref[...]
        logf = jnp.minimum(gp, 0.0) - jnp.log(1.0 + jnp.exp(-jnp.abs(gp)))
        f1 = jnp.where(is_f, logf, 0.0)
        t1 = f1.astype(BF16)
        f2 = f1 - t1.astype(F32)
        t2 = f2.astype(BF16)
        t3 = (f2 - t2.astype(F32)).astype(BF16)
        cum = _dot(tril, t1) + _dot(tril, t2) + _dot(tril, t3)
        e = jnp.where(lane < ML_HEADS, gp, cum)
        e_ref[rows, :] = e
        et_ref[n] = e.T[0:2 * ML_HEADS, :]

    units = [(sq, h) for sq in range(ML_SEQS) for h in range(ML_HEADS)]

    def chunk(n, mms):
        rows_of = [pl.ds(pl.multiple_of(sq * S + n * CHUNK, CHUNK), CHUNK) for sq in range(ML_SEQS)]
        es = [e_ref[rows_of[sq], :] for sq in range(ML_SEQS)]
        ets = [et_ref[sq * n_chunks + n] for sq in range(ML_SEQS)]
        khs, vaugs, s_raws, iqs = [], [], [], []
        for sq in range(ML_SEQS):
            rows = rows_of[sq]
            for pair in range(2):
                q2 = qk_ref[rows, pair * LANES:(pair + 1) * LANES]
                k2 = qk_ref[rows, W // 2 + pair * LANES:W // 2 + (pair + 1) * LANES]
                zero = jnp.zeros_like(q2)
                stb = st_ref[2 * sq + pair].astype(BF16)
                for hs in range(2):
                    h = 2 * pair + hs
                    qh = jnp.where(lo, q2, zero) if hs == 0 else jnp.where(lo, zero, q2)
                    kh = jnp.where(lo, k2, zero) if hs == 0 else jnp.where(lo, zero, k2)
                    vh = v_ref[rows, h * LANES:(h + 1) * LANES]
                    s_raws.append(_dot_nt(qh, k2))
                    iqs.append(_dot(qh, stb))
                    khs.append(kh)
                    vaugs.append(jnp.concatenate([vh, jnp.ones_like(vh)], axis=-1))
        ps, wis, ms_, kws, wcs, new_mms = [], [], [], [], [], []
        for u, (sq, h) in enumerate(units):
            e, et = es[sq], ets[sq]
            mm = mms[u]
            i_b = jnp.broadcast_to(e[:, h:h + 1], (CHUNK, LANES))
            a_b = jnp.broadcast_to(e[:, ML_HEADS + h:ML_HEADS + h + 1], (CHUNK, LANES))
            c_row = et[h:h + 1, :] - et[ML_HEADS + h:ML_HEADS + h + 1, :]
            dmat = jnp.where(causal, a_b + c_row, NEG)
            inter = a_b + mm
            m = jnp.maximum(inter, jnp.max(dmat, axis=-1, keepdims=True))
            ps.append((s_raws[u] * jnp.exp(dmat - m)).astype(BF16))
            wis.append(jnp.exp(inter - m))
            ms_.append(m)
            m_new = m[CHUNK - 1:CHUNK, :]
            a_last = a_b[CHUNK - 1:CHUNK, :]
            wk = jnp.exp(a_last - a_b + i_b - m_new)
            kws.append((khs[u].astype(F32) * wk).astype(BF16))
            wc = jnp.exp(a_last + mm - m_new)
            wcs.append(jnp.concatenate([wc, wc], axis=1))
            new_mms.append(m_new)
        accs = [_dot(ps[u], vaugs[u]) + jnp.concatenate([wis[u], wis[u]], axis=1) * iqs[u]
                for u in range(len(units))]
        upds = [_dot_tn(kws[u], vaugs[u]) for u in range(len(units))]
        for sp in range(2 * ML_SEQS):
            u0, u1 = 2 * sp, 2 * sp + 1
            st_ref[sp] = jnp.where(srow, wcs[u0], wcs[u1]) * st_ref[sp] + (upds[u0] + upds[u1])
        for u, (sq, h) in enumerate(units):
            num = accs[u][:, :LANES]
            den = accs[u][:, LANES:]
            hv = num / jnp.maximum(jnp.abs(den), jnp.exp(-ms_[u]))
            var = jnp.mean(hv * hv, axis=-1, keepdims=True)
            hv = hv * lax.rsqrt(var + EPS) * nw_ref[:, h * LANES:(h + 1) * LANES]
            og = og_ref[rows_of[sq], h * LANES:(h + 1) * LANES].astype(F32)
            o_ref[rows_of[sq], h * LANES:(h + 1) * LANES] = (hv * _sigmoid(og)).astype(o_ref.dtype)
        return tuple(new_mms)

    lax.fori_loop(0, n_chunks, chunk, tuple(jnp.zeros((1, LANES), F32) for _ in units))


def _mlstm(zb, zf, gate_bias, nw, B, S, qk_blk, gate_blk):
    T = B * S
    W = ML_HEADS * LANES
    R = ML_SEQS * S
    return pl.pallas_call(
        _mlstm_kernel,
        grid=(B // ML_SEQS,),
        in_specs=[pl.BlockSpec((R, W), lambda b: (b, qk_blk)),
                  pl.BlockSpec((R, W), lambda b: (b, qk_blk + 1)),
                  pl.BlockSpec((R, W), lambda b: (b, qk_blk + 2)),
                  pl.BlockSpec((R, LANES), lambda b: (b, gate_blk)),
                  pl.BlockSpec((1, LANES), lambda b: (0, 0)),
                  pl.BlockSpec((1, W), lambda b: (0, 0))],
        out_specs=pl.BlockSpec((R, W), lambda b: (b, 0)),
        out_shape=jax.ShapeDtypeStruct((T, W), BF16),
        scratch_shapes=[pltpu.VMEM((2 * ML_SEQS, LANES, 2 * LANES), F32),
                        pltpu.VMEM((R, LANES), F32),
                        pltpu.VMEM((R // CHUNK, 2 * ML_HEADS, LANES), F32)],
        compiler_params=_cparams("parallel"),
        name="mlstm",
    )(zb, zb, zb, zf, gate_bias, nw)


def _merge_value(x_ref, oret_ref, oda_ref, oml_ref, gr_ref, gd_ref, gm_ref,
                 wr_ref, wd_ref, wm_ref, wo_ref):
    y = _sigmoid(gr_ref[...].astype(F32)) * _dot(oret_ref[...], wr_ref[...])
    y = y + _sigmoid(gd_ref[...].astype(F32)) * _dot(oda_ref[...], wd_ref[...])
    y = y + _sigmoid(gm_ref[...].astype(F32)) * _dot(oml_ref[...], wm_ref[...])
    return x_ref[...] + _dot(y.astype(BF16), wo_ref[...])


def _merge_specs(x, o_ret, o_da, o_ml, g_blk, tm):
    D = x.shape[1]
    return [pl.BlockSpec((tm, D), lambda i: (i, 0)),
            pl.BlockSpec((tm, o_ret.shape[1]), lambda i: (i, 0)),
            pl.BlockSpec((tm, o_da.shape[1]), lambda i: (i, 0)),
            pl.BlockSpec((tm, o_ml.shape[1]), lambda i: (i, 0)),
            pl.BlockSpec((tm, D), lambda i: (i, g_blk)),
            pl.BlockSpec((tm, D), lambda i: (i, g_blk + 1)),
            pl.BlockSpec((tm, D), lambda i: (i, g_blk + 2))]


def _resident(a):
    return pl.BlockSpec(a.shape, lambda i: (0,) * a.ndim, pipeline_mode=pl.Buffered(1))


def _swiglu_block(h, wg_ref, wu_ref, wd_ref, a_ref):
    tf = wg_ref.shape[1]
    c0 = 0
    while c0 < tf:
        w = min(MXU_N, tf - c0)
        g = _dot(h, wg_ref[:, c0:c0 + w].astype(BF16))
        u = _dot(h, wu_ref[:, c0:c0 + w].astype(BF16))
        a_ref[:, c0:c0 + w] = (_silu(g) * u).astype(BF16)
        c0 += w
    return _dot(a_ref[...], wd_ref[...].astype(BF16))


def _merge_ffn_kernel(*refs):
    merge_refs, (nw_ref, wg_ref, wu_ref, wdn_ref, o_ref, a_ref) = refs[:11], refs[11:]
    x = _merge_value(*merge_refs)
    ms = jnp.mean(x * x, axis=-1, keepdims=True)
    h = (x * lax.rsqrt(ms + EPS) * nw_ref[...]).astype(BF16)
    o_ref[...] = x + _swiglu_block(h, wg_ref, wu_ref, wdn_ref, a_ref)


def _merge_ffn(x, o_ret, o_da, o_ml, zb, g_blk, wr, wd, wm, wo, nw, wg, wu, wdn, tm):
    T, D = x.shape
    weights = (wr, wd, wm, wo, nw, wg, wu, wdn)
    return pl.pallas_call(
        _merge_ffn_kernel,
        grid=(T // tm,),
        in_specs=_merge_specs(x, o_ret, o_da, o_ml, g_blk, tm) + [_resident(a) for a in weights],
        out_specs=pl.BlockSpec((tm, D), lambda i: (i, 0)),
        out_shape=jax.ShapeDtypeStruct((T, D), F32),
        scratch_shapes=[pltpu.VMEM((tm, wg.shape[1]), BF16)],
        compiler_params=_cparams("parallel"),
        name="merge_ffn",
    )(x, o_ret, o_da, o_ml, zb, zb, zb, *weights)


HI16 = 0xFFFF0000


def _pack_bf16_pairs(a, b):
    ab = pltpu.bitcast(a.astype(BF16).astype(F32), jnp.uint32)
    bb = pltpu.bitcast(b.astype(BF16).astype(F32), jnp.uint32)
    return (ab >> 16) | (bb & jnp.uint32(HI16))


def _unpack_bf16_pairs(p):
    lo = pltpu.bitcast(p << 16, F32)
    hi = pltpu.bitcast(p & jnp.uint32(HI16), F32)
    return lo, hi


def _merge_router_kernel(*refs):
    merge_refs, (nw_ref, wr_ref, br_ref, xo_ref, ha_ref, hb_ref, sel_ref, cnt_ref) = refs[:11], refs[11:]
    x = _merge_value(*merge_refs)
    xo_ref[...] = x
    ms = jnp.mean(x * x, axis=-1, keepdims=True)
    h = x * lax.rsqrt(ms + EPS) * nw_ref[...]
    Q = h.shape[1] // 4
    ha_ref[...] = _pack_bf16_pairs(h[:, 0:Q], h[:, Q:2 * Q])
    hb_ref[...] = _pack_bf16_pairs(h[:, 2 * Q:3 * Q], h[:, 3 * Q:4 * Q])
    w = wr_ref[...]
    h_hi = h.astype(BF16)
    h_lo = (h - h_hi.astype(F32)).astype(BF16)
    w_hi = w.astype(BF16)
    w_lo = (w - w_hi.astype(F32)).astype(BF16)
    logits = _dot(h_hi, w_hi) + _dot(h_lo, w_hi) + _dot(h_hi, w_lo) + br_ref[...]
    lane = lax.broadcasted_iota(jnp.int32, logits.shape, 1).astype(F32)
    logits = jnp.where(lane < N_EXPERTS, logits, NEG)
    m1 = jnp.max(logits, axis=-1, keepdims=True)
    i1 = jnp.min(jnp.where(logits == m1, lane, float(LANES)), axis=-1, keepdims=True)
    rest = jnp.where(lane == i1, NEG, logits)
    m2 = jnp.max(rest, axis=-1, keepdims=True)
    i2 = jnp.min(jnp.where(rest == m2, lane, float(LANES)), axis=-1, keepdims=True)
    e2 = jnp.exp(m2 - m1)
    p1 = 1.0 / (1.0 + e2)
    p2 = e2 / (1.0 + e2)
    sel_ref[...] = jnp.where(lane == 0.0, i1, jnp.where(lane == 1.0, i2,
                             jnp.where(lane == 2.0, p1, jnp.where(lane == 3.0, p2, 0.0))))
    chosen = jnp.where((lane == i1) | (lane == i2), 1.0, 0.0)
    cnt_ref[...] = jnp.broadcast_to(jnp.sum(chosen, axis=0, keepdims=True), cnt_ref.shape)


def _merge_router(x, o_ret, o_da, o_ml, zb, g_blk, wr, wd, wm, wo, nw, w_router, b_router, tm):
    T, D = x.shape
    Q = D // 4
    wrt = jnp.pad(w_router, ((0, 0), (0, LANES - N_EXPERTS)))
    brt = jnp.pad(b_router, (0, LANES - N_EXPERTS)).reshape(1, LANES)
    weights = (wr, wd, wm, wo, nw, wrt, brt)
    return pl.pallas_call(
        _merge_router_kernel,
        grid=(T // tm,),
        in_specs=_merge_specs(x, o_ret, o_da, o_ml, g_blk, tm) + [_resident(a) for a in weights],
        out_specs=[pl.BlockSpec((tm, D), lambda i: (i, 0)),
                   pl.BlockSpec((tm, Q), lambda i: (i, 0)),
                   pl.BlockSpec((tm, Q), lambda i: (i, 0)),
                   pl.BlockSpec((tm, LANES), lambda i: (i, 0)),
                   pl.BlockSpec((None, 8, LANES), lambda i: (i, 0, 0))],
        out_shape=[jax.ShapeDtypeStruct((T, D), F32),
                   jax.ShapeDtypeStruct((T, Q), jnp.uint32), jax.ShapeDtypeStruct((T, Q), jnp.uint32),
                   jax.ShapeDtypeStruct((T, LANES), F32),
                   jax.ShapeDtypeStruct((T // tm, 8, LANES), F32)],
        compiler_params=_cparams("parallel"),
        name="merge_router",
    )(x, o_ret, o_da, o_ml, zb, zb, zb, *weights)


def _moe_rank_kernel(sel_ref, tcnt_ref, pos_ref, meta_ref, cnt_ref, offs_ref, carry_ref, before_ref):
    i = pl.program_id(0)
    tm = sel_ref.shape[0]
    lane = lax.broadcasted_iota(jnp.int32, (tm, LANES), 1).astype(F32)
    lane1 = lax.broadcasted_iota(jnp.int32, (1, LANES), 1).astype(F32)
    sel = sel_ref[...]
    i1 = sel[:, 0:1]
    i2 = sel[:, 1:2]
    onehot = jnp.where((lane == i1) | (lane == i2), 1.0, 0.0)
    colsum = jnp.sum(onehot, axis=0, keepdims=True)

    def padded_counts():
        return jnp.floor((cnt_ref[...] + (MOE_TILE - 1.0)) * (1.0 / MOE_TILE)) * MOE_TILE

    @pl.when(i == 0)
    def _():
        cnt_ref[...] = jnp.sum(tcnt_ref[...], axis=0)
        k = lax.broadcasted_iota(jnp.int32, (LANES, LANES), 0)
        e = lax.broadcasted_iota(jnp.int32, (LANES, LANES), 1)
        upper = (k < e).astype(F32)
        offs_ref[...] = jnp.dot(padded_counts(), upper, preferred_element_type=F32,
                                precision=lax.Precision.HIGHEST)
        carry_ref[...] = jnp.zeros(carry_ref.shape, F32)
        r = lax.broadcasted_iota(jnp.int32, (tm, tm), 0)
        c = lax.broadcasted_iota(jnp.int32, (tm, tm), 1)
        before_ref[...] = (c < r).astype(BF16)

    rank = _dot(before_ref[...], onehot.astype(BF16)) + carry_ref[0:1, :]
    row = rank + offs_ref[0:1, :]
    pos1 = jnp.sum(jnp.where(lane == i1, row, 0.0), axis=-1, keepdims=True)
    pos2 = jnp.sum(jnp.where(lane == i2, row, 0.0), axis=-1, keepdims=True)
    both = jnp.where(lane == 0.0, pos1, jnp.where(lane == 1.0, pos2, 0.0))
    pos_ref[...] = both.T[0:8, :].astype(jnp.int32)
    carry_ref[...] += colsum

    @pl.when(i == pl.num_programs(0) - 1)
    def _():
        nrow = meta_ref.shape[0]
        padded = padded_counts()[0:1, :]
        offs = offs_ref[0:1, :]
        ends = offs + padded
        start = lax.broadcasted_iota(jnp.int32, (nrow, 1), 0).astype(F32) * MOE_TILE
        is_e = lane1 < N_EXPERTS
        te = jnp.sum(jnp.where(is_e & (ends <= start), 1.0, 0.0), axis=-1, keepdims=True)
        te = jnp.minimum(te, N_EXPERTS - 1.0)
        valid_end = jnp.sum(jnp.where(lane1 == te, offs + cnt_ref[0:1, :], 0.0), axis=-1, keepdims=True)
        nv = jnp.clip(valid_end - start, 0.0, MOE_TILE)
        total = jnp.sum(jnp.where(is_e, padded, 0.0), axis=-1, keepdims=True)
        active = jnp.where(start < total, 1.0, 0.0)
        meta_ref[...] = jnp.where(lane1 == 0.0, te, jnp.where(lane1 == 1.0, nv,
                                  jnp.where(lane1 == 2.0, active, 0.0)))


def _moe_rank(sel, tcnt, tm, meta_rows):
    T = sel.shape[0]
    nt = T // tm
    return pl.pallas_call(
        _moe_rank_kernel,
        grid=(nt,),
        in_specs=[pl.BlockSpec((tm, LANES), lambda i: (i, 0)),
                  pl.BlockSpec(tcnt.shape, lambda i: (0, 0, 0))],
        out_specs=[pl.BlockSpec((None, 8, tm), lambda i: (i, 0, 0)),
                   pl.BlockSpec((meta_rows, LANES), lambda i: (0, 0))],
        out_shape=[jax.ShapeDtypeStruct((nt, 8, tm), jnp.int32),
                   jax.ShapeDtypeStruct((meta_rows, LANES), F32)],
        scratch_shapes=[pltpu.VMEM((8, LANES), F32), pltpu.VMEM((8, LANES), F32),
                        pltpu.VMEM((8, LANES), F32), pltpu.VMEM((tm, tm), BF16)],
        compiler_params=_cparams("arbitrary"),
        name="moe_rank",
    )(sel, tcnt)


def _sc_mesh():
    return plsc.VectorSubcoreMesh(core_axis_name="core", subcore_axis_name="subcore")


def _sc_scatter_rows(x, idx, n_rows):
    n_idx = idx.shape[1]
    nt = x.shape[0] // SC_WINDOW
    width = x.shape[1]

    @functools.partial(pl.kernel, out_type=jax.ShapeDtypeStruct((n_rows, width), x.dtype),
                       mesh=_sc_mesh())
    def scatter_kernel(x_hbm, i_hbm, o_hbm):
        def body(x_vmem, i_vmem):
            pltpu.sync_copy(x_vmem, o_hbm.at[i_vmem.at[0]])

        pltpu.emit_pipeline(
            body, grid=(n_idx // SC_WINDOW,),
            in_specs=[pl.BlockSpec((SC_WINDOW, width), lambda i: (i % nt, 0)),
                      pl.BlockSpec((1, SC_WINDOW), lambda i: (0, i))],
            out_specs=[],
            core_axis_name=("core", "subcore"), dimension_semantics=(pltpu.PARALLEL,),
        )(x_hbm, i_hbm)

    return scatter_kernel(x, idx)


def _sc_gather_rows(x, idx):
    n_idx = idx.shape[1]
    width = x.shape[1]

    @functools.partial(pl.kernel, out_type=jax.ShapeDtypeStruct((n_idx, width), x.dtype),
                       mesh=_sc_mesh())
    def gather_kernel(x_hbm, i_hbm, o_hbm):
        def body(i_vmem, o_vmem):
            pltpu.sync_copy(x_hbm.at[i_vmem.at[0]], o_vmem)

        pltpu.emit_pipeline(
            body, grid=(n_idx // SC_WINDOW,),
            in_specs=[pl.BlockSpec((1, SC_WINDOW), lambda i: (0, i))],
            out_specs=[pl.BlockSpec((SC_WINDOW, width), lambda i: (i, 0))],
            core_axis_name=("core", "subcore"), dimension_semantics=(pltpu.PARALLEL,),
        )(i_hbm, o_hbm)

    return gather_kernel(x, idx)


def _moe_group_kernel(te_ref, nv_ref, na_ref, xa_ref, xb_ref, wg_ref, wu_ref, wd_ref,
                      ya_ref, yb_ref, h_ref, a_ref):
    j = pl.program_id(0)

    @pl.when(j < na_ref[0])
    def _():
        tm = h_ref.shape[0]
        Q = xa_ref.shape[1]
        valid = lax.broadcasted_iota(jnp.int32, (tm, 1), 0) < nv_ref[j]
        for src, c0 in ((xa_ref, 0), (xb_ref, 2 * Q)):
            lo, hi = _unpack_bf16_pairs(src[...])
            h_ref[:, c0:c0 + Q] = jnp.where(valid, lo, 0.0).astype(BF16)
            h_ref[:, c0 + Q:c0 + 2 * Q] = jnp.where(valid, hi, 0.0).astype(BF16)
        y = _swiglu_block(h_ref[...], wg_ref, wu_ref, wd_ref, a_ref)
        ya_ref[...] = _pack_bf16_pairs(y[:, 0:Q], y[:, Q:2 * Q])
        yb_ref[...] = _pack_bf16_pairs(y[:, 2 * Q:3 * Q], y[:, 3 * Q:4 * Q])


def _moe_group(te, nv, na, xa, xb, wg, wu, wd):
    R, Q = xa.shape
    E, D, F = wg.shape
    tile = lambda j, te, nv, na: (jnp.minimum(j, na[0] - 1), 0)
    expert = lambda j, te, nv, na: (te[j], 0, 0)
    once = pl.Buffered(1)
    grid_spec = pltpu.PrefetchScalarGridSpec(
        num_scalar_prefetch=3,
        grid=(R // MOE_TILE,),
        in_specs=[pl.BlockSpec((MOE_TILE, Q), tile),
                  pl.BlockSpec((MOE_TILE, Q), tile),
                  pl.BlockSpec((None, D, F), expert),
                  pl.BlockSpec((None, D, F), expert),
                  pl.BlockSpec((None, F, D), expert, pipeline_mode=once)],
        out_specs=[pl.BlockSpec((MOE_TILE, Q), tile), pl.BlockSpec((MOE_TILE, Q), tile)],
        scratch_shapes=[pltpu.VMEM((MOE_TILE, D), BF16), pltpu.VMEM((MOE_TILE, F), BF16)],
    )
    return pl.pallas_call(
        _moe_group_kernel,
        grid_spec=grid_spec,
        out_shape=[jax.ShapeDtypeStruct((R, Q), jnp.uint32), jax.ShapeDtypeStruct((R, Q), jnp.uint32)],
        compiler_params=_cparams("arbitrary"),
        name="moe_experts",
    )(te, nv, na, xa, xb, wg, wu, wd)


def _moe_combine_kernel(x_ref, sel_ref, g1_ref, g2_ref, *rest):
    o_ref = rest[-1]
    sel = sel_ref[...]
    p1 = sel[:, 2:3]
    p2 = sel[:, 3:4]
    Q = g1_ref.shape[1]
    lo1, hi1 = _unpack_bf16_pairs(g1_ref[...])
    lo2, hi2 = _unpack_bf16_pairs(g2_ref[...])
    o_ref[:, 0:Q] = x_ref[:, 0:Q] + (p1 * lo1 + p2 * lo2)
    o_ref[:, Q:2 * Q] = x_ref[:, Q:2 * Q] + (p1 * hi1 + p2 * hi2)


def _moe_combine_half(x, sel, g, half, partial_out, tm):
    T, D = x.shape
    Q = g.shape[1]
    nt = T // tm
    in_specs = [pl.BlockSpec((tm, 2 * Q), lambda i: (i, half)),
                pl.BlockSpec((tm, LANES), lambda i: (i, 0)),
                pl.BlockSpec((tm, Q), lambda i: (i, 0)),
                pl.BlockSpec((tm, Q), lambda i: (i + nt, 0))]
    args = [x, sel, g, g]
    aliases = {}
    if partial_out is not None:
        in_specs.append(pl.BlockSpec(memory_space=pl.ANY))
        args.append(partial_out)
        aliases = {4: 0}
    return pl.pallas_call(
        _moe_combine_kernel,
        grid=(nt,),
        in_specs=in_specs,
        out_specs=pl.BlockSpec((tm, 2 * Q), lambda i: (i, half)),
        out_shape=jax.ShapeDtypeStruct((T, D), F32),
        input_output_aliases=aliases,
        compiler_params=_cparams("parallel"),
        name="moe_combine",
    )(*args)


def _cast_kernel(x_ref, o_ref):
    o_ref[...] = x_ref[...].astype(o_ref.dtype)


def _cast_bf16(w, rows):
    cols = w.shape[-1]
    w2 = w.reshape(-1, cols)
    out = pl.pallas_call(
        _cast_kernel,
        grid=(w2.shape[0] // rows,),
        in_specs=[pl.BlockSpec((rows, cols), lambda i: (i, 0))],
        out_specs=pl.BlockSpec((rows, cols), lambda i: (i, 0)),
        out_shape=jax.ShapeDtypeStruct(w2.shape, BF16),
        compiler_params=_cparams("parallel"),
        name="cast_bf16",
    )(w2)
    return out.reshape(w.shape)


def _moe(x, ha, hb, sel, tcnt, wg, wu, wd):
    T, D = x.shape
    n_rows = TOP_K * T + N_EXPERTS * MOE_TILE
    n_tiles = n_rows // MOE_TILE
    pos, meta = _moe_rank(sel, tcnt, WIDE_ROW_TILE, 256)
    idx = jnp.concatenate([pos[:, 0, :].reshape(1, T), pos[:, 1, :].reshape(1, T)], axis=1)
    te = meta[:n_tiles, 0].astype(jnp.int32)
    nv = meta[:n_tiles, 1].astype(jnp.int32)
    na = jnp.sum(meta[:n_tiles, 2]).astype(jnp.int32).reshape(1)
    te = jnp.where(jnp.arange(n_tiles) < na[0], te, te[na[0] - 1])
    xa = _sc_scatter_rows(ha, idx, n_rows)
    xb = _sc_scatter_rows(hb, idx, n_rows)
    ya, yb = _moe_group(te, nv, na, xa, xb, wg, wu, wd)
    ga = _sc_gather_rows(ya, idx)
    gb = _sc_gather_rows(yb, idx)
    out = _moe_combine_half(x, sel, ga, 0, None, 2 * WIDE_ROW_TILE)
    return _moe_combine_half(x, sel, gb, 1, out, 2 * WIDE_ROW_TILE)


def _split_w_in(w_in):
    sizes = (512, 512, 512, 512, 768, 768, 768, 256, 256, 512, 512, 4, 4, 1024, 1024, 1024)
    offs = [0]
    for s in sizes:
        offs.append(offs[-1] + s)
    part = lambda i: w_in[..., offs[i]:offs[i + 1]]
    rq, rk, rv, rg, dq, dk, dv, mq, mk, mv, mo, mi, mf, g_ret, g_da, g_ml = (part(i) for i in range(16))
    cols = []
    for p in range(2):
        for g in range(3):
            for t in (dq, dk, dv):
                cols.append(t[..., g * 256 + p * 128:g * 256 + (p + 1) * 128])
    wb = jnp.concatenate([rq, rk, rv, rg, g_ret, g_da, g_ml, mq, mk, mv, mo] + cols,
                         axis=-1).astype(BF16)
    pad = jnp.zeros(w_in.shape[:-1] + (LANES - 8,), w_in.dtype)
    wf = jnp.concatenate([mi, mf, pad], axis=-1).astype(BF16)
    return wb, wf


def _rope_tables(S):
    half = LANES // 2
    inv = jnp.power(ROPE_BASE, -jnp.arange(half, dtype=F32) / half)
    ang = jnp.arange(S, dtype=F32)[:, None] * inv[None, :]
    cos = jnp.cos(ang)
    sin = jnp.sin(ang)
    return jnp.concatenate([cos, cos], axis=1), jnp.concatenate([-sin, sin], axis=1)


def kernel(x, norm1_w, w_in, ret_norm_w, da_q_norm_w, da_k_norm_w, ml_conv_w, ml_i_bias, ml_f_bias,
           ml_norm_w, w_br_ret, w_br_da, w_br_ml, w_out, norm2_w, ffn_w_gate, ffn_w_up, ffn_w_down,
           moe_w_router, moe_b_router, moe_w_gate, moe_w_up, moe_w_down):
    B, S, D = x.shape
    T = B * S
    depth = w_in.shape[0]
    cos, sin = _rope_tables(S)
    wb_all, wf_all = _split_w_in(w_in)
    xt = x.reshape(T, D)
    for layer in range(depth):
        wb, wf = wb_all[layer], wf_all[layer]
        nw1 = norm1_w[layer].reshape(1, D)
        assert OFF_MLQK % INPROJ_TN == 0
        zb, zf = _inproj(xt, nw1, wb, wf, ml_conv_w[layer], OFF_MLQK // INPROJ_TN, S, INPROJ_TN)
        o_ret = _retention(zb, cos, sin, ret_norm_w[layer].reshape(1, -1), B, S)
        wq = jnp.tile(da_q_norm_w[layer], 2).reshape(1, LANES)
        wk = jnp.tile(da_k_norm_w[layer], 2).reshape(1, LANES)
        o_da = _dilated_attention(zb, OFF_DA // LANES, wq, wk, B, S)
        gate_bias = jnp.concatenate([ml_i_bias[layer], ml_f_bias[layer],
                                     jnp.zeros((LANES - 2 * ML_HEADS,), F32)]).reshape(1, LANES)
        o_ml = _mlstm(zb, zf, gate_bias, ml_norm_w[layer].reshape(1, -1), B, S,
                      qk_blk=OFF_MLQK // (ML_HEADS * LANES), gate_blk=0)
        merge_args = (xt, o_ret, o_da, o_ml, zb, OFF_GATES // D_MODEL,
                      w_br_ret[layer].astype(BF16), w_br_da[layer].astype(BF16),
                      w_br_ml[layer].astype(BF16), w_out[layer].astype(BF16),
                      norm2_w[layer].reshape(1, D))
        j = layer // 2
        if layer % 2 == 0:
            xt = _merge_ffn(*merge_args, ffn_w_gate[j].astype(BF16), ffn_w_up[j].astype(BF16),
                            ffn_w_down[j].astype(BF16), ROW_TILE)
        else:
            xt, ha, hb, sel, tcnt = _merge_router(*merge_args, moe_w_router[j], moe_b_router[j],
                                                  ROW_TILE)
            xt = _moe(xt, ha, hb, sel, tcnt, _cast_bf16(moe_w_gate[j], 512), _cast_bf16(moe_w_up[j], 512),
                      _cast_bf16(moe_w_down[j], 2048))
    return xt.reshape(B, S, D)
```

```python
import functools
import math

import jax
import jax.numpy as jnp
from jax import lax
from jax.experimental import pallas as pl
from jax.experimental.pallas import tpu as pltpu
from jax.experimental.pallas import tpu_sc as plsc

F32 = jnp.float32
BF16 = jnp.bfloat16

EPS = 1e-6
D_MODEL = 1024
CHUNK = 128
LANES = 128
MXU_N = 256
ROPE_BASE = 10000.0
RET_HEADS = 4
RET_SEQS = 2
DA_GROUPS = ((1, 16), (4, 4), (16, 1))
DA_DH = 64
DA_UNROLL = 4
DA_MAX_STRIDE = 4
ML_HEADS = 4
ML_DK = 64
ML_SEQS = 2
N_EXPERTS = 8
TOP_K = 2
MOE_TILE = 512
SC_WINDOW = 128
INPROJ_TN = 1280
ROW_TILE = 512
WIDE_ROW_TILE = 1024
OFF_GATES = 2048
OFF_MLQK = OFF_GATES + 3 * D_MODEL
OFF_DA = OFF_MLQK + 3 * ML_HEADS * LANES
VMEM_LIMIT = 56 * 1024 * 1024

NEG = -1e30


def _cparams(*sem):
    return pltpu.CompilerParams(dimension_semantics=sem, vmem_limit_bytes=VMEM_LIMIT)


def _dot(a, b):
    return jnp.dot(a, b, preferred_element_type=F32)


def _dot_nt(a, b):
    return lax.dot_general(a, b, (((1,), (1,)), ((), ())), preferred_element_type=F32)


def _dot_tn(a, b):
    return lax.dot_general(a, b, (((0,), (0,)), ((), ())), preferred_element_type=F32)


def _sigmoid(x):
    return 1.0 / (1.0 + jnp.exp(-x))


def _silu(x):
    return x * _sigmoid(x)


def _inproj_kernel(conv_blk, x_ref, nw_ref, wb_ref, wf_ref, cw_ref, zb_ref, zf_ref, h_ref, cv_ref):
    j = pl.program_id(1)
    tm = x_ref.shape[0]

    @pl.when(j == 0)
    def _():
        rc = 512
        for r in range(tm // rc):
            rows = pl.ds(r * rc, rc)
            x = x_ref[rows, :]
            ms = jnp.mean(x * x, axis=-1, keepdims=True)
            h = (x * lax.rsqrt(ms + EPS) * nw_ref[...]).astype(BF16)
            h_ref[rows, :] = h
            zb_ref[rows, :] = _dot(h, wb_ref[...]).astype(zb_ref.dtype)
            zf_ref[rows, :] = _dot(h, wf_ref[...])

    @pl.when((j > 0) & (j != conv_blk))
    def _():
        zb_ref[...] = _dot(h_ref[...], wb_ref[...]).astype(zb_ref.dtype)

    @pl.when(j == conv_blk)
    def _():
        h = h_ref[...]
        cw = cw_ref.shape[1]
        pad = cv_ref.shape[0] - tm
        cv_ref[0:pad, :] = jnp.zeros((pad, cw), F32)
        cv_ref[pad:pad + tm, :] = _dot(h, wb_ref[:, 0:cw])
        taps = cw_ref.shape[0]
        rb = 256

        def conv(piece, cs):
            cols = slice(cs * LANES, (cs + 1) * LANES)
            xe = cv_ref[pl.ds(piece * rb, rb + pad), cols]
            acc = cw_ref[0:1, cols] * xe
            for i in range(1, taps):
                acc = cw_ref[i:i + 1, cols] * xe + pltpu.roll(acc, 1, 0)
            acc = acc[pad:, :]
            scale = ML_DK ** -0.5 if cs >= cw // LANES // 2 else 1.0
            zb_ref[piece * rb:(piece + 1) * rb, cols] = (_silu(acc) * scale).astype(zb_ref.dtype)

        todo = [(piece, cs) for piece in range(tm // rb) for cs in range(cw // LANES)]
        chunks = list(range(cw, zb_ref.shape[1], MXU_N))
        per = -(-len(todo) // len(chunks))
        for n, c0 in enumerate(chunks):
            c1 = min(c0 + MXU_N, zb_ref.shape[1])
            zb_ref[:, c0:c1] = _dot(h, wb_ref[:, c0:c1]).astype(zb_ref.dtype)
            for piece, cs in todo[n * per:(n + 1) * per]:
                conv(piece, cs)


def _inproj(x, nw, wb, wf, conv_w, conv_blk, tm, tnb):
    T, D = x.shape
    return pl.pallas_call(
        functools.partial(_inproj_kernel, conv_blk),
        grid=(T // tm, wb.shape[1] // tnb),
        in_specs=[pl.BlockSpec((tm, D), lambda i, j: (i, 0)),
                  pl.BlockSpec((1, D), lambda i, j: (0, 0)),
                  pl.BlockSpec((D, tnb), lambda i, j: (0, j)),
                  pl.BlockSpec(wf.shape, lambda i, j: (0, 0)),
                  pl.BlockSpec(conv_w.shape, lambda i, j: (0, 0))],
        out_specs=[pl.BlockSpec((tm, tnb), lambda i, j: (i, j)),
                   pl.BlockSpec((tm, wf.shape[1]), lambda i, j: (i, 0))],
        out_shape=[jax.ShapeDtypeStruct((T, wb.shape[1]), BF16),
                   jax.ShapeDtypeStruct((T, wf.shape[1]), F32)],
        scratch_shapes=[pltpu.VMEM((tm, D), BF16), pltpu.VMEM((tm + 8, conv_w.shape[1]), F32)],
        compiler_params=_cparams("parallel", "arbitrary"),
        name="inproj",
    )(x, nw, wb, wf, conv_w)


def _retention_kernel(z_ref, cos_ref, sin_ref, nw_ref, o_ref, dec_ref, st_ref):
    S = z_ref.shape[0] // RET_SEQS
    n_chunks = S // CHUNK
    H = RET_HEADS
    HW = H * LANES
    row = lax.broadcasted_iota(jnp.int32, (CHUNK, CHUNK), 0).astype(F32)
    col = lax.broadcasted_iota(jnp.int32, (CHUNK, CHUNK), 1).astype(F32)
    lgs = [math.log1p(-(2.0 ** (-5.0 - h))) for h in range(H)]
    for h, lg in enumerate(lgs):
        rel = row - col
        dec_ref[h] = jnp.where(rel >= 0, jnp.exp(lg * jnp.maximum(rel, 0.0)), 0.0)
        dec_ref[H + h] = jnp.exp(lg * (row + 1.0))
        dec_ref[2 * H + h] = jnp.exp(lg * (CHUNK - 1.0 - row)) * (LANES ** -0.5)
    st_ref[...] = jnp.zeros(st_ref.shape, F32)
    units = [(sq, h) for sq in range(RET_SEQS) for h in range(H)]

    def body(n, carry):
        r0 = pl.multiple_of(n * CHUNK, CHUNK)
        cos = cos_ref[pl.ds(r0, CHUNK), :]
        sin = sin_ref[pl.ds(r0, CHUNK), :]
        rows_of = [pl.ds(pl.multiple_of(sq * S + n * CHUNK, CHUNK), CHUNK) for sq in range(RET_SEQS)]
        vs, ss, iqs, kvs = [], [], [], []
        for u, (sq, h) in enumerate(units):
            rows = rows_of[sq]
            q = z_ref[rows, h * LANES:(h + 1) * LANES].astype(F32)
            k = z_ref[rows, HW + h * LANES:HW + (h + 1) * LANES].astype(F32)
            v = z_ref[rows, 2 * HW + h * LANES:2 * HW + (h + 1) * LANES]
            q = q * cos + pltpu.roll(q, LANES // 2, 1) * sin
            k = k * cos + pltpu.roll(k, LANES // 2, 1) * sin
            qb = q.astype(BF16)
            kb = (k * (LANES ** -0.5)).astype(BF16)
            kd = (k * dec_ref[2 * H + h]).astype(BF16)
            ss.append(_dot_nt(qb, kb))
            iqs.append(_dot(qb, st_ref[u].astype(BF16)))
            kvs.append(_dot_tn(kd, v))
            vs.append(v)
        ps = [(ss[u] * dec_ref[h]).astype(BF16) for u, (_, h) in enumerate(units)]
        os_ = [_dot(ps[u], vs[u]) + iqs[u] * dec_ref[H + h] for u, (_, h) in enumerate(units)]
        for u, (sq, h) in enumerate(units):
            rows = rows_of[sq]
            st_ref[u] = st_ref[u] * math.exp(lgs[h] * CHUNK) + kvs[u]
            o = os_[u]
            ms = jnp.mean(o * o, axis=-1, keepdims=True)
            o = o * lax.rsqrt(ms + EPS) * nw_ref[:, h * LANES:(h + 1) * LANES]
            g = z_ref[rows, 3 * HW + h * LANES:3 * HW + (h + 1) * LANES].astype(F32)
            o_ref[rows, h * LANES:(h + 1) * LANES] = (o * _silu(g)).astype(o_ref.dtype)
        return carry

    lax.fori_loop(0, n_chunks, body, 0)


def _retention(zb, cos, sin, nw, B, S):
    T = B * S
    W = RET_HEADS * LANES
    R = RET_SEQS * S
    return pl.pallas_call(
        _retention_kernel,
        grid=(B // RET_SEQS,),
        in_specs=[pl.BlockSpec((R, 4 * W), lambda b: (b, 0)),
                  pl.BlockSpec((S, LANES), lambda b: (0, 0)),
                  pl.BlockSpec((S, LANES), lambda b: (0, 0)),
                  pl.BlockSpec((1, W), lambda b: (0, 0))],
        out_specs=pl.BlockSpec((R, W), lambda b: (b, 0)),
        out_shape=jax.ShapeDtypeStruct((T, W), BF16),
        scratch_shapes=[pltpu.VMEM((3 * RET_HEADS, CHUNK, LANES), F32),
                        pltpu.VMEM((RET_SEQS * RET_HEADS, LANES, LANES), F32)],
        compiler_params=_cparams("parallel"),
        name="retention",
    )(zb, cos, sin, nw)


def _da_kernel(*refs):
    z_refs = refs[:9]
    wq_ref, wk_ref, o_ref, qn_ref, kn_ref, v_ref, op_ref, zp_ref = refs[9:17]
    on_refs, zn_refs, spare_refs = refs[17:20], refs[20:23], refs[23:26]
    stage_refs = refs[26:29]
    S = o_ref.shape[0]
    lane = lax.broadcasted_iota(jnp.int32, (1, LANES), 1)
    lo = lane < DA_DH
    row = lax.broadcasted_iota(jnp.int32, (CHUNK, CHUNK), 0)
    col = lax.broadcasted_iota(jnp.int32, (CHUNK, CHUNK), 1)
    mask_cur = col <= row
    row2 = lax.broadcasted_iota(jnp.int32, (CHUNK, 2 * CHUNK), 0)
    col2 = lax.broadcasted_iota(jnp.int32, (CHUNK, 2 * CHUNK), 1)
    mask_band = (col2 >= row2) & (col2 <= row2 + CHUNK)

    seg = (lax.broadcasted_iota(jnp.int32, (LANES, LANES), 0) // DA_DH
           == lax.broadcasted_iota(jnp.int32, (LANES, LANES), 1) // DA_DH).astype(BF16) * (1.0 / DA_DH)

    def head_norm(x, w):
        x2 = x * x
        hi = x2.astype(BF16)
        lo_part = (x2 - hi.astype(F32)).astype(BF16)
        ms = _dot(hi, seg) + _dot(lo_part, seg)
        return x * lax.rsqrt(ms + EPS) * w

    RB = 256
    for g, (dil, nb) in enumerate(DA_GROUPS):
        L = S // dil
        zq_ref, zk_ref, zv_ref = z_refs[3 * g:3 * g + 3]
        if dil > 1:
            for piece in range(S // RB):
                rows = pl.ds(piece * RB, RB)
                for src_ref, dst_ref in zip((zq_ref, zk_ref, zv_ref), stage_refs):
                    dst_ref[rows, :] = src_ref[rows, :].astype(F32)
            zq_ref, zk_ref, zv_ref = stage_refs
        first = min(dil, DA_MAX_STRIDE)
        if dil > first:
            mid_refs = (on_refs[g], zn_refs[g], spare_refs[g])
            sub = S // first
            for b in range(first):
                for piece in range(sub // RB):
                    src = pl.ds(b + first * piece * RB, RB, stride=first)
                    dst = pl.ds(b * sub + piece * RB, RB)
                    for src_ref, dst_ref in zip((zq_ref, zk_ref, zv_ref), mid_refs):
                        dst_ref[dst, :] = src_ref[src, :]
            zq_ref, zk_ref, zv_ref = mid_refs
        rest = dil // first

        def source_rows(rho, piece, n_rows):
            if dil == 1:
                return pl.ds(piece * n_rows, n_rows)
            b, a = rho % first, rho // first
            if rest == 1:
                return pl.ds(b + first * piece * n_rows, n_rows, stride=first)
            return pl.ds(b * (S // first) + a + rest * piece * n_rows, n_rows, stride=rest)

        for rho in range(dil):
            for piece in range(max(L // RB, 1)):
                n_rows = min(RB, L)
                src = source_rows(rho, piece, n_rows)
                dst = pl.ds(rho * L + piece * n_rows, n_rows)
                q = zq_ref[src, :].astype(F32)
                k = zk_ref[src, :].astype(F32)
                v = zv_ref[src, :]
                qn_ref[dst, :] = (head_norm(q, wq_ref[...]) * (DA_DH ** -0.5)).astype(BF16)
                kn_ref[dst, :] = head_norm(k, wk_ref[...]).astype(BF16)
                v_ref[dst, :] = v.astype(BF16)

        def batch(r0, chained, first_has_prev):
            chains = []
            for u in range(DA_UNROLL):
                rows = pl.ds(r0 + u * CHUNK, CHUNK)
                if chained and (u > 0 or first_has_prev):
                    keys, mask = pl.ds(r0 + (u - 1) * CHUNK, 2 * CHUNK), mask_band
                else:
                    keys, mask = rows, mask_cur
                k = kn_ref[keys, :]
                v = v_ref[keys, :]
                vaug = jnp.concatenate([v, jnp.ones_like(v)], axis=1)
                q = qn_ref[rows, :]
                zero = jnp.zeros_like(q)
                for qh in (jnp.where(lo, q, zero), jnp.where(lo, zero, q)):
                    chains.append((_dot_nt(qh, k), mask, vaug))
            probs = []
            for s, mask, _ in chains:
                s = jnp.where(mask, s, NEG)
                m = jnp.max(s, axis=-1, keepdims=True)
                probs.append((jnp.exp(s - m).astype(BF16), m))
            accs = [_dot(p, vaug) for (p, _), (_, _, vaug) in zip(probs, chains)]
            normed, lses = [], []
            for u in range(DA_UNROLL):
                a0, a1 = accs[2 * u], accs[2 * u + 1]
                den = jnp.where(lo, a0[:, LANES:], a1[:, LANES:])
                normed.append(jnp.where(lo, a0[:, :LANES], a1[:, :LANES]) / den)
                lses.append(jnp.where(lo, probs[2 * u][1], probs[2 * u + 1][1]) + jnp.log(den))
            rows = pl.ds(r0, DA_UNROLL * CHUNK)
            op_ref[rows, :] = jnp.concatenate(normed, axis=0)
            zp_ref[rows, :] = jnp.concatenate(lses, axis=0)

        span = DA_UNROLL * CHUNK
        if nb == 1:
            def singles(i, c):
                batch(pl.multiple_of(i * span, span), False, False)
                return c
            lax.fori_loop(0, S // span, singles, 0)
        else:
            def segment(sgi, c, nb=nb):
                base = pl.multiple_of(sgi * (nb * CHUNK), span)
                batch(base, True, False)
                if nb > DA_UNROLL:
                    def inner(n, c2):
                        batch(pl.multiple_of(base + n * span, span), True, True)
                        return c2
                    lax.fori_loop(1, nb // DA_UNROLL, inner, 0)
                return c
            lax.fori_loop(0, dil, segment, 0)

        outs = ((op_ref, on_refs[g]), (zp_ref, zn_refs[g]))
        for rho in range(dil):
            for piece in range(max(L // RB, 1)):
                n_rows = min(RB, L)
                dst = source_rows(rho, piece, n_rows)
                src = pl.ds(rho * L + piece * n_rows, n_rows)
                for k_out, (perm_ref, nat_ref) in enumerate(outs):
                    (stage_refs[k_out] if rest > 1 else nat_ref)[dst, :] = perm_ref[src, :]
        if rest > 1:
            sub = S // first
            for b in range(first):
                for piece in range(sub // RB):
                    dst = pl.ds(b + first * piece * RB, RB, stride=first)
                    src = pl.ds(b * sub + piece * RB, RB)
                    for k_out, (_, nat_ref) in enumerate(outs):
                        nat_ref[dst, :] = stage_refs[k_out][src, :]

    for piece in range(S // RB):
        rows = pl.ds(piece * RB, RB)
        m = jnp.maximum(jnp.maximum(zn_refs[0][rows, :], zn_refs[1][rows, :]), zn_refs[2][rows, :])
        num = jnp.zeros((RB, LANES), F32)
        den = jnp.zeros((RB, LANES), F32)
        for g in range(3):
            e = jnp.exp(zn_refs[g][rows, :] - m)
            num = num + e * on_refs[g][rows, :]
            den = den + e
        o_ref[rows, :] = (num / den).astype(o_ref.dtype)


def _dilated_attention(zb, slab0, wq, wk, B, S):
    T = B * S
    slab = lambda k: pl.BlockSpec((S, LANES), lambda b, p: (b, slab0 + 9 * p + k))
    return pl.pallas_call(
        _da_kernel,
        grid=(B, 2),
        in_specs=[slab(k) for k in range(9)] + [
                  pl.BlockSpec((1, LANES), lambda b, p: (0, 0)),
                  pl.BlockSpec((1, LANES), lambda b, p: (0, 0))],
        out_specs=pl.BlockSpec((S, LANES), lambda b, p: (b, p)),
        out_shape=jax.ShapeDtypeStruct((T, 2 * LANES), BF16),
        scratch_shapes=[pltpu.VMEM((S, LANES), BF16), pltpu.VMEM((S, LANES), BF16),
                        pltpu.VMEM((S, LANES), BF16),
                        pltpu.VMEM((S, LANES), F32), pltpu.VMEM((S, LANES), F32)]
                       + [pltpu.VMEM((S, LANES), F32)] * 12,
        compiler_params=_cparams("parallel", "arbitrary"),
        name="dilated_attention",
    )(*([zb] * 9), wq, wk)


def _mlstm_kernel(qk_ref, v_ref, og_ref, gate_ref, gb_ref, nw_ref, o_ref, st_ref, e_ref, et_ref):
    S = qk_ref.shape[0] // ML_SEQS
    n_chunks = S // CHUNK
    W = qk_ref.shape[1]
    lane = lax.broadcasted_iota(jnp.int32, (1, LANES), 1)
    lo = lane < ML_DK
    row = lax.broadcasted_iota(jnp.int32, (CHUNK, CHUNK), 0)
    col = lax.broadcasted_iota(jnp.int32, (CHUNK, CHUNK), 1)
    causal = col <= row
    tril = causal.astype(BF16)
    srow = lax.broadcasted_iota(jnp.int32, (LANES, 1), 0) < ML_DK
    st_ref[...] = jnp.zeros(st_ref.shape, F32)

    is_f = (lane >= ML_HEADS) & (lane < 2 * ML_HEADS)
    for n in range(ML_SEQS * n_chunks):
        rows = pl.ds(n * CHUNK, CHUNK)
        gp = gate_ref[rows, :] + gb_ref[...]
        logf = jnp.minimum(gp, 0.0) - jnp.log(1.0 + jnp.exp(-jnp.abs(gp)))
        f1 = jnp.where(is_f, logf, 0.0)
        t1 = f1.astype(BF16)
        f2 = f1 - t1.astype(F32)
        t2 = f2.astype(BF16)
        t3 = (f2 - t2.astype(F32)).astype(BF16)
        cum = _dot(tril, t1) + _dot(tril, t2) + _dot(tril, t3)
        e = jnp.where(lane < ML_HEADS, gp, cum)
        e_ref[rows, :] = e
        et_ref[n] = e.T[0:2 * ML_HEADS, :]

    units = [(sq, h) for sq in range(ML_SEQS) for h in range(ML_HEADS)]

    def chunk(n, mms):
        rows_of = [pl.ds(pl.multiple_of(sq * S + n * CHUNK, CHUNK), CHUNK) for sq in range(ML_SEQS)]
        es = [e_ref[rows_of[sq], :] for sq in range(ML_SEQS)]
        ets = [et_ref[sq * n_chunks + n] for sq in range(ML_SEQS)]
        khs, vaugs, s_raws, iqs = [], [], [], []
        for sq in range(ML_SEQS):
            rows = rows_of[sq]
            for pair in range(2):
                q2 = qk_ref[rows, pair * LANES:(pair + 1) * LANES]
                k2 = qk_ref[rows, W // 2 + pair * LANES:W // 2 + (pair + 1) * LANES]
                zero = jnp.zeros_like(q2)
                stb = st_ref[2 * sq + pair].astype(BF16)
                for hs in range(2):
                    h = 2 * pair + hs
                    qh = jnp.where(lo, q2, zero) if hs == 0 else jnp.where(lo, zero, q2)
                    kh = jnp.where(lo, k2, zero) if hs == 0 else jnp.where(lo, zero, k2)
                    vh = v_ref[rows, h * LANES:(h + 1) * LANES]
                    s_raws.append(_dot_nt(qh, k2))
                    iqs.append(_dot(qh, stb))
                    khs.append(kh)
                    vaugs.append(jnp.concatenate([vh, jnp.ones_like(vh)], axis=-1))
        ps, wis, ms_, kws, wcs, new_mms = [], [], [], [], [], []
        for u, (sq, h) in enumerate(units):
            e, et = es[sq], ets[sq]
            mm = mms[u]
            i_b = jnp.broadcast_to(e[:, h:h + 1], (CHUNK, LANES))
            a_b = jnp.broadcast_to(e[:, ML_HEADS + h:ML_HEADS + h + 1], (CHUNK, LANES))
            c_row = et[h:h + 1, :] - et[ML_HEADS + h:ML_HEADS + h + 1, :]
            dmat = jnp.where(causal, a_b + c_row, NEG)
            inter = a_b + mm
            m = jnp.maximum(inter, jnp.max(dmat, axis=-1, keepdims=True))
            ps.append((s_raws[u] * jnp.exp(dmat - m)).astype(BF16))
            wis.append(jnp.exp(inter - m))
            ms_.append(m)
            m_new = m[CHUNK - 1:CHUNK, :]
            a_last = a_b[CHUNK - 1:CHUNK, :]
            wk = jnp.exp(a_last - a_b + i_b - m_new)
            kws.append((khs[u].astype(F32) * wk).astype(BF16))
            wc = jnp.exp(a_last + mm - m_new)
            wcs.append(jnp.concatenate([wc, wc], axis=1))
            new_mms.append(m_new)
        accs = [_dot(ps[u], vaugs[u]) + jnp.concatenate([wis[u], wis[u]], axis=1) * iqs[u]
                for u in range(len(units))]
        upds = [_dot_tn(kws[u], vaugs[u]) for u in range(len(units))]
        for sp in range(2 * ML_SEQS):
            u0, u1 = 2 * sp, 2 * sp + 1
            st_ref[sp] = jnp.where(srow, wcs[u0], wcs[u1]) * st_ref[sp] + (upds[u0] + upds[u1])
        for u, (sq, h) in enumerate(units):
            num = accs[u][:, :LANES]
            den = accs[u][:, LANES:]
            hv = num / jnp.maximum(jnp.abs(den), jnp.exp(-ms_[u]))
            var = jnp.mean(hv * hv, axis=-1, keepdims=True)
            hv = hv * lax.rsqrt(var + EPS) * nw_ref[:, h * LANES:(h + 1) * LANES]
            og = og_ref[rows_of[sq], h * LANES:(h + 1) * LANES].astype(F32)
            o_ref[rows_of[sq], h * LANES:(h + 1) * LANES] = (hv * _sigmoid(og)).astype(o_ref.dtype)
        return tuple(new_mms)

    lax.fori_loop(0, n_chunks, chunk, tuple(jnp.zeros((1, LANES), F32) for _ in units))


def _mlstm(zb, zf, gate_bias, nw, B, S, qk_blk, gate_blk):
    T = B * S
    W = ML_HEADS * LANES
    R = ML_SEQS * S
    return pl.pallas_call(
        _mlstm_kernel,
        grid=(B // ML_SEQS,),
        in_specs=[pl.BlockSpec((R, W), lambda b: (b, qk_blk)),
                  pl.BlockSpec((R, W), lambda b: (b, qk_blk + 1)),
                  pl.BlockSpec((R, W), lambda b: (b, qk_blk + 2)),
                  pl.BlockSpec((R, LANES), lambda b: (b, gate_blk)),
                  pl.BlockSpec((1, LANES), lambda b: (0, 0)),
                  pl.BlockSpec((1, W), lambda b: (0, 0))],
        out_specs=pl.BlockSpec((R, W), lambda b: (b, 0)),
        out_shape=jax.ShapeDtypeStruct((T, W), BF16),
        scratch_shapes=[pltpu.VMEM((2 * ML_SEQS, LANES, 2 * LANES), F32),
                        pltpu.VMEM((R, LANES), F32),
                        pltpu.VMEM((R // CHUNK, 2 * ML_HEADS, LANES), F32)],
        compiler_params=_cparams("parallel"),
        name="mlstm",
    )(zb, zb, zb, zf, gate_bias, nw)


def _merge_value(x_ref, oret_ref, oda_ref, oml_ref, gr_ref, gd_ref, gm_ref,
                 wr_ref, wd_ref, wm_ref, wo_ref):
    y = _sigmoid(gr_ref[...].astype(F32)) * _dot(oret_ref[...], wr_ref[...])
    y = y + _sigmoid(gd_ref[...].astype(F32)) * _dot(oda_ref[...], wd_ref[...])
    y = y + _sigmoid(gm_ref[...].astype(F32)) * _dot(oml_ref[...], wm_ref[...])
    return x_ref[...] + _dot(y.astype(BF16), wo_ref[...])


def _merge_specs(x, o_ret, o_da, o_ml, g_blk, tm):
    D = x.shape[1]
    return [pl.BlockSpec((tm, D), lambda i: (i, 0)),
            pl.BlockSpec((tm, o_ret.shape[1]), lambda i: (i, 0)),
            pl.BlockSpec((tm, o_da.shape[1]), lambda i: (i, 0)),
            pl.BlockSpec((tm, o_ml.shape[1]), lambda i: (i, 0)),
            pl.BlockSpec((tm, D), lambda i: (i, g_blk)),
            pl.BlockSpec((tm, D), lambda i: (i, g_blk + 1)),
            pl.BlockSpec((tm, D), lambda i: (i, g_blk + 2))]


def _resident(a):
    return pl.BlockSpec(a.shape, lambda i: (0,) * a.ndim, pipeline_mode=pl.Buffered(1))


def _swiglu_block(h, wg_ref, wu_ref, wd_ref, a_ref):
    tf = wg_ref.shape[1]
    c0 = 0
    while c0 < tf:
        w = min(MXU_N, tf - c0)
        g = _dot(h, wg_ref[:, c0:c0 + w].astype(BF16))
        u = _dot(h, wu_ref[:, c0:c0 + w].astype(BF16))
        a_ref[:, c0:c0 + w] = (_silu(g) * u).astype(BF16)
        c0 += w
    return _dot(a_ref[...], wd_ref[...].astype(BF16))


def _merge_ffn_kernel(*refs):
    merge_refs, (nw_ref, wg_ref, wu_ref, wdn_ref, o_ref, a_ref) = refs[:11], refs[11:]
    x = _merge_value(*merge_refs)
    ms = jnp.mean(x * x, axis=-1, keepdims=True)
    h = (x * lax.rsqrt(ms + EPS) * nw_ref[...]).astype(BF16)
    o_ref[...] = x + _swiglu_block(h, wg_ref, wu_ref, wdn_ref, a_ref)


def _merge_ffn(x, o_ret, o_da, o_ml, zb, g_blk, wr, wd, wm, wo, nw, wg, wu, wdn, tm):
    T, D = x.shape
    weights = (wr, wd, wm, wo, nw, wg, wu, wdn)
    return pl.pallas_call(
        _merge_ffn_kernel,
        grid=(T // tm,),
        in_specs=_merge_specs(x, o_ret, o_da, o_ml, g_blk, tm) + [_resident(a) for a in weights],
        out_specs=pl.BlockSpec((tm, D), lambda i: (i, 0)),
        out_shape=jax.ShapeDtypeStruct((T, D), F32),
        scratch_shapes=[pltpu.VMEM((tm, wg.shape[1]), BF16)],
        compiler_params=_cparams("parallel"),
        name="merge_ffn",
    )(x, o_ret, o_da, o_ml, zb, zb, zb, *weights)


HI16 = 0xFFFF0000


def _pack_bf16_pairs(a, b):
    ab = pltpu.bitcast(a.astype(BF16).astype(F32), jnp.uint32)
    bb = pltpu.bitcast(b.astype(BF16).astype(F32), jnp.uint32)
    return (ab >> 16) | (bb & jnp.uint32(HI16))


def _unpack_bf16_pairs(p):
    lo = pltpu.bitcast(p << 16, F32)
    hi = pltpu.bitcast(p & jnp.uint32(HI16), F32)
    return lo, hi


def _merge_router_kernel(*refs):
    merge_refs, (nw_ref, wr_ref, br_ref, xo_ref, ha_ref, hb_ref, sel_ref, cnt_ref) = refs[:11], refs[11:]
    x = _merge_value(*merge_refs)
    xo_ref[...] = x
    ms = jnp.mean(x * x, axis=-1, keepdims=True)
    h = x * lax.rsqrt(ms + EPS) * nw_ref[...]
    Q = h.shape[1] // 4
    ha_ref[...] = _pack_bf16_pairs(h[:, 0:Q], h[:, Q:2 * Q])
    hb_ref[...] = _pack_bf16_pairs(h[:, 2 * Q:3 * Q], h[:, 3 * Q:4 * Q])
    w = wr_ref[...]
    h_hi = h.astype(BF16)
    h_lo = (h - h_hi.astype(F32)).astype(BF16)
    w_hi = w.astype(BF16)
    w_lo = (w - w_hi.astype(F32)).astype(BF16)
    logits = _dot(h_hi, w_hi) + _dot(h_lo, w_hi) + _dot(h_hi, w_lo) + br_ref[...]
    lane = lax.broadcasted_iota(jnp.int32, logits.shape, 1).astype(F32)
    logits = jnp.where(lane < N_EXPERTS, logits, NEG)
    m1 = jnp.max(logits, axis=-1, keepdims=True)
    i1 = jnp.min(jnp.where(logits == m1, lane, float(LANES)), axis=-1, keepdims=True)
    rest = jnp.where(lane == i1, NEG, logits)
    m2 = jnp.max(rest, axis=-1, keepdims=True)
    i2 = jnp.min(jnp.where(rest == m2, lane, float(LANES)), axis=-1, keepdims=True)
    e2 = jnp.exp(m2 - m1)
    p1 = 1.0 / (1.0 + e2)
    p2 = e2 / (1.0 + e2)
    sel_ref[...] = jnp.where(lane == 0.0, i1, jnp.where(lane == 1.0, i2,
                             jnp.where(lane == 2.0, p1, jnp.where(lane == 3.0, p2, 0.0))))
    chosen = jnp.where((lane == i1) | (lane == i2), 1.0, 0.0)
    cnt_ref[...] = jnp.broadcast_to(jnp.sum(chosen, axis=0, keepdims=True), cnt_ref.shape)


def _merge_router(x, o_ret, o_da, o_ml, zb, g_blk, wr, wd, wm, wo, nw, w_router, b_router, tm):
    T, D = x.shape
    Q = D // 4
    wrt = jnp.pad(w_router, ((0, 0), (0, LANES - N_EXPERTS)))
    brt = jnp.pad(b_router, (0, LANES - N_EXPERTS)).reshape(1, LANES)
    weights = (wr, wd, wm, wo, nw, wrt, brt)
    return pl.pallas_call(
        _merge_router_kernel,
        grid=(T // tm,),
        in_specs=_merge_specs(x, o_ret, o_da, o_ml, g_blk, tm) + [_resident(a) for a in weights],
        out_specs=[pl.BlockSpec((tm, D), lambda i: (i, 0)),
                   pl.BlockSpec((tm, Q), lambda i: (i, 0)),
                   pl.BlockSpec((tm, Q), lambda i: (i, 0)),
                   pl.BlockSpec((tm, LANES), lambda i: (i, 0)),
                   pl.BlockSpec((None, 8, LANES), lambda i: (i, 0, 0))],
        out_shape=[jax.ShapeDtypeStruct((T, D), F32),
                   jax.ShapeDtypeStruct((T, Q), jnp.uint32), jax.ShapeDtypeStruct((T, Q), jnp.uint32),
                   jax.ShapeDtypeStruct((T, LANES), F32),
                   jax.ShapeDtypeStruct((T // tm, 8, LANES), F32)],
        compiler_params=_cparams("parallel"),
        name="merge_router",
    )(x, o_ret, o_da, o_ml, zb, zb, zb, *weights)


def _moe_rank_kernel(sel_ref, tcnt_ref, pos_ref, meta_ref, cnt_ref, offs_ref, carry_ref, before_ref):
    i = pl.program_id(0)
    tm = sel_ref.shape[0]
    lane = lax.broadcasted_iota(jnp.int32, (tm, LANES), 1).astype(F32)
    lane1 = lax.broadcasted_iota(jnp.int32, (1, LANES), 1).astype(F32)
    sel = sel_ref[...]
    i1 = sel[:, 0:1]
    i2 = sel[:, 1:2]
    onehot = jnp.where((lane == i1) | (lane == i2), 1.0, 0.0)
    colsum = jnp.sum(onehot, axis=0, keepdims=True)

    def padded_counts():
        return jnp.floor((cnt_ref[...] + (MOE_TILE - 1.0)) * (1.0 / MOE_TILE)) * MOE_TILE

    @pl.when(i == 0)
    def _():
        cnt_ref[...] = jnp.sum(tcnt_ref[...], axis=0)
        k = lax.broadcasted_iota(jnp.int32, (LANES, LANES), 0)
        e = lax.broadcasted_iota(jnp.int32, (LANES, LANES), 1)
        upper = (k < e).astype(F32)
        offs_ref[...] = jnp.dot(padded_counts(), upper, preferred_element_type=F32,
                                precision=lax.Precision.HIGHEST)
        carry_ref[...] = jnp.zeros(carry_ref.shape, F32)
        r = lax.broadcasted_iota(jnp.int32, (tm, tm), 0)
        c = lax.broadcasted_iota(jnp.int32, (tm, tm), 1)
        before_ref[...] = (c < r).astype(BF16)

    rank = _dot(before_ref[...], onehot.astype(BF16)) + carry_ref[0:1, :]
    row = rank + offs_ref[0:1, :]
    pos1 = jnp.sum(jnp.where(lane == i1, row, 0.0), axis=-1, keepdims=True)
    pos2 = jnp.sum(jnp.where(lane == i2, row, 0.0), axis=-1, keepdims=True)
    both = jnp.where(lane == 0.0, pos1, jnp.where(lane == 1.0, pos2, 0.0))
    pos_ref[...] = both.T[0:8, :].astype(jnp.int32)
    carry_ref[...] += colsum

    @pl.when(i == pl.num_programs(0) - 1)
    def _():
        nrow = meta_ref.shape[0]
        padded = padded_counts()[0:1, :]
        offs = offs_ref[0:1, :]
        ends = offs + padded
        start = lax.broadcasted_iota(jnp.int32, (nrow, 1), 0).astype(F32) * MOE_TILE
        is_e = lane1 < N_EXPERTS
        te = jnp.sum(jnp.where(is_e & (ends <= start), 1.0, 0.0), axis=-1, keepdims=True)
        te = jnp.minimum(te, N_EXPERTS - 1.0)
        valid_end = jnp.sum(jnp.where(lane1 == te, offs + cnt_ref[0:1, :], 0.0), axis=-1, keepdims=True)
        nv = jnp.clip(valid_end - start, 0.0, MOE_TILE)
        total = jnp.sum(jnp.where(is_e, padded, 0.0), axis=-1, keepdims=True)
        active = jnp.where(start < total, 1.0, 0.0)
        meta_ref[...] = jnp.where(lane1 == 0.0, te, jnp.where(lane1 == 1.0, nv,
                                  jnp.where(lane1 == 2.0, active, 0.0)))


def _moe_rank(sel, tcnt, tm, meta_rows):
    T = sel.shape[0]
    nt = T // tm
    return pl.pallas_call(
        _moe_rank_kernel,
        grid=(nt,),
        in_specs=[pl.BlockSpec((tm, LANES), lambda i: (i, 0)),
                  pl.BlockSpec(tcnt.shape, lambda i: (0, 0, 0))],
        out_specs=[pl.BlockSpec((None, 8, tm), lambda i: (i, 0, 0)),
                   pl.BlockSpec((meta_rows, LANES), lambda i: (0, 0))],
        out_shape=[jax.ShapeDtypeStruct((nt, 8, tm), jnp.int32),
                   jax.ShapeDtypeStruct((meta_rows, LANES), F32)],
        scratch_shapes=[pltpu.VMEM((8, LANES), F32), pltpu.VMEM((8, LANES), F32),
                        pltpu.VMEM((8, LANES), F32), pltpu.VMEM((tm, tm), BF16)],
        compiler_params=_cparams("arbitrary"),
        name="moe_rank",
    )(sel, tcnt)


def _sc_mesh():
    return plsc.VectorSubcoreMesh(core_axis_name="core", subcore_axis_name="subcore")


def _sc_scatter_rows(x, idx, n_rows):
    nt = x.shape[0] // SC_WINDOW
    width = x.shape[1]
    assert idx.shape[1] == TOP_K * x.shape[0]

    @functools.partial(pl.kernel, out_type=jax.ShapeDtypeStruct((n_rows, width), x.dtype),
                       mesh=_sc_mesh())
    def scatter_kernel(x_hbm, i1_hbm, i2_hbm, o_hbm):
        def body(x_vmem, i1_vmem, i2_vmem):
            pltpu.sync_copy(x_vmem, o_hbm.at[i1_vmem.at[0]])
            pltpu.sync_copy(x_vmem, o_hbm.at[i2_vmem.at[0]])

        pltpu.emit_pipeline(
            body, grid=(nt,),
            in_specs=[pl.BlockSpec((SC_WINDOW, width), lambda i: (i, 0)),
                      pl.BlockSpec((1, SC_WINDOW), lambda i: (0, i)),
                      pl.BlockSpec((1, SC_WINDOW), lambda i: (0, i + nt))],
            out_specs=[],
            core_axis_name=("core", "subcore"), dimension_semantics=(pltpu.PARALLEL,),
        )(x_hbm, i1_hbm, i2_hbm)

    return scatter_kernel(x, idx, idx)


def _sc_gather_rows(x, idx):
    n_idx = idx.shape[1]
    width = x.shape[1]

    @functools.partial(pl.kernel, out_type=jax.ShapeDtypeStruct((n_idx, width), x.dtype),
                       mesh=_sc_mesh())
    def gather_kernel(x_hbm, i_hbm, o_hbm):
        def body(i_vmem, o_vmem):
            pltpu.sync_copy(x_hbm.at[i_vmem.at[0]], o_vmem)

        pltpu.emit_pipeline(
            body, grid=(n_idx // SC_WINDOW,),
            in_specs=[pl.BlockSpec((1, SC_WINDOW), lambda i: (0, i))],
            out_specs=[pl.BlockSpec((SC_WINDOW, width), lambda i: (i, 0))],
            core_axis_name=("core", "subcore"), dimension_semantics=(pltpu.PARALLEL,),
        )(i_hbm, o_hbm)

    return gather_kernel(x, idx)


def _moe_group_kernel(te_ref, nv_ref, na_ref, xa_ref, xb_ref, wg_ref, wu_ref, wd_ref,
                      ya_ref, yb_ref, h_ref, a_ref):
    j = pl.program_id(0)

    @pl.when(j < na_ref[0])
    def _():
        tm = h_ref.shape[0]
        Q = xa_ref.shape[1]
        valid = lax.broadcasted_iota(jnp.int32, (tm, 1), 0) < nv_ref[j]
        for src, c0 in ((xa_ref, 0), (xb_ref, 2 * Q)):
            lo, hi = _unpack_bf16_pairs(src[...])
            h_ref[:, c0:c0 + Q] = jnp.where(valid, lo, 0.0).astype(BF16)
            h_ref[:, c0 + Q:c0 + 2 * Q] = jnp.where(valid, hi, 0.0).astype(BF16)
        y = _swiglu_block(h_ref[...], wg_ref, wu_ref, wd_ref, a_ref)
        ya_ref[...] = _pack_bf16_pairs(y[:, 0:Q], y[:, Q:2 * Q])
        yb_ref[...] = _pack_bf16_pairs(y[:, 2 * Q:3 * Q], y[:, 3 * Q:4 * Q])


def _moe_group(te, nv, na, xa, xb, wg, wu, wd):
    R, Q = xa.shape
    E, D, F = wg.shape
    tile = lambda j, te, nv, na: (jnp.minimum(j, na[0] - 1), 0)
    expert = lambda j, te, nv, na: (te[j], 0, 0)
    once = pl.Buffered(1)
    grid_spec = pltpu.PrefetchScalarGridSpec(
        num_scalar_prefetch=3,
        grid=(R // MOE_TILE,),
        in_specs=[pl.BlockSpec((MOE_TILE, Q), tile),
                  pl.BlockSpec((MOE_TILE, Q), tile),
                  pl.BlockSpec((None, D, F), expert),
                  pl.BlockSpec((None, D, F), expert),
                  pl.BlockSpec((None, F, D), expert, pipeline_mode=once)],
        out_specs=[pl.BlockSpec((MOE_TILE, Q), tile), pl.BlockSpec((MOE_TILE, Q), tile)],
        scratch_shapes=[pltpu.VMEM((MOE_TILE, D), BF16), pltpu.VMEM((MOE_TILE, F), BF16)],
    )
    return pl.pallas_call(
        _moe_group_kernel,
        grid_spec=grid_spec,
        out_shape=[jax.ShapeDtypeStruct((R, Q), jnp.uint32), jax.ShapeDtypeStruct((R, Q), jnp.uint32)],
        compiler_params=_cparams("arbitrary"),
        name="moe_experts",
    )(te, nv, na, xa, xb, wg, wu, wd)


def _moe_combine_kernel(x_ref, sel_ref, g1_ref, g2_ref, *rest):
    o_ref = rest[-1]
    sel = sel_ref[...]
    p1 = sel[:, 2:3]
    p2 = sel[:, 3:4]
    Q = g1_ref.shape[1]
    lo1, hi1 = _unpack_bf16_pairs(g1_ref[...])
    lo2, hi2 = _unpack_bf16_pairs(g2_ref[...])
    o_ref[:, 0:Q] = x_ref[:, 0:Q] + (p1 * lo1 + p2 * lo2)
    o_ref[:, Q:2 * Q] = x_ref[:, Q:2 * Q] + (p1 * hi1 + p2 * hi2)


def _moe_combine_half(x, sel, g, half, partial_out, tm):
    T, D = x.shape
    Q = g.shape[1]
    nt = T // tm
    in_specs = [pl.BlockSpec((tm, 2 * Q), lambda i: (i, half)),
                pl.BlockSpec((tm, LANES), lambda i: (i, 0)),
                pl.BlockSpec((tm, Q), lambda i: (i, 0)),
                pl.BlockSpec((tm, Q), lambda i: (i + nt, 0))]
    args = [x, sel, g, g]
    aliases = {}
    if partial_out is not None:
        in_specs.append(pl.BlockSpec(memory_space=pl.ANY))
        args.append(partial_out)
        aliases = {4: 0}
    return pl.pallas_call(
        _moe_combine_kernel,
        grid=(nt,),
        in_specs=in_specs,
        out_specs=pl.BlockSpec((tm, 2 * Q), lambda i: (i, half)),
        out_shape=jax.ShapeDtypeStruct((T, D), F32),
        input_output_aliases=aliases,
        compiler_params=_cparams("parallel"),
        name="moe_combine",
    )(*args)


def _cast_kernel(x_ref, o_ref):
    o_ref[...] = x_ref[...].astype(o_ref.dtype)


def _cast_bf16(w, rows):
    cols = w.shape[-1]
    w2 = w.reshape(-1, cols)
    out = pl.pallas_call(
        _cast_kernel,
        grid=(w2.shape[0] // rows,),
        in_specs=[pl.BlockSpec((rows, cols), lambda i: (i, 0))],
        out_specs=pl.BlockSpec((rows, cols), lambda i: (i, 0)),
        out_shape=jax.ShapeDtypeStruct(w2.shape, BF16),
        compiler_params=_cparams("parallel"),
        name="cast_bf16",
    )(w2)
    return out.reshape(w.shape)


def _moe(x, ha, hb, sel, tcnt, wg, wu, wd):
    T, D = x.shape
    n_rows = TOP_K * T + N_EXPERTS * MOE_TILE
    n_tiles = n_rows // MOE_TILE
    pos, meta = _moe_rank(sel, tcnt, WIDE_ROW_TILE, 256)
    idx = jnp.concatenate([pos[:, 0, :].reshape(1, T), pos[:, 1, :].reshape(1, T)], axis=1)
    te = meta[:n_tiles, 0].astype(jnp.int32)
    nv = meta[:n_tiles, 1].astype(jnp.int32)
    na = jnp.sum(meta[:n_tiles, 2]).astype(jnp.int32).reshape(1)
    te = jnp.where(jnp.arange(n_tiles) < na[0], te, te[na[0] - 1])
    xa = _sc_scatter_rows(ha, idx, n_rows)
    xb = _sc_scatter_rows(hb, idx, n_rows)
    ya, yb = _moe_group(te, nv, na, xa, xb, wg, wu, wd)
    ga = _sc_gather_rows(ya, idx)
    gb = _sc_gather_rows(yb, idx)
    out = _moe_combine_half(x, sel, ga, 0, None, 2 * WIDE_ROW_TILE)
    return _moe_combine_half(x, sel, gb, 1, out, 2 * WIDE_ROW_TILE)


def _split_w_in(w_in):
    sizes = (512, 512, 512, 512, 768, 768, 768, 256, 256, 512, 512, 4, 4, 1024, 1024, 1024)
    offs = [0]
    for s in sizes:
        offs.append(offs[-1] + s)
    part = lambda i: w_in[..., offs[i]:offs[i + 1]]
    rq, rk, rv, rg, dq, dk, dv, mq, mk, mv, mo, mi, mf, g_ret, g_da, g_ml = (part(i) for i in range(16))
    cols = []
    for p in range(2):
        for g in range(3):
            for t in (dq, dk, dv):
                cols.append(t[..., g * 256 + p * 128:g * 256 + (p + 1) * 128])
    wb = jnp.concatenate([rq, rk, rv, rg, g_ret, g_da, g_ml, mq, mk, mv, mo] + cols,
                         axis=-1).astype(BF16)
    pad = jnp.zeros(w_in.shape[:-1] + (LANES - 8,), w_in.dtype)
    wf = jnp.concatenate([mi, mf, pad], axis=-1).astype(BF16)
    return wb, wf


def _rope_tables(S):
    half = LANES // 2
    inv = jnp.power(ROPE_BASE, -jnp.arange(half, dtype=F32) / half)
    ang = jnp.arange(S, dtype=F32)[:, None] * inv[None, :]
    cos = jnp.cos(ang)
    sin = jnp.sin(ang)
    return jnp.concatenate([cos, cos], axis=1), jnp.concatenate([-sin, sin], axis=1)


def kernel(x, norm1_w, w_in, ret_norm_w, da_q_norm_w, da_k_norm_w, ml_conv_w, ml_i_bias, ml_f_bias,
           ml_norm_w, w_br_ret, w_br_da, w_br_ml, w_out, norm2_w, ffn_w_gate, ffn_w_up, ffn_w_down,
           moe_w_router, moe_b_router, moe_w_gate, moe_w_up, moe_w_down):
    B, S, D = x.shape
    T = B * S
    depth = w_in.shape[0]
    cos, sin = _rope_tables(S)
    wb_all, wf_all = _split_w_in(w_in)
    xt = x.reshape(T, D)
    for layer in range(depth):
        wb, wf = wb_all[layer], wf_all[layer]
        nw1 = norm1_w[layer].reshape(1, D)
        assert OFF_MLQK % INPROJ_TN == 0
        zb, zf = _inproj(xt, nw1, wb, wf, ml_conv_w[layer], OFF_MLQK // INPROJ_TN, S, INPROJ_TN)
        o_ret = _retention(zb, cos, sin, ret_norm_w[layer].reshape(1, -1), B, S)
        wq = jnp.tile(da_q_norm_w[layer], 2).reshape(1, LANES)
        wk = jnp.tile(da_k_norm_w[layer], 2).reshape(1, LANES)
        o_da = _dilated_attention(zb, OFF_DA // LANES, wq, wk, B, S)
        gate_bias = jnp.concatenate([ml_i_bias[layer], ml_f_bias[layer],
                                     jnp.zeros((LANES - 2 * ML_HEADS,), F32)]).reshape(1, LANES)
        o_ml = _mlstm(zb, zf, gate_bias, ml_norm_w[layer].reshape(1, -1), B, S,
                      qk_blk=OFF_MLQK // (ML_HEADS * LANES), gate_blk=0)
        merge_args = (xt, o_ret, o_da, o_ml, zb, OFF_GATES // D_MODEL,
                      w_br_ret[layer].astype(BF16), w_br_da[layer].astype(BF16),
                      w_br_ml[layer].astype(BF16), w_out[layer].astype(BF16),
                      norm2_w[layer].reshape(1, D))
        j = layer // 2
        if layer % 2 == 0:
            xt = _merge_ffn(*merge_args, ffn_w_gate[j].astype(BF16), ffn_w_up[j].astype(BF16),
                            ffn_w_down[j].astype(BF16), ROW_TILE)
        else:
            xt, ha, hb, sel, tcnt = _merge_router(*merge_args, moe_w_router[j], moe_b_router[j],
                                                  ROW_TILE)
            xt = _moe(xt, ha, hb, sel, tcnt, _cast_bf16(moe_w_gate[j], 512), _cast_bf16(moe_w_up[j], 512),
                      _cast_bf16(moe_w_down[j], 2048))
    return xt.reshape(B, S, D)
```
